```python
import math
import jax, jax.numpy as jnp
from jax import lax
import numpy as np

D_MODEL = 1024
BATCH = 8
SEQ = 2048
DEPTH = 4

HEAD_DIM = 64
A_GROUPS = ((128, 1), (512, 4), (2048, 16))
A_HEADS_PER_GROUP = 4
A_HEADS = A_HEADS_PER_GROUP * len(A_GROUPS)
A_WIDTH = A_HEADS * HEAD_DIM
B_HEADS = D_MODEL // HEAD_DIM
B_WIDTH = B_HEADS * HEAD_DIM
N_A = DEPTH // 2
N_B = DEPTH - N_A
D_FF = 2816
CONV_W = 3
ROPE_DIM = HEAD_DIM // 4
ROPE_THETA = 500000.0
BLK = 128
EPS = 1e-6
NEG = -1e30

kernel_name = "yoco_dilated_fox_convffn_trunk"


def rms_norm(x, g):
    x32 = x.astype(jnp.float32)
    y = x32 * lax.rsqrt(jnp.mean(x32 * x32, axis=-1, keepdims=True) + EPS)
    return (y * g.astype(jnp.float32)).astype(x.dtype)


def rope_tables(T):
    pos = jnp.arange(T, dtype=jnp.float32)
    inv = ROPE_THETA ** (-jnp.arange(0, ROPE_DIM, 2, dtype=jnp.float32) / ROPE_DIM)
    ang = pos[:, None] * inv[None, :]
    return jnp.cos(ang), jnp.sin(ang)


def apply_rope(t, cos, sin):
    half = ROPE_DIM // 2
    c = cos[None, :, None, :].astype(t.dtype)
    s = sin[None, :, None, :].astype(t.dtype)
    x1 = t[..., :half]
    x2 = t[..., half:ROPE_DIM]
    return jnp.concatenate([x1 * c - x2 * s, x2 * c + x1 * s, t[..., ROPE_DIM:]], axis=-1)


def banded_attention(q, k, v, n_back):
    N, L, H, D = q.shape
    nb = L // BLK
    qb = q.reshape(N, nb, BLK, H, D)
    kb = k.reshape(N, nb, BLK, H, D)
    vb = v.reshape(N, nb, BLK, H, D)

    def with_prev(t):
        prev = jnp.pad(t[:, :-1], ((0, 0), (1, 0), (0, 0), (0, 0), (0, 0)))
        return jnp.concatenate([prev, t], axis=2)

    kc, vc = with_prev(kb), with_prev(vb)
    s = jnp.einsum('nbqhd,nbkhd->nbhqk', qb, kc).astype(jnp.float32) * (D ** -0.5)
    rel = (jnp.arange(BLK)[:, None] + BLK) - jnp.arange(2 * BLK)[None, :]
    band = (rel >= 0) & (rel <= n_back)
    has_prev = (jnp.arange(nb)[:, None, None] > 0) | (jnp.arange(2 * BLK)[None, None, :] >= BLK)
    mask = band[None] & has_prev
    s = jnp.where(mask[None, :, None], s, NEG)
    m = jnp.max(s, axis=-1, keepdims=True)
    p = jnp.exp(s - m)
    l = jnp.sum(p, axis=-1, keepdims=True)
    o = jnp.einsum('nbhqk,nbkhd->nbqhd', (p / l).astype(v.dtype), vc)
    lse = (m + jnp.log(l))[..., 0]
    return o.reshape(N, L, H, D), lse.transpose(0, 1, 3, 2).reshape(N, L, H)


def dilated_mixer(xn, w_qkv, w_o, cos, sin):
    B, T, _ = xn.shape
    qkv = xn @ w_qkv
    q, k, v = jnp.split(qkv, 3, axis=-1)
    q = apply_rope(q.reshape(B, T, A_HEADS, HEAD_DIM), cos, sin)
    k = apply_rope(k.reshape(B, T, A_HEADS, HEAD_DIM), cos, sin)
    v = v.reshape(B, T, A_HEADS, HEAD_DIM)
    G = A_HEADS_PER_GROUP
    outs, lses = [], []
    for g, (window, r) in enumerate(A_GROUPS):
        L = T // r
        Lp = -(-L // BLK) * BLK

        def gather(t):
            t = t[:, :, g * G:(g + 1) * G].reshape(B, L, r, G, HEAD_DIM)
            t = t.transpose(0, 2, 1, 3, 4).reshape(B * r, L, G, HEAD_DIM)
            return jnp.pad(t, ((0, 0), (0, Lp - L), (0, 0), (0, 0)))

        o, lse = banded_attention(gather(q), gather(k), gather(v), window // r)
        o = o[:, :L].reshape(B, r, L, G, HEAD_DIM).transpose(0, 2, 1, 3, 4).reshape(B, T, G, HEAD_DIM)
        lse = lse[:, :L].reshape(B, r, L, G).transpose(0, 2, 1, 3).reshape(B, T, G)
        outs.append(o)
        lses.append(lse)
    alpha = jax.nn.softmax(jnp.stack(lses, axis=0), axis=0)
    o = jnp.concatenate([outs[g] * alpha[g][..., None].astype(outs[g].dtype)
                         for g in range(len(A_GROUPS))], axis=2)
    return o.reshape(B, T, A_WIDTH) @ w_o


def fox_mixer(xn, w_q, w_o, k, v, c):
    B, T, _ = xn.shape
    q = (xn @ w_q).reshape(B, T, B_HEADS, HEAD_DIM)
    c_t = c.transpose(0, 2, 1)
    scale = HEAD_DIM ** -0.5
    outs = []
    for i in range(T // BLK):
        q0, q1 = i * BLK, (i + 1) * BLK
        s = jnp.einsum('bqhd,bkhd->bhqk', q[:, q0:q1], k[:, :q1]).astype(jnp.float32) * scale
        s = s + (c_t[:, :, q0:q1, None] - c_t[:, :, None, :q1])
        causal = jnp.arange(q0, q1)[:, None] >= jnp.arange(q1)[None, :]
        s = jnp.where(causal, s, NEG)
        p = jax.nn.softmax(s, axis=-1).astype(v.dtype)
        outs.append(jnp.einsum('bhqk,bkhd->bqhd', p, v[:, :q1]))
    o = jnp.concatenate(outs, axis=1)
    return o.reshape(B, T, B_WIDTH) @ w_o


def conv_ffn(xn, w_up, cw, cb, w_down):
    a = xn @ w_up
    T = a.shape[1]
    ap = jnp.pad(a, ((0, 0), (CONV_W - 1, 0), (0, 0)))
    a = sum(ap[:, j:j + T] * cw[j] for j in range(CONV_W)) + cb
    gate, val = jnp.split(a, 2, axis=-1)
    return (jax.nn.gelu(gate, approximate=True) * val) @ w_down


def _fwd_setup_inputs(seed: int = 0) -> dict:
    key = jax.random.key(seed)
    ks = jax.random.split(key, 14)

    def nrm(k, shape, fan_in):
        return jax.random.normal(k, shape, jnp.float32) * fan_in ** -0.5

    return {
        "x": jax.random.normal(ks[0], (BATCH, SEQ, D_MODEL), jnp.float32),
        "norm_gains": 1.0 + 0.05 * jax.random.normal(ks[1], (DEPTH, 4, D_MODEL), jnp.float32),
        "w_qkv_a": nrm(ks[2], (N_A, D_MODEL, 3 * A_WIDTH), D_MODEL),
        "w_o_a": nrm(ks[3], (N_A, A_WIDTH, D_MODEL), A_WIDTH),
        "w_q_b": nrm(ks[4], (N_B, D_MODEL, B_WIDTH), D_MODEL),
        "w_o_b": nrm(ks[5], (N_B, B_WIDTH, D_MODEL), B_WIDTH),
        "kv_norm": 1.0 + 0.05 * jax.random.normal(ks[6], (D_MODEL,), jnp.float32),
        "w_kvf": nrm(ks[7], (D_MODEL, 2 * B_WIDTH + B_HEADS), D_MODEL),
        "b_f": 3.0 + 0.5 * jax.random.normal(ks[8], (B_HEADS,), jnp.float32),
        "w_up": nrm(ks[9], (DEPTH, D_MODEL, 2 * D_FF), D_MODEL),
        "conv_w": nrm(ks[10], (DEPTH, CONV_W, 2 * D_FF), CONV_W),
        "conv_b": 0.01 * jax.random.normal(ks[11], (DEPTH, 2 * D_FF), jnp.float32),
        "w_down": nrm(ks[12], (DEPTH, D_FF, D_MODEL), D_FF),
    }


def _fwd_reference(x, norm_gains, w_qkv_a, w_o_a, w_q_b, w_o_b, kv_norm, w_kvf, b_f,
              w_up, conv_w, conv_b, w_down):
    B, T, _ = x.shape
    cos, sin = rope_tables(T)
    h = x
    k_sh = v_sh = c_sh = None
    for l in range(DEPTH):
        g = norm_gains[l]
        if l < N_A:
            mix = dilated_mixer(rms_norm(h, g[0]), w_qkv_a[l], w_o_a[l], cos, sin)
        else:
            if l == N_A:
                kvf = rms_norm(h, kv_norm) @ w_kvf
                k_sh = kvf[..., :B_WIDTH].reshape(B, T, B_HEADS, HEAD_DIM)
                v_sh = kvf[..., B_WIDTH:2 * B_WIDTH].reshape(B, T, B_HEADS, HEAD_DIM)
                log_f = jax.nn.log_sigmoid((kvf[..., 2 * B_WIDTH:] + b_f).astype(jnp.float32))
                c_sh = jnp.cumsum(log_f, axis=1)
            j = l - N_A
            mix = fox_mixer(rms_norm(h, g[0]), w_q_b[j], w_o_b[j], k_sh, v_sh, c_sh)
        h = h + rms_norm(mix, g[1])
        f = conv_ffn(rms_norm(h, g[2]), w_up[l], conv_w[l], conv_b[l], w_down[l])
        h = h + rms_norm(f, g[3])
    return h


import jax as _jax
import jax.numpy as _jnp

TWIN_FORMAT = 'train_step'
FWD_PARAMS = ['x', 'norm_gains', 'w_qkv_a', 'w_o_a', 'w_q_b', 'w_o_b', 'kv_norm', 'w_kvf', 'b_f', 'w_up', 'conv_w', 'conv_b', 'w_down']
TWIN_WEIGHTS = ['norm_gains', 'w_qkv_a', 'w_o_a', 'w_q_b', 'w_o_b', 'kv_norm', 'w_kvf', 'b_f', 'w_up', 'conv_w', 'conv_b', 'w_down']
TWIN_DIFF_INPUT = 'x'
TWIN_INPUTS = ['x', 'norm_gains', 'w_qkv_a', 'w_o_a', 'w_q_b', 'w_o_b', 'kv_norm', 'w_kvf', 'b_f', 'w_up', 'conv_w', 'conv_b', 'w_down', 'loss_target', 'm_norm_gains', 'm_w_qkv_a', 'm_w_o_a', 'm_w_q_b', 'm_w_o_b', 'm_kv_norm', 'm_w_kvf', 'm_b_f', 'm_w_up', 'm_conv_w', 'm_conv_b', 'm_w_down', 'v_norm_gains', 'v_w_qkv_a', 'v_w_o_a', 'v_w_q_b', 'v_w_o_b', 'v_kv_norm', 'v_w_kvf', 'v_b_f', 'v_w_up', 'v_conv_w', 'v_conv_b', 'v_w_down']
TWIN_OUTPUTS = ['loss', 'grad_x', 'grad_norm_gains', 'grad_w_qkv_a', 'grad_w_o_a', 'grad_w_q_b', 'grad_w_o_b', 'grad_kv_norm', 'grad_w_kvf', 'grad_b_f', 'grad_w_up', 'grad_conv_w', 'grad_conv_b', 'grad_w_down', 'delta_norm_gains', 'delta_w_qkv_a', 'delta_w_o_a', 'delta_w_q_b', 'delta_w_o_b', 'delta_kv_norm', 'delta_w_kvf', 'delta_b_f', 'delta_w_up', 'delta_conv_w', 'delta_conv_b', 'delta_w_down', 'new_m_norm_gains', 'new_m_w_qkv_a', 'new_m_w_o_a', 'new_m_w_q_b', 'new_m_w_o_b', 'new_m_kv_norm', 'new_m_w_kvf', 'new_m_b_f', 'new_m_w_up', 'new_m_conv_w', 'new_m_conv_b', 'new_m_w_down', 'new_v_norm_gains', 'new_v_w_qkv_a', 'new_v_w_o_a', 'new_v_w_q_b', 'new_v_w_o_b', 'new_v_kv_norm', 'new_v_w_kvf', 'new_v_b_f', 'new_v_w_up', 'new_v_conv_w', 'new_v_conv_b', 'new_v_w_down']
TWIN_LEAF_KINDS = {'loss': 'loss', 'grad_x': 'grad_x', 'grad_norm_gains': 'grad_w', 'grad_w_qkv_a': 'grad_w', 'grad_w_o_a': 'grad_w', 'grad_w_q_b': 'grad_w', 'grad_w_o_b': 'grad_w', 'grad_kv_norm': 'grad_w', 'grad_w_kvf': 'grad_w', 'grad_b_f': 'grad_w', 'grad_w_up': 'grad_w', 'grad_conv_w': 'grad_w', 'grad_conv_b': 'grad_w', 'grad_w_down': 'grad_w', 'delta_norm_gains': 'delta_w', 'delta_w_qkv_a': 'delta_w', 'delta_w_o_a': 'delta_w', 'delta_w_q_b': 'delta_w', 'delta_w_o_b': 'delta_w', 'delta_kv_norm': 'delta_w', 'delta_w_kvf': 'delta_w', 'delta_b_f': 'delta_w', 'delta_w_up': 'delta_w', 'delta_conv_w': 'delta_w', 'delta_conv_b': 'delta_w', 'delta_w_down': 'delta_w', 'new_m_norm_gains': 'new_m', 'new_m_w_qkv_a': 'new_m', 'new_m_w_o_a': 'new_m', 'new_m_w_q_b': 'new_m', 'new_m_w_o_b': 'new_m', 'new_m_kv_norm': 'new_m', 'new_m_w_kvf': 'new_m', 'new_m_b_f': 'new_m', 'new_m_w_up': 'new_m', 'new_m_conv_w': 'new_m', 'new_m_conv_b': 'new_m', 'new_m_w_down': 'new_m', 'new_v_norm_gains': 'new_v', 'new_v_w_qkv_a': 'new_v', 'new_v_w_o_a': 'new_v', 'new_v_w_q_b': 'new_v', 'new_v_w_o_b': 'new_v', 'new_v_kv_norm': 'new_v', 'new_v_w_kvf': 'new_v', 'new_v_b_f': 'new_v', 'new_v_w_up': 'new_v', 'new_v_conv_w': 'new_v', 'new_v_conv_b': 'new_v', 'new_v_w_down': 'new_v'}


def _forward(args):
    return _fwd_reference(*[args[k] for k in FWD_PARAMS])


def _output_shape():
    out = _jax.eval_shape(lambda: _forward(_fwd_setup_inputs(0)))
    return out.shape, out.dtype

N_MICROBATCH = 1
ADAM_LR = 0.001
ADAM_B1 = 0.9
ADAM_B2 = 0.999
ADAM_EPS = 1e-08
ADAM_WD = 0.01
ADAM_STEP = 10
PER_EXAMPLE_BATCH_AXIS = {'x': 0, 'loss_target': 0}
SHARED_INPUTS = []
_WEIGHT_DTYPES = {'norm_gains': _jnp.float32, 'w_qkv_a': _jnp.float32, 'w_o_a': _jnp.float32, 'w_q_b': _jnp.float32, 'w_o_b': _jnp.float32, 'kv_norm': _jnp.float32, 'w_kvf': _jnp.float32, 'b_f': _jnp.float32, 'w_up': _jnp.float32, 'conv_w': _jnp.float32, 'conv_b': _jnp.float32, 'w_down': _jnp.float32}
MOMENT_SCALE = {'norm_gains': 1.021599e+01, 'w_qkv_a': 2.767649e+00, 'w_o_a': 3.815036e+00, 'w_q_b': 2.912457e-01, 'w_o_b': 2.207073e+00, 'kv_norm': 3.219069e+00, 'w_kvf': 2.277236e+00, 'b_f': 3.222822e+00, 'w_up': 7.006511e-01, 'conv_w': 7.471425e-01, 'conv_b': 1.844558e+00, 'w_down': 1.276930e+00}


def _to_microbatches(a, axis):
    t = _jnp.moveaxis(a, axis, 0)
    t = t.reshape((N_MICROBATCH, t.shape[0] // N_MICROBATCH) + t.shape[1:])
    return _jnp.moveaxis(t, 1, axis + 1)


def setup_inputs(seed: int = 0) -> dict:
    inp = _fwd_setup_inputs(seed)
    key = _jax.random.fold_in(_jax.random.key(seed), 7919)
    shape, _ = _output_shape()
    out = dict(inp)
    out["loss_target"] = _jax.random.normal(_jax.random.fold_in(key, 0), shape, _jnp.float32)
    for i, name in enumerate(TWIN_WEIGHTS):
        w = inp[name].astype(_jnp.float32)
        if MOMENT_SCALE is None:
            s = _jnp.sqrt(_jnp.mean(_jnp.square(w)) + 1e-30)
        else:
            s = MOMENT_SCALE[name]
        km, kv = _jax.random.split(_jax.random.fold_in(key, i + 1))
        out[name] = w
        out["m_" + name] = s * _jax.random.normal(km, w.shape, _jnp.float32)
        out["v_" + name] = (s * s) * _jax.random.uniform(kv, w.shape, _jnp.float32, 0.5, 1.5)
    if N_MICROBATCH > 1:
        for name, axis in PER_EXAMPLE_BATCH_AXIS.items():
            out[name] = _to_microbatches(out[name], axis)
    return {'x': out['x'], 'norm_gains': out['norm_gains'], 'w_qkv_a': out['w_qkv_a'], 'w_o_a': out['w_o_a'], 'w_q_b': out['w_q_b'], 'w_o_b': out['w_o_b'], 'kv_norm': out['kv_norm'], 'w_kvf': out['w_kvf'], 'b_f': out['b_f'], 'w_up': out['w_up'], 'conv_w': out['conv_w'], 'conv_b': out['conv_b'], 'w_down': out['w_down'], 'loss_target': out['loss_target'], 'm_norm_gains': out['m_norm_gains'], 'm_w_qkv_a': out['m_w_qkv_a'], 'm_w_o_a': out['m_w_o_a'], 'm_w_q_b': out['m_w_q_b'], 'm_w_o_b': out['m_w_o_b'], 'm_kv_norm': out['m_kv_norm'], 'm_w_kvf': out['m_w_kvf'], 'm_b_f': out['m_b_f'], 'm_w_up': out['m_w_up'], 'm_conv_w': out['m_conv_w'], 'm_conv_b': out['m_conv_b'], 'm_w_down': out['m_w_down'], 'v_norm_gains': out['v_norm_gains'], 'v_w_qkv_a': out['v_w_qkv_a'], 'v_w_o_a': out['v_w_o_a'], 'v_w_q_b': out['v_w_q_b'], 'v_w_o_b': out['v_w_o_b'], 'v_kv_norm': out['v_kv_norm'], 'v_w_kvf': out['v_w_kvf'], 'v_b_f': out['v_b_f'], 'v_w_up': out['v_w_up'], 'v_conv_w': out['v_conv_w'], 'v_conv_b': out['v_conv_b'], 'v_w_down': out['v_w_down']}


def _loss(weights, diff, rest, loss_target):
    with _jax.named_scope("forward"):
        args = {**rest, TWIN_DIFF_INPUT: diff, **{k: w.astype(_WEIGHT_DTYPES[k]) for k, w in weights.items()}}
        y = _forward(args)
    with _jax.named_scope("loss_head"):
        err = _jnp.square(y.astype(_jnp.float32) - loss_target)
        return 0.5 * _jnp.sum(_jnp.mean(err, axis=-1)) if err.ndim else 0.5 * err


def _adamw(w, g, m, v):
    m = ADAM_B1 * m + (1.0 - ADAM_B1) * g
    v = ADAM_B2 * v + (1.0 - ADAM_B2) * _jnp.square(g)
    m_hat = m / (1.0 - ADAM_B1 ** ADAM_STEP)
    v_hat = v / (1.0 - ADAM_B2 ** ADAM_STEP)
    delta = -ADAM_LR * (m_hat / (_jnp.sqrt(v_hat) + ADAM_EPS) + ADAM_WD * w)
    return delta, m, v


def reference(x, norm_gains, w_qkv_a, w_o_a, w_q_b, w_o_b, kv_norm, w_kvf, b_f, w_up, conv_w, conv_b, w_down, loss_target, m_norm_gains, m_w_qkv_a, m_w_o_a, m_w_q_b, m_w_o_b, m_kv_norm, m_w_kvf, m_b_f, m_w_up, m_conv_w, m_conv_b, m_w_down, v_norm_gains, v_w_qkv_a, v_w_o_a, v_w_q_b, v_w_o_b, v_kv_norm, v_w_kvf, v_b_f, v_w_up, v_conv_w, v_conv_b, v_w_down):
    given = dict(x=x, norm_gains=norm_gains, w_qkv_a=w_qkv_a, w_o_a=w_o_a, w_q_b=w_q_b, w_o_b=w_o_b, kv_norm=kv_norm, w_kvf=w_kvf, b_f=b_f, w_up=w_up, conv_w=conv_w, conv_b=conv_b, w_down=w_down, loss_target=loss_target, m_norm_gains=m_norm_gains, m_w_qkv_a=m_w_qkv_a, m_w_o_a=m_w_o_a, m_w_q_b=m_w_q_b, m_w_o_b=m_w_o_b, m_kv_norm=m_kv_norm, m_w_kvf=m_w_kvf, m_b_f=m_b_f, m_w_up=m_w_up, m_conv_w=m_conv_w, m_conv_b=m_conv_b, m_w_down=m_w_down, v_norm_gains=v_norm_gains, v_w_qkv_a=v_w_qkv_a, v_w_o_a=v_w_o_a, v_w_q_b=v_w_q_b, v_w_o_b=v_w_o_b, v_kv_norm=v_kv_norm, v_w_kvf=v_w_kvf, v_b_f=v_b_f, v_w_up=v_w_up, v_conv_w=v_conv_w, v_conv_b=v_conv_b, v_w_down=v_w_down)
    weights = {n: given[n] for n in TWIN_WEIGHTS}
    shared = {n: given[n] for n in SHARED_INPUTS}
    per_example = {n: given[n] for n in ['x']}
    grad_fn = _jax.value_and_grad(_loss, argnums=(0, 1))

    def one_microbatch(ex, loss_target):
        ex = dict(ex)
        diff = ex.pop(TWIN_DIFF_INPUT)
        return grad_fn(weights, diff, {**shared, **ex}, loss_target)

    if N_MICROBATCH == 1:
        loss, (grad_w, grad_x) = one_microbatch(per_example, given["loss_target"])
    else:
        def body(carry, xs):
            loss_sum, grad_sum = carry
            l_k, (gw_k, gx_k) = one_microbatch(xs[0], xs[1])
            with _jax.named_scope("update"):
                return (loss_sum + l_k, _jax.tree.map(_jnp.add, grad_sum, gw_k)), gx_k

        init = (_jnp.zeros((), _jnp.float32), _jax.tree.map(_jnp.zeros_like, weights))
        (loss, grad_w), grad_x = _jax.lax.scan(body, init, (per_example, given["loss_target"]))
    with _jax.named_scope("update"):
        delta_w, new_m, new_v = {}, {}, {}
        for n in TWIN_WEIGHTS:
            delta_w[n], new_m[n], new_v[n] = _adamw(weights[n], grad_w[n], given["m_" + n], given["v_" + n])
    return (loss, grad_x, *[grad_w[n] for n in TWIN_WEIGHTS], *[delta_w[n] for n in TWIN_WEIGHTS],
            *[new_m[n] for n in TWIN_WEIGHTS], *[new_v[n] for n in TWIN_WEIGHTS])
```

```python
import functools
import math

import jax
import jax.numpy as jnp
from jax import lax
from jax.experimental import pallas as pl
from jax.experimental.pallas import tpu as pltpu

F32 = jnp.float32
BF16 = jnp.bfloat16
MESH = pl.DeviceIdType.MESH
ANY = pl.BlockSpec(memory_space=pl.ANY)

T = 2048
D = 1024
HD = 64
DEPTH = 4
N_A = 2
A_W = 768
GW = 256
DIL = (1, 4, 16)
BLK = 128
D_FF = 2816
ROPE_THETA = 500000.0
EPS = 1e-6
NEG = -1e30
N_CHIPS = 4
PACK_W = 1024
FQ = 256
CT = 128
VMEM_BIG = 48 * 1024 * 1024

ADAM_LR, ADAM_B1, ADAM_B2, ADAM_EPS, ADAM_WD, ADAM_STEP = 0.001, 0.9, 0.999, 1e-08, 0.01, 10

NN = (((1,), (0,)), ((), ()))
NT = (((1,), (1,)), ((), ()))
TN = (((0,), (0,)), ((), ()))


def _dot(a, b, dims):
    return lax.dot_general(a, b, dims, preferred_element_type=F32)


def _pick(dim, pref):
    if dim <= pref:
        return dim
    best = None
    for t in range(128, pref + 1, 128):
        if dim % t == 0:
            best = t
    assert best is not None, (dim, pref)
    return best


def _params(sem=None, vmem=None):
    kw = {}
    if sem is not None:
        kw["dimension_semantics"] = sem
    if vmem is not None:
        kw["vmem_limit_bytes"] = vmem
    return pltpu.CompilerParams(**kw)


def _matmul(a, b, *, mode, out_dtype, name, alpha=None, tm=1024, tn=512, tk=512):
    if mode == "nn":
        (M, K), (K2, N) = a.shape, b.shape
    elif mode == "nt":
        (M, K), (N, K2) = a.shape, b.shape
    else:
        (K, M), (K2, N) = a.shape, b.shape
    assert K == K2, (a.shape, b.shape, mode)
    tm, tn, tk = _pick(M, tm), _pick(N, tn), _pick(K, tk)
    nk = K // tk
    dims = {"nn": NN, "nt": NT, "tn": TN}[mode]

    def body(a_ref, b_ref, o_ref, acc_ref):
        k = pl.program_id(2)

        @pl.when(k == 0)
        def _():
            acc_ref[...] = jnp.zeros_like(acc_ref)

        acc_ref[...] += _dot(a_ref[...], b_ref[...], dims)

        @pl.when(k == nk - 1)
        def _():
            r = acc_ref[...]
            if alpha is not None:
                r = r * alpha
            o_ref[...] = r.astype(out_dtype)

    if mode == "tn":
        a_spec = pl.BlockSpec((tk, tm), lambda i, j, k: (k, i))
    else:
        a_spec = pl.BlockSpec((tm, tk), lambda i, j, k: (i, k))
    if mode == "nt":
        b_spec = pl.BlockSpec((tn, tk), lambda i, j, k: (j, k))
    else:
        b_spec = pl.BlockSpec((tk, tn), lambda i, j, k: (k, j))
    return pl.pallas_call(
        body,
        grid=(M // tm, N // tn, nk),
        in_specs=[a_spec, b_spec],
        out_specs=pl.BlockSpec((tm, tn), lambda i, j, k: (i, j)),
        out_shape=jax.ShapeDtypeStruct((M, N), out_dtype),
        scratch_shapes=[pltpu.VMEM((tm, tn), F32)],
        compiler_params=_params(("parallel", "parallel", "arbitrary"), VMEM_BIG),
        name=name,
    )(a, b)


def _rms_fwd(x, g, *, out_dtype, name, res=None, tr=256):
    n, d = x.shape

    def body(*refs):
        x_ref, g_ref = refs[0], refs[1]
        o_ref = refs[-1]
        xv = x_ref[...].astype(F32)
        y = xv * lax.rsqrt(jnp.mean(xv * xv, axis=-1, keepdims=True) + EPS) * g_ref[...]
        if res is not None:
            y = y + refs[2][...]
        o_ref[...] = y.astype(out_dtype)

    row = pl.BlockSpec((tr, d), lambda i: (i, 0))
    vec = pl.BlockSpec((1, d), lambda i: (0, 0))
    ins = [x, g] + ([] if res is None else [res])
    specs = [row, vec] + ([] if res is None else [row])
    return pl.pallas_call(
        body, grid=(n // tr,), in_specs=specs, out_specs=row,
        out_shape=jax.ShapeDtypeStruct((n, d), out_dtype),
        compiler_params=_params(("parallel",)), name=name,
    )(*ins)


def _rms_bwd(dy, x, g, *, out_dtype, name, dres=None, tr=256):
    n, d = x.shape

    def body(*refs):
        dy_ref, x_ref, g_ref = refs[0], refs[1], refs[2]
        dx_ref, dg_ref = refs[-2], refs[-1]
        xv = x_ref[...].astype(F32)
        dyv = dy_ref[...].astype(F32)
        rstd = lax.rsqrt(jnp.mean(xv * xv, axis=-1, keepdims=True) + EPS)
        xhat = xv * rstd
        dxh = dyv * g_ref[...]
        dx = rstd * (dxh - xhat * jnp.mean(dxh * xhat, axis=-1, keepdims=True))
        if dres is not None:
            dx = dx + refs[3][...]
        dx_ref[...] = dx.astype(out_dtype)

        @pl.when(pl.program_id(0) == 0)
        def _():
            dg_ref[...] = jnp.zeros_like(dg_ref)

        dg_ref[...] += jnp.sum(dyv * xhat, axis=0, keepdims=True)

    row = pl.BlockSpec((tr, d), lambda i: (i, 0))
    vec = pl.BlockSpec((1, d), lambda i: (0, 0))
    ins = [dy, x, g] + ([] if dres is None else [dres])
    specs = [row, row, vec] + ([] if dres is None else [row])
    return pl.pallas_call(
        body, grid=(n // tr,), in_specs=specs, out_specs=[row, vec],
        out_shape=[jax.ShapeDtypeStruct((n, d), out_dtype), jax.ShapeDtypeStruct((1, d), F32)],
        compiler_params=_params(("arbitrary",)), name=name,
    )(*ins)


def _loss_head(h, target, *, tr=256):
    n, d = h.shape

    def body(h_ref, t_ref, dh_ref, s_ref):
        err = h_ref[...] - t_ref[...]
        dh_ref[...] = err * (1.0 / d)

        @pl.when(pl.program_id(0) == 0)
        def _():
            s_ref[...] = jnp.zeros_like(s_ref)

        s_ref[...] += jnp.sum(err * err)

    row = pl.BlockSpec((tr, d), lambda i: (i, 0))
    acc = pl.BlockSpec((8, 128), lambda i: (0, 0))
    return pl.pallas_call(
        body, grid=(n // tr,), in_specs=[row, row], out_specs=[row, acc],
        out_shape=[jax.ShapeDtypeStruct((n, d), F32), jax.ShapeDtypeStruct((8, 128), F32)],
        compiler_params=_params(("arbitrary",)), name="loss_head",
    )(h, target)


def _rope_tables():
    pos = jnp.arange(T, dtype=F32)
    inv = ROPE_THETA ** (-jnp.arange(0, 16, 2, dtype=F32) / 16)
    ang = pos[:, None] * inv[None, :]
    cos, sin = jnp.cos(ang), jnp.sin(ang)
    one = jnp.ones((T, HD - 16), F32)
    zero8 = jnp.zeros((T, 8), F32)
    zero = jnp.zeros((T, HD - 16), F32)
    c = jnp.concatenate([cos, cos, one], axis=1)
    s1 = jnp.concatenate([zero8, sin, zero], axis=1)
    s2 = jnp.concatenate([-sin, zero8, zero], axis=1)
    return tuple(jnp.concatenate([t, t], axis=1) for t in (c, s1, s2))


def _rope_fwd(qkv, tabs, *, tr=256):
    def body(x_ref, c_ref, s1_ref, s2_ref, q_ref, k_ref, v_ref):
        c, s1, s2 = c_ref[...], s1_ref[...], s2_ref[...]
        for which, o_ref, scale in ((0, q_ref, HD ** -0.5), (1, k_ref, None)):
            for j in range(A_W // 128):
                x = x_ref[:, which * A_W + j * 128: which * A_W + (j + 1) * 128]
                y = x * c + pltpu.roll(x, 8, 1) * s1 + pltpu.roll(x, 120, 1) * s2
                if scale is not None:
                    y = y * scale
                o_ref[j // 2, :, (j % 2) * 128:(j % 2 + 1) * 128] = y.astype(BF16)
        for j in range(A_W // 128):
            v_ref[j // 2, :, (j % 2) * 128:(j % 2 + 1) * 128] = x_ref[:, 2 * A_W + j * 128: 2 * A_W + (j + 1) * 128].astype(BF16)

    tab = pl.BlockSpec((tr, 128), lambda i: (i, 0))
    out = pl.BlockSpec((3, tr, GW), lambda i: (0, i, 0))
    shp = jax.ShapeDtypeStruct((3, T, GW), BF16)
    return pl.pallas_call(
        body, grid=(T // tr,), in_specs=[pl.BlockSpec((tr, 3 * A_W), lambda i: (i, 0)), tab, tab, tab],
        out_specs=[out, out, out], out_shape=[shp, shp, shp],
        compiler_params=_params(("parallel",)), name="rope_fwd",
    )(qkv, *tabs)


def _rope_bwd(dq, dk, dv, tabs, *, tr=256):
    def body(dq_ref, dk_ref, dv_ref, c_ref, s1_ref, s2_ref, o_ref):
        c, s1, s2 = c_ref[...], s1_ref[...], s2_ref[...]
        for which, i_ref, scale in ((0, dq_ref, HD ** -0.5), (1, dk_ref, None)):
            for j in range(A_W // 128):
                g = i_ref[j // 2, :, (j % 2) * 128:(j % 2 + 1) * 128]
                y = g * c + pltpu.roll(g * s1, 120, 1) + pltpu.roll(g * s2, 8, 1)
                if scale is not None:
                    y = y * scale
                o_ref[:, which * A_W + j * 128: which * A_W + (j + 1) * 128] = y.astype(BF16)
        for j in range(A_W // 128):
            o_ref[:, 2 * A_W + j * 128: 2 * A_W + (j + 1) * 128] = dv_ref[j // 2, :, (j % 2) * 128:(j % 2 + 1) * 128].astype(BF16)

    tab = pl.BlockSpec((tr, 128), lambda i: (i, 0))
    cot = pl.BlockSpec((3, tr, GW), lambda i: (0, i, 0))
    return pl.pallas_call(
        body, grid=(T // tr,), in_specs=[cot, cot, cot, tab, tab, tab],
        out_specs=pl.BlockSpec((tr, 3 * A_W), lambda i: (i, 0)),
        out_shape=jax.ShapeDtypeStruct((T, 3 * A_W), BF16),
        compiler_params=_params(("parallel",)), name="rope_bwd",
    )(dq, dk, dv, *tabs)


def _perm(x3):
    out = [x3[0]]
    for g in (1, 2):
        r = DIL[g]
        out.append(x3[g].reshape(T // r, r, GW).transpose(1, 0, 2).reshape(T, GW))
    return jnp.stack(out)


def _unperm(x3):
    out = [x3[0]]
    for g in (1, 2):
        r = DIL[g]
        out.append(x3[g].reshape(r, T // r, GW).transpose(1, 0, 2).reshape(T, GW))
    return jnp.stack(out)


def _head_mask(x, lane_lo):
    lane = lax.broadcasted_iota(jnp.int32, x.shape, 1)
    keep = (lane < HD) if lane_lo else (lane >= HD)
    return jnp.where(keep, x.astype(F32), 0.0).astype(BF16)


def _band_scalars():
    g, b = pl.program_id(0), pl.program_id(1)
    nbs = lax.shift_right_logical(jnp.int32(T // BLK), 2 * g)
    has_prev = jnp.where((b & (nbs - 1)) != 0, 1, 0)
    next_ok = jnp.where(((b + 1) & (nbs - 1)) != 0, 1, 0)
    return has_prev, next_ok


def _band_mask_q(has_prev):
    row = lax.broadcasted_iota(jnp.int32, (BLK, 2 * BLK), 0)
    col = lax.broadcasted_iota(jnp.int32, (BLK, 2 * BLK), 1)
    return ((col < BLK) & (col >= row) & (has_prev == 1)) | ((col >= BLK) & (col - BLK <= row))


def _band_mask_k(next_ok):
    row = lax.broadcasted_iota(jnp.int32, (2 * BLK, BLK), 0)
    col = lax.broadcasted_iota(jnp.int32, (2 * BLK, BLK), 1)
    return ((row < BLK) & (col <= row)) | ((row >= BLK) & (col >= row - BLK) & (next_ok == 1))


def _band_fwd(q, k, v):
    nb = T // BLK

    def body(q_ref, kc_ref, kp_ref, vc_ref, vp_ref, o_ref, l_ref):
        has_prev, _ = _band_scalars()
        mask = _band_mask_q(has_prev)
        lane = lax.broadcasted_iota(jnp.int32, (BLK, 128), 1)
        for p in range(2):
            sl = slice(128 * p, 128 * (p + 1))
            qp = q_ref[0, :, sl]
            kcat = jnp.concatenate([kp_ref[0, :, sl], kc_ref[0, :, sl]], axis=0)
            vcat = jnp.concatenate([vp_ref[0, :, sl], vc_ref[0, :, sl]], axis=0)
            o_acc = jnp.zeros((BLK, 128), F32)
            lse = jnp.zeros((BLK, 128), F32)
            for e in range(2):
                s = _dot(_head_mask(qp, e == 0), kcat, NT)
                s = jnp.where(mask, s, NEG)
                m = jnp.max(s, axis=-1, keepdims=True)
                pr = jnp.exp(s - m)
                l = jnp.sum(pr, axis=-1, keepdims=True)
                o_acc = o_acc + _dot(pr.astype(BF16), _head_mask(vcat, e == 0), NN) / l
                lse = jnp.where((lane < HD) if e == 0 else (lane >= HD), m + jnp.log(l), lse)
            o_ref[0, :, sl] = o_acc
            l_ref[0, :, sl] = lse

    cur = pl.BlockSpec((1, BLK, GW), lambda g, b: (g, b, 0))
    prev = pl.BlockSpec((1, BLK, GW), lambda g, b: (g, jnp.maximum(b - 1, 0), 0))
    shp = jax.ShapeDtypeStruct((3, T, GW), F32)
    return pl.pallas_call(
        body, grid=(3, nb), in_specs=[cur, cur, prev, cur, prev], out_specs=[cur, cur], out_shape=[shp, shp],
        compiler_params=_params(("parallel", "parallel")), name="band_fwd",
    )(q, k, k, v, v)


def _band_bwd(q, k, v, do, lse, dlt):
    nb = T // BLK

    def body(qc_ref, qn_ref, kc_ref, kp_ref, vc_ref, vp_ref, doc_ref, don_ref, lc_ref, ln_ref, dc_ref, dn_ref,
             dq_ref, dk_ref, dv_ref):
        has_prev, next_ok = _band_scalars()
        mask_q = _band_mask_q(has_prev)
        mask_k = _band_mask_k(next_ok)
        for p in range(2):
            sl = slice(128 * p, 128 * (p + 1))
            qc, qn = qc_ref[0, :, sl], qn_ref[0, :, sl]
            doc, don = doc_ref[0, :, sl], don_ref[0, :, sl]
            kc, vc = kc_ref[0, :, sl], vc_ref[0, :, sl]
            kcat = jnp.concatenate([kp_ref[0, :, sl], kc], axis=0)
            vcat = jnp.concatenate([vp_ref[0, :, sl], vc], axis=0)
            qcat = jnp.concatenate([qc, qn], axis=0)
            docat = jnp.concatenate([doc, don], axis=0)
            dq = jnp.zeros((BLK, 128), F32)
            dk = jnp.zeros((BLK, 128), F32)
            dv = jnp.zeros((BLK, 128), F32)
            for e in range(2):
                lo = e == 0
                col = slice(128 * p + HD * e, 128 * p + HD * e + 1)
                lse_c, lse_n = lc_ref[0, :, col], ln_ref[0, :, col]
                dl_c, dl_n = dc_ref[0, :, col], dn_ref[0, :, col]
                s = jnp.where(mask_q, _dot(_head_mask(qc, lo), kcat, NT), NEG)
                pr = jnp.exp(s - lse_c)
                dp = _dot(_head_mask(doc, lo), vcat, NT)
                ds = pr * (dp - dl_c)
                dq = dq + _dot(ds.astype(BF16), _head_mask(kcat, lo), NN)
                qm, dom = _head_mask(qcat, lo), _head_mask(docat, lo)
                s2 = jnp.where(mask_k, _dot(qm, kc, NT), NEG)
                p2 = jnp.exp(s2 - jnp.concatenate([lse_c, lse_n], axis=0))
                dv = dv + _dot(p2.astype(BF16), dom, TN)
                dp2 = _dot(dom, vc, NT)
                ds2 = p2 * (dp2 - jnp.concatenate([dl_c, dl_n], axis=0))
                dk = dk + _dot(ds2.astype(BF16), qm, TN)
            dq_ref[0, :, sl] = dq
            dk_ref[0, :, sl] = dk
            dv_ref[0, :, sl] = dv

    cur = pl.BlockSpec((1, BLK, GW), lambda g, b: (g, b, 0))
    prev = pl.BlockSpec((1, BLK, GW), lambda g, b: (g, jnp.maximum(b - 1, 0), 0))
    nxt = pl.BlockSpec((1, BLK, GW), lambda g, b: (g, jnp.minimum(b + 1, nb - 1), 0))
    shp = jax.ShapeDtypeStruct((3, T, GW), F32)
    return pl.pallas_call(
        body, grid=(3, nb),
        in_specs=[cur, nxt, cur, prev, cur, prev, cur, nxt, cur, nxt, cur, nxt],
        out_specs=[cur, cur, cur], out_shape=[shp, shp, shp],
        compiler_params=_params(("parallel", "parallel")), name="band_bwd",
    )(q, q, k, k, v, v, do, do, lse, lse, dlt, dlt)


def _split3(x):
    hi = x.astype(BF16)
    r = x - hi.astype(F32)
    mid = r.astype(BF16)
    lo = (r - mid.astype(F32)).astype(BF16)
    return hi, mid, lo


def _dot3(x, m, dims=NN):
    hi, mid, lo = _split3(x)
    return _dot(hi, m, dims) + _dot(mid, m, dims) + _dot(lo, m, dims)


def _combine_weights(l_ref):
    l0, l1, l2 = l_ref[0], l_ref[1], l_ref[2]
    m = jnp.maximum(jnp.maximum(l0, l1), l2)
    e = [jnp.exp(l0 - m), jnp.exp(l1 - m), jnp.exp(l2 - m)]
    inv = 1.0 / (e[0] + e[1] + e[2])
    return [ei * inv for ei in e]


def _combine_fwd(o, lse, *, tr=256):
    def body(o_ref, l_ref, out_ref):
        alpha = _combine_weights(l_ref)
        for g in range(3):
            out_ref[:, g * GW:(g + 1) * GW] = (o_ref[g] * alpha[g]).astype(BF16)

    blk = pl.BlockSpec((3, tr, GW), lambda i: (0, i, 0))
    return pl.pallas_call(
        body, grid=(T // tr,), in_specs=[blk, blk], out_specs=pl.BlockSpec((tr, A_W), lambda i: (i, 0)),
        out_shape=jax.ShapeDtypeStruct((T, A_W), BF16), compiler_params=_params(("parallel",)), name="combine_fwd",
    )(o, lse)


def _combine_bwd(datt, o, lse, headsum, *, tr=256):
    def body(d_ref, o_ref, l_ref, hs_ref, do_ref, dl_ref):
        alpha = _combine_weights(l_ref)
        hs = hs_ref[...]
        total = jnp.zeros((tr, GW), F32)
        for g in range(3):
            dg = d_ref[:, g * GW:(g + 1) * GW]
            do_ref[g] = (dg * alpha[g]).astype(BF16)
            total = total + alpha[g] * _dot3(dg * o_ref[g], hs)
        for g in range(3):
            dl_ref[g] = alpha[g] * total

    blk = pl.BlockSpec((3, tr, GW), lambda i: (0, i, 0))
    return pl.pallas_call(
        body, grid=(T // tr,),
        in_specs=[pl.BlockSpec((tr, A_W), lambda i: (i, 0)), blk, blk, pl.BlockSpec((GW, GW), lambda i: (0, 0))],
        out_specs=[blk, blk],
        out_shape=[jax.ShapeDtypeStruct((3, T, GW), BF16), jax.ShapeDtypeStruct((3, T, GW), F32)],
        compiler_params=_params(("parallel",)), name="combine_bwd",
    )(datt, o, lse, headsum)


def _fox_scores(qm, k_ref, cq, ck_ref, e, i, n):
    s = _dot(qm, k_ref[0:n, :], NT) + (cq - ck_ref[0, e:e + 1, 0:n])
    row = lax.broadcasted_iota(jnp.int32, (FQ, n), 0)
    col = lax.broadcasted_iota(jnp.int32, (FQ, n), 1)
    s = jnp.where(col <= row + i * FQ, s, NEG)
    m = jnp.max(s, axis=-1, keepdims=True)
    pr = jnp.exp(s - m)
    return pr, jnp.sum(pr, axis=-1, keepdims=True)


def _fox_fwd(q, kv, c_col, c_row):
    def body(q_ref, k_ref, v_ref, cc_ref, cr_ref, o_ref, vm_ref):
        for e in range(2):
            vm_ref[e] = _head_mask(v_ref[...], e == 0)
        for i in range(T // FQ):
            n = (i + 1) * FQ
            rows = slice(i * FQ, n)
            acc = jnp.zeros((FQ, 128), F32)
            for e in range(2):
                qm = _head_mask(q_ref[rows, :], e == 0)
                pr, l = _fox_scores(qm, k_ref, cc_ref[0, rows, e:e + 1], cr_ref, e, i, n)
                acc = acc + _dot(pr.astype(BF16), vm_ref[e, 0:n, :], NN) / l
            o_ref[rows, :] = acc.astype(BF16)

    pair = pl.BlockSpec((T, 128), lambda p: (0, p))
    return pl.pallas_call(
        body, grid=(D // 128,),
        in_specs=[pair, pair, pl.BlockSpec((T, 128), lambda p: (0, D // 128 + p)),
                  pl.BlockSpec((1, T, 2), lambda p: (p, 0, 0)), pl.BlockSpec((1, 2, T), lambda p: (p, 0, 0))],
        out_specs=pair, out_shape=jax.ShapeDtypeStruct((T, D), BF16),
        scratch_shapes=[pltpu.VMEM((2, T, 128), BF16)],
        compiler_params=_params(("parallel",), VMEM_BIG), name="fox_fwd",
    )(q, kv, kv, c_col, c_row)


def _fox_bwd(q, kv, do, c_col, c_row, init):
    def body(q_ref, k_ref, v_ref, do_ref, cc_ref, cr_ref, ik_ref, iv_ref, iq_ref, ic_ref,
             dq_ref, dk_ref, dv_ref, dcq_ref, dck_ref, km_ref):
        dk_ref[...] = ik_ref[...]
        dv_ref[...] = iv_ref[...]
        dcq_ref[...] = iq_ref[...]
        dck_ref[...] = ic_ref[...]
        for e in range(2):
            km_ref[e] = _head_mask(k_ref[...], e == 0)
        for i in range(T // FQ):
            n = (i + 1) * FQ
            rows = slice(i * FQ, n)
            dq = jnp.zeros((FQ, 128), F32)
            for e in range(2):
                qm = _head_mask(q_ref[rows, :], e == 0)
                dom = _head_mask(do_ref[rows, :], e == 0)
                pr, l = _fox_scores(qm, k_ref, cc_ref[0, rows, e:e + 1], cr_ref, e, i, n)
                pr = pr / l
                dp = _dot(dom, v_ref[0:n, :], NT)
                ds = pr * (dp - jnp.sum(pr * dp, axis=-1, keepdims=True))
                dsb = ds.astype(BF16)
                dq = dq + _dot(dsb, km_ref[e, 0:n, :], NN)
                dk_ref[0:n, :] += _dot(dsb, qm, TN)
                dv_ref[0:n, :] += _dot(pr.astype(BF16), dom, TN)
                dcq_ref[0, rows, e:e + 1] += jnp.sum(ds, axis=-1, keepdims=True)
                dck_ref[0, e:e + 1, 0:n] += jnp.sum(ds, axis=0, keepdims=True)
            dq_ref[rows, :] = (dq * HD ** -0.5).astype(BF16)

    pair = pl.BlockSpec((T, 128), lambda p: (0, p))
    cq = pl.BlockSpec((1, T, 128), lambda p: (p, 0, 0))
    ck = pl.BlockSpec((1, 8, T), lambda p: (p, 0, 0))
    return pl.pallas_call(
        body, grid=(D // 128,),
        in_specs=[pair, pair, pl.BlockSpec((T, 128), lambda p: (0, D // 128 + p)), pair,
                  pl.BlockSpec((1, T, 2), lambda p: (p, 0, 0)), pl.BlockSpec((1, 2, T), lambda p: (p, 0, 0)),
                  pair, pair, cq, ck],
        out_specs=[pair, pair, pair, cq, ck],
        out_shape=[jax.ShapeDtypeStruct((T, D), BF16), jax.ShapeDtypeStruct((T, D), F32), jax.ShapeDtypeStruct((T, D), F32),
                   jax.ShapeDtypeStruct((D // 128, T, 128), F32), jax.ShapeDtypeStruct((D // 128, 8, T), F32)],
        scratch_shapes=[pltpu.VMEM((2, T, 128), BF16)],
        compiler_params=_params(("parallel",), VMEM_BIG), name="fox_bwd",
    )(q, kv, kv, do, c_col, c_row, *init)


def _tri(lower):
    r = lax.broadcasted_iota(jnp.int32, (BLK, BLK), 0)
    c = lax.broadcasted_iota(jnp.int32, (BLK, BLK), 1)
    return jnp.where((c <= r) if lower else (c >= r), 1.0, 0.0).astype(BF16)


def _gates_fwd(z, b):
    def body(z_ref, b_ref, c_ref):
        tri = _tri(True)
        carry = jnp.zeros((1, 128), F32)
        for i in range(T // BLK):
            rows = slice(i * BLK, (i + 1) * BLK)
            x = z_ref[rows, :] + b_ref[...]
            logf = jnp.minimum(x, 0.0) - jnp.log(1.0 + jnp.exp(-jnp.abs(x)))
            hi, mid, lo = _split3(logf)
            y = _dot(tri, hi, NN) + _dot(tri, mid, NN) + _dot(tri, lo, NN) + carry
            c_ref[rows, :] = y
            carry = y[BLK - 1:BLK, :]

    return pl.pallas_call(body, out_shape=jax.ShapeDtypeStruct((T, 128), F32), name="gates_fwd")(z, b)


def _gates_bwd(dc, z, b):
    def body(dc_ref, z_ref, b_ref, dz_ref, db_ref):
        tri = _tri(False)
        carry = jnp.zeros((1, 128), F32)
        db = jnp.zeros((1, 128), F32)
        for i in reversed(range(T // BLK)):
            rows = slice(i * BLK, (i + 1) * BLK)
            hi, mid, lo = _split3(dc_ref[rows, :])
            dlogf = _dot(tri, hi, NN) + _dot(tri, mid, NN) + _dot(tri, lo, NN) + carry
            carry = dlogf[0:1, :]
            x = z_ref[rows, :] + b_ref[...]
            dz = dlogf / (1.0 + jnp.exp(x))
            dz_ref[rows, :] = dz.astype(BF16)
            db = db + jnp.sum(dz, axis=0, keepdims=True)
        db_ref[...] = db

    return pl.pallas_call(
        body, out_shape=[jax.ShapeDtypeStruct((T, 128), BF16), jax.ShapeDtypeStruct((1, 128), F32)], name="gates_bwd",
    )(dc, z, b)


def _conv_pair(a_ref, cw_ref, cb_ref):
    row = lax.broadcasted_iota(jnp.int32, (T, CT), 0)
    outs = []
    for h in range(2):
        sl = slice(h * CT, (h + 1) * CT)
        z = a_ref[:, sl]
        z1 = jnp.where(row >= 1, pltpu.roll(z, 1, 0), 0.0)
        z2 = jnp.where(row >= 2, pltpu.roll(z, 2, 0), 0.0)
        y = cw_ref[2:3, sl] * z + cw_ref[1:2, sl] * z1 + cw_ref[0:1, sl] * z2 + cb_ref[:, sl]
        outs.append((y, z, z1, z2))
    return outs


_GELU_K = math.sqrt(2.0 / math.pi)


def _convgate_fwd(a, cw, cb):
    def body(a_ref, cw_ref, cb_ref, u_ref):
        (g, _, _, _), (v, _, _, _) = _conv_pair(a_ref, cw_ref, cb_ref)
        th = jnp.tanh(_GELU_K * (g + 0.044715 * g * g * g))
        u_ref[...] = (0.5 * g * (1.0 + th) * v).astype(BF16)

    return pl.pallas_call(
        body, grid=(D_FF // CT,),
        in_specs=[pl.BlockSpec((T, 2 * CT), lambda j: (0, j)), pl.BlockSpec((3, 2 * CT), lambda j: (0, j)),
                  pl.BlockSpec((1, 2 * CT), lambda j: (0, j))],
        out_specs=pl.BlockSpec((T, CT), lambda j: (0, j)), out_shape=jax.ShapeDtypeStruct((T, D_FF), BF16),
        compiler_params=_params(("parallel",), VMEM_BIG), name="convgate_fwd",
    )(a, cw, cb)


def _convgate_bwd(a, du, cw, cb):
    def body(a_ref, du_ref, cw_ref, cb_ref, da_ref, dcw_ref, dcb_ref):
        (g, gz, gz1, gz2), (v, vz, vz1, vz2) = _conv_pair(a_ref, cw_ref, cb_ref)
        du = du_ref[...].astype(F32)
        inner = _GELU_K * (g + 0.044715 * g * g * g)
        th = jnp.tanh(inner)
        gelu = 0.5 * g * (1.0 + th)
        dgelu = 0.5 * (1.0 + th) + 0.5 * g * (1.0 - th * th) * _GELU_K * (1.0 + 3 * 0.044715 * g * g)
        row = lax.broadcasted_iota(jnp.int32, (T, CT), 0)
        for h, (d, z, z1, z2) in enumerate(((du * v * dgelu, gz, gz1, gz2), (du * gelu, vz, vz1, vz2))):
            sl = slice(h * CT, (h + 1) * CT)
            d1 = jnp.where(row < T - 1, pltpu.roll(d, T - 1, 0), 0.0)
            d2 = jnp.where(row < T - 2, pltpu.roll(d, T - 2, 0), 0.0)
            da_ref[:, sl] = (cw_ref[2:3, sl] * d + cw_ref[1:2, sl] * d1 + cw_ref[0:1, sl] * d2).astype(BF16)
            dcw_ref[0:1, sl] = jnp.sum(d * z2, axis=0, keepdims=True)
            dcw_ref[1:2, sl] = jnp.sum(d * z1, axis=0, keepdims=True)
            dcw_ref[2:3, sl] = jnp.sum(d * z, axis=0, keepdims=True)
            dcb_ref[:, sl] = jnp.sum(d, axis=0, keepdims=True)

    wide = pl.BlockSpec((T, 2 * CT), lambda j: (0, j))
    w3 = pl.BlockSpec((3, 2 * CT), lambda j: (0, j))
    w1 = pl.BlockSpec((1, 2 * CT), lambda j: (0, j))
    return pl.pallas_call(
        body, grid=(D_FF // CT,),
        in_specs=[wide, pl.BlockSpec((T, CT), lambda j: (0, j)), w3, w1],
        out_specs=[wide, w3, w1],
        out_shape=[jax.ShapeDtypeStruct((T, 2 * D_FF), BF16), jax.ShapeDtypeStruct((3, 2 * D_FF), F32),
                   jax.ShapeDtypeStruct((1, 2 * D_FF), F32)],
        compiler_params=_params(("parallel",), VMEM_BIG), name="convgate_bwd",
    )(a, du, cw, cb)


def _interleave(x):
    lead = x.shape[:-1]
    return x.reshape(*lead, 2, D_FF // CT, CT).swapaxes(-3, -2).reshape(*lead, 2 * D_FF)


def _deinterleave(x):
    lead = x.shape[:-1]
    return x.reshape(*lead, D_FF // CT, 2, CT).swapaxes(-3, -2).reshape(*lead, 2 * D_FF)


def _adamw(w, m, v, g, *, name):
    r, c = w.shape
    tr = r
    if r * c > 256 * 1024:
        for cand in range(8, r, 8):
            if r % cand == 0 and cand * c <= 256 * 1024:
                tr = cand

    def body(w_ref, m_ref, v_ref, g_ref, d_ref, nm_ref, nv_ref):
        gv = g_ref[...]
        mn = ADAM_B1 * m_ref[...] + (1.0 - ADAM_B1) * gv
        vn = ADAM_B2 * v_ref[...] + (1.0 - ADAM_B2) * (gv * gv)
        m_hat = mn / (1.0 - ADAM_B1 ** ADAM_STEP)
        v_hat = vn / (1.0 - ADAM_B2 ** ADAM_STEP)
        d_ref[...] = -ADAM_LR * (m_hat / (jnp.sqrt(v_hat) + ADAM_EPS) + ADAM_WD * w_ref[...])
        nm_ref[...] = mn
        nv_ref[...] = vn

    blk = pl.BlockSpec((tr, c), lambda i: (i, 0))
    shp = jax.ShapeDtypeStruct((r, c), F32)
    return pl.pallas_call(
        body, grid=(r // tr,), in_specs=[blk] * 4, out_specs=[blk] * 3, out_shape=[shp] * 3,
        compiler_params=_params(("parallel",)), name=name,
    )(w, m, v, g)


def _place():
    x, y, c = lax.axis_index("x"), lax.axis_index("y"), lax.axis_index("c")
    chips = [(1 - x, y), (x, 1 - y), (1 - x, 1 - y)]
    return x, y, c, chips


def _allgather(tensors, *, name):
    n = len(tensors)

    def body(*refs):
        ins, outs = refs[:n], refs[n:2 * n]
        send, recv, loc = refs[2 * n:]
        x, y, c, chips = _place()
        me = 2 * x + y
        sib = (x, y, 1 - c)

        def piece(i, chip_id, core):
            h = ins[i].shape[0] // 2
            return outs[i].at[chip_id, pl.ds(core * h, h)]

        def rcopy(i, k, src, dst, to):
            return pltpu.make_async_remote_copy(src_ref=src, dst_ref=dst, send_sem=send.at[i * 6 + k], recv_sem=recv.at[i * 6 + k],
                                                device_id=to, device_id_type=MESH)

        started, local = [], []
        for i in range(n):
            h = ins[i].shape[0] // 2
            mine = pltpu.make_async_copy(ins[i], outs[i].at[me], loc.at[i])
            mine.start()
            local.append(mine)
            for k, (px, py) in enumerate(chips):
                cp = rcopy(i, k, ins[i].at[pl.ds(c * h, h)], piece(i, me, c), (px, py, c))
                cp.start()
                started.append(cp)
        for i in range(n):
            for k, (px, py) in enumerate(chips):
                landed = piece(i, 2 * px + py, c)
                rcopy(i, k, landed, landed, (px, py, c)).wait_recv()
                fw = rcopy(i, 3 + k, landed, landed, sib)
                fw.start()
                started.append(fw)
        for i in range(n):
            for k, (px, py) in enumerate(chips):
                other = piece(i, 2 * px + py, 1 - c)
                rcopy(i, 3 + k, other, other, sib).wait_recv()
        for cp in started:
            cp.wait_send()
        for cp in local:
            cp.wait()

    return pl.pallas_call(
        body, in_specs=[ANY] * n, out_specs=[ANY] * n,
        out_shape=[jax.ShapeDtypeStruct((N_CHIPS,) + t.shape, t.dtype) for t in tensors],
        scratch_shapes=[pltpu.SemaphoreType.DMA((6 * n,)), pltpu.SemaphoreType.DMA((6 * n,)), pltpu.SemaphoreType.DMA((n,))],
        name=name,
    )(*tensors)


def _swap_halves(tensors, *, name):
    n = len(tensors)

    def body(*refs):
        ins, outs = refs[:n], refs[n:2 * n]
        send, recv = refs[2 * n:]
        x, y, c, _ = _place()
        cps = []
        for i in range(n):
            h = ins[i].shape[1] // 2
            cp = pltpu.make_async_remote_copy(src_ref=ins[i].at[:, pl.ds((1 - c) * h, h)], dst_ref=outs[i], send_sem=send.at[i],
                                              recv_sem=recv.at[i], device_id=(x, y, 1 - c), device_id_type=MESH)
            cp.start()
            cps.append(cp)
        for cp in cps:
            cp.wait()

    return pl.pallas_call(
        body, in_specs=[ANY] * n, out_specs=[ANY] * n,
        out_shape=[jax.ShapeDtypeStruct((t.shape[0], t.shape[1] // 2, t.shape[2]), t.dtype) for t in tensors],
        scratch_shapes=[pltpu.SemaphoreType.DMA((n,)), pltpu.SemaphoreType.DMA((n,))], name=name,
    )(*tensors)


def _add_half(g, p, core, *, name, tr):
    _, h, c = p.shape
    nt = h // tr

    def body(core_ref, g_ref, p_ref, o_ref):
        o_ref[...] = (g_ref[...].astype(F32) + p_ref[...].astype(F32)).astype(o_ref.dtype)

    return pl.pallas_call(
        body,
        grid_spec=pltpu.PrefetchScalarGridSpec(
            num_scalar_prefetch=1, grid=(N_CHIPS, nt),
            in_specs=[pl.BlockSpec((1, tr, c), lambda s, r, core_ref: (s, core_ref[0] * nt + r, 0)),
                      pl.BlockSpec((1, tr, c), lambda s, r, core_ref: (s, r, 0))],
            out_specs=pl.BlockSpec((1, tr, c), lambda s, r, core_ref: (s, r, 0))),
        out_shape=jax.ShapeDtypeStruct(p.shape, g.dtype),
        compiler_params=_params(("parallel", "parallel")), name=name,
    )(core, g, p)


def _scatter_chips(tensors, *, name):
    n = len(tensors)

    def body(*refs):
        ins, outs = refs[:n], refs[n:2 * n]
        send, recv, loc = refs[2 * n:]
        x, y, c, chips = _place()
        me = 2 * x + y
        cps = []
        for i in range(n):
            mine = pltpu.make_async_copy(ins[i].at[me], outs[i].at[me], loc.at[i])
            mine.start()
            cps.append(mine)
            for k, (px, py) in enumerate(chips):
                cp = pltpu.make_async_remote_copy(src_ref=ins[i].at[2 * px + py], dst_ref=outs[i].at[me], send_sem=send.at[i * 3 + k],
                                                  recv_sem=recv.at[i * 3 + k], device_id=(px, py, c), device_id_type=MESH)
                cp.start()
                cps.append(cp)
        for cp in cps:
            cp.wait()

    return pl.pallas_call(
        body, in_specs=[ANY] * n, out_specs=[ANY] * n,
        out_shape=[jax.ShapeDtypeStruct(t.shape, t.dtype) for t in tensors],
        scratch_shapes=[pltpu.SemaphoreType.DMA((3 * n,)), pltpu.SemaphoreType.DMA((3 * n,)), pltpu.SemaphoreType.DMA((n,))],
        name=name,
    )(*tensors)


def _sum_chips(r, *, name, tr):
    _, h, c = r.shape

    def body(r_ref, o_ref):
        o_ref[...] = ((r_ref[0].astype(F32) + r_ref[1].astype(F32)) + r_ref[2].astype(F32)) + r_ref[3].astype(F32)

    return pl.pallas_call(
        body, grid=(h // tr,), in_specs=[pl.BlockSpec((N_CHIPS, tr, c), lambda i: (0, i, 0))],
        out_specs=pl.BlockSpec((tr, c), lambda i: (i, 0)), out_shape=jax.ShapeDtypeStruct((h, c), F32),
        compiler_params=_params(("parallel",)), name=name,
    )(r)


def _join_halves(tensors, *, name):
    n = len(tensors)

    def body(*refs):
        ins, outs = refs[:n], refs[n:2 * n]
        send, recv, loc = refs[2 * n:]
        x, y, c, _ = _place()
        cps = []
        for i in range(n):
            h = ins[i].shape[0]
            mine = pltpu.make_async_copy(ins[i], outs[i].at[pl.ds(c * h, h)], loc.at[i])
            mine.start()
            cps.append(mine)
            cp = pltpu.make_async_remote_copy(src_ref=ins[i], dst_ref=outs[i].at[pl.ds(c * h, h)], send_sem=send.at[i],
                                              recv_sem=recv.at[i], device_id=(x, y, 1 - c), device_id_type=MESH)
            cp.start()
            cps.append(cp)
        for cp in cps:
            cp.wait()

    return pl.pallas_call(
        body, in_specs=[ANY] * n, out_specs=[ANY] * n,
        out_shape=[jax.ShapeDtypeStruct((2 * t.shape[0],) + t.shape[1:], t.dtype) for t in tensors],
        scratch_shapes=[pltpu.SemaphoreType.DMA((n,)), pltpu.SemaphoreType.DMA((n,)), pltpu.SemaphoreType.DMA((n,))],
        name=name,
    )(*tensors)


BIG = (
    ("w_qkv_a", (2, 1024, 576)), ("w_o_a", (2, 768, 256)), ("w_q_b", (2, 256, 1024)), ("w_o_b", (2, 256, 1024)),
    ("w_kvf", (1024, 516)), ("w_up", (4, 1024, 1408)), ("w_down", (4, 704, 1024)),
)
BIG_ROWS = [math.prod(s) // PACK_W for _, s in BIG]
PACK_ROWS = 11776
SMALL_W = 1792
SMALL_ROWS = 16


def _pack_rows(parts, rows):
    flat = [p.reshape(-1, PACK_W) for p in parts]
    used = sum(f.shape[-2] for f in flat)
    return jnp.concatenate(flat + [jnp.zeros((rows - used, PACK_W), flat[0].dtype)], axis=0)


def _to_shards(full, name):
    if name in ("w_qkv_a", "w_o_a", "w_up"):
        l, k, nn = full.shape
        return full.reshape(l, k, N_CHIPS, nn // N_CHIPS).transpose(2, 0, 1, 3)
    if name in ("w_q_b", "w_o_b", "w_down"):
        l, k, nn = full.shape
        return full.reshape(l, N_CHIPS, k // N_CHIPS, nn).transpose(1, 0, 2, 3)
    k, nn = full.shape
    return full.reshape(k, N_CHIPS, nn // N_CHIPS).transpose(1, 0, 2)


def _from_shards(sh, name):
    if name in ("w_qkv_a", "w_o_a", "w_up"):
        s, l, k, nn = sh.shape
        return sh.transpose(1, 2, 0, 3).reshape(l, k, s * nn)
    if name in ("w_q_b", "w_o_b", "w_down"):
        s, l, k, nn = sh.shape
        return sh.transpose(1, 0, 2, 3).reshape(l, s * k, nn)
    s, k, nn = sh.shape
    return sh.transpose(1, 0, 2).reshape(k, s * nn)


def _headsum_matrix():
    r = lax.broadcasted_iota(jnp.int32, (GW, GW), 0) // HD
    c = lax.broadcasted_iota(jnp.int32, (GW, GW), 1) // HD
    return jnp.where(r == c, 1.0, 0.0).astype(BF16)


def kernel(x, norm_gains, w_qkv_a, w_o_a, w_q_b, w_o_b, kv_norm, w_kvf, b_f, w_up, conv_w, conv_b, w_down, loss_target, m_norm_gains, m_w_qkv_a, m_w_o_a, m_w_q_b, m_w_o_b, m_kv_norm, m_w_kvf, m_b_f, m_w_up, m_conv_w, m_conv_b, m_w_down, v_norm_gains, v_w_qkv_a, v_w_o_a, v_w_q_b, v_w_o_b, v_kv_norm, v_w_kvf, v_b_f, v_w_up, v_conv_w, v_conv_b, v_w_down):
    xi, yi, ci = lax.axis_index("x"), lax.axis_index("y"), lax.axis_index("c")
    chip = 2 * xi + yi
    core = jnp.reshape(ci, (1,)).astype(jnp.int32)
    shard_in = dict(w_qkv_a=w_qkv_a, w_o_a=w_o_a, w_q_b=w_q_b, w_o_b=w_o_b, w_kvf=w_kvf, w_up=w_up, w_down=w_down)

    packed = _pack_rows([shard_in[nm].astype(BF16) for nm, _ in BIG], PACK_ROWS)
    small = jnp.concatenate([
        jnp.pad(norm_gains.reshape(16, 256), ((0, 0), (0, 1408 - 256))),
        jnp.pad(conv_w.reshape(12, 1408), ((0, 4), (0, 0)))], axis=0)
    g_packed, g_small = _allgather([packed, small], name="gather_weights")
    W = {}
    off = 0
    for (nm, shp), rows in zip(BIG, BIG_ROWS):
        W[nm] = _from_shards(g_packed[:, off:off + rows].reshape((N_CHIPS,) + shp), nm)
        off += rows
    gains = g_small[:, :16, :256].transpose(1, 0, 2).reshape(DEPTH, 4, 1, D)
    cw_full = _interleave(g_small[:, 16:28, :].transpose(1, 0, 2).reshape(DEPTH, 3, 2 * D_FF))
    cb_full = _interleave(conv_b).reshape(DEPTH, 1, 2 * D_FF)

    sq, dh, full_grads, small_flat = _fwd_bwd(x[0], loss_target[0], W, gains, cw_full, cb_full, kv_norm, b_f)
    loss = lax.psum(sq[0, 0] * (0.5 / D), ("x", "y", "c"))

    ws = dict(norm_gains=norm_gains, w_qkv_a=w_qkv_a, w_o_a=w_o_a, w_q_b=w_q_b, w_o_b=w_o_b, kv_norm=kv_norm, w_kvf=w_kvf,
              b_f=b_f, w_up=w_up, conv_w=conv_w, conv_b=conv_b, w_down=w_down)
    ms = dict(norm_gains=m_norm_gains, w_qkv_a=m_w_qkv_a, w_o_a=m_w_o_a, w_q_b=m_w_q_b, w_o_b=m_w_o_b, kv_norm=m_kv_norm,
              w_kvf=m_w_kvf, b_f=m_b_f, w_up=m_w_up, conv_w=m_conv_w, conv_b=m_conv_b, w_down=m_w_down)
    vs = dict(norm_gains=v_norm_gains, w_qkv_a=v_w_qkv_a, w_o_a=v_w_o_a, w_q_b=v_w_q_b, w_o_b=v_w_o_b, kv_norm=v_kv_norm,
              w_kvf=v_w_kvf, b_f=v_b_f, w_up=v_w_up, conv_w=v_conv_w, conv_b=v_conv_b, w_down=v_w_down)
    return _reduce_update(loss, dh[None], full_grads, small_flat, chip, core, ws, ms, vs)


def _fwd_bwd(h, target, W, gains, cw_full, cb_full, kv_norm, b_f):
    w_up_i = _interleave(W["w_up"])
    w_kv = W["w_kvf"][:, :2 * D]
    w_kvf_pad = jnp.pad(W["w_kvf"], ((0, 0), (0, 128 - 16)))
    w_f = w_kvf_pad[:, 2 * D:]
    kvn_g = kv_norm.reshape(1, D)
    bf_pad = jnp.pad(b_f, (0, 128 - 16)).reshape(1, 128)
    tabs = _rope_tables()
    headsum = _headsum_matrix()

    saved = []
    kv = zf = c_col = c_row = kvn = h_kv = None
    for l in range(DEPTH):
        s = {"h": h}
        g = gains[l]
        xn = _rms_fwd(h, g[0], out_dtype=BF16, name="rms_in")
        s["xn"] = xn
        if l < N_A:
            qkv = _matmul(xn, W["w_qkv_a"][l], mode="nn", out_dtype=F32, name="mm_qkv", tn=768)
            q3, k3, v3 = _rope_fwd(qkv, tabs)
            qp, kp, vp = _perm(q3), _perm(k3), _perm(v3)
            o_p, lse_p = _band_fwd(qp, kp, vp)
            o3, lse3 = _unperm(o_p), _unperm(lse_p)
            att = _combine_fwd(o3, lse3)
            s.update(qp=qp, kp=kp, vp=vp, o3=o3, lse3=lse3, lse_p=lse_p, att=att)
            mix = _matmul(att, W["w_o_a"][l], mode="nn", out_dtype=F32, name="mm_oa")
        else:
            j = l - N_A
            if l == N_A:
                h_kv = h
                kvn = _rms_fwd(h, kvn_g, out_dtype=BF16, name="rms_in")
                kv = _matmul(kvn, w_kv, mode="nn", out_dtype=BF16, name="mm_kv")
                zf = _matmul(kvn, w_f, mode="nn", out_dtype=F32, name="mm_f")
                cum = _gates_fwd(zf, bf_pad)[:, :16]
                c_col = cum.reshape(T, 8, 2).transpose(1, 0, 2)
                c_row = cum.T.reshape(8, 2, T)
            q = _matmul(xn, W["w_q_b"][j], mode="nn", out_dtype=BF16, name="mm_qb", alpha=HD ** -0.5)
            o = _fox_fwd(q, kv, c_col, c_row)
            s.update(q=q, o=o)
            mix = _matmul(o, W["w_o_b"][j], mode="nn", out_dtype=F32, name="mm_ob")
        s["mix"] = mix
        h1 = _rms_fwd(mix, g[1], res=h, out_dtype=F32, name="rms_res")
        xn2 = _rms_fwd(h1, g[2], out_dtype=BF16, name="rms_in")
        a = _matmul(xn2, w_up_i[l], mode="nn", out_dtype=F32, name="mm_up")
        u = _convgate_fwd(a, cw_full[l], cb_full[l])
        f = _matmul(u, W["w_down"][l], mode="nn", out_dtype=F32, name="mm_down", tk=256)
        h = _rms_fwd(f, g[3], res=h1, out_dtype=F32, name="rms_res")
        s.update(h1=h1, xn2=xn2, a=a, u=u, f=f)
        saved.append(s)

    dh, sq = _loss_head(h, target)

    gw = {nm: [None] * shp[0] for nm, shp in BIG if len(shp) == 3}
    d_gains = [[None] * 4 for _ in range(DEPTH)]
    d_cw, d_cb = [None] * DEPTH, [None] * DEPTH
    zeros_td = jnp.zeros((T, D), F32)
    fox_acc = (zeros_td, zeros_td, jnp.zeros((D // 128, T, 128), F32), jnp.zeros((D // 128, 8, T), F32))
    d_kvf = d_kvnorm = d_bf = None
    for l in reversed(range(DEPTH)):
        s = saved[l]
        g = gains[l]
        df, d_gains[l][3] = _rms_bwd(dh, s["f"], g[3], out_dtype=BF16, name="rms_bwd")
        du = _matmul(df, W["w_down"][l], mode="nt", out_dtype=F32, name="mm_down_dx", tn=256)
        gw["w_down"][l] = _matmul(s["u"], df, mode="tn", out_dtype=BF16, name="mm_down_dw", tm=256)
        da, d_cw[l], d_cb[l] = _convgate_bwd(s["a"], du, cw_full[l], cb_full[l])
        dxn2 = _matmul(da, w_up_i[l], mode="nt", out_dtype=F32, name="mm_up_dx")
        gw["w_up"][l] = _matmul(s["xn2"], da, mode="tn", out_dtype=BF16, name="mm_up_dw")
        dh1, d_gains[l][2] = _rms_bwd(dxn2, s["h1"], g[2], dres=dh, out_dtype=F32, name="rms_bwd_res")
        dmix, d_gains[l][1] = _rms_bwd(dh1, s["mix"], g[1], out_dtype=BF16, name="rms_bwd")
        if l < N_A:
            datt = _matmul(dmix, W["w_o_a"][l], mode="nt", out_dtype=F32, name="mm_oa_dx", tn=768)
            gw["w_o_a"][l] = _matmul(s["att"], dmix, mode="tn", out_dtype=BF16, name="mm_oa_dw", tm=768)
            do3, dlt3 = _combine_bwd(datt, s["o3"], s["lse3"], headsum)
            dqp, dkp, dvp = _band_bwd(s["qp"], s["kp"], s["vp"], _perm(do3), s["lse_p"], _perm(dlt3))
            dqkv = _rope_bwd(_unperm(dqp), _unperm(dkp), _unperm(dvp), tabs)
            dxn = _matmul(dqkv, W["w_qkv_a"][l], mode="nt", out_dtype=F32, name="mm_qkv_dx", tk=768)
            gw["w_qkv_a"][l] = _matmul(s["xn"], dqkv, mode="tn", out_dtype=BF16, name="mm_qkv_dw", tn=768)
        else:
            j = l - N_A
            do = _matmul(dmix, W["w_o_b"][j], mode="nt", out_dtype=BF16, name="mm_ob_dx")
            gw["w_o_b"][j] = _matmul(s["o"], dmix, mode="tn", out_dtype=BF16, name="mm_ob_dw")
            dq, *fox_acc = _fox_bwd(s["q"], kv, do, c_col, c_row, fox_acc)
            dxn = _matmul(dq, W["w_q_b"][j], mode="nt", out_dtype=F32, name="mm_qb_dx")
            gw["w_q_b"][j] = _matmul(s["xn"], dq, mode="tn", out_dtype=BF16, name="mm_qb_dw")
        dh, d_gains[l][0] = _rms_bwd(dxn, s["h"], g[0], dres=dh1, out_dtype=F32, name="rms_bwd_res")
        if l == N_A:
            dk, dv, dcq, dck = fox_acc
            dc16 = dcq[:, :, :2].transpose(1, 0, 2).reshape(T, 16) - dck[:, :2, :].reshape(16, T).T
            dzf, d_bf = _gates_bwd(jnp.pad(dc16, ((0, 0), (0, 128 - 16))), zf, bf_pad)
            dkvf = jnp.concatenate([dk.astype(BF16), dv.astype(BF16), dzf], axis=1)
            d_kvf = _matmul(kvn, dkvf, mode="tn", out_dtype=BF16, name="mm_kvf_dw", tn=128)[:, :2 * D + 16]
            dkvn = _matmul(dkvf, w_kvf_pad, mode="nt", out_dtype=F32, name="mm_kvf_dx", tk=128)
            dh, d_kvnorm = _rms_bwd(dkvn, h_kv, kvn_g, dres=dh, out_dtype=F32, name="rms_bwd_res")
    full_grads = dict(w_qkv_a=jnp.stack(gw["w_qkv_a"]), w_o_a=jnp.stack(gw["w_o_a"]), w_q_b=jnp.stack(gw["w_q_b"]),
                      w_o_b=jnp.stack(gw["w_o_b"]), w_kvf=d_kvf, w_up=_deinterleave(jnp.stack(gw["w_up"])),
                      w_down=jnp.stack(gw["w_down"]))
    small_flat = jnp.concatenate([
        jnp.stack([jnp.stack(r) for r in d_gains]).reshape(-1),
        _deinterleave(jnp.stack(d_cw)).reshape(-1),
        _deinterleave(jnp.stack(d_cb)).reshape(-1),
        d_kvnorm.reshape(-1), d_bf[0, :16]])
    return sq, dh, full_grads, small_flat


def _reduce_update(loss, grad_x, full_grads, small_flat, chip, core, ws, ms, vs):
    parts = [_to_shards(full_grads[nm], nm).reshape(N_CHIPS, -1, PACK_W) for nm, _ in BIG]
    used = sum(p.shape[1] for p in parts)
    gp = jnp.concatenate(parts + [jnp.zeros((N_CHIPS, PACK_ROWS - used, PACK_W), BF16)], axis=1)
    n_small = small_flat.shape[0]
    gs = jnp.pad(small_flat, (0, N_CHIPS * SMALL_ROWS * SMALL_W - n_small)).reshape(N_CHIPS, SMALL_ROWS, SMALL_W)

    sib_p, sib_s = _swap_halves([gp, gs], name="reduce_pair_swap")
    hp = _add_half(gp, sib_p, core, name="reduce_pair_add", tr=256)
    hs = _add_half(gs, sib_s, core, name="reduce_pair_add_small", tr=SMALL_ROWS // 2)
    rp, rs = _scatter_chips([hp, hs], name="reduce_chip_scatter")
    sp = _sum_chips(rp, name="reduce_chip_sum", tr=256)
    ss = _sum_chips(rs, name="reduce_chip_sum_small", tr=SMALL_ROWS // 2)
    red_p, red_s = _join_halves([sp, ss], name="reduce_pair_join")
    (all_s,) = _allgather([red_s], name="gather_small_grads")
    sflat = all_s.reshape(-1)

    grads = {}
    off = 0
    for (nm, shp), rows in zip(BIG, BIG_ROWS):
        grads[nm] = red_p[off:off + rows].reshape(shp)
        off += rows
    o = 0
    g_gains_full = sflat[o:o + 16 * D].reshape(DEPTH, 4, D); o += 16 * D
    g_cw_full = sflat[o:o + 12 * 2 * D_FF].reshape(DEPTH, 3, 2 * D_FF); o += 12 * 2 * D_FF
    grads["conv_b"] = sflat[o:o + 4 * 2 * D_FF].reshape(DEPTH, 2 * D_FF); o += 4 * 2 * D_FF
    grads["kv_norm"] = sflat[o:o + D]; o += D
    grads["b_f"] = sflat[o:o + 16]
    grads["norm_gains"] = lax.dynamic_slice_in_dim(g_gains_full, chip * 256, 256, axis=2)
    grads["conv_w"] = lax.dynamic_slice_in_dim(g_cw_full, chip * 1408, 1408, axis=2)

    names = ["norm_gains", "w_qkv_a", "w_o_a", "w_q_b", "w_o_b", "kv_norm", "w_kvf", "b_f", "w_up", "conv_w", "conv_b", "w_down"]
    deltas, new_m, new_v = {}, {}, {}
    for nm in names:
        shp = ws[nm].shape
        two = (math.prod(shp[:-1]), shp[-1]) if len(shp) > 1 else (1, shp[0])
        d, m2, v2 = _adamw(ws[nm].reshape(two), ms[nm].reshape(two), vs[nm].reshape(two), grads[nm].reshape(two),
                           name="adamw_" + nm)
        deltas[nm], new_m[nm], new_v[nm] = d.reshape(shp), m2.reshape(shp), v2.reshape(shp)

    return (loss, grad_x, *[grads[nm] for nm in names], *[deltas[nm] for nm in names],
            *[new_m[nm] for nm in names], *[new_v[nm] for nm in names])
```

```python
import math

import jax
import jax.numpy as jnp
from jax import lax
from jax.experimental import pallas as pl
from jax.experimental.pallas import tpu as pltpu

F32 = jnp.float32
BF16 = jnp.bfloat16
MESH = pl.DeviceIdType.MESH
ANY = pl.BlockSpec(memory_space=pl.ANY)

T = 2048
D = 1024
HD = 64
DEPTH = 4
N_A = 2
A_W = 768
GW = 256
DIL = (1, 4, 16)
BLK = 128
D_FF = 2816
ROPE_THETA = 500000.0
EPS = 1e-6
NEG = -1e30
N_CHIPS = 4
FQ = 256
CT = 128
VMEM_BIG = 48 * 1024 * 1024

ADAM_LR, ADAM_B1, ADAM_B2, ADAM_EPS, ADAM_WD, ADAM_STEP = 0.001, 0.9, 0.999, 1e-08, 0.01, 10

NN = (((1,), (0,)), ((), ()))
NT = (((1,), (1,)), ((), ()))
TN = (((0,), (0,)), ((), ()))


def _dot(a, b, dims):
    return lax.dot_general(a, b, dims, preferred_element_type=F32)


def _pick(dim, pref):
    if dim <= pref:
        return dim
    best = None
    for t in range(128, pref + 1, 128):
        if dim % t == 0:
            best = t
    assert best is not None, (dim, pref)
    return best


def _params(sem=None, vmem=None):
    kw = {}
    if sem is not None:
        kw["dimension_semantics"] = sem
    if vmem is not None:
        kw["vmem_limit_bytes"] = vmem
    return pltpu.CompilerParams(**kw)


def _matmul(a, b, *, mode, out_dtype, name, mnk=None, alpha=None, tm=1024, tn=512, tk=512,
            a_map=None, b_map=None, acc_init=None, out_slab=None, out_slabs=None, out_buf=None):
    if mnk is not None:
        M, N, K = mnk
    elif mode == "nn":
        (M, K), (_, N) = a.shape, b.shape
    elif mode == "nt":
        (M, K), (N, _) = a.shape, b.shape
    else:
        (K, M), (_, N) = a.shape, b.shape
    tm, tn, tk = _pick(M, tm), _pick(N, tn), _pick(K, tk)
    nk = K // tk
    dims = {"nn": NN, "nt": NT, "tn": TN}[mode]
    n_in = 2 + (acc_init is not None) + (out_buf is not None)

    def body(*refs):
        a_ref, b_ref = refs[0], refs[1]
        o_ref, acc_ref = refs[n_in], refs[n_in + 1]
        k = pl.program_id(2)

        @pl.when(k == 0)
        def _():
            if acc_init is None:
                acc_ref[...] = jnp.zeros_like(acc_ref)
            else:
                acc_ref[...] = refs[2][...]

        acc_ref[...] += _dot(a_ref[...], b_ref[...], dims)

        @pl.when(k == nk - 1)
        def _():
            r = acc_ref[...]
            if alpha is not None:
                r = r * alpha
            o_ref[...] = r.astype(out_dtype)

    a_blk = (tk, tm) if mode == "tn" else (tm, tk)
    b_blk = (tn, tk) if mode == "nt" else (tk, tn)
    if a_map is not None:
        a_spec = pl.BlockSpec((None,) + a_blk, a_map(tm, tn, tk))
    elif mode == "tn":
        a_spec = pl.BlockSpec(a_blk, lambda i, j, k: (k, i))
    else:
        a_spec = pl.BlockSpec(a_blk, lambda i, j, k: (i, k))
    if b_map is not None:
        b_spec = pl.BlockSpec((None,) + b_blk, b_map(tm, tn, tk))
    elif mode == "nt":
        b_spec = pl.BlockSpec(b_blk, lambda i, j, k: (j, k))
    else:
        b_spec = pl.BlockSpec(b_blk, lambda i, j, k: (k, j))
    ins, specs, alias = [a, b], [a_spec, b_spec], {}
    if acc_init is not None:
        ins.append(acc_init)
        specs.append(pl.BlockSpec((tm, tn), lambda i, j, k: (i, j)))
    if out_buf is not None:
        alias = {len(ins): 0}
        ins.append(out_buf)
        specs.append(ANY)
    if out_slab is None:
        o_spec = pl.BlockSpec((tm, tn), lambda i, j, k: (i, j))
        o_shape = jax.ShapeDtypeStruct((M, N), out_dtype)
    else:
        o_spec = pl.BlockSpec((None, tm, tn), lambda i, j, k: (out_slab, i, j))
        o_shape = jax.ShapeDtypeStruct((out_slabs, M, N), out_dtype)
    return pl.pallas_call(
        body,
        grid=(M // tm, N // tn, nk),
        in_specs=specs,
        out_specs=o_spec,
        out_shape=o_shape,
        scratch_shapes=[pltpu.VMEM((tm, tn), F32)],
        input_output_aliases=alias,
        compiler_params=_params(("parallel", "parallel", "arbitrary"), VMEM_BIG),
        name=name,
    )(*ins)


def _slab(l, mode):
    if mode == "nt":
        return lambda tm, tn, tk: (lambda i, j, k: (l, j, k))
    return lambda tm, tn, tk: (lambda i, j, k: (l, k, j))


def _rms_fwd(x, g, *, out_dtype, name, res=None, tr=256):
    n, d = x.shape

    def body(*refs):
        x_ref, g_ref = refs[0], refs[1]
        o_ref = refs[-1]
        xv = x_ref[...].astype(F32)
        y = xv * lax.rsqrt(jnp.mean(xv * xv, axis=-1, keepdims=True) + EPS) * g_ref[...]
        if res is not None:
            y = y + refs[2][...]
        o_ref[...] = y.astype(out_dtype)

    row = pl.BlockSpec((tr, d), lambda i: (i, 0))
    vec = pl.BlockSpec((1, d), lambda i: (0, 0))
    ins = [x, g] + ([] if res is None else [res])
    specs = [row, vec] + ([] if res is None else [row])
    return pl.pallas_call(
        body, grid=(n // tr,), in_specs=specs, out_specs=row,
        out_shape=jax.ShapeDtypeStruct((n, d), out_dtype),
        compiler_params=_params(("parallel",)), name=name,
    )(*ins)


def _rms_bwd(dy, x, g, *, out_dtype, name, dres=None, tr=256):
    n, d = x.shape

    def body(*refs):
        dy_ref, x_ref, g_ref = refs[0], refs[1], refs[2]
        dx_ref, dg_ref = refs[-2], refs[-1]
        xv = x_ref[...].astype(F32)
        dyv = dy_ref[...].astype(F32)
        rstd = lax.rsqrt(jnp.mean(xv * xv, axis=-1, keepdims=True) + EPS)
        xhat = xv * rstd
        dxh = dyv * g_ref[...]
        dx = rstd * (dxh - xhat * jnp.mean(dxh * xhat, axis=-1, keepdims=True))
        if dres is not None:
            dx = dx + refs[3][...]
        dx_ref[...] = dx.astype(out_dtype)

        @pl.when(pl.program_id(0) == 0)
        def _():
            dg_ref[...] = jnp.zeros_like(dg_ref)

        dg_ref[...] += jnp.sum(dyv * xhat, axis=0, keepdims=True)

    row = pl.BlockSpec((tr, d), lambda i: (i, 0))
    vec = pl.BlockSpec((1, d), lambda i: (0, 0))
    ins = [dy, x, g] + ([] if dres is None else [dres])
    specs = [row, row, vec] + ([] if dres is None else [row])
    return pl.pallas_call(
        body, grid=(n // tr,), in_specs=specs, out_specs=[row, vec],
        out_shape=[jax.ShapeDtypeStruct((n, d), out_dtype), jax.ShapeDtypeStruct((1, d), F32)],
        compiler_params=_params(("arbitrary",)), name=name,
    )(*ins)


def _loss_head(h, target, *, tr=256):
    n, d = h.shape

    def body(h_ref, t_ref, dh_ref, s_ref):
        err = h_ref[...] - t_ref[...]
        dh_ref[...] = err * (1.0 / d)

        @pl.when(pl.program_id(0) == 0)
        def _():
            s_ref[...] = jnp.zeros_like(s_ref)

        s_ref[...] += jnp.sum(err * err)

    row = pl.BlockSpec((tr, d), lambda i: (i, 0))
    acc = pl.BlockSpec((8, 128), lambda i: (0, 0))
    return pl.pallas_call(
        body, grid=(n // tr,), in_specs=[row, row], out_specs=[row, acc],
        out_shape=[jax.ShapeDtypeStruct((n, d), F32), jax.ShapeDtypeStruct((8, 128), F32)],
        compiler_params=_params(("arbitrary",)), name="loss_head",
    )(h, target)


def _rope_tables():
    pos = jnp.arange(T, dtype=F32)
    inv = ROPE_THETA ** (-jnp.arange(0, 16, 2, dtype=F32) / 16)
    ang = pos[:, None] * inv[None, :]
    cos, sin = jnp.cos(ang), jnp.sin(ang)
    one = jnp.ones((T, HD - 16), F32)
    zero8 = jnp.zeros((T, 8), F32)
    zero = jnp.zeros((T, HD - 16), F32)
    c = jnp.concatenate([cos, cos, one], axis=1)
    s1 = jnp.concatenate([zero8, sin, zero], axis=1)
    s2 = jnp.concatenate([-sin, zero8, zero], axis=1)
    return tuple(jnp.concatenate([t, t], axis=1) for t in (c, s1, s2))


def _rope_fwd(qkv, tabs, *, tr=256):
    def body(x_ref, c_ref, s1_ref, s2_ref, q_ref, k_ref, v_ref):
        c, s1, s2 = c_ref[...], s1_ref[...], s2_ref[...]
        for which, o_ref, scale in ((0, q_ref, HD ** -0.5), (1, k_ref, None)):
            for j in range(A_W // 128):
                x = x_ref[:, which * A_W + j * 128: which * A_W + (j + 1) * 128]
                y = x * c + pltpu.roll(x, 8, 1) * s1 + pltpu.roll(x, 120, 1) * s2
                if scale is not None:
                    y = y * scale
                o_ref[j // 2, :, (j % 2) * 128:(j % 2 + 1) * 128] = y.astype(BF16)
        for j in range(A_W // 128):
            v_ref[j // 2, :, (j % 2) * 128:(j % 2 + 1) * 128] = x_ref[:, 2 * A_W + j * 128: 2 * A_W + (j + 1) * 128].astype(BF16)

    tab = pl.BlockSpec((tr, 128), lambda i: (i, 0))
    out = pl.BlockSpec((3, tr, GW), lambda i: (0, i, 0))
    shp = jax.ShapeDtypeStruct((3, T, GW), BF16)
    return pl.pallas_call(
        body, grid=(T // tr,), in_specs=[pl.BlockSpec((tr, 3 * A_W), lambda i: (i, 0)), tab, tab, tab],
        out_specs=[out, out, out], out_shape=[shp, shp, shp],
        compiler_params=_params(("parallel",)), name="rope_fwd",
    )(qkv, *tabs)


def _rope_bwd(dq, dk, dv, tabs, *, tr=256):
    def body(dq_ref, dk_ref, dv_ref, c_ref, s1_ref, s2_ref, o_ref):
        c, s1, s2 = c_ref[...], s1_ref[...], s2_ref[...]
        for which, i_ref, scale in ((0, dq_ref, HD ** -0.5), (1, dk_ref, None)):
            for j in range(A_W // 128):
                g = i_ref[j // 2, :, (j % 2) * 128:(j % 2 + 1) * 128]
                y = g * c + pltpu.roll(g * s1, 120, 1) + pltpu.roll(g * s2, 8, 1)
                if scale is not None:
                    y = y * scale
                o_ref[:, which * A_W + j * 128: which * A_W + (j + 1) * 128] = y.astype(BF16)
        for j in range(A_W // 128):
            o_ref[:, 2 * A_W + j * 128: 2 * A_W + (j + 1) * 128] = dv_ref[j // 2, :, (j % 2) * 128:(j % 2 + 1) * 128].astype(BF16)

    tab = pl.BlockSpec((tr, 128), lambda i: (i, 0))
    cot = pl.BlockSpec((3, tr, GW), lambda i: (0, i, 0))
    return pl.pallas_call(
        body, grid=(T // tr,), in_specs=[cot, cot, cot, tab, tab, tab],
        out_specs=pl.BlockSpec((tr, 3 * A_W), lambda i: (i, 0)),
        out_shape=jax.ShapeDtypeStruct((T, 3 * A_W), BF16),
        compiler_params=_params(("parallel",)), name="rope_bwd",
    )(dq, dk, dv, *tabs)


def _perm(x3):
    out = [x3[0]]
    for g in (1, 2):
        r = DIL[g]
        out.append(x3[g].reshape(T // r, r, GW).transpose(1, 0, 2).reshape(T, GW))
    return jnp.stack(out)


def _unperm(x3):
    out = [x3[0]]
    for g in (1, 2):
        r = DIL[g]
        out.append(x3[g].reshape(r, T // r, GW).transpose(1, 0, 2).reshape(T, GW))
    return jnp.stack(out)


def _head_mask(x, lane_lo):
    lane = lax.broadcasted_iota(jnp.int32, x.shape, 1)
    keep = (lane < HD) if lane_lo else (lane >= HD)
    return jnp.where(keep, x.astype(F32), 0.0).astype(BF16)


def _band_scalars():
    g, b = pl.program_id(0), pl.program_id(1)
    nbs = lax.shift_right_logical(jnp.int32(T // BLK), 2 * g)
    has_prev = jnp.where((b & (nbs - 1)) != 0, 1, 0)
    next_ok = jnp.where(((b + 1) & (nbs - 1)) != 0, 1, 0)
    return has_prev, next_ok


def _band_mask_q(has_prev):
    row = lax.broadcasted_iota(jnp.int32, (BLK, 2 * BLK), 0)
    col = lax.broadcasted_iota(jnp.int32, (BLK, 2 * BLK), 1)
    return ((col < BLK) & (col >= row) & (has_prev == 1)) | ((col >= BLK) & (col - BLK <= row))


def _band_mask_k(next_ok):
    row = lax.broadcasted_iota(jnp.int32, (2 * BLK, BLK), 0)
    col = lax.broadcasted_iota(jnp.int32, (2 * BLK, BLK), 1)
    return ((row < BLK) & (col <= row)) | ((row >= BLK) & (col >= row - BLK) & (next_ok == 1))


def _band_fwd(q, k, v):
    nb = T // BLK

    def body(q_ref, kc_ref, kp_ref, vc_ref, vp_ref, o_ref, l_ref):
        has_prev, _ = _band_scalars()
        mask = _band_mask_q(has_prev)
        lane = lax.broadcasted_iota(jnp.int32, (BLK, 128), 1)
        for p in range(2):
            sl = slice(128 * p, 128 * (p + 1))
            qp = q_ref[0, :, sl]
            kcat = jnp.concatenate([kp_ref[0, :, sl], kc_ref[0, :, sl]], axis=0)
            vcat = jnp.concatenate([vp_ref[0, :, sl], vc_ref[0, :, sl]], axis=0)
            o_acc = jnp.zeros((BLK, 128), F32)
            lse = jnp.zeros((BLK, 128), F32)
            for e in range(2):
                s = _dot(_head_mask(qp, e == 0), kcat, NT)
                s = jnp.where(mask, s, NEG)
                m = jnp.max(s, axis=-1, keepdims=True)
                pr = jnp.exp(s - m)
                l = jnp.sum(pr, axis=-1, keepdims=True)
                o_acc = o_acc + _dot(pr.astype(BF16), _head_mask(vcat, e == 0), NN) / l
                lse = jnp.where((lane < HD) if e == 0 else (lane >= HD), m + jnp.log(l), lse)
            o_ref[0, :, sl] = o_acc
            l_ref[0, :, sl] = lse

    cur = pl.BlockSpec((1, BLK, GW), lambda g, b: (g, b, 0))
    prev = pl.BlockSpec((1, BLK, GW), lambda g, b: (g, jnp.maximum(b - 1, 0), 0))
    shp = jax.ShapeDtypeStruct((3, T, GW), F32)
    return pl.pallas_call(
        body, grid=(3, nb), in_specs=[cur, cur, prev, cur, prev], out_specs=[cur, cur], out_shape=[shp, shp],
        compiler_params=_params(("parallel", "parallel")), name="band_fwd",
    )(q, k, k, v, v)


def _band_bwd(q, k, v, do, lse, dlt):
    nb = T // BLK

    def body(qc_ref, qn_ref, kc_ref, kp_ref, vc_ref, vp_ref, doc_ref, don_ref, lc_ref, ln_ref, dc_ref, dn_ref,
             dq_ref, dk_ref, dv_ref):
        has_prev, next_ok = _band_scalars()
        mask_q = _band_mask_q(has_prev)
        mask_k = _band_mask_k(next_ok)
        for p in range(2):
            sl = slice(128 * p, 128 * (p + 1))
            qc, qn = qc_ref[0, :, sl], qn_ref[0, :, sl]
            doc, don = doc_ref[0, :, sl], don_ref[0, :, sl]
            kc, vc = kc_ref[0, :, sl], vc_ref[0, :, sl]
            kcat = jnp.concatenate([kp_ref[0, :, sl], kc], axis=0)
            vcat = jnp.concatenate([vp_ref[0, :, sl], vc], axis=0)
            qcat = jnp.concatenate([qc, qn], axis=0)
            docat = jnp.concatenate([doc, don], axis=0)
            dq = jnp.zeros((BLK, 128), F32)
            dk = jnp.zeros((BLK, 128), F32)
            dv = jnp.zeros((BLK, 128), F32)
            for e in range(2):
                lo = e == 0
                col = slice(128 * p + HD * e, 128 * p + HD * e + 1)
                lse_c, lse_n = lc_ref[0, :, col], ln_ref[0, :, col]
                dl_c, dl_n = dc_ref[0, :, col], dn_ref[0, :, col]
                s = jnp.where(mask_q, _dot(_head_mask(qc, lo), kcat, NT), NEG)
                pr = jnp.exp(s - lse_c)
                dp = _dot(_head_mask(doc, lo), vcat, NT)
                ds = pr * (dp - dl_c)
                dq = dq + _dot(ds.astype(BF16), _head_mask(kcat, lo), NN)
                qm, dom = _head_mask(qcat, lo), _head_mask(docat, lo)
                s2 = jnp.where(mask_k, _dot(qm, kc, NT), NEG)
                p2 = jnp.exp(s2 - jnp.concatenate([lse_c, lse_n], axis=0))
                dv = dv + _dot(p2.astype(BF16), dom, TN)
                dp2 = _dot(dom, vc, NT)
                ds2 = p2 * (dp2 - jnp.concatenate([dl_c, dl_n], axis=0))
                dk = dk + _dot(ds2.astype(BF16), qm, TN)
            dq_ref[0, :, sl] = dq
            dk_ref[0, :, sl] = dk
            dv_ref[0, :, sl] = dv

    cur = pl.BlockSpec((1, BLK, GW), lambda g, b: (g, b, 0))
    prev = pl.BlockSpec((1, BLK, GW), lambda g, b: (g, jnp.maximum(b - 1, 0), 0))
    nxt = pl.BlockSpec((1, BLK, GW), lambda g, b: (g, jnp.minimum(b + 1, nb - 1), 0))
    shp = jax.ShapeDtypeStruct((3, T, GW), F32)
    return pl.pallas_call(
        body, grid=(3, nb),
        in_specs=[cur, nxt, cur, prev, cur, prev, cur, nxt, cur, nxt, cur, nxt],
        out_specs=[cur, cur, cur], out_shape=[shp, shp, shp],
        compiler_params=_params(("parallel", "parallel")), name="band_bwd",
    )(q, q, k, k, v, v, do, do, lse, lse, dlt, dlt)


def _split3(x):
    hi = x.astype(BF16)
    r = x - hi.astype(F32)
    mid = r.astype(BF16)
    lo = (r - mid.astype(F32)).astype(BF16)
    return hi, mid, lo


def _dot3(x, m, dims=NN):
    hi, mid, lo = _split3(x)
    return _dot(hi, m, dims) + _dot(mid, m, dims) + _dot(lo, m, dims)


def _combine_weights(l_ref):
    l0, l1, l2 = l_ref[0], l_ref[1], l_ref[2]
    m = jnp.maximum(jnp.maximum(l0, l1), l2)
    e = [jnp.exp(l0 - m), jnp.exp(l1 - m), jnp.exp(l2 - m)]
    inv = 1.0 / (e[0] + e[1] + e[2])
    return [ei * inv for ei in e]


def _combine_fwd(o, lse, *, tr=256):
    def body(o_ref, l_ref, out_ref):
        alpha = _combine_weights(l_ref)
        for g in range(3):
            out_ref[:, g * GW:(g + 1) * GW] = (o_ref[g] * alpha[g]).astype(BF16)

    blk = pl.BlockSpec((3, tr, GW), lambda i: (0, i, 0))
    return pl.pallas_call(
        body, grid=(T // tr,), in_specs=[blk, blk], out_specs=pl.BlockSpec((tr, A_W), lambda i: (i, 0)),
        out_shape=jax.ShapeDtypeStruct((T, A_W), BF16), compiler_params=_params(("parallel",)), name="combine_fwd",
    )(o, lse)


def _combine_bwd(datt, o, lse, headsum, *, tr=256):
    def body(d_ref, o_ref, l_ref, hs_ref, do_ref, dl_ref):
        alpha = _combine_weights(l_ref)
        hs = hs_ref[...]
        total = jnp.zeros((tr, GW), F32)
        for g in range(3):
            dg = d_ref[:, g * GW:(g + 1) * GW]
            do_ref[g] = (dg * alpha[g]).astype(BF16)
            total = total + alpha[g] * _dot3(dg * o_ref[g], hs)
        for g in range(3):
            dl_ref[g] = alpha[g] * total

    blk = pl.BlockSpec((3, tr, GW), lambda i: (0, i, 0))
    return pl.pallas_call(
        body, grid=(T // tr,),
        in_specs=[pl.BlockSpec((tr, A_W), lambda i: (i, 0)), blk, blk, pl.BlockSpec((GW, GW), lambda i: (0, 0))],
        out_specs=[blk, blk],
        out_shape=[jax.ShapeDtypeStruct((3, T, GW), BF16), jax.ShapeDtypeStruct((3, T, GW), F32)],
        compiler_params=_params(("parallel",)), name="combine_bwd",
    )(datt, o, lse, headsum)


def _fox_scores(qm, k_ref, cq, ck_ref, e, i, n):
    s = _dot(qm, k_ref[0:n, :], NT) + (cq - ck_ref[0, e:e + 1, 0:n])
    row = lax.broadcasted_iota(jnp.int32, (FQ, n), 0)
    col = lax.broadcasted_iota(jnp.int32, (FQ, n), 1)
    s = jnp.where(col <= row + i * FQ, s, NEG)
    m = jnp.max(s, axis=-1, keepdims=True)
    pr = jnp.exp(s - m)
    return pr, jnp.sum(pr, axis=-1, keepdims=True)


def _fox_fwd(q, kv, c_col, c_row):
    def body(q_ref, k_ref, v_ref, cc_ref, cr_ref, o_ref, vm_ref):
        for e in range(2):
            vm_ref[e] = _head_mask(v_ref[...], e == 0)
        for i in range(T // FQ):
            n = (i + 1) * FQ
            rows = slice(i * FQ, n)
            acc = jnp.zeros((FQ, 128), F32)
            for e in range(2):
                qm = _head_mask(q_ref[rows, :], e == 0)
                pr, l = _fox_scores(qm, k_ref, cc_ref[0, rows, e:e + 1], cr_ref, e, i, n)
                acc = acc + _dot(pr.astype(BF16), vm_ref[e, 0:n, :], NN) / l
            o_ref[rows, :] = acc.astype(BF16)

    pair = pl.BlockSpec((T, 128), lambda p: (0, p))
    return pl.pallas_call(
        body, grid=(D // 128,),
        in_specs=[pair, pair, pl.BlockSpec((T, 128), lambda p: (0, D // 128 + p)),
                  pl.BlockSpec((1, T, 2), lambda p: (p, 0, 0)), pl.BlockSpec((1, 2, T), lambda p: (p, 0, 0))],
        out_specs=pair, out_shape=jax.ShapeDtypeStruct((T, D), BF16),
        scratch_shapes=[pltpu.VMEM((2, T, 128), BF16)],
        compiler_params=_params(("parallel",), VMEM_BIG), name="fox_fwd",
    )(q, kv, kv, c_col, c_row)


def _fox_bwd(q, kv, do, c_col, c_row, init):
    def body(q_ref, k_ref, v_ref, do_ref, cc_ref, cr_ref, ik_ref, iv_ref, iq_ref, ic_ref,
             dq_ref, dk_ref, dv_ref, dcq_ref, dck_ref, km_ref):
        dk_ref[...] = ik_ref[...]
        dv_ref[...] = iv_ref[...]
        dcq_ref[...] = iq_ref[...]
        dck_ref[...] = ic_ref[...]
        for e in range(2):
            km_ref[e] = _head_mask(k_ref[...], e == 0)
        for i in range(T // FQ):
            n = (i + 1) * FQ
            rows = slice(i * FQ, n)
            dq = jnp.zeros((FQ, 128), F32)
            for e in range(2):
                qm = _head_mask(q_ref[rows, :], e == 0)
                dom = _head_mask(do_ref[rows, :], e == 0)
                pr, l = _fox_scores(qm, k_ref, cc_ref[0, rows, e:e + 1], cr_ref, e, i, n)
                pr = pr / l
                dp = _dot(dom, v_ref[0:n, :], NT)
                ds = pr * (dp - jnp.sum(pr * dp, axis=-1, keepdims=True))
                dsb = ds.astype(BF16)
                dq = dq + _dot(dsb, km_ref[e, 0:n, :], NN)
                dk_ref[0:n, :] += _dot(dsb, qm, TN)
                dv_ref[0:n, :] += _dot(pr.astype(BF16), dom, TN)
                dcq_ref[0, rows, e:e + 1] += jnp.sum(ds, axis=-1, keepdims=True)
                dck_ref[0, e:e + 1, 0:n] += jnp.sum(ds, axis=0, keepdims=True)
            dq_ref[rows, :] = (dq * HD ** -0.5).astype(BF16)

    pair = pl.BlockSpec((T, 128), lambda p: (0, p))
    cq = pl.BlockSpec((1, T, 128), lambda p: (p, 0, 0))
    ck = pl.BlockSpec((1, 8, T), lambda p: (p, 0, 0))
    return pl.pallas_call(
        body, grid=(D // 128,),
        in_specs=[pair, pair, pl.BlockSpec((T, 128), lambda p: (0, D // 128 + p)), pair,
                  pl.BlockSpec((1, T, 2), lambda p: (p, 0, 0)), pl.BlockSpec((1, 2, T), lambda p: (p, 0, 0)),
                  pair, pair, cq, ck],
        out_specs=[pair, pair, pair, cq, ck],
        out_shape=[jax.ShapeDtypeStruct((T, D), BF16), jax.ShapeDtypeStruct((T, D), F32), jax.ShapeDtypeStruct((T, D), F32),
                   jax.ShapeDtypeStruct((D // 128, T, 128), F32), jax.ShapeDtypeStruct((D // 128, 8, T), F32)],
        scratch_shapes=[pltpu.VMEM((2, T, 128), BF16)],
        compiler_params=_params(("parallel",), VMEM_BIG), name="fox_bwd",
    )(q, kv, kv, do, c_col, c_row, *init)


def _tri(lower):
    r = lax.broadcasted_iota(jnp.int32, (BLK, BLK), 0)
    c = lax.broadcasted_iota(jnp.int32, (BLK, BLK), 1)
    return jnp.where((c <= r) if lower else (c >= r), 1.0, 0.0).astype(BF16)


def _gates_fwd(z, b):
    def body(z_ref, b_ref, c_ref):
        tri = _tri(True)
        carry = jnp.zeros((1, 128), F32)
        for i in range(T // BLK):
            rows = slice(i * BLK, (i + 1) * BLK)
            x = z_ref[rows, :] + b_ref[...]
            logf = jnp.minimum(x, 0.0) - jnp.log(1.0 + jnp.exp(-jnp.abs(x)))
            hi, mid, lo = _split3(logf)
            y = _dot(tri, hi, NN) + _dot(tri, mid, NN) + _dot(tri, lo, NN) + carry
            c_ref[rows, :] = y
            carry = y[BLK - 1:BLK, :]

    return pl.pallas_call(body, out_shape=jax.ShapeDtypeStruct((T, 128), F32), name="gates_fwd")(z, b)


def _gates_bwd(dc, z, b):
    def body(dc_ref, z_ref, b_ref, dz_ref, db_ref):
        tri = _tri(False)
        carry = jnp.zeros((1, 128), F32)
        db = jnp.zeros((1, 128), F32)
        for i in reversed(range(T // BLK)):
            rows = slice(i * BLK, (i + 1) * BLK)
            hi, mid, lo = _split3(dc_ref[rows, :])
            dlogf = _dot(tri, hi, NN) + _dot(tri, mid, NN) + _dot(tri, lo, NN) + carry
            carry = dlogf[0:1, :]
            x = z_ref[rows, :] + b_ref[...]
            dz = dlogf / (1.0 + jnp.exp(x))
            dz_ref[rows, :] = dz.astype(BF16)
            db = db + jnp.sum(dz, axis=0, keepdims=True)
        db_ref[...] = db

    return pl.pallas_call(
        body, out_shape=[jax.ShapeDtypeStruct((T, 128), BF16), jax.ShapeDtypeStruct((1, 128), F32)], name="gates_bwd",
    )(dc, z, b)


def _conv_pair(a_refs, cw_refs, cb_refs):
    row = lax.broadcasted_iota(jnp.int32, (T, CT), 0)
    outs = []
    for a_ref, cw_ref, cb_ref in zip(a_refs, cw_refs, cb_refs):
        z = a_ref[...]
        z1 = jnp.where(row >= 1, pltpu.roll(z, 1, 0), 0.0)
        z2 = jnp.where(row >= 2, pltpu.roll(z, 2, 0), 0.0)
        y = cw_ref[2:3, :] * z + cw_ref[1:2, :] * z1 + cw_ref[0:1, :] * z2 + cb_ref[...]
        outs.append((y, z, z1, z2))
    return outs


_GELU_K = math.sqrt(2.0 / math.pi)
N_CT = D_FF // CT


def _conv_specs():
    def at(rows, off):
        return pl.BlockSpec((rows, CT), lambda j: (0, j + off))
    return [at(T, 0), at(T, N_CT), at(3, 0), at(3, N_CT), at(1, 0), at(1, N_CT)]


def _convgate_fwd(a, cw, cb):
    def body(ag_ref, av_ref, wg_ref, wv_ref, bg_ref, bv_ref, u_ref):
        (g, _, _, _), (v, _, _, _) = _conv_pair((ag_ref, av_ref), (wg_ref, wv_ref), (bg_ref, bv_ref))
        th = jnp.tanh(_GELU_K * (g + 0.044715 * g * g * g))
        u_ref[...] = (0.5 * g * (1.0 + th) * v).astype(BF16)

    return pl.pallas_call(
        body, grid=(N_CT,), in_specs=_conv_specs(),
        out_specs=pl.BlockSpec((T, CT), lambda j: (0, j)), out_shape=jax.ShapeDtypeStruct((T, D_FF), BF16),
        compiler_params=_params(("parallel",), VMEM_BIG), name="convgate_fwd",
    )(a, a, cw, cw, cb, cb)


def _convgate_bwd(a, du, cw, cb):
    def body(ag_ref, av_ref, wg_ref, wv_ref, bg_ref, bv_ref, du_ref, da_ref, dcw_ref, dcb_ref):
        (g, gz, gz1, gz2), (v, vz, vz1, vz2) = _conv_pair((ag_ref, av_ref), (wg_ref, wv_ref), (bg_ref, bv_ref))
        du = du_ref[...].astype(F32)
        th = jnp.tanh(_GELU_K * (g + 0.044715 * g * g * g))
        gelu = 0.5 * g * (1.0 + th)
        dgelu = 0.5 * (1.0 + th) + 0.5 * g * (1.0 - th * th) * _GELU_K * (1.0 + 3 * 0.044715 * g * g)
        row = lax.broadcasted_iota(jnp.int32, (T, CT), 0)
        for h, (d, z, z1, z2, w_ref) in enumerate(((du * v * dgelu, gz, gz1, gz2, wg_ref), (du * gelu, vz, vz1, vz2, wv_ref))):
            d1 = jnp.where(row < T - 1, pltpu.roll(d, T - 1, 0), 0.0)
            d2 = jnp.where(row < T - 2, pltpu.roll(d, T - 2, 0), 0.0)
            da_ref[h] = (w_ref[2:3, :] * d + w_ref[1:2, :] * d1 + w_ref[0:1, :] * d2).astype(BF16)
            dcw_ref[h, 0:1, :] = jnp.sum(d * z2, axis=0, keepdims=True)
            dcw_ref[h, 1:2, :] = jnp.sum(d * z1, axis=0, keepdims=True)
            dcw_ref[h, 2:3, :] = jnp.sum(d * z, axis=0, keepdims=True)
            dcb_ref[h] = jnp.sum(d, axis=0, keepdims=True)

    def both(rows):
        return pl.BlockSpec((2, rows, CT), lambda j: (0, 0, j))

    return pl.pallas_call(
        body, grid=(N_CT,),
        in_specs=_conv_specs() + [pl.BlockSpec((T, CT), lambda j: (0, j))],
        out_specs=[both(T), both(3), both(1)],
        out_shape=[jax.ShapeDtypeStruct((2, T, D_FF), BF16), jax.ShapeDtypeStruct((2, 3, D_FF), F32),
                   jax.ShapeDtypeStruct((2, 1, D_FF), F32)],
        compiler_params=_params(("parallel",), VMEM_BIG), name="convgate_bwd",
    )(a, a, cw, cw, cb, cb, du)


def _halves_a(tm, tn, tk):
    per = D_FF // tk
    return lambda i, j, k: (lax.div(k, per), i, lax.rem(k, per))


def _halves_b(tm, tn, tk):
    per = D_FF // tn
    return lambda i, j, k: (lax.div(j, per), k, lax.rem(j, per))


def _adamw(w, m, v, g, *, name):
    r, c = w.shape
    tr = r
    if r * c > 256 * 1024:
        for cand in range(8, r, 8):
            if r % cand == 0 and cand * c <= 256 * 1024:
                tr = cand

    def body(w_ref, m_ref, v_ref, g_ref, d_ref, nm_ref, nv_ref):
        gv = g_ref[...]
        mn = ADAM_B1 * m_ref[...] + (1.0 - ADAM_B1) * gv
        vn = ADAM_B2 * v_ref[...] + (1.0 - ADAM_B2) * (gv * gv)
        m_hat = mn / (1.0 - ADAM_B1 ** ADAM_STEP)
        v_hat = vn / (1.0 - ADAM_B2 ** ADAM_STEP)
        d_ref[...] = -ADAM_LR * (m_hat / (jnp.sqrt(v_hat) + ADAM_EPS) + ADAM_WD * w_ref[...])
        nm_ref[...] = mn
        nv_ref[...] = vn

    blk = pl.BlockSpec((tr, c), lambda i: (i, 0))
    shp = jax.ShapeDtypeStruct((r, c), F32)
    return pl.pallas_call(
        body, grid=(r // tr,), in_specs=[blk] * 4, out_specs=[blk] * 3, out_shape=[shp] * 3,
        compiler_params=_params(("parallel",)), name=name,
    )(w, m, v, g)


def _place():
    x, y, c = lax.axis_index("x"), lax.axis_index("y"), lax.axis_index("c")
    chips = [(1 - x, y), (x, 1 - y), (1 - x, 1 - y)]
    return x, y, c, chips


def _window(ref, kind, s, half=None):
    lead = () if half is None else (half,)
    b, c = ref.shape[-2], ref.shape[-1]
    if kind == "col":
        return ref.at[lead + (slice(None), slice(None), pl.ds(s * (c // N_CHIPS), c // N_CHIPS))]
    if kind == "row":
        return ref.at[lead + (slice(None), pl.ds(s * (b // N_CHIPS), b // N_CHIPS), slice(None))]
    return ref.at[lead + (s,)]


def _window_shape(shape3, kind):
    a, b, c = shape3
    return {"col": (a, b, c // N_CHIPS), "row": (a, b // N_CHIPS, c), "slab": (b, c)}[kind]


def _allgather(tensors, kinds, *, name):
    n = len(tensors)

    def body(*refs):
        bufs = refs[n:2 * n]
        send, recv = refs[2 * n:]
        x, y, c, chips = _place()
        me = 2 * x + y
        sib = (x, y, 1 - c)

        def rcopy(i, k, win, to):
            return pltpu.make_async_remote_copy(src_ref=win, dst_ref=win, send_sem=send.at[i * 6 + k], recv_sem=recv.at[i * 6 + k],
                                                device_id=to, device_id_type=MESH)

        started = []
        for i in range(n):
            for k, (px, py) in enumerate(chips):
                cp = rcopy(i, k, _window(bufs[i], kinds[i], me, c), (px, py, c))
                cp.start()
                started.append(cp)
        for i in range(n):
            for k, (px, py) in enumerate(chips):
                landed = _window(bufs[i], kinds[i], 2 * px + py, c)
                rcopy(i, k, landed, (px, py, c)).wait_recv()
                fw = rcopy(i, 3 + k, landed, sib)
                fw.start()
                started.append(fw)
        for i in range(n):
            for k, (px, py) in enumerate(chips):
                rcopy(i, 3 + k, _window(bufs[i], kinds[i], 2 * px + py, 1 - c), sib).wait_recv()
        for cp in started:
            cp.wait_send()

    return pl.pallas_call(
        body, in_specs=[ANY] * n, out_specs=[ANY] * n,
        out_shape=[jax.ShapeDtypeStruct(t.shape, t.dtype) for t in tensors],
        scratch_shapes=[pltpu.SemaphoreType.DMA((6 * n,)), pltpu.SemaphoreType.DMA((6 * n,))],
        input_output_aliases={i: i for i in range(n)},
        name=name,
    )(*tensors)


def _swap_halves(tensors, *, name):
    n = len(tensors)

    def body(*refs):
        ins, outs = refs[:n], refs[n:2 * n]
        send, recv = refs[2 * n:]
        x, y, c, _ = _place()
        cps = []
        for i in range(n):
            cp = pltpu.make_async_remote_copy(src_ref=ins[i].at[1 - c], dst_ref=outs[i], send_sem=send.at[i],
                                              recv_sem=recv.at[i], device_id=(x, y, 1 - c), device_id_type=MESH)
            cp.start()
            cps.append(cp)
        for cp in cps:
            cp.wait()

    return pl.pallas_call(
        body, in_specs=[ANY] * n, out_specs=[ANY] * n,
        out_shape=[jax.ShapeDtypeStruct(t.shape[1:], t.dtype) for t in tensors],
        scratch_shapes=[pltpu.SemaphoreType.DMA((n,)), pltpu.SemaphoreType.DMA((n,))], name=name,
    )(*tensors)


def _rows_tile(rows, cols, sub):
    best = None
    for t in range(sub, rows + 1, sub):
        if rows % t == 0 and t * cols <= 512 * 1024:
            best = t
    return rows if best is None else best


def _add_half(g, p, where, *, name):
    a, b, c = p.shape
    tr = _rows_tile(b, c, 16)

    def body(w_ref, g_ref, p_ref, o_ref):
        o_ref[...] = (g_ref[...].astype(F32) + p_ref[...].astype(F32)).astype(o_ref.dtype)

    return pl.pallas_call(
        body,
        grid_spec=pltpu.PrefetchScalarGridSpec(
            num_scalar_prefetch=1, grid=(a, b // tr),
            in_specs=[pl.BlockSpec((None, None, tr, c), lambda i, r, w: (w[1], i, r, 0)),
                      pl.BlockSpec((None, tr, c), lambda i, r, w: (i, r, 0))],
            out_specs=pl.BlockSpec((None, tr, c), lambda i, r, w: (i, r, 0))),
        out_shape=jax.ShapeDtypeStruct(p.shape, g.dtype),
        compiler_params=_params(("parallel", "parallel")), name=name,
    )(where, g, p)


def _scatter_chips(tensors, kinds, *, name):
    n = len(tensors)

    def body(*refs):
        ins, outs = refs[:n], refs[n:2 * n]
        send, recv = refs[2 * n:]
        x, y, c, chips = _place()
        cps = []
        for i in range(n):
            for k, (px, py) in enumerate(chips):
                cp = pltpu.make_async_remote_copy(src_ref=_window(ins[i], kinds[i], 2 * px + py), dst_ref=outs[i].at[k],
                                                  send_sem=send.at[i * 3 + k], recv_sem=recv.at[i * 3 + k],
                                                  device_id=(px, py, c), device_id_type=MESH)
                cp.start()
                cps.append(cp)
        for cp in cps:
            cp.wait()

    return pl.pallas_call(
        body, in_specs=[ANY] * n, out_specs=[ANY] * n,
        out_shape=[jax.ShapeDtypeStruct((3,) + _window_shape(t.shape, k), t.dtype) for t, k in zip(tensors, kinds)],
        scratch_shapes=[pltpu.SemaphoreType.DMA((3 * n,)), pltpu.SemaphoreType.DMA((3 * n,))],
        name=name,
    )(*tensors)


def _sum_chips(r, h, kind, where, *, name):
    if kind == "slab":
        r = r.reshape((3, 1) + r.shape[1:])
    _, a, b, c = r.shape
    tr = _rows_tile(b, c, 16)
    if kind == "col":
        h_spec = pl.BlockSpec((None, tr, c), lambda i, j, w: (i, j, w[0]))
    elif kind == "row":
        h_spec = pl.BlockSpec((None, tr, c), lambda i, j, w: (i, w[0] * (b // tr) + j, 0))
    else:
        h_spec = pl.BlockSpec((None, tr, c), lambda i, j, w: (w[0], j, 0))

    def body(w_ref, h_ref, r0_ref, r1_ref, r2_ref, o_ref):
        o_ref[...] = ((h_ref[...].astype(F32) + r0_ref[...].astype(F32)) + r1_ref[...].astype(F32)) + r2_ref[...].astype(F32)

    def slot(k):
        return pl.BlockSpec((None, None, tr, c), lambda i, j, w: (k, i, j, 0))

    return pl.pallas_call(
        body,
        grid_spec=pltpu.PrefetchScalarGridSpec(
            num_scalar_prefetch=1, grid=(a, b // tr),
            in_specs=[h_spec, slot(0), slot(1), slot(2)],
            out_specs=pl.BlockSpec((None, None, tr, c), lambda i, j, w: (w[1], i, j, 0))),
        out_shape=jax.ShapeDtypeStruct((2, a, b, c), F32),
        compiler_params=_params(("parallel", "parallel")), name=name,
    )(where, h, r, r, r)


def _join_halves(tensors, *, name):
    n = len(tensors)

    def body(*refs):
        bufs = refs[n:2 * n]
        send, recv = refs[2 * n:]
        x, y, c, _ = _place()
        cps = []
        for i in range(n):
            cp = pltpu.make_async_remote_copy(src_ref=bufs[i].at[c], dst_ref=bufs[i].at[c], send_sem=send.at[i],
                                              recv_sem=recv.at[i], device_id=(x, y, 1 - c), device_id_type=MESH)
            cp.start()
            cps.append(cp)
        for i in range(n):
            pltpu.make_async_remote_copy(src_ref=bufs[i].at[1 - c], dst_ref=bufs[i].at[1 - c], send_sem=send.at[i],
                                         recv_sem=recv.at[i], device_id=(x, y, 1 - c), device_id_type=MESH).wait_recv()
        for cp in cps:
            cp.wait_send()

    return pl.pallas_call(
        body, in_specs=[ANY] * n, out_specs=[ANY] * n,
        out_shape=[jax.ShapeDtypeStruct(t.shape, t.dtype) for t in tensors],
        scratch_shapes=[pltpu.SemaphoreType.DMA((n,)), pltpu.SemaphoreType.DMA((n,))],
        input_output_aliases={i: i for i in range(n)},
        name=name,
    )(*tensors)


BIG = (
    ("w_qkv_a", "slab", (2, 4, 1024, 576)), ("w_o_a", "col", (2, 1, 768, 1024)), ("w_q_b", "row", (2, 1, 1024, 1024)),
    ("w_o_b", "row", (2, 1, 1024, 1024)), ("w_kvf", "slab", (2, 4, 512, 516)), ("w_up", "col", (2, 2, 1024, 5632)),
    ("w_down", "row", (2, 2, 2816, 1024)),
)
SMALL_W = 1792
SMALL_ROWS = 8


def _own_in_place(shard, kind, full_shape, chip):
    two, a, b, c = full_shape
    buf = jnp.zeros(full_shape, shard.dtype)
    if kind == "col":
        return lax.dynamic_update_slice(buf, shard.reshape(two, a, b, c // N_CHIPS), (0, 0, 0, chip * (c // N_CHIPS)))
    if kind == "row":
        return lax.dynamic_update_slice(buf, shard.reshape(two, a, b // N_CHIPS, c), (0, 0, chip * (b // N_CHIPS), 0))
    return lax.dynamic_update_slice(buf, shard.reshape(two, 1, b, c), (0, chip, 0, 0))


def _headsum_matrix():
    r = lax.broadcasted_iota(jnp.int32, (GW, GW), 0) // HD
    c = lax.broadcasted_iota(jnp.int32, (GW, GW), 1) // HD
    return jnp.where(r == c, 1.0, 0.0).astype(BF16)


def kernel(x, norm_gains, w_qkv_a, w_o_a, w_q_b, w_o_b, kv_norm, w_kvf, b_f, w_up, conv_w, conv_b, w_down, loss_target, m_norm_gains, m_w_qkv_a, m_w_o_a, m_w_q_b, m_w_o_b, m_kv_norm, m_w_kvf, m_b_f, m_w_up, m_conv_w, m_conv_b, m_w_down, v_norm_gains, v_w_qkv_a, v_w_o_a, v_w_q_b, v_w_o_b, v_kv_norm, v_w_kvf, v_b_f, v_w_up, v_conv_w, v_conv_b, v_w_down):
    xi, yi, ci = lax.axis_index("x"), lax.axis_index("y"), lax.axis_index("c")
    chip = 2 * xi + yi
    where = jnp.stack([chip, ci]).astype(jnp.int32)
    ws = dict(norm_gains=norm_gains, w_qkv_a=w_qkv_a, w_o_a=w_o_a, w_q_b=w_q_b, w_o_b=w_o_b, kv_norm=kv_norm, w_kvf=w_kvf,
              b_f=b_f, w_up=w_up, conv_w=conv_w, conv_b=conv_b, w_down=w_down)
    ms = dict(norm_gains=m_norm_gains, w_qkv_a=m_w_qkv_a, w_o_a=m_w_o_a, w_q_b=m_w_q_b, w_o_b=m_w_o_b, kv_norm=m_kv_norm,
              w_kvf=m_w_kvf, b_f=m_b_f, w_up=m_w_up, conv_w=m_conv_w, conv_b=m_conv_b, w_down=m_w_down)
    vs = dict(norm_gains=v_norm_gains, w_qkv_a=v_w_qkv_a, w_o_a=v_w_o_a, w_q_b=v_w_q_b, w_o_b=v_w_o_b, kv_norm=v_kv_norm,
              w_kvf=v_w_kvf, b_f=v_b_f, w_up=v_w_up, conv_w=v_conv_w, conv_b=v_conv_b, w_down=v_w_down)

    small = jnp.concatenate([
        jnp.pad(norm_gains.reshape(16, 256), ((0, 0), (0, 1408 - 256))),
        jnp.pad(conv_w.reshape(12, 1408), ((0, 4), (0, 0)))], axis=0)
    bufs = [_own_in_place(ws[nm].astype(BF16), kind, shp, chip) for nm, kind, shp in BIG]
    bufs.append(_own_in_place(small, "slab", (2, 4, 16, 1408), chip))
    *gathered, g_small = _allgather(bufs, [k for _, k, _ in BIG] + ["slab"], name="gather_weights")
    W = {}
    for (nm, kind, shp), g in zip(BIG, gathered):
        if nm == "w_qkv_a":
            W[nm] = g.transpose(0, 2, 1, 3).reshape(2, D, 3 * A_W)
        elif nm == "w_kvf":
            W[nm] = g.transpose(0, 2, 1, 3).reshape(D, 2 * D + 16)
        else:
            W[nm] = g.reshape((shp[0] * shp[1],) + shp[2:])
    gains = g_small[0, :, :, :256].transpose(1, 0, 2).reshape(DEPTH, 4, 1, D)
    cw_full = g_small[1, :, :12, :].transpose(1, 0, 2).reshape(DEPTH, 3, 2 * D_FF)
    cb_full = conv_b.reshape(DEPTH, 1, 2 * D_FF)

    sq, dh, full_grads, small_flat = _fwd_bwd(x[0], loss_target[0], W, gains, cw_full, cb_full, kv_norm, b_f)
    loss = lax.psum(sq[0, 0] * (0.5 / D), ("x", "y", "c"))
    return _reduce_update(loss, dh[None], full_grads, small_flat, chip, where, ws, ms, vs)


def _fwd_bwd(h, target, W, gains, cw_full, cb_full, kv_norm, b_f):
    w_kv = W["w_kvf"][:, :2 * D]
    w_kvf_pad = jnp.pad(W["w_kvf"], ((0, 0), (0, 128 - 16)))
    w_f = w_kvf_pad[:, 2 * D:]
    kvn_g = kv_norm.reshape(1, D)
    bf_pad = jnp.pad(b_f, (0, 128 - 16)).reshape(1, 128)
    tabs = _rope_tables()
    headsum = _headsum_matrix()

    saved = []
    kv = zf = c_col = c_row = kvn = h_kv = None
    for l in range(DEPTH):
        s = {"h": h}
        g = gains[l]
        xn = _rms_fwd(h, g[0], out_dtype=BF16, name="rms_in")
        s["xn"] = xn
        if l < N_A:
            qkv = _matmul(xn, W["w_qkv_a"], mode="nn", out_dtype=F32, name="mm_qkv", mnk=(T, 3 * A_W, D), tn=768,
                          b_map=_slab(l, "nn"))
            q3, k3, v3 = _rope_fwd(qkv, tabs)
            qp, kp, vp = _perm(q3), _perm(k3), _perm(v3)
            o_p, lse_p = _band_fwd(qp, kp, vp)
            o3, lse3 = _unperm(o_p), _unperm(lse_p)
            att = _combine_fwd(o3, lse3)
            s.update(qp=qp, kp=kp, vp=vp, o3=o3, lse3=lse3, lse_p=lse_p, att=att)
            mix = _matmul(att, W["w_o_a"], mode="nn", out_dtype=F32, name="mm_oa", mnk=(T, D, A_W), b_map=_slab(l, "nn"))
        else:
            j = l - N_A
            if l == N_A:
                h_kv = h
                kvn = _rms_fwd(h, kvn_g, out_dtype=BF16, name="rms_in")
                kv = _matmul(kvn, w_kv, mode="nn", out_dtype=BF16, name="mm_kv")
                zf = _matmul(kvn, w_f, mode="nn", out_dtype=F32, name="mm_f")
                cum = _gates_fwd(zf, bf_pad)[:, :16]
                c_col = cum.reshape(T, 8, 2).transpose(1, 0, 2)
                c_row = cum.T.reshape(8, 2, T)
            q = _matmul(xn, W["w_q_b"], mode="nn", out_dtype=BF16, name="mm_qb", mnk=(T, D, D), alpha=HD ** -0.5,
                        b_map=_slab(j, "nn"))
            o = _fox_fwd(q, kv, c_col, c_row)
            s.update(q=q, o=o)
            mix = _matmul(o, W["w_o_b"], mode="nn", out_dtype=F32, name="mm_ob", mnk=(T, D, D), b_map=_slab(j, "nn"))
        s["mix"] = mix
        h1 = _rms_fwd(mix, g[1], res=h, out_dtype=F32, name="rms_res")
        xn2 = _rms_fwd(h1, g[2], out_dtype=BF16, name="rms_in")
        a = _matmul(xn2, W["w_up"], mode="nn", out_dtype=F32, name="mm_up", mnk=(T, 2 * D_FF, D), b_map=_slab(l, "nn"))
        u = _convgate_fwd(a, cw_full[l], cb_full[l])
        f = _matmul(u, W["w_down"], mode="nn", out_dtype=F32, name="mm_down", mnk=(T, D, D_FF), tk=256, b_map=_slab(l, "nn"))
        h = _rms_fwd(f, g[3], res=h1, out_dtype=F32, name="rms_res")
        s.update(h1=h1, xn2=xn2, a=a, u=u, f=f)
        saved.append(s)

    dh, sq = _loss_head(h, target)

    gw = dict(w_qkv_a=None, w_o_a=None, w_q_b=None, w_o_b=None, w_up=None, w_down=None)
    d_gains = [[None] * 4 for _ in range(DEPTH)]
    d_cw, d_cb = [None] * DEPTH, [None] * DEPTH
    zeros_td = jnp.zeros((T, D), F32)
    fox_acc = (zeros_td, zeros_td, jnp.zeros((D // 128, T, 128), F32), jnp.zeros((D // 128, 8, T), F32))
    d_kvf = d_kvnorm = d_bf = None

    def dw(nm, slab, slabs, a, b, **kw):
        gw[nm] = _matmul(a, b, mode="tn", out_dtype=BF16, name="mm_dw_" + nm, out_slab=slab, out_slabs=slabs, out_buf=gw[nm], **kw)

    for l in reversed(range(DEPTH)):
        s = saved[l]
        g = gains[l]
        df, d_gains[l][3] = _rms_bwd(dh, s["f"], g[3], out_dtype=BF16, name="rms_bwd")
        du = _matmul(df, W["w_down"], mode="nt", out_dtype=F32, name="mm_down_dx", mnk=(T, D_FF, D), tn=256, b_map=_slab(l, "nt"))
        dw("w_down", l, DEPTH, s["u"], df, tm=256)
        da, d_cw[l], d_cb[l] = _convgate_bwd(s["a"], du, cw_full[l], cb_full[l])
        dxn2 = _matmul(da, W["w_up"], mode="nt", out_dtype=F32, name="mm_up_dx", mnk=(T, D, 2 * D_FF), tk=256,
                       a_map=_halves_a, b_map=_slab(l, "nt"))
        dw("w_up", l, DEPTH, s["xn2"], da, mnk=(D, 2 * D_FF, T), tn=256, b_map=_halves_b)
        dh1, d_gains[l][2] = _rms_bwd(dxn2, s["h1"], g[2], dres=dh, out_dtype=F32, name="rms_bwd_res")
        dmix, d_gains[l][1] = _rms_bwd(dh1, s["mix"], g[1], out_dtype=BF16, name="rms_bwd")
        if l < N_A:
            datt = _matmul(dmix, W["w_o_a"], mode="nt", out_dtype=F32, name="mm_oa_dx", mnk=(T, A_W, D), tn=768, b_map=_slab(l, "nt"))
            dw("w_o_a", l, N_A, s["att"], dmix, tm=768)
            do3, dlt3 = _combine_bwd(datt, s["o3"], s["lse3"], headsum)
            dqp, dkp, dvp = _band_bwd(s["qp"], s["kp"], s["vp"], _perm(do3), s["lse_p"], _perm(dlt3))
            dqkv = _rope_bwd(_unperm(dqp), _unperm(dkp), _unperm(dvp), tabs)
            dxn = _matmul(dqkv, W["w_qkv_a"], mode="nt", out_dtype=F32, name="mm_qkv_dx", mnk=(T, D, 3 * A_W), tk=768,
                          b_map=_slab(l, "nt"))
            dw("w_qkv_a", l, N_A, s["xn"], dqkv, tn=768)
        else:
            j = l - N_A
            do = _matmul(dmix, W["w_o_b"], mode="nt", out_dtype=BF16, name="mm_ob_dx", mnk=(T, D, D), b_map=_slab(j, "nt"))
            dw("w_o_b", j, DEPTH - N_A, s["o"], dmix)
            dq, *fox_acc = _fox_bwd(s["q"], kv, do, c_col, c_row, fox_acc)
            dxn = _matmul(dq, W["w_q_b"], mode="nt", out_dtype=F32, name="mm_qb_dx", mnk=(T, D, D), b_map=_slab(j, "nt"))
            dw("w_q_b", j, DEPTH - N_A, s["xn"], dq)
        dh, d_gains[l][0] = _rms_bwd(dxn, s["h"], g[0], dres=dh1, out_dtype=F32, name="rms_bwd_res")
        if l == N_A:
            dk, dv, dcq, dck = fox_acc
            dc16 = dcq[:, :, :2].transpose(1, 0, 2).reshape(T, 16) - dck[:, :2, :].reshape(16, T).T
            dzf, d_bf = _gates_bwd(jnp.pad(dc16, ((0, 0), (0, 128 - 16))), zf, bf_pad)
            dkvf = jnp.concatenate([dk.astype(BF16), dv.astype(BF16), dzf], axis=1)
            d_kvf = _matmul(kvn, dkvf, mode="tn", out_dtype=BF16, name="mm_kvf_dw", tn=128)[:, :2 * D + 16]
            dkvn = _matmul(dkvf, w_kvf_pad, mode="nt", out_dtype=F32, name="mm_kvf_dx", tk=128)
            dh, d_kvnorm = _rms_bwd(dkvn, h_kv, kvn_g, dres=dh, out_dtype=F32, name="rms_bwd_res")
    full_grads = dict(gw, w_kvf=d_kvf)
    small_flat = jnp.concatenate([
        jnp.stack([jnp.stack(r) for r in d_gains]).reshape(-1),
        jnp.stack(d_cw).transpose(0, 2, 1, 3).reshape(-1),
        jnp.stack(d_cb).reshape(-1),
        d_kvnorm.reshape(-1), d_bf[0, :16]])
    return sq, dh, full_grads, small_flat


def _reduce_update(loss, grad_x, full_grads, small_flat, chip, where, ws, ms, vs):
    parts, kinds = [], []
    for nm, kind, shp in BIG:
        g = full_grads[nm]
        if nm == "w_qkv_a":
            g = g.reshape(2, D, N_CHIPS, 576).transpose(0, 2, 1, 3)
        elif nm == "w_kvf":
            g = g.reshape(2, 512, N_CHIPS, 516).transpose(0, 2, 1, 3)
        parts.append(g.reshape(shp))
        kinds.append(kind)
    n_small = small_flat.shape[0]
    gs = jnp.pad(small_flat, (0, 2 * N_CHIPS * SMALL_ROWS * SMALL_W - n_small))
    parts.append(gs.reshape(N_CHIPS, 2, SMALL_ROWS, SMALL_W).transpose(1, 0, 2, 3))
    kinds.append("slab")
    names = [nm for nm, _, _ in BIG] + ["small"]

    sib = _swap_halves(parts, name="reduce_pair_swap")
    halves = [_add_half(g, p, where, name="reduce_pair_add_" + nm) for g, p, nm in zip(parts, sib, names)]
    landed = _scatter_chips(halves, kinds, name="reduce_chip_scatter")
    sums = [_sum_chips(r, h, k, where, name="reduce_chip_sum_" + nm) for r, h, k, nm in zip(landed, halves, kinds, names)]
    *reduced, red_s = _join_halves(sums, name="reduce_pair_join")
    buf_s = lax.dynamic_update_slice(jnp.zeros((2, N_CHIPS, SMALL_ROWS, SMALL_W), F32), red_s.reshape(2, 1, SMALL_ROWS, SMALL_W),
                                     (0, chip, 0, 0))
    (all_s,) = _allgather([buf_s], ["slab"], name="gather_small_grads")
    sflat = all_s.transpose(1, 0, 2, 3).reshape(-1)

    grads = {nm: r.reshape(ws[nm].shape) for (nm, _, _), r in zip(BIG, reduced)}
    o = 0
    g_gains_full = sflat[o:o + 16 * D].reshape(DEPTH, 4, D); o += 16 * D
    g_cw_full = sflat[o:o + 12 * 2 * D_FF].reshape(DEPTH, 3, 2 * D_FF); o += 12 * 2 * D_FF
    grads["conv_b"] = sflat[o:o + 4 * 2 * D_FF].reshape(DEPTH, 2 * D_FF); o += 4 * 2 * D_FF
    grads["kv_norm"] = sflat[o:o + D]; o += D
    grads["b_f"] = sflat[o:o + 16]
    grads["norm_gains"] = lax.dynamic_slice_in_dim(g_gains_full, chip * 256, 256, axis=2)
    grads["conv_w"] = lax.dynamic_slice_in_dim(g_cw_full, chip * 1408, 1408, axis=2)

    names = ["norm_gains", "w_qkv_a", "w_o_a", "w_q_b", "w_o_b", "kv_norm", "w_kvf", "b_f", "w_up", "conv_w", "conv_b", "w_down"]
    deltas, new_m, new_v = {}, {}, {}
    for nm in names:
        shp = ws[nm].shape
        two = (math.prod(shp[:-1]), shp[-1]) if len(shp) > 1 else (1, shp[0])
        d, m2, v2 = _adamw(ws[nm].reshape(two), ms[nm].reshape(two), vs[nm].reshape(two), grads[nm].reshape(two),
                           name="adamw_" + nm)
        deltas[nm], new_m[nm], new_v[nm] = d.reshape(shp), m2.reshape(shp), v2.reshape(shp)

    return (loss, grad_x, *[grads[nm] for nm in names], *[deltas[nm] for nm in names],
            *[new_m[nm] for nm in names], *[new_v[nm] for nm in names])
```

```python
import math

import jax
import jax.numpy as jnp
from jax import lax
from jax.experimental import pallas as pl
from jax.experimental.pallas import tpu as pltpu

F32 = jnp.float32
BF16 = jnp.bfloat16
MESH = pl.DeviceIdType.MESH
ANY = pl.BlockSpec(memory_space=pl.ANY)

T = 2048
D = 1024
HD = 64
DEPTH = 4
N_A = 2
A_W = 768
GW = 256
DIL = (1, 4, 16)
BLK = 128
D_FF = 2816
ROPE_THETA = 500000.0
EPS = 1e-6
NEG = -1e30
N_CHIPS = 4
FQ = 256
CT = 128
VMEM_BIG = 48 * 1024 * 1024

ADAM_LR, ADAM_B1, ADAM_B2, ADAM_EPS, ADAM_WD, ADAM_STEP = 0.001, 0.9, 0.999, 1e-08, 0.01, 10

NN = (((1,), (0,)), ((), ()))
NT = (((1,), (1,)), ((), ()))
TN = (((0,), (0,)), ((), ()))


def _dot(a, b, dims):
    return lax.dot_general(a, b, dims, preferred_element_type=F32)


def _pick(dim, pref):
    if dim <= pref:
        return dim
    best = None
    for t in range(128, pref + 1, 128):
        if dim % t == 0:
            best = t
    assert best is not None, (dim, pref)
    return best


def _params(sem=None, vmem=None):
    kw = {}
    if sem is not None:
        kw["dimension_semantics"] = sem
    if vmem is not None:
        kw["vmem_limit_bytes"] = vmem
    return pltpu.CompilerParams(**kw)


def _matmul(a, b, *, mode, out_dtype, name, mnk=None, alpha=None, tm=2048, tn=512, tk=2048,
            a_map=None, b_map=None, acc_init=None, out_slab=None, out_slabs=None, out_buf=None):
    if mnk is not None:
        M, N, K = mnk
    elif mode == "nn":
        (M, K), (_, N) = a.shape, b.shape
    elif mode == "nt":
        (M, K), (N, _) = a.shape, b.shape
    else:
        (K, M), (_, N) = a.shape, b.shape
    tm, tn, tk = _pick(M, tm), _pick(N, tn), _pick(K, tk)
    nk = K // tk
    dims = {"nn": NN, "nt": NT, "tn": TN}[mode]
    n_in = 2 + (acc_init is not None) + (out_buf is not None)

    def body(*refs):
        a_ref, b_ref = refs[0], refs[1]
        o_ref = refs[n_in]
        k = pl.program_id(2)

        def finish(r):
            if alpha is not None:
                r = r * alpha
            o_ref[...] = r.astype(out_dtype)

        def product():
            r = _dot(a_ref[...], b_ref[...], dims)
            return r if acc_init is None else r + refs[2][...]

        if nk == 1:
            finish(product())
            return
        acc_ref = refs[n_in + 1]

        @pl.when(k == 0)
        def _():
            acc_ref[...] = product()

        @pl.when((k > 0) & (k < nk - 1))
        def _():
            acc_ref[...] += _dot(a_ref[...], b_ref[...], dims)

        @pl.when(k == nk - 1)
        def _():
            finish(acc_ref[...] + _dot(a_ref[...], b_ref[...], dims))

    a_blk = (tk, tm) if mode == "tn" else (tm, tk)
    b_blk = (tn, tk) if mode == "nt" else (tk, tn)
    if a_map is not None:
        a_spec = pl.BlockSpec((None,) + a_blk, a_map(tm, tn, tk))
    elif mode == "tn":
        a_spec = pl.BlockSpec(a_blk, lambda i, j, k: (k, i))
    else:
        a_spec = pl.BlockSpec(a_blk, lambda i, j, k: (i, k))
    if b_map is not None:
        b_spec = pl.BlockSpec((None,) + b_blk, b_map(tm, tn, tk))
    elif mode == "nt":
        b_spec = pl.BlockSpec(b_blk, lambda i, j, k: (j, k))
    else:
        b_spec = pl.BlockSpec(b_blk, lambda i, j, k: (k, j))
    ins, specs, alias = [a, b], [a_spec, b_spec], {}
    if acc_init is not None:
        ins.append(acc_init)
        specs.append(pl.BlockSpec((tm, tn), lambda i, j, k: (i, j)))
    if out_buf is not None:
        alias = {len(ins): 0}
        ins.append(out_buf)
        specs.append(ANY)
    if out_slab is None:
        o_spec = pl.BlockSpec((tm, tn), lambda i, j, k: (i, j))
        o_shape = jax.ShapeDtypeStruct((M, N), out_dtype)
    else:
        o_spec = pl.BlockSpec((None, tm, tn), lambda i, j, k: (out_slab, i, j))
        o_shape = jax.ShapeDtypeStruct((out_slabs, M, N), out_dtype)
    return pl.pallas_call(
        body,
        grid=(M // tm, N // tn, nk),
        in_specs=specs,
        out_specs=o_spec,
        out_shape=o_shape,
        scratch_shapes=[pltpu.VMEM((tm, tn), F32)] if nk > 1 else [],
        input_output_aliases=alias,
        compiler_params=_params(("parallel", "parallel", "arbitrary"), VMEM_BIG),
        name=name,
    )(*ins)


def _slab(l, mode):
    if mode == "nt":
        return lambda tm, tn, tk: (lambda i, j, k: (l, j, k))
    return lambda tm, tn, tk: (lambda i, j, k: (l, k, j))


def _rms_fwd(x, g, *, out_dtype, name, res=None, tr=256):
    n, d = x.shape

    def body(*refs):
        x_ref, g_ref = refs[0], refs[1]
        o_ref = refs[-1]
        xv = x_ref[...].astype(F32)
        y = xv * lax.rsqrt(jnp.mean(xv * xv, axis=-1, keepdims=True) + EPS) * g_ref[...]
        if res is not None:
            y = y + refs[2][...]
        o_ref[...] = y.astype(out_dtype)

    row = pl.BlockSpec((tr, d), lambda i: (i, 0))
    vec = pl.BlockSpec((1, d), lambda i: (0, 0))
    ins = [x, g] + ([] if res is None else [res])
    specs = [row, vec] + ([] if res is None else [row])
    return pl.pallas_call(
        body, grid=(n // tr,), in_specs=specs, out_specs=row,
        out_shape=jax.ShapeDtypeStruct((n, d), out_dtype),
        compiler_params=_params(("parallel",)), name=name,
    )(*ins)


def _rms_bwd(dy, x, g, *, out_dtype, name, dres=None, tr=256):
    n, d = x.shape

    def body(*refs):
        dy_ref, x_ref, g_ref = refs[0], refs[1], refs[2]
        dx_ref, dg_ref = refs[-2], refs[-1]
        xv = x_ref[...].astype(F32)
        dyv = dy_ref[...].astype(F32)
        rstd = lax.rsqrt(jnp.mean(xv * xv, axis=-1, keepdims=True) + EPS)
        xhat = xv * rstd
        dxh = dyv * g_ref[...]
        dx = rstd * (dxh - xhat * jnp.mean(dxh * xhat, axis=-1, keepdims=True))
        if dres is not None:
            dx = dx + refs[3][...]
        dx_ref[...] = dx.astype(out_dtype)

        @pl.when(pl.program_id(0) == 0)
        def _():
            dg_ref[...] = jnp.zeros_like(dg_ref)

        dg_ref[...] += jnp.sum(dyv * xhat, axis=0, keepdims=True)

    row = pl.BlockSpec((tr, d), lambda i: (i, 0))
    vec = pl.BlockSpec((1, d), lambda i: (0, 0))
    ins = [dy, x, g] + ([] if dres is None else [dres])
    specs = [row, row, vec] + ([] if dres is None else [row])
    return pl.pallas_call(
        body, grid=(n // tr,), in_specs=specs, out_specs=[row, vec],
        out_shape=[jax.ShapeDtypeStruct((n, d), out_dtype), jax.ShapeDtypeStruct((1, d), F32)],
        compiler_params=_params(("arbitrary",)), name=name,
    )(*ins)


def _loss_head(h, target, *, tr=256):
    n, d = h.shape

    def body(h_ref, t_ref, dh_ref, s_ref):
        err = h_ref[...] - t_ref[...]
        dh_ref[...] = err * (1.0 / d)

        @pl.when(pl.program_id(0) == 0)
        def _():
            s_ref[...] = jnp.zeros_like(s_ref)

        s_ref[...] += jnp.sum(err * err)

    row = pl.BlockSpec((tr, d), lambda i: (i, 0))
    acc = pl.BlockSpec((8, 128), lambda i: (0, 0))
    return pl.pallas_call(
        body, grid=(n // tr,), in_specs=[row, row], out_specs=[row, acc],
        out_shape=[jax.ShapeDtypeStruct((n, d), F32), jax.ShapeDtypeStruct((8, 128), F32)],
        compiler_params=_params(("arbitrary",)), name="loss_head",
    )(h, target)


def _rope_tables():
    pos = jnp.arange(T, dtype=F32)
    inv = ROPE_THETA ** (-jnp.arange(0, 16, 2, dtype=F32) / 16)
    ang = pos[:, None] * inv[None, :]
    cos, sin = jnp.cos(ang), jnp.sin(ang)
    one = jnp.ones((T, HD - 16), F32)
    zero8 = jnp.zeros((T, 8), F32)
    zero = jnp.zeros((T, HD - 16), F32)
    c = jnp.concatenate([cos, cos, one], axis=1)
    s1 = jnp.concatenate([zero8, sin, zero], axis=1)
    s2 = jnp.concatenate([-sin, zero8, zero], axis=1)
    return tuple(jnp.concatenate([t, t], axis=1) for t in (c, s1, s2))


def _rope_fwd(qkv, tabs, *, tr=256):
    def body(x_ref, c_ref, s1_ref, s2_ref, q_ref, k_ref, v_ref):
        c, s1, s2 = c_ref[...], s1_ref[...], s2_ref[...]
        for which, o_ref, scale in ((0, q_ref, HD ** -0.5), (1, k_ref, None)):
            for j in range(A_W // 128):
                x = x_ref[:, which * A_W + j * 128: which * A_W + (j + 1) * 128]
                y = x * c + pltpu.roll(x, 8, 1) * s1 + pltpu.roll(x, 120, 1) * s2
                if scale is not None:
                    y = y * scale
                o_ref[j // 2, :, (j % 2) * 128:(j % 2 + 1) * 128] = y.astype(BF16)
        for j in range(A_W // 128):
            v_ref[j // 2, :, (j % 2) * 128:(j % 2 + 1) * 128] = x_ref[:, 2 * A_W + j * 128: 2 * A_W + (j + 1) * 128].astype(BF16)

    tab = pl.BlockSpec((tr, 128), lambda i: (i, 0))
    out = pl.BlockSpec((3, tr, GW), lambda i: (0, i, 0))
    shp = jax.ShapeDtypeStruct((3, T, GW), BF16)
    return pl.pallas_call(
        body, grid=(T // tr,), in_specs=[pl.BlockSpec((tr, 3 * A_W), lambda i: (i, 0)), tab, tab, tab],
        out_specs=[out, out, out], out_shape=[shp, shp, shp],
        compiler_params=_params(("parallel",)), name="rope_fwd",
    )(qkv, *tabs)


def _rope_bwd(dq, dk, dv, tabs, *, tr=256):
    def body(dq_ref, dk_ref, dv_ref, c_ref, s1_ref, s2_ref, o_ref):
        c, s1, s2 = c_ref[...], s1_ref[...], s2_ref[...]
        for which, i_ref, scale in ((0, dq_ref, HD ** -0.5), (1, dk_ref, None)):
            for j in range(A_W // 128):
                g = i_ref[j // 2, :, (j % 2) * 128:(j % 2 + 1) * 128]
                y = g * c + pltpu.roll(g * s1, 120, 1) + pltpu.roll(g * s2, 8, 1)
                if scale is not None:
                    y = y * scale
                o_ref[:, which * A_W + j * 128: which * A_W + (j + 1) * 128] = y.astype(BF16)
        for j in range(A_W // 128):
            o_ref[:, 2 * A_W + j * 128: 2 * A_W + (j + 1) * 128] = dv_ref[j // 2, :, (j % 2) * 128:(j % 2 + 1) * 128].astype(BF16)

    tab = pl.BlockSpec((tr, 128), lambda i: (i, 0))
    cot = pl.BlockSpec((3, tr, GW), lambda i: (0, i, 0))
    return pl.pallas_call(
        body, grid=(T // tr,), in_specs=[cot, cot, cot, tab, tab, tab],
        out_specs=pl.BlockSpec((tr, 3 * A_W), lambda i: (i, 0)),
        out_shape=jax.ShapeDtypeStruct((T, 3 * A_W), BF16),
        compiler_params=_params(("parallel",)), name="rope_bwd",
    )(dq, dk, dv, *tabs)


def _perm(x3):
    out = [x3[0]]
    for g in (1, 2):
        r = DIL[g]
        out.append(x3[g].reshape(T // r, r, GW).transpose(1, 0, 2).reshape(T, GW))
    return jnp.stack(out)


def _unperm(x3):
    out = [x3[0]]
    for g in (1, 2):
        r = DIL[g]
        out.append(x3[g].reshape(r, T // r, GW).transpose(1, 0, 2).reshape(T, GW))
    return jnp.stack(out)


def _head_mask(x, lane_lo):
    lane = lax.broadcasted_iota(jnp.int32, x.shape, 1)
    keep = (lane < HD) if lane_lo else (lane >= HD)
    return jnp.where(keep, x.astype(F32), 0.0).astype(BF16)


def _band_scalars():
    g, b = pl.program_id(0), pl.program_id(1)
    nbs = lax.shift_right_logical(jnp.int32(T // BLK), 2 * g)
    has_prev = jnp.where((b & (nbs - 1)) != 0, 1, 0)
    next_ok = jnp.where(((b + 1) & (nbs - 1)) != 0, 1, 0)
    return has_prev, next_ok


def _band_mask_q(has_prev):
    row = lax.broadcasted_iota(jnp.int32, (BLK, 2 * BLK), 0)
    col = lax.broadcasted_iota(jnp.int32, (BLK, 2 * BLK), 1)
    return ((col < BLK) & (col >= row) & (has_prev == 1)) | ((col >= BLK) & (col - BLK <= row))


def _band_mask_k(next_ok):
    row = lax.broadcasted_iota(jnp.int32, (2 * BLK, BLK), 0)
    col = lax.broadcasted_iota(jnp.int32, (2 * BLK, BLK), 1)
    return ((row < BLK) & (col <= row)) | ((row >= BLK) & (col >= row - BLK) & (next_ok == 1))


def _band_fwd(q, k, v):
    nb = T // BLK

    def body(q_ref, kc_ref, kp_ref, vc_ref, vp_ref, o_ref, l_ref):
        has_prev, _ = _band_scalars()
        mask = _band_mask_q(has_prev)
        lane = lax.broadcasted_iota(jnp.int32, (BLK, 128), 1)
        for p in range(2):
            sl = slice(128 * p, 128 * (p + 1))
            qp = q_ref[0, :, sl]
            kcat = jnp.concatenate([kp_ref[0, :, sl], kc_ref[0, :, sl]], axis=0)
            vcat = jnp.concatenate([vp_ref[0, :, sl], vc_ref[0, :, sl]], axis=0)
            o_acc = jnp.zeros((BLK, 128), F32)
            lse = jnp.zeros((BLK, 128), F32)
            for e in range(2):
                s = _dot(_head_mask(qp, e == 0), kcat, NT)
                s = jnp.where(mask, s, NEG)
                m = jnp.max(s, axis=-1, keepdims=True)
                pr = jnp.exp(s - m)
                l = jnp.sum(pr, axis=-1, keepdims=True)
                o_acc = o_acc + _dot(pr.astype(BF16), _head_mask(vcat, e == 0), NN) / l
                lse = jnp.where((lane < HD) if e == 0 else (lane >= HD), m + jnp.log(l), lse)
            o_ref[0, :, sl] = o_acc
            l_ref[0, :, sl] = lse

    cur = pl.BlockSpec((1, BLK, GW), lambda g, b: (g, b, 0))
    prev = pl.BlockSpec((1, BLK, GW), lambda g, b: (g, jnp.maximum(b - 1, 0), 0))
    shp = jax.ShapeDtypeStruct((3, T, GW), F32)
    return pl.pallas_call(
        body, grid=(3, nb), in_specs=[cur, cur, prev, cur, prev], out_specs=[cur, cur], out_shape=[shp, shp],
        compiler_params=_params(("parallel", "parallel")), name="band_fwd",
    )(q, k, k, v, v)


def _band_bwd(q, k, v, do, lse, dlt):
    nb = T // BLK

    def body(qc_ref, qn_ref, kc_ref, kp_ref, vc_ref, vp_ref, doc_ref, don_ref, lc_ref, ln_ref, dc_ref, dn_ref,
             dq_ref, dk_ref, dv_ref):
        has_prev, next_ok = _band_scalars()
        mask_q = _band_mask_q(has_prev)
        mask_k = _band_mask_k(next_ok)
        for p in range(2):
            sl = slice(128 * p, 128 * (p + 1))
            qc, qn = qc_ref[0, :, sl], qn_ref[0, :, sl]
            doc, don = doc_ref[0, :, sl], don_ref[0, :, sl]
            kc, vc = kc_ref[0, :, sl], vc_ref[0, :, sl]
            kcat = jnp.concatenate([kp_ref[0, :, sl], kc], axis=0)
            vcat = jnp.concatenate([vp_ref[0, :, sl], vc], axis=0)
            qcat = jnp.concatenate([qc, qn], axis=0)
            docat = jnp.concatenate([doc, don], axis=0)
            dq = jnp.zeros((BLK, 128), F32)
            dk = jnp.zeros((BLK, 128), F32)
            dv = jnp.zeros((BLK, 128), F32)
            for e in range(2):
                lo = e == 0
                col = slice(128 * p + HD * e, 128 * p + HD * e + 1)
                lse_c, lse_n = lc_ref[0, :, col], ln_ref[0, :, col]
                dl_c, dl_n = dc_ref[0, :, col], dn_ref[0, :, col]
                s = jnp.where(mask_q, _dot(_head_mask(qc, lo), kcat, NT), NEG)
                pr = jnp.exp(s - lse_c)
                dp = _dot(_head_mask(doc, lo), vcat, NT)
                ds = pr * (dp - dl_c)
                dq = dq + _dot(ds.astype(BF16), _head_mask(kcat, lo), NN)
                qm, dom = _head_mask(qcat, lo), _head_mask(docat, lo)
                s2 = jnp.where(mask_k, _dot(qm, kc, NT), NEG)
                p2 = jnp.exp(s2 - jnp.concatenate([lse_c, lse_n], axis=0))
                dv = dv + _dot(p2.astype(BF16), dom, TN)
                dp2 = _dot(dom, vc, NT)
                ds2 = p2 * (dp2 - jnp.concatenate([dl_c, dl_n], axis=0))
                dk = dk + _dot(ds2.astype(BF16), qm, TN)
            dq_ref[0, :, sl] = dq
            dk_ref[0, :, sl] = dk
            dv_ref[0, :, sl] = dv

    cur = pl.BlockSpec((1, BLK, GW), lambda g, b: (g, b, 0))
    prev = pl.BlockSpec((1, BLK, GW), lambda g, b: (g, jnp.maximum(b - 1, 0), 0))
    nxt = pl.BlockSpec((1, BLK, GW), lambda g, b: (g, jnp.minimum(b + 1, nb - 1), 0))
    shp = jax.ShapeDtypeStruct((3, T, GW), F32)
    return pl.pallas_call(
        body, grid=(3, nb),
        in_specs=[cur, nxt, cur, prev, cur, prev, cur, nxt, cur, nxt, cur, nxt],
        out_specs=[cur, cur, cur], out_shape=[shp, shp, shp],
        compiler_params=_params(("parallel", "parallel")), name="band_bwd",
    )(q, q, k, k, v, v, do, do, lse, lse, dlt, dlt)


def _split3(x):
    hi = x.astype(BF16)
    r = x - hi.astype(F32)
    mid = r.astype(BF16)
    lo = (r - mid.astype(F32)).astype(BF16)
    return hi, mid, lo


def _dot3(x, m, dims=NN):
    hi, mid, lo = _split3(x)
    return _dot(hi, m, dims) + _dot(mid, m, dims) + _dot(lo, m, dims)


def _combine_weights(l_ref):
    l0, l1, l2 = l_ref[0], l_ref[1], l_ref[2]
    m = jnp.maximum(jnp.maximum(l0, l1), l2)
    e = [jnp.exp(l0 - m), jnp.exp(l1 - m), jnp.exp(l2 - m)]
    inv = 1.0 / (e[0] + e[1] + e[2])
    return [ei * inv for ei in e]


def _combine_fwd(o, lse, *, tr=256):
    def body(o_ref, l_ref, out_ref):
        alpha = _combine_weights(l_ref)
        for g in range(3):
            out_ref[:, g * GW:(g + 1) * GW] = (o_ref[g] * alpha[g]).astype(BF16)

    blk = pl.BlockSpec((3, tr, GW), lambda i: (0, i, 0))
    return pl.pallas_call(
        body, grid=(T // tr,), in_specs=[blk, blk], out_specs=pl.BlockSpec((tr, A_W), lambda i: (i, 0)),
        out_shape=jax.ShapeDtypeStruct((T, A_W), BF16), compiler_params=_params(("parallel",)), name="combine_fwd",
    )(o, lse)


def _combine_bwd(datt, o, lse, headsum, *, tr=256):
    def body(d_ref, o_ref, l_ref, hs_ref, do_ref, dl_ref):
        alpha = _combine_weights(l_ref)
        hs = hs_ref[...]
        total = jnp.zeros((tr, GW), F32)
        for g in range(3):
            dg = d_ref[:, g * GW:(g + 1) * GW]
            do_ref[g] = (dg * alpha[g]).astype(BF16)
            total = total + alpha[g] * _dot3(dg * o_ref[g], hs)
        for g in range(3):
            dl_ref[g] = alpha[g] * total

    blk = pl.BlockSpec((3, tr, GW), lambda i: (0, i, 0))
    return pl.pallas_call(
        body, grid=(T // tr,),
        in_specs=[pl.BlockSpec((tr, A_W), lambda i: (i, 0)), blk, blk, pl.BlockSpec((GW, GW), lambda i: (0, 0))],
        out_specs=[blk, blk],
        out_shape=[jax.ShapeDtypeStruct((3, T, GW), BF16), jax.ShapeDtypeStruct((3, T, GW), F32)],
        compiler_params=_params(("parallel",)), name="combine_bwd",
    )(datt, o, lse, headsum)


def _fox_scores(qm, k_ref, cq, ck_ref, e, i, n):
    s = _dot(qm, k_ref[0:n, :], NT) + (cq - ck_ref[0, e:e + 1, 0:n])
    row = lax.broadcasted_iota(jnp.int32, (FQ, n), 0)
    col = lax.broadcasted_iota(jnp.int32, (FQ, n), 1)
    s = jnp.where(col <= row + i * FQ, s, NEG)
    m = jnp.max(s, axis=-1, keepdims=True)
    pr = jnp.exp(s - m)
    return pr, jnp.sum(pr, axis=-1, keepdims=True)


def _fox_fwd(q, kv, c_col, c_row):
    def body(q_ref, k_ref, v_ref, cc_ref, cr_ref, o_ref, vm_ref):
        for e in range(2):
            vm_ref[e] = _head_mask(v_ref[...], e == 0)
        for i in range(T // FQ):
            n = (i + 1) * FQ
            rows = slice(i * FQ, n)
            acc = jnp.zeros((FQ, 128), F32)
            for e in range(2):
                qm = _head_mask(q_ref[rows, :], e == 0)
                pr, l = _fox_scores(qm, k_ref, cc_ref[0, rows, e:e + 1], cr_ref, e, i, n)
                acc = acc + _dot(pr.astype(BF16), vm_ref[e, 0:n, :], NN) / l
            o_ref[rows, :] = acc.astype(BF16)

    pair = pl.BlockSpec((T, 128), lambda p: (0, p))
    return pl.pallas_call(
        body, grid=(D // 128,),
        in_specs=[pair, pair, pl.BlockSpec((T, 128), lambda p: (0, D // 128 + p)),
                  pl.BlockSpec((1, T, 2), lambda p: (p, 0, 0)), pl.BlockSpec((1, 2, T), lambda p: (p, 0, 0))],
        out_specs=pair, out_shape=jax.ShapeDtypeStruct((T, D), BF16),
        scratch_shapes=[pltpu.VMEM((2, T, 128), BF16)],
        compiler_params=_params(("parallel",), VMEM_BIG), name="fox_fwd",
    )(q, kv, kv, c_col, c_row)


def _fox_bwd(q, kv, do, c_col, c_row, init):
    def body(q_ref, k_ref, v_ref, do_ref, cc_ref, cr_ref, ik_ref, iv_ref, iq_ref, ic_ref,
             dq_ref, dk_ref, dv_ref, dcq_ref, dck_ref, km_ref):
        dk_ref[...] = ik_ref[...]
        dv_ref[...] = iv_ref[...]
        dcq_ref[...] = iq_ref[...]
        dck_ref[...] = ic_ref[...]
        for e in range(2):
            km_ref[e] = _head_mask(k_ref[...], e == 0)
        for i in range(T // FQ):
            n = (i + 1) * FQ
            rows = slice(i * FQ, n)
            dq = jnp.zeros((FQ, 128), F32)
            for e in range(2):
                qm = _head_mask(q_ref[rows, :], e == 0)
                dom = _head_mask(do_ref[rows, :], e == 0)
                pr, l = _fox_scores(qm, k_ref, cc_ref[0, rows, e:e + 1], cr_ref, e, i, n)
                pr = pr / l
                dp = _dot(dom, v_ref[0:n, :], NT)
                ds = pr * (dp - jnp.sum(pr * dp, axis=-1, keepdims=True))
                dsb = ds.astype(BF16)
                dq = dq + _dot(dsb, km_ref[e, 0:n, :], NN)
                dk_ref[0:n, :] += _dot(dsb, qm, TN)
                dv_ref[0:n, :] += _dot(pr.astype(BF16), dom, TN)
                dcq_ref[0, rows, e:e + 1] += jnp.sum(ds, axis=-1, keepdims=True)
                dck_ref[0, e:e + 1, 0:n] += jnp.sum(ds, axis=0, keepdims=True)
            dq_ref[rows, :] = (dq * HD ** -0.5).astype(BF16)

    pair = pl.BlockSpec((T, 128), lambda p: (0, p))
    cq = pl.BlockSpec((1, T, 128), lambda p: (p, 0, 0))
    ck = pl.BlockSpec((1, 8, T), lambda p: (p, 0, 0))
    return pl.pallas_call(
        body, grid=(D // 128,),
        in_specs=[pair, pair, pl.BlockSpec((T, 128), lambda p: (0, D // 128 + p)), pair,
                  pl.BlockSpec((1, T, 2), lambda p: (p, 0, 0)), pl.BlockSpec((1, 2, T), lambda p: (p, 0, 0)),
                  pair, pair, cq, ck],
        out_specs=[pair, pair, pair, cq, ck],
        out_shape=[jax.ShapeDtypeStruct((T, D), BF16), jax.ShapeDtypeStruct((T, D), F32), jax.ShapeDtypeStruct((T, D), F32),
                   jax.ShapeDtypeStruct((D // 128, T, 128), F32), jax.ShapeDtypeStruct((D // 128, 8, T), F32)],
        scratch_shapes=[pltpu.VMEM((2, T, 128), BF16)],
        compiler_params=_params(("parallel",), VMEM_BIG), name="fox_bwd",
    )(q, kv, kv, do, c_col, c_row, *init)


def _tri(lower):
    r = lax.broadcasted_iota(jnp.int32, (BLK, BLK), 0)
    c = lax.broadcasted_iota(jnp.int32, (BLK, BLK), 1)
    return jnp.where((c <= r) if lower else (c >= r), 1.0, 0.0).astype(BF16)


def _gates_fwd(z, b):
    def body(z_ref, b_ref, c_ref):
        tri = _tri(True)
        carry = jnp.zeros((1, 128), F32)
        for i in range(T // BLK):
            rows = slice(i * BLK, (i + 1) * BLK)
            x = z_ref[rows, :] + b_ref[...]
            logf = jnp.minimum(x, 0.0) - jnp.log(1.0 + jnp.exp(-jnp.abs(x)))
            hi, mid, lo = _split3(logf)
            y = _dot(tri, hi, NN) + _dot(tri, mid, NN) + _dot(tri, lo, NN) + carry
            c_ref[rows, :] = y
            carry = y[BLK - 1:BLK, :]

    return pl.pallas_call(body, out_shape=jax.ShapeDtypeStruct((T, 128), F32), name="gates_fwd")(z, b)


def _gates_bwd(dc, z, b):
    def body(dc_ref, z_ref, b_ref, dz_ref, db_ref):
        tri = _tri(False)
        carry = jnp.zeros((1, 128), F32)
        db = jnp.zeros((1, 128), F32)
        for i in reversed(range(T // BLK)):
            rows = slice(i * BLK, (i + 1) * BLK)
            hi, mid, lo = _split3(dc_ref[rows, :])
            dlogf = _dot(tri, hi, NN) + _dot(tri, mid, NN) + _dot(tri, lo, NN) + carry
            carry = dlogf[0:1, :]
            x = z_ref[rows, :] + b_ref[...]
            dz = dlogf / (1.0 + jnp.exp(x))
            dz_ref[rows, :] = dz.astype(BF16)
            db = db + jnp.sum(dz, axis=0, keepdims=True)
        db_ref[...] = db

    return pl.pallas_call(
        body, out_shape=[jax.ShapeDtypeStruct((T, 128), BF16), jax.ShapeDtypeStruct((1, 128), F32)], name="gates_bwd",
    )(dc, z, b)


def _conv_pair(a_refs, cw_refs, cb_refs):
    row = lax.broadcasted_iota(jnp.int32, (T, CT), 0)
    outs = []
    for a_ref, cw_ref, cb_ref in zip(a_refs, cw_refs, cb_refs):
        z = a_ref[...]
        z1 = jnp.where(row >= 1, pltpu.roll(z, 1, 0), 0.0)
        z2 = jnp.where(row >= 2, pltpu.roll(z, 2, 0), 0.0)
        y = cw_ref[2:3, :] * z + cw_ref[1:2, :] * z1 + cw_ref[0:1, :] * z2 + cb_ref[...]
        outs.append((y, z, z1, z2))
    return outs


_GELU_K = math.sqrt(2.0 / math.pi)
N_CT = D_FF // CT


def _conv_specs():
    def at(rows, off):
        return pl.BlockSpec((rows, CT), lambda j: (0, j + off))
    return [at(T, 0), at(T, N_CT), at(3, 0), at(3, N_CT), at(1, 0), at(1, N_CT)]


def _convgate_fwd(a, cw, cb):
    def body(ag_ref, av_ref, wg_ref, wv_ref, bg_ref, bv_ref, u_ref):
        (g, _, _, _), (v, _, _, _) = _conv_pair((ag_ref, av_ref), (wg_ref, wv_ref), (bg_ref, bv_ref))
        th = jnp.tanh(_GELU_K * (g + 0.044715 * g * g * g))
        u_ref[...] = (0.5 * g * (1.0 + th) * v).astype(BF16)

    return pl.pallas_call(
        body, grid=(N_CT,), in_specs=_conv_specs(),
        out_specs=pl.BlockSpec((T, CT), lambda j: (0, j)), out_shape=jax.ShapeDtypeStruct((T, D_FF), BF16),
        compiler_params=_params(("parallel",), VMEM_BIG), name="convgate_fwd",
    )(a, a, cw, cw, cb, cb)


def _convgate_bwd(a, du, cw, cb):
    def body(ag_ref, av_ref, wg_ref, wv_ref, bg_ref, bv_ref, du_ref, da_ref, dcw_ref, dcb_ref):
        (g, gz, gz1, gz2), (v, vz, vz1, vz2) = _conv_pair((ag_ref, av_ref), (wg_ref, wv_ref), (bg_ref, bv_ref))
        du = du_ref[...].astype(F32)
        th = jnp.tanh(_GELU_K * (g + 0.044715 * g * g * g))
        gelu = 0.5 * g * (1.0 + th)
        dgelu = 0.5 * (1.0 + th) + 0.5 * g * (1.0 - th * th) * _GELU_K * (1.0 + 3 * 0.044715 * g * g)
        row = lax.broadcasted_iota(jnp.int32, (T, CT), 0)
        for h, (d, z, z1, z2, w_ref) in enumerate(((du * v * dgelu, gz, gz1, gz2, wg_ref), (du * gelu, vz, vz1, vz2, wv_ref))):
            d1 = jnp.where(row < T - 1, pltpu.roll(d, T - 1, 0), 0.0)
            d2 = jnp.where(row < T - 2, pltpu.roll(d, T - 2, 0), 0.0)
            da_ref[h] = (w_ref[2:3, :] * d + w_ref[1:2, :] * d1 + w_ref[0:1, :] * d2).astype(BF16)
            dcw_ref[h, 0:1, :] = jnp.sum(d * z2, axis=0, keepdims=True)
            dcw_ref[h, 1:2, :] = jnp.sum(d * z1, axis=0, keepdims=True)
            dcw_ref[h, 2:3, :] = jnp.sum(d * z, axis=0, keepdims=True)
            dcb_ref[h] = jnp.sum(d, axis=0, keepdims=True)

    def both(rows):
        return pl.BlockSpec((2, rows, CT), lambda j: (0, 0, j))

    return pl.pallas_call(
        body, grid=(N_CT,),
        in_specs=_conv_specs() + [pl.BlockSpec((T, CT), lambda j: (0, j))],
        out_specs=[both(T), both(3), both(1)],
        out_shape=[jax.ShapeDtypeStruct((2, T, D_FF), BF16), jax.ShapeDtypeStruct((2, 3, D_FF), F32),
                   jax.ShapeDtypeStruct((2, 1, D_FF), F32)],
        compiler_params=_params(("parallel",), VMEM_BIG), name="convgate_bwd",
    )(a, a, cw, cw, cb, cb, du)


def _halves_a(tm, tn, tk):
    per = D_FF // tk
    return lambda i, j, k: (lax.div(k, per), i, lax.rem(k, per))


def _halves_b(tm, tn, tk):
    per = D_FF // tn
    return lambda i, j, k: (lax.div(j, per), k, lax.rem(j, per))


def _adamw(w, m, v, g, *, name):
    r, c = w.shape
    tr = r
    if r * c > 256 * 1024:
        for cand in range(8, r, 8):
            if r % cand == 0 and cand * c <= 256 * 1024:
                tr = cand

    def body(w_ref, m_ref, v_ref, g_ref, d_ref, nm_ref, nv_ref):
        gv = g_ref[...]
        mn = ADAM_B1 * m_ref[...] + (1.0 - ADAM_B1) * gv
        vn = ADAM_B2 * v_ref[...] + (1.0 - ADAM_B2) * (gv * gv)
        m_hat = mn / (1.0 - ADAM_B1 ** ADAM_STEP)
        v_hat = vn / (1.0 - ADAM_B2 ** ADAM_STEP)
        d_ref[...] = -ADAM_LR * (m_hat / (jnp.sqrt(v_hat) + ADAM_EPS) + ADAM_WD * w_ref[...])
        nm_ref[...] = mn
        nv_ref[...] = vn

    blk = pl.BlockSpec((tr, c), lambda i: (i, 0))
    shp = jax.ShapeDtypeStruct((r, c), F32)
    return pl.pallas_call(
        body, grid=(r // tr,), in_specs=[blk] * 4, out_specs=[blk] * 3, out_shape=[shp] * 3,
        compiler_params=_params(("parallel",)), name=name,
    )(w, m, v, g)


def _place():
    x, y, c = lax.axis_index("x"), lax.axis_index("y"), lax.axis_index("c")
    chips = [(1 - x, y), (x, 1 - y), (1 - x, 1 - y)]
    return x, y, c, chips


def _window(ref, kind, s, half=None):
    lead = () if half is None else (half,)
    b, c = ref.shape[-2], ref.shape[-1]
    if kind == "col":
        return ref.at[lead + (slice(None), slice(None), pl.ds(s * (c // N_CHIPS), c // N_CHIPS))]
    if kind == "row":
        return ref.at[lead + (slice(None), pl.ds(s * (b // N_CHIPS), b // N_CHIPS), slice(None))]
    return ref.at[lead + (s,)]


def _window_shape(shape3, kind):
    a, b, c = shape3
    return {"col": (a, b, c // N_CHIPS), "row": (a, b // N_CHIPS, c), "slab": (b, c)}[kind]


def _allgather(tensors, kinds, *, name):
    n = len(tensors)

    def body(*refs):
        bufs = refs[n:2 * n]
        send, recv = refs[2 * n:]
        x, y, c, chips = _place()
        me = 2 * x + y
        sib = (x, y, 1 - c)

        def rcopy(i, k, win, to):
            return pltpu.make_async_remote_copy(src_ref=win, dst_ref=win, send_sem=send.at[i * 6 + k], recv_sem=recv.at[i * 6 + k],
                                                device_id=to, device_id_type=MESH)

        started = []
        for i in range(n):
            for k, (px, py) in enumerate(chips):
                cp = rcopy(i, k, _window(bufs[i], kinds[i], me, c), (px, py, c))
                cp.start()
                started.append(cp)
        for i in range(n):
            for k, (px, py) in enumerate(chips):
                landed = _window(bufs[i], kinds[i], 2 * px + py, c)
                rcopy(i, k, landed, (px, py, c)).wait_recv()
                fw = rcopy(i, 3 + k, landed, sib)
                fw.start()
                started.append(fw)
        for i in range(n):
            for k, (px, py) in enumerate(chips):
                rcopy(i, 3 + k, _window(bufs[i], kinds[i], 2 * px + py, 1 - c), sib).wait_recv()
        for cp in started:
            cp.wait_send()

    return pl.pallas_call(
        body, in_specs=[ANY] * n, out_specs=[ANY] * n,
        out_shape=[jax.ShapeDtypeStruct(t.shape, t.dtype) for t in tensors],
        scratch_shapes=[pltpu.SemaphoreType.DMA((6 * n,)), pltpu.SemaphoreType.DMA((6 * n,))],
        input_output_aliases={i: i for i in range(n)},
        name=name,
    )(*tensors)


def _swap_halves(tensors, *, name):
    n = len(tensors)

    def body(*refs):
        ins, outs = refs[:n], refs[n:2 * n]
        send, recv = refs[2 * n:]
        x, y, c, _ = _place()
        cps = []
        for i in range(n):
            cp = pltpu.make_async_remote_copy(src_ref=ins[i].at[1 - c], dst_ref=outs[i], send_sem=send.at[i],
                                              recv_sem=recv.at[i], device_id=(x, y, 1 - c), device_id_type=MESH)
            cp.start()
            cps.append(cp)
        for cp in cps:
            cp.wait()

    return pl.pallas_call(
        body, in_specs=[ANY] * n, out_specs=[ANY] * n,
        out_shape=[jax.ShapeDtypeStruct(t.shape[1:], t.dtype) for t in tensors],
        scratch_shapes=[pltpu.SemaphoreType.DMA((n,)), pltpu.SemaphoreType.DMA((n,))], name=name,
    )(*tensors)


def _rows_tile(rows, cols, sub):
    best = None
    for t in range(sub, rows + 1, sub):
        if rows % t == 0 and t * cols <= 512 * 1024:
            best = t
    return rows if best is None else best


def _add_half(g, p, where, *, name):
    a, b, c = p.shape
    tr = _rows_tile(b, c, 16)

    def body(w_ref, g_ref, p_ref, o_ref):
        o_ref[...] = (g_ref[...].astype(F32) + p_ref[...].astype(F32)).astype(o_ref.dtype)

    return pl.pallas_call(
        body,
        grid_spec=pltpu.PrefetchScalarGridSpec(
            num_scalar_prefetch=1, grid=(a, b // tr),
            in_specs=[pl.BlockSpec((None, None, tr, c), lambda i, r, w: (w[1], i, r, 0)),
                      pl.BlockSpec((None, tr, c), lambda i, r, w: (i, r, 0))],
            out_specs=pl.BlockSpec((None, tr, c), lambda i, r, w: (i, r, 0))),
        out_shape=jax.ShapeDtypeStruct(p.shape, g.dtype),
        compiler_params=_params(("parallel", "parallel")), name=name,
    )(where, g, p)


def _scatter_chips(tensors, kinds, *, name):
    n = len(tensors)

    def body(*refs):
        ins, outs = refs[:n], refs[n:2 * n]
        send, recv = refs[2 * n:]
        x, y, c, chips = _place()
        cps = []
        for i in range(n):
            for k, (px, py) in enumerate(chips):
                cp = pltpu.make_async_remote_copy(src_ref=_window(ins[i], kinds[i], 2 * px + py), dst_ref=outs[i].at[k],
                                                  send_sem=send.at[i * 3 + k], recv_sem=recv.at[i * 3 + k],
                                                  device_id=(px, py, c), device_id_type=MESH)
                cp.start()
                cps.append(cp)
        for cp in cps:
            cp.wait()

    return pl.pallas_call(
        body, in_specs=[ANY] * n, out_specs=[ANY] * n,
        out_shape=[jax.ShapeDtypeStruct((3,) + _window_shape(t.shape, k), t.dtype) for t, k in zip(tensors, kinds)],
        scratch_shapes=[pltpu.SemaphoreType.DMA((3 * n,)), pltpu.SemaphoreType.DMA((3 * n,))],
        name=name,
    )(*tensors)


def _sum_chips(r, h, kind, where, *, name):
    if kind == "slab":
        r = r.reshape((3, 1) + r.shape[1:])
    _, a, b, c = r.shape
    tr = _rows_tile(b, c, 16)
    if kind == "col":
        h_spec = pl.BlockSpec((None, tr, c), lambda i, j, w: (i, j, w[0]))
    elif kind == "row":
        h_spec = pl.BlockSpec((None, tr, c), lambda i, j, w: (i, w[0] * (b // tr) + j, 0))
    else:
        h_spec = pl.BlockSpec((None, tr, c), lambda i, j, w: (w[0], j, 0))

    def body(w_ref, h_ref, r0_ref, r1_ref, r2_ref, o_ref):
        o_ref[...] = ((h_ref[...].astype(F32) + r0_ref[...].astype(F32)) + r1_ref[...].astype(F32)) + r2_ref[...].astype(F32)

    def slot(k):
        return pl.BlockSpec((None, None, tr, c), lambda i, j, w: (k, i, j, 0))

    return pl.pallas_call(
        body,
        grid_spec=pltpu.PrefetchScalarGridSpec(
            num_scalar_prefetch=1, grid=(a, b // tr),
            in_specs=[h_spec, slot(0), slot(1), slot(2)],
            out_specs=pl.BlockSpec((None, None, tr, c), lambda i, j, w: (w[1], i, j, 0))),
        out_shape=jax.ShapeDtypeStruct((2, a, b, c), F32),
        compiler_params=_params(("parallel", "parallel")), name=name,
    )(where, h, r, r, r)


def _join_halves(tensors, *, name):
    n = len(tensors)

    def body(*refs):
        bufs = refs[n:2 * n]
        send, recv = refs[2 * n:]
        x, y, c, _ = _place()
        cps = []
        for i in range(n):
            cp = pltpu.make_async_remote_copy(src_ref=bufs[i].at[c], dst_ref=bufs[i].at[c], send_sem=send.at[i],
                                              recv_sem=recv.at[i], device_id=(x, y, 1 - c), device_id_type=MESH)
            cp.start()
            cps.append(cp)
        for i in range(n):
            pltpu.make_async_remote_copy(src_ref=bufs[i].at[1 - c], dst_ref=bufs[i].at[1 - c], send_sem=send.at[i],
                                         recv_sem=recv.at[i], device_id=(x, y, 1 - c), device_id_type=MESH).wait_recv()
        for cp in cps:
            cp.wait_send()

    return pl.pallas_call(
        body, in_specs=[ANY] * n, out_specs=[ANY] * n,
        out_shape=[jax.ShapeDtypeStruct(t.shape, t.dtype) for t in tensors],
        scratch_shapes=[pltpu.SemaphoreType.DMA((n,)), pltpu.SemaphoreType.DMA((n,))],
        input_output_aliases={i: i for i in range(n)},
        name=name,
    )(*tensors)


BIG = (
    ("w_qkv_a", "slab", (2, 4, 1024, 576)), ("w_o_a", "col", (2, 1, 768, 1024)), ("w_q_b", "row", (2, 1, 1024, 1024)),
    ("w_o_b", "row", (2, 1, 1024, 1024)), ("w_kvf", "slab", (2, 4, 512, 516)), ("w_up", "col", (2, 2, 1024, 5632)),
    ("w_down", "row", (2, 2, 2816, 1024)),
)
SMALL_W = 1792
SMALL_ROWS = 8


def _own_in_place(shard, kind, full_shape, chip):
    two, a, b, c = full_shape
    buf = lax.empty(full_shape, shard.dtype)
    if kind == "col":
        return lax.dynamic_update_slice(buf, shard.reshape(two, a, b, c // N_CHIPS), (0, 0, 0, chip * (c // N_CHIPS)))
    if kind == "row":
        return lax.dynamic_update_slice(buf, shard.reshape(two, a, b // N_CHIPS, c), (0, 0, chip * (b // N_CHIPS), 0))
    return lax.dynamic_update_slice(buf, shard.reshape(two, 1, b, c), (0, chip, 0, 0))


def _headsum_matrix():
    r = lax.broadcasted_iota(jnp.int32, (GW, GW), 0) // HD
    c = lax.broadcasted_iota(jnp.int32, (GW, GW), 1) // HD
    return jnp.where(r == c, 1.0, 0.0).astype(BF16)


def kernel(x, norm_gains, w_qkv_a, w_o_a, w_q_b, w_o_b, kv_norm, w_kvf, b_f, w_up, conv_w, conv_b, w_down, loss_target, m_norm_gains, m_w_qkv_a, m_w_o_a, m_w_q_b, m_w_o_b, m_kv_norm, m_w_kvf, m_b_f, m_w_up, m_conv_w, m_conv_b, m_w_down, v_norm_gains, v_w_qkv_a, v_w_o_a, v_w_q_b, v_w_o_b, v_kv_norm, v_w_kvf, v_b_f, v_w_up, v_conv_w, v_conv_b, v_w_down):
    xi, yi, ci = lax.axis_index("x"), lax.axis_index("y"), lax.axis_index("c")
    chip = 2 * xi + yi
    where = jnp.stack([chip, ci]).astype(jnp.int32)
    ws = dict(norm_gains=norm_gains, w_qkv_a=w_qkv_a, w_o_a=w_o_a, w_q_b=w_q_b, w_o_b=w_o_b, kv_norm=kv_norm, w_kvf=w_kvf,
              b_f=b_f, w_up=w_up, conv_w=conv_w, conv_b=conv_b, w_down=w_down)
    ms = dict(norm_gains=m_norm_gains, w_qkv_a=m_w_qkv_a, w_o_a=m_w_o_a, w_q_b=m_w_q_b, w_o_b=m_w_o_b, kv_norm=m_kv_norm,
              w_kvf=m_w_kvf, b_f=m_b_f, w_up=m_w_up, conv_w=m_conv_w, conv_b=m_conv_b, w_down=m_w_down)
    vs = dict(norm_gains=v_norm_gains, w_qkv_a=v_w_qkv_a, w_o_a=v_w_o_a, w_q_b=v_w_q_b, w_o_b=v_w_o_b, kv_norm=v_kv_norm,
              w_kvf=v_w_kvf, b_f=v_b_f, w_up=v_w_up, conv_w=v_conv_w, conv_b=v_conv_b, w_down=v_w_down)

    small = jnp.concatenate([
        jnp.pad(norm_gains.reshape(16, 256), ((0, 0), (0, 1408 - 256))),
        jnp.pad(conv_w.reshape(12, 1408), ((0, 4), (0, 0)))], axis=0)
    bufs = [_own_in_place(ws[nm].astype(BF16), kind, shp, chip) for nm, kind, shp in BIG]
    bufs.append(_own_in_place(small, "slab", (2, 4, 16, 1408), chip))
    *gathered, g_small = _allgather(bufs, [k for _, k, _ in BIG] + ["slab"], name="gather_weights")
    W = {}
    for (nm, kind, shp), g in zip(BIG, gathered):
        if nm == "w_qkv_a":
            W[nm] = g.transpose(0, 2, 1, 3).reshape(2, D, 3 * A_W)
        elif nm == "w_kvf":
            W[nm] = g.transpose(0, 2, 1, 3).reshape(D, 2 * D + 16)
        else:
            W[nm] = g.reshape((shp[0] * shp[1],) + shp[2:])
    gains = g_small[0, :, :, :256].transpose(1, 0, 2).reshape(DEPTH, 4, 1, D)
    cw_full = g_small[1, :, :12, :].transpose(1, 0, 2).reshape(DEPTH, 3, 2 * D_FF)
    cb_full = conv_b.reshape(DEPTH, 1, 2 * D_FF)

    sq, dh, full_grads, small_flat = _fwd_bwd(x[0], loss_target[0], W, gains, cw_full, cb_full, kv_norm, b_f)
    loss = lax.psum(sq[0, 0] * (0.5 / D), ("x", "y", "c"))
    return _reduce_update(loss, dh[None], full_grads, small_flat, chip, where, ws, ms, vs)


def _fwd_bwd(h, target, W, gains, cw_full, cb_full, kv_norm, b_f):
    w_kv = W["w_kvf"][:, :2 * D]
    w_kvf_pad = jnp.pad(W["w_kvf"], ((0, 0), (0, 128 - 16)))
    w_f = w_kvf_pad[:, 2 * D:]
    kvn_g = kv_norm.reshape(1, D)
    bf_pad = jnp.pad(b_f, (0, 128 - 16)).reshape(1, 128)
    tabs = _rope_tables()
    headsum = _headsum_matrix()

    saved = []
    kv = zf = c_col = c_row = kvn = h_kv = None
    for l in range(DEPTH):
        s = {"h": h}
        g = gains[l]
        xn = _rms_fwd(h, g[0], out_dtype=BF16, name="rms_in")
        s["xn"] = xn
        if l < N_A:
            qkv = _matmul(xn, W["w_qkv_a"], mode="nn", out_dtype=F32, name="mm_qkv", mnk=(T, 3 * A_W, D), tn=768,
                          b_map=_slab(l, "nn"))
            q3, k3, v3 = _rope_fwd(qkv, tabs)
            qp, kp, vp = _perm(q3), _perm(k3), _perm(v3)
            o_p, lse_p = _band_fwd(qp, kp, vp)
            o3, lse3 = _unperm(o_p), _unperm(lse_p)
            att = _combine_fwd(o3, lse3)
            s.update(qp=qp, kp=kp, vp=vp, o3=o3, lse3=lse3, lse_p=lse_p, att=att)
            mix = _matmul(att, W["w_o_a"], mode="nn", out_dtype=F32, name="mm_oa", mnk=(T, D, A_W), b_map=_slab(l, "nn"))
        else:
            j = l - N_A
            if l == N_A:
                h_kv = h
                kvn = _rms_fwd(h, kvn_g, out_dtype=BF16, name="rms_in")
                kv = _matmul(kvn, w_kv, mode="nn", out_dtype=BF16, name="mm_kv")
                zf = _matmul(kvn, w_f, mode="nn", out_dtype=F32, name="mm_f")
                cum = _gates_fwd(zf, bf_pad)[:, :16]
                c_col = cum.reshape(T, 8, 2).transpose(1, 0, 2)
                c_row = cum.T.reshape(8, 2, T)
            q = _matmul(xn, W["w_q_b"], mode="nn", out_dtype=BF16, name="mm_qb", mnk=(T, D, D), alpha=HD ** -0.5,
                        b_map=_slab(j, "nn"))
            o = _fox_fwd(q, kv, c_col, c_row)
            s.update(q=q, o=o)
            mix = _matmul(o, W["w_o_b"], mode="nn", out_dtype=F32, name="mm_ob", mnk=(T, D, D), b_map=_slab(j, "nn"))
        s["mix"] = mix
        h1 = _rms_fwd(mix, g[1], res=h, out_dtype=F32, name="rms_res")
        xn2 = _rms_fwd(h1, g[2], out_dtype=BF16, name="rms_in")
        a = _matmul(xn2, W["w_up"], mode="nn", out_dtype=F32, name="mm_up", mnk=(T, 2 * D_FF, D), b_map=_slab(l, "nn"))
        u = _convgate_fwd(a, cw_full[l], cb_full[l])
        f = _matmul(u, W["w_down"], mode="nn", out_dtype=F32, name="mm_down", mnk=(T, D, D_FF), tm=1024, tk=D_FF, b_map=_slab(l, "nn"))
        h = _rms_fwd(f, g[3], res=h1, out_dtype=F32, name="rms_res")
        s.update(h1=h1, xn2=xn2, a=a, u=u, f=f)
        saved.append(s)

    dh, sq = _loss_head(h, target)

    gw = dict(w_qkv_a=None, w_o_a=None, w_q_b=None, w_o_b=None, w_up=None, w_down=None)
    d_gains = [[None] * 4 for _ in range(DEPTH)]
    d_cw, d_cb = [None] * DEPTH, [None] * DEPTH
    zeros_td = jnp.zeros((T, D), F32)
    fox_acc = (zeros_td, zeros_td, jnp.zeros((D // 128, T, 128), F32), jnp.zeros((D // 128, 8, T), F32))
    d_kvf = d_kvnorm = d_bf = None

    def dw(nm, slab, slabs, a, b, **kw):
        gw[nm] = _matmul(a, b, mode="tn", out_dtype=BF16, name="mm_dw_" + nm, out_slab=slab, out_slabs=slabs, out_buf=gw[nm], **kw)

    for l in reversed(range(DEPTH)):
        s = saved[l]
        g = gains[l]
        df, d_gains[l][3] = _rms_bwd(dh, s["f"], g[3], out_dtype=BF16, name="rms_bwd")
        du = _matmul(df, W["w_down"], mode="nt", out_dtype=F32, name="mm_down_dx", mnk=(T, D_FF, D), tn=256, b_map=_slab(l, "nt"))
        dw("w_down", l, DEPTH, s["u"], df, tm=1408, tn=1024)
        da, d_cw[l], d_cb[l] = _convgate_bwd(s["a"], du, cw_full[l], cb_full[l])
        dxn2 = _matmul(da, W["w_up"], mode="nt", out_dtype=F32, name="mm_up_dx", mnk=(T, D, 2 * D_FF), tm=1024, tn=1024, tk=1408,
                       a_map=_halves_a, b_map=_slab(l, "nt"))
        dw("w_up", l, DEPTH, s["xn2"], da, mnk=(D, 2 * D_FF, T), tn=1408, b_map=_halves_b)
        dh1, d_gains[l][2] = _rms_bwd(dxn2, s["h1"], g[2], dres=dh, out_dtype=F32, name="rms_bwd_res")
        dmix, d_gains[l][1] = _rms_bwd(dh1, s["mix"], g[1], out_dtype=BF16, name="rms_bwd")
        if l < N_A:
            datt = _matmul(dmix, W["w_o_a"], mode="nt", out_dtype=F32, name="mm_oa_dx", mnk=(T, A_W, D), tn=768, b_map=_slab(l, "nt"))
            dw("w_o_a", l, N_A, s["att"], dmix, tm=768, tn=1024)
            do3, dlt3 = _combine_bwd(datt, s["o3"], s["lse3"], headsum)
            dqp, dkp, dvp = _band_bwd(s["qp"], s["kp"], s["vp"], _perm(do3), s["lse_p"], _perm(dlt3))
            dqkv = _rope_bwd(_unperm(dqp), _unperm(dkp), _unperm(dvp), tabs)
            dxn = _matmul(dqkv, W["w_qkv_a"], mode="nt", out_dtype=F32, name="mm_qkv_dx", mnk=(T, D, 3 * A_W), tm=1024, tn=1024, tk=3 * A_W,
                          b_map=_slab(l, "nt"))
            dw("w_qkv_a", l, N_A, s["xn"], dqkv, tn=768)
        else:
            j = l - N_A
            do = _matmul(dmix, W["w_o_b"], mode="nt", out_dtype=BF16, name="mm_ob_dx", mnk=(T, D, D), b_map=_slab(j, "nt"))
            dw("w_o_b", j, DEPTH - N_A, s["o"], dmix, tn=1024)
            dq, *fox_acc = _fox_bwd(s["q"], kv, do, c_col, c_row, fox_acc)
            dxn = _matmul(dq, W["w_q_b"], mode="nt", out_dtype=F32, name="mm_qb_dx", mnk=(T, D, D), b_map=_slab(j, "nt"))
            dw("w_q_b", j, DEPTH - N_A, s["xn"], dq, tn=1024)
        dh, d_gains[l][0] = _rms_bwd(dxn, s["h"], g[0], dres=dh1, out_dtype=F32, name="rms_bwd_res")
        if l == N_A:
            dk, dv, dcq, dck = fox_acc
            dc16 = dcq[:, :, :2].transpose(1, 0, 2).reshape(T, 16) - dck[:, :2, :].reshape(16, T).T
            dzf, d_bf = _gates_bwd(jnp.pad(dc16, ((0, 0), (0, 128 - 16))), zf, bf_pad)
            dkvf = jnp.concatenate([dk.astype(BF16), dv.astype(BF16), dzf], axis=1)
            d_kvf = _matmul(kvn, dkvf, mode="tn", out_dtype=BF16, name="mm_kvf_dw", tm=512, tn=2 * D + 128)[:, :2 * D + 16]
            dkvn = _matmul(dkvf, w_kvf_pad, mode="nt", out_dtype=F32, name="mm_kvf_dx", tm=1024, tn=1024, tk=2 * D + 128)
            dh, d_kvnorm = _rms_bwd(dkvn, h_kv, kvn_g, dres=dh, out_dtype=F32, name="rms_bwd_res")
    full_grads = dict(gw, w_kvf=d_kvf)
    small_flat = jnp.concatenate([
        jnp.stack([jnp.stack(r) for r in d_gains]).reshape(-1),
        jnp.stack(d_cw).transpose(0, 2, 1, 3).reshape(-1),
        jnp.stack(d_cb).reshape(-1),
        d_kvnorm.reshape(-1), d_bf[0, :16]])
    return sq, dh, full_grads, small_flat


def _reduce_update(loss, grad_x, full_grads, small_flat, chip, where, ws, ms, vs):
    parts, kinds = [], []
    for nm, kind, shp in BIG:
        g = full_grads[nm]
        if nm == "w_qkv_a":
            g = g.reshape(2, D, N_CHIPS, 576).transpose(0, 2, 1, 3)
        elif nm == "w_kvf":
            g = g.reshape(2, 512, N_CHIPS, 516).transpose(0, 2, 1, 3)
        parts.append(g.reshape(shp))
        kinds.append(kind)
    n_small = small_flat.shape[0]
    gs = jnp.pad(small_flat, (0, 2 * N_CHIPS * SMALL_ROWS * SMALL_W - n_small))
    parts.append(gs.reshape(N_CHIPS, 2, SMALL_ROWS, SMALL_W).transpose(1, 0, 2, 3))
    kinds.append("slab")
    names = [nm for nm, _, _ in BIG] + ["small"]

    sib = _swap_halves(parts, name="reduce_pair_swap")
    halves = [_add_half(g, p, where, name="reduce_pair_add_" + nm) for g, p, nm in zip(parts, sib, names)]
    landed = _scatter_chips(halves, kinds, name="reduce_chip_scatter")
    sums = [_sum_chips(r, h, k, where, name="reduce_chip_sum_" + nm) for r, h, k, nm in zip(landed, halves, kinds, names)]
    *reduced, red_s = _join_halves(sums, name="reduce_pair_join")
    buf_s = lax.dynamic_update_slice(jnp.zeros((2, N_CHIPS, SMALL_ROWS, SMALL_W), F32), red_s.reshape(2, 1, SMALL_ROWS, SMALL_W),
                                     (0, chip, 0, 0))
    (all_s,) = _allgather([buf_s], ["slab"], name="gather_small_grads")
    sflat = all_s.transpose(1, 0, 2, 3).reshape(-1)

    grads = {nm: r.reshape(ws[nm].shape) for (nm, _, _), r in zip(BIG, reduced)}
    o = 0
    g_gains_full = sflat[o:o + 16 * D].reshape(DEPTH, 4, D); o += 16 * D
    g_cw_full = sflat[o:o + 12 * 2 * D_FF].reshape(DEPTH, 3, 2 * D_FF); o += 12 * 2 * D_FF
    grads["conv_b"] = sflat[o:o + 4 * 2 * D_FF].reshape(DEPTH, 2 * D_FF); o += 4 * 2 * D_FF
    grads["kv_norm"] = sflat[o:o + D]; o += D
    grads["b_f"] = sflat[o:o + 16]
    grads["norm_gains"] = lax.dynamic_slice_in_dim(g_gains_full, chip * 256, 256, axis=2)
    grads["conv_w"] = lax.dynamic_slice_in_dim(g_cw_full, chip * 1408, 1408, axis=2)

    names = ["norm_gains", "w_qkv_a", "w_o_a", "w_q_b", "w_o_b", "kv_norm", "w_kvf", "b_f", "w_up", "conv_w", "conv_b", "w_down"]
    deltas, new_m, new_v = {}, {}, {}
    for nm in names:
        shp = ws[nm].shape
        two = (math.prod(shp[:-1]), shp[-1]) if len(shp) > 1 else (1, shp[0])
        d, m2, v2 = _adamw(ws[nm].reshape(two), ms[nm].reshape(two), vs[nm].reshape(two), grads[nm].reshape(two),
                           name="adamw_" + nm)
        deltas[nm], new_m[nm], new_v[nm] = d.reshape(shp), m2.reshape(shp), v2.reshape(shp)

    return (loss, grad_x, *[grads[nm] for nm in names], *[deltas[nm] for nm in names],
            *[new_m[nm] for nm in names], *[new_v[nm] for nm in names])
```

```python
import math

import jax
import jax.numpy as jnp
from jax import lax
from jax.experimental import pallas as pl
from jax.experimental.pallas import tpu as pltpu
from jax.experimental.pallas import tpu_sc as plsc

F32 = jnp.float32
BF16 = jnp.bfloat16
MESH = pl.DeviceIdType.MESH
ANY = pl.BlockSpec(memory_space=pl.ANY)

T = 2048
D = 1024
HD = 64
DEPTH = 4
N_A = 2
A_W = 768
GW = 256
DIL = (1, 4, 16)
BLK = 128
D_FF = 2816
ROPE_THETA = 500000.0
EPS = 1e-6
NEG = -1e30
N_CHIPS = 4
FQ = 256
CT = 128
VMEM_BIG = 48 * 1024 * 1024

ADAM_LR, ADAM_B1, ADAM_B2, ADAM_EPS, ADAM_WD, ADAM_STEP = 0.001, 0.9, 0.999, 1e-08, 0.01, 10

NN = (((1,), (0,)), ((), ()))
NT = (((1,), (1,)), ((), ()))
TN = (((0,), (0,)), ((), ()))


def _dot(a, b, dims):
    return lax.dot_general(a, b, dims, preferred_element_type=F32)


def _pick(dim, pref):
    if dim <= pref:
        return dim
    best = None
    for t in range(128, pref + 1, 128):
        if dim % t == 0:
            best = t
    assert best is not None, (dim, pref)
    return best


def _params(sem=None, vmem=None):
    kw = {}
    if sem is not None:
        kw["dimension_semantics"] = sem
    if vmem is not None:
        kw["vmem_limit_bytes"] = vmem
    return pltpu.CompilerParams(**kw)


def _matmul(a, b, *, mode, out_dtype, name, mnk=None, alpha=None, tm=2048, tn=512, tk=2048,
            a_map=None, b_map=None, acc_init=None, out_slab=None, out_slabs=None, out_buf=None):
    if mnk is not None:
        M, N, K = mnk
    elif mode == "nn":
        (M, K), (_, N) = a.shape, b.shape
    elif mode == "nt":
        (M, K), (N, _) = a.shape, b.shape
    else:
        (K, M), (_, N) = a.shape, b.shape
    tm, tn, tk = _pick(M, tm), _pick(N, tn), _pick(K, tk)
    nk = K // tk
    dims = {"nn": NN, "nt": NT, "tn": TN}[mode]
    n_in = 2 + (acc_init is not None) + (out_buf is not None)

    def body(*refs):
        a_ref, b_ref = refs[0], refs[1]
        o_ref = refs[n_in]
        k = pl.program_id(2)

        def finish(r):
            if alpha is not None:
                r = r * alpha
            o_ref[...] = r.astype(out_dtype)

        def product():
            r = _dot(a_ref[...], b_ref[...], dims)
            return r if acc_init is None else r + refs[2][...]

        if nk == 1:
            finish(product())
            return
        acc_ref = refs[n_in + 1]

        @pl.when(k == 0)
        def _():
            acc_ref[...] = product()

        @pl.when((k > 0) & (k < nk - 1))
        def _():
            acc_ref[...] += _dot(a_ref[...], b_ref[...], dims)

        @pl.when(k == nk - 1)
        def _():
            finish(acc_ref[...] + _dot(a_ref[...], b_ref[...], dims))

    a_blk = (tk, tm) if mode == "tn" else (tm, tk)
    b_blk = (tn, tk) if mode == "nt" else (tk, tn)
    if a_map is not None:
        a_spec = pl.BlockSpec((None,) + a_blk, a_map(tm, tn, tk))
    elif mode == "tn":
        a_spec = pl.BlockSpec(a_blk, lambda i, j, k: (k, i))
    else:
        a_spec = pl.BlockSpec(a_blk, lambda i, j, k: (i, k))
    if b_map is not None:
        b_spec = pl.BlockSpec((None,) + b_blk, b_map(tm, tn, tk))
    elif mode == "nt":
        b_spec = pl.BlockSpec(b_blk, lambda i, j, k: (j, k))
    else:
        b_spec = pl.BlockSpec(b_blk, lambda i, j, k: (k, j))
    ins, specs, alias = [a, b], [a_spec, b_spec], {}
    if acc_init is not None:
        ins.append(acc_init)
        specs.append(pl.BlockSpec((tm, tn), lambda i, j, k: (i, j)))
    if out_buf is not None:
        alias = {len(ins): 0}
        ins.append(out_buf)
        specs.append(ANY)
    if out_slab is None:
        o_spec = pl.BlockSpec((tm, tn), lambda i, j, k: (i, j))
        o_shape = jax.ShapeDtypeStruct((M, N), out_dtype)
    else:
        o_spec = pl.BlockSpec((None, tm, tn), lambda i, j, k: (out_slab, i, j))
        o_shape = jax.ShapeDtypeStruct((out_slabs, M, N), out_dtype)
    return pl.pallas_call(
        body,
        grid=(M // tm, N // tn, nk),
        in_specs=specs,
        out_specs=o_spec,
        out_shape=o_shape,
        scratch_shapes=[pltpu.VMEM((tm, tn), F32)] if nk > 1 else [],
        input_output_aliases=alias,
        compiler_params=_params(("parallel", "parallel", "arbitrary"), VMEM_BIG),
        name=name,
    )(*ins)


def _slab(l, mode):
    if mode == "nt":
        return lambda tm, tn, tk: (lambda i, j, k: (l, j, k))
    return lambda tm, tn, tk: (lambda i, j, k: (l, k, j))


def _rms_fwd(x, g, *, out_dtype, name, res=None, tr=256):
    n, d = x.shape

    def body(*refs):
        x_ref, g_ref = refs[0], refs[1]
        o_ref = refs[-1]
        xv = x_ref[...].astype(F32)
        y = xv * lax.rsqrt(jnp.mean(xv * xv, axis=-1, keepdims=True) + EPS) * g_ref[...]
        if res is not None:
            y = y + refs[2][...]
        o_ref[...] = y.astype(out_dtype)

    row = pl.BlockSpec((tr, d), lambda i: (i, 0))
    vec = pl.BlockSpec((1, d), lambda i: (0, 0))
    ins = [x, g] + ([] if res is None else [res])
    specs = [row, vec] + ([] if res is None else [row])
    return pl.pallas_call(
        body, grid=(n // tr,), in_specs=specs, out_specs=row,
        out_shape=jax.ShapeDtypeStruct((n, d), out_dtype),
        compiler_params=_params(("parallel",)), name=name,
    )(*ins)


def _rms_bwd(dy, x, g, *, out_dtype, name, dres=None, tr=256):
    n, d = x.shape

    def body(*refs):
        dy_ref, x_ref, g_ref = refs[0], refs[1], refs[2]
        dx_ref, dg_ref = refs[-2], refs[-1]
        xv = x_ref[...].astype(F32)
        dyv = dy_ref[...].astype(F32)
        rstd = lax.rsqrt(jnp.mean(xv * xv, axis=-1, keepdims=True) + EPS)
        xhat = xv * rstd
        dxh = dyv * g_ref[...]
        dx = rstd * (dxh - xhat * jnp.mean(dxh * xhat, axis=-1, keepdims=True))
        if dres is not None:
            dx = dx + refs[3][...]
        dx_ref[...] = dx.astype(out_dtype)

        @pl.when(pl.program_id(0) == 0)
        def _():
            dg_ref[...] = jnp.zeros_like(dg_ref)

        dg_ref[...] += jnp.sum(dyv * xhat, axis=0, keepdims=True)

    row = pl.BlockSpec((tr, d), lambda i: (i, 0))
    vec = pl.BlockSpec((1, d), lambda i: (0, 0))
    ins = [dy, x, g] + ([] if dres is None else [dres])
    specs = [row, row, vec] + ([] if dres is None else [row])
    return pl.pallas_call(
        body, grid=(n // tr,), in_specs=specs, out_specs=[row, vec],
        out_shape=[jax.ShapeDtypeStruct((n, d), out_dtype), jax.ShapeDtypeStruct((1, d), F32)],
        compiler_params=_params(("arbitrary",)), name=name,
    )(*ins)


def _loss_head(h, target, *, tr=256):
    n, d = h.shape

    def body(h_ref, t_ref, dh_ref, s_ref):
        err = h_ref[...] - t_ref[...]
        dh_ref[...] = err * (1.0 / d)

        @pl.when(pl.program_id(0) == 0)
        def _():
            s_ref[...] = jnp.zeros_like(s_ref)

        s_ref[...] += jnp.sum(err * err)

    row = pl.BlockSpec((tr, d), lambda i: (i, 0))
    acc = pl.BlockSpec((8, 128), lambda i: (0, 0))
    return pl.pallas_call(
        body, grid=(n // tr,), in_specs=[row, row], out_specs=[row, acc],
        out_shape=[jax.ShapeDtypeStruct((n, d), F32), jax.ShapeDtypeStruct((8, 128), F32)],
        compiler_params=_params(("arbitrary",)), name="loss_head",
    )(h, target)


def _rope_tables():
    pos = jnp.arange(T, dtype=F32)
    inv = ROPE_THETA ** (-jnp.arange(0, 16, 2, dtype=F32) / 16)
    ang = pos[:, None] * inv[None, :]
    cos, sin = jnp.cos(ang), jnp.sin(ang)
    one = jnp.ones((T, HD - 16), F32)
    zero8 = jnp.zeros((T, 8), F32)
    zero = jnp.zeros((T, HD - 16), F32)
    c = jnp.concatenate([cos, cos, one], axis=1)
    s1 = jnp.concatenate([zero8, sin, zero], axis=1)
    s2 = jnp.concatenate([-sin, zero8, zero], axis=1)
    return tuple(jnp.concatenate([t, t], axis=1) for t in (c, s1, s2))


def _rope_fwd(qkv, tabs, *, tr=256):
    def body(x_ref, c_ref, s1_ref, s2_ref, q_ref, k_ref, v_ref):
        c, s1, s2 = c_ref[...], s1_ref[...], s2_ref[...]
        for which, o_ref, scale in ((0, q_ref, HD ** -0.5), (1, k_ref, None)):
            for j in range(A_W // 128):
                x = x_ref[:, which * A_W + j * 128: which * A_W + (j + 1) * 128]
                y = x * c + pltpu.roll(x, 8, 1) * s1 + pltpu.roll(x, 120, 1) * s2
                if scale is not None:
                    y = y * scale
                o_ref[j // 2, :, (j % 2) * 128:(j % 2 + 1) * 128] = y.astype(BF16)
        for j in range(A_W // 128):
            v_ref[j // 2, :, (j % 2) * 128:(j % 2 + 1) * 128] = x_ref[:, 2 * A_W + j * 128: 2 * A_W + (j + 1) * 128].astype(BF16)

    tab = pl.BlockSpec((tr, 128), lambda i: (i, 0))
    out = pl.BlockSpec((3, tr, GW), lambda i: (0, i, 0))
    shp = jax.ShapeDtypeStruct((3, T, GW), BF16)
    return pl.pallas_call(
        body, grid=(T // tr,), in_specs=[pl.BlockSpec((tr, 3 * A_W), lambda i: (i, 0)), tab, tab, tab],
        out_specs=[out, out, out], out_shape=[shp, shp, shp],
        compiler_params=_params(("parallel",)), name="rope_fwd",
    )(qkv, *tabs)


def _rope_bwd(dq, dk, dv, tabs, *, tr=256):
    def body(dq_ref, dk_ref, dv_ref, c_ref, s1_ref, s2_ref, o_ref):
        c, s1, s2 = c_ref[...], s1_ref[...], s2_ref[...]
        for which, i_ref, scale in ((0, dq_ref, HD ** -0.5), (1, dk_ref, None)):
            for j in range(A_W // 128):
                g = i_ref[j // 2, :, (j % 2) * 128:(j % 2 + 1) * 128]
                y = g * c + pltpu.roll(g * s1, 120, 1) + pltpu.roll(g * s2, 8, 1)
                if scale is not None:
                    y = y * scale
                o_ref[:, which * A_W + j * 128: which * A_W + (j + 1) * 128] = y.astype(BF16)
        for j in range(A_W // 128):
            o_ref[:, 2 * A_W + j * 128: 2 * A_W + (j + 1) * 128] = dv_ref[j // 2, :, (j % 2) * 128:(j % 2 + 1) * 128].astype(BF16)

    tab = pl.BlockSpec((tr, 128), lambda i: (i, 0))
    cot = pl.BlockSpec((3, tr, GW), lambda i: (0, i, 0))
    return pl.pallas_call(
        body, grid=(T // tr,), in_specs=[cot, cot, cot, tab, tab, tab],
        out_specs=pl.BlockSpec((tr, 3 * A_W), lambda i: (i, 0)),
        out_shape=jax.ShapeDtypeStruct((T, 3 * A_W), BF16),
        compiler_params=_params(("parallel",)), name="rope_bwd",
    )(dq, dk, dv, *tabs)


def _perm(x3):
    out = [x3[0]]
    for g in (1, 2):
        r = DIL[g]
        out.append(x3[g].reshape(T // r, r, GW).transpose(1, 0, 2).reshape(T, GW))
    return jnp.stack(out)


def _unperm(x3):
    out = [x3[0]]
    for g in (1, 2):
        r = DIL[g]
        out.append(x3[g].reshape(r, T // r, GW).transpose(1, 0, 2).reshape(T, GW))
    return jnp.stack(out)


def _head_mask(x, lane_lo):
    lane = lax.broadcasted_iota(jnp.int32, x.shape, 1)
    keep = (lane < HD) if lane_lo else (lane >= HD)
    return jnp.where(keep, x.astype(F32), 0.0).astype(BF16)


def _band_scalars():
    g, b = pl.program_id(0), pl.program_id(1)
    nbs = lax.shift_right_logical(jnp.int32(T // BLK), 2 * g)
    has_prev = jnp.where((b & (nbs - 1)) != 0, 1, 0)
    next_ok = jnp.where(((b + 1) & (nbs - 1)) != 0, 1, 0)
    return has_prev, next_ok


def _band_mask_q(has_prev):
    row = lax.broadcasted_iota(jnp.int32, (BLK, 2 * BLK), 0)
    col = lax.broadcasted_iota(jnp.int32, (BLK, 2 * BLK), 1)
    return ((col < BLK) & (col >= row) & (has_prev == 1)) | ((col >= BLK) & (col - BLK <= row))


def _band_mask_k(next_ok):
    row = lax.broadcasted_iota(jnp.int32, (2 * BLK, BLK), 0)
    col = lax.broadcasted_iota(jnp.int32, (2 * BLK, BLK), 1)
    return ((row < BLK) & (col <= row)) | ((row >= BLK) & (col >= row - BLK) & (next_ok == 1))


def _band_fwd(q, k, v):
    nb = T // BLK

    def body(q_ref, kc_ref, kp_ref, vc_ref, vp_ref, o_ref, l_ref):
        has_prev, _ = _band_scalars()
        mask = _band_mask_q(has_prev)
        lane = lax.broadcasted_iota(jnp.int32, (BLK, 128), 1)
        for p in range(2):
            sl = slice(128 * p, 128 * (p + 1))
            qp = q_ref[0, :, sl]
            kcat = jnp.concatenate([kp_ref[0, :, sl], kc_ref[0, :, sl]], axis=0)
            vcat = jnp.concatenate([vp_ref[0, :, sl], vc_ref[0, :, sl]], axis=0)
            o_acc = jnp.zeros((BLK, 128), F32)
            lse = jnp.zeros((BLK, 128), F32)
            for e in range(2):
                s = _dot(_head_mask(qp, e == 0), kcat, NT)
                s = jnp.where(mask, s, NEG)
                m = jnp.max(s, axis=-1, keepdims=True)
                pr = jnp.exp(s - m)
                l = jnp.sum(pr, axis=-1, keepdims=True)
                o_acc = o_acc + _dot(pr.astype(BF16), _head_mask(vcat, e == 0), NN) / l
                lse = jnp.where((lane < HD) if e == 0 else (lane >= HD), m + jnp.log(l), lse)
            o_ref[0, :, sl] = o_acc
            l_ref[0, :, sl] = lse

    cur = pl.BlockSpec((1, BLK, GW), lambda g, b: (g, b, 0))
    prev = pl.BlockSpec((1, BLK, GW), lambda g, b: (g, jnp.maximum(b - 1, 0), 0))
    shp = jax.ShapeDtypeStruct((3, T, GW), F32)
    return pl.pallas_call(
        body, grid=(3, nb), in_specs=[cur, cur, prev, cur, prev], out_specs=[cur, cur], out_shape=[shp, shp],
        compiler_params=_params(("parallel", "parallel")), name="band_fwd",
    )(q, k, k, v, v)


def _band_bwd(q, k, v, do, lse, dlt):
    nb = T // BLK

    def body(qc_ref, qn_ref, kc_ref, kp_ref, vc_ref, vp_ref, doc_ref, don_ref, lc_ref, ln_ref, dc_ref, dn_ref,
             dq_ref, dk_ref, dv_ref):
        has_prev, next_ok = _band_scalars()
        mask_q = _band_mask_q(has_prev)
        mask_k = _band_mask_k(next_ok)
        for p in range(2):
            sl = slice(128 * p, 128 * (p + 1))
            qc, qn = qc_ref[0, :, sl], qn_ref[0, :, sl]
            doc, don = doc_ref[0, :, sl], don_ref[0, :, sl]
            kc, vc = kc_ref[0, :, sl], vc_ref[0, :, sl]
            kcat = jnp.concatenate([kp_ref[0, :, sl], kc], axis=0)
            vcat = jnp.concatenate([vp_ref[0, :, sl], vc], axis=0)
            qcat = jnp.concatenate([qc, qn], axis=0)
            docat = jnp.concatenate([doc, don], axis=0)
            dq = jnp.zeros((BLK, 128), F32)
            dk = jnp.zeros((BLK, 128), F32)
            dv = jnp.zeros((BLK, 128), F32)
            for e in range(2):
                lo = e == 0
                col = slice(128 * p + HD * e, 128 * p + HD * e + 1)
                lse_c, lse_n = lc_ref[0, :, col], ln_ref[0, :, col]
                dl_c, dl_n = dc_ref[0, :, col], dn_ref[0, :, col]
                s = jnp.where(mask_q, _dot(_head_mask(qc, lo), kcat, NT), NEG)
                pr = jnp.exp(s - lse_c)
                dp = _dot(_head_mask(doc, lo), vcat, NT)
                ds = pr * (dp - dl_c)
                dq = dq + _dot(ds.astype(BF16), _head_mask(kcat, lo), NN)
                qm, dom = _head_mask(qcat, lo), _head_mask(docat, lo)
                s2 = jnp.where(mask_k, _dot(qm, kc, NT), NEG)
                p2 = jnp.exp(s2 - jnp.concatenate([lse_c, lse_n], axis=0))
                dv = dv + _dot(p2.astype(BF16), dom, TN)
                dp2 = _dot(dom, vc, NT)
                ds2 = p2 * (dp2 - jnp.concatenate([dl_c, dl_n], axis=0))
                dk = dk + _dot(ds2.astype(BF16), qm, TN)
            dq_ref[0, :, sl] = dq
            dk_ref[0, :, sl] = dk
            dv_ref[0, :, sl] = dv

    cur = pl.BlockSpec((1, BLK, GW), lambda g, b: (g, b, 0))
    prev = pl.BlockSpec((1, BLK, GW), lambda g, b: (g, jnp.maximum(b - 1, 0), 0))
    nxt = pl.BlockSpec((1, BLK, GW), lambda g, b: (g, jnp.minimum(b + 1, nb - 1), 0))
    shp = jax.ShapeDtypeStruct((3, T, GW), F32)
    return pl.pallas_call(
        body, grid=(3, nb),
        in_specs=[cur, nxt, cur, prev, cur, prev, cur, nxt, cur, nxt, cur, nxt],
        out_specs=[cur, cur, cur], out_shape=[shp, shp, shp],
        compiler_params=_params(("parallel", "parallel")), name="band_bwd",
    )(q, q, k, k, v, v, do, do, lse, lse, dlt, dlt)


def _split3(x):
    hi = x.astype(BF16)
    r = x - hi.astype(F32)
    mid = r.astype(BF16)
    lo = (r - mid.astype(F32)).astype(BF16)
    return hi, mid, lo


def _dot3(x, m, dims=NN):
    hi, mid, lo = _split3(x)
    return _dot(hi, m, dims) + _dot(mid, m, dims) + _dot(lo, m, dims)


def _combine_weights(l_ref):
    l0, l1, l2 = l_ref[0], l_ref[1], l_ref[2]
    m = jnp.maximum(jnp.maximum(l0, l1), l2)
    e = [jnp.exp(l0 - m), jnp.exp(l1 - m), jnp.exp(l2 - m)]
    inv = 1.0 / (e[0] + e[1] + e[2])
    return [ei * inv for ei in e]


def _combine_fwd(o, lse, *, tr=256):
    def body(o_ref, l_ref, out_ref):
        alpha = _combine_weights(l_ref)
        for g in range(3):
            out_ref[:, g * GW:(g + 1) * GW] = (o_ref[g] * alpha[g]).astype(BF16)

    blk = pl.BlockSpec((3, tr, GW), lambda i: (0, i, 0))
    return pl.pallas_call(
        body, grid=(T // tr,), in_specs=[blk, blk], out_specs=pl.BlockSpec((tr, A_W), lambda i: (i, 0)),
        out_shape=jax.ShapeDtypeStruct((T, A_W), BF16), compiler_params=_params(("parallel",)), name="combine_fwd",
    )(o, lse)


def _combine_bwd(datt, o, lse, headsum, *, tr=256):
    def body(d_ref, o_ref, l_ref, hs_ref, do_ref, dl_ref):
        alpha = _combine_weights(l_ref)
        hs = hs_ref[...]
        total = jnp.zeros((tr, GW), F32)
        for g in range(3):
            dg = d_ref[:, g * GW:(g + 1) * GW]
            do_ref[g] = (dg * alpha[g]).astype(BF16)
            total = total + alpha[g] * _dot3(dg * o_ref[g], hs)
        for g in range(3):
            dl_ref[g] = alpha[g] * total

    blk = pl.BlockSpec((3, tr, GW), lambda i: (0, i, 0))
    return pl.pallas_call(
        body, grid=(T // tr,),
        in_specs=[pl.BlockSpec((tr, A_W), lambda i: (i, 0)), blk, blk, pl.BlockSpec((GW, GW), lambda i: (0, 0))],
        out_specs=[blk, blk],
        out_shape=[jax.ShapeDtypeStruct((3, T, GW), BF16), jax.ShapeDtypeStruct((3, T, GW), F32)],
        compiler_params=_params(("parallel",)), name="combine_bwd",
    )(datt, o, lse, headsum)


def _fox_scores(qm, k_ref, cq, ck_ref, e, i, n):
    s = _dot(qm, k_ref[0:n, :], NT) + (cq - ck_ref[0, e:e + 1, 0:n])
    row = lax.broadcasted_iota(jnp.int32, (FQ, n), 0)
    col = lax.broadcasted_iota(jnp.int32, (FQ, n), 1)
    s = jnp.where(col <= row + i * FQ, s, NEG)
    m = jnp.max(s, axis=-1, keepdims=True)
    pr = jnp.exp(s - m)
    return pr, jnp.sum(pr, axis=-1, keepdims=True)


def _fox_fwd(q, kv, c_col, c_row):
    def body(q_ref, k_ref, v_ref, cc_ref, cr_ref, o_ref, vm_ref):
        for e in range(2):
            vm_ref[e] = _head_mask(v_ref[...], e == 0)
        for i in range(T // FQ):
            n = (i + 1) * FQ
            rows = slice(i * FQ, n)
            acc = jnp.zeros((FQ, 128), F32)
            for e in range(2):
                qm = _head_mask(q_ref[rows, :], e == 0)
                pr, l = _fox_scores(qm, k_ref, cc_ref[0, rows, e:e + 1], cr_ref, e, i, n)
                acc = acc + _dot(pr.astype(BF16), vm_ref[e, 0:n, :], NN) / l
            o_ref[rows, :] = acc.astype(BF16)

    pair = pl.BlockSpec((T, 128), lambda p: (0, p))
    return pl.pallas_call(
        body, grid=(D // 128,),
        in_specs=[pair, pair, pl.BlockSpec((T, 128), lambda p: (0, D // 128 + p)),
                  pl.BlockSpec((1, T, 2), lambda p: (p, 0, 0)), pl.BlockSpec((1, 2, T), lambda p: (p, 0, 0))],
        out_specs=pair, out_shape=jax.ShapeDtypeStruct((T, D), BF16),
        scratch_shapes=[pltpu.VMEM((2, T, 128), BF16)],
        compiler_params=_params(("parallel",), VMEM_BIG), name="fox_fwd",
    )(q, kv, kv, c_col, c_row)


def _fox_bwd(q, kv, do, c_col, c_row, init):
    def body(q_ref, k_ref, v_ref, do_ref, cc_ref, cr_ref, ik_ref, iv_ref, iq_ref, ic_ref,
             dq_ref, dk_ref, dv_ref, dcq_ref, dck_ref, km_ref):
        dk_ref[...] = ik_ref[...]
        dv_ref[...] = iv_ref[...]
        dcq_ref[...] = iq_ref[...]
        dck_ref[...] = ic_ref[...]
        for e in range(2):
            km_ref[e] = _head_mask(k_ref[...], e == 0)
        for i in range(T // FQ):
            n = (i + 1) * FQ
            rows = slice(i * FQ, n)
            dq = jnp.zeros((FQ, 128), F32)
            for e in range(2):
                qm = _head_mask(q_ref[rows, :], e == 0)
                dom = _head_mask(do_ref[rows, :], e == 0)
                pr, l = _fox_scores(qm, k_ref, cc_ref[0, rows, e:e + 1], cr_ref, e, i, n)
                pr = pr / l
                dp = _dot(dom, v_ref[0:n, :], NT)
                ds = pr * (dp - jnp.sum(pr * dp, axis=-1, keepdims=True))
                dsb = ds.astype(BF16)
                dq = dq + _dot(dsb, km_ref[e, 0:n, :], NN)
                dk_ref[0:n, :] += _dot(dsb, qm, TN)
                dv_ref[0:n, :] += _dot(pr.astype(BF16), dom, TN)
                dcq_ref[0, rows, e:e + 1] += jnp.sum(ds, axis=-1, keepdims=True)
                dck_ref[0, e:e + 1, 0:n] += jnp.sum(ds, axis=0, keepdims=True)
            dq_ref[rows, :] = (dq * HD ** -0.5).astype(BF16)

    pair = pl.BlockSpec((T, 128), lambda p: (0, p))
    cq = pl.BlockSpec((1, T, 128), lambda p: (p, 0, 0))
    ck = pl.BlockSpec((1, 8, T), lambda p: (p, 0, 0))
    return pl.pallas_call(
        body, grid=(D // 128,),
        in_specs=[pair, pair, pl.BlockSpec((T, 128), lambda p: (0, D // 128 + p)), pair,
                  pl.BlockSpec((1, T, 2), lambda p: (p, 0, 0)), pl.BlockSpec((1, 2, T), lambda p: (p, 0, 0)),
                  pair, pair, cq, ck],
        out_specs=[pair, pair, pair, cq, ck],
        out_shape=[jax.ShapeDtypeStruct((T, D), BF16), jax.ShapeDtypeStruct((T, D), F32), jax.ShapeDtypeStruct((T, D), F32),
                   jax.ShapeDtypeStruct((D // 128, T, 128), F32), jax.ShapeDtypeStruct((D // 128, 8, T), F32)],
        scratch_shapes=[pltpu.VMEM((2, T, 128), BF16)],
        compiler_params=_params(("parallel",), VMEM_BIG), name="fox_bwd",
    )(q, kv, kv, do, c_col, c_row, *init)


def _tri(lower):
    r = lax.broadcasted_iota(jnp.int32, (BLK, BLK), 0)
    c = lax.broadcasted_iota(jnp.int32, (BLK, BLK), 1)
    return jnp.where((c <= r) if lower else (c >= r), 1.0, 0.0).astype(BF16)


def _gates_fwd(z, b):
    def body(z_ref, b_ref, c_ref):
        tri = _tri(True)
        carry = jnp.zeros((1, 128), F32)
        for i in range(T // BLK):
            rows = slice(i * BLK, (i + 1) * BLK)
            x = z_ref[rows, :] + b_ref[...]
            logf = jnp.minimum(x, 0.0) - jnp.log(1.0 + jnp.exp(-jnp.abs(x)))
            hi, mid, lo = _split3(logf)
            y = _dot(tri, hi, NN) + _dot(tri, mid, NN) + _dot(tri, lo, NN) + carry
            c_ref[rows, :] = y
            carry = y[BLK - 1:BLK, :]

    return pl.pallas_call(body, out_shape=jax.ShapeDtypeStruct((T, 128), F32), name="gates_fwd")(z, b)


def _gates_bwd(dc, z, b):
    def body(dc_ref, z_ref, b_ref, dz_ref, db_ref):
        tri = _tri(False)
        carry = jnp.zeros((1, 128), F32)
        db = jnp.zeros((1, 128), F32)
        for i in reversed(range(T // BLK)):
            rows = slice(i * BLK, (i + 1) * BLK)
            hi, mid, lo = _split3(dc_ref[rows, :])
            dlogf = _dot(tri, hi, NN) + _dot(tri, mid, NN) + _dot(tri, lo, NN) + carry
            carry = dlogf[0:1, :]
            x = z_ref[rows, :] + b_ref[...]
            dz = dlogf / (1.0 + jnp.exp(x))
            dz_ref[rows, :] = dz.astype(BF16)
            db = db + jnp.sum(dz, axis=0, keepdims=True)
        db_ref[...] = db

    return pl.pallas_call(
        body, out_shape=[jax.ShapeDtypeStruct((T, 128), BF16), jax.ShapeDtypeStruct((1, 128), F32)], name="gates_bwd",
    )(dc, z, b)


def _conv_pair(a_refs, cw_refs, cb_refs):
    row = lax.broadcasted_iota(jnp.int32, (T, CT), 0)
    outs = []
    for a_ref, cw_ref, cb_ref in zip(a_refs, cw_refs, cb_refs):
        z = a_ref[...]
        z1 = jnp.where(row >= 1, pltpu.roll(z, 1, 0), 0.0)
        z2 = jnp.where(row >= 2, pltpu.roll(z, 2, 0), 0.0)
        y = cw_ref[2:3, :] * z + cw_ref[1:2, :] * z1 + cw_ref[0:1, :] * z2 + cb_ref[...]
        outs.append((y, z, z1, z2))
    return outs


_GELU_K = math.sqrt(2.0 / math.pi)
N_CT = D_FF // CT


def _conv_specs():
    def at(rows, off):
        return pl.BlockSpec((rows, CT), lambda j: (0, j + off))
    return [at(T, 0), at(T, N_CT), at(3, 0), at(3, N_CT), at(1, 0), at(1, N_CT)]


def _convgate_fwd(a, cw, cb):
    def body(ag_ref, av_ref, wg_ref, wv_ref, bg_ref, bv_ref, u_ref):
        (g, _, _, _), (v, _, _, _) = _conv_pair((ag_ref, av_ref), (wg_ref, wv_ref), (bg_ref, bv_ref))
        th = jnp.tanh(_GELU_K * (g + 0.044715 * g * g * g))
        u_ref[...] = (0.5 * g * (1.0 + th) * v).astype(BF16)

    return pl.pallas_call(
        body, grid=(N_CT,), in_specs=_conv_specs(),
        out_specs=pl.BlockSpec((T, CT), lambda j: (0, j)), out_shape=jax.ShapeDtypeStruct((T, D_FF), BF16),
        compiler_params=_params(("parallel",), VMEM_BIG), name="convgate_fwd",
    )(a, a, cw, cw, cb, cb)


def _convgate_bwd(a, du, cw, cb):
    def body(ag_ref, av_ref, wg_ref, wv_ref, bg_ref, bv_ref, du_ref, da_ref, dcw_ref, dcb_ref):
        (g, gz, gz1, gz2), (v, vz, vz1, vz2) = _conv_pair((ag_ref, av_ref), (wg_ref, wv_ref), (bg_ref, bv_ref))
        du = du_ref[...].astype(F32)
        th = jnp.tanh(_GELU_K * (g + 0.044715 * g * g * g))
        gelu = 0.5 * g * (1.0 + th)
        dgelu = 0.5 * (1.0 + th) + 0.5 * g * (1.0 - th * th) * _GELU_K * (1.0 + 3 * 0.044715 * g * g)
        row = lax.broadcasted_iota(jnp.int32, (T, CT), 0)
        for h, (d, z, z1, z2, w_ref) in enumerate(((du * v * dgelu, gz, gz1, gz2, wg_ref), (du * gelu, vz, vz1, vz2, wv_ref))):
            d1 = jnp.where(row < T - 1, pltpu.roll(d, T - 1, 0), 0.0)
            d2 = jnp.where(row < T - 2, pltpu.roll(d, T - 2, 0), 0.0)
            da_ref[h] = (w_ref[2:3, :] * d + w_ref[1:2, :] * d1 + w_ref[0:1, :] * d2).astype(BF16)
            dcw_ref[h, 0:1, :] = jnp.sum(d * z2, axis=0, keepdims=True)
            dcw_ref[h, 1:2, :] = jnp.sum(d * z1, axis=0, keepdims=True)
            dcw_ref[h, 2:3, :] = jnp.sum(d * z, axis=0, keepdims=True)
            dcb_ref[h] = jnp.sum(d, axis=0, keepdims=True)

    def both(rows):
        return pl.BlockSpec((2, rows, CT), lambda j: (0, 0, j))

    return pl.pallas_call(
        body, grid=(N_CT,),
        in_specs=_conv_specs() + [pl.BlockSpec((T, CT), lambda j: (0, j))],
        out_specs=[both(T), both(3), both(1)],
        out_shape=[jax.ShapeDtypeStruct((2, T, D_FF), BF16), jax.ShapeDtypeStruct((2, 3, D_FF), F32),
                   jax.ShapeDtypeStruct((2, 1, D_FF), F32)],
        compiler_params=_params(("parallel",), VMEM_BIG), name="convgate_bwd",
    )(a, a, cw, cw, cb, cb, du)


def _halves_a(tm, tn, tk):
    per = D_FF // tk
    return lambda i, j, k: (lax.div(k, per), i, lax.rem(k, per))


def _halves_b(tm, tn, tk):
    per = D_FF // tn
    return lambda i, j, k: (lax.div(j, per), k, lax.rem(j, per))


def _adamw(w, m, v, g, *, name):
    r, c = w.shape
    tr = r
    if r * c > 256 * 1024:
        for cand in range(8, r, 8):
            if r % cand == 0 and cand * c <= 256 * 1024:
                tr = cand

    def body(w_ref, m_ref, v_ref, g_ref, d_ref, nm_ref, nv_ref):
        gv = g_ref[...]
        mn = ADAM_B1 * m_ref[...] + (1.0 - ADAM_B1) * gv
        vn = ADAM_B2 * v_ref[...] + (1.0 - ADAM_B2) * (gv * gv)
        m_hat = mn / (1.0 - ADAM_B1 ** ADAM_STEP)
        v_hat = vn / (1.0 - ADAM_B2 ** ADAM_STEP)
        d_ref[...] = -ADAM_LR * (m_hat / (jnp.sqrt(v_hat) + ADAM_EPS) + ADAM_WD * w_ref[...])
        nm_ref[...] = mn
        nv_ref[...] = vn

    blk = pl.BlockSpec((tr, c), lambda i: (i, 0))
    shp = jax.ShapeDtypeStruct((r, c), F32)
    return pl.pallas_call(
        body, grid=(r // tr,), in_specs=[blk] * 4, out_specs=[blk] * 3, out_shape=[shp] * 3,
        compiler_params=_params(("parallel",)), name=name,
    )(w, m, v, g)


def _place():
    x, y, c = lax.axis_index("x"), lax.axis_index("y"), lax.axis_index("c")
    chips = [(1 - x, y), (x, 1 - y), (1 - x, 1 - y)]
    return x, y, c, chips


def _window(ref, kind, s, half=None):
    lead = () if half is None else (half,)
    b, c = ref.shape[-2], ref.shape[-1]
    if kind == "col":
        return ref.at[lead + (slice(None), slice(None), pl.ds(s * (c // N_CHIPS), c // N_CHIPS))]
    if kind == "row":
        return ref.at[lead + (slice(None), pl.ds(s * (b // N_CHIPS), b // N_CHIPS), slice(None))]
    return ref.at[lead + (s,)]


def _window_shape(shape3, kind):
    a, b, c = shape3
    return {"col": (a, b, c // N_CHIPS), "row": (a, b // N_CHIPS, c), "slab": (b, c)}[kind]


def _allgather(tensors, kinds, *, name):
    n = len(tensors)

    def body(*refs):
        bufs = refs[n:2 * n]
        send, recv = refs[2 * n:]
        x, y, c, chips = _place()
        me = 2 * x + y
        sib = (x, y, 1 - c)

        def rcopy(i, k, win, to):
            return pltpu.make_async_remote_copy(src_ref=win, dst_ref=win, send_sem=send.at[i * 6 + k], recv_sem=recv.at[i * 6 + k],
                                                device_id=to, device_id_type=MESH)

        started = []
        for i in range(n):
            for k, (px, py) in enumerate(chips):
                cp = rcopy(i, k, _window(bufs[i], kinds[i], me, c), (px, py, c))
                cp.start()
                started.append(cp)
        for i in range(n):
            for k, (px, py) in enumerate(chips):
                landed = _window(bufs[i], kinds[i], 2 * px + py, c)
                rcopy(i, k, landed, (px, py, c)).wait_recv()
                fw = rcopy(i, 3 + k, landed, sib)
                fw.start()
                started.append(fw)
        for i in range(n):
            for k, (px, py) in enumerate(chips):
                rcopy(i, 3 + k, _window(bufs[i], kinds[i], 2 * px + py, 1 - c), sib).wait_recv()
        for cp in started:
            cp.wait_send()

    return pl.pallas_call(
        body, in_specs=[ANY] * n, out_specs=[ANY] * n,
        out_shape=[jax.ShapeDtypeStruct(t.shape, t.dtype) for t in tensors],
        scratch_shapes=[pltpu.SemaphoreType.DMA((6 * n,)), pltpu.SemaphoreType.DMA((6 * n,))],
        input_output_aliases={i: i for i in range(n)},
        name=name,
    )(*tensors)


def _swap_halves(tensors, *, name):
    n = len(tensors)

    def body(*refs):
        ins, outs = refs[:n], refs[n:2 * n]
        send, recv = refs[2 * n:]
        x, y, c, _ = _place()
        cps = []
        for i in range(n):
            cp = pltpu.make_async_remote_copy(src_ref=ins[i].at[1 - c], dst_ref=outs[i], send_sem=send.at[i],
                                              recv_sem=recv.at[i], device_id=(x, y, 1 - c), device_id_type=MESH)
            cp.start()
            cps.append(cp)
        for cp in cps:
            cp.wait()

    return pl.pallas_call(
        body, in_specs=[ANY] * n, out_specs=[ANY] * n,
        out_shape=[jax.ShapeDtypeStruct(t.shape[1:], t.dtype) for t in tensors],
        scratch_shapes=[pltpu.SemaphoreType.DMA((n,)), pltpu.SemaphoreType.DMA((n,))], name=name,
    )(*tensors)


def _rows_tile(rows, cols, sub):
    best = None
    for t in range(sub, rows + 1, sub):
        if rows % t == 0 and t * cols <= 512 * 1024:
            best = t
    return rows if best is None else best


def _add_half(g, p, where, *, name):
    a, b, c = p.shape
    tr = _rows_tile(b, c, 16)

    def body(w_ref, g_ref, p_ref, o_ref):
        o_ref[...] = (g_ref[...].astype(F32) + p_ref[...].astype(F32)).astype(o_ref.dtype)

    return pl.pallas_call(
        body,
        grid_spec=pltpu.PrefetchScalarGridSpec(
            num_scalar_prefetch=1, grid=(a, b // tr),
            in_specs=[pl.BlockSpec((None, None, tr, c), lambda i, r, w: (w[1], i, r, 0)),
                      pl.BlockSpec((None, tr, c), lambda i, r, w: (i, r, 0))],
            out_specs=pl.BlockSpec((None, tr, c), lambda i, r, w: (i, r, 0))),
        out_shape=jax.ShapeDtypeStruct(p.shape, g.dtype),
        compiler_params=_params(("parallel", "parallel")), name=name,
    )(where, g, p)


def _scatter_chips(tensors, kinds, *, name):
    n = len(tensors)

    def body(*refs):
        ins, outs = refs[:n], refs[n:2 * n]
        send, recv = refs[2 * n:]
        x, y, c, chips = _place()
        cps = []
        for i in range(n):
            for k, (px, py) in enumerate(chips):
                cp = pltpu.make_async_remote_copy(src_ref=_window(ins[i], kinds[i], 2 * px + py), dst_ref=outs[i].at[k],
                                                  send_sem=send.at[i * 3 + k], recv_sem=recv.at[i * 3 + k],
                                                  device_id=(px, py, c), device_id_type=MESH)
                cp.start()
                cps.append(cp)
        for cp in cps:
            cp.wait()

    return pl.pallas_call(
        body, in_specs=[ANY] * n, out_specs=[ANY] * n,
        out_shape=[jax.ShapeDtypeStruct((3,) + _window_shape(t.shape, k), t.dtype) for t, k in zip(tensors, kinds)],
        scratch_shapes=[pltpu.SemaphoreType.DMA((3 * n,)), pltpu.SemaphoreType.DMA((3 * n,))],
        name=name,
    )(*tensors)


def _sum_chips(r, h, kind, where, *, name):
    if kind == "slab":
        r = r.reshape((3, 1) + r.shape[1:])
    _, a, b, c = r.shape
    tr = _rows_tile(b, c, 16)
    if kind == "col":
        h_spec = pl.BlockSpec((None, tr, c), lambda i, j, w: (i, j, w[0]))
    elif kind == "row":
        h_spec = pl.BlockSpec((None, tr, c), lambda i, j, w: (i, w[0] * (b // tr) + j, 0))
    else:
        h_spec = pl.BlockSpec((None, tr, c), lambda i, j, w: (w[0], j, 0))

    def body(w_ref, h_ref, r0_ref, r1_ref, r2_ref, o_ref):
        o_ref[...] = ((h_ref[...].astype(F32) + r0_ref[...].astype(F32)) + r1_ref[...].astype(F32)) + r2_ref[...].astype(F32)

    def slot(k):
        return pl.BlockSpec((None, None, tr, c), lambda i, j, w: (k, i, j, 0))

    return pl.pallas_call(
        body,
        grid_spec=pltpu.PrefetchScalarGridSpec(
            num_scalar_prefetch=1, grid=(a, b // tr),
            in_specs=[h_spec, slot(0), slot(1), slot(2)],
            out_specs=pl.BlockSpec((None, None, tr, c), lambda i, j, w: (w[1], i, j, 0))),
        out_shape=jax.ShapeDtypeStruct((2, a, b, c), F32),
        compiler_params=_params(("parallel", "parallel")), name=name,
    )(where, h, r, r, r)


def _join_halves(tensors, *, name):
    n = len(tensors)

    def body(*refs):
        bufs = refs[n:2 * n]
        send, recv = refs[2 * n:]
        x, y, c, _ = _place()
        cps = []
        for i in range(n):
            cp = pltpu.make_async_remote_copy(src_ref=bufs[i].at[c], dst_ref=bufs[i].at[c], send_sem=send.at[i],
                                              recv_sem=recv.at[i], device_id=(x, y, 1 - c), device_id_type=MESH)
            cp.start()
            cps.append(cp)
        for i in range(n):
            pltpu.make_async_remote_copy(src_ref=bufs[i].at[1 - c], dst_ref=bufs[i].at[1 - c], send_sem=send.at[i],
                                         recv_sem=recv.at[i], device_id=(x, y, 1 - c), device_id_type=MESH).wait_recv()
        for cp in cps:
            cp.wait_send()

    return pl.pallas_call(
        body, in_specs=[ANY] * n, out_specs=[ANY] * n,
        out_shape=[jax.ShapeDtypeStruct(t.shape, t.dtype) for t in tensors],
        scratch_shapes=[pltpu.SemaphoreType.DMA((n,)), pltpu.SemaphoreType.DMA((n,))],
        input_output_aliases={i: i for i in range(n)},
        name=name,
    )(*tensors)


def _win(ref, kind, s, h=None):
    if kind == "col":
        b, c = ref.shape
        cols = pl.ds(s * (c // N_CHIPS), c // N_CHIPS)
        return ref.at[:, cols] if h is None else ref.at[pl.ds(h * (b // 2), b // 2), cols]
    if kind == "row":
        b, c = ref.shape
        rows = pl.ds(s * (b // N_CHIPS), b // N_CHIPS)
        return ref.at[rows, :] if h is None else ref.at[rows, pl.ds(h * (c // 2), c // 2)]
    b = ref.shape[1]
    return ref.at[s] if h is None else ref.at[s, pl.ds(h * (b // 2), b // 2)]


def _half(ref, kind, h):
    b, c = ref.shape
    if kind == "row":
        return ref.at[:, pl.ds(h * (c // 2), c // 2)]
    return ref.at[pl.ds(h * (b // 2), b // 2), :]


def _full_shape(shard_shape, kind):
    b, c = shard_shape
    return {"col": (b, N_CHIPS * c), "row": (N_CHIPS * b, c), "slab": (N_CHIPS, b, c)}[kind]


def _gather_body(srcs, outs, kinds, send, recv):
    x, y, c, chips = _place()
    me = 2 * x + y
    sib = (x, y, 1 - c)

    def rcopy(i, k, src, dst, to):
        return pltpu.make_async_remote_copy(src_ref=src, dst_ref=dst, send_sem=send.at[7 * i + k], recv_sem=recv.at[7 * i + k],
                                            device_id=to, device_id_type=MESH)

    started = []
    for i, (src, out, kind) in enumerate(zip(srcs, outs, kinds)):
        own = rcopy(i, 6, src, _win(out, kind, me), sib)
        own.start()
        started.append(own)
        for k, (px, py) in enumerate(chips):
            cp = rcopy(i, k, _half(src, kind, c), _win(out, kind, me, c), (px, py, c))
            cp.start()
            started.append(cp)
    for i, (out, kind) in enumerate(zip(outs, kinds)):
        for k, (px, py) in enumerate(chips):
            landed = _win(out, kind, 2 * px + py, c)
            rcopy(i, k, landed, landed, (px, py, c)).wait_recv()
            fw = rcopy(i, 3 + k, landed, landed, sib)
            fw.start()
            started.append(fw)
    for i, (src, out, kind) in enumerate(zip(srcs, outs, kinds)):
        for k, (px, py) in enumerate(chips):
            other = _win(out, kind, 2 * px + py, 1 - c)
            rcopy(i, 3 + k, other, other, sib).wait_recv()
        rcopy(i, 6, src, _win(out, kind, me), sib).wait_recv()
    for cp in started:
        cp.wait_send()


def _seq_gather(shards, kinds, *, name, cid):
    n = len(shards)
    srcs = [jax.new_ref(s, memory_space=pltpu.MemorySpace.HBM) for s in shards]
    outs = [jax.empty_ref(jax.ShapeDtypeStruct(_full_shape(s.shape, k), s.dtype), memory_space=pltpu.MemorySpace.HBM)
            for s, k in zip(shards, kinds)]

    @pl.kernel(mesh=plsc.ScalarSubcoreMesh(axis_name="seq", num_cores=1), name=name,
               scratch_types=(pltpu.SemaphoreType.DMA((7 * n,)), pltpu.SemaphoreType.DMA((7 * n,))),
               compiler_params=pltpu.CompilerParams(collective_id=cid))
    def launch(send, recv):
        x, y, c, chips = _place()
        barrier = pltpu.get_barrier_semaphore()
        for px, py in chips:
            pl.semaphore_signal(barrier, inc=1, device_id=(px, py, c), device_id_type=MESH)
        pl.semaphore_signal(barrier, inc=1, device_id=(x, y, 1 - c), device_id_type=MESH)
        pl.semaphore_wait(barrier, 4)
        _gather_body(srcs, outs, kinds, send, recv)

    launch()
    return [o[...] for o in outs]


BIG = (
    ("w_qkv_a", "slab", (2, 4, 1024, 576)), ("w_o_a", "col", (2, 1, 768, 1024)), ("w_q_b", "row", (2, 1, 1024, 1024)),
    ("w_o_b", "row", (2, 1, 1024, 1024)), ("w_kvf", "slab", (2, 4, 512, 516)), ("w_up", "col", (2, 2, 1024, 5632)),
    ("w_down", "row", (2, 2, 2816, 1024)),
)
SMALL_W = 1792
SMALL_ROWS = 8


def _own_in_place(shard, kind, full_shape, chip):
    two, a, b, c = full_shape
    buf = lax.empty(full_shape, shard.dtype)
    if kind == "col":
        return lax.dynamic_update_slice(buf, shard.reshape(two, a, b, c // N_CHIPS), (0, 0, 0, chip * (c // N_CHIPS)))
    if kind == "row":
        return lax.dynamic_update_slice(buf, shard.reshape(two, a, b // N_CHIPS, c), (0, 0, chip * (b // N_CHIPS), 0))
    return lax.dynamic_update_slice(buf, shard.reshape(two, 1, b, c), (0, chip, 0, 0))


def _headsum_matrix():
    r = lax.broadcasted_iota(jnp.int32, (GW, GW), 0) // HD
    c = lax.broadcasted_iota(jnp.int32, (GW, GW), 1) // HD
    return jnp.where(r == c, 1.0, 0.0).astype(BF16)


def kernel(x, norm_gains, w_qkv_a, w_o_a, w_q_b, w_o_b, kv_norm, w_kvf, b_f, w_up, conv_w, conv_b, w_down, loss_target, m_norm_gains, m_w_qkv_a, m_w_o_a, m_w_q_b, m_w_o_b, m_kv_norm, m_w_kvf, m_b_f, m_w_up, m_conv_w, m_conv_b, m_w_down, v_norm_gains, v_w_qkv_a, v_w_o_a, v_w_q_b, v_w_o_b, v_kv_norm, v_w_kvf, v_b_f, v_w_up, v_conv_w, v_conv_b, v_w_down):
    xi, yi, ci = lax.axis_index("x"), lax.axis_index("y"), lax.axis_index("c")
    chip = 2 * xi + yi
    where = jnp.stack([chip, ci]).astype(jnp.int32)
    ws = dict(norm_gains=norm_gains, w_qkv_a=w_qkv_a, w_o_a=w_o_a, w_q_b=w_q_b, w_o_b=w_o_b, kv_norm=kv_norm, w_kvf=w_kvf,
              b_f=b_f, w_up=w_up, conv_w=conv_w, conv_b=conv_b, w_down=w_down)
    ms = dict(norm_gains=m_norm_gains, w_qkv_a=m_w_qkv_a, w_o_a=m_w_o_a, w_q_b=m_w_q_b, w_o_b=m_w_o_b, kv_norm=m_kv_norm,
              w_kvf=m_w_kvf, b_f=m_b_f, w_up=m_w_up, conv_w=m_conv_w, conv_b=m_conv_b, w_down=m_w_down)
    vs = dict(norm_gains=v_norm_gains, w_qkv_a=v_w_qkv_a, w_o_a=v_w_o_a, w_q_b=v_w_q_b, w_o_b=v_w_o_b, kv_norm=v_kv_norm,
              w_kvf=v_w_kvf, b_f=v_b_f, w_up=v_w_up, conv_w=v_conv_w, conv_b=v_conv_b, w_down=v_w_down)

    small = jnp.concatenate([
        jnp.pad(norm_gains.reshape(16, 256), ((0, 0), (0, 1408 - 256))),
        jnp.pad(conv_w.reshape(12, 1408), ((0, 4), (0, 0)))], axis=0)
    half = {nm: ws[nm].astype(BF16) for nm, _, _ in BIG}
    W = {nm: [None] * ws[nm].shape[0] for nm, _, _ in BIG if nm != "w_kvf"}
    g_small = None
    for l in range(DEPTH):
        j = l - N_A
        group = [("w_up", "col", l), ("w_down", "row", l)]
        group += [("w_qkv_a", "slab", l), ("w_o_a", "col", l)] if l < N_A else [("w_q_b", "row", j), ("w_o_b", "row", j)]
        shards = [half[nm][i] for nm, _, i in group] + ([half["w_kvf"]] if l == N_A else []) + ([small] if l == 0 else [])
        kinds = [k for _, k, _ in group] + (["slab"] if l == N_A else []) + (["slab"] if l == 0 else [])
        got = _seq_gather(shards, kinds, name="gather_layer%d" % l, cid=1)
        for (nm, _, i), g in zip(group, got):
            W[nm][i] = g.transpose(1, 0, 2).reshape(D, 3 * A_W) if nm == "w_qkv_a" else g
        if l == N_A:
            W["w_kvf"] = got[len(group)].transpose(1, 0, 2).reshape(D, 2 * D + 16)
        if l == 0:
            g_small = got[-1]
    gains = g_small[:, :16, :256].transpose(1, 0, 2).reshape(DEPTH, 4, 1, D)
    cw_full = g_small[:, 16:28, :].transpose(1, 0, 2).reshape(DEPTH, 3, 2 * D_FF)
    cb_full = conv_b.reshape(DEPTH, 1, 2 * D_FF)

    sq, dh, full_grads, small_flat = _fwd_bwd(x[0], loss_target[0], W, gains, cw_full, cb_full, kv_norm, b_f)
    loss = lax.psum(sq[0, 0] * (0.5 / D), ("x", "y", "c"))
    return _reduce_update(loss, dh[None], full_grads, small_flat, chip, where, ws, ms, vs)


def _fwd_bwd(h, target, W, gains, cw_full, cb_full, kv_norm, b_f):
    w_kv = W["w_kvf"][:, :2 * D]
    w_kvf_pad = jnp.pad(W["w_kvf"], ((0, 0), (0, 128 - 16)))
    w_f = w_kvf_pad[:, 2 * D:]
    kvn_g = kv_norm.reshape(1, D)
    bf_pad = jnp.pad(b_f, (0, 128 - 16)).reshape(1, 128)
    tabs = _rope_tables()
    headsum = _headsum_matrix()

    saved = []
    kv = zf = c_col = c_row = kvn = h_kv = None
    for l in range(DEPTH):
        s = {"h": h}
        g = gains[l]
        xn = _rms_fwd(h, g[0], out_dtype=BF16, name="rms_in")
        s["xn"] = xn
        if l < N_A:
            qkv = _matmul(xn, W["w_qkv_a"][l], mode="nn", out_dtype=F32, name="mm_qkv", mnk=(T, 3 * A_W, D), tn=768)
            q3, k3, v3 = _rope_fwd(qkv, tabs)
            qp, kp, vp = _perm(q3), _perm(k3), _perm(v3)
            o_p, lse_p = _band_fwd(qp, kp, vp)
            o3, lse3 = _unperm(o_p), _unperm(lse_p)
            att = _combine_fwd(o3, lse3)
            s.update(qp=qp, kp=kp, vp=vp, o3=o3, lse3=lse3, lse_p=lse_p, att=att)
            mix = _matmul(att, W["w_o_a"][l], mode="nn", out_dtype=F32, name="mm_oa", mnk=(T, D, A_W))
        else:
            j = l - N_A
            if l == N_A:
                h_kv = h
                kvn = _rms_fwd(h, kvn_g, out_dtype=BF16, name="rms_in")
                kv = _matmul(kvn, w_kv, mode="nn", out_dtype=BF16, name="mm_kv")
                zf = _matmul(kvn, w_f, mode="nn", out_dtype=F32, name="mm_f")
                cum = _gates_fwd(zf, bf_pad)[:, :16]
                c_col = cum.reshape(T, 8, 2).transpose(1, 0, 2)
                c_row = cum.T.reshape(8, 2, T)
            q = _matmul(xn, W["w_q_b"][j], mode="nn", out_dtype=BF16, name="mm_qb", mnk=(T, D, D), alpha=HD ** -0.5)
            o = _fox_fwd(q, kv, c_col, c_row)
            s.update(q=q, o=o)
            mix = _matmul(o, W["w_o_b"][j], mode="nn", out_dtype=F32, name="mm_ob", mnk=(T, D, D))
        s["mix"] = mix
        h1 = _rms_fwd(mix, g[1], res=h, out_dtype=F32, name="rms_res")
        xn2 = _rms_fwd(h1, g[2], out_dtype=BF16, name="rms_in")
        a = _matmul(xn2, W["w_up"][l], mode="nn", out_dtype=F32, name="mm_up", mnk=(T, 2 * D_FF, D))
        u = _convgate_fwd(a, cw_full[l], cb_full[l])
        f = _matmul(u, W["w_down"][l], mode="nn", out_dtype=F32, name="mm_down", mnk=(T, D, D_FF), tm=1024, tk=D_FF)
        h = _rms_fwd(f, g[3], res=h1, out_dtype=F32, name="rms_res")
        s.update(h1=h1, xn2=xn2, a=a, u=u, f=f)
        saved.append(s)

    dh, sq = _loss_head(h, target)

    gw = dict(w_qkv_a=None, w_o_a=None, w_q_b=None, w_o_b=None, w_up=None, w_down=None)
    d_gains = [[None] * 4 for _ in range(DEPTH)]
    d_cw, d_cb = [None] * DEPTH, [None] * DEPTH
    zeros_td = jnp.zeros((T, D), F32)
    fox_acc = (zeros_td, zeros_td, jnp.zeros((D // 128, T, 128), F32), jnp.zeros((D // 128, 8, T), F32))
    d_kvf = d_kvnorm = d_bf = None

    def dw(nm, slab, slabs, a, b, **kw):
        gw[nm] = _matmul(a, b, mode="tn", out_dtype=BF16, name="mm_dw_" + nm, out_slab=slab, out_slabs=slabs, out_buf=gw[nm], **kw)

    for l in reversed(range(DEPTH)):
        s = saved[l]
        g = gains[l]
        df, d_gains[l][3] = _rms_bwd(dh, s["f"], g[3], out_dtype=BF16, name="rms_bwd")
        du = _matmul(df, W["w_down"][l], mode="nt", out_dtype=F32, name="mm_down_dx", mnk=(T, D_FF, D), tn=256)
        dw("w_down", l, DEPTH, s["u"], df, tm=1408, tn=1024)
        da, d_cw[l], d_cb[l] = _convgate_bwd(s["a"], du, cw_full[l], cb_full[l])
        dxn2 = _matmul(da, W["w_up"][l], mode="nt", out_dtype=F32, name="mm_up_dx", mnk=(T, D, 2 * D_FF), tm=1024, tn=1024, tk=1408,
                       a_map=_halves_a)
        dw("w_up", l, DEPTH, s["xn2"], da, mnk=(D, 2 * D_FF, T), tn=1408, b_map=_halves_b)
        dh1, d_gains[l][2] = _rms_bwd(dxn2, s["h1"], g[2], dres=dh, out_dtype=F32, name="rms_bwd_res")
        dmix, d_gains[l][1] = _rms_bwd(dh1, s["mix"], g[1], out_dtype=BF16, name="rms_bwd")
        if l < N_A:
            datt = _matmul(dmix, W["w_o_a"][l], mode="nt", out_dtype=F32, name="mm_oa_dx", mnk=(T, A_W, D), tn=768)
            dw("w_o_a", l, N_A, s["att"], dmix, tm=768, tn=1024)
            do3, dlt3 = _combine_bwd(datt, s["o3"], s["lse3"], headsum)
            dqp, dkp, dvp = _band_bwd(s["qp"], s["kp"], s["vp"], _perm(do3), s["lse_p"], _perm(dlt3))
            dqkv = _rope_bwd(_unperm(dqp), _unperm(dkp), _unperm(dvp), tabs)
            dxn = _matmul(dqkv, W["w_qkv_a"][l], mode="nt", out_dtype=F32, name="mm_qkv_dx", mnk=(T, D, 3 * A_W), tm=1024, tn=1024, tk=3 * A_W)
            dw("w_qkv_a", l, N_A, s["xn"], dqkv, tn=768)
        else:
            j = l - N_A
            do = _matmul(dmix, W["w_o_b"][j], mode="nt", out_dtype=BF16, name="mm_ob_dx", mnk=(T, D, D))
            dw("w_o_b", j, DEPTH - N_A, s["o"], dmix, tn=1024)
            dq, *fox_acc = _fox_bwd(s["q"], kv, do, c_col, c_row, fox_acc)
            dxn = _matmul(dq, W["w_q_b"][j], mode="nt", out_dtype=F32, name="mm_qb_dx", mnk=(T, D, D))
            dw("w_q_b", j, DEPTH - N_A, s["xn"], dq, tn=1024)
        dh, d_gains[l][0] = _rms_bwd(dxn, s["h"], g[0], dres=dh1, out_dtype=F32, name="rms_bwd_res")
        if l == N_A:
            dk, dv, dcq, dck = fox_acc
            dc16 = dcq[:, :, :2].transpose(1, 0, 2).reshape(T, 16) - dck[:, :2, :].reshape(16, T).T
            dzf, d_bf = _gates_bwd(jnp.pad(dc16, ((0, 0), (0, 128 - 16))), zf, bf_pad)
            dkvf = jnp.concatenate([dk.astype(BF16), dv.astype(BF16), dzf], axis=1)
            d_kvf = _matmul(kvn, dkvf, mode="tn", out_dtype=BF16, name="mm_kvf_dw", tm=512, tn=2 * D + 128)[:, :2 * D + 16]
            dkvn = _matmul(dkvf, w_kvf_pad, mode="nt", out_dtype=F32, name="mm_kvf_dx", tm=1024, tn=1024, tk=2 * D + 128)
            dh, d_kvnorm = _rms_bwd(dkvn, h_kv, kvn_g, dres=dh, out_dtype=F32, name="rms_bwd_res")
    full_grads = dict(gw, w_kvf=d_kvf)
    small_flat = jnp.concatenate([
        jnp.stack([jnp.stack(r) for r in d_gains]).reshape(-1),
        jnp.stack(d_cw).transpose(0, 2, 1, 3).reshape(-1),
        jnp.stack(d_cb).reshape(-1),
        d_kvnorm.reshape(-1), d_bf[0, :16]])
    return sq, dh, full_grads, small_flat


def _reduce_update(loss, grad_x, full_grads, small_flat, chip, where, ws, ms, vs):
    parts, kinds = [], []
    for nm, kind, shp in BIG:
        g = full_grads[nm]
        if nm == "w_qkv_a":
            g = g.reshape(2, D, N_CHIPS, 576).transpose(0, 2, 1, 3)
        elif nm == "w_kvf":
            g = g.reshape(2, 512, N_CHIPS, 516).transpose(0, 2, 1, 3)
        parts.append(g.reshape(shp))
        kinds.append(kind)
    n_small = small_flat.shape[0]
    gs = jnp.pad(small_flat, (0, 2 * N_CHIPS * SMALL_ROWS * SMALL_W - n_small))
    parts.append(gs.reshape(N_CHIPS, 2, SMALL_ROWS, SMALL_W).transpose(1, 0, 2, 3))
    kinds.append("slab")
    names = [nm for nm, _, _ in BIG] + ["small"]

    sib = _swap_halves(parts, name="reduce_pair_swap")
    halves = [_add_half(g, p, where, name="reduce_pair_add_" + nm) for g, p, nm in zip(parts, sib, names)]
    landed = _scatter_chips(halves, kinds, name="reduce_chip_scatter")
    sums = [_sum_chips(r, h, k, where, name="reduce_chip_sum_" + nm) for r, h, k, nm in zip(landed, halves, kinds, names)]
    *reduced, red_s = _join_halves(sums, name="reduce_pair_join")
    buf_s = lax.dynamic_update_slice(jnp.zeros((2, N_CHIPS, SMALL_ROWS, SMALL_W), F32), red_s.reshape(2, 1, SMALL_ROWS, SMALL_W),
                                     (0, chip, 0, 0))
    (all_s,) = _allgather([buf_s], ["slab"], name="gather_small_grads")
    sflat = all_s.transpose(1, 0, 2, 3).reshape(-1)

    grads = {nm: r.reshape(ws[nm].shape) for (nm, _, _), r in zip(BIG, reduced)}
    o = 0
    g_gains_full = sflat[o:o + 16 * D].reshape(DEPTH, 4, D); o += 16 * D
    g_cw_full = sflat[o:o + 12 * 2 * D_FF].reshape(DEPTH, 3, 2 * D_FF); o += 12 * 2 * D_FF
    grads["conv_b"] = sflat[o:o + 4 * 2 * D_FF].reshape(DEPTH, 2 * D_FF); o += 4 * 2 * D_FF
    grads["kv_norm"] = sflat[o:o + D]; o += D
    grads["b_f"] = sflat[o:o + 16]
    grads["norm_gains"] = lax.dynamic_slice_in_dim(g_gains_full, chip * 256, 256, axis=2)
    grads["conv_w"] = lax.dynamic_slice_in_dim(g_cw_full, chip * 1408, 1408, axis=2)

    names = ["norm_gains", "w_qkv_a", "w_o_a", "w_q_b", "w_o_b", "kv_norm", "w_kvf", "b_f", "w_up", "conv_w", "conv_b", "w_down"]
    deltas, new_m, new_v = {}, {}, {}
    for nm in names:
        shp = ws[nm].shape
        two = (math.prod(shp[:-1]), shp[-1]) if len(shp) > 1 else (1, shp[0])
        d, m2, v2 = _adamw(ws[nm].reshape(two), ms[nm].reshape(two), vs[nm].reshape(two), grads[nm].reshape(two),
                           name="adamw_" + nm)
        deltas[nm], new_m[nm], new_v[nm] = d.reshape(shp), m2.reshape(shp), v2.reshape(shp)

    return (loss, grad_x, *[grads[nm] for nm in names], *[deltas[nm] for nm in names],
            *[new_m[nm] for nm in names], *[new_v[nm] for nm in names])
```

```python
import math

import jax
import jax.numpy as jnp
from jax import lax
from jax.experimental import pallas as pl
from jax.experimental.pallas import tpu as pltpu
from jax.experimental.pallas import tpu_sc as plsc

F32 = jnp.float32
BF16 = jnp.bfloat16
MESH = pl.DeviceIdType.MESH
ANY = pl.BlockSpec(memory_space=pl.ANY)

T = 2048
D = 1024
HD = 64
DEPTH = 4
N_A = 2
A_W = 768
GW = 256
DIL = (1, 4, 16)
BLK = 128
D_FF = 2816
ROPE_THETA = 500000.0
EPS = 1e-6
NEG = -1e30
N_CHIPS = 4
FQ = 256
CT = 128
VMEM_BIG = 48 * 1024 * 1024

ADAM_LR, ADAM_B1, ADAM_B2, ADAM_EPS, ADAM_WD, ADAM_STEP = 0.001, 0.9, 0.999, 1e-08, 0.01, 10

NN = (((1,), (0,)), ((), ()))
NT = (((1,), (1,)), ((), ()))
TN = (((0,), (0,)), ((), ()))


def _dot(a, b, dims):
    return lax.dot_general(a, b, dims, preferred_element_type=F32)


def _pick(dim, pref):
    if dim <= pref:
        return dim
    best = None
    for t in range(128, pref + 1, 128):
        if dim % t == 0:
            best = t
    assert best is not None, (dim, pref)
    return best


def _params(sem=None, vmem=None):
    kw = {}
    if sem is not None:
        kw["dimension_semantics"] = sem
    if vmem is not None:
        kw["vmem_limit_bytes"] = vmem
    return pltpu.CompilerParams(**kw)


def _matmul(a, b, *, mode, out_dtype, name, mnk=None, alpha=None, tm=2048, tn=512, tk=2048,
            a_map=None, b_map=None, acc_init=None, out_slab=None, out_slabs=None, out_buf=None):
    if mnk is not None:
        M, N, K = mnk
    elif mode == "nn":
        (M, K), (_, N) = a.shape, b.shape
    elif mode == "nt":
        (M, K), (N, _) = a.shape, b.shape
    else:
        (K, M), (_, N) = a.shape, b.shape
    tm, tn, tk = _pick(M, tm), _pick(N, tn), _pick(K, tk)
    nk = K // tk
    dims = {"nn": NN, "nt": NT, "tn": TN}[mode]
    n_in = 2 + (acc_init is not None) + (out_buf is not None)

    def body(*refs):
        a_ref, b_ref = refs[0], refs[1]
        o_ref = refs[n_in]
        k = pl.program_id(2)

        def finish(r):
            if alpha is not None:
                r = r * alpha
            o_ref[...] = r.astype(out_dtype)

        def product():
            r = _dot(a_ref[...], b_ref[...], dims)
            return r if acc_init is None else r + refs[2][...]

        if nk == 1:
            finish(product())
            return
        acc_ref = refs[n_in + 1]

        @pl.when(k == 0)
        def _():
            acc_ref[...] = product()

        @pl.when((k > 0) & (k < nk - 1))
        def _():
            acc_ref[...] += _dot(a_ref[...], b_ref[...], dims)

        @pl.when(k == nk - 1)
        def _():
            finish(acc_ref[...] + _dot(a_ref[...], b_ref[...], dims))

    a_blk = (tk, tm) if mode == "tn" else (tm, tk)
    b_blk = (tn, tk) if mode == "nt" else (tk, tn)
    if a_map is not None:
        a_spec = pl.BlockSpec((None,) + a_blk, a_map(tm, tn, tk))
    elif mode == "tn":
        a_spec = pl.BlockSpec(a_blk, lambda i, j, k: (k, i))
    else:
        a_spec = pl.BlockSpec(a_blk, lambda i, j, k: (i, k))
    if b_map is not None:
        b_spec = pl.BlockSpec((None,) + b_blk, b_map(tm, tn, tk))
    elif mode == "nt":
        b_spec = pl.BlockSpec(b_blk, lambda i, j, k: (j, k))
    else:
        b_spec = pl.BlockSpec(b_blk, lambda i, j, k: (k, j))
    ins, specs, alias = [a, b], [a_spec, b_spec], {}
    if acc_init is not None:
        ins.append(acc_init)
        specs.append(pl.BlockSpec((tm, tn), lambda i, j, k: (i, j)))
    if out_buf is not None:
        alias = {len(ins): 0}
        ins.append(out_buf)
        specs.append(ANY)
    if out_slab is None:
        o_spec = pl.BlockSpec((tm, tn), lambda i, j, k: (i, j))
        o_shape = jax.ShapeDtypeStruct((M, N), out_dtype)
    else:
        o_spec = pl.BlockSpec((None, tm, tn), lambda i, j, k: (out_slab, i, j))
        o_shape = jax.ShapeDtypeStruct((out_slabs, M, N), out_dtype)
    return pl.pallas_call(
        body,
        grid=(M // tm, N // tn, nk),
        in_specs=specs,
        out_specs=o_spec,
        out_shape=o_shape,
        scratch_shapes=[pltpu.VMEM((tm, tn), F32)] if nk > 1 else [],
        input_output_aliases=alias,
        compiler_params=_params(("parallel", "parallel", "arbitrary"), VMEM_BIG),
        name=name,
    )(*ins)


def _slab(l, mode):
    if mode == "nt":
        return lambda tm, tn, tk: (lambda i, j, k: (l, j, k))
    return lambda tm, tn, tk: (lambda i, j, k: (l, k, j))


def _rms_fwd(x, g, *, out_dtype, name, res=None, tr=256):
    n, d = x.shape

    def body(*refs):
        x_ref, g_ref = refs[0], refs[1]
        o_ref = refs[-1]
        xv = x_ref[...].astype(F32)
        y = xv * lax.rsqrt(jnp.mean(xv * xv, axis=-1, keepdims=True) + EPS) * g_ref[...]
        if res is not None:
            y = y + refs[2][...]
        o_ref[...] = y.astype(out_dtype)

    row = pl.BlockSpec((tr, d), lambda i: (i, 0))
    vec = pl.BlockSpec((1, d), lambda i: (0, 0))
    ins = [x, g] + ([] if res is None else [res])
    specs = [row, vec] + ([] if res is None else [row])
    return pl.pallas_call(
        body, grid=(n // tr,), in_specs=specs, out_specs=row,
        out_shape=jax.ShapeDtypeStruct((n, d), out_dtype),
        compiler_params=_params(("parallel",)), name=name,
    )(*ins)


def _rms_bwd(dy, x, g, *, out_dtype, name, dres=None, tr=256):
    n, d = x.shape

    def body(*refs):
        dy_ref, x_ref, g_ref = refs[0], refs[1], refs[2]
        dx_ref, dg_ref = refs[-2], refs[-1]
        xv = x_ref[...].astype(F32)
        dyv = dy_ref[...].astype(F32)
        rstd = lax.rsqrt(jnp.mean(xv * xv, axis=-1, keepdims=True) + EPS)
        xhat = xv * rstd
        dxh = dyv * g_ref[...]
        dx = rstd * (dxh - xhat * jnp.mean(dxh * xhat, axis=-1, keepdims=True))
        if dres is not None:
            dx = dx + refs[3][...]
        dx_ref[...] = dx.astype(out_dtype)

        @pl.when(pl.program_id(0) == 0)
        def _():
            dg_ref[...] = jnp.zeros_like(dg_ref)

        dg_ref[...] += jnp.sum(dyv * xhat, axis=0, keepdims=True)

    row = pl.BlockSpec((tr, d), lambda i: (i, 0))
    vec = pl.BlockSpec((1, d), lambda i: (0, 0))
    ins = [dy, x, g] + ([] if dres is None else [dres])
    specs = [row, row, vec] + ([] if dres is None else [row])
    return pl.pallas_call(
        body, grid=(n // tr,), in_specs=specs, out_specs=[row, vec],
        out_shape=[jax.ShapeDtypeStruct((n, d), out_dtype), jax.ShapeDtypeStruct((1, d), F32)],
        compiler_params=_params(("arbitrary",)), name=name,
    )(*ins)


def _loss_head(h, target, *, tr=256):
    n, d = h.shape

    def body(h_ref, t_ref, dh_ref, s_ref):
        err = h_ref[...] - t_ref[...]
        dh_ref[...] = err * (1.0 / d)

        @pl.when(pl.program_id(0) == 0)
        def _():
            s_ref[...] = jnp.zeros_like(s_ref)

        s_ref[...] += jnp.sum(err * err)

    row = pl.BlockSpec((tr, d), lambda i: (i, 0))
    acc = pl.BlockSpec((8, 128), lambda i: (0, 0))
    return pl.pallas_call(
        body, grid=(n // tr,), in_specs=[row, row], out_specs=[row, acc],
        out_shape=[jax.ShapeDtypeStruct((n, d), F32), jax.ShapeDtypeStruct((8, 128), F32)],
        compiler_params=_params(("arbitrary",)), name="loss_head",
    )(h, target)


def _rope_tables():
    pos = jnp.arange(T, dtype=F32)
    inv = ROPE_THETA ** (-jnp.arange(0, 16, 2, dtype=F32) / 16)
    ang = pos[:, None] * inv[None, :]
    cos, sin = jnp.cos(ang), jnp.sin(ang)
    one = jnp.ones((T, HD - 16), F32)
    zero8 = jnp.zeros((T, 8), F32)
    zero = jnp.zeros((T, HD - 16), F32)
    c = jnp.concatenate([cos, cos, one], axis=1)
    s1 = jnp.concatenate([zero8, sin, zero], axis=1)
    s2 = jnp.concatenate([-sin, zero8, zero], axis=1)
    return tuple(jnp.concatenate([t, t], axis=1) for t in (c, s1, s2))


def _rope_fwd(qkv, tabs, *, tr=256):
    def body(x_ref, c_ref, s1_ref, s2_ref, q_ref, k_ref, v_ref):
        c, s1, s2 = c_ref[...], s1_ref[...], s2_ref[...]
        for which, o_ref, scale in ((0, q_ref, HD ** -0.5), (1, k_ref, None)):
            for j in range(A_W // 128):
                x = x_ref[:, which * A_W + j * 128: which * A_W + (j + 1) * 128]
                y = x * c + pltpu.roll(x, 8, 1) * s1 + pltpu.roll(x, 120, 1) * s2
                if scale is not None:
                    y = y * scale
                o_ref[j // 2, :, (j % 2) * 128:(j % 2 + 1) * 128] = y.astype(BF16)
        for j in range(A_W // 128):
            v_ref[j // 2, :, (j % 2) * 128:(j % 2 + 1) * 128] = x_ref[:, 2 * A_W + j * 128: 2 * A_W + (j + 1) * 128].astype(BF16)

    tab = pl.BlockSpec((tr, 128), lambda i: (i, 0))
    out = pl.BlockSpec((3, tr, GW), lambda i: (0, i, 0))
    shp = jax.ShapeDtypeStruct((3, T, GW), BF16)
    return pl.pallas_call(
        body, grid=(T // tr,), in_specs=[pl.BlockSpec((tr, 3 * A_W), lambda i: (i, 0)), tab, tab, tab],
        out_specs=[out, out, out], out_shape=[shp, shp, shp],
        compiler_params=_params(("parallel",)), name="rope_fwd",
    )(qkv, *tabs)


def _rope_bwd(dq, dk, dv, tabs, *, tr=256):
    def body(dq_ref, dk_ref, dv_ref, c_ref, s1_ref, s2_ref, o_ref):
        c, s1, s2 = c_ref[...], s1_ref[...], s2_ref[...]
        for which, i_ref, scale in ((0, dq_ref, HD ** -0.5), (1, dk_ref, None)):
            for j in range(A_W // 128):
                g = i_ref[j // 2, :, (j % 2) * 128:(j % 2 + 1) * 128]
                y = g * c + pltpu.roll(g * s1, 120, 1) + pltpu.roll(g * s2, 8, 1)
                if scale is not None:
                    y = y * scale
                o_ref[:, which * A_W + j * 128: which * A_W + (j + 1) * 128] = y.astype(BF16)
        for j in range(A_W // 128):
            o_ref[:, 2 * A_W + j * 128: 2 * A_W + (j + 1) * 128] = dv_ref[j // 2, :, (j % 2) * 128:(j % 2 + 1) * 128].astype(BF16)

    tab = pl.BlockSpec((tr, 128), lambda i: (i, 0))
    cot = pl.BlockSpec((3, tr, GW), lambda i: (0, i, 0))
    return pl.pallas_call(
        body, grid=(T // tr,), in_specs=[cot, cot, cot, tab, tab, tab],
        out_specs=pl.BlockSpec((tr, 3 * A_W), lambda i: (i, 0)),
        out_shape=jax.ShapeDtypeStruct((T, 3 * A_W), BF16),
        compiler_params=_params(("parallel",)), name="rope_bwd",
    )(dq, dk, dv, *tabs)


def _perm(x3):
    out = [x3[0]]
    for g in (1, 2):
        r = DIL[g]
        out.append(x3[g].reshape(T // r, r, GW).transpose(1, 0, 2).reshape(T, GW))
    return jnp.stack(out)


def _unperm(x3):
    out = [x3[0]]
    for g in (1, 2):
        r = DIL[g]
        out.append(x3[g].reshape(r, T // r, GW).transpose(1, 0, 2).reshape(T, GW))
    return jnp.stack(out)


def _head_mask(x, lane_lo):
    lane = lax.broadcasted_iota(jnp.int32, x.shape, 1)
    keep = (lane < HD) if lane_lo else (lane >= HD)
    return jnp.where(keep, x.astype(F32), 0.0).astype(BF16)


def _band_scalars():
    g, b = pl.program_id(0), pl.program_id(1)
    nbs = lax.shift_right_logical(jnp.int32(T // BLK), 2 * g)
    has_prev = jnp.where((b & (nbs - 1)) != 0, 1, 0)
    next_ok = jnp.where(((b + 1) & (nbs - 1)) != 0, 1, 0)
    return has_prev, next_ok


def _band_mask_q(has_prev):
    row = lax.broadcasted_iota(jnp.int32, (BLK, 2 * BLK), 0)
    col = lax.broadcasted_iota(jnp.int32, (BLK, 2 * BLK), 1)
    return ((col < BLK) & (col >= row) & (has_prev == 1)) | ((col >= BLK) & (col - BLK <= row))


def _band_mask_k(next_ok):
    row = lax.broadcasted_iota(jnp.int32, (2 * BLK, BLK), 0)
    col = lax.broadcasted_iota(jnp.int32, (2 * BLK, BLK), 1)
    return ((row < BLK) & (col <= row)) | ((row >= BLK) & (col >= row - BLK) & (next_ok == 1))


def _band_fwd(q, k, v):
    nb = T // BLK

    def body(q_ref, kc_ref, kp_ref, vc_ref, vp_ref, o_ref, l_ref):
        has_prev, _ = _band_scalars()
        mask = _band_mask_q(has_prev)
        lane = lax.broadcasted_iota(jnp.int32, (BLK, 128), 1)
        for p in range(2):
            sl = slice(128 * p, 128 * (p + 1))
            qp = q_ref[0, :, sl]
            kcat = jnp.concatenate([kp_ref[0, :, sl], kc_ref[0, :, sl]], axis=0)
            vcat = jnp.concatenate([vp_ref[0, :, sl], vc_ref[0, :, sl]], axis=0)
            o_acc = jnp.zeros((BLK, 128), F32)
            lse = jnp.zeros((BLK, 128), F32)
            for e in range(2):
                s = _dot(_head_mask(qp, e == 0), kcat, NT)
                s = jnp.where(mask, s, NEG)
                m = jnp.max(s, axis=-1, keepdims=True)
                pr = jnp.exp(s - m)
                l = jnp.sum(pr, axis=-1, keepdims=True)
                o_acc = o_acc + _dot(pr.astype(BF16), _head_mask(vcat, e == 0), NN) / l
                lse = jnp.where((lane < HD) if e == 0 else (lane >= HD), m + jnp.log(l), lse)
            o_ref[0, :, sl] = o_acc
            l_ref[0, :, sl] = lse

    cur = pl.BlockSpec((1, BLK, GW), lambda g, b: (g, b, 0))
    prev = pl.BlockSpec((1, BLK, GW), lambda g, b: (g, jnp.maximum(b - 1, 0), 0))
    shp = jax.ShapeDtypeStruct((3, T, GW), F32)
    return pl.pallas_call(
        body, grid=(3, nb), in_specs=[cur, cur, prev, cur, prev], out_specs=[cur, cur], out_shape=[shp, shp],
        compiler_params=_params(("parallel", "parallel")), name="band_fwd",
    )(q, k, k, v, v)


def _band_bwd(q, k, v, do, lse, dlt):
    nb = T // BLK

    def body(qc_ref, qn_ref, kc_ref, kp_ref, vc_ref, vp_ref, doc_ref, don_ref, lc_ref, ln_ref, dc_ref, dn_ref,
             dq_ref, dk_ref, dv_ref):
        has_prev, next_ok = _band_scalars()
        mask_q = _band_mask_q(has_prev)
        mask_k = _band_mask_k(next_ok)
        for p in range(2):
            sl = slice(128 * p, 128 * (p + 1))
            qc, qn = qc_ref[0, :, sl], qn_ref[0, :, sl]
            doc, don = doc_ref[0, :, sl], don_ref[0, :, sl]
            kc, vc = kc_ref[0, :, sl], vc_ref[0, :, sl]
            kcat = jnp.concatenate([kp_ref[0, :, sl], kc], axis=0)
            vcat = jnp.concatenate([vp_ref[0, :, sl], vc], axis=0)
            qcat = jnp.concatenate([qc, qn], axis=0)
            docat = jnp.concatenate([doc, don], axis=0)
            dq = jnp.zeros((BLK, 128), F32)
            dk = jnp.zeros((BLK, 128), F32)
            dv = jnp.zeros((BLK, 128), F32)
            for e in range(2):
                lo = e == 0
                col = slice(128 * p + HD * e, 128 * p + HD * e + 1)
                lse_c, lse_n = lc_ref[0, :, col], ln_ref[0, :, col]
                dl_c, dl_n = dc_ref[0, :, col], dn_ref[0, :, col]
                s = jnp.where(mask_q, _dot(_head_mask(qc, lo), kcat, NT), NEG)
                pr = jnp.exp(s - lse_c)
                dp = _dot(_head_mask(doc, lo), vcat, NT)
                ds = pr * (dp - dl_c)
                dq = dq + _dot(ds.astype(BF16), _head_mask(kcat, lo), NN)
                qm, dom = _head_mask(qcat, lo), _head_mask(docat, lo)
                s2 = jnp.where(mask_k, _dot(qm, kc, NT), NEG)
                p2 = jnp.exp(s2 - jnp.concatenate([lse_c, lse_n], axis=0))
                dv = dv + _dot(p2.astype(BF16), dom, TN)
                dp2 = _dot(dom, vc, NT)
                ds2 = p2 * (dp2 - jnp.concatenate([dl_c, dl_n], axis=0))
                dk = dk + _dot(ds2.astype(BF16), qm, TN)
            dq_ref[0, :, sl] = dq
            dk_ref[0, :, sl] = dk
            dv_ref[0, :, sl] = dv

    cur = pl.BlockSpec((1, BLK, GW), lambda g, b: (g, b, 0))
    prev = pl.BlockSpec((1, BLK, GW), lambda g, b: (g, jnp.maximum(b - 1, 0), 0))
    nxt = pl.BlockSpec((1, BLK, GW), lambda g, b: (g, jnp.minimum(b + 1, nb - 1), 0))
    shp = jax.ShapeDtypeStruct((3, T, GW), F32)
    return pl.pallas_call(
        body, grid=(3, nb),
        in_specs=[cur, nxt, cur, prev, cur, prev, cur, nxt, cur, nxt, cur, nxt],
        out_specs=[cur, cur, cur], out_shape=[shp, shp, shp],
        compiler_params=_params(("parallel", "parallel")), name="band_bwd",
    )(q, q, k, k, v, v, do, do, lse, lse, dlt, dlt)


def _split3(x):
    hi = x.astype(BF16)
    r = x - hi.astype(F32)
    mid = r.astype(BF16)
    lo = (r - mid.astype(F32)).astype(BF16)
    return hi, mid, lo


def _dot3(x, m, dims=NN):
    hi, mid, lo = _split3(x)
    return _dot(hi, m, dims) + _dot(mid, m, dims) + _dot(lo, m, dims)


def _combine_weights(l_ref):
    l0, l1, l2 = l_ref[0], l_ref[1], l_ref[2]
    m = jnp.maximum(jnp.maximum(l0, l1), l2)
    e = [jnp.exp(l0 - m), jnp.exp(l1 - m), jnp.exp(l2 - m)]
    inv = 1.0 / (e[0] + e[1] + e[2])
    return [ei * inv for ei in e]


def _combine_fwd(o, lse, *, tr=256):
    def body(o_ref, l_ref, out_ref):
        alpha = _combine_weights(l_ref)
        for g in range(3):
            out_ref[:, g * GW:(g + 1) * GW] = (o_ref[g] * alpha[g]).astype(BF16)

    blk = pl.BlockSpec((3, tr, GW), lambda i: (0, i, 0))
    return pl.pallas_call(
        body, grid=(T // tr,), in_specs=[blk, blk], out_specs=pl.BlockSpec((tr, A_W), lambda i: (i, 0)),
        out_shape=jax.ShapeDtypeStruct((T, A_W), BF16), compiler_params=_params(("parallel",)), name="combine_fwd",
    )(o, lse)


def _combine_bwd(datt, o, lse, headsum, *, tr=256):
    def body(d_ref, o_ref, l_ref, hs_ref, do_ref, dl_ref):
        alpha = _combine_weights(l_ref)
        hs = hs_ref[...]
        total = jnp.zeros((tr, GW), F32)
        for g in range(3):
            dg = d_ref[:, g * GW:(g + 1) * GW]
            do_ref[g] = (dg * alpha[g]).astype(BF16)
            total = total + alpha[g] * _dot3(dg * o_ref[g], hs)
        for g in range(3):
            dl_ref[g] = alpha[g] * total

    blk = pl.BlockSpec((3, tr, GW), lambda i: (0, i, 0))
    return pl.pallas_call(
        body, grid=(T // tr,),
        in_specs=[pl.BlockSpec((tr, A_W), lambda i: (i, 0)), blk, blk, pl.BlockSpec((GW, GW), lambda i: (0, 0))],
        out_specs=[blk, blk],
        out_shape=[jax.ShapeDtypeStruct((3, T, GW), BF16), jax.ShapeDtypeStruct((3, T, GW), F32)],
        compiler_params=_params(("parallel",)), name="combine_bwd",
    )(datt, o, lse, headsum)


def _fox_scores(qm, k_ref, cq, ck_ref, e, i, n):
    s = _dot(qm, k_ref[0:n, :], NT) + (cq - ck_ref[0, e:e + 1, 0:n])
    row = lax.broadcasted_iota(jnp.int32, (FQ, n), 0)
    col = lax.broadcasted_iota(jnp.int32, (FQ, n), 1)
    s = jnp.where(col <= row + i * FQ, s, NEG)
    m = jnp.max(s, axis=-1, keepdims=True)
    pr = jnp.exp(s - m)
    return pr, jnp.sum(pr, axis=-1, keepdims=True)


def _fox_fwd(q, kv, c_col, c_row):
    def body(q_ref, k_ref, v_ref, cc_ref, cr_ref, o_ref, vm_ref):
        for e in range(2):
            vm_ref[e] = _head_mask(v_ref[...], e == 0)
        for i in range(T // FQ):
            n = (i + 1) * FQ
            rows = slice(i * FQ, n)
            acc = jnp.zeros((FQ, 128), F32)
            for e in range(2):
                qm = _head_mask(q_ref[rows, :], e == 0)
                pr, l = _fox_scores(qm, k_ref, cc_ref[0, rows, e:e + 1], cr_ref, e, i, n)
                acc = acc + _dot(pr.astype(BF16), vm_ref[e, 0:n, :], NN) / l
            o_ref[rows, :] = acc.astype(BF16)

    pair = pl.BlockSpec((T, 128), lambda p: (0, p))
    return pl.pallas_call(
        body, grid=(D // 128,),
        in_specs=[pair, pair, pl.BlockSpec((T, 128), lambda p: (0, D // 128 + p)),
                  pl.BlockSpec((1, T, 2), lambda p: (p, 0, 0)), pl.BlockSpec((1, 2, T), lambda p: (p, 0, 0))],
        out_specs=pair, out_shape=jax.ShapeDtypeStruct((T, D), BF16),
        scratch_shapes=[pltpu.VMEM((2, T, 128), BF16)],
        compiler_params=_params(("parallel",), VMEM_BIG), name="fox_fwd",
    )(q, kv, kv, c_col, c_row)


def _fox_bwd(q, kv, do, c_col, c_row, init):
    def body(q_ref, k_ref, v_ref, do_ref, cc_ref, cr_ref, ik_ref, iv_ref, iq_ref, ic_ref,
             dq_ref, dk_ref, dv_ref, dcq_ref, dck_ref, km_ref):
        dk_ref[...] = ik_ref[...]
        dv_ref[...] = iv_ref[...]
        dcq_ref[...] = iq_ref[...]
        dck_ref[...] = ic_ref[...]
        for e in range(2):
            km_ref[e] = _head_mask(k_ref[...], e == 0)
        for i in range(T // FQ):
            n = (i + 1) * FQ
            rows = slice(i * FQ, n)
            dq = jnp.zeros((FQ, 128), F32)
            for e in range(2):
                qm = _head_mask(q_ref[rows, :], e == 0)
                dom = _head_mask(do_ref[rows, :], e == 0)
                pr, l = _fox_scores(qm, k_ref, cc_ref[0, rows, e:e + 1], cr_ref, e, i, n)
                pr = pr / l
                dp = _dot(dom, v_ref[0:n, :], NT)
                ds = pr * (dp - jnp.sum(pr * dp, axis=-1, keepdims=True))
                dsb = ds.astype(BF16)
                dq = dq + _dot(dsb, km_ref[e, 0:n, :], NN)
                dk_ref[0:n, :] += _dot(dsb, qm, TN)
                dv_ref[0:n, :] += _dot(pr.astype(BF16), dom, TN)
                dcq_ref[0, rows, e:e + 1] += jnp.sum(ds, axis=-1, keepdims=True)
                dck_ref[0, e:e + 1, 0:n] += jnp.sum(ds, axis=0, keepdims=True)
            dq_ref[rows, :] = (dq * HD ** -0.5).astype(BF16)

    pair = pl.BlockSpec((T, 128), lambda p: (0, p))
    cq = pl.BlockSpec((1, T, 128), lambda p: (p, 0, 0))
    ck = pl.BlockSpec((1, 8, T), lambda p: (p, 0, 0))
    return pl.pallas_call(
        body, grid=(D // 128,),
        in_specs=[pair, pair, pl.BlockSpec((T, 128), lambda p: (0, D // 128 + p)), pair,
                  pl.BlockSpec((1, T, 2), lambda p: (p, 0, 0)), pl.BlockSpec((1, 2, T), lambda p: (p, 0, 0)),
                  pair, pair, cq, ck],
        out_specs=[pair, pair, pair, cq, ck],
        out_shape=[jax.ShapeDtypeStruct((T, D), BF16), jax.ShapeDtypeStruct((T, D), F32), jax.ShapeDtypeStruct((T, D), F32),
                   jax.ShapeDtypeStruct((D // 128, T, 128), F32), jax.ShapeDtypeStruct((D // 128, 8, T), F32)],
        scratch_shapes=[pltpu.VMEM((2, T, 128), BF16)],
        compiler_params=_params(("parallel",), VMEM_BIG), name="fox_bwd",
    )(q, kv, kv, do, c_col, c_row, *init)


def _tri(lower):
    r = lax.broadcasted_iota(jnp.int32, (BLK, BLK), 0)
    c = lax.broadcasted_iota(jnp.int32, (BLK, BLK), 1)
    return jnp.where((c <= r) if lower else (c >= r), 1.0, 0.0).astype(BF16)


def _gates_fwd(z, b):
    def body(z_ref, b_ref, c_ref):
        tri = _tri(True)
        carry = jnp.zeros((1, 128), F32)
        for i in range(T // BLK):
            rows = slice(i * BLK, (i + 1) * BLK)
            x = z_ref[rows, :] + b_ref[...]
            logf = jnp.minimum(x, 0.0) - jnp.log(1.0 + jnp.exp(-jnp.abs(x)))
            hi, mid, lo = _split3(logf)
            y = _dot(tri, hi, NN) + _dot(tri, mid, NN) + _dot(tri, lo, NN) + carry
            c_ref[rows, :] = y
            carry = y[BLK - 1:BLK, :]

    return pl.pallas_call(body, out_shape=jax.ShapeDtypeStruct((T, 128), F32), name="gates_fwd")(z, b)


def _gates_bwd(dc, z, b):
    def body(dc_ref, z_ref, b_ref, dz_ref, db_ref):
        tri = _tri(False)
        carry = jnp.zeros((1, 128), F32)
        db = jnp.zeros((1, 128), F32)
        for i in reversed(range(T // BLK)):
            rows = slice(i * BLK, (i + 1) * BLK)
            hi, mid, lo = _split3(dc_ref[rows, :])
            dlogf = _dot(tri, hi, NN) + _dot(tri, mid, NN) + _dot(tri, lo, NN) + carry
            carry = dlogf[0:1, :]
            x = z_ref[rows, :] + b_ref[...]
            dz = dlogf / (1.0 + jnp.exp(x))
            dz_ref[rows, :] = dz.astype(BF16)
            db = db + jnp.sum(dz, axis=0, keepdims=True)
        db_ref[...] = db

    return pl.pallas_call(
        body, out_shape=[jax.ShapeDtypeStruct((T, 128), BF16), jax.ShapeDtypeStruct((1, 128), F32)], name="gates_bwd",
    )(dc, z, b)


def _conv_pair(a_refs, cw_refs, cb_refs):
    row = lax.broadcasted_iota(jnp.int32, (T, CT), 0)
    outs = []
    for a_ref, cw_ref, cb_ref in zip(a_refs, cw_refs, cb_refs):
        z = a_ref[...]
        z1 = jnp.where(row >= 1, pltpu.roll(z, 1, 0), 0.0)
        z2 = jnp.where(row >= 2, pltpu.roll(z, 2, 0), 0.0)
        y = cw_ref[2:3, :] * z + cw_ref[1:2, :] * z1 + cw_ref[0:1, :] * z2 + cb_ref[...]
        outs.append((y, z, z1, z2))
    return outs


_GELU_K = math.sqrt(2.0 / math.pi)
N_CT = D_FF // CT


def _conv_specs():
    def at(rows, off):
        return pl.BlockSpec((rows, CT), lambda j: (0, j + off))
    return [at(T, 0), at(T, N_CT), at(3, 0), at(3, N_CT), at(1, 0), at(1, N_CT)]


def _convgate_fwd(a, cw, cb):
    def body(ag_ref, av_ref, wg_ref, wv_ref, bg_ref, bv_ref, u_ref):
        (g, _, _, _), (v, _, _, _) = _conv_pair((ag_ref, av_ref), (wg_ref, wv_ref), (bg_ref, bv_ref))
        th = jnp.tanh(_GELU_K * (g + 0.044715 * g * g * g))
        u_ref[...] = (0.5 * g * (1.0 + th) * v).astype(BF16)

    return pl.pallas_call(
        body, grid=(N_CT,), in_specs=_conv_specs(),
        out_specs=pl.BlockSpec((T, CT), lambda j: (0, j)), out_shape=jax.ShapeDtypeStruct((T, D_FF), BF16),
        compiler_params=_params(("parallel",), VMEM_BIG), name="convgate_fwd",
    )(a, a, cw, cw, cb, cb)


def _convgate_bwd(a, du, cw, cb):
    def body(ag_ref, av_ref, wg_ref, wv_ref, bg_ref, bv_ref, du_ref, da_ref, dcw_ref, dcb_ref):
        (g, gz, gz1, gz2), (v, vz, vz1, vz2) = _conv_pair((ag_ref, av_ref), (wg_ref, wv_ref), (bg_ref, bv_ref))
        du = du_ref[...].astype(F32)
        th = jnp.tanh(_GELU_K * (g + 0.044715 * g * g * g))
        gelu = 0.5 * g * (1.0 + th)
        dgelu = 0.5 * (1.0 + th) + 0.5 * g * (1.0 - th * th) * _GELU_K * (1.0 + 3 * 0.044715 * g * g)
        row = lax.broadcasted_iota(jnp.int32, (T, CT), 0)
        for h, (d, z, z1, z2, w_ref) in enumerate(((du * v * dgelu, gz, gz1, gz2, wg_ref), (du * gelu, vz, vz1, vz2, wv_ref))):
            d1 = jnp.where(row < T - 1, pltpu.roll(d, T - 1, 0), 0.0)
            d2 = jnp.where(row < T - 2, pltpu.roll(d, T - 2, 0), 0.0)
            da_ref[h] = (w_ref[2:3, :] * d + w_ref[1:2, :] * d1 + w_ref[0:1, :] * d2).astype(BF16)
            dcw_ref[h, 0:1, :] = jnp.sum(d * z2, axis=0, keepdims=True)
            dcw_ref[h, 1:2, :] = jnp.sum(d * z1, axis=0, keepdims=True)
            dcw_ref[h, 2:3, :] = jnp.sum(d * z, axis=0, keepdims=True)
            dcb_ref[h] = jnp.sum(d, axis=0, keepdims=True)

    def both(rows):
        return pl.BlockSpec((2, rows, CT), lambda j: (0, 0, j))

    return pl.pallas_call(
        body, grid=(N_CT,),
        in_specs=_conv_specs() + [pl.BlockSpec((T, CT), lambda j: (0, j))],
        out_specs=[both(T), both(3), both(1)],
        out_shape=[jax.ShapeDtypeStruct((2, T, D_FF), BF16), jax.ShapeDtypeStruct((2, 3, D_FF), F32),
                   jax.ShapeDtypeStruct((2, 1, D_FF), F32)],
        compiler_params=_params(("parallel",), VMEM_BIG), name="convgate_bwd",
    )(a, a, cw, cw, cb, cb, du)


def _halves_a(tm, tn, tk):
    per = D_FF // tk
    return lambda i, j, k: (lax.div(k, per), i, lax.rem(k, per))


def _halves_b(tm, tn, tk):
    per = D_FF // tn
    return lambda i, j, k: (lax.div(j, per), k, lax.rem(j, per))


def _adamw(w, m, v, g, *, name):
    r, c = w.shape
    tr = r
    if r * c > 256 * 1024:
        for cand in range(8, r, 8):
            if r % cand == 0 and cand * c <= 256 * 1024:
                tr = cand

    def body(w_ref, m_ref, v_ref, g_ref, d_ref, nm_ref, nv_ref):
        gv = g_ref[...]
        mn = ADAM_B1 * m_ref[...] + (1.0 - ADAM_B1) * gv
        vn = ADAM_B2 * v_ref[...] + (1.0 - ADAM_B2) * (gv * gv)
        m_hat = mn / (1.0 - ADAM_B1 ** ADAM_STEP)
        v_hat = vn / (1.0 - ADAM_B2 ** ADAM_STEP)
        d_ref[...] = -ADAM_LR * (m_hat / (jnp.sqrt(v_hat) + ADAM_EPS) + ADAM_WD * w_ref[...])
        nm_ref[...] = mn
        nv_ref[...] = vn

    blk = pl.BlockSpec((tr, c), lambda i: (i, 0))
    shp = jax.ShapeDtypeStruct((r, c), F32)
    return pl.pallas_call(
        body, grid=(r // tr,), in_specs=[blk] * 4, out_specs=[blk] * 3, out_shape=[shp] * 3,
        compiler_params=_params(("parallel",)), name=name,
    )(w, m, v, g)


def _place():
    x, y, c = lax.axis_index("x"), lax.axis_index("y"), lax.axis_index("c")
    chips = [(1 - x, y), (x, 1 - y), (1 - x, 1 - y)]
    return x, y, c, chips


def _window(ref, kind, s, half=None):
    lead = () if half is None else (half,)
    b, c = ref.shape[-2], ref.shape[-1]
    if kind == "col":
        return ref.at[lead + (slice(None), slice(None), pl.ds(s * (c // N_CHIPS), c // N_CHIPS))]
    if kind == "row":
        return ref.at[lead + (slice(None), pl.ds(s * (b // N_CHIPS), b // N_CHIPS), slice(None))]
    return ref.at[lead + (s,)]


def _window_shape(shape3, kind):
    a, b, c = shape3
    return {"col": (a, b, c // N_CHIPS), "row": (a, b // N_CHIPS, c), "slab": (b, c)}[kind]


def _allgather(tensors, kinds, *, name):
    n = len(tensors)

    def body(*refs):
        bufs = refs[n:2 * n]
        send, recv = refs[2 * n:]
        x, y, c, chips = _place()
        me = 2 * x + y
        sib = (x, y, 1 - c)

        def rcopy(i, k, win, to):
            return pltpu.make_async_remote_copy(src_ref=win, dst_ref=win, send_sem=send.at[i * 6 + k], recv_sem=recv.at[i * 6 + k],
                                                device_id=to, device_id_type=MESH)

        started = []
        for i in range(n):
            for k, (px, py) in enumerate(chips):
                cp = rcopy(i, k, _window(bufs[i], kinds[i], me, c), (px, py, c))
                cp.start()
                started.append(cp)
        for i in range(n):
            for k, (px, py) in enumerate(chips):
                landed = _window(bufs[i], kinds[i], 2 * px + py, c)
                rcopy(i, k, landed, (px, py, c)).wait_recv()
                fw = rcopy(i, 3 + k, landed, sib)
                fw.start()
                started.append(fw)
        for i in range(n):
            for k, (px, py) in enumerate(chips):
                rcopy(i, 3 + k, _window(bufs[i], kinds[i], 2 * px + py, 1 - c), sib).wait_recv()
        for cp in started:
            cp.wait_send()

    return pl.pallas_call(
        body, in_specs=[ANY] * n, out_specs=[ANY] * n,
        out_shape=[jax.ShapeDtypeStruct(t.shape, t.dtype) for t in tensors],
        scratch_shapes=[pltpu.SemaphoreType.DMA((6 * n,)), pltpu.SemaphoreType.DMA((6 * n,))],
        input_output_aliases={i: i for i in range(n)},
        name=name,
    )(*tensors)


def _rows_tile(rows, cols, sub):
    best = None
    for t in range(sub, rows + 1, sub):
        if rows % t == 0 and t * cols <= 512 * 1024:
            best = t
    return rows if best is None else best


def _sequencer(name, cid, n_sems, peers_of, body):
    @pl.kernel(mesh=plsc.ScalarSubcoreMesh(axis_name="seq", num_cores=1), name=name,
               scratch_types=(pltpu.SemaphoreType.DMA((n_sems,)), pltpu.SemaphoreType.DMA((n_sems,))),
               compiler_params=pltpu.CompilerParams(collective_id=cid))
    def launch(send, recv):
        x, y, c, chips = _place()
        peers = peers_of(x, y, c, chips)
        barrier = pltpu.get_barrier_semaphore()
        for peer in peers:
            pl.semaphore_signal(barrier, inc=1, device_id=peer, device_id_type=MESH)
        pl.semaphore_wait(barrier, len(peers))
        body(send, recv)

    launch()


def _half_of_full(ref, kind, h):
    if kind == "col":
        b = ref.shape[0]
        return ref.at[pl.ds(h * (b // 2), b // 2), :]
    if kind == "row":
        c = ref.shape[1]
        return ref.at[:, pl.ds(h * (c // 2), c // 2)]
    b = ref.shape[1]
    return ref.at[:, pl.ds(h * (b // 2), b // 2), :]


def _half_shape(full, kind):
    if kind == "col":
        return (full[0] // 2, full[1])
    if kind == "row":
        return (full[0], full[1] // 2)
    return (full[0], full[1] // 2, full[2])


def _win_of_half(ref, kind, s):
    if kind == "col":
        c = ref.shape[1]
        return ref.at[:, pl.ds(s * (c // N_CHIPS), c // N_CHIPS)]
    if kind == "row":
        b = ref.shape[0]
        return ref.at[pl.ds(s * (b // N_CHIPS), b // N_CHIPS), :]
    return ref.at[s]


def _win_shape(half, kind):
    if kind == "col":
        return (half[0], half[1] // N_CHIPS)
    if kind == "row":
        return (half[0] // N_CHIPS, half[1])
    return half[1:]


def _seq_swap(parts, kinds, *, name):
    n = len(parts)
    srcs = [jax.new_ref(p, memory_space=pltpu.MemorySpace.HBM) for p in parts]
    outs = [jax.empty_ref(jax.ShapeDtypeStruct(_half_shape(p.shape, k), p.dtype), memory_space=pltpu.MemorySpace.HBM)
            for p, k in zip(parts, kinds)]

    def body(send, recv):
        x, y, c, _ = _place()
        cps = []
        for i in range(n):
            cp = pltpu.make_async_remote_copy(src_ref=_half_of_full(srcs[i], kinds[i], 1 - c), dst_ref=outs[i], send_sem=send.at[i],
                                              recv_sem=recv.at[i], device_id=(x, y, 1 - c), device_id_type=MESH)
            cp.start()
            cps.append(cp)
        for cp in cps:
            cp.wait()

    _sequencer(name, 2, n, lambda x, y, c, chips: [(x, y, 1 - c)], body)
    return [o[...] for o in outs]


def _seq_scatter(halves, kinds, *, name):
    n = len(halves)
    srcs = [jax.new_ref(h, memory_space=pltpu.MemorySpace.HBM) for h in halves]
    outs = [jax.empty_ref(jax.ShapeDtypeStruct((3,) + _win_shape(h.shape, k), h.dtype), memory_space=pltpu.MemorySpace.HBM)
            for h, k in zip(halves, kinds)]

    def body(send, recv):
        x, y, c, chips = _place()
        cps = []
        for i in range(n):
            for k, (px, py) in enumerate(chips):
                cp = pltpu.make_async_remote_copy(src_ref=_win_of_half(srcs[i], kinds[i], 2 * px + py), dst_ref=outs[i].at[k],
                                                  send_sem=send.at[3 * i + k], recv_sem=recv.at[3 * i + k],
                                                  device_id=(px, py, c), device_id_type=MESH)
                cp.start()
                cps.append(cp)
        for cp in cps:
            cp.wait()

    _sequencer(name, 3, 3 * n, lambda x, y, c, chips: [(px, py, c) for px, py in chips], body)
    return [o[...] for o in outs]


def _add_half(g, p, kind, where, *, name):
    if kind == "slab":
        s, b2, c = p.shape
        tr = _rows_tile(b2, c, 16)
        nr = b2 // tr
        grid = (s, nr)
        g_spec = pl.BlockSpec((None, tr, c), lambda i, r, w: (i, w[1] * nr + r, 0))
        p_spec = pl.BlockSpec((None, tr, c), lambda i, r, w: (i, r, 0))
    elif kind == "col":
        b2, c = p.shape
        tr = _rows_tile(b2, c, 16)
        nr = b2 // tr
        grid = (1, nr)
        g_spec = pl.BlockSpec((tr, c), lambda i, r, w: (w[1] * nr + r, 0))
        p_spec = pl.BlockSpec((tr, c), lambda i, r, w: (r, 0))
    else:
        b, c2 = p.shape
        tr = _rows_tile(b, c2, 16)
        grid = (1, b // tr)
        g_spec = pl.BlockSpec((tr, c2), lambda i, r, w: (r, w[1]))
        p_spec = pl.BlockSpec((tr, c2), lambda i, r, w: (r, 0))

    def body(w_ref, g_ref, p_ref, o_ref):
        o_ref[...] = (g_ref[...].astype(F32) + p_ref[...].astype(F32)).astype(o_ref.dtype)

    return pl.pallas_call(
        body,
        grid_spec=pltpu.PrefetchScalarGridSpec(num_scalar_prefetch=1, grid=grid, in_specs=[g_spec, p_spec], out_specs=p_spec),
        out_shape=jax.ShapeDtypeStruct(p.shape, g.dtype),
        compiler_params=_params(("parallel", "parallel")), name=name,
    )(where, g, p)


def _sum_chips(r, h, kind, where, layer, layers, out_buf, *, name):
    _, br, cr = r.shape
    tr = _rows_tile(br, cr, 16)
    nr = br // tr
    if kind == "col":
        h_spec = pl.BlockSpec((tr, cr), lambda j, w: (j, w[0]))
        o_shape, o_spec = (layers, 2 * br, cr), pl.BlockSpec((None, tr, cr), lambda j, w: (layer, w[1] * nr + j, 0))
    elif kind == "row":
        h_spec = pl.BlockSpec((tr, cr), lambda j, w: (w[0] * nr + j, 0))
        o_shape, o_spec = (layers, br, 2 * cr), pl.BlockSpec((None, tr, cr), lambda j, w: (layer, j, w[1]))
    else:
        h_spec = pl.BlockSpec((None, tr, cr), lambda j, w: (w[0], j, 0))
        o_shape, o_spec = (layers, 2 * br, cr), pl.BlockSpec((None, tr, cr), lambda j, w: (layer, w[1] * nr + j, 0))

    def body(w_ref, h_ref, r0_ref, r1_ref, r2_ref, *rest):
        o_ref = rest[-1]
        o_ref[...] = ((h_ref[...].astype(F32) + r0_ref[...].astype(F32)) + r1_ref[...].astype(F32)) + r2_ref[...].astype(F32)

    def slot(k):
        return pl.BlockSpec((None, tr, cr), lambda j, w: (k, j, 0))

    ins, specs, alias = [h, r, r, r], [h_spec, slot(0), slot(1), slot(2)], {}
    if out_buf is not None:
        alias = {1 + len(ins): 0}
        ins.append(out_buf)
        specs.append(ANY)
    return pl.pallas_call(
        body,
        grid_spec=pltpu.PrefetchScalarGridSpec(num_scalar_prefetch=1, grid=(nr,), in_specs=specs, out_specs=o_spec),
        out_shape=jax.ShapeDtypeStruct(o_shape, F32), input_output_aliases=alias,
        compiler_params=_params(("parallel",)), name=name,
    )(where, *ins)


def _join_halves(tensors, kinds, *, name):
    n = len(tensors)

    def mine(ref, kind, h):
        if kind == "row":
            c = ref.shape[2]
            return ref.at[:, :, pl.ds(h * (c // 2), c // 2)]
        b = ref.shape[1]
        return ref.at[:, pl.ds(h * (b // 2), b // 2), :]

    def body(*refs):
        bufs = refs[n:2 * n]
        send, recv = refs[2 * n:]
        x, y, c, _ = _place()
        cps = []
        for i in range(n):
            part = mine(bufs[i], kinds[i], c)
            cp = pltpu.make_async_remote_copy(src_ref=part, dst_ref=part, send_sem=send.at[i],
                                              recv_sem=recv.at[i], device_id=(x, y, 1 - c), device_id_type=MESH)
            cp.start()
            cps.append(cp)
        for i in range(n):
            other = mine(bufs[i], kinds[i], 1 - c)
            pltpu.make_async_remote_copy(src_ref=other, dst_ref=other, send_sem=send.at[i],
                                         recv_sem=recv.at[i], device_id=(x, y, 1 - c), device_id_type=MESH).wait_recv()
        for cp in cps:
            cp.wait_send()

    return pl.pallas_call(
        body, in_specs=[ANY] * n, out_specs=[ANY] * n,
        out_shape=[jax.ShapeDtypeStruct(t.shape, t.dtype) for t in tensors],
        scratch_shapes=[pltpu.SemaphoreType.DMA((n,)), pltpu.SemaphoreType.DMA((n,))],
        input_output_aliases={i: i for i in range(n)},
        name=name,
    )(*tensors)


def _win(ref, kind, s, h=None):
    if kind == "col":
        b, c = ref.shape
        cols = pl.ds(s * (c // N_CHIPS), c // N_CHIPS)
        return ref.at[:, cols] if h is None else ref.at[pl.ds(h * (b // 2), b // 2), cols]
    if kind == "row":
        b, c = ref.shape
        rows = pl.ds(s * (b // N_CHIPS), b // N_CHIPS)
        return ref.at[rows, :] if h is None else ref.at[rows, pl.ds(h * (c // 2), c // 2)]
    b = ref.shape[1]
    return ref.at[s] if h is None else ref.at[s, pl.ds(h * (b // 2), b // 2)]


def _half(ref, kind, h):
    b, c = ref.shape
    if kind == "row":
        return ref.at[:, pl.ds(h * (c // 2), c // 2)]
    return ref.at[pl.ds(h * (b // 2), b // 2), :]


def _full_shape(shard_shape, kind):
    b, c = shard_shape
    return {"col": (b, N_CHIPS * c), "row": (N_CHIPS * b, c), "slab": (N_CHIPS, b, c)}[kind]


def _gather_body(srcs, outs, kinds, send, recv):
    x, y, c, chips = _place()
    me = 2 * x + y
    sib = (x, y, 1 - c)

    def rcopy(i, k, src, dst, to):
        return pltpu.make_async_remote_copy(src_ref=src, dst_ref=dst, send_sem=send.at[7 * i + k], recv_sem=recv.at[7 * i + k],
                                            device_id=to, device_id_type=MESH)

    started = []
    for i, (src, out, kind) in enumerate(zip(srcs, outs, kinds)):
        own = rcopy(i, 6, src, _win(out, kind, me), sib)
        own.start()
        started.append(own)
        for k, (px, py) in enumerate(chips):
            cp = rcopy(i, k, _half(src, kind, c), _win(out, kind, me, c), (px, py, c))
            cp.start()
            started.append(cp)
    for i, (out, kind) in enumerate(zip(outs, kinds)):
        for k, (px, py) in enumerate(chips):
            landed = _win(out, kind, 2 * px + py, c)
            rcopy(i, k, landed, landed, (px, py, c)).wait_recv()
            fw = rcopy(i, 3 + k, landed, landed, sib)
            fw.start()
            started.append(fw)
    for i, (src, out, kind) in enumerate(zip(srcs, outs, kinds)):
        for k, (px, py) in enumerate(chips):
            other = _win(out, kind, 2 * px + py, 1 - c)
            rcopy(i, 3 + k, other, other, sib).wait_recv()
        rcopy(i, 6, src, _win(out, kind, me), sib).wait_recv()
    for cp in started:
        cp.wait_send()


def _seq_gather(shards, kinds, *, name, cid):
    n = len(shards)
    srcs = [jax.new_ref(s, memory_space=pltpu.MemorySpace.HBM) for s in shards]
    outs = [jax.empty_ref(jax.ShapeDtypeStruct(_full_shape(s.shape, k), s.dtype), memory_space=pltpu.MemorySpace.HBM)
            for s, k in zip(shards, kinds)]

    @pl.kernel(mesh=plsc.ScalarSubcoreMesh(axis_name="seq", num_cores=1), name=name,
               scratch_types=(pltpu.SemaphoreType.DMA((7 * n,)), pltpu.SemaphoreType.DMA((7 * n,))),
               compiler_params=pltpu.CompilerParams(collective_id=cid))
    def launch(send, recv):
        x, y, c, chips = _place()
        barrier = pltpu.get_barrier_semaphore()
        for px, py in chips:
            pl.semaphore_signal(barrier, inc=1, device_id=(px, py, c), device_id_type=MESH)
        pl.semaphore_signal(barrier, inc=1, device_id=(x, y, 1 - c), device_id_type=MESH)
        pl.semaphore_wait(barrier, 4)
        _gather_body(srcs, outs, kinds, send, recv)

    launch()
    return [o[...] for o in outs]


KIND = dict(w_qkv_a="slab", w_o_a="col", w_q_b="row", w_o_b="row", w_kvf="slab", w_up="col", w_down="row", small="slab")
LAYERS = dict(w_qkv_a=N_A, w_o_a=N_A, w_q_b=DEPTH - N_A, w_o_b=DEPTH - N_A, w_kvf=1, w_up=DEPTH, w_down=DEPTH, small=1)
SMALL_W = 1792
SMALL_ROWS = 8


class _Reducer:
    def __init__(self, where):
        self.where = where
        self.acc = {nm: None for nm in KIND}

    def __call__(self, group, tag):
        names, layers, parts = zip(*group)
        kinds = [KIND[nm] for nm in names]
        sib = _seq_swap(list(parts), kinds, name="reduce_swap_" + tag)
        halves = [_add_half(g, p, k, self.where, name="reduce_add_" + nm) for g, p, k, nm in zip(parts, sib, kinds, names)]
        landed = _seq_scatter(halves, kinds, name="reduce_scatter_" + tag)
        for nm, l, r, h, k in zip(names, layers, landed, halves, kinds):
            self.acc[nm] = _sum_chips(r, h, k, self.where, l, LAYERS[nm], self.acc[nm], name="reduce_sum_" + nm)

    def finish(self):
        names = list(KIND)
        joined = _join_halves([self.acc[nm] for nm in names], [KIND[nm] for nm in names], name="reduce_pair_join")
        return dict(zip(names, joined))


def _headsum_matrix():
    r = lax.broadcasted_iota(jnp.int32, (GW, GW), 0) // HD
    c = lax.broadcasted_iota(jnp.int32, (GW, GW), 1) // HD
    return jnp.where(r == c, 1.0, 0.0).astype(BF16)


def kernel(x, norm_gains, w_qkv_a, w_o_a, w_q_b, w_o_b, kv_norm, w_kvf, b_f, w_up, conv_w, conv_b, w_down, loss_target, m_norm_gains, m_w_qkv_a, m_w_o_a, m_w_q_b, m_w_o_b, m_kv_norm, m_w_kvf, m_b_f, m_w_up, m_conv_w, m_conv_b, m_w_down, v_norm_gains, v_w_qkv_a, v_w_o_a, v_w_q_b, v_w_o_b, v_kv_norm, v_w_kvf, v_b_f, v_w_up, v_conv_w, v_conv_b, v_w_down):
    xi, yi, ci = lax.axis_index("x"), lax.axis_index("y"), lax.axis_index("c")
    chip = 2 * xi + yi
    where = jnp.stack([chip, ci]).astype(jnp.int32)
    ws = dict(norm_gains=norm_gains, w_qkv_a=w_qkv_a, w_o_a=w_o_a, w_q_b=w_q_b, w_o_b=w_o_b, kv_norm=kv_norm, w_kvf=w_kvf,
              b_f=b_f, w_up=w_up, conv_w=conv_w, conv_b=conv_b, w_down=w_down)
    ms = dict(norm_gains=m_norm_gains, w_qkv_a=m_w_qkv_a, w_o_a=m_w_o_a, w_q_b=m_w_q_b, w_o_b=m_w_o_b, kv_norm=m_kv_norm,
              w_kvf=m_w_kvf, b_f=m_b_f, w_up=m_w_up, conv_w=m_conv_w, conv_b=m_conv_b, w_down=m_w_down)
    vs = dict(norm_gains=v_norm_gains, w_qkv_a=v_w_qkv_a, w_o_a=v_w_o_a, w_q_b=v_w_q_b, w_o_b=v_w_o_b, kv_norm=v_kv_norm,
              w_kvf=v_w_kvf, b_f=v_b_f, w_up=v_w_up, conv_w=v_conv_w, conv_b=v_conv_b, w_down=v_w_down)

    small = jnp.concatenate([
        jnp.pad(norm_gains.reshape(16, 256), ((0, 0), (0, 1408 - 256))),
        jnp.pad(conv_w.reshape(12, 1408), ((0, 4), (0, 0)))], axis=0)
    big = [nm for nm in KIND if nm != "small"]
    half = {nm: ws[nm].astype(BF16) for nm in big}
    W = {nm: [None] * LAYERS[nm] for nm in big if nm != "w_kvf"}
    g_small = None
    for l in range(DEPTH):
        j = l - N_A
        group = [("w_up", "col", l), ("w_down", "row", l)]
        group += [("w_qkv_a", "slab", l), ("w_o_a", "col", l)] if l < N_A else [("w_q_b", "row", j), ("w_o_b", "row", j)]
        shards = [half[nm][i] for nm, _, i in group] + ([half["w_kvf"]] if l == N_A else []) + ([small] if l == 0 else [])
        kinds = [k for _, k, _ in group] + (["slab"] if l == N_A else []) + (["slab"] if l == 0 else [])
        got = _seq_gather(shards, kinds, name="gather_layer%d" % l, cid=1)
        for (nm, _, i), g in zip(group, got):
            W[nm][i] = g.transpose(1, 0, 2).reshape(D, 3 * A_W) if nm == "w_qkv_a" else g
        if l == N_A:
            W["w_kvf"] = got[len(group)].transpose(1, 0, 2).reshape(D, 2 * D + 16)
        if l == 0:
            g_small = got[-1]
    gains = g_small[:, :16, :256].transpose(1, 0, 2).reshape(DEPTH, 4, 1, D)
    cw_full = g_small[:, 16:28, :].transpose(1, 0, 2).reshape(DEPTH, 3, 2 * D_FF)
    cb_full = conv_b.reshape(DEPTH, 1, 2 * D_FF)

    reducer = _Reducer(where)
    sq, dh = _fwd_bwd(x[0], loss_target[0], W, gains, cw_full, cb_full, kv_norm, b_f, reducer)
    loss = lax.psum(sq[0, 0] * (0.5 / D), ("x", "y", "c"))
    return _update(loss, dh[None], reducer.finish(), chip, ws, ms, vs)


def _fwd_bwd(h, target, W, gains, cw_full, cb_full, kv_norm, b_f, reduce):
    w_kv = W["w_kvf"][:, :2 * D]
    w_kvf_pad = jnp.pad(W["w_kvf"], ((0, 0), (0, 128 - 16)))
    w_f = w_kvf_pad[:, 2 * D:]
    kvn_g = kv_norm.reshape(1, D)
    bf_pad = jnp.pad(b_f, (0, 128 - 16)).reshape(1, 128)
    tabs = _rope_tables()
    headsum = _headsum_matrix()

    saved = []
    kv = zf = c_col = c_row = kvn = h_kv = None
    for l in range(DEPTH):
        s = {"h": h}
        g = gains[l]
        xn = _rms_fwd(h, g[0], out_dtype=BF16, name="rms_in")
        s["xn"] = xn
        if l < N_A:
            qkv = _matmul(xn, W["w_qkv_a"][l], mode="nn", out_dtype=F32, name="mm_qkv", mnk=(T, 3 * A_W, D), tn=768)
            q3, k3, v3 = _rope_fwd(qkv, tabs)
            qp, kp, vp = _perm(q3), _perm(k3), _perm(v3)
            o_p, lse_p = _band_fwd(qp, kp, vp)
            o3, lse3 = _unperm(o_p), _unperm(lse_p)
            att = _combine_fwd(o3, lse3)
            s.update(qp=qp, kp=kp, vp=vp, o3=o3, lse3=lse3, lse_p=lse_p, att=att)
            mix = _matmul(att, W["w_o_a"][l], mode="nn", out_dtype=F32, name="mm_oa", mnk=(T, D, A_W))
        else:
            j = l - N_A
            if l == N_A:
                h_kv = h
                kvn = _rms_fwd(h, kvn_g, out_dtype=BF16, name="rms_in")
                kv = _matmul(kvn, w_kv, mode="nn", out_dtype=BF16, name="mm_kv")
                zf = _matmul(kvn, w_f, mode="nn", out_dtype=F32, name="mm_f")
                cum = _gates_fwd(zf, bf_pad)[:, :16]
                c_col = cum.reshape(T, 8, 2).transpose(1, 0, 2)
                c_row = cum.T.reshape(8, 2, T)
            q = _matmul(xn, W["w_q_b"][j], mode="nn", out_dtype=BF16, name="mm_qb", mnk=(T, D, D), alpha=HD ** -0.5)
            o = _fox_fwd(q, kv, c_col, c_row)
            s.update(q=q, o=o)
            mix = _matmul(o, W["w_o_b"][j], mode="nn", out_dtype=F32, name="mm_ob", mnk=(T, D, D))
        s["mix"] = mix
        h1 = _rms_fwd(mix, g[1], res=h, out_dtype=F32, name="rms_res")
        xn2 = _rms_fwd(h1, g[2], out_dtype=BF16, name="rms_in")
        a = _matmul(xn2, W["w_up"][l], mode="nn", out_dtype=F32, name="mm_up", mnk=(T, 2 * D_FF, D))
        u = _convgate_fwd(a, cw_full[l], cb_full[l])
        f = _matmul(u, W["w_down"][l], mode="nn", out_dtype=F32, name="mm_down", mnk=(T, D, D_FF), tm=1024, tk=D_FF)
        h = _rms_fwd(f, g[3], res=h1, out_dtype=F32, name="rms_res")
        s.update(h1=h1, xn2=xn2, a=a, u=u, f=f)
        saved.append(s)

    dh, sq = _loss_head(h, target)

    d_gains = [[None] * 4 for _ in range(DEPTH)]
    d_cw, d_cb = [None] * DEPTH, [None] * DEPTH
    zeros_td = jnp.zeros((T, D), F32)
    fox_acc = (zeros_td, zeros_td, jnp.zeros((D // 128, T, 128), F32), jnp.zeros((D // 128, 8, T), F32))
    d_kvnorm = d_bf = None

    def dw(nm, a, b, **kw):
        return _matmul(a, b, mode="tn", out_dtype=BF16, name="mm_dw_" + nm, **kw)

    def slabs(full, width):
        return full.reshape(full.shape[0], N_CHIPS, width).transpose(1, 0, 2)

    for l in reversed(range(DEPTH)):
        s = saved[l]
        g = gains[l]
        df, d_gains[l][3] = _rms_bwd(dh, s["f"], g[3], out_dtype=BF16, name="rms_bwd")
        du = _matmul(df, W["w_down"][l], mode="nt", out_dtype=F32, name="mm_down_dx", mnk=(T, D_FF, D), tn=256)
        g_down = dw("w_down", s["u"], df, tm=1408, tn=1024)
        da, d_cw[l], d_cb[l] = _convgate_bwd(s["a"], du, cw_full[l], cb_full[l])
        dxn2 = _matmul(da, W["w_up"][l], mode="nt", out_dtype=F32, name="mm_up_dx", mnk=(T, D, 2 * D_FF), tm=1024, tn=1024, tk=1408,
                       a_map=_halves_a)
        g_up = dw("w_up", s["xn2"], da, mnk=(D, 2 * D_FF, T), tn=1408, b_map=_halves_b)
        reduce([("w_down", l, g_down), ("w_up", l, g_up)], "ffn%d" % l)
        dh1, d_gains[l][2] = _rms_bwd(dxn2, s["h1"], g[2], dres=dh, out_dtype=F32, name="rms_bwd_res")
        dmix, d_gains[l][1] = _rms_bwd(dh1, s["mix"], g[1], out_dtype=BF16, name="rms_bwd")
        if l < N_A:
            datt = _matmul(dmix, W["w_o_a"][l], mode="nt", out_dtype=F32, name="mm_oa_dx", mnk=(T, A_W, D), tn=768)
            g_o = dw("w_o_a", s["att"], dmix, tm=768, tn=1024)
            do3, dlt3 = _combine_bwd(datt, s["o3"], s["lse3"], headsum)
            dqp, dkp, dvp = _band_bwd(s["qp"], s["kp"], s["vp"], _perm(do3), s["lse_p"], _perm(dlt3))
            dqkv = _rope_bwd(_unperm(dqp), _unperm(dkp), _unperm(dvp), tabs)
            dxn = _matmul(dqkv, W["w_qkv_a"][l], mode="nt", out_dtype=F32, name="mm_qkv_dx", mnk=(T, D, 3 * A_W), tm=1024, tn=1024, tk=3 * A_W)
            g_qkv = dw("w_qkv_a", s["xn"], dqkv, tn=768)
            group = [("w_o_a", l, g_o), ("w_qkv_a", l, slabs(g_qkv, 576))]
        else:
            j = l - N_A
            do = _matmul(dmix, W["w_o_b"][j], mode="nt", out_dtype=BF16, name="mm_ob_dx", mnk=(T, D, D))
            g_o = dw("w_o_b", s["o"], dmix, tn=1024)
            dq, *fox_acc = _fox_bwd(s["q"], kv, do, c_col, c_row, fox_acc)
            dxn = _matmul(dq, W["w_q_b"][j], mode="nt", out_dtype=F32, name="mm_qb_dx", mnk=(T, D, D))
            g_q = dw("w_q_b", s["xn"], dq, tn=1024)
            group = [("w_o_b", j, g_o), ("w_q_b", j, g_q)]
        dh, d_gains[l][0] = _rms_bwd(dxn, s["h"], g[0], dres=dh1, out_dtype=F32, name="rms_bwd_res")
        if l == N_A:
            dk, dv, dcq, dck = fox_acc
            dc16 = dcq[:, :, :2].transpose(1, 0, 2).reshape(T, 16) - dck[:, :2, :].reshape(16, T).T
            dzf, d_bf = _gates_bwd(jnp.pad(dc16, ((0, 0), (0, 128 - 16))), zf, bf_pad)
            dkvf = jnp.concatenate([dk.astype(BF16), dv.astype(BF16), dzf], axis=1)
            g_kvf = _matmul(kvn, dkvf, mode="tn", out_dtype=BF16, name="mm_kvf_dw", tm=512, tn=2 * D + 128)[:, :2 * D + 16]
            dkvn = _matmul(dkvf, w_kvf_pad, mode="nt", out_dtype=F32, name="mm_kvf_dx", tm=1024, tn=1024, tk=2 * D + 128)
            dh, d_kvnorm = _rms_bwd(dkvn, h_kv, kvn_g, dres=dh, out_dtype=F32, name="rms_bwd_res")
            group.append(("w_kvf", 0, slabs(g_kvf, 516)))
        reduce(group, "mix%d" % l)
    small_flat = jnp.concatenate([
        jnp.stack([jnp.stack(r) for r in d_gains]).reshape(-1),
        jnp.stack(d_cw).transpose(0, 2, 1, 3).reshape(-1),
        jnp.stack(d_cb).reshape(-1),
        d_kvnorm.reshape(-1), d_bf[0, :16]])
    small = jnp.pad(small_flat, (0, 2 * N_CHIPS * SMALL_ROWS * SMALL_W - small_flat.shape[0]))
    reduce([("small", 0, small.reshape(N_CHIPS, 2 * SMALL_ROWS, SMALL_W))], "small")
    return sq, dh


def _update(loss, grad_x, reduced, chip, ws, ms, vs):
    red_s = reduced.pop("small")
    buf_s = lax.dynamic_update_slice(jnp.zeros((2, N_CHIPS, SMALL_ROWS, SMALL_W), F32), red_s.reshape(2, 1, SMALL_ROWS, SMALL_W),
                                     (0, chip, 0, 0))
    (all_s,) = _allgather([buf_s], ["slab"], name="gather_small_grads")
    sflat = all_s.transpose(1, 0, 2, 3).reshape(-1)

    grads = {nm: r.reshape(ws[nm].shape) for nm, r in reduced.items()}
    o = 0
    g_gains_full = sflat[o:o + 16 * D].reshape(DEPTH, 4, D); o += 16 * D
    g_cw_full = sflat[o:o + 12 * 2 * D_FF].reshape(DEPTH, 3, 2 * D_FF); o += 12 * 2 * D_FF
    grads["conv_b"] = sflat[o:o + 4 * 2 * D_FF].reshape(DEPTH, 2 * D_FF); o += 4 * 2 * D_FF
    grads["kv_norm"] = sflat[o:o + D]; o += D
    grads["b_f"] = sflat[o:o + 16]
    grads["norm_gains"] = lax.dynamic_slice_in_dim(g_gains_full, chip * 256, 256, axis=2)
    grads["conv_w"] = lax.dynamic_slice_in_dim(g_cw_full, chip * 1408, 1408, axis=2)

    names = ["norm_gains", "w_qkv_a", "w_o_a", "w_q_b", "w_o_b", "kv_norm", "w_kvf", "b_f", "w_up", "conv_w", "conv_b", "w_down"]
    deltas, new_m, new_v = {}, {}, {}
    for nm in names:
        shp = ws[nm].shape
        two = (math.prod(shp[:-1]), shp[-1]) if len(shp) > 1 else (1, shp[0])
        d, m2, v2 = _adamw(ws[nm].reshape(two), ms[nm].reshape(two), vs[nm].reshape(two), grads[nm].reshape(two),
                           name="adamw_" + nm)
        deltas[nm], new_m[nm], new_v[nm] = d.reshape(shp), m2.reshape(shp), v2.reshape(shp)

    return (loss, grad_x, *[grads[nm] for nm in names], *[deltas[nm] for nm in names],
            *[new_m[nm] for nm in names], *[new_v[nm] for nm in names])
```

```python
import math

import jax
import jax.numpy as jnp
from jax import lax
from jax.experimental import pallas as pl
from jax.experimental.pallas import tpu as pltpu
from jax.experimental.pallas import tpu_sc as plsc

F32 = jnp.float32
BF16 = jnp.bfloat16
MESH = pl.DeviceIdType.MESH
ANY = pl.BlockSpec(memory_space=pl.ANY)

T = 2048
D = 1024
HD = 64
DEPTH = 4
N_A = 2
A_W = 768
GW = 256
DIL = (1, 4, 16)
BLK = 128
D_FF = 2816
ROPE_THETA = 500000.0
EPS = 1e-6
NEG = -1e30
N_CHIPS = 4
FQ = 256
CT = 128
VMEM_BIG = 48 * 1024 * 1024

ADAM_LR, ADAM_B1, ADAM_B2, ADAM_EPS, ADAM_WD, ADAM_STEP = 0.001, 0.9, 0.999, 1e-08, 0.01, 10

NN = (((1,), (0,)), ((), ()))
NT = (((1,), (1,)), ((), ()))
TN = (((0,), (0,)), ((), ()))


def _dot(a, b, dims):
    return lax.dot_general(a, b, dims, preferred_element_type=F32)


def _pick(dim, pref):
    if dim <= pref:
        return dim
    best = None
    for t in range(128, pref + 1, 128):
        if dim % t == 0:
            best = t
    assert best is not None, (dim, pref)
    return best


def _params(sem=None, vmem=None):
    kw = {}
    if sem is not None:
        kw["dimension_semantics"] = sem
    if vmem is not None:
        kw["vmem_limit_bytes"] = vmem
    return pltpu.CompilerParams(**kw)


def _matmul(a, b, *, mode, out_dtype, name, mnk=None, alpha=None, tm=2048, tn=512, tk=2048,
            a_map=None, b_map=None, acc_init=None, out_slab=None, out_slabs=None, out_buf=None):
    if mnk is not None:
        M, N, K = mnk
    elif mode == "nn":
        (M, K), (_, N) = a.shape, b.shape
    elif mode == "nt":
        (M, K), (N, _) = a.shape, b.shape
    else:
        (K, M), (_, N) = a.shape, b.shape
    tm, tn, tk = _pick(M, tm), _pick(N, tn), _pick(K, tk)
    nk = K // tk
    dims = {"nn": NN, "nt": NT, "tn": TN}[mode]
    n_in = 2 + (acc_init is not None) + (out_buf is not None)

    def body(*refs):
        a_ref, b_ref = refs[0], refs[1]
        o_ref = refs[n_in]
        k = pl.program_id(2)

        def finish(r):
            if alpha is not None:
                r = r * alpha
            o_ref[...] = r.astype(out_dtype)

        def product():
            r = _dot(a_ref[...], b_ref[...], dims)
            return r if acc_init is None else r + refs[2][...]

        if nk == 1:
            finish(product())
            return
        acc_ref = refs[n_in + 1]

        @pl.when(k == 0)
        def _():
            acc_ref[...] = product()

        @pl.when((k > 0) & (k < nk - 1))
        def _():
            acc_ref[...] += _dot(a_ref[...], b_ref[...], dims)

        @pl.when(k == nk - 1)
        def _():
            finish(acc_ref[...] + _dot(a_ref[...], b_ref[...], dims))

    a_blk = (tk, tm) if mode == "tn" else (tm, tk)
    b_blk = (tn, tk) if mode == "nt" else (tk, tn)
    if a_map is not None:
        a_spec = pl.BlockSpec((None,) + a_blk, a_map(tm, tn, tk))
    elif mode == "tn":
        a_spec = pl.BlockSpec(a_blk, lambda i, j, k: (k, i))
    else:
        a_spec = pl.BlockSpec(a_blk, lambda i, j, k: (i, k))
    if b_map is not None:
        b_spec = pl.BlockSpec((None,) + b_blk, b_map(tm, tn, tk))
    elif mode == "nt":
        b_spec = pl.BlockSpec(b_blk, lambda i, j, k: (j, k))
    else:
        b_spec = pl.BlockSpec(b_blk, lambda i, j, k: (k, j))
    ins, specs, alias = [a, b], [a_spec, b_spec], {}
    if acc_init is not None:
        ins.append(acc_init)
        specs.append(pl.BlockSpec((tm, tn), lambda i, j, k: (i, j)))
    if out_buf is not None:
        alias = {len(ins): 0}
        ins.append(out_buf)
        specs.append(ANY)
    if out_slab is None:
        o_spec = pl.BlockSpec((tm, tn), lambda i, j, k: (i, j))
        o_shape = jax.ShapeDtypeStruct((M, N), out_dtype)
    else:
        o_spec = pl.BlockSpec((None, tm, tn), lambda i, j, k: (out_slab, i, j))
        o_shape = jax.ShapeDtypeStruct((out_slabs, M, N), out_dtype)
    return pl.pallas_call(
        body,
        grid=(M // tm, N // tn, nk),
        in_specs=specs,
        out_specs=o_spec,
        out_shape=o_shape,
        scratch_shapes=[pltpu.VMEM((tm, tn), F32)] if nk > 1 else [],
        input_output_aliases=alias,
        compiler_params=_params(("parallel", "parallel", "arbitrary"), VMEM_BIG),
        name=name,
    )(*ins)


def _slab(l, mode):
    if mode == "nt":
        return lambda tm, tn, tk: (lambda i, j, k: (l, j, k))
    return lambda tm, tn, tk: (lambda i, j, k: (l, k, j))


def _rms_fwd(x, g, *, out_dtype, name, res=None, tr=256):
    n, d = x.shape

    def body(*refs):
        x_ref, g_ref = refs[0], refs[1]
        o_ref = refs[-1]
        xv = x_ref[...].astype(F32)
        y = xv * lax.rsqrt(jnp.mean(xv * xv, axis=-1, keepdims=True) + EPS) * g_ref[...]
        if res is not None:
            y = y + refs[2][...]
        o_ref[...] = y.astype(out_dtype)

    row = pl.BlockSpec((tr, d), lambda i: (i, 0))
    vec = pl.BlockSpec((1, d), lambda i: (0, 0))
    ins = [x, g] + ([] if res is None else [res])
    specs = [row, vec] + ([] if res is None else [row])
    return pl.pallas_call(
        body, grid=(n // tr,), in_specs=specs, out_specs=row,
        out_shape=jax.ShapeDtypeStruct((n, d), out_dtype),
        compiler_params=_params(("parallel",)), name=name,
    )(*ins)


def _rms_bwd(dy, x, g, *, out_dtype, name, dres=None, tr=256):
    n, d = x.shape

    def body(*refs):
        dy_ref, x_ref, g_ref = refs[0], refs[1], refs[2]
        dx_ref, dg_ref = refs[-2], refs[-1]
        xv = x_ref[...].astype(F32)
        dyv = dy_ref[...].astype(F32)
        rstd = lax.rsqrt(jnp.mean(xv * xv, axis=-1, keepdims=True) + EPS)
        xhat = xv * rstd
        dxh = dyv * g_ref[...]
        dx = rstd * (dxh - xhat * jnp.mean(dxh * xhat, axis=-1, keepdims=True))
        if dres is not None:
            dx = dx + refs[3][...]
        dx_ref[...] = dx.astype(out_dtype)

        @pl.when(pl.program_id(0) == 0)
        def _():
            dg_ref[...] = jnp.zeros_like(dg_ref)

        dg_ref[...] += jnp.sum(dyv * xhat, axis=0, keepdims=True)

    row = pl.BlockSpec((tr, d), lambda i: (i, 0))
    vec = pl.BlockSpec((1, d), lambda i: (0, 0))
    ins = [dy, x, g] + ([] if dres is None else [dres])
    specs = [row, row, vec] + ([] if dres is None else [row])
    return pl.pallas_call(
        body, grid=(n // tr,), in_specs=specs, out_specs=[row, vec],
        out_shape=[jax.ShapeDtypeStruct((n, d), out_dtype), jax.ShapeDtypeStruct((1, d), F32)],
        compiler_params=_params(("arbitrary",)), name=name,
    )(*ins)


def _loss_head(h, target, *, tr=256):
    n, d = h.shape

    def body(h_ref, t_ref, dh_ref, s_ref):
        err = h_ref[...] - t_ref[...]
        dh_ref[...] = err * (1.0 / d)

        @pl.when(pl.program_id(0) == 0)
        def _():
            s_ref[...] = jnp.zeros_like(s_ref)

        s_ref[...] += jnp.sum(err * err)

    row = pl.BlockSpec((tr, d), lambda i: (i, 0))
    acc = pl.BlockSpec((8, 128), lambda i: (0, 0))
    return pl.pallas_call(
        body, grid=(n // tr,), in_specs=[row, row], out_specs=[row, acc],
        out_shape=[jax.ShapeDtypeStruct((n, d), F32), jax.ShapeDtypeStruct((8, 128), F32)],
        compiler_params=_params(("arbitrary",)), name="loss_head",
    )(h, target)


def _rope_tables():
    pos = jnp.arange(T, dtype=F32)
    inv = ROPE_THETA ** (-jnp.arange(0, 16, 2, dtype=F32) / 16)
    ang = pos[:, None] * inv[None, :]
    cos, sin = jnp.cos(ang), jnp.sin(ang)
    one = jnp.ones((T, HD - 16), F32)
    zero8 = jnp.zeros((T, 8), F32)
    zero = jnp.zeros((T, HD - 16), F32)
    c = jnp.concatenate([cos, cos, one], axis=1)
    s1 = jnp.concatenate([zero8, sin, zero], axis=1)
    s2 = jnp.concatenate([-sin, zero8, zero], axis=1)
    return tuple(jnp.concatenate([t, t], axis=1) for t in (c, s1, s2))


def _rope_fwd(qkv, tabs, *, tr=256):
    def body(x_ref, c_ref, s1_ref, s2_ref, q_ref, k_ref, v_ref):
        c, s1, s2 = c_ref[...], s1_ref[...], s2_ref[...]
        for which, o_ref, scale in ((0, q_ref, HD ** -0.5), (1, k_ref, None)):
            for j in range(A_W // 128):
                x = x_ref[:, which * A_W + j * 128: which * A_W + (j + 1) * 128]
                y = x * c + pltpu.roll(x, 8, 1) * s1 + pltpu.roll(x, 120, 1) * s2
                if scale is not None:
                    y = y * scale
                o_ref[j // 2, :, (j % 2) * 128:(j % 2 + 1) * 128] = y.astype(BF16)
        for j in range(A_W // 128):
            v_ref[j // 2, :, (j % 2) * 128:(j % 2 + 1) * 128] = x_ref[:, 2 * A_W + j * 128: 2 * A_W + (j + 1) * 128].astype(BF16)

    tab = pl.BlockSpec((tr, 128), lambda i: (i, 0))
    out = pl.BlockSpec((3, tr, GW), lambda i: (0, i, 0))
    shp = jax.ShapeDtypeStruct((3, T, GW), BF16)
    return pl.pallas_call(
        body, grid=(T // tr,), in_specs=[pl.BlockSpec((tr, 3 * A_W), lambda i: (i, 0)), tab, tab, tab],
        out_specs=[out, out, out], out_shape=[shp, shp, shp],
        compiler_params=_params(("parallel",)), name="rope_fwd",
    )(qkv, *tabs)


def _rope_bwd(dq, dk, dv, tabs, *, tr=256):
    def body(dq_ref, dk_ref, dv_ref, c_ref, s1_ref, s2_ref, o_ref):
        c, s1, s2 = c_ref[...], s1_ref[...], s2_ref[...]
        for which, i_ref, scale in ((0, dq_ref, HD ** -0.5), (1, dk_ref, None)):
            for j in range(A_W // 128):
                g = i_ref[j // 2, :, (j % 2) * 128:(j % 2 + 1) * 128]
                y = g * c + pltpu.roll(g * s1, 120, 1) + pltpu.roll(g * s2, 8, 1)
                if scale is not None:
                    y = y * scale
                o_ref[:, which * A_W + j * 128: which * A_W + (j + 1) * 128] = y.astype(BF16)
        for j in range(A_W // 128):
            o_ref[:, 2 * A_W + j * 128: 2 * A_W + (j + 1) * 128] = dv_ref[j // 2, :, (j % 2) * 128:(j % 2 + 1) * 128].astype(BF16)

    tab = pl.BlockSpec((tr, 128), lambda i: (i, 0))
    cot = pl.BlockSpec((3, tr, GW), lambda i: (0, i, 0))
    return pl.pallas_call(
        body, grid=(T // tr,), in_specs=[cot, cot, cot, tab, tab, tab],
        out_specs=pl.BlockSpec((tr, 3 * A_W), lambda i: (i, 0)),
        out_shape=jax.ShapeDtypeStruct((T, 3 * A_W), BF16),
        compiler_params=_params(("parallel",)), name="rope_bwd",
    )(dq, dk, dv, *tabs)


def _perm(x3):
    out = [x3[0]]
    for g in (1, 2):
        r = DIL[g]
        out.append(x3[g].reshape(T // r, r, GW).transpose(1, 0, 2).reshape(T, GW))
    return jnp.stack(out)


def _unperm(x3):
    out = [x3[0]]
    for g in (1, 2):
        r = DIL[g]
        out.append(x3[g].reshape(r, T // r, GW).transpose(1, 0, 2).reshape(T, GW))
    return jnp.stack(out)


def _head_mask(x, lane_lo):
    lane = lax.broadcasted_iota(jnp.int32, x.shape, 1)
    keep = (lane < HD) if lane_lo else (lane >= HD)
    return jnp.where(keep, x.astype(F32), 0.0).astype(BF16)


def _band_scalars():
    g, b = pl.program_id(0), pl.program_id(1)
    nbs = lax.shift_right_logical(jnp.int32(T // BLK), 2 * g)
    has_prev = jnp.where((b & (nbs - 1)) != 0, 1, 0)
    next_ok = jnp.where(((b + 1) & (nbs - 1)) != 0, 1, 0)
    return has_prev, next_ok


def _band_mask_q(has_prev):
    row = lax.broadcasted_iota(jnp.int32, (BLK, 2 * BLK), 0)
    col = lax.broadcasted_iota(jnp.int32, (BLK, 2 * BLK), 1)
    return ((col < BLK) & (col >= row) & (has_prev == 1)) | ((col >= BLK) & (col - BLK <= row))


def _band_mask_k(next_ok):
    row = lax.broadcasted_iota(jnp.int32, (2 * BLK, BLK), 0)
    col = lax.broadcasted_iota(jnp.int32, (2 * BLK, BLK), 1)
    return ((row < BLK) & (col <= row)) | ((row >= BLK) & (col >= row - BLK) & (next_ok == 1))


def _band_fwd(q, k, v):
    nb = T // BLK

    def body(q_ref, kc_ref, kp_ref, vc_ref, vp_ref, o_ref, l_ref):
        has_prev, _ = _band_scalars()
        mask = _band_mask_q(has_prev)
        lane = lax.broadcasted_iota(jnp.int32, (BLK, 128), 1)
        for p in range(2):
            sl = slice(128 * p, 128 * (p + 1))
            qp = q_ref[0, :, sl]
            kcat = jnp.concatenate([kp_ref[0, :, sl], kc_ref[0, :, sl]], axis=0)
            vcat = jnp.concatenate([vp_ref[0, :, sl], vc_ref[0, :, sl]], axis=0)
            o_acc = jnp.zeros((BLK, 128), F32)
            lse = jnp.zeros((BLK, 128), F32)
            for e in range(2):
                s = _dot(_head_mask(qp, e == 0), kcat, NT)
                s = jnp.where(mask, s, NEG)
                m = jnp.max(s, axis=-1, keepdims=True)
                pr = jnp.exp(s - m)
                l = jnp.sum(pr, axis=-1, keepdims=True)
                o_acc = o_acc + _dot(pr.astype(BF16), _head_mask(vcat, e == 0), NN) / l
                lse = jnp.where((lane < HD) if e == 0 else (lane >= HD), m + jnp.log(l), lse)
            o_ref[0, :, sl] = o_acc
            l_ref[0, :, sl] = lse

    cur = pl.BlockSpec((1, BLK, GW), lambda g, b: (g, b, 0))
    prev = pl.BlockSpec((1, BLK, GW), lambda g, b: (g, jnp.maximum(b - 1, 0), 0))
    shp = jax.ShapeDtypeStruct((3, T, GW), F32)
    return pl.pallas_call(
        body, grid=(3, nb), in_specs=[cur, cur, prev, cur, prev], out_specs=[cur, cur], out_shape=[shp, shp],
        compiler_params=_params(("parallel", "parallel")), name="band_fwd",
    )(q, k, k, v, v)


def _band_bwd(q, k, v, do, lse, dlt):
    nb = T // BLK

    def body(qc_ref, qn_ref, kc_ref, kp_ref, vc_ref, vp_ref, doc_ref, don_ref, lc_ref, ln_ref, dc_ref, dn_ref,
             dq_ref, dk_ref, dv_ref):
        has_prev, next_ok = _band_scalars()
        mask_q = _band_mask_q(has_prev)
        mask_k = _band_mask_k(next_ok)
        for p in range(2):
            sl = slice(128 * p, 128 * (p + 1))
            qc, qn = qc_ref[0, :, sl], qn_ref[0, :, sl]
            doc, don = doc_ref[0, :, sl], don_ref[0, :, sl]
            kc, vc = kc_ref[0, :, sl], vc_ref[0, :, sl]
            kcat = jnp.concatenate([kp_ref[0, :, sl], kc], axis=0)
            vcat = jnp.concatenate([vp_ref[0, :, sl], vc], axis=0)
            qcat = jnp.concatenate([qc, qn], axis=0)
            docat = jnp.concatenate([doc, don], axis=0)
            dq = jnp.zeros((BLK, 128), F32)
            dk = jnp.zeros((BLK, 128), F32)
            dv = jnp.zeros((BLK, 128), F32)
            for e in range(2):
                lo = e == 0
                col = slice(128 * p + HD * e, 128 * p + HD * e + 1)
                lse_c, lse_n = lc_ref[0, :, col], ln_ref[0, :, col]
                dl_c, dl_n = dc_ref[0, :, col], dn_ref[0, :, col]
                s = jnp.where(mask_q, _dot(_head_mask(qc, lo), kcat, NT), NEG)
                pr = jnp.exp(s - lse_c)
                dp = _dot(_head_mask(doc, lo), vcat, NT)
                ds = pr * (dp - dl_c)
                dq = dq + _dot(ds.astype(BF16), _head_mask(kcat, lo), NN)
                qm, dom = _head_mask(qcat, lo), _head_mask(docat, lo)
                s2 = jnp.where(mask_k, _dot(qm, kc, NT), NEG)
                p2 = jnp.exp(s2 - jnp.concatenate([lse_c, lse_n], axis=0))
                dv = dv + _dot(p2.astype(BF16), dom, TN)
                dp2 = _dot(dom, vc, NT)
                ds2 = p2 * (dp2 - jnp.concatenate([dl_c, dl_n], axis=0))
                dk = dk + _dot(ds2.astype(BF16), qm, TN)
            dq_ref[0, :, sl] = dq
            dk_ref[0, :, sl] = dk
            dv_ref[0, :, sl] = dv

    cur = pl.BlockSpec((1, BLK, GW), lambda g, b: (g, b, 0))
    prev = pl.BlockSpec((1, BLK, GW), lambda g, b: (g, jnp.maximum(b - 1, 0), 0))
    nxt = pl.BlockSpec((1, BLK, GW), lambda g, b: (g, jnp.minimum(b + 1, nb - 1), 0))
    shp = jax.ShapeDtypeStruct((3, T, GW), F32)
    return pl.pallas_call(
        body, grid=(3, nb),
        in_specs=[cur, nxt, cur, prev, cur, prev, cur, nxt, cur, nxt, cur, nxt],
        out_specs=[cur, cur, cur], out_shape=[shp, shp, shp],
        compiler_params=_params(("parallel", "parallel")), name="band_bwd",
    )(q, q, k, k, v, v, do, do, lse, lse, dlt, dlt)


def _split3(x):
    hi = x.astype(BF16)
    r = x - hi.astype(F32)
    mid = r.astype(BF16)
    lo = (r - mid.astype(F32)).astype(BF16)
    return hi, mid, lo


def _dot3(x, m, dims=NN):
    hi, mid, lo = _split3(x)
    return _dot(hi, m, dims) + _dot(mid, m, dims) + _dot(lo, m, dims)


def _combine_weights(l_ref):
    l0, l1, l2 = l_ref[0], l_ref[1], l_ref[2]
    m = jnp.maximum(jnp.maximum(l0, l1), l2)
    e = [jnp.exp(l0 - m), jnp.exp(l1 - m), jnp.exp(l2 - m)]
    inv = 1.0 / (e[0] + e[1] + e[2])
    return [ei * inv for ei in e]


def _combine_fwd(o, lse, *, tr=256):
    def body(o_ref, l_ref, out_ref):
        alpha = _combine_weights(l_ref)
        for g in range(3):
            out_ref[:, g * GW:(g + 1) * GW] = (o_ref[g] * alpha[g]).astype(BF16)

    blk = pl.BlockSpec((3, tr, GW), lambda i: (0, i, 0))
    return pl.pallas_call(
        body, grid=(T // tr,), in_specs=[blk, blk], out_specs=pl.BlockSpec((tr, A_W), lambda i: (i, 0)),
        out_shape=jax.ShapeDtypeStruct((T, A_W), BF16), compiler_params=_params(("parallel",)), name="combine_fwd",
    )(o, lse)


def _combine_bwd(datt, o, lse, headsum, *, tr=256):
    def body(d_ref, o_ref, l_ref, hs_ref, do_ref, dl_ref):
        alpha = _combine_weights(l_ref)
        hs = hs_ref[...]
        total = jnp.zeros((tr, GW), F32)
        for g in range(3):
            dg = d_ref[:, g * GW:(g + 1) * GW]
            do_ref[g] = (dg * alpha[g]).astype(BF16)
            total = total + alpha[g] * _dot3(dg * o_ref[g], hs)
        for g in range(3):
            dl_ref[g] = alpha[g] * total

    blk = pl.BlockSpec((3, tr, GW), lambda i: (0, i, 0))
    return pl.pallas_call(
        body, grid=(T // tr,),
        in_specs=[pl.BlockSpec((tr, A_W), lambda i: (i, 0)), blk, blk, pl.BlockSpec((GW, GW), lambda i: (0, 0))],
        out_specs=[blk, blk],
        out_shape=[jax.ShapeDtypeStruct((3, T, GW), BF16), jax.ShapeDtypeStruct((3, T, GW), F32)],
        compiler_params=_params(("parallel",)), name="combine_bwd",
    )(datt, o, lse, headsum)


def _fox_scores(qm, k_ref, cq, ck_ref, e, i, n):
    s = _dot(qm, k_ref[0:n, :], NT) + (cq - ck_ref[0, e:e + 1, 0:n])
    row = lax.broadcasted_iota(jnp.int32, (FQ, n), 0)
    col = lax.broadcasted_iota(jnp.int32, (FQ, n), 1)
    s = jnp.where(col <= row + i * FQ, s, NEG)
    m = jnp.max(s, axis=-1, keepdims=True)
    pr = jnp.exp(s - m)
    return pr, jnp.sum(pr, axis=-1, keepdims=True)


def _fox_fwd(q, kv, c_col, c_row):
    def body(q_ref, k_ref, v_ref, cc_ref, cr_ref, o_ref, vm_ref):
        for e in range(2):
            vm_ref[e] = _head_mask(v_ref[...], e == 0)
        for i in range(T // FQ):
            n = (i + 1) * FQ
            rows = slice(i * FQ, n)
            acc = jnp.zeros((FQ, 128), F32)
            for e in range(2):
                qm = _head_mask(q_ref[rows, :], e == 0)
                pr, l = _fox_scores(qm, k_ref, cc_ref[0, rows, e:e + 1], cr_ref, e, i, n)
                acc = acc + _dot(pr.astype(BF16), vm_ref[e, 0:n, :], NN) / l
            o_ref[rows, :] = acc.astype(BF16)

    pair = pl.BlockSpec((T, 128), lambda p: (0, p))
    return pl.pallas_call(
        body, grid=(D // 128,),
        in_specs=[pair, pair, pl.BlockSpec((T, 128), lambda p: (0, D // 128 + p)),
                  pl.BlockSpec((1, T, 2), lambda p: (p, 0, 0)), pl.BlockSpec((1, 2, T), lambda p: (p, 0, 0))],
        out_specs=pair, out_shape=jax.ShapeDtypeStruct((T, D), BF16),
        scratch_shapes=[pltpu.VMEM((2, T, 128), BF16)],
        compiler_params=_params(("parallel",), VMEM_BIG), name="fox_fwd",
    )(q, kv, kv, c_col, c_row)


def _fox_bwd(q, kv, do, c_col, c_row, init):
    def body(q_ref, k_ref, v_ref, do_ref, cc_ref, cr_ref, ik_ref, iv_ref, iq_ref, ic_ref,
             dq_ref, dk_ref, dv_ref, dcq_ref, dck_ref, km_ref):
        dk_ref[...] = ik_ref[...]
        dv_ref[...] = iv_ref[...]
        dcq_ref[...] = iq_ref[...]
        dck_ref[...] = ic_ref[...]
        for e in range(2):
            km_ref[e] = _head_mask(k_ref[...], e == 0)
        for i in range(T // FQ):
            n = (i + 1) * FQ
            rows = slice(i * FQ, n)
            dq = jnp.zeros((FQ, 128), F32)
            for e in range(2):
                qm = _head_mask(q_ref[rows, :], e == 0)
                dom = _head_mask(do_ref[rows, :], e == 0)
                pr, l = _fox_scores(qm, k_ref, cc_ref[0, rows, e:e + 1], cr_ref, e, i, n)
                pr = pr / l
                dp = _dot(dom, v_ref[0:n, :], NT)
                ds = pr * (dp - jnp.sum(pr * dp, axis=-1, keepdims=True))
                dsb = ds.astype(BF16)
                dq = dq + _dot(dsb, km_ref[e, 0:n, :], NN)
                dk_ref[0:n, :] += _dot(dsb, qm, TN)
                dv_ref[0:n, :] += _dot(pr.astype(BF16), dom, TN)
                dcq_ref[0, rows, e:e + 1] += jnp.sum(ds, axis=-1, keepdims=True)
                dck_ref[0, e:e + 1, 0:n] += jnp.sum(ds, axis=0, keepdims=True)
            dq_ref[rows, :] = (dq * HD ** -0.5).astype(BF16)

    pair = pl.BlockSpec((T, 128), lambda p: (0, p))
    cq = pl.BlockSpec((1, T, 128), lambda p: (p, 0, 0))
    ck = pl.BlockSpec((1, 8, T), lambda p: (p, 0, 0))
    return pl.pallas_call(
        body, grid=(D // 128,),
        in_specs=[pair, pair, pl.BlockSpec((T, 128), lambda p: (0, D // 128 + p)), pair,
                  pl.BlockSpec((1, T, 2), lambda p: (p, 0, 0)), pl.BlockSpec((1, 2, T), lambda p: (p, 0, 0)),
                  pair, pair, cq, ck],
        out_specs=[pair, pair, pair, cq, ck],
        out_shape=[jax.ShapeDtypeStruct((T, D), BF16), jax.ShapeDtypeStruct((T, D), F32), jax.ShapeDtypeStruct((T, D), F32),
                   jax.ShapeDtypeStruct((D // 128, T, 128), F32), jax.ShapeDtypeStruct((D // 128, 8, T), F32)],
        scratch_shapes=[pltpu.VMEM((2, T, 128), BF16)],
        compiler_params=_params(("parallel",), VMEM_BIG), name="fox_bwd",
    )(q, kv, kv, do, c_col, c_row, *init)


def _tri(lower):
    r = lax.broadcasted_iota(jnp.int32, (BLK, BLK), 0)
    c = lax.broadcasted_iota(jnp.int32, (BLK, BLK), 1)
    return jnp.where((c <= r) if lower else (c >= r), 1.0, 0.0).astype(BF16)


def _gates_fwd(z, b):
    def body(z_ref, b_ref, c_ref):
        tri = _tri(True)
        carry = jnp.zeros((1, 128), F32)
        for i in range(T // BLK):
            rows = slice(i * BLK, (i + 1) * BLK)
            x = z_ref[rows, :] + b_ref[...]
            logf = jnp.minimum(x, 0.0) - jnp.log(1.0 + jnp.exp(-jnp.abs(x)))
            hi, mid, lo = _split3(logf)
            y = _dot(tri, hi, NN) + _dot(tri, mid, NN) + _dot(tri, lo, NN) + carry
            c_ref[rows, :] = y
            carry = y[BLK - 1:BLK, :]

    return pl.pallas_call(body, out_shape=jax.ShapeDtypeStruct((T, 128), F32), name="gates_fwd")(z, b)


def _gates_bwd(dc, z, b):
    def body(dc_ref, z_ref, b_ref, dz_ref, db_ref):
        tri = _tri(False)
        carry = jnp.zeros((1, 128), F32)
        db = jnp.zeros((1, 128), F32)
        for i in reversed(range(T // BLK)):
            rows = slice(i * BLK, (i + 1) * BLK)
            hi, mid, lo = _split3(dc_ref[rows, :])
            dlogf = _dot(tri, hi, NN) + _dot(tri, mid, NN) + _dot(tri, lo, NN) + carry
            carry = dlogf[0:1, :]
            x = z_ref[rows, :] + b_ref[...]
            dz = dlogf / (1.0 + jnp.exp(x))
            dz_ref[rows, :] = dz.astype(BF16)
            db = db + jnp.sum(dz, axis=0, keepdims=True)
        db_ref[...] = db

    return pl.pallas_call(
        body, out_shape=[jax.ShapeDtypeStruct((T, 128), BF16), jax.ShapeDtypeStruct((1, 128), F32)], name="gates_bwd",
    )(dc, z, b)


def _conv_pair(a_refs, cw_refs, cb_refs):
    row = lax.broadcasted_iota(jnp.int32, (T, CT), 0)
    outs = []
    for a_ref, cw_ref, cb_ref in zip(a_refs, cw_refs, cb_refs):
        z = a_ref[...]
        z1 = jnp.where(row >= 1, pltpu.roll(z, 1, 0), 0.0)
        z2 = jnp.where(row >= 2, pltpu.roll(z, 2, 0), 0.0)
        y = cw_ref[2:3, :] * z + cw_ref[1:2, :] * z1 + cw_ref[0:1, :] * z2 + cb_ref[...]
        outs.append((y, z, z1, z2))
    return outs


_GELU_K = math.sqrt(2.0 / math.pi)
N_CT = D_FF // CT


def _conv_specs():
    def at(rows, off):
        return pl.BlockSpec((rows, CT), lambda j: (0, j + off))
    return [at(T, 0), at(T, N_CT), at(3, 0), at(3, N_CT), at(1, 0), at(1, N_CT)]


def _convgate_fwd(a, cw, cb):
    def body(ag_ref, av_ref, wg_ref, wv_ref, bg_ref, bv_ref, u_ref):
        (g, _, _, _), (v, _, _, _) = _conv_pair((ag_ref, av_ref), (wg_ref, wv_ref), (bg_ref, bv_ref))
        th = jnp.tanh(_GELU_K * (g + 0.044715 * g * g * g))
        u_ref[...] = (0.5 * g * (1.0 + th) * v).astype(BF16)

    return pl.pallas_call(
        body, grid=(N_CT,), in_specs=_conv_specs(),
        out_specs=pl.BlockSpec((T, CT), lambda j: (0, j)), out_shape=jax.ShapeDtypeStruct((T, D_FF), BF16),
        compiler_params=_params(("parallel",), VMEM_BIG), name="convgate_fwd",
    )(a, a, cw, cw, cb, cb)


def _convgate_bwd(a, du, cw, cb):
    def body(ag_ref, av_ref, wg_ref, wv_ref, bg_ref, bv_ref, du_ref, da_ref, dcw_ref, dcb_ref):
        (g, gz, gz1, gz2), (v, vz, vz1, vz2) = _conv_pair((ag_ref, av_ref), (wg_ref, wv_ref), (bg_ref, bv_ref))
        du = du_ref[...].astype(F32)
        th = jnp.tanh(_GELU_K * (g + 0.044715 * g * g * g))
        gelu = 0.5 * g * (1.0 + th)
        dgelu = 0.5 * (1.0 + th) + 0.5 * g * (1.0 - th * th) * _GELU_K * (1.0 + 3 * 0.044715 * g * g)
        row = lax.broadcasted_iota(jnp.int32, (T, CT), 0)
        for h, (d, z, z1, z2, w_ref) in enumerate(((du * v * dgelu, gz, gz1, gz2, wg_ref), (du * gelu, vz, vz1, vz2, wv_ref))):
            d1 = jnp.where(row < T - 1, pltpu.roll(d, T - 1, 0), 0.0)
            d2 = jnp.where(row < T - 2, pltpu.roll(d, T - 2, 0), 0.0)
            da_ref[h] = (w_ref[2:3, :] * d + w_ref[1:2, :] * d1 + w_ref[0:1, :] * d2).astype(BF16)
            dcw_ref[h, 0:1, :] = jnp.sum(d * z2, axis=0, keepdims=True)
            dcw_ref[h, 1:2, :] = jnp.sum(d * z1, axis=0, keepdims=True)
            dcw_ref[h, 2:3, :] = jnp.sum(d * z, axis=0, keepdims=True)
            dcb_ref[h] = jnp.sum(d, axis=0, keepdims=True)

    def both(rows):
        return pl.BlockSpec((2, rows, CT), lambda j: (0, 0, j))

    return pl.pallas_call(
        body, grid=(N_CT,),
        in_specs=_conv_specs() + [pl.BlockSpec((T, CT), lambda j: (0, j))],
        out_specs=[both(T), both(3), both(1)],
        out_shape=[jax.ShapeDtypeStruct((2, T, D_FF), BF16), jax.ShapeDtypeStruct((2, 3, D_FF), F32),
                   jax.ShapeDtypeStruct((2, 1, D_FF), F32)],
        compiler_params=_params(("parallel",), VMEM_BIG), name="convgate_bwd",
    )(a, a, cw, cw, cb, cb, du)


def _halves_a(tm, tn, tk):
    per = D_FF // tk
    return lambda i, j, k: (lax.div(k, per), i, lax.rem(k, per))


def _halves_b(tm, tn, tk):
    per = D_FF // tn
    return lambda i, j, k: (lax.div(j, per), k, lax.rem(j, per))


def _adamw(w, m, v, g, *, name):
    r, c = w.shape
    tr = r
    if r * c > 256 * 1024:
        for cand in range(8, r, 8):
            if r % cand == 0 and cand * c <= 256 * 1024:
                tr = cand

    def body(w_ref, m_ref, v_ref, g_ref, d_ref, nm_ref, nv_ref):
        gv = g_ref[...]
        mn = ADAM_B1 * m_ref[...] + (1.0 - ADAM_B1) * gv
        vn = ADAM_B2 * v_ref[...] + (1.0 - ADAM_B2) * (gv * gv)
        m_hat = mn / (1.0 - ADAM_B1 ** ADAM_STEP)
        v_hat = vn / (1.0 - ADAM_B2 ** ADAM_STEP)
        d_ref[...] = -ADAM_LR * (m_hat / (jnp.sqrt(v_hat) + ADAM_EPS) + ADAM_WD * w_ref[...])
        nm_ref[...] = mn
        nv_ref[...] = vn

    blk = pl.BlockSpec((tr, c), lambda i: (i, 0))
    shp = jax.ShapeDtypeStruct((r, c), F32)
    return pl.pallas_call(
        body, grid=(r // tr,), in_specs=[blk] * 4, out_specs=[blk] * 3, out_shape=[shp] * 3,
        compiler_params=_params(("parallel",)), name=name,
    )(w, m, v, g)


def _place():
    x, y, c = lax.axis_index("x"), lax.axis_index("y"), lax.axis_index("c")
    chips = [(1 - x, y), (x, 1 - y), (1 - x, 1 - y)]
    return x, y, c, chips


def _window(ref, kind, s, half=None):
    lead = () if half is None else (half,)
    b, c = ref.shape[-2], ref.shape[-1]
    if kind == "col":
        return ref.at[lead + (slice(None), slice(None), pl.ds(s * (c // N_CHIPS), c // N_CHIPS))]
    if kind == "row":
        return ref.at[lead + (slice(None), pl.ds(s * (b // N_CHIPS), b // N_CHIPS), slice(None))]
    return ref.at[lead + (s,)]


def _window_shape(shape3, kind):
    a, b, c = shape3
    return {"col": (a, b, c // N_CHIPS), "row": (a, b // N_CHIPS, c), "slab": (b, c)}[kind]


def _allgather(tensors, kinds, *, name):
    n = len(tensors)

    def body(*refs):
        bufs = refs[n:2 * n]
        send, recv = refs[2 * n:]
        x, y, c, chips = _place()
        me = 2 * x + y
        sib = (x, y, 1 - c)

        def rcopy(i, k, win, to):
            return pltpu.make_async_remote_copy(src_ref=win, dst_ref=win, send_sem=send.at[i * 6 + k], recv_sem=recv.at[i * 6 + k],
                                                device_id=to, device_id_type=MESH)

        started = []
        for i in range(n):
            for k, (px, py) in enumerate(chips):
                cp = rcopy(i, k, _window(bufs[i], kinds[i], me, c), (px, py, c))
                cp.start()
                started.append(cp)
        for i in range(n):
            for k, (px, py) in enumerate(chips):
                landed = _window(bufs[i], kinds[i], 2 * px + py, c)
                rcopy(i, k, landed, (px, py, c)).wait_recv()
                fw = rcopy(i, 3 + k, landed, sib)
                fw.start()
                started.append(fw)
        for i in range(n):
            for k, (px, py) in enumerate(chips):
                rcopy(i, 3 + k, _window(bufs[i], kinds[i], 2 * px + py, 1 - c), sib).wait_recv()
        for cp in started:
            cp.wait_send()

    return pl.pallas_call(
        body, in_specs=[ANY] * n, out_specs=[ANY] * n,
        out_shape=[jax.ShapeDtypeStruct(t.shape, t.dtype) for t in tensors],
        scratch_shapes=[pltpu.SemaphoreType.DMA((6 * n,)), pltpu.SemaphoreType.DMA((6 * n,))],
        input_output_aliases={i: i for i in range(n)},
        name=name,
    )(*tensors)


def _rows_tile(rows, cols, sub):
    best = None
    for t in range(sub, rows + 1, sub):
        if rows % t == 0 and t * cols <= 512 * 1024:
            best = t
    return rows if best is None else best


def _sequencer(name, cid, n_sems, peers_of, body):
    @pl.kernel(mesh=plsc.ScalarSubcoreMesh(axis_name="seq", num_cores=1), name=name,
               scratch_types=(pltpu.SemaphoreType.DMA((n_sems,)), pltpu.SemaphoreType.DMA((n_sems,))),
               compiler_params=pltpu.CompilerParams(collective_id=cid))
    def launch(send, recv):
        x, y, c, chips = _place()
        peers = peers_of(x, y, c, chips)
        barrier = pltpu.get_barrier_semaphore()
        for peer in peers:
            pl.semaphore_signal(barrier, inc=1, device_id=peer, device_id_type=MESH)
        pl.semaphore_wait(barrier, len(peers))
        body(send, recv)

    launch()


def _half_of_full(ref, kind, h):
    if kind == "col":
        b = ref.shape[0]
        return ref.at[pl.ds(h * (b // 2), b // 2), :]
    if kind == "row":
        c = ref.shape[1]
        return ref.at[:, pl.ds(h * (c // 2), c // 2)]
    b = ref.shape[1]
    return ref.at[:, pl.ds(h * (b // 2), b // 2), :]


def _half_shape(full, kind):
    if kind == "col":
        return (full[0] // 2, full[1])
    if kind == "row":
        return (full[0], full[1] // 2)
    return (full[0], full[1] // 2, full[2])


def _win_of_half(ref, kind, s):
    if kind == "col":
        c = ref.shape[1]
        return ref.at[:, pl.ds(s * (c // N_CHIPS), c // N_CHIPS)]
    if kind == "row":
        b = ref.shape[0]
        return ref.at[pl.ds(s * (b // N_CHIPS), b // N_CHIPS), :]
    return ref.at[s]


def _win_shape(half, kind):
    if kind == "col":
        return (half[0], half[1] // N_CHIPS)
    if kind == "row":
        return (half[0] // N_CHIPS, half[1])
    return half[1:]


def _seq_swap(parts, kinds, *, name):
    n = len(parts)
    srcs = [jax.new_ref(p, memory_space=pltpu.MemorySpace.HBM) for p in parts]
    outs = [jax.empty_ref(jax.ShapeDtypeStruct(_half_shape(p.shape, k), p.dtype), memory_space=pltpu.MemorySpace.HBM)
            for p, k in zip(parts, kinds)]

    def body(send, recv):
        x, y, c, _ = _place()
        cps = []
        for i in range(n):
            cp = pltpu.make_async_remote_copy(src_ref=_half_of_full(srcs[i], kinds[i], 1 - c), dst_ref=outs[i], send_sem=send.at[i],
                                              recv_sem=recv.at[i], device_id=(x, y, 1 - c), device_id_type=MESH)
            cp.start()
            cps.append(cp)
        for cp in cps:
            cp.wait()

    _sequencer(name, 2, n, lambda x, y, c, chips: [(x, y, 1 - c)], body)
    return [o[...] for o in outs]


def _seq_scatter(halves, kinds, *, name):
    n = len(halves)
    srcs = [jax.new_ref(h, memory_space=pltpu.MemorySpace.HBM) for h in halves]
    outs = [jax.empty_ref(jax.ShapeDtypeStruct((3,) + _win_shape(h.shape, k), h.dtype), memory_space=pltpu.MemorySpace.HBM)
            for h, k in zip(halves, kinds)]

    def body(send, recv):
        x, y, c, chips = _place()
        cps = []
        for i in range(n):
            for k, (px, py) in enumerate(chips):
                cp = pltpu.make_async_remote_copy(src_ref=_win_of_half(srcs[i], kinds[i], 2 * px + py), dst_ref=outs[i].at[k],
                                                  send_sem=send.at[3 * i + k], recv_sem=recv.at[3 * i + k],
                                                  device_id=(px, py, c), device_id_type=MESH)
                cp.start()
                cps.append(cp)
        for cp in cps:
            cp.wait()

    _sequencer(name, 3, 3 * n, lambda x, y, c, chips: [(px, py, c) for px, py in chips], body)
    return [o[...] for o in outs]


def _add_half(g, p, kind, where, *, name):
    if kind == "slab":
        s, b2, c = p.shape
        tr = _rows_tile(b2, c, 16)
        nr = b2 // tr
        grid = (s, nr)
        g_spec = pl.BlockSpec((None, tr, c), lambda i, r, w: (i, w[1] * nr + r, 0))
        p_spec = pl.BlockSpec((None, tr, c), lambda i, r, w: (i, r, 0))
    elif kind == "col":
        b2, c = p.shape
        tr = _rows_tile(b2, c, 16)
        nr = b2 // tr
        grid = (1, nr)
        g_spec = pl.BlockSpec((tr, c), lambda i, r, w: (w[1] * nr + r, 0))
        p_spec = pl.BlockSpec((tr, c), lambda i, r, w: (r, 0))
    else:
        b, c2 = p.shape
        tr = _rows_tile(b, c2, 16)
        grid = (1, b // tr)
        g_spec = pl.BlockSpec((tr, c2), lambda i, r, w: (r, w[1]))
        p_spec = pl.BlockSpec((tr, c2), lambda i, r, w: (r, 0))

    def body(w_ref, g_ref, p_ref, o_ref):
        o_ref[...] = (g_ref[...].astype(F32) + p_ref[...].astype(F32)).astype(o_ref.dtype)

    return pl.pallas_call(
        body,
        grid_spec=pltpu.PrefetchScalarGridSpec(num_scalar_prefetch=1, grid=grid, in_specs=[g_spec, p_spec], out_specs=p_spec),
        out_shape=jax.ShapeDtypeStruct(p.shape, g.dtype),
        compiler_params=_params(("parallel", "parallel")), name=name,
    )(where, g, p)


def _sum_chips(r, h, kind, where, layer, layers, out_buf, after, *, name):
    _, br, cr = r.shape
    tr = _rows_tile(br, cr, 16)
    nr = br // tr
    if kind == "col":
        h_spec = pl.BlockSpec((tr, cr), lambda j, w: (j, w[0]))
        o_shape, o_spec = (layers, 2 * br, cr), pl.BlockSpec((None, tr, cr), lambda j, w: (layer, w[1] * nr + j, 0))
    elif kind == "row":
        h_spec = pl.BlockSpec((tr, cr), lambda j, w: (w[0] * nr + j, 0))
        o_shape, o_spec = (layers, br, 2 * cr), pl.BlockSpec((None, tr, cr), lambda j, w: (layer, j, w[1]))
    else:
        h_spec = pl.BlockSpec((None, tr, cr), lambda j, w: (w[0], j, 0))
        o_shape, o_spec = (layers, 2 * br, cr), pl.BlockSpec((None, tr, cr), lambda j, w: (layer, w[1] * nr + j, 0))

    def body(w_ref, h_ref, r0_ref, r1_ref, r2_ref, *rest):
        o_ref = rest[-1]
        o_ref[...] = ((h_ref[...].astype(F32) + r0_ref[...].astype(F32)) + r1_ref[...].astype(F32)) + r2_ref[...].astype(F32)

    def slot(k):
        return pl.BlockSpec((None, tr, cr), lambda j, w: (k, j, 0))

    ins, specs, alias = [h, r, r, r], [h_spec, slot(0), slot(1), slot(2)], {}
    if after is not None:
        ins.append(after)
        specs.append(ANY)
    if out_buf is not None:
        alias = {1 + len(ins): 0}
        ins.append(out_buf)
        specs.append(ANY)
    return pl.pallas_call(
        body,
        grid_spec=pltpu.PrefetchScalarGridSpec(num_scalar_prefetch=1, grid=(nr,), in_specs=specs, out_specs=o_spec),
        out_shape=jax.ShapeDtypeStruct(o_shape, F32), input_output_aliases=alias,
        compiler_params=_params(("parallel",)), name=name,
    )(where, *ins)


def _join_halves(tensors, kinds, *, name):
    n = len(tensors)

    def mine(ref, kind, h):
        if kind == "row":
            c = ref.shape[2]
            return ref.at[:, :, pl.ds(h * (c // 2), c // 2)]
        b = ref.shape[1]
        return ref.at[:, pl.ds(h * (b // 2), b // 2), :]

    def body(*refs):
        bufs = refs[n:2 * n]
        send, recv = refs[2 * n:]
        x, y, c, _ = _place()
        cps = []
        for i in range(n):
            part = mine(bufs[i], kinds[i], c)
            cp = pltpu.make_async_remote_copy(src_ref=part, dst_ref=part, send_sem=send.at[i],
                                              recv_sem=recv.at[i], device_id=(x, y, 1 - c), device_id_type=MESH)
            cp.start()
            cps.append(cp)
        for i in range(n):
            other = mine(bufs[i], kinds[i], 1 - c)
            pltpu.make_async_remote_copy(src_ref=other, dst_ref=other, send_sem=send.at[i],
                                         recv_sem=recv.at[i], device_id=(x, y, 1 - c), device_id_type=MESH).wait_recv()
        for cp in cps:
            cp.wait_send()

    return pl.pallas_call(
        body, in_specs=[ANY] * n, out_specs=[ANY] * n,
        out_shape=[jax.ShapeDtypeStruct(t.shape, t.dtype) for t in tensors],
        scratch_shapes=[pltpu.SemaphoreType.DMA((n,)), pltpu.SemaphoreType.DMA((n,))],
        input_output_aliases={i: i for i in range(n)},
        name=name,
    )(*tensors)


def _win(ref, kind, s, h=None):
    if kind == "col":
        b, c = ref.shape
        cols = pl.ds(s * (c // N_CHIPS), c // N_CHIPS)
        return ref.at[:, cols] if h is None else ref.at[pl.ds(h * (b // 2), b // 2), cols]
    if kind == "row":
        b, c = ref.shape
        rows = pl.ds(s * (b // N_CHIPS), b // N_CHIPS)
        return ref.at[rows, :] if h is None else ref.at[rows, pl.ds(h * (c // 2), c // 2)]
    b = ref.shape[1]
    return ref.at[s] if h is None else ref.at[s, pl.ds(h * (b // 2), b // 2)]


def _half(ref, kind, h):
    b, c = ref.shape
    if kind == "row":
        return ref.at[:, pl.ds(h * (c // 2), c // 2)]
    return ref.at[pl.ds(h * (b // 2), b // 2), :]


def _full_shape(shard_shape, kind):
    b, c = shard_shape
    return {"col": (b, N_CHIPS * c), "row": (N_CHIPS * b, c), "slab": (N_CHIPS, b, c)}[kind]


def _gather_body(srcs, outs, kinds, send, recv):
    x, y, c, chips = _place()
    me = 2 * x + y
    sib = (x, y, 1 - c)

    def rcopy(i, k, src, dst, to):
        return pltpu.make_async_remote_copy(src_ref=src, dst_ref=dst, send_sem=send.at[7 * i + k], recv_sem=recv.at[7 * i + k],
                                            device_id=to, device_id_type=MESH)

    started = []
    for i, (src, out, kind) in enumerate(zip(srcs, outs, kinds)):
        own = rcopy(i, 6, src, _win(out, kind, me), sib)
        own.start()
        started.append(own)
        for k, (px, py) in enumerate(chips):
            cp = rcopy(i, k, _half(src, kind, c), _win(out, kind, me, c), (px, py, c))
            cp.start()
            started.append(cp)
    for i, (out, kind) in enumerate(zip(outs, kinds)):
        for k, (px, py) in enumerate(chips):
            landed = _win(out, kind, 2 * px + py, c)
            rcopy(i, k, landed, landed, (px, py, c)).wait_recv()
            fw = rcopy(i, 3 + k, landed, landed, sib)
            fw.start()
            started.append(fw)
    for i, (src, out, kind) in enumerate(zip(srcs, outs, kinds)):
        for k, (px, py) in enumerate(chips):
            other = _win(out, kind, 2 * px + py, 1 - c)
            rcopy(i, 3 + k, other, other, sib).wait_recv()
        rcopy(i, 6, src, _win(out, kind, me), sib).wait_recv()
    for cp in started:
        cp.wait_send()


def _seq_gather(shards, kinds, *, name, cid):
    n = len(shards)
    srcs = [jax.new_ref(s, memory_space=pltpu.MemorySpace.HBM) for s in shards]
    outs = [jax.empty_ref(jax.ShapeDtypeStruct(_full_shape(s.shape, k), s.dtype), memory_space=pltpu.MemorySpace.HBM)
            for s, k in zip(shards, kinds)]

    @pl.kernel(mesh=plsc.ScalarSubcoreMesh(axis_name="seq", num_cores=1), name=name,
               scratch_types=(pltpu.SemaphoreType.DMA((7 * n,)), pltpu.SemaphoreType.DMA((7 * n,))),
               compiler_params=pltpu.CompilerParams(collective_id=cid))
    def launch(send, recv):
        x, y, c, chips = _place()
        barrier = pltpu.get_barrier_semaphore()
        for px, py in chips:
            pl.semaphore_signal(barrier, inc=1, device_id=(px, py, c), device_id_type=MESH)
        pl.semaphore_signal(barrier, inc=1, device_id=(x, y, 1 - c), device_id_type=MESH)
        pl.semaphore_wait(barrier, 4)
        _gather_body(srcs, outs, kinds, send, recv)

    launch()
    return [o[...] for o in outs]


KIND = dict(w_qkv_a="slab", w_o_a="col", w_q_b="row", w_o_b="row", w_kvf="slab", w_up="col", w_down="row", small="slab")
LAYERS = dict(w_qkv_a=N_A, w_o_a=N_A, w_q_b=DEPTH - N_A, w_o_b=DEPTH - N_A, w_kvf=1, w_up=DEPTH, w_down=DEPTH, small=1)
SMALL_W = 1792
SMALL_ROWS = 8


class _Reducer:
    def __init__(self, where):
        self.where = where
        self.acc = {nm: None for nm in KIND}
        self.pending = None

    def __call__(self, group, tag):
        names, layers, parts = zip(*group)
        kinds = [KIND[nm] for nm in names]
        self._sum_pending(after=parts[0])
        sib = _seq_swap(list(parts), kinds, name="reduce_swap_" + tag)
        halves = [_add_half(g, p, k, self.where, name="reduce_add_" + nm) for g, p, k, nm in zip(parts, sib, kinds, names)]
        landed = _seq_scatter(halves, kinds, name="reduce_scatter_" + tag)
        self.pending = (names, layers, landed, halves, kinds)

    def _sum_pending(self, after):
        if self.pending is None:
            return
        for nm, l, r, h, k in zip(*self.pending):
            self.acc[nm] = _sum_chips(r, h, k, self.where, l, LAYERS[nm], self.acc[nm], after, name="reduce_sum_" + nm)
        self.pending = None

    def finish(self):
        self._sum_pending(after=None)
        names = list(KIND)
        joined = _join_halves([self.acc[nm] for nm in names], [KIND[nm] for nm in names], name="reduce_pair_join")
        return dict(zip(names, joined))


def _headsum_matrix():
    r = lax.broadcasted_iota(jnp.int32, (GW, GW), 0) // HD
    c = lax.broadcasted_iota(jnp.int32, (GW, GW), 1) // HD
    return jnp.where(r == c, 1.0, 0.0).astype(BF16)


def kernel(x, norm_gains, w_qkv_a, w_o_a, w_q_b, w_o_b, kv_norm, w_kvf, b_f, w_up, conv_w, conv_b, w_down, loss_target, m_norm_gains, m_w_qkv_a, m_w_o_a, m_w_q_b, m_w_o_b, m_kv_norm, m_w_kvf, m_b_f, m_w_up, m_conv_w, m_conv_b, m_w_down, v_norm_gains, v_w_qkv_a, v_w_o_a, v_w_q_b, v_w_o_b, v_kv_norm, v_w_kvf, v_b_f, v_w_up, v_conv_w, v_conv_b, v_w_down):
    xi, yi, ci = lax.axis_index("x"), lax.axis_index("y"), lax.axis_index("c")
    chip = 2 * xi + yi
    where = jnp.stack([chip, ci]).astype(jnp.int32)
    ws = dict(norm_gains=norm_gains, w_qkv_a=w_qkv_a, w_o_a=w_o_a, w_q_b=w_q_b, w_o_b=w_o_b, kv_norm=kv_norm, w_kvf=w_kvf,
              b_f=b_f, w_up=w_up, conv_w=conv_w, conv_b=conv_b, w_down=w_down)
    ms = dict(norm_gains=m_norm_gains, w_qkv_a=m_w_qkv_a, w_o_a=m_w_o_a, w_q_b=m_w_q_b, w_o_b=m_w_o_b, kv_norm=m_kv_norm,
              w_kvf=m_w_kvf, b_f=m_b_f, w_up=m_w_up, conv_w=m_conv_w, conv_b=m_conv_b, w_down=m_w_down)
    vs = dict(norm_gains=v_norm_gains, w_qkv_a=v_w_qkv_a, w_o_a=v_w_o_a, w_q_b=v_w_q_b, w_o_b=v_w_o_b, kv_norm=v_kv_norm,
              w_kvf=v_w_kvf, b_f=v_b_f, w_up=v_w_up, conv_w=v_conv_w, conv_b=v_conv_b, w_down=v_w_down)

    small = jnp.concatenate([
        jnp.pad(norm_gains.reshape(16, 256), ((0, 0), (0, 1408 - 256))),
        jnp.pad(conv_w.reshape(12, 1408), ((0, 4), (0, 0)))], axis=0)
    big = [nm for nm in KIND if nm != "small"]
    half = {nm: ws[nm].astype(BF16) for nm in big}
    W = {nm: [None] * LAYERS[nm] for nm in big if nm != "w_kvf"}
    g_small = None
    groups = [("0a", [("w_qkv_a", 0), ("w_o_a", 0), ("small", 0)]), ("0b", [("w_up", 0), ("w_down", 0)]),
              ("1", [("w_qkv_a", 1), ("w_o_a", 1), ("w_up", 1), ("w_down", 1)]),
              ("2", [("w_kvf", 0), ("w_q_b", 0), ("w_o_b", 0), ("w_up", 2), ("w_down", 2)]),
              ("3", [("w_q_b", 1), ("w_o_b", 1), ("w_up", 3), ("w_down", 3)])]
    for tag, group in groups:
        shards = [small if nm == "small" else half[nm] if nm == "w_kvf" else half[nm][i] for nm, i in group]
        got = _seq_gather(shards, [KIND[nm] for nm, _ in group], name="gather_layer" + tag, cid=1)
        for (nm, i), g in zip(group, got):
            if nm == "small":
                g_small = g
            elif nm == "w_kvf":
                W[nm] = g.transpose(1, 0, 2).reshape(D, 2 * D + 16)
            else:
                W[nm][i] = g.transpose(1, 0, 2).reshape(D, 3 * A_W) if nm == "w_qkv_a" else g
    gains = g_small[:, :16, :256].transpose(1, 0, 2).reshape(DEPTH, 4, 1, D)
    cw_full = g_small[:, 16:28, :].transpose(1, 0, 2).reshape(DEPTH, 3, 2 * D_FF)
    cb_full = conv_b.reshape(DEPTH, 1, 2 * D_FF)

    reducer = _Reducer(where)
    sq, dh = _fwd_bwd(x[0], loss_target[0], W, gains, cw_full, cb_full, kv_norm, b_f, reducer)
    loss = lax.psum(sq[0, 0] * (0.5 / D), ("x", "y", "c"))
    return _update(loss, dh[None], reducer.finish(), chip, ws, ms, vs)


def _fwd_bwd(h, target, W, gains, cw_full, cb_full, kv_norm, b_f, reduce):
    w_kv = W["w_kvf"][:, :2 * D]
    w_kvf_pad = jnp.pad(W["w_kvf"], ((0, 0), (0, 128 - 16)))
    w_f = w_kvf_pad[:, 2 * D:]
    kvn_g = kv_norm.reshape(1, D)
    bf_pad = jnp.pad(b_f, (0, 128 - 16)).reshape(1, 128)
    tabs = _rope_tables()
    headsum = _headsum_matrix()

    saved = []
    kv = zf = c_col = c_row = kvn = h_kv = None
    for l in range(DEPTH):
        s = {"h": h}
        g = gains[l]
        xn = _rms_fwd(h, g[0], out_dtype=BF16, name="rms_in")
        s["xn"] = xn
        if l < N_A:
            qkv = _matmul(xn, W["w_qkv_a"][l], mode="nn", out_dtype=F32, name="mm_qkv", mnk=(T, 3 * A_W, D), tn=768)
            q3, k3, v3 = _rope_fwd(qkv, tabs)
            qp, kp, vp = _perm(q3), _perm(k3), _perm(v3)
            o_p, lse_p = _band_fwd(qp, kp, vp)
            o3, lse3 = _unperm(o_p), _unperm(lse_p)
            att = _combine_fwd(o3, lse3)
            s.update(qp=qp, kp=kp, vp=vp, o3=o3, lse3=lse3, lse_p=lse_p, att=att)
            mix = _matmul(att, W["w_o_a"][l], mode="nn", out_dtype=F32, name="mm_oa", mnk=(T, D, A_W))
        else:
            j = l - N_A
            if l == N_A:
                h_kv = h
                kvn = _rms_fwd(h, kvn_g, out_dtype=BF16, name="rms_in")
                kv = _matmul(kvn, w_kv, mode="nn", out_dtype=BF16, name="mm_kv")
                zf = _matmul(kvn, w_f, mode="nn", out_dtype=F32, name="mm_f")
                cum = _gates_fwd(zf, bf_pad)[:, :16]
                c_col = cum.reshape(T, 8, 2).transpose(1, 0, 2)
                c_row = cum.T.reshape(8, 2, T)
            q = _matmul(xn, W["w_q_b"][j], mode="nn", out_dtype=BF16, name="mm_qb", mnk=(T, D, D), alpha=HD ** -0.5)
            o = _fox_fwd(q, kv, c_col, c_row)
            s.update(q=q, o=o)
            mix = _matmul(o, W["w_o_b"][j], mode="nn", out_dtype=F32, name="mm_ob", mnk=(T, D, D))
        s["mix"] = mix
        h1 = _rms_fwd(mix, g[1], res=h, out_dtype=F32, name="rms_res")
        xn2 = _rms_fwd(h1, g[2], out_dtype=BF16, name="rms_in")
        a = _matmul(xn2, W["w_up"][l], mode="nn", out_dtype=F32, name="mm_up", mnk=(T, 2 * D_FF, D))
        u = _convgate_fwd(a, cw_full[l], cb_full[l])
        f = _matmul(u, W["w_down"][l], mode="nn", out_dtype=F32, name="mm_down", mnk=(T, D, D_FF), tm=1024, tk=D_FF)
        h = _rms_fwd(f, g[3], res=h1, out_dtype=F32, name="rms_res")
        s.update(h1=h1, xn2=xn2, a=a, u=u, f=f)
        saved.append(s)

    dh, sq = _loss_head(h, target)

    d_gains = [[None] * 4 for _ in range(DEPTH)]
    d_cw, d_cb = [None] * DEPTH, [None] * DEPTH
    zeros_td = jnp.zeros((T, D), F32)
    fox_acc = (zeros_td, zeros_td, jnp.zeros((D // 128, T, 128), F32), jnp.zeros((D // 128, 8, T), F32))
    d_kvnorm = d_bf = None

    def dw(nm, a, b, **kw):
        return _matmul(a, b, mode="tn", out_dtype=BF16, name="mm_dw_" + nm, **kw)

    def slabs(full, width):
        return full.reshape(full.shape[0], N_CHIPS, width).transpose(1, 0, 2)

    for l in reversed(range(DEPTH)):
        s = saved[l]
        g = gains[l]
        df, d_gains[l][3] = _rms_bwd(dh, s["f"], g[3], out_dtype=BF16, name="rms_bwd")
        du = _matmul(df, W["w_down"][l], mode="nt", out_dtype=F32, name="mm_down_dx", mnk=(T, D_FF, D), tn=256)
        g_down = dw("w_down", s["u"], df, tm=1408, tn=1024)
        da, d_cw[l], d_cb[l] = _convgate_bwd(s["a"], du, cw_full[l], cb_full[l])
        dxn2 = _matmul(da, W["w_up"][l], mode="nt", out_dtype=F32, name="mm_up_dx", mnk=(T, D, 2 * D_FF), tm=1024, tn=1024, tk=1408,
                       a_map=_halves_a)
        g_up = dw("w_up", s["xn2"], da, mnk=(D, 2 * D_FF, T), tn=1408, b_map=_halves_b)
        reduce([("w_down", l, g_down), ("w_up", l, g_up)], "ffn%d" % l)
        dh1, d_gains[l][2] = _rms_bwd(dxn2, s["h1"], g[2], dres=dh, out_dtype=F32, name="rms_bwd_res")
        dmix, d_gains[l][1] = _rms_bwd(dh1, s["mix"], g[1], out_dtype=BF16, name="rms_bwd")
        if l < N_A:
            datt = _matmul(dmix, W["w_o_a"][l], mode="nt", out_dtype=F32, name="mm_oa_dx", mnk=(T, A_W, D), tn=768)
            g_o = dw("w_o_a", s["att"], dmix, tm=768, tn=1024)
            do3, dlt3 = _combine_bwd(datt, s["o3"], s["lse3"], headsum)
            dqp, dkp, dvp = _band_bwd(s["qp"], s["kp"], s["vp"], _perm(do3), s["lse_p"], _perm(dlt3))
            dqkv = _rope_bwd(_unperm(dqp), _unperm(dkp), _unperm(dvp), tabs)
            dxn = _matmul(dqkv, W["w_qkv_a"][l], mode="nt", out_dtype=F32, name="mm_qkv_dx", mnk=(T, D, 3 * A_W), tm=1024, tn=1024, tk=3 * A_W)
            g_qkv = dw("w_qkv_a", s["xn"], dqkv, tn=768)
            group = [("w_o_a", l, g_o), ("w_qkv_a", l, slabs(g_qkv, 576))]
        else:
            j = l - N_A
            do = _matmul(dmix, W["w_o_b"][j], mode="nt", out_dtype=BF16, name="mm_ob_dx", mnk=(T, D, D))
            g_o = dw("w_o_b", s["o"], dmix, tn=1024)
            dq, *fox_acc = _fox_bwd(s["q"], kv, do, c_col, c_row, fox_acc)
            dxn = _matmul(dq, W["w_q_b"][j], mode="nt", out_dtype=F32, name="mm_qb_dx", mnk=(T, D, D))
            g_q = dw("w_q_b", s["xn"], dq, tn=1024)
            group = [("w_o_b", j, g_o), ("w_q_b", j, g_q)]
        dh, d_gains[l][0] = _rms_bwd(dxn, s["h"], g[0], dres=dh1, out_dtype=F32, name="rms_bwd_res")
        if l == N_A:
            dk, dv, dcq, dck = fox_acc
            dc16 = dcq[:, :, :2].transpose(1, 0, 2).reshape(T, 16) - dck[:, :2, :].reshape(16, T).T
            dzf, d_bf = _gates_bwd(jnp.pad(dc16, ((0, 0), (0, 128 - 16))), zf, bf_pad)
            dkvf = jnp.concatenate([dk.astype(BF16), dv.astype(BF16), dzf], axis=1)
            g_kvf = _matmul(kvn, dkvf, mode="tn", out_dtype=BF16, name="mm_kvf_dw", tm=512, tn=2 * D + 128)[:, :2 * D + 16]
            dkvn = _matmul(dkvf, w_kvf_pad, mode="nt", out_dtype=F32, name="mm_kvf_dx", tm=1024, tn=1024, tk=2 * D + 128)
            dh, d_kvnorm = _rms_bwd(dkvn, h_kv, kvn_g, dres=dh, out_dtype=F32, name="rms_bwd_res")
            group.append(("w_kvf", 0, slabs(g_kvf, 516)))
        reduce(group, "mix%d" % l)
    small_flat = jnp.concatenate([
        jnp.stack([jnp.stack(r) for r in d_gains]).reshape(-1),
        jnp.stack(d_cw).transpose(0, 2, 1, 3).reshape(-1),
        jnp.stack(d_cb).reshape(-1),
        d_kvnorm.reshape(-1), d_bf[0, :16]])
    small = jnp.pad(small_flat, (0, 2 * N_CHIPS * SMALL_ROWS * SMALL_W - small_flat.shape[0]))
    reduce([("small", 0, small.reshape(N_CHIPS, 2 * SMALL_ROWS, SMALL_W))], "small")
    return sq, dh


def _update(loss, grad_x, reduced, chip, ws, ms, vs):
    red_s = reduced.pop("small")
    buf_s = lax.dynamic_update_slice(jnp.zeros((2, N_CHIPS, SMALL_ROWS, SMALL_W), F32), red_s.reshape(2, 1, SMALL_ROWS, SMALL_W),
                                     (0, chip, 0, 0))
    (all_s,) = _allgather([buf_s], ["slab"], name="gather_small_grads")
    sflat = all_s.transpose(1, 0, 2, 3).reshape(-1)

    grads = {nm: r.reshape(ws[nm].shape) for nm, r in reduced.items()}
    o = 0
    g_gains_full = sflat[o:o + 16 * D].reshape(DEPTH, 4, D); o += 16 * D
    g_cw_full = sflat[o:o + 12 * 2 * D_FF].reshape(DEPTH, 3, 2 * D_FF); o += 12 * 2 * D_FF
    grads["conv_b"] = sflat[o:o + 4 * 2 * D_FF].reshape(DEPTH, 2 * D_FF); o += 4 * 2 * D_FF
    grads["kv_norm"] = sflat[o:o + D]; o += D
    grads["b_f"] = sflat[o:o + 16]
    grads["norm_gains"] = lax.dynamic_slice_in_dim(g_gains_full, chip * 256, 256, axis=2)
    grads["conv_w"] = lax.dynamic_slice_in_dim(g_cw_full, chip * 1408, 1408, axis=2)

    names = ["norm_gains", "w_qkv_a", "w_o_a", "w_q_b", "w_o_b", "kv_norm", "w_kvf", "b_f", "w_up", "conv_w", "conv_b", "w_down"]
    deltas, new_m, new_v = {}, {}, {}
    for nm in names:
        shp = ws[nm].shape
        two = (math.prod(shp[:-1]), shp[-1]) if len(shp) > 1 else (1, shp[0])
        d, m2, v2 = _adamw(ws[nm].reshape(two), ms[nm].reshape(two), vs[nm].reshape(two), grads[nm].reshape(two),
                           name="adamw_" + nm)
        deltas[nm], new_m[nm], new_v[nm] = d.reshape(shp), m2.reshape(shp), v2.reshape(shp)

    return (loss, grad_x, *[grads[nm] for nm in names], *[deltas[nm] for nm in names],
            *[new_m[nm] for nm in names], *[new_v[nm] for nm in names])
```

```python
import math

import jax
import jax.numpy as jnp
from jax import lax
from jax.experimental import pallas as pl
from jax.experimental.pallas import tpu as pltpu
from jax.experimental.pallas import tpu_sc as plsc

F32 = jnp.float32
BF16 = jnp.bfloat16
MESH = pl.DeviceIdType.MESH
ANY = pl.BlockSpec(memory_space=pl.ANY)

T = 2048
D = 1024
HD = 64
DEPTH = 4
N_A = 2
A_W = 768
GW = 256
DIL = (1, 4, 16)
BLK = 128
D_FF = 2816
ROPE_THETA = 500000.0
EPS = 1e-6
NEG = -1e30
N_CHIPS = 4
FQ = 256
CT = 128
VMEM_BIG = 48 * 1024 * 1024

ADAM_LR, ADAM_B1, ADAM_B2, ADAM_EPS, ADAM_WD, ADAM_STEP = 0.001, 0.9, 0.999, 1e-08, 0.01, 10

NN = (((1,), (0,)), ((), ()))
NT = (((1,), (1,)), ((), ()))
TN = (((0,), (0,)), ((), ()))


def _dot(a, b, dims):
    return lax.dot_general(a, b, dims, preferred_element_type=F32)


def _pick(dim, pref):
    if dim <= pref:
        return dim
    best = None
    for t in range(128, pref + 1, 128):
        if dim % t == 0:
            best = t
    assert best is not None, (dim, pref)
    return best


def _params(sem=None, vmem=None):
    kw = {}
    if sem is not None:
        kw["dimension_semantics"] = sem
    if vmem is not None:
        kw["vmem_limit_bytes"] = vmem
    return pltpu.CompilerParams(**kw)


def _matmul(a, b, *, mode, out_dtype, name, mnk=None, alpha=None, tm=2048, tn=512, tk=2048,
            a_map=None, b_map=None, acc_init=None, out_slab=None, out_slabs=None, out_buf=None):
    if mnk is not None:
        M, N, K = mnk
    elif mode == "nn":
        (M, K), (_, N) = a.shape, b.shape
    elif mode == "nt":
        (M, K), (N, _) = a.shape, b.shape
    else:
        (K, M), (_, N) = a.shape, b.shape
    tm, tn, tk = _pick(M, tm), _pick(N, tn), _pick(K, tk)
    nk = K // tk
    dims = {"nn": NN, "nt": NT, "tn": TN}[mode]
    n_in = 2 + (acc_init is not None) + (out_buf is not None)

    def body(*refs):
        a_ref, b_ref = refs[0], refs[1]
        o_ref = refs[n_in]
        k = pl.program_id(2)

        def finish(r):
            if alpha is not None:
                r = r * alpha
            o_ref[...] = r.astype(out_dtype)

        def product():
            r = _dot(a_ref[...], b_ref[...], dims)
            return r if acc_init is None else r + refs[2][...]

        if nk == 1:
            finish(product())
            return
        acc_ref = refs[n_in + 1]

        @pl.when(k == 0)
        def _():
            acc_ref[...] = product()

        @pl.when((k > 0) & (k < nk - 1))
        def _():
            acc_ref[...] += _dot(a_ref[...], b_ref[...], dims)

        @pl.when(k == nk - 1)
        def _():
            finish(acc_ref[...] + _dot(a_ref[...], b_ref[...], dims))

    a_blk = (tk, tm) if mode == "tn" else (tm, tk)
    b_blk = (tn, tk) if mode == "nt" else (tk, tn)
    if a_map is not None:
        a_spec = pl.BlockSpec((None,) + a_blk, a_map(tm, tn, tk))
    elif mode == "tn":
        a_spec = pl.BlockSpec(a_blk, lambda i, j, k: (k, i))
    else:
        a_spec = pl.BlockSpec(a_blk, lambda i, j, k: (i, k))
    if b_map is not None:
        b_spec = pl.BlockSpec((None,) + b_blk, b_map(tm, tn, tk))
    elif mode == "nt":
        b_spec = pl.BlockSpec(b_blk, lambda i, j, k: (j, k))
    else:
        b_spec = pl.BlockSpec(b_blk, lambda i, j, k: (k, j))
    ins, specs, alias = [a, b], [a_spec, b_spec], {}
    if acc_init is not None:
        ins.append(acc_init)
        specs.append(pl.BlockSpec((tm, tn), lambda i, j, k: (i, j)))
    if out_buf is not None:
        alias = {len(ins): 0}
        ins.append(out_buf)
        specs.append(ANY)
    if out_slab is None:
        o_spec = pl.BlockSpec((tm, tn), lambda i, j, k: (i, j))
        o_shape = jax.ShapeDtypeStruct((M, N), out_dtype)
    else:
        o_spec = pl.BlockSpec((None, tm, tn), lambda i, j, k: (out_slab, i, j))
        o_shape = jax.ShapeDtypeStruct((out_slabs, M, N), out_dtype)
    return pl.pallas_call(
        body,
        grid=(M // tm, N // tn, nk),
        in_specs=specs,
        out_specs=o_spec,
        out_shape=o_shape,
        scratch_shapes=[pltpu.VMEM((tm, tn), F32)] if nk > 1 else [],
        input_output_aliases=alias,
        compiler_params=_params(("parallel", "parallel", "arbitrary"), VMEM_BIG),
        name=name,
    )(*ins)


def _slab(l, mode):
    if mode == "nt":
        return lambda tm, tn, tk: (lambda i, j, k: (l, j, k))
    return lambda tm, tn, tk: (lambda i, j, k: (l, k, j))


def _rms_fwd(x, g, *, out_dtype, name, res=None, tr=256):
    n, d = x.shape

    def body(*refs):
        x_ref, g_ref = refs[0], refs[1]
        o_ref = refs[-1]
        xv = x_ref[...].astype(F32)
        y = xv * lax.rsqrt(jnp.mean(xv * xv, axis=-1, keepdims=True) + EPS) * g_ref[...]
        if res is not None:
            y = y + refs[2][...]
        o_ref[...] = y.astype(out_dtype)

    row = pl.BlockSpec((tr, d), lambda i: (i, 0))
    vec = pl.BlockSpec((1, d), lambda i: (0, 0))
    ins = [x, g] + ([] if res is None else [res])
    specs = [row, vec] + ([] if res is None else [row])
    return pl.pallas_call(
        body, grid=(n // tr,), in_specs=specs, out_specs=row,
        out_shape=jax.ShapeDtypeStruct((n, d), out_dtype),
        compiler_params=_params(("parallel",)), name=name,
    )(*ins)


def _rms_bwd(dy, x, g, *, out_dtype, name, dres=None, tr=256):
    n, d = x.shape

    def body(*refs):
        dy_ref, x_ref, g_ref = refs[0], refs[1], refs[2]
        dx_ref, dg_ref = refs[-2], refs[-1]
        xv = x_ref[...].astype(F32)
        dyv = dy_ref[...].astype(F32)
        rstd = lax.rsqrt(jnp.mean(xv * xv, axis=-1, keepdims=True) + EPS)
        xhat = xv * rstd
        dxh = dyv * g_ref[...]
        dx = rstd * (dxh - xhat * jnp.mean(dxh * xhat, axis=-1, keepdims=True))
        if dres is not None:
            dx = dx + refs[3][...]
        dx_ref[...] = dx.astype(out_dtype)

        @pl.when(pl.program_id(0) == 0)
        def _():
            dg_ref[...] = jnp.zeros_like(dg_ref)

        dg_ref[...] += jnp.sum(dyv * xhat, axis=0, keepdims=True)

    row = pl.BlockSpec((tr, d), lambda i: (i, 0))
    vec = pl.BlockSpec((1, d), lambda i: (0, 0))
    ins = [dy, x, g] + ([] if dres is None else [dres])
    specs = [row, row, vec] + ([] if dres is None else [row])
    return pl.pallas_call(
        body, grid=(n // tr,), in_specs=specs, out_specs=[row, vec],
        out_shape=[jax.ShapeDtypeStruct((n, d), out_dtype), jax.ShapeDtypeStruct((1, d), F32)],
        compiler_params=_params(("arbitrary",)), name=name,
    )(*ins)


def _loss_head(h, target, *, tr=256):
    n, d = h.shape

    def body(h_ref, t_ref, dh_ref, s_ref):
        err = h_ref[...] - t_ref[...]
        dh_ref[...] = err * (1.0 / d)

        @pl.when(pl.program_id(0) == 0)
        def _():
            s_ref[...] = jnp.zeros_like(s_ref)

        s_ref[...] += jnp.sum(err * err)

    row = pl.BlockSpec((tr, d), lambda i: (i, 0))
    acc = pl.BlockSpec((8, 128), lambda i: (0, 0))
    return pl.pallas_call(
        body, grid=(n // tr,), in_specs=[row, row], out_specs=[row, acc],
        out_shape=[jax.ShapeDtypeStruct((n, d), F32), jax.ShapeDtypeStruct((8, 128), F32)],
        compiler_params=_params(("arbitrary",)), name="loss_head",
    )(h, target)


def _rope_tables():
    pos = jnp.arange(T, dtype=F32)
    inv = ROPE_THETA ** (-jnp.arange(0, 16, 2, dtype=F32) / 16)
    ang = pos[:, None] * inv[None, :]
    cos, sin = jnp.cos(ang), jnp.sin(ang)
    one = jnp.ones((T, HD - 16), F32)
    zero8 = jnp.zeros((T, 8), F32)
    zero = jnp.zeros((T, HD - 16), F32)
    c = jnp.concatenate([cos, cos, one], axis=1)
    s1 = jnp.concatenate([zero8, sin, zero], axis=1)
    s2 = jnp.concatenate([-sin, zero8, zero], axis=1)
    return tuple(jnp.concatenate([t, t], axis=1) for t in (c, s1, s2))


def _rope_fwd(qkv, tabs, *, tr=256):
    def body(x_ref, c_ref, s1_ref, s2_ref, q_ref, k_ref, v_ref):
        c, s1, s2 = c_ref[...], s1_ref[...], s2_ref[...]
        for which, o_ref, scale in ((0, q_ref, HD ** -0.5), (1, k_ref, None)):
            for j in range(A_W // 128):
                x = x_ref[:, which * A_W + j * 128: which * A_W + (j + 1) * 128]
                y = x * c + pltpu.roll(x, 8, 1) * s1 + pltpu.roll(x, 120, 1) * s2
                if scale is not None:
                    y = y * scale
                o_ref[j // 2, :, (j % 2) * 128:(j % 2 + 1) * 128] = y.astype(BF16)
        for j in range(A_W // 128):
            v_ref[j // 2, :, (j % 2) * 128:(j % 2 + 1) * 128] = x_ref[:, 2 * A_W + j * 128: 2 * A_W + (j + 1) * 128].astype(BF16)

    tab = pl.BlockSpec((tr, 128), lambda i: (i, 0))
    out = pl.BlockSpec((3, tr, GW), lambda i: (0, i, 0))
    shp = jax.ShapeDtypeStruct((3, T, GW), BF16)
    return pl.pallas_call(
        body, grid=(T // tr,), in_specs=[pl.BlockSpec((tr, 3 * A_W), lambda i: (i, 0)), tab, tab, tab],
        out_specs=[out, out, out], out_shape=[shp, shp, shp],
        compiler_params=_params(("parallel",)), name="rope_fwd",
    )(qkv, *tabs)


def _rope_bwd(dq, dk, dv, tabs, *, tr=256):
    def body(dq_ref, dk_ref, dv_ref, c_ref, s1_ref, s2_ref, o_ref):
        c, s1, s2 = c_ref[...], s1_ref[...], s2_ref[...]
        for which, i_ref, scale in ((0, dq_ref, HD ** -0.5), (1, dk_ref, None)):
            for j in range(A_W // 128):
                g = i_ref[j // 2, :, (j % 2) * 128:(j % 2 + 1) * 128]
                y = g * c + pltpu.roll(g * s1, 120, 1) + pltpu.roll(g * s2, 8, 1)
                if scale is not None:
                    y = y * scale
                o_ref[:, which * A_W + j * 128: which * A_W + (j + 1) * 128] = y.astype(BF16)
        for j in range(A_W // 128):
            o_ref[:, 2 * A_W + j * 128: 2 * A_W + (j + 1) * 128] = dv_ref[j // 2, :, (j % 2) * 128:(j % 2 + 1) * 128].astype(BF16)

    tab = pl.BlockSpec((tr, 128), lambda i: (i, 0))
    cot = pl.BlockSpec((3, tr, GW), lambda i: (0, i, 0))
    return pl.pallas_call(
        body, grid=(T // tr,), in_specs=[cot, cot, cot, tab, tab, tab],
        out_specs=pl.BlockSpec((tr, 3 * A_W), lambda i: (i, 0)),
        out_shape=jax.ShapeDtypeStruct((T, 3 * A_W), BF16),
        compiler_params=_params(("parallel",)), name="rope_bwd",
    )(dq, dk, dv, *tabs)


def _perm(x3):
    out = [x3[0]]
    for g in (1, 2):
        r = DIL[g]
        out.append(x3[g].reshape(T // r, r, GW).transpose(1, 0, 2).reshape(T, GW))
    return jnp.stack(out)


def _unperm(x3):
    out = [x3[0]]
    for g in (1, 2):
        r = DIL[g]
        out.append(x3[g].reshape(r, T // r, GW).transpose(1, 0, 2).reshape(T, GW))
    return jnp.stack(out)


def _head_mask(x, lane_lo):
    lane = lax.broadcasted_iota(jnp.int32, x.shape, 1)
    keep = (lane < HD) if lane_lo else (lane >= HD)
    return jnp.where(keep, x.astype(F32), 0.0).astype(BF16)


def _band_scalars():
    g, b = pl.program_id(0), pl.program_id(1)
    nbs = lax.shift_right_logical(jnp.int32(T // BLK), 2 * g)
    has_prev = jnp.where((b & (nbs - 1)) != 0, 1, 0)
    next_ok = jnp.where(((b + 1) & (nbs - 1)) != 0, 1, 0)
    return has_prev, next_ok


def _band_mask_q(has_prev):
    row = lax.broadcasted_iota(jnp.int32, (BLK, 2 * BLK), 0)
    col = lax.broadcasted_iota(jnp.int32, (BLK, 2 * BLK), 1)
    return ((col < BLK) & (col >= row) & (has_prev == 1)) | ((col >= BLK) & (col - BLK <= row))


def _band_mask_k(next_ok):
    row = lax.broadcasted_iota(jnp.int32, (2 * BLK, BLK), 0)
    col = lax.broadcasted_iota(jnp.int32, (2 * BLK, BLK), 1)
    return ((row < BLK) & (col <= row)) | ((row >= BLK) & (col >= row - BLK) & (next_ok == 1))


def _band_fwd(q, k, v):
    nb = T // BLK

    def body(q_ref, kc_ref, kp_ref, vc_ref, vp_ref, o_ref, l_ref):
        has_prev, _ = _band_scalars()
        mask = _band_mask_q(has_prev)
        lane = lax.broadcasted_iota(jnp.int32, (BLK, 128), 1)
        for p in range(2):
            sl = slice(128 * p, 128 * (p + 1))
            qp = q_ref[0, :, sl]
            kcat = jnp.concatenate([kp_ref[0, :, sl], kc_ref[0, :, sl]], axis=0)
            vcat = jnp.concatenate([vp_ref[0, :, sl], vc_ref[0, :, sl]], axis=0)
            o_acc = jnp.zeros((BLK, 128), F32)
            lse = jnp.zeros((BLK, 128), F32)
            for e in range(2):
                s = _dot(_head_mask(qp, e == 0), kcat, NT)
                s = jnp.where(mask, s, NEG)
                m = jnp.max(s, axis=-1, keepdims=True)
                pr = jnp.exp(s - m)
                l = jnp.sum(pr, axis=-1, keepdims=True)
                o_acc = o_acc + _dot(pr.astype(BF16), _head_mask(vcat, e == 0), NN) / l
                lse = jnp.where((lane < HD) if e == 0 else (lane >= HD), m + jnp.log(l), lse)
            o_ref[0, :, sl] = o_acc
            l_ref[0, :, sl] = lse

    cur = pl.BlockSpec((1, BLK, GW), lambda g, b: (g, b, 0))
    prev = pl.BlockSpec((1, BLK, GW), lambda g, b: (g, jnp.maximum(b - 1, 0), 0))
    shp = jax.ShapeDtypeStruct((3, T, GW), F32)
    return pl.pallas_call(
        body, grid=(3, nb), in_specs=[cur, cur, prev, cur, prev], out_specs=[cur, cur], out_shape=[shp, shp],
        compiler_params=_params(("parallel", "parallel")), name="band_fwd",
    )(q, k, k, v, v)


def _band_bwd(q, k, v, do, lse, dlt):
    nb = T // BLK

    def body(qc_ref, qn_ref, kc_ref, kp_ref, vc_ref, vp_ref, doc_ref, don_ref, lc_ref, ln_ref, dc_ref, dn_ref,
             dq_ref, dk_ref, dv_ref):
        has_prev, next_ok = _band_scalars()
        mask_q = _band_mask_q(has_prev)
        mask_k = _band_mask_k(next_ok)
        for p in range(2):
            sl = slice(128 * p, 128 * (p + 1))
            qc, qn = qc_ref[0, :, sl], qn_ref[0, :, sl]
            doc, don = doc_ref[0, :, sl], don_ref[0, :, sl]
            kc, vc = kc_ref[0, :, sl], vc_ref[0, :, sl]
            kcat = jnp.concatenate([kp_ref[0, :, sl], kc], axis=0)
            vcat = jnp.concatenate([vp_ref[0, :, sl], vc], axis=0)
            qcat = jnp.concatenate([qc, qn], axis=0)
            docat = jnp.concatenate([doc, don], axis=0)
            dq = jnp.zeros((BLK, 128), F32)
            dk = jnp.zeros((BLK, 128), F32)
            dv = jnp.zeros((BLK, 128), F32)
            for e in range(2):
                lo = e == 0
                col = slice(128 * p + HD * e, 128 * p + HD * e + 1)
                lse_c, lse_n = lc_ref[0, :, col], ln_ref[0, :, col]
                dl_c, dl_n = dc_ref[0, :, col], dn_ref[0, :, col]
                s = jnp.where(mask_q, _dot(_head_mask(qc, lo), kcat, NT), NEG)
                pr = jnp.exp(s - lse_c)
                dp = _dot(_head_mask(doc, lo), vcat, NT)
                ds = pr * (dp - dl_c)
                dq = dq + _dot(ds.astype(BF16), _head_mask(kcat, lo), NN)
                qm, dom = _head_mask(qcat, lo), _head_mask(docat, lo)
                s2 = jnp.where(mask_k, _dot(qm, kc, NT), NEG)
                p2 = jnp.exp(s2 - jnp.concatenate([lse_c, lse_n], axis=0))
                dv = dv + _dot(p2.astype(BF16), dom, TN)
                dp2 = _dot(dom, vc, NT)
                ds2 = p2 * (dp2 - jnp.concatenate([dl_c, dl_n], axis=0))
                dk = dk + _dot(ds2.astype(BF16), qm, TN)
            dq_ref[0, :, sl] = dq
            dk_ref[0, :, sl] = dk
            dv_ref[0, :, sl] = dv

    cur = pl.BlockSpec((1, BLK, GW), lambda g, b: (g, b, 0))
    prev = pl.BlockSpec((1, BLK, GW), lambda g, b: (g, jnp.maximum(b - 1, 0), 0))
    nxt = pl.BlockSpec((1, BLK, GW), lambda g, b: (g, jnp.minimum(b + 1, nb - 1), 0))
    shp = jax.ShapeDtypeStruct((3, T, GW), F32)
    return pl.pallas_call(
        body, grid=(3, nb),
        in_specs=[cur, nxt, cur, prev, cur, prev, cur, nxt, cur, nxt, cur, nxt],
        out_specs=[cur, cur, cur], out_shape=[shp, shp, shp],
        compiler_params=_params(("parallel", "parallel")), name="band_bwd",
    )(q, q, k, k, v, v, do, do, lse, lse, dlt, dlt)


def _split3(x):
    hi = x.astype(BF16)
    r = x - hi.astype(F32)
    mid = r.astype(BF16)
    lo = (r - mid.astype(F32)).astype(BF16)
    return hi, mid, lo


def _dot3(x, m, dims=NN):
    hi, mid, lo = _split3(x)
    return _dot(hi, m, dims) + _dot(mid, m, dims) + _dot(lo, m, dims)


def _combine_weights(l_ref):
    l0, l1, l2 = l_ref[0], l_ref[1], l_ref[2]
    m = jnp.maximum(jnp.maximum(l0, l1), l2)
    e = [jnp.exp(l0 - m), jnp.exp(l1 - m), jnp.exp(l2 - m)]
    inv = 1.0 / (e[0] + e[1] + e[2])
    return [ei * inv for ei in e]


def _combine_fwd(o, lse, *, tr=256):
    def body(o_ref, l_ref, out_ref):
        alpha = _combine_weights(l_ref)
        for g in range(3):
            out_ref[:, g * GW:(g + 1) * GW] = (o_ref[g] * alpha[g]).astype(BF16)

    blk = pl.BlockSpec((3, tr, GW), lambda i: (0, i, 0))
    return pl.pallas_call(
        body, grid=(T // tr,), in_specs=[blk, blk], out_specs=pl.BlockSpec((tr, A_W), lambda i: (i, 0)),
        out_shape=jax.ShapeDtypeStruct((T, A_W), BF16), compiler_params=_params(("parallel",)), name="combine_fwd",
    )(o, lse)


def _combine_bwd(datt, o, lse, headsum, *, tr=256):
    def body(d_ref, o_ref, l_ref, hs_ref, do_ref, dl_ref):
        alpha = _combine_weights(l_ref)
        hs = hs_ref[...]
        total = jnp.zeros((tr, GW), F32)
        for g in range(3):
            dg = d_ref[:, g * GW:(g + 1) * GW]
            do_ref[g] = (dg * alpha[g]).astype(BF16)
            total = total + alpha[g] * _dot3(dg * o_ref[g], hs)
        for g in range(3):
            dl_ref[g] = alpha[g] * total

    blk = pl.BlockSpec((3, tr, GW), lambda i: (0, i, 0))
    return pl.pallas_call(
        body, grid=(T // tr,),
        in_specs=[pl.BlockSpec((tr, A_W), lambda i: (i, 0)), blk, blk, pl.BlockSpec((GW, GW), lambda i: (0, 0))],
        out_specs=[blk, blk],
        out_shape=[jax.ShapeDtypeStruct((3, T, GW), BF16), jax.ShapeDtypeStruct((3, T, GW), F32)],
        compiler_params=_params(("parallel",)), name="combine_bwd",
    )(datt, o, lse, headsum)


def _fox_scores(qm, k_ref, cq, ck_ref, e, i, n):
    s = _dot(qm, k_ref[0:n, :], NT) + (cq - ck_ref[0, e:e + 1, 0:n])
    row = lax.broadcasted_iota(jnp.int32, (FQ, n), 0)
    col = lax.broadcasted_iota(jnp.int32, (FQ, n), 1)
    s = jnp.where(col <= row + i * FQ, s, NEG)
    m = jnp.max(s, axis=-1, keepdims=True)
    pr = jnp.exp(s - m)
    return pr, jnp.sum(pr, axis=-1, keepdims=True)


def _fox_fwd(q, kv, c_col, c_row):
    def body(q_ref, k_ref, v_ref, cc_ref, cr_ref, o_ref, vm_ref):
        for e in range(2):
            vm_ref[e] = _head_mask(v_ref[...], e == 0)
        for i in range(T // FQ):
            n = (i + 1) * FQ
            rows = slice(i * FQ, n)
            acc = jnp.zeros((FQ, 128), F32)
            for e in range(2):
                qm = _head_mask(q_ref[rows, :], e == 0)
                pr, l = _fox_scores(qm, k_ref, cc_ref[0, rows, e:e + 1], cr_ref, e, i, n)
                acc = acc + _dot(pr.astype(BF16), vm_ref[e, 0:n, :], NN) / l
            o_ref[rows, :] = acc.astype(BF16)

    pair = pl.BlockSpec((T, 128), lambda p: (0, p))
    return pl.pallas_call(
        body, grid=(D // 128,),
        in_specs=[pair, pair, pl.BlockSpec((T, 128), lambda p: (0, D // 128 + p)),
                  pl.BlockSpec((1, T, 2), lambda p: (p, 0, 0)), pl.BlockSpec((1, 2, T), lambda p: (p, 0, 0))],
        out_specs=pair, out_shape=jax.ShapeDtypeStruct((T, D), BF16),
        scratch_shapes=[pltpu.VMEM((2, T, 128), BF16)],
        compiler_params=_params(("parallel",), VMEM_BIG), name="fox_fwd",
    )(q, kv, kv, c_col, c_row)


def _fox_bwd(q, kv, do, c_col, c_row, init):
    def body(q_ref, k_ref, v_ref, do_ref, cc_ref, cr_ref, ik_ref, iv_ref, iq_ref, ic_ref,
             dq_ref, dk_ref, dv_ref, dcq_ref, dck_ref, km_ref):
        dk_ref[...] = ik_ref[...]
        dv_ref[...] = iv_ref[...]
        dcq_ref[...] = iq_ref[...]
        dck_ref[...] = ic_ref[...]
        for e in range(2):
            km_ref[e] = _head_mask(k_ref[...], e == 0)
        for i in range(T // FQ):
            n = (i + 1) * FQ
            rows = slice(i * FQ, n)
            dq = jnp.zeros((FQ, 128), F32)
            for e in range(2):
                qm = _head_mask(q_ref[rows, :], e == 0)
                dom = _head_mask(do_ref[rows, :], e == 0)
                pr, l = _fox_scores(qm, k_ref, cc_ref[0, rows, e:e + 1], cr_ref, e, i, n)
                pr = pr / l
                dp = _dot(dom, v_ref[0:n, :], NT)
                ds = pr * (dp - jnp.sum(pr * dp, axis=-1, keepdims=True))
                dsb = ds.astype(BF16)
                dq = dq + _dot(dsb, km_ref[e, 0:n, :], NN)
                dk_ref[0:n, :] += _dot(dsb, qm, TN)
                dv_ref[0:n, :] += _dot(pr.astype(BF16), dom, TN)
                dcq_ref[0, rows, e:e + 1] += jnp.sum(ds, axis=-1, keepdims=True)
                dck_ref[0, e:e + 1, 0:n] += jnp.sum(ds, axis=0, keepdims=True)
            dq_ref[rows, :] = (dq * HD ** -0.5).astype(BF16)

    pair = pl.BlockSpec((T, 128), lambda p: (0, p))
    cq = pl.BlockSpec((1, T, 128), lambda p: (p, 0, 0))
    ck = pl.BlockSpec((1, 8, T), lambda p: (p, 0, 0))
    return pl.pallas_call(
        body, grid=(D // 128,),
        in_specs=[pair, pair, pl.BlockSpec((T, 128), lambda p: (0, D // 128 + p)), pair,
                  pl.BlockSpec((1, T, 2), lambda p: (p, 0, 0)), pl.BlockSpec((1, 2, T), lambda p: (p, 0, 0)),
                  pair, pair, cq, ck],
        out_specs=[pair, pair, pair, cq, ck],
        out_shape=[jax.ShapeDtypeStruct((T, D), BF16), jax.ShapeDtypeStruct((T, D), F32), jax.ShapeDtypeStruct((T, D), F32),
                   jax.ShapeDtypeStruct((D // 128, T, 128), F32), jax.ShapeDtypeStruct((D // 128, 8, T), F32)],
        scratch_shapes=[pltpu.VMEM((2, T, 128), BF16)],
        compiler_params=_params(("parallel",), VMEM_BIG), name="fox_bwd",
    )(q, kv, kv, do, c_col, c_row, *init)


def _tri(lower):
    r = lax.broadcasted_iota(jnp.int32, (BLK, BLK), 0)
    c = lax.broadcasted_iota(jnp.int32, (BLK, BLK), 1)
    return jnp.where((c <= r) if lower else (c >= r), 1.0, 0.0).astype(BF16)


def _gates_fwd(z, b):
    def body(z_ref, b_ref, c_ref):
        tri = _tri(True)
        carry = jnp.zeros((1, 128), F32)
        for i in range(T // BLK):
            rows = slice(i * BLK, (i + 1) * BLK)
            x = z_ref[rows, :] + b_ref[...]
            logf = jnp.minimum(x, 0.0) - jnp.log(1.0 + jnp.exp(-jnp.abs(x)))
            hi, mid, lo = _split3(logf)
            y = _dot(tri, hi, NN) + _dot(tri, mid, NN) + _dot(tri, lo, NN) + carry
            c_ref[rows, :] = y
            carry = y[BLK - 1:BLK, :]

    return pl.pallas_call(body, out_shape=jax.ShapeDtypeStruct((T, 128), F32), name="gates_fwd")(z, b)


def _gates_bwd(dc, z, b):
    def body(dc_ref, z_ref, b_ref, dz_ref, db_ref):
        tri = _tri(False)
        carry = jnp.zeros((1, 128), F32)
        db = jnp.zeros((1, 128), F32)
        for i in reversed(range(T // BLK)):
            rows = slice(i * BLK, (i + 1) * BLK)
            hi, mid, lo = _split3(dc_ref[rows, :])
            dlogf = _dot(tri, hi, NN) + _dot(tri, mid, NN) + _dot(tri, lo, NN) + carry
            carry = dlogf[0:1, :]
            x = z_ref[rows, :] + b_ref[...]
            dz = dlogf / (1.0 + jnp.exp(x))
            dz_ref[rows, :] = dz.astype(BF16)
            db = db + jnp.sum(dz, axis=0, keepdims=True)
        db_ref[...] = db

    return pl.pallas_call(
        body, out_shape=[jax.ShapeDtypeStruct((T, 128), BF16), jax.ShapeDtypeStruct((1, 128), F32)], name="gates_bwd",
    )(dc, z, b)


def _conv_pair(a_refs, cw_refs, cb_refs):
    row = lax.broadcasted_iota(jnp.int32, (T, CT), 0)
    outs = []
    for a_ref, cw_ref, cb_ref in zip(a_refs, cw_refs, cb_refs):
        z = a_ref[...]
        z1 = jnp.where(row >= 1, pltpu.roll(z, 1, 0), 0.0)
        z2 = jnp.where(row >= 2, pltpu.roll(z, 2, 0), 0.0)
        y = cw_ref[2:3, :] * z + cw_ref[1:2, :] * z1 + cw_ref[0:1, :] * z2 + cb_ref[...]
        outs.append((y, z, z1, z2))
    return outs


_GELU_K = math.sqrt(2.0 / math.pi)
N_CT = D_FF // CT


def _conv_specs():
    def at(rows, off):
        return pl.BlockSpec((rows, CT), lambda j: (0, j + off))
    return [at(T, 0), at(T, N_CT), at(3, 0), at(3, N_CT), at(1, 0), at(1, N_CT)]


def _convgate_fwd(a, cw, cb):
    def body(ag_ref, av_ref, wg_ref, wv_ref, bg_ref, bv_ref, u_ref):
        (g, _, _, _), (v, _, _, _) = _conv_pair((ag_ref, av_ref), (wg_ref, wv_ref), (bg_ref, bv_ref))
        th = jnp.tanh(_GELU_K * (g + 0.044715 * g * g * g))
        u_ref[...] = (0.5 * g * (1.0 + th) * v).astype(BF16)

    return pl.pallas_call(
        body, grid=(N_CT,), in_specs=_conv_specs(),
        out_specs=pl.BlockSpec((T, CT), lambda j: (0, j)), out_shape=jax.ShapeDtypeStruct((T, D_FF), BF16),
        compiler_params=_params(("parallel",), VMEM_BIG), name="convgate_fwd",
    )(a, a, cw, cw, cb, cb)


def _convgate_bwd(a, du, cw, cb):
    def body(ag_ref, av_ref, wg_ref, wv_ref, bg_ref, bv_ref, du_ref, da_ref, dcw_ref, dcb_ref):
        (g, gz, gz1, gz2), (v, vz, vz1, vz2) = _conv_pair((ag_ref, av_ref), (wg_ref, wv_ref), (bg_ref, bv_ref))
        du = du_ref[...].astype(F32)
        th = jnp.tanh(_GELU_K * (g + 0.044715 * g * g * g))
        gelu = 0.5 * g * (1.0 + th)
        dgelu = 0.5 * (1.0 + th) + 0.5 * g * (1.0 - th * th) * _GELU_K * (1.0 + 3 * 0.044715 * g * g)
        row = lax.broadcasted_iota(jnp.int32, (T, CT), 0)
        for h, (d, z, z1, z2, w_ref) in enumerate(((du * v * dgelu, gz, gz1, gz2, wg_ref), (du * gelu, vz, vz1, vz2, wv_ref))):
            d1 = jnp.where(row < T - 1, pltpu.roll(d, T - 1, 0), 0.0)
            d2 = jnp.where(row < T - 2, pltpu.roll(d, T - 2, 0), 0.0)
            da_ref[h] = (w_ref[2:3, :] * d + w_ref[1:2, :] * d1 + w_ref[0:1, :] * d2).astype(BF16)
            dcw_ref[h, 0:1, :] = jnp.sum(d * z2, axis=0, keepdims=True)
            dcw_ref[h, 1:2, :] = jnp.sum(d * z1, axis=0, keepdims=True)
            dcw_ref[h, 2:3, :] = jnp.sum(d * z, axis=0, keepdims=True)
            dcb_ref[h] = jnp.sum(d, axis=0, keepdims=True)

    def both(rows):
        return pl.BlockSpec((2, rows, CT), lambda j: (0, 0, j))

    return pl.pallas_call(
        body, grid=(N_CT,),
        in_specs=_conv_specs() + [pl.BlockSpec((T, CT), lambda j: (0, j))],
        out_specs=[both(T), both(3), both(1)],
        out_shape=[jax.ShapeDtypeStruct((2, T, D_FF), BF16), jax.ShapeDtypeStruct((2, 3, D_FF), F32),
                   jax.ShapeDtypeStruct((2, 1, D_FF), F32)],
        compiler_params=_params(("parallel",), VMEM_BIG), name="convgate_bwd",
    )(a, a, cw, cw, cb, cb, du)


def _halves_a(tm, tn, tk):
    per = D_FF // tk
    return lambda i, j, k: (lax.div(k, per), i, lax.rem(k, per))


def _halves_b(tm, tn, tk):
    per = D_FF // tn
    return lambda i, j, k: (lax.div(j, per), k, lax.rem(j, per))


def _adamw(w, m, v, g, *, name):
    r, c = w.shape
    tr = r
    if r * c > 256 * 1024:
        for cand in range(8, r, 8):
            if r % cand == 0 and cand * c <= 256 * 1024:
                tr = cand

    def body(w_ref, m_ref, v_ref, g_ref, d_ref, nm_ref, nv_ref):
        gv = g_ref[...]
        mn = ADAM_B1 * m_ref[...] + (1.0 - ADAM_B1) * gv
        vn = ADAM_B2 * v_ref[...] + (1.0 - ADAM_B2) * (gv * gv)
        m_hat = mn / (1.0 - ADAM_B1 ** ADAM_STEP)
        v_hat = vn / (1.0 - ADAM_B2 ** ADAM_STEP)
        d_ref[...] = -ADAM_LR * (m_hat / (jnp.sqrt(v_hat) + ADAM_EPS) + ADAM_WD * w_ref[...])
        nm_ref[...] = mn
        nv_ref[...] = vn

    blk = pl.BlockSpec((tr, c), lambda i: (i, 0))
    shp = jax.ShapeDtypeStruct((r, c), F32)
    return pl.pallas_call(
        body, grid=(r // tr,), in_specs=[blk] * 4, out_specs=[blk] * 3, out_shape=[shp] * 3,
        compiler_params=_params(("parallel",)), name=name,
    )(w, m, v, g)


def _place():
    x, y, c = lax.axis_index("x"), lax.axis_index("y"), lax.axis_index("c")
    chips = [(1 - x, y), (x, 1 - y), (1 - x, 1 - y)]
    return x, y, c, chips


def _window(ref, kind, s, half=None):
    lead = () if half is None else (half,)
    b, c = ref.shape[-2], ref.shape[-1]
    if kind == "col":
        return ref.at[lead + (slice(None), slice(None), pl.ds(s * (c // N_CHIPS), c // N_CHIPS))]
    if kind == "row":
        return ref.at[lead + (slice(None), pl.ds(s * (b // N_CHIPS), b // N_CHIPS), slice(None))]
    return ref.at[lead + (s,)]


def _window_shape(shape3, kind):
    a, b, c = shape3
    return {"col": (a, b, c // N_CHIPS), "row": (a, b // N_CHIPS, c), "slab": (b, c)}[kind]


def _allgather(tensors, kinds, *, name):
    n = len(tensors)

    def body(*refs):
        bufs = refs[n:2 * n]
        send, recv = refs[2 * n:]
        x, y, c, chips = _place()
        me = 2 * x + y
        sib = (x, y, 1 - c)

        def rcopy(i, k, win, to):
            return pltpu.make_async_remote_copy(src_ref=win, dst_ref=win, send_sem=send.at[i * 6 + k], recv_sem=recv.at[i * 6 + k],
                                                device_id=to, device_id_type=MESH)

        started = []
        for i in range(n):
            for k, (px, py) in enumerate(chips):
                cp = rcopy(i, k, _window(bufs[i], kinds[i], me, c), (px, py, c))
                cp.start()
                started.append(cp)
        for i in range(n):
            for k, (px, py) in enumerate(chips):
                landed = _window(bufs[i], kinds[i], 2 * px + py, c)
                rcopy(i, k, landed, (px, py, c)).wait_recv()
                fw = rcopy(i, 3 + k, landed, sib)
                fw.start()
                started.append(fw)
        for i in range(n):
            for k, (px, py) in enumerate(chips):
                rcopy(i, 3 + k, _window(bufs[i], kinds[i], 2 * px + py, 1 - c), sib).wait_recv()
        for cp in started:
            cp.wait_send()

    return pl.pallas_call(
        body, in_specs=[ANY] * n, out_specs=[ANY] * n,
        out_shape=[jax.ShapeDtypeStruct(t.shape, t.dtype) for t in tensors],
        scratch_shapes=[pltpu.SemaphoreType.DMA((6 * n,)), pltpu.SemaphoreType.DMA((6 * n,))],
        input_output_aliases={i: i for i in range(n)},
        name=name,
    )(*tensors)


def _rows_tile(rows, cols, sub):
    best = None
    for t in range(sub, rows + 1, sub):
        if rows % t == 0 and t * cols <= 512 * 1024:
            best = t
    return rows if best is None else best


def _sequencer(name, cid, n_sems, peers_of, body):
    @pl.kernel(mesh=plsc.ScalarSubcoreMesh(axis_name="seq", num_cores=1), name=name,
               scratch_types=(pltpu.SemaphoreType.DMA((n_sems,)), pltpu.SemaphoreType.DMA((n_sems,))),
               compiler_params=pltpu.CompilerParams(collective_id=cid))
    def launch(send, recv):
        x, y, c, chips = _place()
        peers = peers_of(x, y, c, chips)
        barrier = pltpu.get_barrier_semaphore()
        for peer in peers:
            pl.semaphore_signal(barrier, inc=1, device_id=peer, device_id_type=MESH)
        pl.semaphore_wait(barrier, len(peers))
        body(send, recv)

    launch()


def _half_of_full(ref, kind, h):
    if kind == "col":
        b = ref.shape[0]
        return ref.at[pl.ds(h * (b // 2), b // 2), :]
    if kind == "row":
        c = ref.shape[1]
        return ref.at[:, pl.ds(h * (c // 2), c // 2)]
    b = ref.shape[1]
    return ref.at[:, pl.ds(h * (b // 2), b // 2), :]


def _half_shape(full, kind):
    if kind == "col":
        return (full[0] // 2, full[1])
    if kind == "row":
        return (full[0], full[1] // 2)
    return (full[0], full[1] // 2, full[2])


def _win_of_half(ref, kind, s):
    if kind == "col":
        c = ref.shape[1]
        return ref.at[:, pl.ds(s * (c // N_CHIPS), c // N_CHIPS)]
    if kind == "row":
        b = ref.shape[0]
        return ref.at[pl.ds(s * (b // N_CHIPS), b // N_CHIPS), :]
    return ref.at[s]


def _win_shape(half, kind):
    if kind == "col":
        return (half[0], half[1] // N_CHIPS)
    if kind == "row":
        return (half[0] // N_CHIPS, half[1])
    return half[1:]


def _seq_swap(parts, kinds, *, name):
    n = len(parts)
    srcs = [jax.new_ref(p, memory_space=pltpu.MemorySpace.HBM) for p in parts]
    outs = [jax.empty_ref(jax.ShapeDtypeStruct(_half_shape(p.shape, k), p.dtype), memory_space=pltpu.MemorySpace.HBM)
            for p, k in zip(parts, kinds)]

    def body(send, recv):
        x, y, c, _ = _place()
        cps = []
        for i in range(n):
            cp = pltpu.make_async_remote_copy(src_ref=_half_of_full(srcs[i], kinds[i], 1 - c), dst_ref=outs[i], send_sem=send.at[i],
                                              recv_sem=recv.at[i], device_id=(x, y, 1 - c), device_id_type=MESH)
            cp.start()
            cps.append(cp)
        for cp in cps:
            cp.wait()

    _sequencer(name, 2, n, lambda x, y, c, chips: [(x, y, 1 - c)], body)
    return [o[...] for o in outs]


def _seq_scatter(halves, kinds, *, name):
    n = len(halves)
    srcs = [jax.new_ref(h, memory_space=pltpu.MemorySpace.HBM) for h in halves]
    outs = [jax.empty_ref(jax.ShapeDtypeStruct((3,) + _win_shape(h.shape, k), h.dtype), memory_space=pltpu.MemorySpace.HBM)
            for h, k in zip(halves, kinds)]

    def body(send, recv):
        x, y, c, chips = _place()
        cps = []
        for i in range(n):
            for k, (px, py) in enumerate(chips):
                cp = pltpu.make_async_remote_copy(src_ref=_win_of_half(srcs[i], kinds[i], 2 * px + py), dst_ref=outs[i].at[k],
                                                  send_sem=send.at[3 * i + k], recv_sem=recv.at[3 * i + k],
                                                  device_id=(px, py, c), device_id_type=MESH)
                cp.start()
                cps.append(cp)
        for cp in cps:
            cp.wait()

    _sequencer(name, 3, 3 * n, lambda x, y, c, chips: [(px, py, c) for px, py in chips], body)
    return [o[...] for o in outs]


def _add_half(g, p, kind, where, *, name):
    if kind == "slab":
        s, b2, c = p.shape
        tr = _rows_tile(b2, c, 16)
        nr = b2 // tr
        grid = (s, nr)
        g_spec = pl.BlockSpec((None, tr, c), lambda i, r, w: (i, w[1] * nr + r, 0))
        p_spec = pl.BlockSpec((None, tr, c), lambda i, r, w: (i, r, 0))
    elif kind == "col":
        b2, c = p.shape
        tr = _rows_tile(b2, c, 16)
        nr = b2 // tr
        grid = (1, nr)
        g_spec = pl.BlockSpec((tr, c), lambda i, r, w: (w[1] * nr + r, 0))
        p_spec = pl.BlockSpec((tr, c), lambda i, r, w: (r, 0))
    else:
        b, c2 = p.shape
        tr = _rows_tile(b, c2, 16)
        grid = (1, b // tr)
        g_spec = pl.BlockSpec((tr, c2), lambda i, r, w: (r, w[1]))
        p_spec = pl.BlockSpec((tr, c2), lambda i, r, w: (r, 0))

    def body(w_ref, g_ref, p_ref, o_ref):
        o_ref[...] = (g_ref[...].astype(F32) + p_ref[...].astype(F32)).astype(o_ref.dtype)

    return pl.pallas_call(
        body,
        grid_spec=pltpu.PrefetchScalarGridSpec(num_scalar_prefetch=1, grid=grid, in_specs=[g_spec, p_spec], out_specs=p_spec),
        out_shape=jax.ShapeDtypeStruct(p.shape, g.dtype),
        compiler_params=_params(("parallel", "parallel")), name=name,
    )(where, g, p)


def _sum_chips(r, h, kind, where, layer, layers, out_buf, after, *, name):
    _, br, cr = r.shape
    tr = _rows_tile(br, cr, 16)
    nr = br // tr
    if kind == "col":
        h_spec = pl.BlockSpec((tr, cr), lambda j, w: (j, w[0]))
        o_shape, o_spec = (layers, 2 * br, cr), pl.BlockSpec((None, tr, cr), lambda j, w: (layer, w[1] * nr + j, 0))
    elif kind == "row":
        h_spec = pl.BlockSpec((tr, cr), lambda j, w: (w[0] * nr + j, 0))
        o_shape, o_spec = (layers, br, 2 * cr), pl.BlockSpec((None, tr, cr), lambda j, w: (layer, j, w[1]))
    else:
        h_spec = pl.BlockSpec((None, tr, cr), lambda j, w: (w[0], j, 0))
        o_shape, o_spec = (layers, 2 * br, cr), pl.BlockSpec((None, tr, cr), lambda j, w: (layer, w[1] * nr + j, 0))

    def body(w_ref, h_ref, r0_ref, r1_ref, r2_ref, *rest):
        o_ref = rest[-1]
        o_ref[...] = ((h_ref[...].astype(F32) + r0_ref[...].astype(F32)) + r1_ref[...].astype(F32)) + r2_ref[...].astype(F32)

    def slot(k):
        return pl.BlockSpec((None, tr, cr), lambda j, w: (k, j, 0))

    ins, specs, alias = [h, r, r, r], [h_spec, slot(0), slot(1), slot(2)], {}
    if after is not None:
        ins.append(after)
        specs.append(ANY)
    if out_buf is not None:
        alias = {1 + len(ins): 0}
        ins.append(out_buf)
        specs.append(ANY)
    return pl.pallas_call(
        body,
        grid_spec=pltpu.PrefetchScalarGridSpec(num_scalar_prefetch=1, grid=(nr,), in_specs=specs, out_specs=o_spec),
        out_shape=jax.ShapeDtypeStruct(o_shape, F32), input_output_aliases=alias,
        compiler_params=_params(("parallel",)), name=name,
    )(where, *ins)


def _join_halves(tensors, kinds, *, name):
    n = len(tensors)

    def mine(ref, kind, h):
        if kind == "row":
            c = ref.shape[2]
            return ref.at[:, :, pl.ds(h * (c // 2), c // 2)]
        b = ref.shape[1]
        return ref.at[:, pl.ds(h * (b // 2), b // 2), :]

    def body(*refs):
        bufs = refs[n:2 * n]
        send, recv = refs[2 * n:]
        x, y, c, _ = _place()
        cps = []
        for i in range(n):
            part = mine(bufs[i], kinds[i], c)
            cp = pltpu.make_async_remote_copy(src_ref=part, dst_ref=part, send_sem=send.at[i],
                                              recv_sem=recv.at[i], device_id=(x, y, 1 - c), device_id_type=MESH)
            cp.start()
            cps.append(cp)
        for i in range(n):
            other = mine(bufs[i], kinds[i], 1 - c)
            pltpu.make_async_remote_copy(src_ref=other, dst_ref=other, send_sem=send.at[i],
                                         recv_sem=recv.at[i], device_id=(x, y, 1 - c), device_id_type=MESH).wait_recv()
        for cp in cps:
            cp.wait_send()

    return pl.pallas_call(
        body, in_specs=[ANY] * n, out_specs=[ANY] * n,
        out_shape=[jax.ShapeDtypeStruct(t.shape, t.dtype) for t in tensors],
        scratch_shapes=[pltpu.SemaphoreType.DMA((n,)), pltpu.SemaphoreType.DMA((n,))],
        input_output_aliases={i: i for i in range(n)},
        name=name,
    )(*tensors)


def _win(ref, kind, s, h=None):
    if kind == "col":
        b, c = ref.shape
        cols = pl.ds(s * (c // N_CHIPS), c // N_CHIPS)
        return ref.at[:, cols] if h is None else ref.at[pl.ds(h * (b // 2), b // 2), cols]
    if kind == "row":
        b, c = ref.shape
        rows = pl.ds(s * (b // N_CHIPS), b // N_CHIPS)
        return ref.at[rows, :] if h is None else ref.at[rows, pl.ds(h * (c // 2), c // 2)]
    b = ref.shape[1]
    return ref.at[s] if h is None else ref.at[s, pl.ds(h * (b // 2), b // 2)]


def _half(ref, kind, h):
    b, c = ref.shape
    if kind == "row":
        return ref.at[:, pl.ds(h * (c // 2), c // 2)]
    return ref.at[pl.ds(h * (b // 2), b // 2), :]


def _full_shape(shard_shape, kind):
    b, c = shard_shape
    return {"col": (b, N_CHIPS * c), "row": (N_CHIPS * b, c), "slab": (N_CHIPS, b, c)}[kind]


def _gather_body(srcs, outs, kinds, send, recv):
    x, y, c, chips = _place()
    me = 2 * x + y
    sib = (x, y, 1 - c)

    def rcopy(i, k, src, dst, to):
        return pltpu.make_async_remote_copy(src_ref=src, dst_ref=dst, send_sem=send.at[7 * i + k], recv_sem=recv.at[7 * i + k],
                                            device_id=to, device_id_type=MESH)

    started = []
    for i, (src, out, kind) in enumerate(zip(srcs, outs, kinds)):
        own = rcopy(i, 6, src, _win(out, kind, me), sib)
        own.start()
        started.append(own)
        for k, (px, py) in enumerate(chips):
            cp = rcopy(i, k, _half(src, kind, c), _win(out, kind, me, c), (px, py, c))
            cp.start()
            started.append(cp)
    for i, (out, kind) in enumerate(zip(outs, kinds)):
        for k, (px, py) in enumerate(chips):
            landed = _win(out, kind, 2 * px + py, c)
            rcopy(i, k, landed, landed, (px, py, c)).wait_recv()
            fw = rcopy(i, 3 + k, landed, landed, sib)
            fw.start()
            started.append(fw)
    for i, (src, out, kind) in enumerate(zip(srcs, outs, kinds)):
        for k, (px, py) in enumerate(chips):
            other = _win(out, kind, 2 * px + py, 1 - c)
            rcopy(i, 3 + k, other, other, sib).wait_recv()
        rcopy(i, 6, src, _win(out, kind, me), sib).wait_recv()
    for cp in started:
        cp.wait_send()


def _seq_gather(shards, kinds, *, name, cid):
    n = len(shards)
    srcs = [jax.new_ref(s, memory_space=pltpu.MemorySpace.HBM) for s in shards]
    outs = [jax.empty_ref(jax.ShapeDtypeStruct(_full_shape(s.shape, k), s.dtype), memory_space=pltpu.MemorySpace.HBM)
            for s, k in zip(shards, kinds)]

    @pl.kernel(mesh=plsc.ScalarSubcoreMesh(axis_name="seq", num_cores=1), name=name,
               scratch_types=(pltpu.SemaphoreType.DMA((7 * n,)), pltpu.SemaphoreType.DMA((7 * n,))),
               compiler_params=pltpu.CompilerParams(collective_id=cid))
    def launch(send, recv):
        x, y, c, chips = _place()
        barrier = pltpu.get_barrier_semaphore()
        for px, py in chips:
            pl.semaphore_signal(barrier, inc=1, device_id=(px, py, c), device_id_type=MESH)
        pl.semaphore_signal(barrier, inc=1, device_id=(x, y, 1 - c), device_id_type=MESH)
        pl.semaphore_wait(barrier, 4)
        _gather_body(srcs, outs, kinds, send, recv)

    launch()
    return [o[...] for o in outs]


KIND = dict(w_qkv_a="slab", w_o_a="col", w_q_b="row", w_o_b="row", w_kvf="slab", w_up="col", w_down="row", small="slab")
LAYERS = dict(w_qkv_a=N_A, w_o_a=N_A, w_q_b=DEPTH - N_A, w_o_b=DEPTH - N_A, w_kvf=1, w_up=DEPTH, w_down=DEPTH, small=1)
SMALL_W = 1792
SMALL_ROWS = 8


class _Reducer:
    def __init__(self, where):
        self.where = where
        self.acc = {nm: None for nm in KIND}
        self.pending = None

    def __call__(self, group, tag):
        names, layers, parts = zip(*group)
        kinds = [KIND[nm] for nm in names]
        self._sum_pending(after=parts[-1])
        sib = _seq_swap(list(parts), kinds, name="reduce_swap_" + tag)
        halves = [_add_half(g, p, k, self.where, name="reduce_add_" + nm) for g, p, k, nm in zip(parts, sib, kinds, names)]
        landed = _seq_scatter(halves, kinds, name="reduce_scatter_" + tag)
        self.pending = (names, layers, landed, halves, kinds)

    def _sum_pending(self, after):
        if self.pending is None:
            return
        for nm, l, r, h, k in zip(*self.pending):
            self.acc[nm] = _sum_chips(r, h, k, self.where, l, LAYERS[nm], self.acc[nm], after, name="reduce_sum_" + nm)
        self.pending = None

    def finish(self):
        self._sum_pending(after=None)
        names = list(KIND)
        joined = _join_halves([self.acc[nm] for nm in names], [KIND[nm] for nm in names], name="reduce_pair_join")
        return dict(zip(names, joined))


def _headsum_matrix():
    r = lax.broadcasted_iota(jnp.int32, (GW, GW), 0) // HD
    c = lax.broadcasted_iota(jnp.int32, (GW, GW), 1) // HD
    return jnp.where(r == c, 1.0, 0.0).astype(BF16)


def kernel(x, norm_gains, w_qkv_a, w_o_a, w_q_b, w_o_b, kv_norm, w_kvf, b_f, w_up, conv_w, conv_b, w_down, loss_target, m_norm_gains, m_w_qkv_a, m_w_o_a, m_w_q_b, m_w_o_b, m_kv_norm, m_w_kvf, m_b_f, m_w_up, m_conv_w, m_conv_b, m_w_down, v_norm_gains, v_w_qkv_a, v_w_o_a, v_w_q_b, v_w_o_b, v_kv_norm, v_w_kvf, v_b_f, v_w_up, v_conv_w, v_conv_b, v_w_down):
    xi, yi, ci = lax.axis_index("x"), lax.axis_index("y"), lax.axis_index("c")
    chip = 2 * xi + yi
    where = jnp.stack([chip, ci]).astype(jnp.int32)
    ws = dict(norm_gains=norm_gains, w_qkv_a=w_qkv_a, w_o_a=w_o_a, w_q_b=w_q_b, w_o_b=w_o_b, kv_norm=kv_norm, w_kvf=w_kvf,
              b_f=b_f, w_up=w_up, conv_w=conv_w, conv_b=conv_b, w_down=w_down)
    ms = dict(norm_gains=m_norm_gains, w_qkv_a=m_w_qkv_a, w_o_a=m_w_o_a, w_q_b=m_w_q_b, w_o_b=m_w_o_b, kv_norm=m_kv_norm,
              w_kvf=m_w_kvf, b_f=m_b_f, w_up=m_w_up, conv_w=m_conv_w, conv_b=m_conv_b, w_down=m_w_down)
    vs = dict(norm_gains=v_norm_gains, w_qkv_a=v_w_qkv_a, w_o_a=v_w_o_a, w_q_b=v_w_q_b, w_o_b=v_w_o_b, kv_norm=v_kv_norm,
              w_kvf=v_w_kvf, b_f=v_b_f, w_up=v_w_up, conv_w=v_conv_w, conv_b=v_conv_b, w_down=v_w_down)

    small = jnp.concatenate([
        jnp.pad(norm_gains.reshape(16, 256), ((0, 0), (0, 1408 - 256))),
        jnp.pad(conv_w.reshape(12, 1408), ((0, 4), (0, 0)))], axis=0)
    big = [nm for nm in KIND if nm != "small"]
    half = {nm: ws[nm].astype(BF16) for nm in big}
    W = {nm: [None] * LAYERS[nm] for nm in big if nm != "w_kvf"}
    g_small = None
    groups = [("0a", [("w_qkv_a", 0), ("w_o_a", 0), ("small", 0)]), ("0b", [("w_up", 0), ("w_down", 0)]),
              ("1", [("w_qkv_a", 1), ("w_o_a", 1), ("w_up", 1), ("w_down", 1)]),
              ("2", [("w_kvf", 0), ("w_q_b", 0), ("w_o_b", 0), ("w_up", 2), ("w_down", 2)]),
              ("3", [("w_q_b", 1), ("w_o_b", 1), ("w_up", 3), ("w_down", 3)])]
    for tag, group in groups:
        shards = [small if nm == "small" else half[nm] if nm == "w_kvf" else half[nm][i] for nm, i in group]
        got = _seq_gather(shards, [KIND[nm] for nm, _ in group], name="gather_layer" + tag, cid=1)
        for (nm, i), g in zip(group, got):
            if nm == "small":
                g_small = g
            elif nm == "w_kvf":
                W[nm] = g.transpose(1, 0, 2).reshape(D, 2 * D + 16)
            else:
                W[nm][i] = g.transpose(1, 0, 2).reshape(D, 3 * A_W) if nm == "w_qkv_a" else g
    gains = g_small[:, :16, :256].transpose(1, 0, 2).reshape(DEPTH, 4, 1, D)
    cw_full = g_small[:, 16:28, :].transpose(1, 0, 2).reshape(DEPTH, 3, 2 * D_FF)
    cb_full = conv_b.reshape(DEPTH, 1, 2 * D_FF)

    reducer = _Reducer(where)
    sq, dh = _fwd_bwd(x[0], loss_target[0], W, gains, cw_full, cb_full, kv_norm, b_f, reducer)
    loss = lax.psum(sq[0, 0] * (0.5 / D), ("x", "y", "c"))
    return _update(loss, dh[None], reducer.finish(), chip, ws, ms, vs)


def _fwd_bwd(h, target, W, gains, cw_full, cb_full, kv_norm, b_f, reduce):
    w_kv = W["w_kvf"][:, :2 * D]
    w_kvf_pad = jnp.pad(W["w_kvf"], ((0, 0), (0, 128 - 16)))
    w_f = w_kvf_pad[:, 2 * D:]
    kvn_g = kv_norm.reshape(1, D)
    bf_pad = jnp.pad(b_f, (0, 128 - 16)).reshape(1, 128)
    tabs = _rope_tables()
    headsum = _headsum_matrix()

    saved = []
    kv = zf = c_col = c_row = kvn = h_kv = None
    for l in range(DEPTH):
        s = {"h": h}
        g = gains[l]
        xn = _rms_fwd(h, g[0], out_dtype=BF16, name="rms_in")
        s["xn"] = xn
        if l < N_A:
            qkv = _matmul(xn, W["w_qkv_a"][l], mode="nn", out_dtype=F32, name="mm_qkv", mnk=(T, 3 * A_W, D), tn=768)
            q3, k3, v3 = _rope_fwd(qkv, tabs)
            qp, kp, vp = _perm(q3), _perm(k3), _perm(v3)
            o_p, lse_p = _band_fwd(qp, kp, vp)
            o3, lse3 = _unperm(o_p), _unperm(lse_p)
            att = _combine_fwd(o3, lse3)
            s.update(qp=qp, kp=kp, vp=vp, o3=o3, lse3=lse3, lse_p=lse_p, att=att)
            mix = _matmul(att, W["w_o_a"][l], mode="nn", out_dtype=F32, name="mm_oa", mnk=(T, D, A_W))
        else:
            j = l - N_A
            if l == N_A:
                h_kv = h
                kvn = _rms_fwd(h, kvn_g, out_dtype=BF16, name="rms_in")
                kv = _matmul(kvn, w_kv, mode="nn", out_dtype=BF16, name="mm_kv")
                zf = _matmul(kvn, w_f, mode="nn", out_dtype=F32, name="mm_f")
                cum = _gates_fwd(zf, bf_pad)[:, :16]
                c_col = cum.reshape(T, 8, 2).transpose(1, 0, 2)
                c_row = cum.T.reshape(8, 2, T)
            q = _matmul(xn, W["w_q_b"][j], mode="nn", out_dtype=BF16, name="mm_qb", mnk=(T, D, D), alpha=HD ** -0.5)
            o = _fox_fwd(q, kv, c_col, c_row)
            s.update(q=q, o=o)
            mix = _matmul(o, W["w_o_b"][j], mode="nn", out_dtype=F32, name="mm_ob", mnk=(T, D, D))
        s["mix"] = mix
        h1 = _rms_fwd(mix, g[1], res=h, out_dtype=F32, name="rms_res")
        xn2 = _rms_fwd(h1, g[2], out_dtype=BF16, name="rms_in")
        a = _matmul(xn2, W["w_up"][l], mode="nn", out_dtype=F32, name="mm_up", mnk=(T, 2 * D_FF, D))
        u = _convgate_fwd(a, cw_full[l], cb_full[l])
        f = _matmul(u, W["w_down"][l], mode="nn", out_dtype=F32, name="mm_down", mnk=(T, D, D_FF), tm=1024, tk=D_FF)
        h = _rms_fwd(f, g[3], res=h1, out_dtype=F32, name="rms_res")
        s.update(h1=h1, xn2=xn2, a=a, u=u, f=f)
        saved.append(s)

    dh, sq = _loss_head(h, target)

    d_gains = [[None] * 4 for _ in range(DEPTH)]
    d_cw, d_cb = [None] * DEPTH, [None] * DEPTH
    zeros_td = jnp.zeros((T, D), F32)
    fox_acc = (zeros_td, zeros_td, jnp.zeros((D // 128, T, 128), F32), jnp.zeros((D // 128, 8, T), F32))
    d_kvnorm = d_bf = None

    def dw(nm, a, b, **kw):
        return _matmul(a, b, mode="tn", out_dtype=BF16, name="mm_dw_" + nm, **kw)

    def slabs(full, width):
        return full.reshape(full.shape[0], N_CHIPS, width).transpose(1, 0, 2)

    for l in reversed(range(DEPTH)):
        s = saved[l]
        g = gains[l]
        df, d_gains[l][3] = _rms_bwd(dh, s["f"], g[3], out_dtype=BF16, name="rms_bwd")
        du = _matmul(df, W["w_down"][l], mode="nt", out_dtype=F32, name="mm_down_dx", mnk=(T, D_FF, D), tn=256)
        g_down = dw("w_down", s["u"], df, tm=1408, tn=1024)
        da, d_cw[l], d_cb[l] = _convgate_bwd(s["a"], du, cw_full[l], cb_full[l])
        dxn2 = _matmul(da, W["w_up"][l], mode="nt", out_dtype=F32, name="mm_up_dx", mnk=(T, D, 2 * D_FF), tm=1024, tn=1024, tk=1408,
                       a_map=_halves_a)
        g_up = dw("w_up", s["xn2"], da, mnk=(D, 2 * D_FF, T), tn=1408, b_map=_halves_b)
        reduce([("w_down", l, g_down), ("w_up", l, g_up)], "ffn%d" % l)
        dh1, d_gains[l][2] = _rms_bwd(dxn2, s["h1"], g[2], dres=dh, out_dtype=F32, name="rms_bwd_res")
        dmix, d_gains[l][1] = _rms_bwd(dh1, s["mix"], g[1], out_dtype=BF16, name="rms_bwd")
        if l < N_A:
            datt = _matmul(dmix, W["w_o_a"][l], mode="nt", out_dtype=F32, name="mm_oa_dx", mnk=(T, A_W, D), tn=768)
            g_o = dw("w_o_a", s["att"], dmix, tm=768, tn=1024)
            do3, dlt3 = _combine_bwd(datt, s["o3"], s["lse3"], headsum)
            dqp, dkp, dvp = _band_bwd(s["qp"], s["kp"], s["vp"], _perm(do3), s["lse_p"], _perm(dlt3))
            dqkv = _rope_bwd(_unperm(dqp), _unperm(dkp), _unperm(dvp), tabs)
            dxn = _matmul(dqkv, W["w_qkv_a"][l], mode="nt", out_dtype=F32, name="mm_qkv_dx", mnk=(T, D, 3 * A_W), tm=1024, tn=1024, tk=3 * A_W)
            g_qkv = dw("w_qkv_a", s["xn"], dqkv, tn=768)
            group = [("w_o_a", l, g_o), ("w_qkv_a", l, slabs(g_qkv, 576))]
        else:
            j = l - N_A
            do = _matmul(dmix, W["w_o_b"][j], mode="nt", out_dtype=BF16, name="mm_ob_dx", mnk=(T, D, D))
            g_o = dw("w_o_b", s["o"], dmix, tn=1024)
            dq, *fox_acc = _fox_bwd(s["q"], kv, do, c_col, c_row, fox_acc)
            dxn = _matmul(dq, W["w_q_b"][j], mode="nt", out_dtype=F32, name="mm_qb_dx", mnk=(T, D, D))
            g_q = dw("w_q_b", s["xn"], dq, tn=1024)
            group = [("w_o_b", j, g_o), ("w_q_b", j, g_q)]
        dh, d_gains[l][0] = _rms_bwd(dxn, s["h"], g[0], dres=dh1, out_dtype=F32, name="rms_bwd_res")
        if l == N_A:
            dk, dv, dcq, dck = fox_acc
            dc16 = dcq[:, :, :2].transpose(1, 0, 2).reshape(T, 16) - dck[:, :2, :].reshape(16, T).T
            dzf, d_bf = _gates_bwd(jnp.pad(dc16, ((0, 0), (0, 128 - 16))), zf, bf_pad)
            dkvf = jnp.concatenate([dk.astype(BF16), dv.astype(BF16), dzf], axis=1)
            g_kvf = _matmul(kvn, dkvf, mode="tn", out_dtype=BF16, name="mm_kvf_dw", tm=512, tn=2 * D + 128)[:, :2 * D + 16]
            dkvn = _matmul(dkvf, w_kvf_pad, mode="nt", out_dtype=F32, name="mm_kvf_dx", tm=1024, tn=1024, tk=2 * D + 128)
            dh, d_kvnorm = _rms_bwd(dkvn, h_kv, kvn_g, dres=dh, out_dtype=F32, name="rms_bwd_res")
            group.append(("w_kvf", 0, slabs(g_kvf, 516)))
        reduce(group, "mix%d" % l)
    small_flat = jnp.concatenate([
        jnp.stack([jnp.stack(r) for r in d_gains]).reshape(-1),
        jnp.stack(d_cw).transpose(0, 2, 1, 3).reshape(-1),
        jnp.stack(d_cb).reshape(-1),
        d_kvnorm.reshape(-1), d_bf[0, :16]])
    small = jnp.pad(small_flat, (0, 2 * N_CHIPS * SMALL_ROWS * SMALL_W - small_flat.shape[0]))
    reduce([("small", 0, small.reshape(N_CHIPS, 2 * SMALL_ROWS, SMALL_W))], "small")
    return sq, dh


def _update(loss, grad_x, reduced, chip, ws, ms, vs):
    red_s = reduced.pop("small")
    buf_s = lax.dynamic_update_slice(jnp.zeros((2, N_CHIPS, SMALL_ROWS, SMALL_W), F32), red_s.reshape(2, 1, SMALL_ROWS, SMALL_W),
                                     (0, chip, 0, 0))
    (all_s,) = _allgather([buf_s], ["slab"], name="gather_small_grads")
    sflat = all_s.transpose(1, 0, 2, 3).reshape(-1)

    grads = {nm: r.reshape(ws[nm].shape) for nm, r in reduced.items()}
    o = 0
    g_gains_full = sflat[o:o + 16 * D].reshape(DEPTH, 4, D); o += 16 * D
    g_cw_full = sflat[o:o + 12 * 2 * D_FF].reshape(DEPTH, 3, 2 * D_FF); o += 12 * 2 * D_FF
    grads["conv_b"] = sflat[o:o + 4 * 2 * D_FF].reshape(DEPTH, 2 * D_FF); o += 4 * 2 * D_FF
    grads["kv_norm"] = sflat[o:o + D]; o += D
    grads["b_f"] = sflat[o:o + 16]
    grads["norm_gains"] = lax.dynamic_slice_in_dim(g_gains_full, chip * 256, 256, axis=2)
    grads["conv_w"] = lax.dynamic_slice_in_dim(g_cw_full, chip * 1408, 1408, axis=2)

    names = ["norm_gains", "w_qkv_a", "w_o_a", "w_q_b", "w_o_b", "kv_norm", "w_kvf", "b_f", "w_up", "conv_w", "conv_b", "w_down"]
    deltas, new_m, new_v = {}, {}, {}
    for nm in names:
        shp = ws[nm].shape
        two = (math.prod(shp[:-1]), shp[-1]) if len(shp) > 1 else (1, shp[0])
        d, m2, v2 = _adamw(ws[nm].reshape(two), ms[nm].reshape(two), vs[nm].reshape(two), grads[nm].reshape(two),
                           name="adamw_" + nm)
        deltas[nm], new_m[nm], new_v[nm] = d.reshape(shp), m2.reshape(shp), v2.reshape(shp)

    return (loss, grad_x, *[grads[nm] for nm in names], *[deltas[nm] for nm in names],
            *[new_m[nm] for nm in names], *[new_v[nm] for nm in names])
```

```python
import math

import jax
import jax.numpy as jnp
from jax import lax
from jax.experimental import pallas as pl
from jax.experimental.pallas import tpu as pltpu
from jax.experimental.pallas import tpu_sc as plsc

F32 = jnp.float32
BF16 = jnp.bfloat16
MESH = pl.DeviceIdType.MESH
ANY = pl.BlockSpec(memory_space=pl.ANY)

T = 2048
D = 1024
HD = 64
DEPTH = 4
N_A = 2
A_W = 768
GW = 256
DIL = (1, 4, 16)
BLK = 128
D_FF = 2816
ROPE_THETA = 500000.0
EPS = 1e-6
NEG = -1e30
N_CHIPS = 4
FQ = 256
CT = 128
VMEM_BIG = 48 * 1024 * 1024

ADAM_LR, ADAM_B1, ADAM_B2, ADAM_EPS, ADAM_WD, ADAM_STEP = 0.001, 0.9, 0.999, 1e-08, 0.01, 10

NN = (((1,), (0,)), ((), ()))
NT = (((1,), (1,)), ((), ()))
TN = (((0,), (0,)), ((), ()))


def _dot(a, b, dims):
    return lax.dot_general(a, b, dims, preferred_element_type=F32)


def _pick(dim, pref):
    if dim <= pref:
        return dim
    best = None
    for t in range(128, pref + 1, 128):
        if dim % t == 0:
            best = t
    assert best is not None, (dim, pref)
    return best


def _params(sem=None, vmem=None):
    kw = {}
    if sem is not None:
        kw["dimension_semantics"] = sem
    if vmem is not None:
        kw["vmem_limit_bytes"] = vmem
    return pltpu.CompilerParams(**kw)


def _matmul(a, b, *, mode, out_dtype, name, mnk=None, alpha=None, tm=2048, tn=512, tk=2048,
            a_map=None, b_map=None, acc_init=None, out_slab=None, out_slabs=None, out_buf=None, after=None):
    if mnk is not None:
        M, N, K = mnk
    elif mode == "nn":
        (M, K), (_, N) = a.shape, b.shape
    elif mode == "nt":
        (M, K), (N, _) = a.shape, b.shape
    else:
        (K, M), (_, N) = a.shape, b.shape
    tm, tn, tk = _pick(M, tm), _pick(N, tn), _pick(K, tk)
    nk = K // tk
    dims = {"nn": NN, "nt": NT, "tn": TN}[mode]
    after = [t for t in (after or ()) if t is not None]
    n_in = 2 + (acc_init is not None) + len(after) + (out_buf is not None)

    def body(*refs):
        a_ref, b_ref = refs[0], refs[1]
        o_ref = refs[n_in]
        k = pl.program_id(2)

        def finish(r):
            if alpha is not None:
                r = r * alpha
            o_ref[...] = r.astype(out_dtype)

        def product():
            r = _dot(a_ref[...], b_ref[...], dims)
            return r if acc_init is None else r + refs[2][...]

        if nk == 1:
            finish(product())
            return
        acc_ref = refs[n_in + 1]

        @pl.when(k == 0)
        def _():
            acc_ref[...] = product()

        @pl.when((k > 0) & (k < nk - 1))
        def _():
            acc_ref[...] += _dot(a_ref[...], b_ref[...], dims)

        @pl.when(k == nk - 1)
        def _():
            finish(acc_ref[...] + _dot(a_ref[...], b_ref[...], dims))

    a_blk = (tk, tm) if mode == "tn" else (tm, tk)
    b_blk = (tn, tk) if mode == "nt" else (tk, tn)
    if a_map is not None:
        a_spec = pl.BlockSpec((None,) + a_blk, a_map(tm, tn, tk))
    elif mode == "tn":
        a_spec = pl.BlockSpec(a_blk, lambda i, j, k: (k, i))
    else:
        a_spec = pl.BlockSpec(a_blk, lambda i, j, k: (i, k))
    if b_map is not None:
        b_spec = pl.BlockSpec((None,) + b_blk, b_map(tm, tn, tk))
    elif mode == "nt":
        b_spec = pl.BlockSpec(b_blk, lambda i, j, k: (j, k))
    else:
        b_spec = pl.BlockSpec(b_blk, lambda i, j, k: (k, j))
    ins, specs, alias = [a, b], [a_spec, b_spec], {}
    if acc_init is not None:
        ins.append(acc_init)
        specs.append(pl.BlockSpec((tm, tn), lambda i, j, k: (i, j)))
    ins += after
    specs += [ANY] * len(after)
    if out_buf is not None:
        alias = {len(ins): 0}
        ins.append(out_buf)
        specs.append(ANY)
    if out_slab is None:
        o_spec = pl.BlockSpec((tm, tn), lambda i, j, k: (i, j))
        o_shape = jax.ShapeDtypeStruct((M, N), out_dtype)
    else:
        o_spec = pl.BlockSpec((None, tm, tn), lambda i, j, k: (out_slab, i, j))
        o_shape = jax.ShapeDtypeStruct((out_slabs, M, N), out_dtype)
    return pl.pallas_call(
        body,
        grid=(M // tm, N // tn, nk),
        in_specs=specs,
        out_specs=o_spec,
        out_shape=o_shape,
        scratch_shapes=[pltpu.VMEM((tm, tn), F32)] if nk > 1 else [],
        input_output_aliases=alias,
        compiler_params=_params(("parallel", "parallel", "arbitrary"), VMEM_BIG),
        name=name,
    )(*ins)


def _slab(l, mode):
    if mode == "nt":
        return lambda tm, tn, tk: (lambda i, j, k: (l, j, k))
    return lambda tm, tn, tk: (lambda i, j, k: (l, k, j))


def _rms_fwd(x, g, *, out_dtype, name, res=None, tr=256):
    n, d = x.shape

    def body(*refs):
        x_ref, g_ref = refs[0], refs[1]
        o_ref = refs[-1]
        xv = x_ref[...].astype(F32)
        y = xv * lax.rsqrt(jnp.mean(xv * xv, axis=-1, keepdims=True) + EPS) * g_ref[...]
        if res is not None:
            y = y + refs[2][...]
        o_ref[...] = y.astype(out_dtype)

    row = pl.BlockSpec((tr, d), lambda i: (i, 0))
    vec = pl.BlockSpec((1, d), lambda i: (0, 0))
    ins = [x, g] + ([] if res is None else [res])
    specs = [row, vec] + ([] if res is None else [row])
    return pl.pallas_call(
        body, grid=(n // tr,), in_specs=specs, out_specs=row,
        out_shape=jax.ShapeDtypeStruct((n, d), out_dtype),
        compiler_params=_params(("parallel",)), name=name,
    )(*ins)


def _rms_bwd(dy, x, g, *, out_dtype, name, dres=None, tr=256):
    n, d = x.shape

    def body(*refs):
        dy_ref, x_ref, g_ref = refs[0], refs[1], refs[2]
        dx_ref, dg_ref = refs[-2], refs[-1]
        xv = x_ref[...].astype(F32)
        dyv = dy_ref[...].astype(F32)
        rstd = lax.rsqrt(jnp.mean(xv * xv, axis=-1, keepdims=True) + EPS)
        xhat = xv * rstd
        dxh = dyv * g_ref[...]
        dx = rstd * (dxh - xhat * jnp.mean(dxh * xhat, axis=-1, keepdims=True))
        if dres is not None:
            dx = dx + refs[3][...]
        dx_ref[...] = dx.astype(out_dtype)

        @pl.when(pl.program_id(0) == 0)
        def _():
            dg_ref[...] = jnp.zeros_like(dg_ref)

        dg_ref[...] += jnp.sum(dyv * xhat, axis=0, keepdims=True)

    row = pl.BlockSpec((tr, d), lambda i: (i, 0))
    vec = pl.BlockSpec((1, d), lambda i: (0, 0))
    ins = [dy, x, g] + ([] if dres is None else [dres])
    specs = [row, row, vec] + ([] if dres is None else [row])
    return pl.pallas_call(
        body, grid=(n // tr,), in_specs=specs, out_specs=[row, vec],
        out_shape=[jax.ShapeDtypeStruct((n, d), out_dtype), jax.ShapeDtypeStruct((1, d), F32)],
        compiler_params=_params(("arbitrary",)), name=name,
    )(*ins)


def _loss_head(h, target, *, tr=256):
    n, d = h.shape

    def body(h_ref, t_ref, dh_ref, s_ref):
        err = h_ref[...] - t_ref[...]
        dh_ref[...] = err * (1.0 / d)

        @pl.when(pl.program_id(0) == 0)
        def _():
            s_ref[...] = jnp.zeros_like(s_ref)

        s_ref[...] += jnp.sum(err * err)

    row = pl.BlockSpec((tr, d), lambda i: (i, 0))
    acc = pl.BlockSpec((8, 128), lambda i: (0, 0))
    return pl.pallas_call(
        body, grid=(n // tr,), in_specs=[row, row], out_specs=[row, acc],
        out_shape=[jax.ShapeDtypeStruct((n, d), F32), jax.ShapeDtypeStruct((8, 128), F32)],
        compiler_params=_params(("arbitrary",)), name="loss_head",
    )(h, target)


def _rope_tables():
    pos = jnp.arange(T, dtype=F32)
    inv = ROPE_THETA ** (-jnp.arange(0, 16, 2, dtype=F32) / 16)
    ang = pos[:, None] * inv[None, :]
    cos, sin = jnp.cos(ang), jnp.sin(ang)
    one = jnp.ones((T, HD - 16), F32)
    zero8 = jnp.zeros((T, 8), F32)
    zero = jnp.zeros((T, HD - 16), F32)
    c = jnp.concatenate([cos, cos, one], axis=1)
    s1 = jnp.concatenate([zero8, sin, zero], axis=1)
    s2 = jnp.concatenate([-sin, zero8, zero], axis=1)
    return tuple(jnp.concatenate([t, t], axis=1) for t in (c, s1, s2))


def _rope_fwd(qkv, tabs, *, tr=256):
    def body(x_ref, c_ref, s1_ref, s2_ref, q_ref, k_ref, v_ref):
        c, s1, s2 = c_ref[...], s1_ref[...], s2_ref[...]
        for which, o_ref, scale in ((0, q_ref, HD ** -0.5), (1, k_ref, None)):
            for j in range(A_W // 128):
                x = x_ref[:, which * A_W + j * 128: which * A_W + (j + 1) * 128]
                y = x * c + pltpu.roll(x, 8, 1) * s1 + pltpu.roll(x, 120, 1) * s2
                if scale is not None:
                    y = y * scale
                o_ref[j // 2, :, (j % 2) * 128:(j % 2 + 1) * 128] = y.astype(BF16)
        for j in range(A_W // 128):
            v_ref[j // 2, :, (j % 2) * 128:(j % 2 + 1) * 128] = x_ref[:, 2 * A_W + j * 128: 2 * A_W + (j + 1) * 128].astype(BF16)

    tab = pl.BlockSpec((tr, 128), lambda i: (i, 0))
    out = pl.BlockSpec((3, tr, GW), lambda i: (0, i, 0))
    shp = jax.ShapeDtypeStruct((3, T, GW), BF16)
    return pl.pallas_call(
        body, grid=(T // tr,), in_specs=[pl.BlockSpec((tr, 3 * A_W), lambda i: (i, 0)), tab, tab, tab],
        out_specs=[out, out, out], out_shape=[shp, shp, shp],
        compiler_params=_params(("parallel",)), name="rope_fwd",
    )(qkv, *tabs)


def _rope_bwd(dq, dk, dv, tabs, *, tr=256):
    def body(dq_ref, dk_ref, dv_ref, c_ref, s1_ref, s2_ref, o_ref):
        c, s1, s2 = c_ref[...], s1_ref[...], s2_ref[...]
        for which, i_ref, scale in ((0, dq_ref, HD ** -0.5), (1, dk_ref, None)):
            for j in range(A_W // 128):
                g = i_ref[j // 2, :, (j % 2) * 128:(j % 2 + 1) * 128]
                y = g * c + pltpu.roll(g * s1, 120, 1) + pltpu.roll(g * s2, 8, 1)
                if scale is not None:
                    y = y * scale
                o_ref[:, which * A_W + j * 128: which * A_W + (j + 1) * 128] = y.astype(BF16)
        for j in range(A_W // 128):
            o_ref[:, 2 * A_W + j * 128: 2 * A_W + (j + 1) * 128] = dv_ref[j // 2, :, (j % 2) * 128:(j % 2 + 1) * 128].astype(BF16)

    tab = pl.BlockSpec((tr, 128), lambda i: (i, 0))
    cot = pl.BlockSpec((3, tr, GW), lambda i: (0, i, 0))
    return pl.pallas_call(
        body, grid=(T // tr,), in_specs=[cot, cot, cot, tab, tab, tab],
        out_specs=pl.BlockSpec((tr, 3 * A_W), lambda i: (i, 0)),
        out_shape=jax.ShapeDtypeStruct((T, 3 * A_W), BF16),
        compiler_params=_params(("parallel",)), name="rope_bwd",
    )(dq, dk, dv, *tabs)


def _perm(x3):
    out = [x3[0]]
    for g in (1, 2):
        r = DIL[g]
        out.append(x3[g].reshape(T // r, r, GW).transpose(1, 0, 2).reshape(T, GW))
    return jnp.stack(out)


def _unperm(x3):
    out = [x3[0]]
    for g in (1, 2):
        r = DIL[g]
        out.append(x3[g].reshape(r, T // r, GW).transpose(1, 0, 2).reshape(T, GW))
    return jnp.stack(out)


def _head_mask(x, lane_lo):
    lane = lax.broadcasted_iota(jnp.int32, x.shape, 1)
    keep = (lane < HD) if lane_lo else (lane >= HD)
    return jnp.where(keep, x.astype(F32), 0.0).astype(BF16)


def _band_scalars():
    g, b = pl.program_id(0), pl.program_id(1)
    nbs = lax.shift_right_logical(jnp.int32(T // BLK), 2 * g)
    has_prev = jnp.where((b & (nbs - 1)) != 0, 1, 0)
    next_ok = jnp.where(((b + 1) & (nbs - 1)) != 0, 1, 0)
    return has_prev, next_ok


def _band_mask_q(has_prev):
    row = lax.broadcasted_iota(jnp.int32, (BLK, 2 * BLK), 0)
    col = lax.broadcasted_iota(jnp.int32, (BLK, 2 * BLK), 1)
    return ((col < BLK) & (col >= row) & (has_prev == 1)) | ((col >= BLK) & (col - BLK <= row))


def _band_mask_k(next_ok):
    row = lax.broadcasted_iota(jnp.int32, (2 * BLK, BLK), 0)
    col = lax.broadcasted_iota(jnp.int32, (2 * BLK, BLK), 1)
    return ((row < BLK) & (col <= row)) | ((row >= BLK) & (col >= row - BLK) & (next_ok == 1))


def _band_fwd(q, k, v):
    nb = T // BLK

    def body(q_ref, kc_ref, kp_ref, vc_ref, vp_ref, o_ref, l_ref):
        has_prev, _ = _band_scalars()
        mask = _band_mask_q(has_prev)
        lane = lax.broadcasted_iota(jnp.int32, (BLK, 128), 1)
        for p in range(2):
            sl = slice(128 * p, 128 * (p + 1))
            qp = q_ref[0, :, sl]
            kcat = jnp.concatenate([kp_ref[0, :, sl], kc_ref[0, :, sl]], axis=0)
            vcat = jnp.concatenate([vp_ref[0, :, sl], vc_ref[0, :, sl]], axis=0)
            o_acc = jnp.zeros((BLK, 128), F32)
            lse = jnp.zeros((BLK, 128), F32)
            for e in range(2):
                s = _dot(_head_mask(qp, e == 0), kcat, NT)
                s = jnp.where(mask, s, NEG)
                m = jnp.max(s, axis=-1, keepdims=True)
                pr = jnp.exp(s - m)
                l = jnp.sum(pr, axis=-1, keepdims=True)
                o_acc = o_acc + _dot(pr.astype(BF16), _head_mask(vcat, e == 0), NN) / l
                lse = jnp.where((lane < HD) if e == 0 else (lane >= HD), m + jnp.log(l), lse)
            o_ref[0, :, sl] = o_acc
            l_ref[0, :, sl] = lse

    cur = pl.BlockSpec((1, BLK, GW), lambda g, b: (g, b, 0))
    prev = pl.BlockSpec((1, BLK, GW), lambda g, b: (g, jnp.maximum(b - 1, 0), 0))
    shp = jax.ShapeDtypeStruct((3, T, GW), F32)
    return pl.pallas_call(
        body, grid=(3, nb), in_specs=[cur, cur, prev, cur, prev], out_specs=[cur, cur], out_shape=[shp, shp],
        compiler_params=_params(("parallel", "parallel")), name="band_fwd",
    )(q, k, k, v, v)


def _band_bwd(q, k, v, do, lse, dlt):
    nb = T // BLK

    def body(qc_ref, qn_ref, kc_ref, kp_ref, vc_ref, vp_ref, doc_ref, don_ref, lc_ref, ln_ref, dc_ref, dn_ref,
             dq_ref, dk_ref, dv_ref):
        has_prev, next_ok = _band_scalars()
        mask_q = _band_mask_q(has_prev)
        mask_k = _band_mask_k(next_ok)
        for p in range(2):
            sl = slice(128 * p, 128 * (p + 1))
            qc, qn = qc_ref[0, :, sl], qn_ref[0, :, sl]
            doc, don = doc_ref[0, :, sl], don_ref[0, :, sl]
            kc, vc = kc_ref[0, :, sl], vc_ref[0, :, sl]
            kcat = jnp.concatenate([kp_ref[0, :, sl], kc], axis=0)
            vcat = jnp.concatenate([vp_ref[0, :, sl], vc], axis=0)
            qcat = jnp.concatenate([qc, qn], axis=0)
            docat = jnp.concatenate([doc, don], axis=0)
            dq = jnp.zeros((BLK, 128), F32)
            dk = jnp.zeros((BLK, 128), F32)
            dv = jnp.zeros((BLK, 128), F32)
            for e in range(2):
                lo = e == 0
                col = slice(128 * p + HD * e, 128 * p + HD * e + 1)
                lse_c, lse_n = lc_ref[0, :, col], ln_ref[0, :, col]
                dl_c, dl_n = dc_ref[0, :, col], dn_ref[0, :, col]
                s = jnp.where(mask_q, _dot(_head_mask(qc, lo), kcat, NT), NEG)
                pr = jnp.exp(s - lse_c)
                dp = _dot(_head_mask(doc, lo), vcat, NT)
                ds = pr * (dp - dl_c)
                dq = dq + _dot(ds.astype(BF16), _head_mask(kcat, lo), NN)
                qm, dom = _head_mask(qcat, lo), _head_mask(docat, lo)
                s2 = jnp.where(mask_k, _dot(qm, kc, NT), NEG)
                p2 = jnp.exp(s2 - jnp.concatenate([lse_c, lse_n], axis=0))
                dv = dv + _dot(p2.astype(BF16), dom, TN)
                dp2 = _dot(dom, vc, NT)
                ds2 = p2 * (dp2 - jnp.concatenate([dl_c, dl_n], axis=0))
                dk = dk + _dot(ds2.astype(BF16), qm, TN)
            dq_ref[0, :, sl] = dq
            dk_ref[0, :, sl] = dk
            dv_ref[0, :, sl] = dv

    cur = pl.BlockSpec((1, BLK, GW), lambda g, b: (g, b, 0))
    prev = pl.BlockSpec((1, BLK, GW), lambda g, b: (g, jnp.maximum(b - 1, 0), 0))
    nxt = pl.BlockSpec((1, BLK, GW), lambda g, b: (g, jnp.minimum(b + 1, nb - 1), 0))
    shp = jax.ShapeDtypeStruct((3, T, GW), F32)
    return pl.pallas_call(
        body, grid=(3, nb),
        in_specs=[cur, nxt, cur, prev, cur, prev, cur, nxt, cur, nxt, cur, nxt],
        out_specs=[cur, cur, cur], out_shape=[shp, shp, shp],
        compiler_params=_params(("parallel", "parallel")), name="band_bwd",
    )(q, q, k, k, v, v, do, do, lse, lse, dlt, dlt)


def _split3(x):
    hi = x.astype(BF16)
    r = x - hi.astype(F32)
    mid = r.astype(BF16)
    lo = (r - mid.astype(F32)).astype(BF16)
    return hi, mid, lo


def _dot3(x, m, dims=NN):
    hi, mid, lo = _split3(x)
    return _dot(hi, m, dims) + _dot(mid, m, dims) + _dot(lo, m, dims)


def _combine_weights(l_ref):
    l0, l1, l2 = l_ref[0], l_ref[1], l_ref[2]
    m = jnp.maximum(jnp.maximum(l0, l1), l2)
    e = [jnp.exp(l0 - m), jnp.exp(l1 - m), jnp.exp(l2 - m)]
    inv = 1.0 / (e[0] + e[1] + e[2])
    return [ei * inv for ei in e]


def _combine_fwd(o, lse, *, tr=256):
    def body(o_ref, l_ref, out_ref):
        alpha = _combine_weights(l_ref)
        for g in range(3):
            out_ref[:, g * GW:(g + 1) * GW] = (o_ref[g] * alpha[g]).astype(BF16)

    blk = pl.BlockSpec((3, tr, GW), lambda i: (0, i, 0))
    return pl.pallas_call(
        body, grid=(T // tr,), in_specs=[blk, blk], out_specs=pl.BlockSpec((tr, A_W), lambda i: (i, 0)),
        out_shape=jax.ShapeDtypeStruct((T, A_W), BF16), compiler_params=_params(("parallel",)), name="combine_fwd",
    )(o, lse)


def _combine_bwd(datt, o, lse, headsum, *, tr=256):
    def body(d_ref, o_ref, l_ref, hs_ref, do_ref, dl_ref):
        alpha = _combine_weights(l_ref)
        hs = hs_ref[...]
        total = jnp.zeros((tr, GW), F32)
        for g in range(3):
            dg = d_ref[:, g * GW:(g + 1) * GW]
            do_ref[g] = (dg * alpha[g]).astype(BF16)
            total = total + alpha[g] * _dot3(dg * o_ref[g], hs)
        for g in range(3):
            dl_ref[g] = alpha[g] * total

    blk = pl.BlockSpec((3, tr, GW), lambda i: (0, i, 0))
    return pl.pallas_call(
        body, grid=(T // tr,),
        in_specs=[pl.BlockSpec((tr, A_W), lambda i: (i, 0)), blk, blk, pl.BlockSpec((GW, GW), lambda i: (0, 0))],
        out_specs=[blk, blk],
        out_shape=[jax.ShapeDtypeStruct((3, T, GW), BF16), jax.ShapeDtypeStruct((3, T, GW), F32)],
        compiler_params=_params(("parallel",)), name="combine_bwd",
    )(datt, o, lse, headsum)


def _fox_scores(qm, k_ref, cq, ck_ref, e, i, n):
    s = _dot(qm, k_ref[0:n, :], NT) + (cq - ck_ref[0, e:e + 1, 0:n])
    row = lax.broadcasted_iota(jnp.int32, (FQ, n), 0)
    col = lax.broadcasted_iota(jnp.int32, (FQ, n), 1)
    s = jnp.where(col <= row + i * FQ, s, NEG)
    m = jnp.max(s, axis=-1, keepdims=True)
    pr = jnp.exp(s - m)
    return pr, jnp.sum(pr, axis=-1, keepdims=True)


def _fox_fwd(q, kv, c_col, c_row):
    def body(q_ref, k_ref, v_ref, cc_ref, cr_ref, o_ref, vm_ref):
        for e in range(2):
            vm_ref[e] = _head_mask(v_ref[...], e == 0)
        for i in range(T // FQ):
            n = (i + 1) * FQ
            rows = slice(i * FQ, n)
            acc = jnp.zeros((FQ, 128), F32)
            for e in range(2):
                qm = _head_mask(q_ref[rows, :], e == 0)
                pr, l = _fox_scores(qm, k_ref, cc_ref[0, rows, e:e + 1], cr_ref, e, i, n)
                acc = acc + _dot(pr.astype(BF16), vm_ref[e, 0:n, :], NN) / l
            o_ref[rows, :] = acc.astype(BF16)

    pair = pl.BlockSpec((T, 128), lambda p: (0, p))
    return pl.pallas_call(
        body, grid=(D // 128,),
        in_specs=[pair, pair, pl.BlockSpec((T, 128), lambda p: (0, D // 128 + p)),
                  pl.BlockSpec((1, T, 2), lambda p: (p, 0, 0)), pl.BlockSpec((1, 2, T), lambda p: (p, 0, 0))],
        out_specs=pair, out_shape=jax.ShapeDtypeStruct((T, D), BF16),
        scratch_shapes=[pltpu.VMEM((2, T, 128), BF16)],
        compiler_params=_params(("parallel",), VMEM_BIG), name="fox_fwd",
    )(q, kv, kv, c_col, c_row)


def _fox_bwd(q, kv, do, c_col, c_row, init):
    def body(q_ref, k_ref, v_ref, do_ref, cc_ref, cr_ref, ik_ref, iv_ref, iq_ref, ic_ref,
             dq_ref, dk_ref, dv_ref, dcq_ref, dck_ref, km_ref):
        dk_ref[...] = ik_ref[...]
        dv_ref[...] = iv_ref[...]
        dcq_ref[...] = iq_ref[...]
        dck_ref[...] = ic_ref[...]
        for e in range(2):
            km_ref[e] = _head_mask(k_ref[...], e == 0)
        for i in range(T // FQ):
            n = (i + 1) * FQ
            rows = slice(i * FQ, n)
            dq = jnp.zeros((FQ, 128), F32)
            for e in range(2):
                qm = _head_mask(q_ref[rows, :], e == 0)
                dom = _head_mask(do_ref[rows, :], e == 0)
                pr, l = _fox_scores(qm, k_ref, cc_ref[0, rows, e:e + 1], cr_ref, e, i, n)
                pr = pr / l
                dp = _dot(dom, v_ref[0:n, :], NT)
                ds = pr * (dp - jnp.sum(pr * dp, axis=-1, keepdims=True))
                dsb = ds.astype(BF16)
                dq = dq + _dot(dsb, km_ref[e, 0:n, :], NN)
                dk_ref[0:n, :] += _dot(dsb, qm, TN)
                dv_ref[0:n, :] += _dot(pr.astype(BF16), dom, TN)
                dcq_ref[0, rows, e:e + 1] += jnp.sum(ds, axis=-1, keepdims=True)
                dck_ref[0, e:e + 1, 0:n] += jnp.sum(ds, axis=0, keepdims=True)
            dq_ref[rows, :] = (dq * HD ** -0.5).astype(BF16)

    pair = pl.BlockSpec((T, 128), lambda p: (0, p))
    cq = pl.BlockSpec((1, T, 128), lambda p: (p, 0, 0))
    ck = pl.BlockSpec((1, 8, T), lambda p: (p, 0, 0))
    return pl.pallas_call(
        body, grid=(D // 128,),
        in_specs=[pair, pair, pl.BlockSpec((T, 128), lambda p: (0, D // 128 + p)), pair,
                  pl.BlockSpec((1, T, 2), lambda p: (p, 0, 0)), pl.BlockSpec((1, 2, T), lambda p: (p, 0, 0)),
                  pair, pair, cq, ck],
        out_specs=[pair, pair, pair, cq, ck],
        out_shape=[jax.ShapeDtypeStruct((T, D), BF16), jax.ShapeDtypeStruct((T, D), F32), jax.ShapeDtypeStruct((T, D), F32),
                   jax.ShapeDtypeStruct((D // 128, T, 128), F32), jax.ShapeDtypeStruct((D // 128, 8, T), F32)],
        scratch_shapes=[pltpu.VMEM((2, T, 128), BF16)],
        compiler_params=_params(("parallel",), VMEM_BIG), name="fox_bwd",
    )(q, kv, kv, do, c_col, c_row, *init)


def _tri(lower):
    r = lax.broadcasted_iota(jnp.int32, (BLK, BLK), 0)
    c = lax.broadcasted_iota(jnp.int32, (BLK, BLK), 1)
    return jnp.where((c <= r) if lower else (c >= r), 1.0, 0.0).astype(BF16)


def _gates_fwd(z, b):
    def body(z_ref, b_ref, c_ref):
        tri = _tri(True)
        carry = jnp.zeros((1, 128), F32)
        for i in range(T // BLK):
            rows = slice(i * BLK, (i + 1) * BLK)
            x = z_ref[rows, :] + b_ref[...]
            logf = jnp.minimum(x, 0.0) - jnp.log(1.0 + jnp.exp(-jnp.abs(x)))
            hi, mid, lo = _split3(logf)
            y = _dot(tri, hi, NN) + _dot(tri, mid, NN) + _dot(tri, lo, NN) + carry
            c_ref[rows, :] = y
            carry = y[BLK - 1:BLK, :]

    return pl.pallas_call(body, out_shape=jax.ShapeDtypeStruct((T, 128), F32), name="gates_fwd")(z, b)


def _gates_bwd(dc, z, b):
    def body(dc_ref, z_ref, b_ref, dz_ref, db_ref):
        tri = _tri(False)
        carry = jnp.zeros((1, 128), F32)
        db = jnp.zeros((1, 128), F32)
        for i in reversed(range(T // BLK)):
            rows = slice(i * BLK, (i + 1) * BLK)
            hi, mid, lo = _split3(dc_ref[rows, :])
            dlogf = _dot(tri, hi, NN) + _dot(tri, mid, NN) + _dot(tri, lo, NN) + carry
            carry = dlogf[0:1, :]
            x = z_ref[rows, :] + b_ref[...]
            dz = dlogf / (1.0 + jnp.exp(x))
            dz_ref[rows, :] = dz.astype(BF16)
            db = db + jnp.sum(dz, axis=0, keepdims=True)
        db_ref[...] = db

    return pl.pallas_call(
        body, out_shape=[jax.ShapeDtypeStruct((T, 128), BF16), jax.ShapeDtypeStruct((1, 128), F32)], name="gates_bwd",
    )(dc, z, b)


def _conv_pair(a_refs, cw_refs, cb_refs):
    row = lax.broadcasted_iota(jnp.int32, (T, CT), 0)
    outs = []
    for a_ref, cw_ref, cb_ref in zip(a_refs, cw_refs, cb_refs):
        z = a_ref[...]
        z1 = jnp.where(row >= 1, pltpu.roll(z, 1, 0), 0.0)
        z2 = jnp.where(row >= 2, pltpu.roll(z, 2, 0), 0.0)
        y = cw_ref[2:3, :] * z + cw_ref[1:2, :] * z1 + cw_ref[0:1, :] * z2 + cb_ref[...]
        outs.append((y, z, z1, z2))
    return outs


_GELU_K = math.sqrt(2.0 / math.pi)
N_CT = D_FF // CT


def _conv_specs():
    def at(rows, off):
        return pl.BlockSpec((rows, CT), lambda j: (0, j + off))
    return [at(T, 0), at(T, N_CT), at(3, 0), at(3, N_CT), at(1, 0), at(1, N_CT)]


def _convgate_fwd(a, cw, cb):
    def body(ag_ref, av_ref, wg_ref, wv_ref, bg_ref, bv_ref, u_ref):
        (g, _, _, _), (v, _, _, _) = _conv_pair((ag_ref, av_ref), (wg_ref, wv_ref), (bg_ref, bv_ref))
        th = jnp.tanh(_GELU_K * (g + 0.044715 * g * g * g))
        u_ref[...] = (0.5 * g * (1.0 + th) * v).astype(BF16)

    return pl.pallas_call(
        body, grid=(N_CT,), in_specs=_conv_specs(),
        out_specs=pl.BlockSpec((T, CT), lambda j: (0, j)), out_shape=jax.ShapeDtypeStruct((T, D_FF), BF16),
        compiler_params=_params(("parallel",), VMEM_BIG), name="convgate_fwd",
    )(a, a, cw, cw, cb, cb)


def _convgate_bwd(a, du, cw, cb):
    def body(ag_ref, av_ref, wg_ref, wv_ref, bg_ref, bv_ref, du_ref, da_ref, dcw_ref, dcb_ref):
        (g, gz, gz1, gz2), (v, vz, vz1, vz2) = _conv_pair((ag_ref, av_ref), (wg_ref, wv_ref), (bg_ref, bv_ref))
        du = du_ref[...].astype(F32)
        th = jnp.tanh(_GELU_K * (g + 0.044715 * g * g * g))
        gelu = 0.5 * g * (1.0 + th)
        dgelu = 0.5 * (1.0 + th) + 0.5 * g * (1.0 - th * th) * _GELU_K * (1.0 + 3 * 0.044715 * g * g)
        row = lax.broadcasted_iota(jnp.int32, (T, CT), 0)
        for h, (d, z, z1, z2, w_ref) in enumerate(((du * v * dgelu, gz, gz1, gz2, wg_ref), (du * gelu, vz, vz1, vz2, wv_ref))):
            d1 = jnp.where(row < T - 1, pltpu.roll(d, T - 1, 0), 0.0)
            d2 = jnp.where(row < T - 2, pltpu.roll(d, T - 2, 0), 0.0)
            da_ref[h] = (w_ref[2:3, :] * d + w_ref[1:2, :] * d1 + w_ref[0:1, :] * d2).astype(BF16)
            dcw_ref[h, 0:1, :] = jnp.sum(d * z2, axis=0, keepdims=True)
            dcw_ref[h, 1:2, :] = jnp.sum(d * z1, axis=0, keepdims=True)
            dcw_ref[h, 2:3, :] = jnp.sum(d * z, axis=0, keepdims=True)
            dcb_ref[h] = jnp.sum(d, axis=0, keepdims=True)

    def both(rows):
        return pl.BlockSpec((2, rows, CT), lambda j: (0, 0, j))

    return pl.pallas_call(
        body, grid=(N_CT,),
        in_specs=_conv_specs() + [pl.BlockSpec((T, CT), lambda j: (0, j))],
        out_specs=[both(T), both(3), both(1)],
        out_shape=[jax.ShapeDtypeStruct((2, T, D_FF), BF16), jax.ShapeDtypeStruct((2, 3, D_FF), F32),
                   jax.ShapeDtypeStruct((2, 1, D_FF), F32)],
        compiler_params=_params(("parallel",), VMEM_BIG), name="convgate_bwd",
    )(a, a, cw, cw, cb, cb, du)


def _halves_a(tm, tn, tk):
    per = D_FF // tk
    return lambda i, j, k: (lax.div(k, per), i, lax.rem(k, per))


def _halves_b(tm, tn, tk):
    per = D_FF // tn
    return lambda i, j, k: (lax.div(j, per), k, lax.rem(j, per))


def _adamw(w, m, v, g, *, name):
    r, c = w.shape
    tr = r
    if r * c > 256 * 1024:
        for cand in range(8, r, 8):
            if r % cand == 0 and cand * c <= 256 * 1024:
                tr = cand

    def body(w_ref, m_ref, v_ref, g_ref, d_ref, nm_ref, nv_ref):
        gv = g_ref[...]
        mn = ADAM_B1 * m_ref[...] + (1.0 - ADAM_B1) * gv
        vn = ADAM_B2 * v_ref[...] + (1.0 - ADAM_B2) * (gv * gv)
        m_hat = mn / (1.0 - ADAM_B1 ** ADAM_STEP)
        v_hat = vn / (1.0 - ADAM_B2 ** ADAM_STEP)
        d_ref[...] = -ADAM_LR * (m_hat / (jnp.sqrt(v_hat) + ADAM_EPS) + ADAM_WD * w_ref[...])
        nm_ref[...] = mn
        nv_ref[...] = vn

    blk = pl.BlockSpec((tr, c), lambda i: (i, 0))
    shp = jax.ShapeDtypeStruct((r, c), F32)
    return pl.pallas_call(
        body, grid=(r // tr,), in_specs=[blk] * 4, out_specs=[blk] * 3, out_shape=[shp] * 3,
        compiler_params=_params(("parallel",)), name=name,
    )(w, m, v, g)


def _place():
    x, y, c = lax.axis_index("x"), lax.axis_index("y"), lax.axis_index("c")
    chips = [(1 - x, y), (x, 1 - y), (1 - x, 1 - y)]
    return x, y, c, chips


def _window(ref, kind, s, half=None):
    lead = () if half is None else (half,)
    b, c = ref.shape[-2], ref.shape[-1]
    if kind == "col":
        return ref.at[lead + (slice(None), slice(None), pl.ds(s * (c // N_CHIPS), c // N_CHIPS))]
    if kind == "row":
        return ref.at[lead + (slice(None), pl.ds(s * (b // N_CHIPS), b // N_CHIPS), slice(None))]
    return ref.at[lead + (s,)]


def _window_shape(shape3, kind):
    a, b, c = shape3
    return {"col": (a, b, c // N_CHIPS), "row": (a, b // N_CHIPS, c), "slab": (b, c)}[kind]


def _allgather(tensors, kinds, *, name):
    n = len(tensors)

    def body(*refs):
        bufs = refs[n:2 * n]
        send, recv = refs[2 * n:]
        x, y, c, chips = _place()
        me = 2 * x + y
        sib = (x, y, 1 - c)

        def rcopy(i, k, win, to):
            return pltpu.make_async_remote_copy(src_ref=win, dst_ref=win, send_sem=send.at[i * 6 + k], recv_sem=recv.at[i * 6 + k],
                                                device_id=to, device_id_type=MESH)

        started = []
        for i in range(n):
            for k, (px, py) in enumerate(chips):
                cp = rcopy(i, k, _window(bufs[i], kinds[i], me, c), (px, py, c))
                cp.start()
                started.append(cp)
        for i in range(n):
            for k, (px, py) in enumerate(chips):
                landed = _window(bufs[i], kinds[i], 2 * px + py, c)
                rcopy(i, k, landed, (px, py, c)).wait_recv()
                fw = rcopy(i, 3 + k, landed, sib)
                fw.start()
                started.append(fw)
        for i in range(n):
            for k, (px, py) in enumerate(chips):
                rcopy(i, 3 + k, _window(bufs[i], kinds[i], 2 * px + py, 1 - c), sib).wait_recv()
        for cp in started:
            cp.wait_send()

    return pl.pallas_call(
        body, in_specs=[ANY] * n, out_specs=[ANY] * n,
        out_shape=[jax.ShapeDtypeStruct(t.shape, t.dtype) for t in tensors],
        scratch_shapes=[pltpu.SemaphoreType.DMA((6 * n,)), pltpu.SemaphoreType.DMA((6 * n,))],
        input_output_aliases={i: i for i in range(n)},
        name=name,
    )(*tensors)


def _rows_tile(rows, cols, sub):
    best = None
    for t in range(sub, rows + 1, sub):
        if rows % t == 0 and t * cols <= 512 * 1024:
            best = t
    return rows if best is None else best


def _sequencer(name, cid, n_sems, peers_of, body):
    @pl.kernel(mesh=plsc.ScalarSubcoreMesh(axis_name="seq", num_cores=1), name=name,
               scratch_types=(pltpu.SemaphoreType.DMA((n_sems,)), pltpu.SemaphoreType.DMA((n_sems,))),
               compiler_params=pltpu.CompilerParams(collective_id=cid))
    def launch(send, recv):
        x, y, c, chips = _place()
        peers = peers_of(x, y, c, chips)
        barrier = pltpu.get_barrier_semaphore()
        for peer in peers:
            pl.semaphore_signal(barrier, inc=1, device_id=peer, device_id_type=MESH)
        pl.semaphore_wait(barrier, len(peers))
        body(send, recv)

    launch()


def _half_of_full(ref, kind, h):
    if kind == "col":
        b = ref.shape[0]
        return ref.at[pl.ds(h * (b // 2), b // 2), :]
    if kind == "row":
        c = ref.shape[1]
        return ref.at[:, pl.ds(h * (c // 2), c // 2)]
    b = ref.shape[1]
    return ref.at[:, pl.ds(h * (b // 2), b // 2), :]


def _half_shape(full, kind):
    if kind == "col":
        return (full[0] // 2, full[1])
    if kind == "row":
        return (full[0], full[1] // 2)
    return (full[0], full[1] // 2, full[2])


def _win_of_half(ref, kind, s):
    if kind == "col":
        c = ref.shape[1]
        return ref.at[:, pl.ds(s * (c // N_CHIPS), c // N_CHIPS)]
    if kind == "row":
        b = ref.shape[0]
        return ref.at[pl.ds(s * (b // N_CHIPS), b // N_CHIPS), :]
    return ref.at[s]


def _win_shape(half, kind):
    if kind == "col":
        return (half[0], half[1] // N_CHIPS)
    if kind == "row":
        return (half[0] // N_CHIPS, half[1])
    return half[1:]


def _seq_swap(parts, kinds, *, name):
    n = len(parts)
    srcs = [jax.new_ref(p, memory_space=pltpu.MemorySpace.HBM) for p in parts]
    outs = [jax.empty_ref(jax.ShapeDtypeStruct(_half_shape(p.shape, k), p.dtype), memory_space=pltpu.MemorySpace.HBM)
            for p, k in zip(parts, kinds)]

    def body(send, recv):
        x, y, c, _ = _place()
        cps = []
        for i in range(n):
            cp = pltpu.make_async_remote_copy(src_ref=_half_of_full(srcs[i], kinds[i], 1 - c), dst_ref=outs[i], send_sem=send.at[i],
                                              recv_sem=recv.at[i], device_id=(x, y, 1 - c), device_id_type=MESH)
            cp.start()
            cps.append(cp)
        for cp in cps:
            cp.wait()

    _sequencer(name, 2, n, lambda x, y, c, chips: [(x, y, 1 - c)], body)
    return [o[...] for o in outs]


def _seq_scatter(halves, kinds, *, name):
    n = len(halves)
    srcs = [jax.new_ref(h, memory_space=pltpu.MemorySpace.HBM) for h in halves]
    outs = [jax.empty_ref(jax.ShapeDtypeStruct((3,) + _win_shape(h.shape, k), h.dtype), memory_space=pltpu.MemorySpace.HBM)
            for h, k in zip(halves, kinds)]

    def body(send, recv):
        x, y, c, chips = _place()
        cps = []
        for i in range(n):
            for k, (px, py) in enumerate(chips):
                cp = pltpu.make_async_remote_copy(src_ref=_win_of_half(srcs[i], kinds[i], 2 * px + py), dst_ref=outs[i].at[k],
                                                  send_sem=send.at[3 * i + k], recv_sem=recv.at[3 * i + k],
                                                  device_id=(px, py, c), device_id_type=MESH)
                cp.start()
                cps.append(cp)
        for cp in cps:
            cp.wait()

    _sequencer(name, 3, 3 * n, lambda x, y, c, chips: [(px, py, c) for px, py in chips], body)
    return [o[...] for o in outs]


def _add_half(g, p, kind, where, after, *, name):
    if kind == "slab":
        s, b2, c = p.shape
        tr = _rows_tile(b2, c, 16)
        nr = b2 // tr
        grid = (s, nr)
        g_spec = pl.BlockSpec((None, tr, c), lambda i, r, w: (i, w[1] * nr + r, 0))
        p_spec = pl.BlockSpec((None, tr, c), lambda i, r, w: (i, r, 0))
    elif kind == "col":
        b2, c = p.shape
        tr = _rows_tile(b2, c, 16)
        nr = b2 // tr
        grid = (1, nr)
        g_spec = pl.BlockSpec((tr, c), lambda i, r, w: (w[1] * nr + r, 0))
        p_spec = pl.BlockSpec((tr, c), lambda i, r, w: (r, 0))
    else:
        b, c2 = p.shape
        tr = _rows_tile(b, c2, 16)
        grid = (1, b // tr)
        g_spec = pl.BlockSpec((tr, c2), lambda i, r, w: (r, w[1]))
        p_spec = pl.BlockSpec((tr, c2), lambda i, r, w: (r, 0))

    def body(w_ref, g_ref, p_ref, *rest):
        o_ref = rest[-1]
        o_ref[...] = (g_ref[...].astype(F32) + p_ref[...].astype(F32)).astype(o_ref.dtype)

    extra = [] if after is None else [after]
    return pl.pallas_call(
        body,
        grid_spec=pltpu.PrefetchScalarGridSpec(num_scalar_prefetch=1, grid=grid, in_specs=[g_spec, p_spec] + [ANY] * len(extra),
                                               out_specs=p_spec),
        out_shape=jax.ShapeDtypeStruct(p.shape, g.dtype),
        compiler_params=_params(("parallel", "parallel")), name=name,
    )(where, g, p, *extra)


def _sum_chips(r, h, kind, where, layer, layers, out_buf, after, *, name):
    _, br, cr = r.shape
    tr = _rows_tile(br, cr, 16)
    nr = br // tr
    if kind == "col":
        h_spec = pl.BlockSpec((tr, cr), lambda j, w: (j, w[0]))
        o_shape, o_spec = (layers, 2 * br, cr), pl.BlockSpec((None, tr, cr), lambda j, w: (layer, w[1] * nr + j, 0))
    elif kind == "row":
        h_spec = pl.BlockSpec((tr, cr), lambda j, w: (w[0] * nr + j, 0))
        o_shape, o_spec = (layers, br, 2 * cr), pl.BlockSpec((None, tr, cr), lambda j, w: (layer, j, w[1]))
    else:
        h_spec = pl.BlockSpec((None, tr, cr), lambda j, w: (w[0], j, 0))
        o_shape, o_spec = (layers, 2 * br, cr), pl.BlockSpec((None, tr, cr), lambda j, w: (layer, w[1] * nr + j, 0))

    def body(w_ref, h_ref, r0_ref, r1_ref, r2_ref, *rest):
        o_ref, t_ref = rest[-2], rest[-1]
        o_ref[...] = ((h_ref[...].astype(F32) + r0_ref[...].astype(F32)) + r1_ref[...].astype(F32)) + r2_ref[...].astype(F32)
        t_ref[...] = jnp.zeros_like(t_ref)

    def slot(k):
        return pl.BlockSpec((None, tr, cr), lambda j, w: (k, j, 0))

    ins, specs, alias = [h, r, r, r], [h_spec, slot(0), slot(1), slot(2)], {}
    if after is not None:
        ins.append(after)
        specs.append(ANY)
    if out_buf is not None:
        alias = {1 + len(ins): 0}
        ins.append(out_buf)
        specs.append(ANY)
    return pl.pallas_call(
        body,
        grid_spec=pltpu.PrefetchScalarGridSpec(num_scalar_prefetch=1, grid=(nr,), in_specs=specs,
                                               out_specs=[o_spec, pl.BlockSpec((8, 128), lambda j, w: (0, 0))]),
        out_shape=[jax.ShapeDtypeStruct(o_shape, F32), jax.ShapeDtypeStruct((8, 128), F32)], input_output_aliases=alias,
        compiler_params=_params(("arbitrary",)), name=name,
    )(where, *ins)


def _join_halves(tensors, kinds, *, name):
    n = len(tensors)

    def mine(ref, kind, h):
        if kind == "row":
            c = ref.shape[2]
            return ref.at[:, :, pl.ds(h * (c // 2), c // 2)]
        b = ref.shape[1]
        return ref.at[:, pl.ds(h * (b // 2), b // 2), :]

    def body(*refs):
        bufs = refs[n:2 * n]
        send, recv = refs[2 * n:]
        x, y, c, _ = _place()
        cps = []
        for i in range(n):
            part = mine(bufs[i], kinds[i], c)
            cp = pltpu.make_async_remote_copy(src_ref=part, dst_ref=part, send_sem=send.at[i],
                                              recv_sem=recv.at[i], device_id=(x, y, 1 - c), device_id_type=MESH)
            cp.start()
            cps.append(cp)
        for i in range(n):
            other = mine(bufs[i], kinds[i], 1 - c)
            pltpu.make_async_remote_copy(src_ref=other, dst_ref=other, send_sem=send.at[i],
                                         recv_sem=recv.at[i], device_id=(x, y, 1 - c), device_id_type=MESH).wait_recv()
        for cp in cps:
            cp.wait_send()

    return pl.pallas_call(
        body, in_specs=[ANY] * n, out_specs=[ANY] * n,
        out_shape=[jax.ShapeDtypeStruct(t.shape, t.dtype) for t in tensors],
        scratch_shapes=[pltpu.SemaphoreType.DMA((n,)), pltpu.SemaphoreType.DMA((n,))],
        input_output_aliases={i: i for i in range(n)},
        name=name,
    )(*tensors)


def _win(ref, kind, s, h=None):
    if kind == "col":
        b, c = ref.shape
        cols = pl.ds(s * (c // N_CHIPS), c // N_CHIPS)
        return ref.at[:, cols] if h is None else ref.at[pl.ds(h * (b // 2), b // 2), cols]
    if kind == "row":
        b, c = ref.shape
        rows = pl.ds(s * (b // N_CHIPS), b // N_CHIPS)
        return ref.at[rows, :] if h is None else ref.at[rows, pl.ds(h * (c // 2), c // 2)]
    b = ref.shape[1]
    return ref.at[s] if h is None else ref.at[s, pl.ds(h * (b // 2), b // 2)]


def _half(ref, kind, h):
    b, c = ref.shape
    if kind == "row":
        return ref.at[:, pl.ds(h * (c // 2), c // 2)]
    return ref.at[pl.ds(h * (b // 2), b // 2), :]


def _full_shape(shard_shape, kind):
    b, c = shard_shape
    return {"col": (b, N_CHIPS * c), "row": (N_CHIPS * b, c), "slab": (N_CHIPS, b, c)}[kind]


def _gather_body(srcs, outs, kinds, send, recv):
    x, y, c, chips = _place()
    me = 2 * x + y
    sib = (x, y, 1 - c)

    def rcopy(i, k, src, dst, to):
        return pltpu.make_async_remote_copy(src_ref=src, dst_ref=dst, send_sem=send.at[7 * i + k], recv_sem=recv.at[7 * i + k],
                                            device_id=to, device_id_type=MESH)

    started = []
    for i, (src, out, kind) in enumerate(zip(srcs, outs, kinds)):
        own = rcopy(i, 6, src, _win(out, kind, me), sib)
        own.start()
        started.append(own)
        for k, (px, py) in enumerate(chips):
            cp = rcopy(i, k, _half(src, kind, c), _win(out, kind, me, c), (px, py, c))
            cp.start()
            started.append(cp)
    for i, (out, kind) in enumerate(zip(outs, kinds)):
        for k, (px, py) in enumerate(chips):
            landed = _win(out, kind, 2 * px + py, c)
            rcopy(i, k, landed, landed, (px, py, c)).wait_recv()
            fw = rcopy(i, 3 + k, landed, landed, sib)
            fw.start()
            started.append(fw)
    for i, (src, out, kind) in enumerate(zip(srcs, outs, kinds)):
        for k, (px, py) in enumerate(chips):
            other = _win(out, kind, 2 * px + py, 1 - c)
            rcopy(i, 3 + k, other, other, sib).wait_recv()
        rcopy(i, 6, src, _win(out, kind, me), sib).wait_recv()
    for cp in started:
        cp.wait_send()


def _seq_gather(shards, kinds, *, name, cid):
    n = len(shards)
    srcs = [jax.new_ref(s, memory_space=pltpu.MemorySpace.HBM) for s in shards]
    outs = [jax.empty_ref(jax.ShapeDtypeStruct(_full_shape(s.shape, k), s.dtype), memory_space=pltpu.MemorySpace.HBM)
            for s, k in zip(shards, kinds)]

    @pl.kernel(mesh=plsc.ScalarSubcoreMesh(axis_name="seq", num_cores=1), name=name,
               scratch_types=(pltpu.SemaphoreType.DMA((7 * n,)), pltpu.SemaphoreType.DMA((7 * n,))),
               compiler_params=pltpu.CompilerParams(collective_id=cid))
    def launch(send, recv):
        x, y, c, chips = _place()
        barrier = pltpu.get_barrier_semaphore()
        for px, py in chips:
            pl.semaphore_signal(barrier, inc=1, device_id=(px, py, c), device_id_type=MESH)
        pl.semaphore_signal(barrier, inc=1, device_id=(x, y, 1 - c), device_id_type=MESH)
        pl.semaphore_wait(barrier, 4)
        _gather_body(srcs, outs, kinds, send, recv)

    launch()
    return [o[...] for o in outs]


KIND = dict(w_qkv_a="slab", w_o_a="col", w_q_b="row", w_o_b="row", w_kvf="slab", w_up="col", w_down="row", small="slab")
LAYERS = dict(w_qkv_a=N_A, w_o_a=N_A, w_q_b=DEPTH - N_A, w_o_b=DEPTH - N_A, w_kvf=1, w_up=DEPTH, w_down=DEPTH, small=1)
SMALL_W = 1792
SMALL_ROWS = 8


class _Reducer:
    def __init__(self, where):
        self.where = where
        self.acc = {nm: None for nm in KIND}
        self.pending = None

    def __call__(self, group, tag):
        names, layers, parts = zip(*group)
        kinds = [KIND[nm] for nm in names]
        summed = self._sum_pending(after=parts[-1])
        sib = _seq_swap(list(parts), kinds, name="reduce_swap_" + tag)
        halves = []
        for g, p, k, nm in zip(parts, sib, kinds, names):
            halves.append(_add_half(g, p, k, self.where, halves[-1] if halves else None, name="reduce_add_" + nm))
        landed = _seq_scatter(halves, kinds, name="reduce_scatter_" + tag)
        self.pending = (names, layers, landed, halves, kinds)
        return [halves[-1], summed]

    def _sum_pending(self, after):
        if self.pending is None:
            return None
        for nm, l, r, h, k in zip(*self.pending):
            self.acc[nm], after = _sum_chips(r, h, k, self.where, l, LAYERS[nm], self.acc[nm], after, name="reduce_sum_" + nm)
        self.pending = None
        return after

    def finish(self):
        self._sum_pending(after=None)
        names = list(KIND)
        joined = _join_halves([self.acc[nm] for nm in names], [KIND[nm] for nm in names], name="reduce_pair_join")
        return dict(zip(names, joined))


def _headsum_matrix():
    r = lax.broadcasted_iota(jnp.int32, (GW, GW), 0) // HD
    c = lax.broadcasted_iota(jnp.int32, (GW, GW), 1) // HD
    return jnp.where(r == c, 1.0, 0.0).astype(BF16)


def kernel(x, norm_gains, w_qkv_a, w_o_a, w_q_b, w_o_b, kv_norm, w_kvf, b_f, w_up, conv_w, conv_b, w_down, loss_target, m_norm_gains, m_w_qkv_a, m_w_o_a, m_w_q_b, m_w_o_b, m_kv_norm, m_w_kvf, m_b_f, m_w_up, m_conv_w, m_conv_b, m_w_down, v_norm_gains, v_w_qkv_a, v_w_o_a, v_w_q_b, v_w_o_b, v_kv_norm, v_w_kvf, v_b_f, v_w_up, v_conv_w, v_conv_b, v_w_down):
    xi, yi, ci = lax.axis_index("x"), lax.axis_index("y"), lax.axis_index("c")
    chip = 2 * xi + yi
    where = jnp.stack([chip, ci]).astype(jnp.int32)
    ws = dict(norm_gains=norm_gains, w_qkv_a=w_qkv_a, w_o_a=w_o_a, w_q_b=w_q_b, w_o_b=w_o_b, kv_norm=kv_norm, w_kvf=w_kvf,
              b_f=b_f, w_up=w_up, conv_w=conv_w, conv_b=conv_b, w_down=w_down)
    ms = dict(norm_gains=m_norm_gains, w_qkv_a=m_w_qkv_a, w_o_a=m_w_o_a, w_q_b=m_w_q_b, w_o_b=m_w_o_b, kv_norm=m_kv_norm,
              w_kvf=m_w_kvf, b_f=m_b_f, w_up=m_w_up, conv_w=m_conv_w, conv_b=m_conv_b, w_down=m_w_down)
    vs = dict(norm_gains=v_norm_gains, w_qkv_a=v_w_qkv_a, w_o_a=v_w_o_a, w_q_b=v_w_q_b, w_o_b=v_w_o_b, kv_norm=v_kv_norm,
              w_kvf=v_w_kvf, b_f=v_b_f, w_up=v_w_up, conv_w=v_conv_w, conv_b=v_conv_b, w_down=v_w_down)

    small = jnp.concatenate([
        jnp.pad(norm_gains.reshape(16, 256), ((0, 0), (0, 1408 - 256))),
        jnp.pad(conv_w.reshape(12, 1408), ((0, 4), (0, 0)))], axis=0)
    big = [nm for nm in KIND if nm != "small"]
    half = {nm: ws[nm].astype(BF16) for nm in big}
    W = {nm: [None] * LAYERS[nm] for nm in big if nm != "w_kvf"}
    g_small = None
    groups = [("0a", [("w_qkv_a", 0), ("w_o_a", 0), ("small", 0)]), ("0b", [("w_up", 0), ("w_down", 0)]),
              ("1", [("w_qkv_a", 1), ("w_o_a", 1), ("w_up", 1), ("w_down", 1)]),
              ("2", [("w_kvf", 0), ("w_q_b", 0), ("w_o_b", 0), ("w_up", 2), ("w_down", 2)]),
              ("3", [("w_q_b", 1), ("w_o_b", 1), ("w_up", 3), ("w_down", 3)])]
    for tag, group in groups:
        shards = [small if nm == "small" else half[nm] if nm == "w_kvf" else half[nm][i] for nm, i in group]
        got = _seq_gather(shards, [KIND[nm] for nm, _ in group], name="gather_layer" + tag, cid=1)
        for (nm, i), g in zip(group, got):
            if nm == "small":
                g_small = g
            elif nm == "w_kvf":
                W[nm] = g.transpose(1, 0, 2).reshape(D, 2 * D + 16)
            else:
                W[nm][i] = g.transpose(1, 0, 2).reshape(D, 3 * A_W) if nm == "w_qkv_a" else g
    gains = g_small[:, :16, :256].transpose(1, 0, 2).reshape(DEPTH, 4, 1, D)
    cw_full = g_small[:, 16:28, :].transpose(1, 0, 2).reshape(DEPTH, 3, 2 * D_FF)
    cb_full = conv_b.reshape(DEPTH, 1, 2 * D_FF)

    reducer = _Reducer(where)
    sq, dh = _fwd_bwd(x[0], loss_target[0], W, gains, cw_full, cb_full, kv_norm, b_f, reducer)
    loss = lax.psum(sq[0, 0] * (0.5 / D), ("x", "y", "c"))
    return _update(loss, dh[None], reducer.finish(), chip, ws, ms, vs)


def _fwd_bwd(h, target, W, gains, cw_full, cb_full, kv_norm, b_f, reduce):
    w_kv = W["w_kvf"][:, :2 * D]
    w_kvf_pad = jnp.pad(W["w_kvf"], ((0, 0), (0, 128 - 16)))
    w_f = w_kvf_pad[:, 2 * D:]
    kvn_g = kv_norm.reshape(1, D)
    bf_pad = jnp.pad(b_f, (0, 128 - 16)).reshape(1, 128)
    tabs = _rope_tables()
    headsum = _headsum_matrix()

    saved = []
    kv = zf = c_col = c_row = kvn = h_kv = None
    for l in range(DEPTH):
        s = {"h": h}
        g = gains[l]
        xn = _rms_fwd(h, g[0], out_dtype=BF16, name="rms_in")
        s["xn"] = xn
        if l < N_A:
            qkv = _matmul(xn, W["w_qkv_a"][l], mode="nn", out_dtype=F32, name="mm_qkv", mnk=(T, 3 * A_W, D), tn=768)
            q3, k3, v3 = _rope_fwd(qkv, tabs)
            qp, kp, vp = _perm(q3), _perm(k3), _perm(v3)
            o_p, lse_p = _band_fwd(qp, kp, vp)
            o3, lse3 = _unperm(o_p), _unperm(lse_p)
            att = _combine_fwd(o3, lse3)
            s.update(qp=qp, kp=kp, vp=vp, o3=o3, lse3=lse3, lse_p=lse_p, att=att)
            mix = _matmul(att, W["w_o_a"][l], mode="nn", out_dtype=F32, name="mm_oa", mnk=(T, D, A_W))
        else:
            j = l - N_A
            if l == N_A:
                h_kv = h
                kvn = _rms_fwd(h, kvn_g, out_dtype=BF16, name="rms_in")
                kv = _matmul(kvn, w_kv, mode="nn", out_dtype=BF16, name="mm_kv")
                zf = _matmul(kvn, w_f, mode="nn", out_dtype=F32, name="mm_f")
                cum = _gates_fwd(zf, bf_pad)[:, :16]
                c_col = cum.reshape(T, 8, 2).transpose(1, 0, 2)
                c_row = cum.T.reshape(8, 2, T)
            q = _matmul(xn, W["w_q_b"][j], mode="nn", out_dtype=BF16, name="mm_qb", mnk=(T, D, D), alpha=HD ** -0.5)
            o = _fox_fwd(q, kv, c_col, c_row)
            s.update(q=q, o=o)
            mix = _matmul(o, W["w_o_b"][j], mode="nn", out_dtype=F32, name="mm_ob", mnk=(T, D, D))
        s["mix"] = mix
        h1 = _rms_fwd(mix, g[1], res=h, out_dtype=F32, name="rms_res")
        xn2 = _rms_fwd(h1, g[2], out_dtype=BF16, name="rms_in")
        a = _matmul(xn2, W["w_up"][l], mode="nn", out_dtype=F32, name="mm_up", mnk=(T, 2 * D_FF, D))
        u = _convgate_fwd(a, cw_full[l], cb_full[l])
        f = _matmul(u, W["w_down"][l], mode="nn", out_dtype=F32, name="mm_down", mnk=(T, D, D_FF), tm=1024, tk=D_FF)
        h = _rms_fwd(f, g[3], res=h1, out_dtype=F32, name="rms_res")
        s.update(h1=h1, xn2=xn2, a=a, u=u, f=f)
        saved.append(s)

    dh, sq = _loss_head(h, target)

    d_gains = [[None] * 4 for _ in range(DEPTH)]
    d_cw, d_cb = [None] * DEPTH, [None] * DEPTH
    zeros_td = jnp.zeros((T, D), F32)
    fox_acc = (zeros_td, zeros_td, jnp.zeros((D // 128, T, 128), F32), jnp.zeros((D // 128, 8, T), F32))
    d_kvnorm = d_bf = token = None

    def dw(nm, a, b, **kw):
        return _matmul(a, b, mode="tn", out_dtype=BF16, name="mm_dw_" + nm, **kw)

    def slabs(full, width):
        return full.reshape(full.shape[0], N_CHIPS, width).transpose(1, 0, 2)

    for l in reversed(range(DEPTH)):
        s = saved[l]
        g = gains[l]
        df, d_gains[l][3] = _rms_bwd(dh, s["f"], g[3], out_dtype=BF16, name="rms_bwd")
        du = _matmul(df, W["w_down"][l], mode="nt", out_dtype=F32, name="mm_down_dx", mnk=(T, D_FF, D), tn=256, after=token)
        g_down = dw("w_down", s["u"], df, tm=1408, tn=1024)
        da, d_cw[l], d_cb[l] = _convgate_bwd(s["a"], du, cw_full[l], cb_full[l])
        dxn2 = _matmul(da, W["w_up"][l], mode="nt", out_dtype=F32, name="mm_up_dx", mnk=(T, D, 2 * D_FF), tm=1024, tn=1024, tk=1408,
                       a_map=_halves_a)
        g_up = dw("w_up", s["xn2"], da, mnk=(D, 2 * D_FF, T), tn=1408, b_map=_halves_b)
        token = reduce([("w_down", l, g_down), ("w_up", l, g_up)], "ffn%d" % l)
        dh1, d_gains[l][2] = _rms_bwd(dxn2, s["h1"], g[2], dres=dh, out_dtype=F32, name="rms_bwd_res")
        dmix, d_gains[l][1] = _rms_bwd(dh1, s["mix"], g[1], out_dtype=BF16, name="rms_bwd")
        if l < N_A:
            datt = _matmul(dmix, W["w_o_a"][l], mode="nt", out_dtype=F32, name="mm_oa_dx", mnk=(T, A_W, D), tn=768, after=token)
            g_o = dw("w_o_a", s["att"], dmix, tm=768, tn=1024)
            do3, dlt3 = _combine_bwd(datt, s["o3"], s["lse3"], headsum)
            dqp, dkp, dvp = _band_bwd(s["qp"], s["kp"], s["vp"], _perm(do3), s["lse_p"], _perm(dlt3))
            dqkv = _rope_bwd(_unperm(dqp), _unperm(dkp), _unperm(dvp), tabs)
            dxn = _matmul(dqkv, W["w_qkv_a"][l], mode="nt", out_dtype=F32, name="mm_qkv_dx", mnk=(T, D, 3 * A_W), tm=1024, tn=1024, tk=3 * A_W)
            g_qkv = dw("w_qkv_a", s["xn"], dqkv, tn=768)
            group = [("w_o_a", l, g_o), ("w_qkv_a", l, slabs(g_qkv, 576))]
        else:
            j = l - N_A
            do = _matmul(dmix, W["w_o_b"][j], mode="nt", out_dtype=BF16, name="mm_ob_dx", mnk=(T, D, D), after=token)
            g_o = dw("w_o_b", s["o"], dmix, tn=1024)
            dq, *fox_acc = _fox_bwd(s["q"], kv, do, c_col, c_row, fox_acc)
            dxn = _matmul(dq, W["w_q_b"][j], mode="nt", out_dtype=F32, name="mm_qb_dx", mnk=(T, D, D))
            g_q = dw("w_q_b", s["xn"], dq, tn=1024)
            group = [("w_o_b", j, g_o), ("w_q_b", j, g_q)]
        dh, d_gains[l][0] = _rms_bwd(dxn, s["h"], g[0], dres=dh1, out_dtype=F32, name="rms_bwd_res")
        if l == N_A:
            dk, dv, dcq, dck = fox_acc
            dc16 = dcq[:, :, :2].transpose(1, 0, 2).reshape(T, 16) - dck[:, :2, :].reshape(16, T).T
            dzf, d_bf = _gates_bwd(jnp.pad(dc16, ((0, 0), (0, 128 - 16))), zf, bf_pad)
            dkvf = jnp.concatenate([dk.astype(BF16), dv.astype(BF16), dzf], axis=1)
            g_kvf = _matmul(kvn, dkvf, mode="tn", out_dtype=BF16, name="mm_kvf_dw", tm=512, tn=2 * D + 128)[:, :2 * D + 16]
            dkvn = _matmul(dkvf, w_kvf_pad, mode="nt", out_dtype=F32, name="mm_kvf_dx", tm=1024, tn=1024, tk=2 * D + 128)
            dh, d_kvnorm = _rms_bwd(dkvn, h_kv, kvn_g, dres=dh, out_dtype=F32, name="rms_bwd_res")
            group.append(("w_kvf", 0, slabs(g_kvf, 516)))
        token = reduce(group, "mix%d" % l)
    small_flat = jnp.concatenate([
        jnp.stack([jnp.stack(r) for r in d_gains]).reshape(-1),
        jnp.stack(d_cw).transpose(0, 2, 1, 3).reshape(-1),
        jnp.stack(d_cb).reshape(-1),
        d_kvnorm.reshape(-1), d_bf[0, :16]])
    small = jnp.pad(small_flat, (0, 2 * N_CHIPS * SMALL_ROWS * SMALL_W - small_flat.shape[0]))
    reduce([("small", 0, small.reshape(N_CHIPS, 2 * SMALL_ROWS, SMALL_W))], "small")
    return sq, dh


def _update(loss, grad_x, reduced, chip, ws, ms, vs):
    red_s = reduced.pop("small")
    buf_s = lax.dynamic_update_slice(jnp.zeros((2, N_CHIPS, SMALL_ROWS, SMALL_W), F32), red_s.reshape(2, 1, SMALL_ROWS, SMALL_W),
                                     (0, chip, 0, 0))
    (all_s,) = _allgather([buf_s], ["slab"], name="gather_small_grads")
    sflat = all_s.transpose(1, 0, 2, 3).reshape(-1)

    grads = {nm: r.reshape(ws[nm].shape) for nm, r in reduced.items()}
    o = 0
    g_gains_full = sflat[o:o + 16 * D].reshape(DEPTH, 4, D); o += 16 * D
    g_cw_full = sflat[o:o + 12 * 2 * D_FF].reshape(DEPTH, 3, 2 * D_FF); o += 12 * 2 * D_FF
    grads["conv_b"] = sflat[o:o + 4 * 2 * D_FF].reshape(DEPTH, 2 * D_FF); o += 4 * 2 * D_FF
    grads["kv_norm"] = sflat[o:o + D]; o += D
    grads["b_f"] = sflat[o:o + 16]
    grads["norm_gains"] = lax.dynamic_slice_in_dim(g_gains_full, chip * 256, 256, axis=2)
    grads["conv_w"] = lax.dynamic_slice_in_dim(g_cw_full, chip * 1408, 1408, axis=2)

    names = ["norm_gains", "w_qkv_a", "w_o_a", "w_q_b", "w_o_b", "kv_norm", "w_kvf", "b_f", "w_up", "conv_w", "conv_b", "w_down"]
    deltas, new_m, new_v = {}, {}, {}
    for nm in names:
        shp = ws[nm].shape
        two = (math.prod(shp[:-1]), shp[-1]) if len(shp) > 1 else (1, shp[0])
        d, m2, v2 = _adamw(ws[nm].reshape(two), ms[nm].reshape(two), vs[nm].reshape(two), grads[nm].reshape(two),
                           name="adamw_" + nm)
        deltas[nm], new_m[nm], new_v[nm] = d.reshape(shp), m2.reshape(shp), v2.reshape(shp)

    return (loss, grad_x, *[grads[nm] for nm in names], *[deltas[nm] for nm in names],
            *[new_m[nm] for nm in names], *[new_v[nm] for nm in names])
```

```python
import math

import jax
import jax.numpy as jnp
from jax import lax
from jax.experimental import pallas as pl
from jax.experimental.pallas import tpu as pltpu
from jax.experimental.pallas import tpu_sc as plsc

F32 = jnp.float32
BF16 = jnp.bfloat16
MESH = pl.DeviceIdType.MESH
ANY = pl.BlockSpec(memory_space=pl.ANY)

T = 2048
D = 1024
HD = 64
DEPTH = 4
N_A = 2
A_W = 768
GW = 256
DIL = (1, 4, 16)
BLK = 128
D_FF = 2816
ROPE_THETA = 500000.0
EPS = 1e-6
NEG = -1e30
N_CHIPS = 4
FQ = 256
CT = 128
VMEM_BIG = 48 * 1024 * 1024

ADAM_LR, ADAM_B1, ADAM_B2, ADAM_EPS, ADAM_WD, ADAM_STEP = 0.001, 0.9, 0.999, 1e-08, 0.01, 10

NN = (((1,), (0,)), ((), ()))
NT = (((1,), (1,)), ((), ()))
TN = (((0,), (0,)), ((), ()))


def _dot(a, b, dims):
    return lax.dot_general(a, b, dims, preferred_element_type=F32)


def _pick(dim, pref):
    if dim <= pref:
        return dim
    best = None
    for t in range(128, pref + 1, 128):
        if dim % t == 0:
            best = t
    assert best is not None, (dim, pref)
    return best


def _params(sem=None, vmem=None):
    kw = {}
    if sem is not None:
        kw["dimension_semantics"] = sem
    if vmem is not None:
        kw["vmem_limit_bytes"] = vmem
    return pltpu.CompilerParams(**kw)


def _matmul(a, b, *, mode, out_dtype, name, mnk=None, alpha=None, tm=2048, tn=512, tk=2048,
            a_map=None, b_map=None, acc_init=None, out_slab=None, out_slabs=None, out_buf=None, after=None):
    if mnk is not None:
        M, N, K = mnk
    elif mode == "nn":
        (M, K), (_, N) = a.shape, b.shape
    elif mode == "nt":
        (M, K), (N, _) = a.shape, b.shape
    else:
        (K, M), (_, N) = a.shape, b.shape
    tm, tn, tk = _pick(M, tm), _pick(N, tn), _pick(K, tk)
    nk = K // tk
    dims = {"nn": NN, "nt": NT, "tn": TN}[mode]
    after = [t for t in (after or ()) if t is not None]
    n_in = 2 + (acc_init is not None) + len(after) + (out_buf is not None)

    def body(*refs):
        a_ref, b_ref = refs[0], refs[1]
        o_ref = refs[n_in]
        k = pl.program_id(2)

        def finish(r):
            if alpha is not None:
                r = r * alpha
            o_ref[...] = r.astype(out_dtype)

        def product():
            r = _dot(a_ref[...], b_ref[...], dims)
            return r if acc_init is None else r + refs[2][...]

        if nk == 1:
            finish(product())
            return
        acc_ref = refs[n_in + 1]

        @pl.when(k == 0)
        def _():
            acc_ref[...] = product()

        @pl.when((k > 0) & (k < nk - 1))
        def _():
            acc_ref[...] += _dot(a_ref[...], b_ref[...], dims)

        @pl.when(k == nk - 1)
        def _():
            finish(acc_ref[...] + _dot(a_ref[...], b_ref[...], dims))

    a_blk = (tk, tm) if mode == "tn" else (tm, tk)
    b_blk = (tn, tk) if mode == "nt" else (tk, tn)
    if a_map is not None:
        a_spec = pl.BlockSpec((None,) + a_blk, a_map(tm, tn, tk))
    elif mode == "tn":
        a_spec = pl.BlockSpec(a_blk, lambda i, j, k: (k, i))
    else:
        a_spec = pl.BlockSpec(a_blk, lambda i, j, k: (i, k))
    if b_map is not None:
        b_spec = pl.BlockSpec((None,) + b_blk, b_map(tm, tn, tk))
    elif mode == "nt":
        b_spec = pl.BlockSpec(b_blk, lambda i, j, k: (j, k))
    else:
        b_spec = pl.BlockSpec(b_blk, lambda i, j, k: (k, j))
    ins, specs, alias = [a, b], [a_spec, b_spec], {}
    if acc_init is not None:
        ins.append(acc_init)
        specs.append(pl.BlockSpec((tm, tn), lambda i, j, k: (i, j)))
    ins += after
    specs += [ANY] * len(after)
    if out_buf is not None:
        alias = {len(ins): 0}
        ins.append(out_buf)
        specs.append(ANY)
    if out_slab is None:
        o_spec = pl.BlockSpec((tm, tn), lambda i, j, k: (i, j))
        o_shape = jax.ShapeDtypeStruct((M, N), out_dtype)
    else:
        o_spec = pl.BlockSpec((None, tm, tn), lambda i, j, k: (out_slab, i, j))
        o_shape = jax.ShapeDtypeStruct((out_slabs, M, N), out_dtype)
    return pl.pallas_call(
        body,
        grid=(M // tm, N // tn, nk),
        in_specs=specs,
        out_specs=o_spec,
        out_shape=o_shape,
        scratch_shapes=[pltpu.VMEM((tm, tn), F32)] if nk > 1 else [],
        input_output_aliases=alias,
        compiler_params=_params(("parallel", "parallel", "arbitrary"), VMEM_BIG),
        name=name,
    )(*ins)


def _slab(l, mode):
    if mode == "nt":
        return lambda tm, tn, tk: (lambda i, j, k: (l, j, k))
    return lambda tm, tn, tk: (lambda i, j, k: (l, k, j))


def _rms_fwd(x, g, *, out_dtype, name, res=None, tr=256):
    n, d = x.shape

    def body(*refs):
        x_ref, g_ref = refs[0], refs[1]
        o_ref = refs[-1]
        xv = x_ref[...].astype(F32)
        y = xv * lax.rsqrt(jnp.mean(xv * xv, axis=-1, keepdims=True) + EPS) * g_ref[...]
        if res is not None:
            y = y + refs[2][...]
        o_ref[...] = y.astype(out_dtype)

    row = pl.BlockSpec((tr, d), lambda i: (i, 0))
    vec = pl.BlockSpec((1, d), lambda i: (0, 0))
    ins = [x, g] + ([] if res is None else [res])
    specs = [row, vec] + ([] if res is None else [row])
    return pl.pallas_call(
        body, grid=(n // tr,), in_specs=specs, out_specs=row,
        out_shape=jax.ShapeDtypeStruct((n, d), out_dtype),
        compiler_params=_params(("parallel",)), name=name,
    )(*ins)


def _rms_bwd(dy, x, g, *, out_dtype, name, dres=None, tr=256):
    n, d = x.shape

    def body(*refs):
        dy_ref, x_ref, g_ref = refs[0], refs[1], refs[2]
        dx_ref, dg_ref = refs[-2], refs[-1]
        xv = x_ref[...].astype(F32)
        dyv = dy_ref[...].astype(F32)
        rstd = lax.rsqrt(jnp.mean(xv * xv, axis=-1, keepdims=True) + EPS)
        xhat = xv * rstd
        dxh = dyv * g_ref[...]
        dx = rstd * (dxh - xhat * jnp.mean(dxh * xhat, axis=-1, keepdims=True))
        if dres is not None:
            dx = dx + refs[3][...]
        dx_ref[...] = dx.astype(out_dtype)

        @pl.when(pl.program_id(0) == 0)
        def _():
            dg_ref[...] = jnp.zeros_like(dg_ref)

        dg_ref[...] += jnp.sum(dyv * xhat, axis=0, keepdims=True)

    row = pl.BlockSpec((tr, d), lambda i: (i, 0))
    vec = pl.BlockSpec((1, d), lambda i: (0, 0))
    ins = [dy, x, g] + ([] if dres is None else [dres])
    specs = [row, row, vec] + ([] if dres is None else [row])
    return pl.pallas_call(
        body, grid=(n // tr,), in_specs=specs, out_specs=[row, vec],
        out_shape=[jax.ShapeDtypeStruct((n, d), out_dtype), jax.ShapeDtypeStruct((1, d), F32)],
        compiler_params=_params(("arbitrary",)), name=name,
    )(*ins)


def _loss_head(h, target, *, tr=256):
    n, d = h.shape

    def body(h_ref, t_ref, dh_ref, s_ref):
        err = h_ref[...] - t_ref[...]
        dh_ref[...] = err * (1.0 / d)

        @pl.when(pl.program_id(0) == 0)
        def _():
            s_ref[...] = jnp.zeros_like(s_ref)

        s_ref[...] += jnp.sum(err * err)

    row = pl.BlockSpec((tr, d), lambda i: (i, 0))
    acc = pl.BlockSpec((8, 128), lambda i: (0, 0))
    return pl.pallas_call(
        body, grid=(n // tr,), in_specs=[row, row], out_specs=[row, acc],
        out_shape=[jax.ShapeDtypeStruct((n, d), F32), jax.ShapeDtypeStruct((8, 128), F32)],
        compiler_params=_params(("arbitrary",)), name="loss_head",
    )(h, target)


def _rope_tables():
    pos = jnp.arange(T, dtype=F32)
    inv = ROPE_THETA ** (-jnp.arange(0, 16, 2, dtype=F32) / 16)
    ang = pos[:, None] * inv[None, :]
    cos, sin = jnp.cos(ang), jnp.sin(ang)
    one = jnp.ones((T, HD - 16), F32)
    zero8 = jnp.zeros((T, 8), F32)
    zero = jnp.zeros((T, HD - 16), F32)
    c = jnp.concatenate([cos, cos, one], axis=1)
    s1 = jnp.concatenate([zero8, sin, zero], axis=1)
    s2 = jnp.concatenate([-sin, zero8, zero], axis=1)
    c, s1, s2 = (jnp.concatenate([t, t], axis=1) for t in (c, s1, s2))
    scale = HD ** -0.5
    return (jnp.stack([c * scale, c, jnp.ones_like(c)]), jnp.stack([s1 * scale, s1, jnp.zeros_like(c)]),
            jnp.stack([s2 * scale, s2, jnp.zeros_like(c)]))


def _row_chunks(r):
    if r == 1:
        n = 4
        return [(slice(i * (T // n), (i + 1) * (T // n)),) * 2 for i in range(n)]
    per = T // r
    return [(pl.ds(j, per, stride=r), slice(j * per, (j + 1) * per)) for j in range(r)]


def _rope_fwd(qkv, tabs):
    def body(x_ref, c_ref, s1_ref, s2_ref, o_ref):
        g = lax.rem(lax.div(pl.program_id(0), 2), 3)
        for gi, r in enumerate(DIL):
            @pl.when(g == gi)
            def _(r=r):
                for tok, prm in _row_chunks(r):
                    x = x_ref[tok, :]
                    y = x * c_ref[tok, :] + pltpu.roll(x, 8, 1) * s1_ref[tok, :] + pltpu.roll(x, 120, 1) * s2_ref[tok, :]
                    o_ref[prm, :] = y.astype(BF16)

    tab = pl.BlockSpec((None, T, 128), lambda b: (lax.div(b, 6), 0, 0))
    return pl.pallas_call(
        body, grid=(18,), in_specs=[pl.BlockSpec((T, 128), lambda b: (0, b)), tab, tab, tab],
        out_specs=pl.BlockSpec((None, T, 128), lambda b: (b, 0, 0)), out_shape=jax.ShapeDtypeStruct((18, T, 128), BF16),
        compiler_params=_params(("parallel",)), name="rope_fwd",
    )(qkv, *tabs)


def _rope_bwd(d, which, tabs, out_buf):
    def body(d_ref, c_ref, s1_ref, s2_ref, *rest):
        o_ref, tok_ref = rest[-2], rest[-1]
        g = lax.div(pl.program_id(0), 2)
        for gi, r in enumerate(DIL):
            @pl.when(g == gi)
            def _(r=r):
                for tok, prm in _row_chunks(r):
                    tok_ref[tok, :] = d_ref[prm, :]
                for rows, _ in _row_chunks(1):
                    gx = tok_ref[rows, :]
                    y = gx * c_ref[rows, :] + pltpu.roll(gx * s1_ref[rows, :], 120, 1) + pltpu.roll(gx * s2_ref[rows, :], 8, 1)
                    o_ref[rows, :] = y.astype(BF16)

    tab = pl.BlockSpec((None, T, 128), lambda b: (which, 0, 0))
    ins = [d, *tabs] + ([] if out_buf is None else [out_buf])
    specs = [pl.BlockSpec((None, None, T, 128), lambda b: (lax.div(b, 2), lax.rem(b, 2), 0, 0)), tab, tab, tab]
    return pl.pallas_call(
        body, grid=(6,), in_specs=specs + ([] if out_buf is None else [ANY]),
        out_specs=pl.BlockSpec((T, 128), lambda b: (0, 6 * which + b)),
        out_shape=jax.ShapeDtypeStruct((T, 3 * A_W), BF16), scratch_shapes=[pltpu.VMEM((T, 128), F32)],
        input_output_aliases={} if out_buf is None else {4: 0},
        compiler_params=_params(("arbitrary",)), name="rope_bwd",
    )(*ins)


def _head_mask(x, lane_lo):
    lane = lax.broadcasted_iota(jnp.int32, x.shape, 1)
    keep = (lane < HD) if lane_lo else (lane >= HD)
    return jnp.where(keep, x.astype(F32), 0.0).astype(BF16)


def _band_scalars():
    g, b = pl.program_id(0), pl.program_id(1)
    nbs = lax.shift_right_logical(jnp.int32(T // BLK), 2 * g)
    has_prev = jnp.where((b & (nbs - 1)) != 0, 1, 0)
    next_ok = jnp.where(((b + 1) & (nbs - 1)) != 0, 1, 0)
    return has_prev, next_ok


def _band_mask_q(has_prev):
    row = lax.broadcasted_iota(jnp.int32, (BLK, 2 * BLK), 0)
    col = lax.broadcasted_iota(jnp.int32, (BLK, 2 * BLK), 1)
    return ((col < BLK) & (col >= row) & (has_prev == 1)) | ((col >= BLK) & (col - BLK <= row))


def _band_mask_k(next_ok):
    row = lax.broadcasted_iota(jnp.int32, (2 * BLK, BLK), 0)
    col = lax.broadcasted_iota(jnp.int32, (2 * BLK, BLK), 1)
    return ((row < BLK) & (col <= row)) | ((row >= BLK) & (col >= row - BLK) & (next_ok == 1))


def _band_spec(base, step):
    nb = T // BLK
    at = {"cur": lambda b: b, "prev": lambda b: jnp.maximum(b - 1, 0), "next": lambda b: jnp.minimum(b + 1, nb - 1)}[step]
    return pl.BlockSpec((None, 2, BLK, 128), lambda g, b: (base + g, 0, at(b), 0))


def _band_fwd(qkv):
    nb = T // BLK

    def body(q_ref, kc_ref, kp_ref, vc_ref, vp_ref, o_ref, l_ref):
        has_prev, _ = _band_scalars()
        mask = _band_mask_q(has_prev)
        lane = lax.broadcasted_iota(jnp.int32, (BLK, 128), 1)
        for p in range(2):
            qp = q_ref[p]
            kcat = jnp.concatenate([kp_ref[p], kc_ref[p]], axis=0)
            vcat = jnp.concatenate([vp_ref[p], vc_ref[p]], axis=0)
            o_acc = jnp.zeros((BLK, 128), F32)
            lse = jnp.zeros((BLK, 128), F32)
            for e in range(2):
                s = _dot(_head_mask(qp, e == 0), kcat, NT)
                s = jnp.where(mask, s, NEG)
                m = jnp.max(s, axis=-1, keepdims=True)
                pr = jnp.exp(s - m)
                l = jnp.sum(pr, axis=-1, keepdims=True)
                o_acc = o_acc + _dot(pr.astype(BF16), _head_mask(vcat, e == 0), NN) / l
                lse = jnp.where((lane < HD) if e == 0 else (lane >= HD), m + jnp.log(l), lse)
            o_ref[p] = o_acc
            l_ref[p] = lse

    out = _band_spec(0, "cur")
    shp = jax.ShapeDtypeStruct((3, 2, T, 128), F32)
    return pl.pallas_call(
        body, grid=(3, nb),
        in_specs=[_band_spec(0, "cur"), _band_spec(3, "cur"), _band_spec(3, "prev"), _band_spec(6, "cur"), _band_spec(6, "prev")],
        out_specs=[out, out], out_shape=[shp, shp],
        compiler_params=_params(("parallel", "parallel")), name="band_fwd",
    )(qkv, qkv, qkv, qkv, qkv)


def _band_bwd(qkv, do, lse, dlt):
    nb = T // BLK

    def body(qc_ref, qn_ref, kc_ref, kp_ref, vc_ref, vp_ref, doc_ref, don_ref, lc_ref, ln_ref, dc_ref, dn_ref,
             dq_ref, dk_ref, dv_ref):
        has_prev, next_ok = _band_scalars()
        mask_q = _band_mask_q(has_prev)
        mask_k = _band_mask_k(next_ok)
        for p in range(2):
            qc, qn = qc_ref[p], qn_ref[p]
            doc, don = doc_ref[p], don_ref[p]
            kc, vc = kc_ref[p], vc_ref[p]
            kcat = jnp.concatenate([kp_ref[p], kc], axis=0)
            vcat = jnp.concatenate([vp_ref[p], vc], axis=0)
            qcat = jnp.concatenate([qc, qn], axis=0)
            docat = jnp.concatenate([doc, don], axis=0)
            dq = jnp.zeros((BLK, 128), F32)
            dk = jnp.zeros((BLK, 128), F32)
            dv = jnp.zeros((BLK, 128), F32)
            for e in range(2):
                lo = e == 0
                col = slice(HD * e, HD * e + 1)
                lse_c, lse_n = lc_ref[p, :, col], ln_ref[p, :, col]
                dl_c, dl_n = dc_ref[p, :, col], dn_ref[p, :, col]
                s = jnp.where(mask_q, _dot(_head_mask(qc, lo), kcat, NT), NEG)
                pr = jnp.exp(s - lse_c)
                dp = _dot(_head_mask(doc, lo), vcat, NT)
                ds = pr * (dp - dl_c)
                dq = dq + _dot(ds.astype(BF16), _head_mask(kcat, lo), NN)
                qm, dom = _head_mask(qcat, lo), _head_mask(docat, lo)
                s2 = jnp.where(mask_k, _dot(qm, kc, NT), NEG)
                p2 = jnp.exp(s2 - jnp.concatenate([lse_c, lse_n], axis=0))
                dv = dv + _dot(p2.astype(BF16), dom, TN)
                dp2 = _dot(dom, vc, NT)
                ds2 = p2 * (dp2 - jnp.concatenate([dl_c, dl_n], axis=0))
                dk = dk + _dot(ds2.astype(BF16), qm, TN)
            dq_ref[p] = dq
            dk_ref[p] = dk
            dv_ref[p] = dv

    cur, nxt = _band_spec(0, "cur"), _band_spec(0, "next")
    shp = jax.ShapeDtypeStruct((3, 2, T, 128), F32)
    return pl.pallas_call(
        body, grid=(3, nb),
        in_specs=[cur, nxt, _band_spec(3, "cur"), _band_spec(3, "prev"), _band_spec(6, "cur"), _band_spec(6, "prev"),
                  cur, nxt, cur, nxt, cur, nxt],
        out_specs=[cur, cur, cur], out_shape=[shp, shp, shp],
        compiler_params=_params(("parallel", "parallel")), name="band_bwd",
    )(qkv, qkv, qkv, qkv, qkv, qkv, do, do, lse, lse, dlt, dlt)


def _split3(x):
    hi = x.astype(BF16)
    r = x - hi.astype(F32)
    mid = r.astype(BF16)
    lo = (r - mid.astype(F32)).astype(BF16)
    return hi, mid, lo


def _dot3(x, m, dims=NN):
    hi, mid, lo = _split3(x)
    return _dot(hi, m, dims) + _dot(mid, m, dims) + _dot(lo, m, dims)


def _combine_weights(lses):
    l0, l1, l2 = lses
    m = jnp.maximum(jnp.maximum(l0, l1), l2)
    e = [jnp.exp(l0 - m), jnp.exp(l1 - m), jnp.exp(l2 - m)]
    inv = 1.0 / (e[0] + e[1] + e[2])
    return [ei * inv for ei in e]


CR = 256


def _combine_fwd(o, lse):
    def body(o_ref, l_ref, att_ref, o3_ref, l3_ref):
        for g, r in enumerate(DIL):
            for p in range(2):
                for tok, prm in _row_chunks(r):
                    o3_ref[g, p, tok, :] = o_ref[g, p, prm, :]
                    l3_ref[g, p, tok, :] = l_ref[g, p, prm, :]
        for i in range(T // CR):
            rows = slice(i * CR, (i + 1) * CR)
            for p in range(2):
                alpha = _combine_weights([l3_ref[g, p, rows, :] for g in range(3)])
                for g in range(3):
                    att_ref[rows, g * GW + p * 128: g * GW + (p + 1) * 128] = (o3_ref[g, p, rows, :] * alpha[g]).astype(BF16)

    shp = jax.ShapeDtypeStruct((3, 2, T, 128), F32)
    return pl.pallas_call(
        body, out_shape=[jax.ShapeDtypeStruct((T, A_W), BF16), shp, shp],
        compiler_params=_params(vmem=VMEM_BIG), name="combine_fwd",
    )(o, lse)


def _combine_bwd(datt, o3, l3, headsum):
    def body(d_ref, o_ref, l_ref, hs_ref, do_ref, dl_ref, tdo_ref, tdl_ref):
        hs = hs_ref[...]
        for p in range(2):
            for i in range(T // CR):
                rows = slice(i * CR, (i + 1) * CR)
                alpha = _combine_weights([l_ref[g, p, rows, :] for g in range(3)])
                total = jnp.zeros((CR, 128), F32)
                for g in range(3):
                    dg = d_ref[rows, g * GW + p * 128: g * GW + (p + 1) * 128]
                    tdo_ref[g, rows, :] = dg * alpha[g]
                    total = total + alpha[g] * _dot3(dg * o_ref[g, p, rows, :], hs)
                for g in range(3):
                    tdl_ref[g, rows, :] = alpha[g] * total
            for g, r in enumerate(DIL):
                for tok, prm in _row_chunks(r):
                    do_ref[g, p, prm, :] = tdo_ref[g, tok, :].astype(BF16)
                    dl_ref[g, p, prm, :] = tdl_ref[g, tok, :]

    return pl.pallas_call(
        body, out_shape=[jax.ShapeDtypeStruct((3, 2, T, 128), BF16), jax.ShapeDtypeStruct((3, 2, T, 128), F32)],
        scratch_shapes=[pltpu.VMEM((3, T, 128), F32), pltpu.VMEM((3, T, 128), F32)],
        compiler_params=_params(vmem=VMEM_BIG), name="combine_bwd",
    )(datt, o3, l3, headsum)


def _fox_scores(qm, k_ref, cq, ck_ref, e, i, n):
    s = _dot(qm, k_ref[0:n, :], NT) + (cq - ck_ref[0, e:e + 1, 0:n])
    row = lax.broadcasted_iota(jnp.int32, (FQ, n), 0)
    col = lax.broadcasted_iota(jnp.int32, (FQ, n), 1)
    s = jnp.where(col <= row + i * FQ, s, NEG)
    m = jnp.max(s, axis=-1, keepdims=True)
    pr = jnp.exp(s - m)
    return pr, jnp.sum(pr, axis=-1, keepdims=True)


def _fox_fwd(q, kv, c_col, c_row):
    def body(q_ref, k_ref, v_ref, cc_ref, cr_ref, o_ref, vm_ref):
        for e in range(2):
            vm_ref[e] = _head_mask(v_ref[...], e == 0)
        for i in range(T // FQ):
            n = (i + 1) * FQ
            rows = slice(i * FQ, n)
            acc = jnp.zeros((FQ, 128), F32)
            for e in range(2):
                qm = _head_mask(q_ref[rows, :], e == 0)
                pr, l = _fox_scores(qm, k_ref, cc_ref[0, rows, e:e + 1], cr_ref, e, i, n)
                acc = acc + _dot(pr.astype(BF16), vm_ref[e, 0:n, :], NN) / l
            o_ref[rows, :] = acc.astype(BF16)

    pair = pl.BlockSpec((T, 128), lambda p: (0, p))
    return pl.pallas_call(
        body, grid=(D // 128,),
        in_specs=[pair, pair, pl.BlockSpec((T, 128), lambda p: (0, D // 128 + p)),
                  pl.BlockSpec((1, T, 2), lambda p: (p, 0, 0)), pl.BlockSpec((1, 2, T), lambda p: (p, 0, 0))],
        out_specs=pair, out_shape=jax.ShapeDtypeStruct((T, D), BF16),
        scratch_shapes=[pltpu.VMEM((2, T, 128), BF16)],
        compiler_params=_params(("parallel",), VMEM_BIG), name="fox_fwd",
    )(q, kv, kv, c_col, c_row)


def _fox_bwd(q, kv, do, c_col, c_row, init):
    def body(q_ref, k_ref, v_ref, do_ref, cc_ref, cr_ref, ik_ref, iv_ref, iq_ref, ic_ref,
             dq_ref, dk_ref, dv_ref, dcq_ref, dck_ref, km_ref):
        dk_ref[...] = ik_ref[...]
        dv_ref[...] = iv_ref[...]
        dcq_ref[...] = iq_ref[...]
        dck_ref[...] = ic_ref[...]
        for e in range(2):
            km_ref[e] = _head_mask(k_ref[...], e == 0)
        for i in range(T // FQ):
            n = (i + 1) * FQ
            rows = slice(i * FQ, n)
            dq = jnp.zeros((FQ, 128), F32)
            for e in range(2):
                qm = _head_mask(q_ref[rows, :], e == 0)
                dom = _head_mask(do_ref[rows, :], e == 0)
                pr, l = _fox_scores(qm, k_ref, cc_ref[0, rows, e:e + 1], cr_ref, e, i, n)
                pr = pr / l
                dp = _dot(dom, v_ref[0:n, :], NT)
                ds = pr * (dp - jnp.sum(pr * dp, axis=-1, keepdims=True))
                dsb = ds.astype(BF16)
                dq = dq + _dot(dsb, km_ref[e, 0:n, :], NN)
                dk_ref[0:n, :] += _dot(dsb, qm, TN)
                dv_ref[0:n, :] += _dot(pr.astype(BF16), dom, TN)
                dcq_ref[0, rows, e:e + 1] += jnp.sum(ds, axis=-1, keepdims=True)
                dck_ref[0, e:e + 1, 0:n] += jnp.sum(ds, axis=0, keepdims=True)
            dq_ref[rows, :] = (dq * HD ** -0.5).astype(BF16)

    pair = pl.BlockSpec((T, 128), lambda p: (0, p))
    cq = pl.BlockSpec((1, T, 128), lambda p: (p, 0, 0))
    ck = pl.BlockSpec((1, 8, T), lambda p: (p, 0, 0))
    return pl.pallas_call(
        body, grid=(D // 128,),
        in_specs=[pair, pair, pl.BlockSpec((T, 128), lambda p: (0, D // 128 + p)), pair,
                  pl.BlockSpec((1, T, 2), lambda p: (p, 0, 0)), pl.BlockSpec((1, 2, T), lambda p: (p, 0, 0)),
                  pair, pair, cq, ck],
        out_specs=[pair, pair, pair, cq, ck],
        out_shape=[jax.ShapeDtypeStruct((T, D), BF16), jax.ShapeDtypeStruct((T, D), F32), jax.ShapeDtypeStruct((T, D), F32),
                   jax.ShapeDtypeStruct((D // 128, T, 128), F32), jax.ShapeDtypeStruct((D // 128, 8, T), F32)],
        scratch_shapes=[pltpu.VMEM((2, T, 128), BF16)],
        compiler_params=_params(("parallel",), VMEM_BIG), name="fox_bwd",
    )(q, kv, kv, do, c_col, c_row, *init)


def _tri(lower):
    r = lax.broadcasted_iota(jnp.int32, (BLK, BLK), 0)
    c = lax.broadcasted_iota(jnp.int32, (BLK, BLK), 1)
    return jnp.where((c <= r) if lower else (c >= r), 1.0, 0.0).astype(BF16)


def _gates_fwd(z, b):
    def body(z_ref, b_ref, c_ref):
        tri = _tri(True)
        carry = jnp.zeros((1, 128), F32)
        for i in range(T // BLK):
            rows = slice(i * BLK, (i + 1) * BLK)
            x = z_ref[rows, :] + b_ref[...]
            logf = jnp.minimum(x, 0.0) - jnp.log(1.0 + jnp.exp(-jnp.abs(x)))
            hi, mid, lo = _split3(logf)
            y = _dot(tri, hi, NN) + _dot(tri, mid, NN) + _dot(tri, lo, NN) + carry
            c_ref[rows, :] = y
            carry = y[BLK - 1:BLK, :]

    return pl.pallas_call(body, out_shape=jax.ShapeDtypeStruct((T, 128), F32), name="gates_fwd")(z, b)


def _gates_bwd(dc, z, b):
    def body(dc_ref, z_ref, b_ref, dz_ref, db_ref):
        tri = _tri(False)
        carry = jnp.zeros((1, 128), F32)
        db = jnp.zeros((1, 128), F32)
        for i in reversed(range(T // BLK)):
            rows = slice(i * BLK, (i + 1) * BLK)
            hi, mid, lo = _split3(dc_ref[rows, :])
            dlogf = _dot(tri, hi, NN) + _dot(tri, mid, NN) + _dot(tri, lo, NN) + carry
            carry = dlogf[0:1, :]
            x = z_ref[rows, :] + b_ref[...]
            dz = dlogf / (1.0 + jnp.exp(x))
            dz_ref[rows, :] = dz.astype(BF16)
            db = db + jnp.sum(dz, axis=0, keepdims=True)
        db_ref[...] = db

    return pl.pallas_call(
        body, out_shape=[jax.ShapeDtypeStruct((T, 128), BF16), jax.ShapeDtypeStruct((1, 128), F32)], name="gates_bwd",
    )(dc, z, b)


def _conv_pair(a_refs, cw_refs, cb_refs):
    row = lax.broadcasted_iota(jnp.int32, (T, CT), 0)
    outs = []
    for a_ref, cw_ref, cb_ref in zip(a_refs, cw_refs, cb_refs):
        z = a_ref[...]
        z1 = jnp.where(row >= 1, pltpu.roll(z, 1, 0), 0.0)
        z2 = jnp.where(row >= 2, pltpu.roll(z, 2, 0), 0.0)
        y = cw_ref[2:3, :] * z + cw_ref[1:2, :] * z1 + cw_ref[0:1, :] * z2 + cb_ref[...]
        outs.append((y, z, z1, z2))
    return outs


_GELU_K = math.sqrt(2.0 / math.pi)
N_CT = D_FF // CT


def _conv_specs():
    def at(rows, off):
        return pl.BlockSpec((rows, CT), lambda j: (0, j + off))
    return [at(T, 0), at(T, N_CT), at(3, 0), at(3, N_CT), at(1, 0), at(1, N_CT)]


def _convgate_fwd(a, cw, cb):
    def body(ag_ref, av_ref, wg_ref, wv_ref, bg_ref, bv_ref, u_ref):
        (g, _, _, _), (v, _, _, _) = _conv_pair((ag_ref, av_ref), (wg_ref, wv_ref), (bg_ref, bv_ref))
        th = jnp.tanh(_GELU_K * (g + 0.044715 * g * g * g))
        u_ref[...] = (0.5 * g * (1.0 + th) * v).astype(BF16)

    return pl.pallas_call(
        body, grid=(N_CT,), in_specs=_conv_specs(),
        out_specs=pl.BlockSpec((T, CT), lambda j: (0, j)), out_shape=jax.ShapeDtypeStruct((T, D_FF), BF16),
        compiler_params=_params(("parallel",), VMEM_BIG), name="convgate_fwd",
    )(a, a, cw, cw, cb, cb)


def _convgate_bwd(a, du, cw, cb):
    def body(ag_ref, av_ref, wg_ref, wv_ref, bg_ref, bv_ref, du_ref, da_ref, dcw_ref, dcb_ref):
        (g, gz, gz1, gz2), (v, vz, vz1, vz2) = _conv_pair((ag_ref, av_ref), (wg_ref, wv_ref), (bg_ref, bv_ref))
        du = du_ref[...].astype(F32)
        th = jnp.tanh(_GELU_K * (g + 0.044715 * g * g * g))
        gelu = 0.5 * g * (1.0 + th)
        dgelu = 0.5 * (1.0 + th) + 0.5 * g * (1.0 - th * th) * _GELU_K * (1.0 + 3 * 0.044715 * g * g)
        row = lax.broadcasted_iota(jnp.int32, (T, CT), 0)
        for h, (d, z, z1, z2, w_ref) in enumerate(((du * v * dgelu, gz, gz1, gz2, wg_ref), (du * gelu, vz, vz1, vz2, wv_ref))):
            d1 = jnp.where(row < T - 1, pltpu.roll(d, T - 1, 0), 0.0)
            d2 = jnp.where(row < T - 2, pltpu.roll(d, T - 2, 0), 0.0)
            da_ref[h] = (w_ref[2:3, :] * d + w_ref[1:2, :] * d1 + w_ref[0:1, :] * d2).astype(BF16)
            dcw_ref[h, 0:1, :] = jnp.sum(d * z2, axis=0, keepdims=True)
            dcw_ref[h, 1:2, :] = jnp.sum(d * z1, axis=0, keepdims=True)
            dcw_ref[h, 2:3, :] = jnp.sum(d * z, axis=0, keepdims=True)
            dcb_ref[h] = jnp.sum(d, axis=0, keepdims=True)

    def both(rows):
        return pl.BlockSpec((2, rows, CT), lambda j: (0, 0, j))

    return pl.pallas_call(
        body, grid=(N_CT,),
        in_specs=_conv_specs() + [pl.BlockSpec((T, CT), lambda j: (0, j))],
        out_specs=[both(T), both(3), both(1)],
        out_shape=[jax.ShapeDtypeStruct((2, T, D_FF), BF16), jax.ShapeDtypeStruct((2, 3, D_FF), F32),
                   jax.ShapeDtypeStruct((2, 1, D_FF), F32)],
        compiler_params=_params(("parallel",), VMEM_BIG), name="convgate_bwd",
    )(a, a, cw, cw, cb, cb, du)


def _halves_a(tm, tn, tk):
    per = D_FF // tk
    return lambda i, j, k: (lax.div(k, per), i, lax.rem(k, per))


def _halves_b(tm, tn, tk):
    per = D_FF // tn
    return lambda i, j, k: (lax.div(j, per), k, lax.rem(j, per))


def _adamw(w, m, v, g, *, name):
    r, c = w.shape
    tr = r
    if r * c > 256 * 1024:
        for cand in range(8, r, 8):
            if r % cand == 0 and cand * c <= 256 * 1024:
                tr = cand

    def body(w_ref, m_ref, v_ref, g_ref, d_ref, nm_ref, nv_ref):
        gv = g_ref[...]
        mn = ADAM_B1 * m_ref[...] + (1.0 - ADAM_B1) * gv
        vn = ADAM_B2 * v_ref[...] + (1.0 - ADAM_B2) * (gv * gv)
        m_hat = mn / (1.0 - ADAM_B1 ** ADAM_STEP)
        v_hat = vn / (1.0 - ADAM_B2 ** ADAM_STEP)
        d_ref[...] = -ADAM_LR * (m_hat / (jnp.sqrt(v_hat) + ADAM_EPS) + ADAM_WD * w_ref[...])
        nm_ref[...] = mn
        nv_ref[...] = vn

    blk = pl.BlockSpec((tr, c), lambda i: (i, 0))
    shp = jax.ShapeDtypeStruct((r, c), F32)
    return pl.pallas_call(
        body, grid=(r // tr,), in_specs=[blk] * 4, out_specs=[blk] * 3, out_shape=[shp] * 3,
        compiler_params=_params(("parallel",)), name=name,
    )(w, m, v, g)


def _place():
    x, y, c = lax.axis_index("x"), lax.axis_index("y"), lax.axis_index("c")
    chips = [(1 - x, y), (x, 1 - y), (1 - x, 1 - y)]
    return x, y, c, chips


def _window(ref, kind, s, half=None):
    lead = () if half is None else (half,)
    b, c = ref.shape[-2], ref.shape[-1]
    if kind == "col":
        return ref.at[lead + (slice(None), slice(None), pl.ds(s * (c // N_CHIPS), c // N_CHIPS))]
    if kind == "row":
        return ref.at[lead + (slice(None), pl.ds(s * (b // N_CHIPS), b // N_CHIPS), slice(None))]
    return ref.at[lead + (s,)]


def _window_shape(shape3, kind):
    a, b, c = shape3
    return {"col": (a, b, c // N_CHIPS), "row": (a, b // N_CHIPS, c), "slab": (b, c)}[kind]


def _allgather(tensors, kinds, *, name):
    n = len(tensors)

    def body(*refs):
        bufs = refs[n:2 * n]
        send, recv = refs[2 * n:]
        x, y, c, chips = _place()
        me = 2 * x + y
        sib = (x, y, 1 - c)

        def rcopy(i, k, win, to):
            return pltpu.make_async_remote_copy(src_ref=win, dst_ref=win, send_sem=send.at[i * 6 + k], recv_sem=recv.at[i * 6 + k],
                                                device_id=to, device_id_type=MESH)

        started = []
        for i in range(n):
            for k, (px, py) in enumerate(chips):
                cp = rcopy(i, k, _window(bufs[i], kinds[i], me, c), (px, py, c))
                cp.start()
                started.append(cp)
        for i in range(n):
            for k, (px, py) in enumerate(chips):
                landed = _window(bufs[i], kinds[i], 2 * px + py, c)
                rcopy(i, k, landed, (px, py, c)).wait_recv()
                fw = rcopy(i, 3 + k, landed, sib)
                fw.start()
                started.append(fw)
        for i in range(n):
            for k, (px, py) in enumerate(chips):
                rcopy(i, 3 + k, _window(bufs[i], kinds[i], 2 * px + py, 1 - c), sib).wait_recv()
        for cp in started:
            cp.wait_send()

    return pl.pallas_call(
        body, in_specs=[ANY] * n, out_specs=[ANY] * n,
        out_shape=[jax.ShapeDtypeStruct(t.shape, t.dtype) for t in tensors],
        scratch_shapes=[pltpu.SemaphoreType.DMA((6 * n,)), pltpu.SemaphoreType.DMA((6 * n,))],
        input_output_aliases={i: i for i in range(n)},
        name=name,
    )(*tensors)


def _rows_tile(rows, cols, sub):
    best = None
    for t in range(sub, rows + 1, sub):
        if rows % t == 0 and t * cols <= 512 * 1024:
            best = t
    return rows if best is None else best


def _sequencer(name, cid, n_sems, peers_of, body):
    @pl.kernel(mesh=plsc.ScalarSubcoreMesh(axis_name="seq", num_cores=1), name=name,
               scratch_types=(pltpu.SemaphoreType.DMA((n_sems,)), pltpu.SemaphoreType.DMA((n_sems,))),
               compiler_params=pltpu.CompilerParams(collective_id=cid))
    def launch(send, recv):
        x, y, c, chips = _place()
        peers = peers_of(x, y, c, chips)
        barrier = pltpu.get_barrier_semaphore()
        for peer in peers:
            pl.semaphore_signal(barrier, inc=1, device_id=peer, device_id_type=MESH)
        pl.semaphore_wait(barrier, len(peers))
        body(send, recv)

    launch()


def _half_of_full(ref, kind, h):
    if kind == "col":
        b = ref.shape[0]
        return ref.at[pl.ds(h * (b // 2), b // 2), :]
    if kind == "row":
        c = ref.shape[1]
        return ref.at[:, pl.ds(h * (c // 2), c // 2)]
    b = ref.shape[1]
    return ref.at[:, pl.ds(h * (b // 2), b // 2), :]


def _half_shape(full, kind):
    if kind == "col":
        return (full[0] // 2, full[1])
    if kind == "row":
        return (full[0], full[1] // 2)
    return (full[0], full[1] // 2, full[2])


def _win_of_half(ref, kind, s):
    if kind == "col":
        c = ref.shape[1]
        return ref.at[:, pl.ds(s * (c // N_CHIPS), c // N_CHIPS)]
    if kind == "row":
        b = ref.shape[0]
        return ref.at[pl.ds(s * (b // N_CHIPS), b // N_CHIPS), :]
    return ref.at[s]


def _win_shape(half, kind):
    if kind == "col":
        return (half[0], half[1] // N_CHIPS)
    if kind == "row":
        return (half[0] // N_CHIPS, half[1])
    return half[1:]


def _seq_swap(parts, kinds, *, name):
    n = len(parts)
    srcs = [jax.new_ref(p, memory_space=pltpu.MemorySpace.HBM) for p in parts]
    outs = [jax.empty_ref(jax.ShapeDtypeStruct(_half_shape(p.shape, k), p.dtype), memory_space=pltpu.MemorySpace.HBM)
            for p, k in zip(parts, kinds)]

    def body(send, recv):
        x, y, c, _ = _place()
        cps = []
        for i in range(n):
            cp = pltpu.make_async_remote_copy(src_ref=_half_of_full(srcs[i], kinds[i], 1 - c), dst_ref=outs[i], send_sem=send.at[i],
                                              recv_sem=recv.at[i], device_id=(x, y, 1 - c), device_id_type=MESH)
            cp.start()
            cps.append(cp)
        for cp in cps:
            cp.wait()

    _sequencer(name, 2, n, lambda x, y, c, chips: [(x, y, 1 - c)], body)
    return [o[...] for o in outs]


def _seq_scatter(halves, kinds, *, name):
    n = len(halves)
    srcs = [jax.new_ref(h, memory_space=pltpu.MemorySpace.HBM) for h in halves]
    outs = [jax.empty_ref(jax.ShapeDtypeStruct((3,) + _win_shape(h.shape, k), h.dtype), memory_space=pltpu.MemorySpace.HBM)
            for h, k in zip(halves, kinds)]

    def body(send, recv):
        x, y, c, chips = _place()
        cps = []
        for i in range(n):
            for k, (px, py) in enumerate(chips):
                cp = pltpu.make_async_remote_copy(src_ref=_win_of_half(srcs[i], kinds[i], 2 * px + py), dst_ref=outs[i].at[k],
                                                  send_sem=send.at[3 * i + k], recv_sem=recv.at[3 * i + k],
                                                  device_id=(px, py, c), device_id_type=MESH)
                cp.start()
                cps.append(cp)
        for cp in cps:
            cp.wait()

    _sequencer(name, 3, 3 * n, lambda x, y, c, chips: [(px, py, c) for px, py in chips], body)
    return [o[...] for o in outs]


def _add_half(g, p, kind, where, after, *, name):
    if kind == "slab":
        s, b2, c = p.shape
        tr = _rows_tile(b2, c, 16)
        nr = b2 // tr
        grid = (s, nr)
        g_spec = pl.BlockSpec((None, tr, c), lambda i, r, w: (i, w[1] * nr + r, 0))
        p_spec = pl.BlockSpec((None, tr, c), lambda i, r, w: (i, r, 0))
    elif kind == "col":
        b2, c = p.shape
        tr = _rows_tile(b2, c, 16)
        nr = b2 // tr
        grid = (1, nr)
        g_spec = pl.BlockSpec((tr, c), lambda i, r, w: (w[1] * nr + r, 0))
        p_spec = pl.BlockSpec((tr, c), lambda i, r, w: (r, 0))
    else:
        b, c2 = p.shape
        tr = _rows_tile(b, c2, 16)
        grid = (1, b // tr)
        g_spec = pl.BlockSpec((tr, c2), lambda i, r, w: (r, w[1]))
        p_spec = pl.BlockSpec((tr, c2), lambda i, r, w: (r, 0))

    def body(w_ref, g_ref, p_ref, *rest):
        o_ref = rest[-1]
        o_ref[...] = (g_ref[...].astype(F32) + p_ref[...].astype(F32)).astype(o_ref.dtype)

    extra = [] if after is None else [after]
    return pl.pallas_call(
        body,
        grid_spec=pltpu.PrefetchScalarGridSpec(num_scalar_prefetch=1, grid=grid, in_specs=[g_spec, p_spec] + [ANY] * len(extra),
                                               out_specs=p_spec),
        out_shape=jax.ShapeDtypeStruct(p.shape, g.dtype),
        compiler_params=_params(("parallel", "parallel")), name=name,
    )(where, g, p, *extra)


def _sum_chips(r, h, kind, where, layer, layers, out_buf, after, *, name):
    _, br, cr = r.shape
    tr = _rows_tile(br, cr, 16)
    nr = br // tr
    if kind == "col":
        h_spec = pl.BlockSpec((tr, cr), lambda j, w: (j, w[0]))
        o_shape, o_spec = (layers, 2 * br, cr), pl.BlockSpec((None, tr, cr), lambda j, w: (layer, w[1] * nr + j, 0))
    elif kind == "row":
        h_spec = pl.BlockSpec((tr, cr), lambda j, w: (w[0] * nr + j, 0))
        o_shape, o_spec = (layers, br, 2 * cr), pl.BlockSpec((None, tr, cr), lambda j, w: (layer, j, w[1]))
    else:
        h_spec = pl.BlockSpec((None, tr, cr), lambda j, w: (w[0], j, 0))
        o_shape, o_spec = (layers, 2 * br, cr), pl.BlockSpec((None, tr, cr), lambda j, w: (layer, w[1] * nr + j, 0))

    def body(w_ref, h_ref, r0_ref, r1_ref, r2_ref, *rest):
        o_ref, t_ref = rest[-2], rest[-1]
        o_ref[...] = ((h_ref[...].astype(F32) + r0_ref[...].astype(F32)) + r1_ref[...].astype(F32)) + r2_ref[...].astype(F32)
        t_ref[...] = jnp.zeros_like(t_ref)

    def slot(k):
        return pl.BlockSpec((None, tr, cr), lambda j, w: (k, j, 0))

    ins, specs, alias = [h, r, r, r], [h_spec, slot(0), slot(1), slot(2)], {}
    if after is not None:
        ins.append(after)
        specs.append(ANY)
    if out_buf is not None:
        alias = {1 + len(ins): 0}
        ins.append(out_buf)
        specs.append(ANY)
    return pl.pallas_call(
        body,
        grid_spec=pltpu.PrefetchScalarGridSpec(num_scalar_prefetch=1, grid=(nr,), in_specs=specs,
                                               out_specs=[o_spec, pl.BlockSpec((8, 128), lambda j, w: (0, 0))]),
        out_shape=[jax.ShapeDtypeStruct(o_shape, F32), jax.ShapeDtypeStruct((8, 128), F32)], input_output_aliases=alias,
        compiler_params=_params(("arbitrary",)), name=name,
    )(where, *ins)


def _join_halves(tensors, kinds, *, name):
    n = len(tensors)

    def mine(ref, kind, h):
        if kind == "row":
            c = ref.shape[2]
            return ref.at[:, :, pl.ds(h * (c // 2), c // 2)]
        b = ref.shape[1]
        return ref.at[:, pl.ds(h * (b // 2), b // 2), :]

    def body(*refs):
        bufs = refs[n:2 * n]
        send, recv = refs[2 * n:]
        x, y, c, _ = _place()
        cps = []
        for i in range(n):
            part = mine(bufs[i], kinds[i], c)
            cp = pltpu.make_async_remote_copy(src_ref=part, dst_ref=part, send_sem=send.at[i],
                                              recv_sem=recv.at[i], device_id=(x, y, 1 - c), device_id_type=MESH)
            cp.start()
            cps.append(cp)
        for i in range(n):
            other = mine(bufs[i], kinds[i], 1 - c)
            pltpu.make_async_remote_copy(src_ref=other, dst_ref=other, send_sem=send.at[i],
                                         recv_sem=recv.at[i], device_id=(x, y, 1 - c), device_id_type=MESH).wait_recv()
        for cp in cps:
            cp.wait_send()

    return pl.pallas_call(
        body, in_specs=[ANY] * n, out_specs=[ANY] * n,
        out_shape=[jax.ShapeDtypeStruct(t.shape, t.dtype) for t in tensors],
        scratch_shapes=[pltpu.SemaphoreType.DMA((n,)), pltpu.SemaphoreType.DMA((n,))],
        input_output_aliases={i: i for i in range(n)},
        name=name,
    )(*tensors)


def _win(ref, kind, s, h=None):
    if kind == "col":
        b, c = ref.shape
        cols = pl.ds(s * (c // N_CHIPS), c // N_CHIPS)
        return ref.at[:, cols] if h is None else ref.at[pl.ds(h * (b // 2), b // 2), cols]
    if kind == "row":
        b, c = ref.shape
        rows = pl.ds(s * (b // N_CHIPS), b // N_CHIPS)
        return ref.at[rows, :] if h is None else ref.at[rows, pl.ds(h * (c // 2), c // 2)]
    b = ref.shape[1]
    return ref.at[s] if h is None else ref.at[s, pl.ds(h * (b // 2), b // 2)]


def _half(ref, kind, h):
    b, c = ref.shape
    if kind == "row":
        return ref.at[:, pl.ds(h * (c // 2), c // 2)]
    return ref.at[pl.ds(h * (b // 2), b // 2), :]


def _full_shape(shard_shape, kind):
    b, c = shard_shape
    return {"col": (b, N_CHIPS * c), "row": (N_CHIPS * b, c), "slab": (N_CHIPS, b, c)}[kind]


def _gather_body(srcs, outs, kinds, send, recv):
    x, y, c, chips = _place()
    me = 2 * x + y
    sib = (x, y, 1 - c)

    def rcopy(i, k, src, dst, to):
        return pltpu.make_async_remote_copy(src_ref=src, dst_ref=dst, send_sem=send.at[7 * i + k], recv_sem=recv.at[7 * i + k],
                                            device_id=to, device_id_type=MESH)

    started = []
    for i, (src, out, kind) in enumerate(zip(srcs, outs, kinds)):
        own = rcopy(i, 6, src, _win(out, kind, me), sib)
        own.start()
        started.append(own)
        for k, (px, py) in enumerate(chips):
            cp = rcopy(i, k, _half(src, kind, c), _win(out, kind, me, c), (px, py, c))
            cp.start()
            started.append(cp)
    for i, (out, kind) in enumerate(zip(outs, kinds)):
        for k, (px, py) in enumerate(chips):
            landed = _win(out, kind, 2 * px + py, c)
            rcopy(i, k, landed, landed, (px, py, c)).wait_recv()
            fw = rcopy(i, 3 + k, landed, landed, sib)
            fw.start()
            started.append(fw)
    for i, (src, out, kind) in enumerate(zip(srcs, outs, kinds)):
        for k, (px, py) in enumerate(chips):
            other = _win(out, kind, 2 * px + py, 1 - c)
            rcopy(i, 3 + k, other, other, sib).wait_recv()
        rcopy(i, 6, src, _win(out, kind, me), sib).wait_recv()
    for cp in started:
        cp.wait_send()


def _seq_gather(shards, kinds, *, name, cid):
    n = len(shards)
    srcs = [jax.new_ref(s, memory_space=pltpu.MemorySpace.HBM) for s in shards]
    outs = [jax.empty_ref(jax.ShapeDtypeStruct(_full_shape(s.shape, k), s.dtype), memory_space=pltpu.MemorySpace.HBM)
            for s, k in zip(shards, kinds)]

    @pl.kernel(mesh=plsc.ScalarSubcoreMesh(axis_name="seq", num_cores=1), name=name,
               scratch_types=(pltpu.SemaphoreType.DMA((7 * n,)), pltpu.SemaphoreType.DMA((7 * n,))),
               compiler_params=pltpu.CompilerParams(collective_id=cid))
    def launch(send, recv):
        x, y, c, chips = _place()
        barrier = pltpu.get_barrier_semaphore()
        for px, py in chips:
            pl.semaphore_signal(barrier, inc=1, device_id=(px, py, c), device_id_type=MESH)
        pl.semaphore_signal(barrier, inc=1, device_id=(x, y, 1 - c), device_id_type=MESH)
        pl.semaphore_wait(barrier, 4)
        _gather_body(srcs, outs, kinds, send, recv)

    launch()
    return [o[...] for o in outs]


KIND = dict(w_qkv_a="slab", w_o_a="col", w_q_b="row", w_o_b="row", w_kvf="slab", w_up="col", w_down="row", small="slab")
LAYERS = dict(w_qkv_a=N_A, w_o_a=N_A, w_q_b=DEPTH - N_A, w_o_b=DEPTH - N_A, w_kvf=1, w_up=DEPTH, w_down=DEPTH, small=1)
SMALL_W = 1792
SMALL_ROWS = 8


class _Reducer:
    def __init__(self, where):
        self.where = where
        self.acc = {nm: None for nm in KIND}
        self.pending = None

    def __call__(self, group, tag):
        names, layers, parts = zip(*group)
        kinds = [KIND[nm] for nm in names]
        summed = self._sum_pending(after=parts[-1])
        sib = _seq_swap(list(parts), kinds, name="reduce_swap_" + tag)
        halves = []
        for g, p, k, nm in zip(parts, sib, kinds, names):
            halves.append(_add_half(g, p, k, self.where, halves[-1] if halves else None, name="reduce_add_" + nm))
        landed = _seq_scatter(halves, kinds, name="reduce_scatter_" + tag)
        self.pending = (names, layers, landed, halves, kinds)
        return [halves[-1], summed]

    def _sum_pending(self, after):
        if self.pending is None:
            return None
        for nm, l, r, h, k in zip(*self.pending):
            self.acc[nm], after = _sum_chips(r, h, k, self.where, l, LAYERS[nm], self.acc[nm], after, name="reduce_sum_" + nm)
        self.pending = None
        return after

    def finish(self):
        self._sum_pending(after=None)
        names = list(KIND)
        joined = _join_halves([self.acc[nm] for nm in names], [KIND[nm] for nm in names], name="reduce_pair_join")
        return dict(zip(names, joined))


def _headsum_matrix():
    r = lax.broadcasted_iota(jnp.int32, (128, 128), 0) // HD
    c = lax.broadcasted_iota(jnp.int32, (128, 128), 1) // HD
    return jnp.where(r == c, 1.0, 0.0).astype(BF16)


def kernel(x, norm_gains, w_qkv_a, w_o_a, w_q_b, w_o_b, kv_norm, w_kvf, b_f, w_up, conv_w, conv_b, w_down, loss_target, m_norm_gains, m_w_qkv_a, m_w_o_a, m_w_q_b, m_w_o_b, m_kv_norm, m_w_kvf, m_b_f, m_w_up, m_conv_w, m_conv_b, m_w_down, v_norm_gains, v_w_qkv_a, v_w_o_a, v_w_q_b, v_w_o_b, v_kv_norm, v_w_kvf, v_b_f, v_w_up, v_conv_w, v_conv_b, v_w_down):
    xi, yi, ci = lax.axis_index("x"), lax.axis_index("y"), lax.axis_index("c")
    chip = 2 * xi + yi
    where = jnp.stack([chip, ci]).astype(jnp.int32)
    ws = dict(norm_gains=norm_gains, w_qkv_a=w_qkv_a, w_o_a=w_o_a, w_q_b=w_q_b, w_o_b=w_o_b, kv_norm=kv_norm, w_kvf=w_kvf,
              b_f=b_f, w_up=w_up, conv_w=conv_w, conv_b=conv_b, w_down=w_down)
    ms = dict(norm_gains=m_norm_gains, w_qkv_a=m_w_qkv_a, w_o_a=m_w_o_a, w_q_b=m_w_q_b, w_o_b=m_w_o_b, kv_norm=m_kv_norm,
              w_kvf=m_w_kvf, b_f=m_b_f, w_up=m_w_up, conv_w=m_conv_w, conv_b=m_conv_b, w_down=m_w_down)
    vs = dict(norm_gains=v_norm_gains, w_qkv_a=v_w_qkv_a, w_o_a=v_w_o_a, w_q_b=v_w_q_b, w_o_b=v_w_o_b, kv_norm=v_kv_norm,
              w_kvf=v_w_kvf, b_f=v_b_f, w_up=v_w_up, conv_w=v_conv_w, conv_b=v_conv_b, w_down=v_w_down)

    small = jnp.concatenate([
        jnp.pad(norm_gains.reshape(16, 256), ((0, 0), (0, 1408 - 256))),
        jnp.pad(conv_w.reshape(12, 1408), ((0, 4), (0, 0)))], axis=0)
    big = [nm for nm in KIND if nm != "small"]
    half = {nm: ws[nm].astype(BF16) for nm in big}
    W = {nm: [None] * LAYERS[nm] for nm in big if nm != "w_kvf"}
    g_small = None
    groups = [("0a", [("w_qkv_a", 0), ("w_o_a", 0), ("small", 0)]), ("0b", [("w_up", 0), ("w_down", 0)]),
              ("1", [("w_qkv_a", 1), ("w_o_a", 1), ("w_up", 1), ("w_down", 1)]),
              ("2", [("w_kvf", 0), ("w_q_b", 0), ("w_o_b", 0), ("w_up", 2), ("w_down", 2)]),
              ("3", [("w_q_b", 1), ("w_o_b", 1), ("w_up", 3), ("w_down", 3)])]
    for tag, group in groups:
        shards = [small if nm == "small" else half[nm] if nm == "w_kvf" else half[nm][i] for nm, i in group]
        got = _seq_gather(shards, [KIND[nm] for nm, _ in group], name="gather_layer" + tag, cid=1)
        for (nm, i), g in zip(group, got):
            if nm == "small":
                g_small = g
            elif nm == "w_kvf":
                W[nm] = g.transpose(1, 0, 2).reshape(D, 2 * D + 16)
            else:
                W[nm][i] = g.transpose(1, 0, 2).reshape(D, 3 * A_W) if nm == "w_qkv_a" else g
    gains = g_small[:, :16, :256].transpose(1, 0, 2).reshape(DEPTH, 4, 1, D)
    cw_full = g_small[:, 16:28, :].transpose(1, 0, 2).reshape(DEPTH, 3, 2 * D_FF)
    cb_full = conv_b.reshape(DEPTH, 1, 2 * D_FF)

    reducer = _Reducer(where)
    sq, dh = _fwd_bwd(x[0], loss_target[0], W, gains, cw_full, cb_full, kv_norm, b_f, reducer)
    loss = lax.psum(sq[0, 0] * (0.5 / D), ("x", "y", "c"))
    return _update(loss, dh[None], reducer.finish(), chip, ws, ms, vs)


def _fwd_bwd(h, target, W, gains, cw_full, cb_full, kv_norm, b_f, reduce):
    w_kv = W["w_kvf"][:, :2 * D]
    w_kvf_pad = jnp.pad(W["w_kvf"], ((0, 0), (0, 128 - 16)))
    w_f = w_kvf_pad[:, 2 * D:]
    kvn_g = kv_norm.reshape(1, D)
    bf_pad = jnp.pad(b_f, (0, 128 - 16)).reshape(1, 128)
    tabs = _rope_tables()
    headsum = _headsum_matrix()

    saved = []
    kv = zf = c_col = c_row = kvn = h_kv = None
    for l in range(DEPTH):
        s = {"h": h}
        g = gains[l]
        xn = _rms_fwd(h, g[0], out_dtype=BF16, name="rms_in")
        s["xn"] = xn
        if l < N_A:
            qkv = _matmul(xn, W["w_qkv_a"][l], mode="nn", out_dtype=F32, name="mm_qkv", mnk=(T, 3 * A_W, D), tn=768)
            qkvp = _rope_fwd(qkv, tabs).reshape(9, 2, T, 128)
            o_p, lse_p = _band_fwd(qkvp)
            att, o3, lse3 = _combine_fwd(o_p, lse_p)
            s.update(qkvp=qkvp, o3=o3, lse3=lse3, lse_p=lse_p, att=att)
            mix = _matmul(att, W["w_o_a"][l], mode="nn", out_dtype=F32, name="mm_oa", mnk=(T, D, A_W))
        else:
            j = l - N_A
            if l == N_A:
                h_kv = h
                kvn = _rms_fwd(h, kvn_g, out_dtype=BF16, name="rms_in")
                kv = _matmul(kvn, w_kv, mode="nn", out_dtype=BF16, name="mm_kv")
                zf = _matmul(kvn, w_f, mode="nn", out_dtype=F32, name="mm_f")
                cum = _gates_fwd(zf, bf_pad)[:, :16]
                c_col = cum.reshape(T, 8, 2).transpose(1, 0, 2)
                c_row = cum.T.reshape(8, 2, T)
            q = _matmul(xn, W["w_q_b"][j], mode="nn", out_dtype=BF16, name="mm_qb", mnk=(T, D, D), alpha=HD ** -0.5)
            o = _fox_fwd(q, kv, c_col, c_row)
            s.update(q=q, o=o)
            mix = _matmul(o, W["w_o_b"][j], mode="nn", out_dtype=F32, name="mm_ob", mnk=(T, D, D))
        s["mix"] = mix
        h1 = _rms_fwd(mix, g[1], res=h, out_dtype=F32, name="rms_res")
        xn2 = _rms_fwd(h1, g[2], out_dtype=BF16, name="rms_in")
        a = _matmul(xn2, W["w_up"][l], mode="nn", out_dtype=F32, name="mm_up", mnk=(T, 2 * D_FF, D))
        u = _convgate_fwd(a, cw_full[l], cb_full[l])
        f = _matmul(u, W["w_down"][l], mode="nn", out_dtype=F32, name="mm_down", mnk=(T, D, D_FF), tm=1024, tk=D_FF)
        h = _rms_fwd(f, g[3], res=h1, out_dtype=F32, name="rms_res")
        s.update(h1=h1, xn2=xn2, a=a, u=u, f=f)
        saved.append(s)

    dh, sq = _loss_head(h, target)

    d_gains = [[None] * 4 for _ in range(DEPTH)]
    d_cw, d_cb = [None] * DEPTH, [None] * DEPTH
    zeros_td = jnp.zeros((T, D), F32)
    fox_acc = (zeros_td, zeros_td, jnp.zeros((D // 128, T, 128), F32), jnp.zeros((D // 128, 8, T), F32))
    d_kvnorm = d_bf = token = None

    def dw(nm, a, b, **kw):
        return _matmul(a, b, mode="tn", out_dtype=BF16, name="mm_dw_" + nm, **kw)

    def slabs(full, width):
        return full.reshape(full.shape[0], N_CHIPS, width).transpose(1, 0, 2)

    for l in reversed(range(DEPTH)):
        s = saved[l]
        g = gains[l]
        df, d_gains[l][3] = _rms_bwd(dh, s["f"], g[3], out_dtype=BF16, name="rms_bwd")
        du = _matmul(df, W["w_down"][l], mode="nt", out_dtype=F32, name="mm_down_dx", mnk=(T, D_FF, D), tn=256, after=token)
        g_down = dw("w_down", s["u"], df, tm=1408, tn=1024)
        da, d_cw[l], d_cb[l] = _convgate_bwd(s["a"], du, cw_full[l], cb_full[l])
        dxn2 = _matmul(da, W["w_up"][l], mode="nt", out_dtype=F32, name="mm_up_dx", mnk=(T, D, 2 * D_FF), tm=1024, tn=1024, tk=1408,
                       a_map=_halves_a)
        g_up = dw("w_up", s["xn2"], da, mnk=(D, 2 * D_FF, T), tn=1408, b_map=_halves_b)
        token = reduce([("w_down", l, g_down), ("w_up", l, g_up)], "ffn%d" % l)
        dh1, d_gains[l][2] = _rms_bwd(dxn2, s["h1"], g[2], dres=dh, out_dtype=F32, name="rms_bwd_res")
        dmix, d_gains[l][1] = _rms_bwd(dh1, s["mix"], g[1], out_dtype=BF16, name="rms_bwd")
        if l < N_A:
            datt = _matmul(dmix, W["w_o_a"][l], mode="nt", out_dtype=F32, name="mm_oa_dx", mnk=(T, A_W, D), tn=768, after=token)
            g_o = dw("w_o_a", s["att"], dmix, tm=768, tn=1024)
            do_p, dlt_p = _combine_bwd(datt, s["o3"], s["lse3"], headsum)
            dqkv = None
            for which, d in enumerate(_band_bwd(s["qkvp"], do_p, s["lse_p"], dlt_p)):
                dqkv = _rope_bwd(d, which, tabs, dqkv)
            dxn = _matmul(dqkv, W["w_qkv_a"][l], mode="nt", out_dtype=F32, name="mm_qkv_dx", mnk=(T, D, 3 * A_W), tm=1024, tn=1024, tk=3 * A_W)
            g_qkv = dw("w_qkv_a", s["xn"], dqkv, tn=768)
            group = [("w_o_a", l, g_o), ("w_qkv_a", l, slabs(g_qkv, 576))]
        else:
            j = l - N_A
            do = _matmul(dmix, W["w_o_b"][j], mode="nt", out_dtype=BF16, name="mm_ob_dx", mnk=(T, D, D), after=token)
            g_o = dw("w_o_b", s["o"], dmix, tn=1024)
            dq, *fox_acc = _fox_bwd(s["q"], kv, do, c_col, c_row, fox_acc)
            dxn = _matmul(dq, W["w_q_b"][j], mode="nt", out_dtype=F32, name="mm_qb_dx", mnk=(T, D, D))
            g_q = dw("w_q_b", s["xn"], dq, tn=1024)
            group = [("w_o_b", j, g_o), ("w_q_b", j, g_q)]
        dh, d_gains[l][0] = _rms_bwd(dxn, s["h"], g[0], dres=dh1, out_dtype=F32, name="rms_bwd_res")
        if l == N_A:
            dk, dv, dcq, dck = fox_acc
            dc16 = dcq[:, :, :2].transpose(1, 0, 2).reshape(T, 16) - dck[:, :2, :].reshape(16, T).T
            dzf, d_bf = _gates_bwd(jnp.pad(dc16, ((0, 0), (0, 128 - 16))), zf, bf_pad)
            dkvf = jnp.concatenate([dk.astype(BF16), dv.astype(BF16), dzf], axis=1)
            g_kvf = _matmul(kvn, dkvf, mode="tn", out_dtype=BF16, name="mm_kvf_dw", tm=512, tn=2 * D + 128)[:, :2 * D + 16]
            dkvn = _matmul(dkvf, w_kvf_pad, mode="nt", out_dtype=F32, name="mm_kvf_dx", tm=1024, tn=1024, tk=2 * D + 128)
            dh, d_kvnorm = _rms_bwd(dkvn, h_kv, kvn_g, dres=dh, out_dtype=F32, name="rms_bwd_res")
            group.append(("w_kvf", 0, slabs(g_kvf, 516)))
        token = reduce(group, "mix%d" % l)
    small_flat = jnp.concatenate([
        jnp.stack([jnp.stack(r) for r in d_gains]).reshape(-1),
        jnp.stack(d_cw).transpose(0, 2, 1, 3).reshape(-1),
        jnp.stack(d_cb).reshape(-1),
        d_kvnorm.reshape(-1), d_bf[0, :16]])
    small = jnp.pad(small_flat, (0, 2 * N_CHIPS * SMALL_ROWS * SMALL_W - small_flat.shape[0]))
    reduce([("small", 0, small.reshape(N_CHIPS, 2 * SMALL_ROWS, SMALL_W))], "small")
    return sq, dh


def _update(loss, grad_x, reduced, chip, ws, ms, vs):
    red_s = reduced.pop("small")
    buf_s = lax.dynamic_update_slice(jnp.zeros((2, N_CHIPS, SMALL_ROWS, SMALL_W), F32), red_s.reshape(2, 1, SMALL_ROWS, SMALL_W),
                                     (0, chip, 0, 0))
    (all_s,) = _allgather([buf_s], ["slab"], name="gather_small_grads")
    sflat = all_s.transpose(1, 0, 2, 3).reshape(-1)

    grads = {nm: r.reshape(ws[nm].shape) for nm, r in reduced.items()}
    o = 0
    g_gains_full = sflat[o:o + 16 * D].reshape(DEPTH, 4, D); o += 16 * D
    g_cw_full = sflat[o:o + 12 * 2 * D_FF].reshape(DEPTH, 3, 2 * D_FF); o += 12 * 2 * D_FF
    grads["conv_b"] = sflat[o:o + 4 * 2 * D_FF].reshape(DEPTH, 2 * D_FF); o += 4 * 2 * D_FF
    grads["kv_norm"] = sflat[o:o + D]; o += D
    grads["b_f"] = sflat[o:o + 16]
    grads["norm_gains"] = lax.dynamic_slice_in_dim(g_gains_full, chip * 256, 256, axis=2)
    grads["conv_w"] = lax.dynamic_slice_in_dim(g_cw_full, chip * 1408, 1408, axis=2)

    names = ["norm_gains", "w_qkv_a", "w_o_a", "w_q_b", "w_o_b", "kv_norm", "w_kvf", "b_f", "w_up", "conv_w", "conv_b", "w_down"]
    deltas, new_m, new_v = {}, {}, {}
    for nm in names:
        shp = ws[nm].shape
        two = (math.prod(shp[:-1]), shp[-1]) if len(shp) > 1 else (1, shp[0])
        d, m2, v2 = _adamw(ws[nm].reshape(two), ms[nm].reshape(two), vs[nm].reshape(two), grads[nm].reshape(two),
                           name="adamw_" + nm)
        deltas[nm], new_m[nm], new_v[nm] = d.reshape(shp), m2.reshape(shp), v2.reshape(shp)

    return (loss, grad_x, *[grads[nm] for nm in names], *[deltas[nm] for nm in names],
            *[new_m[nm] for nm in names], *[new_v[nm] for nm in names])
```

```python
import math

import jax
import jax.numpy as jnp
from jax import lax
from jax.experimental import pallas as pl
from jax.experimental.pallas import tpu as pltpu
from jax.experimental.pallas import tpu_sc as plsc

F32 = jnp.float32
BF16 = jnp.bfloat16
MESH = pl.DeviceIdType.MESH
ANY = pl.BlockSpec(memory_space=pl.ANY)

T = 2048
D = 1024
HD = 64
DEPTH = 4
N_A = 2
A_W = 768
GW = 256
DIL = (1, 4, 16)
BLK = 128
D_FF = 2816
ROPE_THETA = 500000.0
EPS = 1e-6
NEG = -1e30
N_CHIPS = 4
FQ = 256
CT = 128
VMEM_BIG = 48 * 1024 * 1024

ADAM_LR, ADAM_B1, ADAM_B2, ADAM_EPS, ADAM_WD, ADAM_STEP = 0.001, 0.9, 0.999, 1e-08, 0.01, 10

NN = (((1,), (0,)), ((), ()))
NT = (((1,), (1,)), ((), ()))
TN = (((0,), (0,)), ((), ()))


def _dot(a, b, dims):
    return lax.dot_general(a, b, dims, preferred_element_type=F32)


def _pick(dim, pref):
    if dim <= pref:
        return dim
    best = None
    for t in range(128, pref + 1, 128):
        if dim % t == 0:
            best = t
    assert best is not None, (dim, pref)
    return best


def _params(sem=None, vmem=None):
    kw = {}
    if sem is not None:
        kw["dimension_semantics"] = sem
    if vmem is not None:
        kw["vmem_limit_bytes"] = vmem
    return pltpu.CompilerParams(**kw)


def _matmul(a, b, *, mode, out_dtype, name, mnk=None, alpha=None, tm=2048, tn=512, tk=2048,
            a_map=None, b_map=None, acc_init=None, out_slab=None, out_slabs=None, out_buf=None, after=None):
    if mnk is not None:
        M, N, K = mnk
    elif mode == "nn":
        (M, K), (_, N) = a.shape, b.shape
    elif mode == "nt":
        (M, K), (N, _) = a.shape, b.shape
    else:
        (K, M), (_, N) = a.shape, b.shape
    tm, tn, tk = _pick(M, tm), _pick(N, tn), _pick(K, tk)
    nk = K // tk
    dims = {"nn": NN, "nt": NT, "tn": TN}[mode]
    after = [t for t in (after or ()) if t is not None]
    n_in = 2 + (acc_init is not None) + len(after) + (out_buf is not None)

    def body(*refs):
        a_ref, b_ref = refs[0], refs[1]
        o_ref = refs[n_in]
        k = pl.program_id(2)

        def finish(r):
            if alpha is not None:
                r = r * alpha
            o_ref[...] = r.astype(out_dtype)

        def product():
            r = _dot(a_ref[...], b_ref[...], dims)
            return r if acc_init is None else r + refs[2][...]

        if nk == 1:
            finish(product())
            return
        acc_ref = refs[n_in + 1]

        @pl.when(k == 0)
        def _():
            acc_ref[...] = product()

        @pl.when((k > 0) & (k < nk - 1))
        def _():
            acc_ref[...] += _dot(a_ref[...], b_ref[...], dims)

        @pl.when(k == nk - 1)
        def _():
            finish(acc_ref[...] + _dot(a_ref[...], b_ref[...], dims))

    a_blk = (tk, tm) if mode == "tn" else (tm, tk)
    b_blk = (tn, tk) if mode == "nt" else (tk, tn)
    if a_map is not None:
        a_spec = pl.BlockSpec((None,) + a_blk, a_map(tm, tn, tk))
    elif mode == "tn":
        a_spec = pl.BlockSpec(a_blk, lambda i, j, k: (k, i))
    else:
        a_spec = pl.BlockSpec(a_blk, lambda i, j, k: (i, k))
    if b_map is not None:
        b_spec = pl.BlockSpec((None,) + b_blk, b_map(tm, tn, tk))
    elif mode == "nt":
        b_spec = pl.BlockSpec(b_blk, lambda i, j, k: (j, k))
    else:
        b_spec = pl.BlockSpec(b_blk, lambda i, j, k: (k, j))
    ins, specs, alias = [a, b], [a_spec, b_spec], {}
    if acc_init is not None:
        ins.append(acc_init)
        specs.append(pl.BlockSpec((tm, tn), lambda i, j, k: (i, j)))
    ins += after
    specs += [ANY] * len(after)
    if out_buf is not None:
        alias = {len(ins): 0}
        ins.append(out_buf)
        specs.append(ANY)
    if out_slab is None:
        o_spec = pl.BlockSpec((tm, tn), lambda i, j, k: (i, j))
        o_shape = jax.ShapeDtypeStruct((M, N), out_dtype)
    else:
        o_spec = pl.BlockSpec((None, tm, tn), lambda i, j, k: (out_slab, i, j))
        o_shape = jax.ShapeDtypeStruct((out_slabs, M, N), out_dtype)
    return pl.pallas_call(
        body,
        grid=(M // tm, N // tn, nk),
        in_specs=specs,
        out_specs=o_spec,
        out_shape=o_shape,
        scratch_shapes=[pltpu.VMEM((tm, tn), F32)] if nk > 1 else [],
        input_output_aliases=alias,
        compiler_params=_params(("parallel", "parallel", "arbitrary"), VMEM_BIG),
        name=name,
    )(*ins)


def _slab(l, mode):
    if mode == "nt":
        return lambda tm, tn, tk: (lambda i, j, k: (l, j, k))
    return lambda tm, tn, tk: (lambda i, j, k: (l, k, j))


def _rms_fwd(x, g, *, out_dtype, name, res=None, tr=256):
    n, d = x.shape

    def body(*refs):
        x_ref, g_ref = refs[0], refs[1]
        o_ref = refs[-1]
        xv = x_ref[...].astype(F32)
        y = xv * lax.rsqrt(jnp.mean(xv * xv, axis=-1, keepdims=True) + EPS) * g_ref[...]
        if res is not None:
            y = y + refs[2][...]
        o_ref[...] = y.astype(out_dtype)

    row = pl.BlockSpec((tr, d), lambda i: (i, 0))
    vec = pl.BlockSpec((1, d), lambda i: (0, 0))
    ins = [x, g] + ([] if res is None else [res])
    specs = [row, vec] + ([] if res is None else [row])
    return pl.pallas_call(
        body, grid=(n // tr,), in_specs=specs, out_specs=row,
        out_shape=jax.ShapeDtypeStruct((n, d), out_dtype),
        compiler_params=_params(("parallel",)), name=name,
    )(*ins)


def _rms_bwd(dy, x, g, *, out_dtype, name, dres=None, tr=256):
    n, d = x.shape

    def body(*refs):
        dy_ref, x_ref, g_ref = refs[0], refs[1], refs[2]
        dx_ref, dg_ref = refs[-2], refs[-1]
        xv = x_ref[...].astype(F32)
        dyv = dy_ref[...].astype(F32)
        rstd = lax.rsqrt(jnp.mean(xv * xv, axis=-1, keepdims=True) + EPS)
        xhat = xv * rstd
        dxh = dyv * g_ref[...]
        dx = rstd * (dxh - xhat * jnp.mean(dxh * xhat, axis=-1, keepdims=True))
        if dres is not None:
            dx = dx + refs[3][...]
        dx_ref[...] = dx.astype(out_dtype)

        @pl.when(pl.program_id(0) == 0)
        def _():
            dg_ref[...] = jnp.zeros_like(dg_ref)

        dg_ref[...] += jnp.sum(dyv * xhat, axis=0, keepdims=True)

    row = pl.BlockSpec((tr, d), lambda i: (i, 0))
    vec = pl.BlockSpec((1, d), lambda i: (0, 0))
    ins = [dy, x, g] + ([] if dres is None else [dres])
    specs = [row, row, vec] + ([] if dres is None else [row])
    return pl.pallas_call(
        body, grid=(n // tr,), in_specs=specs, out_specs=[row, vec],
        out_shape=[jax.ShapeDtypeStruct((n, d), out_dtype), jax.ShapeDtypeStruct((1, d), F32)],
        compiler_params=_params(("arbitrary",)), name=name,
    )(*ins)


def _rms_res_in(x, g_res, res, g_in, *, name, tr=256):
    n, d = x.shape

    def body(x_ref, gr_ref, r_ref, gi_ref, h_ref, n_ref):
        xv = x_ref[...].astype(F32)
        h = r_ref[...] + xv * lax.rsqrt(jnp.mean(xv * xv, axis=-1, keepdims=True) + EPS) * gr_ref[...]
        h_ref[...] = h
        n_ref[...] = (h * lax.rsqrt(jnp.mean(h * h, axis=-1, keepdims=True) + EPS) * gi_ref[...]).astype(BF16)

    row = pl.BlockSpec((tr, d), lambda i: (i, 0))
    vec = pl.BlockSpec((1, d), lambda i: (0, 0))
    return pl.pallas_call(
        body, grid=(n // tr,), in_specs=[row, vec, row, vec], out_specs=[row, row],
        out_shape=[jax.ShapeDtypeStruct((n, d), F32), jax.ShapeDtypeStruct((n, d), BF16)],
        compiler_params=_params(("parallel",)), name=name,
    )(x, g_res, res, g_in)


def _rms_bwd2(dy, x, g, dres, x2, g2, *, name, tr=256):
    n, d = x.shape

    def one(dyv, xv, gv):
        rstd = lax.rsqrt(jnp.mean(xv * xv, axis=-1, keepdims=True) + EPS)
        xhat = xv * rstd
        dxh = dyv * gv
        return rstd * (dxh - xhat * jnp.mean(dxh * xhat, axis=-1, keepdims=True)), jnp.sum(dyv * xhat, axis=0, keepdims=True)

    def body(dy_ref, x_ref, g_ref, r_ref, x2_ref, g2_ref, dx_ref, d2_ref, dg_ref, dg2_ref):
        dx, dg = one(dy_ref[...].astype(F32), x_ref[...].astype(F32), g_ref[...])
        dx = dx + r_ref[...]
        dx_ref[...] = dx
        d2, dg2 = one(dx, x2_ref[...].astype(F32), g2_ref[...])
        d2_ref[...] = d2.astype(BF16)

        @pl.when(pl.program_id(0) == 0)
        def _():
            dg_ref[...] = jnp.zeros_like(dg_ref)
            dg2_ref[...] = jnp.zeros_like(dg2_ref)

        dg_ref[...] += dg
        dg2_ref[...] += dg2

    row = pl.BlockSpec((tr, d), lambda i: (i, 0))
    vec = pl.BlockSpec((1, d), lambda i: (0, 0))
    return pl.pallas_call(
        body, grid=(n // tr,), in_specs=[row, row, vec, row, row, vec], out_specs=[row, row, vec, vec],
        out_shape=[jax.ShapeDtypeStruct((n, d), F32), jax.ShapeDtypeStruct((n, d), BF16),
                   jax.ShapeDtypeStruct((1, d), F32), jax.ShapeDtypeStruct((1, d), F32)],
        compiler_params=_params(("arbitrary",)), name=name,
    )(dy, x, g, dres, x2, g2)


def _loss_head(h, target, *, tr=256):
    n, d = h.shape

    def body(h_ref, t_ref, dh_ref, s_ref):
        err = h_ref[...] - t_ref[...]
        dh_ref[...] = err * (1.0 / d)

        @pl.when(pl.program_id(0) == 0)
        def _():
            s_ref[...] = jnp.zeros_like(s_ref)

        s_ref[...] += jnp.sum(err * err)

    row = pl.BlockSpec((tr, d), lambda i: (i, 0))
    acc = pl.BlockSpec((8, 128), lambda i: (0, 0))
    return pl.pallas_call(
        body, grid=(n // tr,), in_specs=[row, row], out_specs=[row, acc],
        out_shape=[jax.ShapeDtypeStruct((n, d), F32), jax.ShapeDtypeStruct((8, 128), F32)],
        compiler_params=_params(("arbitrary",)), name="loss_head",
    )(h, target)


def _rope_tables():
    pos = jnp.arange(T, dtype=F32)
    inv = ROPE_THETA ** (-jnp.arange(0, 16, 2, dtype=F32) / 16)
    ang = pos[:, None] * inv[None, :]
    cos, sin = jnp.cos(ang), jnp.sin(ang)
    one = jnp.ones((T, HD - 16), F32)
    zero8 = jnp.zeros((T, 8), F32)
    zero = jnp.zeros((T, HD - 16), F32)
    c = jnp.concatenate([cos, cos, one], axis=1)
    s1 = jnp.concatenate([zero8, sin, zero], axis=1)
    s2 = jnp.concatenate([-sin, zero8, zero], axis=1)
    c, s1, s2 = (jnp.concatenate([t, t], axis=1) for t in (c, s1, s2))
    scale = HD ** -0.5
    return (jnp.stack([c * scale, c, jnp.ones_like(c)]), jnp.stack([s1 * scale, s1, jnp.zeros_like(c)]),
            jnp.stack([s2 * scale, s2, jnp.zeros_like(c)]))


def _row_chunks(r):
    if r == 1:
        n = 4
        return [(slice(i * (T // n), (i + 1) * (T // n)),) * 2 for i in range(n)]
    per = T // r
    return [(pl.ds(j, per, stride=r), slice(j * per, (j + 1) * per)) for j in range(r)]


def _rope_fwd(qkv, tabs):
    def body(x_ref, c_ref, s1_ref, s2_ref, o_ref):
        g = lax.rem(lax.div(pl.program_id(0), 2), 3)
        for gi, r in enumerate(DIL):
            @pl.when(g == gi)
            def _(r=r):
                for tok, prm in _row_chunks(r):
                    x = x_ref[tok, :]
                    y = x * c_ref[tok, :] + pltpu.roll(x, 8, 1) * s1_ref[tok, :] + pltpu.roll(x, 120, 1) * s2_ref[tok, :]
                    o_ref[prm, :] = y.astype(BF16)

    tab = pl.BlockSpec((None, T, 128), lambda b: (lax.div(b, 6), 0, 0))
    return pl.pallas_call(
        body, grid=(18,), in_specs=[pl.BlockSpec((T, 128), lambda b: (0, b)), tab, tab, tab],
        out_specs=pl.BlockSpec((None, T, 128), lambda b: (b, 0, 0)), out_shape=jax.ShapeDtypeStruct((18, T, 128), BF16),
        compiler_params=_params(("parallel",)), name="rope_fwd",
    )(qkv, *tabs)


def _rope_bwd(d, which, tabs, out_buf):
    def body(d_ref, c_ref, s1_ref, s2_ref, *rest):
        o_ref, tok_ref = rest[-2], rest[-1]
        g = lax.div(pl.program_id(0), 2)
        for gi, r in enumerate(DIL):
            @pl.when(g == gi)
            def _(r=r):
                for tok, prm in _row_chunks(r):
                    tok_ref[tok, :] = d_ref[prm, :]
                for rows, _ in _row_chunks(1):
                    gx = tok_ref[rows, :]
                    y = gx * c_ref[rows, :] + pltpu.roll(gx * s1_ref[rows, :], 120, 1) + pltpu.roll(gx * s2_ref[rows, :], 8, 1)
                    o_ref[rows, :] = y.astype(BF16)

    tab = pl.BlockSpec((None, T, 128), lambda b: (which, 0, 0))
    ins = [d, *tabs] + ([] if out_buf is None else [out_buf])
    specs = [pl.BlockSpec((None, None, T, 128), lambda b: (lax.div(b, 2), lax.rem(b, 2), 0, 0)), tab, tab, tab]
    return pl.pallas_call(
        body, grid=(6,), in_specs=specs + ([] if out_buf is None else [ANY]),
        out_specs=pl.BlockSpec((T, 128), lambda b: (0, 6 * which + b)),
        out_shape=jax.ShapeDtypeStruct((T, 3 * A_W), BF16), scratch_shapes=[pltpu.VMEM((T, 128), F32)],
        input_output_aliases={} if out_buf is None else {4: 0},
        compiler_params=_params(("arbitrary",)), name="rope_bwd",
    )(*ins)


def _head_mask(x, lane_lo):
    lane = lax.broadcasted_iota(jnp.int32, x.shape, 1)
    keep = (lane < HD) if lane_lo else (lane >= HD)
    return jnp.where(keep, x.astype(F32), 0.0).astype(BF16)


def _band_scalars():
    g, b = pl.program_id(0), pl.program_id(1)
    nbs = lax.shift_right_logical(jnp.int32(T // BLK), 2 * g)
    has_prev = jnp.where((b & (nbs - 1)) != 0, 1, 0)
    next_ok = jnp.where(((b + 1) & (nbs - 1)) != 0, 1, 0)
    return has_prev, next_ok


def _band_mask_q(has_prev):
    row = lax.broadcasted_iota(jnp.int32, (BLK, 2 * BLK), 0)
    col = lax.broadcasted_iota(jnp.int32, (BLK, 2 * BLK), 1)
    return ((col < BLK) & (col >= row) & (has_prev == 1)) | ((col >= BLK) & (col - BLK <= row))


def _band_mask_k(next_ok):
    row = lax.broadcasted_iota(jnp.int32, (2 * BLK, BLK), 0)
    col = lax.broadcasted_iota(jnp.int32, (2 * BLK, BLK), 1)
    return ((row < BLK) & (col <= row)) | ((row >= BLK) & (col >= row - BLK) & (next_ok == 1))


def _band_spec(base, step):
    nb = T // BLK
    at = {"cur": lambda b: b, "prev": lambda b: jnp.maximum(b - 1, 0), "next": lambda b: jnp.minimum(b + 1, nb - 1)}[step]
    return pl.BlockSpec((None, 2, BLK, 128), lambda g, b: (base + g, 0, at(b), 0))


def _band_fwd(qkv):
    nb = T // BLK

    def body(q_ref, kc_ref, kp_ref, vc_ref, vp_ref, o_ref, l_ref):
        has_prev, _ = _band_scalars()
        mask = _band_mask_q(has_prev)
        lane = lax.broadcasted_iota(jnp.int32, (BLK, 128), 1)
        for p in range(2):
            qp = q_ref[p]
            kcat = jnp.concatenate([kp_ref[p], kc_ref[p]], axis=0)
            vcat = jnp.concatenate([vp_ref[p], vc_ref[p]], axis=0)
            o_acc = jnp.zeros((BLK, 128), F32)
            lse = jnp.zeros((BLK, 128), F32)
            for e in range(2):
                s = _dot(_head_mask(qp, e == 0), kcat, NT)
                s = jnp.where(mask, s, NEG)
                m = jnp.max(s, axis=-1, keepdims=True)
                pr = jnp.exp(s - m)
                l = jnp.sum(pr, axis=-1, keepdims=True)
                o_acc = o_acc + _dot(pr.astype(BF16), _head_mask(vcat, e == 0), NN) / l
                lse = jnp.where((lane < HD) if e == 0 else (lane >= HD), m + jnp.log(l), lse)
            o_ref[p] = o_acc
            l_ref[p] = lse

    out = _band_spec(0, "cur")
    shp = jax.ShapeDtypeStruct((3, 2, T, 128), F32)
    return pl.pallas_call(
        body, grid=(3, nb),
        in_specs=[_band_spec(0, "cur"), _band_spec(3, "cur"), _band_spec(3, "prev"), _band_spec(6, "cur"), _band_spec(6, "prev")],
        out_specs=[out, out], out_shape=[shp, shp],
        compiler_params=_params(("parallel", "parallel")), name="band_fwd",
    )(qkv, qkv, qkv, qkv, qkv)


def _band_bwd(qkv, do, lse, dlt):
    nb = T // BLK

    def body(qc_ref, qn_ref, kc_ref, kp_ref, vc_ref, vp_ref, doc_ref, don_ref, lc_ref, ln_ref, dc_ref, dn_ref,
             dq_ref, dk_ref, dv_ref):
        has_prev, next_ok = _band_scalars()
        mask_q = _band_mask_q(has_prev)
        mask_k = _band_mask_k(next_ok)
        for p in range(2):
            qc, qn = qc_ref[p], qn_ref[p]
            doc, don = doc_ref[p], don_ref[p]
            kc, vc = kc_ref[p], vc_ref[p]
            kcat = jnp.concatenate([kp_ref[p], kc], axis=0)
            vcat = jnp.concatenate([vp_ref[p], vc], axis=0)
            qcat = jnp.concatenate([qc, qn], axis=0)
            docat = jnp.concatenate([doc, don], axis=0)
            dq = jnp.zeros((BLK, 128), F32)
            dk = jnp.zeros((BLK, 128), F32)
            dv = jnp.zeros((BLK, 128), F32)
            for e in range(2):
                lo = e == 0
                col = slice(HD * e, HD * e + 1)
                lse_c, lse_n = lc_ref[p, :, col], ln_ref[p, :, col]
                dl_c, dl_n = dc_ref[p, :, col], dn_ref[p, :, col]
                s = jnp.where(mask_q, _dot(_head_mask(qc, lo), kcat, NT), NEG)
                pr = jnp.exp(s - lse_c)
                dp = _dot(_head_mask(doc, lo), vcat, NT)
                ds = pr * (dp - dl_c)
                dq = dq + _dot(ds.astype(BF16), _head_mask(kcat, lo), NN)
                qm, dom = _head_mask(qcat, lo), _head_mask(docat, lo)
                s2 = jnp.where(mask_k, _dot(qm, kc, NT), NEG)
                p2 = jnp.exp(s2 - jnp.concatenate([lse_c, lse_n], axis=0))
                dv = dv + _dot(p2.astype(BF16), dom, TN)
                dp2 = _dot(dom, vc, NT)
                ds2 = p2 * (dp2 - jnp.concatenate([dl_c, dl_n], axis=0))
                dk = dk + _dot(ds2.astype(BF16), qm, TN)
            dq_ref[p] = dq
            dk_ref[p] = dk
            dv_ref[p] = dv

    cur, nxt = _band_spec(0, "cur"), _band_spec(0, "next")
    shp = jax.ShapeDtypeStruct((3, 2, T, 128), F32)
    return pl.pallas_call(
        body, grid=(3, nb),
        in_specs=[cur, nxt, _band_spec(3, "cur"), _band_spec(3, "prev"), _band_spec(6, "cur"), _band_spec(6, "prev"),
                  cur, nxt, cur, nxt, cur, nxt],
        out_specs=[cur, cur, cur], out_shape=[shp, shp, shp],
        compiler_params=_params(("parallel", "parallel")), name="band_bwd",
    )(qkv, qkv, qkv, qkv, qkv, qkv, do, do, lse, lse, dlt, dlt)


def _split3(x):
    hi = x.astype(BF16)
    r = x - hi.astype(F32)
    mid = r.astype(BF16)
    lo = (r - mid.astype(F32)).astype(BF16)
    return hi, mid, lo


def _dot3(x, m, dims=NN):
    hi, mid, lo = _split3(x)
    return _dot(hi, m, dims) + _dot(mid, m, dims) + _dot(lo, m, dims)


def _combine_weights(lses):
    l0, l1, l2 = lses
    m = jnp.maximum(jnp.maximum(l0, l1), l2)
    e = [jnp.exp(l0 - m), jnp.exp(l1 - m), jnp.exp(l2 - m)]
    inv = 1.0 / (e[0] + e[1] + e[2])
    return [ei * inv for ei in e]


CR = 256


def _combine_fwd(o, lse):
    def body(o_ref, l_ref, att_ref, o3_ref, l3_ref):
        for g, r in enumerate(DIL):
            for p in range(2):
                for tok, prm in _row_chunks(r):
                    o3_ref[g, p, tok, :] = o_ref[g, p, prm, :]
                    l3_ref[g, p, tok, :] = l_ref[g, p, prm, :]
        for i in range(T // CR):
            rows = slice(i * CR, (i + 1) * CR)
            for p in range(2):
                alpha = _combine_weights([l3_ref[g, p, rows, :] for g in range(3)])
                for g in range(3):
                    att_ref[rows, g * GW + p * 128: g * GW + (p + 1) * 128] = (o3_ref[g, p, rows, :] * alpha[g]).astype(BF16)

    shp = jax.ShapeDtypeStruct((3, 2, T, 128), F32)
    return pl.pallas_call(
        body, out_shape=[jax.ShapeDtypeStruct((T, A_W), BF16), shp, shp],
        compiler_params=_params(vmem=VMEM_BIG), name="combine_fwd",
    )(o, lse)


def _combine_bwd(datt, o3, l3, headsum):
    def body(d_ref, o_ref, l_ref, hs_ref, do_ref, dl_ref, tdo_ref, tdl_ref):
        hs = hs_ref[...]
        for p in range(2):
            for i in range(T // CR):
                rows = slice(i * CR, (i + 1) * CR)
                alpha = _combine_weights([l_ref[g, p, rows, :] for g in range(3)])
                total = jnp.zeros((CR, 128), F32)
                for g in range(3):
                    dg = d_ref[rows, g * GW + p * 128: g * GW + (p + 1) * 128]
                    tdo_ref[g, rows, :] = dg * alpha[g]
                    total = total + alpha[g] * _dot3(dg * o_ref[g, p, rows, :], hs)
                for g in range(3):
                    tdl_ref[g, rows, :] = alpha[g] * total
            for g, r in enumerate(DIL):
                for tok, prm in _row_chunks(r):
                    do_ref[g, p, prm, :] = tdo_ref[g, tok, :].astype(BF16)
                    dl_ref[g, p, prm, :] = tdl_ref[g, tok, :]

    return pl.pallas_call(
        body, out_shape=[jax.ShapeDtypeStruct((3, 2, T, 128), BF16), jax.ShapeDtypeStruct((3, 2, T, 128), F32)],
        scratch_shapes=[pltpu.VMEM((3, T, 128), F32), pltpu.VMEM((3, T, 128), F32)],
        compiler_params=_params(vmem=VMEM_BIG), name="combine_bwd",
    )(datt, o3, l3, headsum)


def _fox_scores(qm, k_ref, cq, ck_ref, e, i, n):
    s = _dot(qm, k_ref[0:n, :], NT) + (cq - ck_ref[0, e:e + 1, 0:n])
    row = lax.broadcasted_iota(jnp.int32, (FQ, n), 0)
    col = lax.broadcasted_iota(jnp.int32, (FQ, n), 1)
    s = jnp.where(col <= row + i * FQ, s, NEG)
    m = jnp.max(s, axis=-1, keepdims=True)
    pr = jnp.exp(s - m)
    return pr, jnp.sum(pr, axis=-1, keepdims=True)


def _fox_fwd(q, kv, c_col, c_row):
    def body(q_ref, k_ref, v_ref, cc_ref, cr_ref, o_ref, vm_ref):
        for e in range(2):
            vm_ref[e] = _head_mask(v_ref[...], e == 0)
        for i in range(T // FQ):
            n = (i + 1) * FQ
            rows = slice(i * FQ, n)
            acc = jnp.zeros((FQ, 128), F32)
            for e in range(2):
                qm = _head_mask(q_ref[rows, :], e == 0)
                pr, l = _fox_scores(qm, k_ref, cc_ref[0, rows, e:e + 1], cr_ref, e, i, n)
                acc = acc + _dot(pr.astype(BF16), vm_ref[e, 0:n, :], NN) / l
            o_ref[rows, :] = acc.astype(BF16)

    pair = pl.BlockSpec((T, 128), lambda p: (0, p))
    return pl.pallas_call(
        body, grid=(D // 128,),
        in_specs=[pair, pair, pl.BlockSpec((T, 128), lambda p: (0, D // 128 + p)),
                  pl.BlockSpec((1, T, 2), lambda p: (p, 0, 0)), pl.BlockSpec((1, 2, T), lambda p: (p, 0, 0))],
        out_specs=pair, out_shape=jax.ShapeDtypeStruct((T, D), BF16),
        scratch_shapes=[pltpu.VMEM((2, T, 128), BF16)],
        compiler_params=_params(("parallel",), VMEM_BIG), name="fox_fwd",
    )(q, kv, kv, c_col, c_row)


def _fox_bwd(q, kv, do, c_col, c_row, init):
    def body(q_ref, k_ref, v_ref, do_ref, cc_ref, cr_ref, ik_ref, iv_ref, iq_ref, ic_ref,
             dq_ref, dk_ref, dv_ref, dcq_ref, dck_ref, km_ref):
        dk_ref[...] = ik_ref[...]
        dv_ref[...] = iv_ref[...]
        dcq_ref[...] = iq_ref[...]
        dck_ref[...] = ic_ref[...]
        for e in range(2):
            km_ref[e] = _head_mask(k_ref[...], e == 0)
        for i in range(T // FQ):
            n = (i + 1) * FQ
            rows = slice(i * FQ, n)
            dq = jnp.zeros((FQ, 128), F32)
            for e in range(2):
                qm = _head_mask(q_ref[rows, :], e == 0)
                dom = _head_mask(do_ref[rows, :], e == 0)
                pr, l = _fox_scores(qm, k_ref, cc_ref[0, rows, e:e + 1], cr_ref, e, i, n)
                pr = pr / l
                dp = _dot(dom, v_ref[0:n, :], NT)
                ds = pr * (dp - jnp.sum(pr * dp, axis=-1, keepdims=True))
                dsb = ds.astype(BF16)
                dq = dq + _dot(dsb, km_ref[e, 0:n, :], NN)
                dk_ref[0:n, :] += _dot(dsb, qm, TN)
                dv_ref[0:n, :] += _dot(pr.astype(BF16), dom, TN)
                dcq_ref[0, rows, e:e + 1] += jnp.sum(ds, axis=-1, keepdims=True)
                dck_ref[0, e:e + 1, 0:n] += jnp.sum(ds, axis=0, keepdims=True)
            dq_ref[rows, :] = (dq * HD ** -0.5).astype(BF16)

    pair = pl.BlockSpec((T, 128), lambda p: (0, p))
    cq = pl.BlockSpec((1, T, 128), lambda p: (p, 0, 0))
    ck = pl.BlockSpec((1, 8, T), lambda p: (p, 0, 0))
    return pl.pallas_call(
        body, grid=(D // 128,),
        in_specs=[pair, pair, pl.BlockSpec((T, 128), lambda p: (0, D // 128 + p)), pair,
                  pl.BlockSpec((1, T, 2), lambda p: (p, 0, 0)), pl.BlockSpec((1, 2, T), lambda p: (p, 0, 0)),
                  pair, pair, cq, ck],
        out_specs=[pair, pair, pair, cq, ck],
        out_shape=[jax.ShapeDtypeStruct((T, D), BF16), jax.ShapeDtypeStruct((T, D), F32), jax.ShapeDtypeStruct((T, D), F32),
                   jax.ShapeDtypeStruct((D // 128, T, 128), F32), jax.ShapeDtypeStruct((D // 128, 8, T), F32)],
        scratch_shapes=[pltpu.VMEM((2, T, 128), BF16)],
        compiler_params=_params(("parallel",), VMEM_BIG), name="fox_bwd",
    )(q, kv, kv, do, c_col, c_row, *init)


def _tri(lower):
    r = lax.broadcasted_iota(jnp.int32, (BLK, BLK), 0)
    c = lax.broadcasted_iota(jnp.int32, (BLK, BLK), 1)
    return jnp.where((c <= r) if lower else (c >= r), 1.0, 0.0).astype(BF16)


def _gates_fwd(z, b):
    def body(z_ref, b_ref, c_ref):
        tri = _tri(True)
        carry = jnp.zeros((1, 128), F32)
        for i in range(T // BLK):
            rows = slice(i * BLK, (i + 1) * BLK)
            x = z_ref[rows, :] + b_ref[...]
            logf = jnp.minimum(x, 0.0) - jnp.log(1.0 + jnp.exp(-jnp.abs(x)))
            hi, mid, lo = _split3(logf)
            y = _dot(tri, hi, NN) + _dot(tri, mid, NN) + _dot(tri, lo, NN) + carry
            c_ref[rows, :] = y
            carry = y[BLK - 1:BLK, :]

    return pl.pallas_call(body, out_shape=jax.ShapeDtypeStruct((T, 128), F32), name="gates_fwd")(z, b)


def _gates_bwd(dc, z, b):
    def body(dc_ref, z_ref, b_ref, dz_ref, db_ref):
        tri = _tri(False)
        carry = jnp.zeros((1, 128), F32)
        db = jnp.zeros((1, 128), F32)
        for i in reversed(range(T // BLK)):
            rows = slice(i * BLK, (i + 1) * BLK)
            hi, mid, lo = _split3(dc_ref[rows, :])
            dlogf = _dot(tri, hi, NN) + _dot(tri, mid, NN) + _dot(tri, lo, NN) + carry
            carry = dlogf[0:1, :]
            x = z_ref[rows, :] + b_ref[...]
            dz = dlogf / (1.0 + jnp.exp(x))
            dz_ref[rows, :] = dz.astype(BF16)
            db = db + jnp.sum(dz, axis=0, keepdims=True)
        db_ref[...] = db

    return pl.pallas_call(
        body, out_shape=[jax.ShapeDtypeStruct((T, 128), BF16), jax.ShapeDtypeStruct((1, 128), F32)], name="gates_bwd",
    )(dc, z, b)


def _conv_pair(a_refs, cw_refs, cb_refs):
    row = lax.broadcasted_iota(jnp.int32, (T, CT), 0)
    outs = []
    for a_ref, cw_ref, cb_ref in zip(a_refs, cw_refs, cb_refs):
        z = a_ref[...]
        z1 = jnp.where(row >= 1, pltpu.roll(z, 1, 0), 0.0)
        z2 = jnp.where(row >= 2, pltpu.roll(z, 2, 0), 0.0)
        y = cw_ref[2:3, :] * z + cw_ref[1:2, :] * z1 + cw_ref[0:1, :] * z2 + cb_ref[...]
        outs.append((y, z, z1, z2))
    return outs


_GELU_K = math.sqrt(2.0 / math.pi)
N_CT = D_FF // CT


def _conv_specs():
    def at(rows, off):
        return pl.BlockSpec((rows, CT), lambda j: (0, j + off))
    return [at(T, 0), at(T, N_CT), at(3, 0), at(3, N_CT), at(1, 0), at(1, N_CT)]


def _convgate_fwd(a, cw, cb):
    def body(ag_ref, av_ref, wg_ref, wv_ref, bg_ref, bv_ref, u_ref):
        (g, _, _, _), (v, _, _, _) = _conv_pair((ag_ref, av_ref), (wg_ref, wv_ref), (bg_ref, bv_ref))
        th = jnp.tanh(_GELU_K * (g + 0.044715 * g * g * g))
        u_ref[...] = (0.5 * g * (1.0 + th) * v).astype(BF16)

    return pl.pallas_call(
        body, grid=(N_CT,), in_specs=_conv_specs(),
        out_specs=pl.BlockSpec((T, CT), lambda j: (0, j)), out_shape=jax.ShapeDtypeStruct((T, D_FF), BF16),
        compiler_params=_params(("parallel",), VMEM_BIG), name="convgate_fwd",
    )(a, a, cw, cw, cb, cb)


def _convgate_bwd(a, du, cw, cb):
    def body(ag_ref, av_ref, wg_ref, wv_ref, bg_ref, bv_ref, du_ref, da_ref, dcw_ref, dcb_ref):
        (g, gz, gz1, gz2), (v, vz, vz1, vz2) = _conv_pair((ag_ref, av_ref), (wg_ref, wv_ref), (bg_ref, bv_ref))
        du = du_ref[...].astype(F32)
        th = jnp.tanh(_GELU_K * (g + 0.044715 * g * g * g))
        gelu = 0.5 * g * (1.0 + th)
        dgelu = 0.5 * (1.0 + th) + 0.5 * g * (1.0 - th * th) * _GELU_K * (1.0 + 3 * 0.044715 * g * g)
        row = lax.broadcasted_iota(jnp.int32, (T, CT), 0)
        for h, (d, z, z1, z2, w_ref) in enumerate(((du * v * dgelu, gz, gz1, gz2, wg_ref), (du * gelu, vz, vz1, vz2, wv_ref))):
            d1 = jnp.where(row < T - 1, pltpu.roll(d, T - 1, 0), 0.0)
            d2 = jnp.where(row < T - 2, pltpu.roll(d, T - 2, 0), 0.0)
            da_ref[h] = (w_ref[2:3, :] * d + w_ref[1:2, :] * d1 + w_ref[0:1, :] * d2).astype(BF16)
            dcw_ref[h, 0:1, :] = jnp.sum(d * z2, axis=0, keepdims=True)
            dcw_ref[h, 1:2, :] = jnp.sum(d * z1, axis=0, keepdims=True)
            dcw_ref[h, 2:3, :] = jnp.sum(d * z, axis=0, keepdims=True)
            dcb_ref[h] = jnp.sum(d, axis=0, keepdims=True)

    def both(rows):
        return pl.BlockSpec((2, rows, CT), lambda j: (0, 0, j))

    return pl.pallas_call(
        body, grid=(N_CT,),
        in_specs=_conv_specs() + [pl.BlockSpec((T, CT), lambda j: (0, j))],
        out_specs=[both(T), both(3), both(1)],
        out_shape=[jax.ShapeDtypeStruct((2, T, D_FF), BF16), jax.ShapeDtypeStruct((2, 3, D_FF), F32),
                   jax.ShapeDtypeStruct((2, 1, D_FF), F32)],
        compiler_params=_params(("parallel",), VMEM_BIG), name="convgate_bwd",
    )(a, a, cw, cw, cb, cb, du)


def _halves_a(tm, tn, tk):
    per = D_FF // tk
    return lambda i, j, k: (lax.div(k, per), i, lax.rem(k, per))


def _halves_b(tm, tn, tk):
    per = D_FF // tn
    return lambda i, j, k: (lax.div(j, per), k, lax.rem(j, per))


def _adamw(w, m, v, g, *, name):
    r, c = w.shape
    tr = r
    if r * c > 256 * 1024:
        for cand in range(8, r, 8):
            if r % cand == 0 and cand * c <= 256 * 1024:
                tr = cand

    def body(w_ref, m_ref, v_ref, g_ref, d_ref, nm_ref, nv_ref):
        gv = g_ref[...]
        mn = ADAM_B1 * m_ref[...] + (1.0 - ADAM_B1) * gv
        vn = ADAM_B2 * v_ref[...] + (1.0 - ADAM_B2) * (gv * gv)
        m_hat = mn / (1.0 - ADAM_B1 ** ADAM_STEP)
        v_hat = vn / (1.0 - ADAM_B2 ** ADAM_STEP)
        d_ref[...] = -ADAM_LR * (m_hat / (jnp.sqrt(v_hat) + ADAM_EPS) + ADAM_WD * w_ref[...])
        nm_ref[...] = mn
        nv_ref[...] = vn

    blk = pl.BlockSpec((tr, c), lambda i: (i, 0))
    shp = jax.ShapeDtypeStruct((r, c), F32)
    return pl.pallas_call(
        body, grid=(r // tr,), in_specs=[blk] * 4, out_specs=[blk] * 3, out_shape=[shp] * 3,
        compiler_params=_params(("parallel",)), name=name,
    )(w, m, v, g)


def _place():
    x, y, c = lax.axis_index("x"), lax.axis_index("y"), lax.axis_index("c")
    chips = [(1 - x, y), (x, 1 - y), (1 - x, 1 - y)]
    return x, y, c, chips


def _window(ref, kind, s, half=None):
    lead = () if half is None else (half,)
    b, c = ref.shape[-2], ref.shape[-1]
    if kind == "col":
        return ref.at[lead + (slice(None), slice(None), pl.ds(s * (c // N_CHIPS), c // N_CHIPS))]
    if kind == "row":
        return ref.at[lead + (slice(None), pl.ds(s * (b // N_CHIPS), b // N_CHIPS), slice(None))]
    return ref.at[lead + (s,)]


def _window_shape(shape3, kind):
    a, b, c = shape3
    return {"col": (a, b, c // N_CHIPS), "row": (a, b // N_CHIPS, c), "slab": (b, c)}[kind]


def _allgather(tensors, kinds, *, name):
    n = len(tensors)

    def body(*refs):
        bufs = refs[n:2 * n]
        send, recv = refs[2 * n:]
        x, y, c, chips = _place()
        me = 2 * x + y
        sib = (x, y, 1 - c)

        def rcopy(i, k, win, to):
            return pltpu.make_async_remote_copy(src_ref=win, dst_ref=win, send_sem=send.at[i * 6 + k], recv_sem=recv.at[i * 6 + k],
                                                device_id=to, device_id_type=MESH)

        started = []
        for i in range(n):
            for k, (px, py) in enumerate(chips):
                cp = rcopy(i, k, _window(bufs[i], kinds[i], me, c), (px, py, c))
                cp.start()
                started.append(cp)
        for i in range(n):
            for k, (px, py) in enumerate(chips):
                landed = _window(bufs[i], kinds[i], 2 * px + py, c)
                rcopy(i, k, landed, (px, py, c)).wait_recv()
                fw = rcopy(i, 3 + k, landed, sib)
                fw.start()
                started.append(fw)
        for i in range(n):
            for k, (px, py) in enumerate(chips):
                rcopy(i, 3 + k, _window(bufs[i], kinds[i], 2 * px + py, 1 - c), sib).wait_recv()
        for cp in started:
            cp.wait_send()

    return pl.pallas_call(
        body, in_specs=[ANY] * n, out_specs=[ANY] * n,
        out_shape=[jax.ShapeDtypeStruct(t.shape, t.dtype) for t in tensors],
        scratch_shapes=[pltpu.SemaphoreType.DMA((6 * n,)), pltpu.SemaphoreType.DMA((6 * n,))],
        input_output_aliases={i: i for i in range(n)},
        name=name,
    )(*tensors)


def _rows_tile(rows, cols, sub):
    best = None
    for t in range(sub, rows + 1, sub):
        if rows % t == 0 and t * cols <= 512 * 1024:
            best = t
    return rows if best is None else best


def _sequencer(name, cid, n_sems, peers_of, body):
    @pl.kernel(mesh=plsc.ScalarSubcoreMesh(axis_name="seq", num_cores=1), name=name,
               scratch_types=(pltpu.SemaphoreType.DMA((n_sems,)), pltpu.SemaphoreType.DMA((n_sems,))),
               compiler_params=pltpu.CompilerParams(collective_id=cid))
    def launch(send, recv):
        x, y, c, chips = _place()
        peers = peers_of(x, y, c, chips)
        barrier = pltpu.get_barrier_semaphore()
        for peer in peers:
            pl.semaphore_signal(barrier, inc=1, device_id=peer, device_id_type=MESH)
        pl.semaphore_wait(barrier, len(peers))
        body(send, recv)

    launch()


def _half_of_full(ref, kind, h):
    if kind == "col":
        b = ref.shape[0]
        return ref.at[pl.ds(h * (b // 2), b // 2), :]
    if kind == "row":
        c = ref.shape[1]
        return ref.at[:, pl.ds(h * (c // 2), c // 2)]
    b = ref.shape[1]
    return ref.at[:, pl.ds(h * (b // 2), b // 2), :]


def _half_shape(full, kind):
    if kind == "col":
        return (full[0] // 2, full[1])
    if kind == "row":
        return (full[0], full[1] // 2)
    return (full[0], full[1] // 2, full[2])


def _win_of_half(ref, kind, s):
    if kind == "col":
        c = ref.shape[1]
        return ref.at[:, pl.ds(s * (c // N_CHIPS), c // N_CHIPS)]
    if kind == "row":
        b = ref.shape[0]
        return ref.at[pl.ds(s * (b // N_CHIPS), b // N_CHIPS), :]
    return ref.at[s]


def _win_shape(half, kind):
    if kind == "col":
        return (half[0], half[1] // N_CHIPS)
    if kind == "row":
        return (half[0] // N_CHIPS, half[1])
    return half[1:]


def _seq_swap(parts, kinds, *, name):
    n = len(parts)
    srcs = [jax.new_ref(p, memory_space=pltpu.MemorySpace.HBM) for p in parts]
    outs = [jax.empty_ref(jax.ShapeDtypeStruct(_half_shape(p.shape, k), p.dtype), memory_space=pltpu.MemorySpace.HBM)
            for p, k in zip(parts, kinds)]

    def body(send, recv):
        x, y, c, _ = _place()
        cps = []
        for i in range(n):
            cp = pltpu.make_async_remote_copy(src_ref=_half_of_full(srcs[i], kinds[i], 1 - c), dst_ref=outs[i], send_sem=send.at[i],
                                              recv_sem=recv.at[i], device_id=(x, y, 1 - c), device_id_type=MESH)
            cp.start()
            cps.append(cp)
        for cp in cps:
            cp.wait()

    _sequencer(name, 2, n, lambda x, y, c, chips: [(x, y, 1 - c)], body)
    return [o[...] for o in outs]


def _seq_scatter(halves, kinds, *, name):
    n = len(halves)
    srcs = [jax.new_ref(h, memory_space=pltpu.MemorySpace.HBM) for h in halves]
    outs = [jax.empty_ref(jax.ShapeDtypeStruct((3,) + _win_shape(h.shape, k), h.dtype), memory_space=pltpu.MemorySpace.HBM)
            for h, k in zip(halves, kinds)]

    def body(send, recv):
        x, y, c, chips = _place()
        cps = []
        for i in range(n):
            for k, (px, py) in enumerate(chips):
                cp = pltpu.make_async_remote_copy(src_ref=_win_of_half(srcs[i], kinds[i], 2 * px + py), dst_ref=outs[i].at[k],
                                                  send_sem=send.at[3 * i + k], recv_sem=recv.at[3 * i + k],
                                                  device_id=(px, py, c), device_id_type=MESH)
                cp.start()
                cps.append(cp)
        for cp in cps:
            cp.wait()

    _sequencer(name, 3, 3 * n, lambda x, y, c, chips: [(px, py, c) for px, py in chips], body)
    return [o[...] for o in outs]


def _add_half(g, p, kind, where, after, *, name):
    if kind == "slab":
        s, b2, c = p.shape
        tr = _rows_tile(b2, c, 16)
        nr = b2 // tr
        grid = (s, nr)
        g_spec = pl.BlockSpec((None, tr, c), lambda i, r, w: (i, w[1] * nr + r, 0))
        p_spec = pl.BlockSpec((None, tr, c), lambda i, r, w: (i, r, 0))
    elif kind == "col":
        b2, c = p.shape
        tr = _rows_tile(b2, c, 16)
        nr = b2 // tr
        grid = (1, nr)
        g_spec = pl.BlockSpec((tr, c), lambda i, r, w: (w[1] * nr + r, 0))
        p_spec = pl.BlockSpec((tr, c), lambda i, r, w: (r, 0))
    else:
        b, c2 = p.shape
        tr = _rows_tile(b, c2, 16)
        grid = (1, b // tr)
        g_spec = pl.BlockSpec((tr, c2), lambda i, r, w: (r, w[1]))
        p_spec = pl.BlockSpec((tr, c2), lambda i, r, w: (r, 0))

    def body(w_ref, g_ref, p_ref, *rest):
        o_ref = rest[-1]
        o_ref[...] = (g_ref[...].astype(F32) + p_ref[...].astype(F32)).astype(o_ref.dtype)

    extra = [] if after is None else [after]
    return pl.pallas_call(
        body,
        grid_spec=pltpu.PrefetchScalarGridSpec(num_scalar_prefetch=1, grid=grid, in_specs=[g_spec, p_spec] + [ANY] * len(extra),
                                               out_specs=p_spec),
        out_shape=jax.ShapeDtypeStruct(p.shape, g.dtype),
        compiler_params=_params(("parallel", "parallel")), name=name,
    )(where, g, p, *extra)


def _sum_chips(r, h, kind, where, layer, layers, out_buf, after, *, name):
    _, br, cr = r.shape
    tr = _rows_tile(br, cr, 16)
    nr = br // tr
    if kind == "col":
        h_spec = pl.BlockSpec((tr, cr), lambda j, w: (j, w[0]))
        o_shape, o_spec = (layers, 2 * br, cr), pl.BlockSpec((None, tr, cr), lambda j, w: (layer, w[1] * nr + j, 0))
    elif kind == "row":
        h_spec = pl.BlockSpec((tr, cr), lambda j, w: (w[0] * nr + j, 0))
        o_shape, o_spec = (layers, br, 2 * cr), pl.BlockSpec((None, tr, cr), lambda j, w: (layer, j, w[1]))
    else:
        h_spec = pl.BlockSpec((None, tr, cr), lambda j, w: (w[0], j, 0))
        o_shape, o_spec = (layers, 2 * br, cr), pl.BlockSpec((None, tr, cr), lambda j, w: (layer, w[1] * nr + j, 0))

    def body(w_ref, h_ref, r0_ref, r1_ref, r2_ref, *rest):
        o_ref, t_ref = rest[-2], rest[-1]
        o_ref[...] = ((h_ref[...].astype(F32) + r0_ref[...].astype(F32)) + r1_ref[...].astype(F32)) + r2_ref[...].astype(F32)
        t_ref[...] = jnp.zeros_like(t_ref)

    def slot(k):
        return pl.BlockSpec((None, tr, cr), lambda j, w: (k, j, 0))

    ins, specs, alias = [h, r, r, r], [h_spec, slot(0), slot(1), slot(2)], {}
    if after is not None:
        ins.append(after)
        specs.append(ANY)
    if out_buf is not None:
        alias = {1 + len(ins): 0}
        ins.append(out_buf)
        specs.append(ANY)
    return pl.pallas_call(
        body,
        grid_spec=pltpu.PrefetchScalarGridSpec(num_scalar_prefetch=1, grid=(nr,), in_specs=specs,
                                               out_specs=[o_spec, pl.BlockSpec((8, 128), lambda j, w: (0, 0))]),
        out_shape=[jax.ShapeDtypeStruct(o_shape, F32), jax.ShapeDtypeStruct((8, 128), F32)], input_output_aliases=alias,
        compiler_params=_params(("arbitrary",)), name=name,
    )(where, *ins)


def _join_halves(tensors, kinds, *, name):
    n = len(tensors)

    def mine(ref, kind, h):
        if kind == "row":
            c = ref.shape[2]
            return ref.at[:, :, pl.ds(h * (c // 2), c // 2)]
        b = ref.shape[1]
        return ref.at[:, pl.ds(h * (b // 2), b // 2), :]

    def body(*refs):
        bufs = refs[n:2 * n]
        send, recv = refs[2 * n:]
        x, y, c, _ = _place()
        cps = []
        for i in range(n):
            part = mine(bufs[i], kinds[i], c)
            cp = pltpu.make_async_remote_copy(src_ref=part, dst_ref=part, send_sem=send.at[i],
                                              recv_sem=recv.at[i], device_id=(x, y, 1 - c), device_id_type=MESH)
            cp.start()
            cps.append(cp)
        for i in range(n):
            other = mine(bufs[i], kinds[i], 1 - c)
            pltpu.make_async_remote_copy(src_ref=other, dst_ref=other, send_sem=send.at[i],
                                         recv_sem=recv.at[i], device_id=(x, y, 1 - c), device_id_type=MESH).wait_recv()
        for cp in cps:
            cp.wait_send()

    return pl.pallas_call(
        body, in_specs=[ANY] * n, out_specs=[ANY] * n,
        out_shape=[jax.ShapeDtypeStruct(t.shape, t.dtype) for t in tensors],
        scratch_shapes=[pltpu.SemaphoreType.DMA((n,)), pltpu.SemaphoreType.DMA((n,))],
        input_output_aliases={i: i for i in range(n)},
        name=name,
    )(*tensors)


def _win(ref, kind, s, h=None):
    if kind == "col":
        b, c = ref.shape
        cols = pl.ds(s * (c // N_CHIPS), c // N_CHIPS)
        return ref.at[:, cols] if h is None else ref.at[pl.ds(h * (b // 2), b // 2), cols]
    if kind == "row":
        b, c = ref.shape
        rows = pl.ds(s * (b // N_CHIPS), b // N_CHIPS)
        return ref.at[rows, :] if h is None else ref.at[rows, pl.ds(h * (c // 2), c // 2)]
    b = ref.shape[1]
    return ref.at[s] if h is None else ref.at[s, pl.ds(h * (b // 2), b // 2)]


def _half(ref, kind, h):
    b, c = ref.shape
    if kind == "row":
        return ref.at[:, pl.ds(h * (c // 2), c // 2)]
    return ref.at[pl.ds(h * (b // 2), b // 2), :]


def _full_shape(shard_shape, kind):
    b, c = shard_shape
    return {"col": (b, N_CHIPS * c), "row": (N_CHIPS * b, c), "slab": (N_CHIPS, b, c)}[kind]


def _gather_body(srcs, outs, kinds, send, recv):
    x, y, c, chips = _place()
    me = 2 * x + y
    sib = (x, y, 1 - c)

    def rcopy(i, k, src, dst, to):
        return pltpu.make_async_remote_copy(src_ref=src, dst_ref=dst, send_sem=send.at[7 * i + k], recv_sem=recv.at[7 * i + k],
                                            device_id=to, device_id_type=MESH)

    started = []
    for i, (src, out, kind) in enumerate(zip(srcs, outs, kinds)):
        own = rcopy(i, 6, src, _win(out, kind, me), sib)
        own.start()
        started.append(own)
        for k, (px, py) in enumerate(chips):
            cp = rcopy(i, k, _half(src, kind, c), _win(out, kind, me, c), (px, py, c))
            cp.start()
            started.append(cp)
    for i, (out, kind) in enumerate(zip(outs, kinds)):
        for k, (px, py) in enumerate(chips):
            landed = _win(out, kind, 2 * px + py, c)
            rcopy(i, k, landed, landed, (px, py, c)).wait_recv()
            fw = rcopy(i, 3 + k, landed, landed, sib)
            fw.start()
            started.append(fw)
    for i, (src, out, kind) in enumerate(zip(srcs, outs, kinds)):
        for k, (px, py) in enumerate(chips):
            other = _win(out, kind, 2 * px + py, 1 - c)
            rcopy(i, 3 + k, other, other, sib).wait_recv()
        rcopy(i, 6, src, _win(out, kind, me), sib).wait_recv()
    for cp in started:
        cp.wait_send()


def _seq_gather(shards, kinds, *, name, cid):
    n = len(shards)
    srcs = [jax.new_ref(s, memory_space=pltpu.MemorySpace.HBM) for s in shards]
    outs = [jax.empty_ref(jax.ShapeDtypeStruct(_full_shape(s.shape, k), s.dtype), memory_space=pltpu.MemorySpace.HBM)
            for s, k in zip(shards, kinds)]

    @pl.kernel(mesh=plsc.ScalarSubcoreMesh(axis_name="seq", num_cores=1), name=name,
               scratch_types=(pltpu.SemaphoreType.DMA((7 * n,)), pltpu.SemaphoreType.DMA((7 * n,))),
               compiler_params=pltpu.CompilerParams(collective_id=cid))
    def launch(send, recv):
        x, y, c, chips = _place()
        barrier = pltpu.get_barrier_semaphore()
        for px, py in chips:
            pl.semaphore_signal(barrier, inc=1, device_id=(px, py, c), device_id_type=MESH)
        pl.semaphore_signal(barrier, inc=1, device_id=(x, y, 1 - c), device_id_type=MESH)
        pl.semaphore_wait(barrier, 4)
        _gather_body(srcs, outs, kinds, send, recv)

    launch()
    return [o[...] for o in outs]


KIND = dict(w_qkv_a="slab", w_o_a="col", w_q_b="row", w_o_b="row", w_kvf="slab", w_up="col", w_down="row", small="slab")
LAYERS = dict(w_qkv_a=N_A, w_o_a=N_A, w_q_b=DEPTH - N_A, w_o_b=DEPTH - N_A, w_kvf=1, w_up=DEPTH, w_down=DEPTH, small=1)
SMALL_W = 1792
SMALL_ROWS = 8


class _Reducer:
    def __init__(self, where):
        self.where = where
        self.acc = {nm: None for nm in KIND}
        self.pending = None

    def __call__(self, group, tag):
        names, layers, parts = zip(*group)
        kinds = [KIND[nm] for nm in names]
        summed = self._sum_pending(after=parts[-1])
        sib = _seq_swap(list(parts), kinds, name="reduce_swap_" + tag)
        halves = []
        for g, p, k, nm in zip(parts, sib, kinds, names):
            halves.append(_add_half(g, p, k, self.where, halves[-1] if halves else None, name="reduce_add_" + nm))
        landed = _seq_scatter(halves, kinds, name="reduce_scatter_" + tag)
        self.pending = (names, layers, landed, halves, kinds)
        return [halves[-1], summed]

    def _sum_pending(self, after):
        if self.pending is None:
            return None
        for nm, l, r, h, k in zip(*self.pending):
            self.acc[nm], after = _sum_chips(r, h, k, self.where, l, LAYERS[nm], self.acc[nm], after, name="reduce_sum_" + nm)
        self.pending = None
        return after

    def finish(self):
        self._sum_pending(after=None)
        names = list(KIND)
        joined = _join_halves([self.acc[nm] for nm in names], [KIND[nm] for nm in names], name="reduce_pair_join")
        return dict(zip(names, joined))


def _headsum_matrix():
    r = lax.broadcasted_iota(jnp.int32, (128, 128), 0) // HD
    c = lax.broadcasted_iota(jnp.int32, (128, 128), 1) // HD
    return jnp.where(r == c, 1.0, 0.0).astype(BF16)


def kernel(x, norm_gains, w_qkv_a, w_o_a, w_q_b, w_o_b, kv_norm, w_kvf, b_f, w_up, conv_w, conv_b, w_down, loss_target, m_norm_gains, m_w_qkv_a, m_w_o_a, m_w_q_b, m_w_o_b, m_kv_norm, m_w_kvf, m_b_f, m_w_up, m_conv_w, m_conv_b, m_w_down, v_norm_gains, v_w_qkv_a, v_w_o_a, v_w_q_b, v_w_o_b, v_kv_norm, v_w_kvf, v_b_f, v_w_up, v_conv_w, v_conv_b, v_w_down):
    xi, yi, ci = lax.axis_index("x"), lax.axis_index("y"), lax.axis_index("c")
    chip = 2 * xi + yi
    where = jnp.stack([chip, ci]).astype(jnp.int32)
    ws = dict(norm_gains=norm_gains, w_qkv_a=w_qkv_a, w_o_a=w_o_a, w_q_b=w_q_b, w_o_b=w_o_b, kv_norm=kv_norm, w_kvf=w_kvf,
              b_f=b_f, w_up=w_up, conv_w=conv_w, conv_b=conv_b, w_down=w_down)
    ms = dict(norm_gains=m_norm_gains, w_qkv_a=m_w_qkv_a, w_o_a=m_w_o_a, w_q_b=m_w_q_b, w_o_b=m_w_o_b, kv_norm=m_kv_norm,
              w_kvf=m_w_kvf, b_f=m_b_f, w_up=m_w_up, conv_w=m_conv_w, conv_b=m_conv_b, w_down=m_w_down)
    vs = dict(norm_gains=v_norm_gains, w_qkv_a=v_w_qkv_a, w_o_a=v_w_o_a, w_q_b=v_w_q_b, w_o_b=v_w_o_b, kv_norm=v_kv_norm,
              w_kvf=v_w_kvf, b_f=v_b_f, w_up=v_w_up, conv_w=v_conv_w, conv_b=v_conv_b, w_down=v_w_down)

    small = jnp.concatenate([
        jnp.pad(norm_gains.reshape(16, 256), ((0, 0), (0, 1408 - 256))),
        jnp.pad(conv_w.reshape(12, 1408), ((0, 4), (0, 0)))], axis=0)
    big = [nm for nm in KIND if nm != "small"]
    half = {nm: ws[nm].astype(BF16) for nm in big}
    W = {nm: [None] * LAYERS[nm] for nm in big if nm != "w_kvf"}
    g_small = None
    groups = [("0a", [("w_qkv_a", 0), ("w_o_a", 0), ("small", 0)]), ("0b", [("w_up", 0), ("w_down", 0)]),
              ("1", [("w_qkv_a", 1), ("w_o_a", 1), ("w_up", 1), ("w_down", 1)]),
              ("2", [("w_kvf", 0), ("w_q_b", 0), ("w_o_b", 0), ("w_up", 2), ("w_down", 2)]),
              ("3", [("w_q_b", 1), ("w_o_b", 1), ("w_up", 3), ("w_down", 3)])]
    for tag, group in groups:
        shards = [small if nm == "small" else half[nm] if nm == "w_kvf" else half[nm][i] for nm, i in group]
        got = _seq_gather(shards, [KIND[nm] for nm, _ in group], name="gather_layer" + tag, cid=1)
        for (nm, i), g in zip(group, got):
            if nm == "small":
                g_small = g
            elif nm == "w_kvf":
                W[nm] = g.transpose(1, 0, 2).reshape(D, 2 * D + 16)
            else:
                W[nm][i] = g.transpose(1, 0, 2).reshape(D, 3 * A_W) if nm == "w_qkv_a" else g
    gains = g_small[:, :16, :256].transpose(1, 0, 2).reshape(DEPTH, 4, 1, D)
    cw_full = g_small[:, 16:28, :].transpose(1, 0, 2).reshape(DEPTH, 3, 2 * D_FF)
    cb_full = conv_b.reshape(DEPTH, 1, 2 * D_FF)

    reducer = _Reducer(where)
    sq, dh = _fwd_bwd(x[0], loss_target[0], W, gains, cw_full, cb_full, kv_norm, b_f, reducer)
    loss = lax.psum(sq[0, 0] * (0.5 / D), ("x", "y", "c"))
    return _update(loss, dh[None], reducer.finish(), chip, ws, ms, vs)


def _fwd_bwd(h, target, W, gains, cw_full, cb_full, kv_norm, b_f, reduce):
    w_kv = W["w_kvf"][:, :2 * D]
    w_kvf_pad = jnp.pad(W["w_kvf"], ((0, 0), (0, 128 - 16)))
    w_f = w_kvf_pad[:, 2 * D:]
    kvn_g = kv_norm.reshape(1, D)
    bf_pad = jnp.pad(b_f, (0, 128 - 16)).reshape(1, 128)
    tabs = _rope_tables()
    headsum = _headsum_matrix()

    saved = []
    kv = zf = c_col = c_row = kvn = h_kv = None
    xn = _rms_fwd(h, gains[0][0], out_dtype=BF16, name="rms_in")
    for l in range(DEPTH):
        s = {"h": h}
        g = gains[l]
        s["xn"] = xn
        if l < N_A:
            qkv = _matmul(xn, W["w_qkv_a"][l], mode="nn", out_dtype=F32, name="mm_qkv", mnk=(T, 3 * A_W, D), tn=768)
            qkvp = _rope_fwd(qkv, tabs).reshape(9, 2, T, 128)
            o_p, lse_p = _band_fwd(qkvp)
            att, o3, lse3 = _combine_fwd(o_p, lse_p)
            s.update(qkvp=qkvp, o3=o3, lse3=lse3, lse_p=lse_p, att=att)
            mix = _matmul(att, W["w_o_a"][l], mode="nn", out_dtype=F32, name="mm_oa", mnk=(T, D, A_W))
        else:
            j = l - N_A
            if l == N_A:
                h_kv = h
                kvn = _rms_fwd(h, kvn_g, out_dtype=BF16, name="rms_in")
                kv = _matmul(kvn, w_kv, mode="nn", out_dtype=BF16, name="mm_kv")
                zf = _matmul(kvn, w_f, mode="nn", out_dtype=F32, name="mm_f")
                cum = _gates_fwd(zf, bf_pad)[:, :16]
                c_col = cum.reshape(T, 8, 2).transpose(1, 0, 2)
                c_row = cum.T.reshape(8, 2, T)
            q = _matmul(xn, W["w_q_b"][j], mode="nn", out_dtype=BF16, name="mm_qb", mnk=(T, D, D), alpha=HD ** -0.5)
            o = _fox_fwd(q, kv, c_col, c_row)
            s.update(q=q, o=o)
            mix = _matmul(o, W["w_o_b"][j], mode="nn", out_dtype=F32, name="mm_ob", mnk=(T, D, D))
        s["mix"] = mix
        h1, xn2 = _rms_res_in(mix, g[1], h, g[2], name="rms_res_in")
        a = _matmul(xn2, W["w_up"][l], mode="nn", out_dtype=F32, name="mm_up", mnk=(T, 2 * D_FF, D))
        u = _convgate_fwd(a, cw_full[l], cb_full[l])
        f = _matmul(u, W["w_down"][l], mode="nn", out_dtype=F32, name="mm_down", mnk=(T, D, D_FF), tm=1024, tk=D_FF)
        if l + 1 < DEPTH:
            h, xn = _rms_res_in(f, g[3], h1, gains[l + 1][0], name="rms_res_in")
        else:
            h = _rms_fwd(f, g[3], res=h1, out_dtype=F32, name="rms_res")
        s.update(h1=h1, xn2=xn2, a=a, u=u, f=f)
        saved.append(s)

    dh, sq = _loss_head(h, target)

    d_gains = [[None] * 4 for _ in range(DEPTH)]
    d_cw, d_cb = [None] * DEPTH, [None] * DEPTH
    zeros_td = jnp.zeros((T, D), F32)
    fox_acc = (zeros_td, zeros_td, jnp.zeros((D // 128, T, 128), F32), jnp.zeros((D // 128, 8, T), F32))
    d_kvnorm = d_bf = token = df = None

    def dw(nm, a, b, **kw):
        return _matmul(a, b, mode="tn", out_dtype=BF16, name="mm_dw_" + nm, **kw)

    def slabs(full, width):
        return full.reshape(full.shape[0], N_CHIPS, width).transpose(1, 0, 2)

    for l in reversed(range(DEPTH)):
        s = saved[l]
        g = gains[l]
        if df is None:
            df, d_gains[l][3] = _rms_bwd(dh, s["f"], g[3], out_dtype=BF16, name="rms_bwd")
        du = _matmul(df, W["w_down"][l], mode="nt", out_dtype=F32, name="mm_down_dx", mnk=(T, D_FF, D), tn=256, after=token)
        g_down = dw("w_down", s["u"], df, tm=1408, tn=1024)
        da, d_cw[l], d_cb[l] = _convgate_bwd(s["a"], du, cw_full[l], cb_full[l])
        dxn2 = _matmul(da, W["w_up"][l], mode="nt", out_dtype=F32, name="mm_up_dx", mnk=(T, D, 2 * D_FF), tm=1024, tn=1024, tk=1408,
                       a_map=_halves_a)
        g_up = dw("w_up", s["xn2"], da, mnk=(D, 2 * D_FF, T), tn=1408, b_map=_halves_b)
        token = reduce([("w_down", l, g_down), ("w_up", l, g_up)], "ffn%d" % l)
        dh1, dmix, d_gains[l][2], d_gains[l][1] = _rms_bwd2(dxn2, s["h1"], g[2], dh, s["mix"], g[1], name="rms_bwd2")
        if l < N_A:
            datt = _matmul(dmix, W["w_o_a"][l], mode="nt", out_dtype=F32, name="mm_oa_dx", mnk=(T, A_W, D), tn=768, after=token)
            g_o = dw("w_o_a", s["att"], dmix, tm=768, tn=1024)
            do_p, dlt_p = _combine_bwd(datt, s["o3"], s["lse3"], headsum)
            dqkv = None
            for which, d in enumerate(_band_bwd(s["qkvp"], do_p, s["lse_p"], dlt_p)):
                dqkv = _rope_bwd(d, which, tabs, dqkv)
            dxn = _matmul(dqkv, W["w_qkv_a"][l], mode="nt", out_dtype=F32, name="mm_qkv_dx", mnk=(T, D, 3 * A_W), tm=1024, tn=1024, tk=3 * A_W)
            g_qkv = dw("w_qkv_a", s["xn"], dqkv, tn=768)
            group = [("w_o_a", l, g_o), ("w_qkv_a", l, slabs(g_qkv, 576))]
        else:
            j = l - N_A
            do = _matmul(dmix, W["w_o_b"][j], mode="nt", out_dtype=BF16, name="mm_ob_dx", mnk=(T, D, D), after=token)
            g_o = dw("w_o_b", s["o"], dmix, tn=1024)
            dq, *fox_acc = _fox_bwd(s["q"], kv, do, c_col, c_row, fox_acc)
            dxn = _matmul(dq, W["w_q_b"][j], mode="nt", out_dtype=F32, name="mm_qb_dx", mnk=(T, D, D))
            g_q = dw("w_q_b", s["xn"], dq, tn=1024)
            group = [("w_o_b", j, g_o), ("w_q_b", j, g_q)]
        if l > 0 and l != N_A:
            dh, df, d_gains[l][0], d_gains[l - 1][3] = _rms_bwd2(dxn, s["h"], g[0], dh1, saved[l - 1]["f"], gains[l - 1][3],
                                                                 name="rms_bwd2")
        else:
            dh, d_gains[l][0] = _rms_bwd(dxn, s["h"], g[0], dres=dh1, out_dtype=F32, name="rms_bwd_res")
            df = None
        if l == N_A:
            dk, dv, dcq, dck = fox_acc
            dc16 = dcq[:, :, :2].transpose(1, 0, 2).reshape(T, 16) - dck[:, :2, :].reshape(16, T).T
            dzf, d_bf = _gates_bwd(jnp.pad(dc16, ((0, 0), (0, 128 - 16))), zf, bf_pad)
            dkvf = jnp.concatenate([dk.astype(BF16), dv.astype(BF16), dzf], axis=1)
            g_kvf = _matmul(kvn, dkvf, mode="tn", out_dtype=BF16, name="mm_kvf_dw", tm=512, tn=2 * D + 128)[:, :2 * D + 16]
            dkvn = _matmul(dkvf, w_kvf_pad, mode="nt", out_dtype=F32, name="mm_kvf_dx", tm=1024, tn=1024, tk=2 * D + 128)
            dh, d_kvnorm = _rms_bwd(dkvn, h_kv, kvn_g, dres=dh, out_dtype=F32, name="rms_bwd_res")
            group.append(("w_kvf", 0, slabs(g_kvf, 516)))
        token = reduce(group, "mix%d" % l)
    small_flat = jnp.concatenate([
        jnp.stack([jnp.stack(r) for r in d_gains]).reshape(-1),
        jnp.stack(d_cw).transpose(0, 2, 1, 3).reshape(-1),
        jnp.stack(d_cb).reshape(-1),
        d_kvnorm.reshape(-1), d_bf[0, :16]])
    small = jnp.pad(small_flat, (0, 2 * N_CHIPS * SMALL_ROWS * SMALL_W - small_flat.shape[0]))
    reduce([("small", 0, small.reshape(N_CHIPS, 2 * SMALL_ROWS, SMALL_W))], "small")
    return sq, dh


def _update(loss, grad_x, reduced, chip, ws, ms, vs):
    red_s = reduced.pop("small")
    buf_s = lax.dynamic_update_slice(jnp.zeros((2, N_CHIPS, SMALL_ROWS, SMALL_W), F32), red_s.reshape(2, 1, SMALL_ROWS, SMALL_W),
                                     (0, chip, 0, 0))
    (all_s,) = _allgather([buf_s], ["slab"], name="gather_small_grads")
    sflat = all_s.transpose(1, 0, 2, 3).reshape(-1)

    grads = {nm: r.reshape(ws[nm].shape) for nm, r in reduced.items()}
    o = 0
    g_gains_full = sflat[o:o + 16 * D].reshape(DEPTH, 4, D); o += 16 * D
    g_cw_full = sflat[o:o + 12 * 2 * D_FF].reshape(DEPTH, 3, 2 * D_FF); o += 12 * 2 * D_FF
    grads["conv_b"] = sflat[o:o + 4 * 2 * D_FF].reshape(DEPTH, 2 * D_FF); o += 4 * 2 * D_FF
    grads["kv_norm"] = sflat[o:o + D]; o += D
    grads["b_f"] = sflat[o:o + 16]
    grads["norm_gains"] = lax.dynamic_slice_in_dim(g_gains_full, chip * 256, 256, axis=2)
    grads["conv_w"] = lax.dynamic_slice_in_dim(g_cw_full, chip * 1408, 1408, axis=2)

    names = ["norm_gains", "w_qkv_a", "w_o_a", "w_q_b", "w_o_b", "kv_norm", "w_kvf", "b_f", "w_up", "conv_w", "conv_b", "w_down"]
    deltas, new_m, new_v = {}, {}, {}
    for nm in names:
        shp = ws[nm].shape
        two = (math.prod(shp[:-1]), shp[-1]) if len(shp) > 1 else (1, shp[0])
        d, m2, v2 = _adamw(ws[nm].reshape(two), ms[nm].reshape(two), vs[nm].reshape(two), grads[nm].reshape(two),
                           name="adamw_" + nm)
        deltas[nm], new_m[nm], new_v[nm] = d.reshape(shp), m2.reshape(shp), v2.reshape(shp)

    return (loss, grad_x, *[grads[nm] for nm in names], *[deltas[nm] for nm in names],
            *[new_m[nm] for nm in names], *[new_v[nm] for nm in names])
```

```python
import math

import jax
import jax.numpy as jnp
from jax import lax
from jax.experimental import pallas as pl
from jax.experimental.pallas import tpu as pltpu
from jax.experimental.pallas import tpu_sc as plsc

F32 = jnp.float32
BF16 = jnp.bfloat16
MESH = pl.DeviceIdType.MESH
ANY = pl.BlockSpec(memory_space=pl.ANY)

T = 2048
D = 1024
HD = 64
DEPTH = 4
N_A = 2
A_W = 768
GW = 256
DIL = (1, 4, 16)
BLK = 128
D_FF = 2816
ROPE_THETA = 500000.0
EPS = 1e-6
NEG = -1e30
N_CHIPS = 4
FQ = 256
CT = 128
VMEM_BIG = 48 * 1024 * 1024

ADAM_LR, ADAM_B1, ADAM_B2, ADAM_EPS, ADAM_WD, ADAM_STEP = 0.001, 0.9, 0.999, 1e-08, 0.01, 10

NN = (((1,), (0,)), ((), ()))
NT = (((1,), (1,)), ((), ()))
TN = (((0,), (0,)), ((), ()))


def _dot(a, b, dims):
    return lax.dot_general(a, b, dims, preferred_element_type=F32)


def _pick(dim, pref):
    if dim <= pref:
        return dim
    best = None
    for t in range(128, pref + 1, 128):
        if dim % t == 0:
            best = t
    assert best is not None, (dim, pref)
    return best


def _params(sem=None, vmem=None):
    kw = {}
    if sem is not None:
        kw["dimension_semantics"] = sem
    if vmem is not None:
        kw["vmem_limit_bytes"] = vmem
    return pltpu.CompilerParams(**kw)


def _matmul(a, b, *, mode, out_dtype, name, mnk=None, alpha=None, tm=2048, tn=512, tk=2048,
            a_map=None, b_map=None, acc_init=None, out_slab=None, out_slabs=None, out_buf=None, after=None):
    if mnk is not None:
        M, N, K = mnk
    elif mode == "nn":
        (M, K), (_, N) = a.shape, b.shape
    elif mode == "nt":
        (M, K), (N, _) = a.shape, b.shape
    else:
        (K, M), (_, N) = a.shape, b.shape
    tm, tn, tk = _pick(M, tm), _pick(N, tn), _pick(K, tk)
    nk = K // tk
    dims = {"nn": NN, "nt": NT, "tn": TN}[mode]
    after = [t for t in (after or ()) if t is not None]
    n_in = 2 + (acc_init is not None) + len(after) + (out_buf is not None)

    def body(*refs):
        a_ref, b_ref = refs[0], refs[1]
        o_ref = refs[n_in]
        k = pl.program_id(2)

        def finish(r):
            if alpha is not None:
                r = r * alpha
            o_ref[...] = r.astype(out_dtype)

        def product():
            r = _dot(a_ref[...], b_ref[...], dims)
            return r if acc_init is None else r + refs[2][...]

        if nk == 1:
            finish(product())
            return
        acc_ref = refs[n_in + 1]

        @pl.when(k == 0)
        def _():
            acc_ref[...] = product()

        @pl.when((k > 0) & (k < nk - 1))
        def _():
            acc_ref[...] += _dot(a_ref[...], b_ref[...], dims)

        @pl.when(k == nk - 1)
        def _():
            finish(acc_ref[...] + _dot(a_ref[...], b_ref[...], dims))

    a_blk = (tk, tm) if mode == "tn" else (tm, tk)
    b_blk = (tn, tk) if mode == "nt" else (tk, tn)
    if a_map is not None:
        a_spec = pl.BlockSpec((None,) + a_blk, a_map(tm, tn, tk))
    elif mode == "tn":
        a_spec = pl.BlockSpec(a_blk, lambda i, j, k: (k, i))
    else:
        a_spec = pl.BlockSpec(a_blk, lambda i, j, k: (i, k))
    if b_map is not None:
        b_spec = pl.BlockSpec((None,) + b_blk, b_map(tm, tn, tk))
    elif mode == "nt":
        b_spec = pl.BlockSpec(b_blk, lambda i, j, k: (j, k))
    else:
        b_spec = pl.BlockSpec(b_blk, lambda i, j, k: (k, j))
    ins, specs, alias = [a, b], [a_spec, b_spec], {}
    if acc_init is not None:
        ins.append(acc_init)
        specs.append(pl.BlockSpec((tm, tn), lambda i, j, k: (i, j)))
    ins += after
    specs += [ANY] * len(after)
    if out_buf is not None:
        alias = {len(ins): 0}
        ins.append(out_buf)
        specs.append(ANY)
    if out_slab is None:
        o_spec = pl.BlockSpec((tm, tn), lambda i, j, k: (i, j))
        o_shape = jax.ShapeDtypeStruct((M, N), out_dtype)
    else:
        o_spec = pl.BlockSpec((None, tm, tn), lambda i, j, k: (out_slab, i, j))
        o_shape = jax.ShapeDtypeStruct((out_slabs, M, N), out_dtype)
    return pl.pallas_call(
        body,
        grid=(M // tm, N // tn, nk),
        in_specs=specs,
        out_specs=o_spec,
        out_shape=o_shape,
        scratch_shapes=[pltpu.VMEM((tm, tn), F32)] if nk > 1 else [],
        input_output_aliases=alias,
        compiler_params=_params(("parallel", "parallel", "arbitrary"), VMEM_BIG),
        name=name,
    )(*ins)


def _slab(l, mode):
    if mode == "nt":
        return lambda tm, tn, tk: (lambda i, j, k: (l, j, k))
    return lambda tm, tn, tk: (lambda i, j, k: (l, k, j))


def _rms_fwd(x, g, *, out_dtype, name, res=None, tr=256):
    n, d = x.shape

    def body(*refs):
        x_ref, g_ref = refs[0], refs[1]
        o_ref = refs[-1]
        xv = x_ref[...].astype(F32)
        y = xv * lax.rsqrt(jnp.mean(xv * xv, axis=-1, keepdims=True) + EPS) * g_ref[...]
        if res is not None:
            y = y + refs[2][...]
        o_ref[...] = y.astype(out_dtype)

    row = pl.BlockSpec((tr, d), lambda i: (i, 0))
    vec = pl.BlockSpec((1, d), lambda i: (0, 0))
    ins = [x, g] + ([] if res is None else [res])
    specs = [row, vec] + ([] if res is None else [row])
    return pl.pallas_call(
        body, grid=(n // tr,), in_specs=specs, out_specs=row,
        out_shape=jax.ShapeDtypeStruct((n, d), out_dtype),
        compiler_params=_params(("parallel",)), name=name,
    )(*ins)


def _rms_bwd(dy, x, g, *, out_dtype, name, dres=None, tr=256):
    n, d = x.shape

    def body(*refs):
        dy_ref, x_ref, g_ref = refs[0], refs[1], refs[2]
        dx_ref, dg_ref = refs[-2], refs[-1]
        xv = x_ref[...].astype(F32)
        dyv = dy_ref[...].astype(F32)
        rstd = lax.rsqrt(jnp.mean(xv * xv, axis=-1, keepdims=True) + EPS)
        xhat = xv * rstd
        dxh = dyv * g_ref[...]
        dx = rstd * (dxh - xhat * jnp.mean(dxh * xhat, axis=-1, keepdims=True))
        if dres is not None:
            dx = dx + refs[3][...]
        dx_ref[...] = dx.astype(out_dtype)

        @pl.when(pl.program_id(0) == 0)
        def _():
            dg_ref[...] = jnp.zeros_like(dg_ref)

        dg_ref[...] += jnp.sum(dyv * xhat, axis=0, keepdims=True)

    row = pl.BlockSpec((tr, d), lambda i: (i, 0))
    vec = pl.BlockSpec((1, d), lambda i: (0, 0))
    ins = [dy, x, g] + ([] if dres is None else [dres])
    specs = [row, row, vec] + ([] if dres is None else [row])
    return pl.pallas_call(
        body, grid=(n // tr,), in_specs=specs, out_specs=[row, vec],
        out_shape=[jax.ShapeDtypeStruct((n, d), out_dtype), jax.ShapeDtypeStruct((1, d), F32)],
        compiler_params=_params(("arbitrary",)), name=name,
    )(*ins)


def _rms_res_in(x, g_res, res, g_in, *, name, tr=256):
    n, d = x.shape

    def body(x_ref, gr_ref, r_ref, gi_ref, h_ref, n_ref):
        xv = x_ref[...].astype(F32)
        h = r_ref[...] + xv * lax.rsqrt(jnp.mean(xv * xv, axis=-1, keepdims=True) + EPS) * gr_ref[...]
        h_ref[...] = h
        n_ref[...] = (h * lax.rsqrt(jnp.mean(h * h, axis=-1, keepdims=True) + EPS) * gi_ref[...]).astype(BF16)

    row = pl.BlockSpec((tr, d), lambda i: (i, 0))
    vec = pl.BlockSpec((1, d), lambda i: (0, 0))
    return pl.pallas_call(
        body, grid=(n // tr,), in_specs=[row, vec, row, vec], out_specs=[row, row],
        out_shape=[jax.ShapeDtypeStruct((n, d), F32), jax.ShapeDtypeStruct((n, d), BF16)],
        compiler_params=_params(("parallel",)), name=name,
    )(x, g_res, res, g_in)


def _rms_bwd2(dy, x, g, dres, x2, g2, *, name, tr=256):
    n, d = x.shape

    def one(dyv, xv, gv):
        rstd = lax.rsqrt(jnp.mean(xv * xv, axis=-1, keepdims=True) + EPS)
        xhat = xv * rstd
        dxh = dyv * gv
        return rstd * (dxh - xhat * jnp.mean(dxh * xhat, axis=-1, keepdims=True)), jnp.sum(dyv * xhat, axis=0, keepdims=True)

    def body(dy_ref, x_ref, g_ref, r_ref, x2_ref, g2_ref, dx_ref, d2_ref, dg_ref, dg2_ref):
        dx, dg = one(dy_ref[...].astype(F32), x_ref[...].astype(F32), g_ref[...])
        dx = dx + r_ref[...]
        dx_ref[...] = dx
        d2, dg2 = one(dx, x2_ref[...].astype(F32), g2_ref[...])
        d2_ref[...] = d2.astype(BF16)

        @pl.when(pl.program_id(0) == 0)
        def _():
            dg_ref[...] = jnp.zeros_like(dg_ref)
            dg2_ref[...] = jnp.zeros_like(dg2_ref)

        dg_ref[...] += dg
        dg2_ref[...] += dg2

    row = pl.BlockSpec((tr, d), lambda i: (i, 0))
    vec = pl.BlockSpec((1, d), lambda i: (0, 0))
    return pl.pallas_call(
        body, grid=(n // tr,), in_specs=[row, row, vec, row, row, vec], out_specs=[row, row, vec, vec],
        out_shape=[jax.ShapeDtypeStruct((n, d), F32), jax.ShapeDtypeStruct((n, d), BF16),
                   jax.ShapeDtypeStruct((1, d), F32), jax.ShapeDtypeStruct((1, d), F32)],
        compiler_params=_params(("arbitrary",)), name=name,
    )(dy, x, g, dres, x2, g2)


def _loss_head(h, target, *, tr=256):
    n, d = h.shape

    def body(h_ref, t_ref, dh_ref, s_ref):
        err = h_ref[...] - t_ref[...]
        dh_ref[...] = err * (1.0 / d)

        @pl.when(pl.program_id(0) == 0)
        def _():
            s_ref[...] = jnp.zeros_like(s_ref)

        s_ref[...] += jnp.sum(err * err)

    row = pl.BlockSpec((tr, d), lambda i: (i, 0))
    acc = pl.BlockSpec((8, 128), lambda i: (0, 0))
    return pl.pallas_call(
        body, grid=(n // tr,), in_specs=[row, row], out_specs=[row, acc],
        out_shape=[jax.ShapeDtypeStruct((n, d), F32), jax.ShapeDtypeStruct((8, 128), F32)],
        compiler_params=_params(("arbitrary",)), name="loss_head",
    )(h, target)


def _rope_tables():
    pos = jnp.arange(T, dtype=F32)
    inv = ROPE_THETA ** (-jnp.arange(0, 16, 2, dtype=F32) / 16)
    ang = pos[:, None] * inv[None, :]
    cos, sin = jnp.cos(ang), jnp.sin(ang)
    one = jnp.ones((T, HD - 16), F32)
    zero8 = jnp.zeros((T, 8), F32)
    zero = jnp.zeros((T, HD - 16), F32)
    c = jnp.concatenate([cos, cos, one], axis=1)
    s1 = jnp.concatenate([zero8, sin, zero], axis=1)
    s2 = jnp.concatenate([-sin, zero8, zero], axis=1)
    c, s1, s2 = (jnp.concatenate([t, t], axis=1) for t in (c, s1, s2))
    scale = HD ** -0.5
    return (jnp.stack([c * scale, c, jnp.ones_like(c)]), jnp.stack([s1 * scale, s1, jnp.zeros_like(c)]),
            jnp.stack([s2 * scale, s2, jnp.zeros_like(c)]))


def _row_chunks(r):
    if r == 1:
        n = 4
        return [(slice(i * (T // n), (i + 1) * (T // n)),) * 2 for i in range(n)]
    per = T // r
    return [(pl.ds(j, per, stride=r), slice(j * per, (j + 1) * per)) for j in range(r)]


def _rope_fwd(qkv, tabs):
    def body(x_ref, c_ref, s1_ref, s2_ref, o_ref):
        g = lax.rem(lax.div(pl.program_id(0), 2), 3)
        for gi, r in enumerate(DIL):
            @pl.when(g == gi)
            def _(r=r):
                for tok, prm in _row_chunks(r):
                    x = x_ref[tok, :]
                    y = x * c_ref[tok, :] + pltpu.roll(x, 8, 1) * s1_ref[tok, :] + pltpu.roll(x, 120, 1) * s2_ref[tok, :]
                    o_ref[prm, :] = y.astype(BF16)

    tab = pl.BlockSpec((None, T, 128), lambda b: (lax.div(b, 6), 0, 0))
    return pl.pallas_call(
        body, grid=(18,), in_specs=[pl.BlockSpec((T, 128), lambda b: (0, b)), tab, tab, tab],
        out_specs=pl.BlockSpec((None, T, 128), lambda b: (b, 0, 0)), out_shape=jax.ShapeDtypeStruct((18, T, 128), BF16),
        compiler_params=_params(("parallel",)), name="rope_fwd",
    )(qkv, *tabs)


def _rope_bwd(d, which, tabs, out_buf):
    def body(d_ref, c_ref, s1_ref, s2_ref, *rest):
        o_ref, tok_ref = rest[-2], rest[-1]
        g = lax.div(pl.program_id(0), 2)
        for gi, r in enumerate(DIL):
            @pl.when(g == gi)
            def _(r=r):
                for tok, prm in _row_chunks(r):
                    tok_ref[tok, :] = d_ref[prm, :]
                for rows, _ in _row_chunks(1):
                    gx = tok_ref[rows, :]
                    y = gx * c_ref[rows, :] + pltpu.roll(gx * s1_ref[rows, :], 120, 1) + pltpu.roll(gx * s2_ref[rows, :], 8, 1)
                    o_ref[rows, :] = y.astype(BF16)

    tab = pl.BlockSpec((None, T, 128), lambda b: (which, 0, 0))
    ins = [d, *tabs] + ([] if out_buf is None else [out_buf])
    specs = [pl.BlockSpec((None, None, T, 128), lambda b: (lax.div(b, 2), lax.rem(b, 2), 0, 0)), tab, tab, tab]
    return pl.pallas_call(
        body, grid=(6,), in_specs=specs + ([] if out_buf is None else [ANY]),
        out_specs=pl.BlockSpec((T, 128), lambda b: (0, 6 * which + b)),
        out_shape=jax.ShapeDtypeStruct((T, 3 * A_W), BF16), scratch_shapes=[pltpu.VMEM((T, 128), F32)],
        input_output_aliases={} if out_buf is None else {4: 0},
        compiler_params=_params(("arbitrary",)), name="rope_bwd",
    )(*ins)


def _head_mask(x, lane_lo):
    lane = lax.broadcasted_iota(jnp.int32, x.shape, 1)
    keep = (lane < HD) if lane_lo else (lane >= HD)
    return jnp.where(keep, x.astype(F32), 0.0).astype(BF16)


def _band_scalars():
    g, b = pl.program_id(0), pl.program_id(1)
    nbs = lax.shift_right_logical(jnp.int32(T // BLK), 2 * g)
    has_prev = jnp.where((b & (nbs - 1)) != 0, 1, 0)
    next_ok = jnp.where(((b + 1) & (nbs - 1)) != 0, 1, 0)
    return has_prev, next_ok


def _band_mask_q(has_prev):
    row = lax.broadcasted_iota(jnp.int32, (BLK, 2 * BLK), 0)
    col = lax.broadcasted_iota(jnp.int32, (BLK, 2 * BLK), 1)
    return ((col < BLK) & (col >= row) & (has_prev == 1)) | ((col >= BLK) & (col - BLK <= row))


def _band_mask_k(next_ok):
    row = lax.broadcasted_iota(jnp.int32, (2 * BLK, BLK), 0)
    col = lax.broadcasted_iota(jnp.int32, (2 * BLK, BLK), 1)
    return ((row < BLK) & (col <= row)) | ((row >= BLK) & (col >= row - BLK) & (next_ok == 1))


def _band_spec(base, step):
    nb = T // BLK
    at = {"cur": lambda b: b, "prev": lambda b: jnp.maximum(b - 1, 0), "next": lambda b: jnp.minimum(b + 1, nb - 1)}[step]
    return pl.BlockSpec((None, 2, BLK, 128), lambda g, b: (base + g, 0, at(b), 0))


def _band_fwd(qkv):
    nb = T // BLK

    def body(q_ref, kc_ref, kp_ref, vc_ref, vp_ref, o_ref, l_ref):
        has_prev, _ = _band_scalars()
        mask = _band_mask_q(has_prev)
        lane = lax.broadcasted_iota(jnp.int32, (BLK, 128), 1)
        for p in range(2):
            qp = q_ref[p]
            kcat = jnp.concatenate([kp_ref[p], kc_ref[p]], axis=0)
            vcat = jnp.concatenate([vp_ref[p], vc_ref[p]], axis=0)
            o_acc = jnp.zeros((BLK, 128), F32)
            lse = jnp.zeros((BLK, 128), F32)
            for e in range(2):
                s = _dot(_head_mask(qp, e == 0), kcat, NT)
                s = jnp.where(mask, s, NEG)
                m = jnp.max(s, axis=-1, keepdims=True)
                pr = jnp.exp(s - m)
                l = jnp.sum(pr, axis=-1, keepdims=True)
                o_acc = o_acc + _dot(pr.astype(BF16), _head_mask(vcat, e == 0), NN) / l
                lse = jnp.where((lane < HD) if e == 0 else (lane >= HD), m + jnp.log(l), lse)
            o_ref[p] = o_acc
            l_ref[p] = lse

    out = _band_spec(0, "cur")
    shp = jax.ShapeDtypeStruct((3, 2, T, 128), F32)
    return pl.pallas_call(
        body, grid=(3, nb),
        in_specs=[_band_spec(0, "cur"), _band_spec(3, "cur"), _band_spec(3, "prev"), _band_spec(6, "cur"), _band_spec(6, "prev")],
        out_specs=[out, out], out_shape=[shp, shp],
        compiler_params=_params(("parallel", "parallel")), name="band_fwd",
    )(qkv, qkv, qkv, qkv, qkv)


def _band_bwd(qkv, do, lse, dlt):
    nb = T // BLK

    def body(qc_ref, qn_ref, kc_ref, kp_ref, vc_ref, vp_ref, doc_ref, don_ref, lc_ref, ln_ref, dc_ref, dn_ref,
             dq_ref, dk_ref, dv_ref):
        has_prev, next_ok = _band_scalars()
        mask_q = _band_mask_q(has_prev)
        mask_k = _band_mask_k(next_ok)
        for p in range(2):
            qc, qn = qc_ref[p], qn_ref[p]
            doc, don = doc_ref[p], don_ref[p]
            kc, vc = kc_ref[p], vc_ref[p]
            kcat = jnp.concatenate([kp_ref[p], kc], axis=0)
            vcat = jnp.concatenate([vp_ref[p], vc], axis=0)
            qcat = jnp.concatenate([qc, qn], axis=0)
            docat = jnp.concatenate([doc, don], axis=0)
            dq = jnp.zeros((BLK, 128), F32)
            dk = jnp.zeros((BLK, 128), F32)
            dv = jnp.zeros((BLK, 128), F32)
            for e in range(2):
                lo = e == 0
                col = slice(HD * e, HD * e + 1)
                lse_c, lse_n = lc_ref[p, :, col], ln_ref[p, :, col]
                dl_c, dl_n = dc_ref[p, :, col], dn_ref[p, :, col]
                s = jnp.where(mask_q, _dot(_head_mask(qc, lo), kcat, NT), NEG)
                pr = jnp.exp(s - lse_c)
                dp = _dot(_head_mask(doc, lo), vcat, NT)
                ds = pr * (dp - dl_c)
                dq = dq + _dot(ds.astype(BF16), _head_mask(kcat, lo), NN)
                qm, dom = _head_mask(qcat, lo), _head_mask(docat, lo)
                s2 = jnp.where(mask_k, _dot(qm, kc, NT), NEG)
                p2 = jnp.exp(s2 - jnp.concatenate([lse_c, lse_n], axis=0))
                dv = dv + _dot(p2.astype(BF16), dom, TN)
                dp2 = _dot(dom, vc, NT)
                ds2 = p2 * (dp2 - jnp.concatenate([dl_c, dl_n], axis=0))
                dk = dk + _dot(ds2.astype(BF16), qm, TN)
            dq_ref[p] = dq
            dk_ref[p] = dk
            dv_ref[p] = dv

    cur, nxt = _band_spec(0, "cur"), _band_spec(0, "next")
    shp = jax.ShapeDtypeStruct((3, 2, T, 128), F32)
    return pl.pallas_call(
        body, grid=(3, nb),
        in_specs=[cur, nxt, _band_spec(3, "cur"), _band_spec(3, "prev"), _band_spec(6, "cur"), _band_spec(6, "prev"),
                  cur, nxt, cur, nxt, cur, nxt],
        out_specs=[cur, cur, cur], out_shape=[shp, shp, shp],
        compiler_params=_params(("parallel", "parallel")), name="band_bwd",
    )(qkv, qkv, qkv, qkv, qkv, qkv, do, do, lse, lse, dlt, dlt)


def _split3(x):
    hi = x.astype(BF16)
    r = x - hi.astype(F32)
    mid = r.astype(BF16)
    lo = (r - mid.astype(F32)).astype(BF16)
    return hi, mid, lo


def _dot3(x, m, dims=NN):
    hi, mid, lo = _split3(x)
    return _dot(hi, m, dims) + _dot(mid, m, dims) + _dot(lo, m, dims)


def _combine_weights(lses):
    l0, l1, l2 = lses
    m = jnp.maximum(jnp.maximum(l0, l1), l2)
    e = [jnp.exp(l0 - m), jnp.exp(l1 - m), jnp.exp(l2 - m)]
    inv = 1.0 / (e[0] + e[1] + e[2])
    return [ei * inv for ei in e]


CR = 256


def _combine_fwd(o, lse):
    def body(o_ref, l_ref, att_ref, o3_ref, l3_ref):
        for g, r in enumerate(DIL):
            for p in range(2):
                for tok, prm in _row_chunks(r):
                    o3_ref[g, p, tok, :] = o_ref[g, p, prm, :]
                    l3_ref[g, p, tok, :] = l_ref[g, p, prm, :]
        for i in range(T // CR):
            rows = slice(i * CR, (i + 1) * CR)
            for p in range(2):
                alpha = _combine_weights([l3_ref[g, p, rows, :] for g in range(3)])
                for g in range(3):
                    att_ref[rows, g * GW + p * 128: g * GW + (p + 1) * 128] = (o3_ref[g, p, rows, :] * alpha[g]).astype(BF16)

    shp = jax.ShapeDtypeStruct((3, 2, T, 128), F32)
    return pl.pallas_call(
        body, out_shape=[jax.ShapeDtypeStruct((T, A_W), BF16), shp, shp],
        compiler_params=_params(vmem=VMEM_BIG), name="combine_fwd",
    )(o, lse)


def _combine_bwd(datt, o3, l3, headsum):
    def body(d_ref, o_ref, l_ref, hs_ref, do_ref, dl_ref, tdo_ref, tdl_ref):
        hs = hs_ref[...]
        for p in range(2):
            for i in range(T // CR):
                rows = slice(i * CR, (i + 1) * CR)
                alpha = _combine_weights([l_ref[g, p, rows, :] for g in range(3)])
                total = jnp.zeros((CR, 128), F32)
                for g in range(3):
                    dg = d_ref[rows, g * GW + p * 128: g * GW + (p + 1) * 128]
                    tdo_ref[g, rows, :] = dg * alpha[g]
                    total = total + alpha[g] * _dot3(dg * o_ref[g, p, rows, :], hs)
                for g in range(3):
                    tdl_ref[g, rows, :] = alpha[g] * total
            for g, r in enumerate(DIL):
                for tok, prm in _row_chunks(r):
                    do_ref[g, p, prm, :] = tdo_ref[g, tok, :].astype(BF16)
                    dl_ref[g, p, prm, :] = tdl_ref[g, tok, :]

    return pl.pallas_call(
        body, out_shape=[jax.ShapeDtypeStruct((3, 2, T, 128), BF16), jax.ShapeDtypeStruct((3, 2, T, 128), F32)],
        scratch_shapes=[pltpu.VMEM((3, T, 128), F32), pltpu.VMEM((3, T, 128), F32)],
        compiler_params=_params(vmem=VMEM_BIG), name="combine_bwd",
    )(datt, o3, l3, headsum)


def _fox_scores(qm, k_ref, cq, ck_ref, e, i, n):
    s = _dot(qm, k_ref[0:n, :], NT) + (cq - ck_ref[0, e:e + 1, 0:n])
    row = lax.broadcasted_iota(jnp.int32, (FQ, n), 0)
    col = lax.broadcasted_iota(jnp.int32, (FQ, n), 1)
    s = jnp.where(col <= row + i * FQ, s, NEG)
    m = jnp.max(s, axis=-1, keepdims=True)
    pr = jnp.exp(s - m)
    return pr, jnp.sum(pr, axis=-1, keepdims=True)


def _fox_fwd(q, kv, c_col, c_row):
    def body(q_ref, k_ref, v_ref, cc_ref, cr_ref, o_ref, vm_ref):
        for e in range(2):
            vm_ref[e] = _head_mask(v_ref[...], e == 0)
        for i in range(T // FQ):
            n = (i + 1) * FQ
            rows = slice(i * FQ, n)
            acc = jnp.zeros((FQ, 128), F32)
            for e in range(2):
                qm = _head_mask(q_ref[rows, :], e == 0)
                pr, l = _fox_scores(qm, k_ref, cc_ref[0, rows, e:e + 1], cr_ref, e, i, n)
                acc = acc + _dot(pr.astype(BF16), vm_ref[e, 0:n, :], NN) / l
            o_ref[rows, :] = acc.astype(BF16)

    pair = pl.BlockSpec((T, 128), lambda p: (0, p))
    return pl.pallas_call(
        body, grid=(D // 128,),
        in_specs=[pair, pair, pl.BlockSpec((T, 128), lambda p: (0, D // 128 + p)),
                  pl.BlockSpec((1, T, 2), lambda p: (p, 0, 0)), pl.BlockSpec((1, 2, T), lambda p: (p, 0, 0))],
        out_specs=pair, out_shape=jax.ShapeDtypeStruct((T, D), BF16),
        scratch_shapes=[pltpu.VMEM((2, T, 128), BF16)],
        compiler_params=_params(("parallel",), VMEM_BIG), name="fox_fwd",
    )(q, kv, kv, c_col, c_row)


def _fox_bwd(q, kv, do, c_col, c_row, init):
    def body(q_ref, k_ref, v_ref, do_ref, cc_ref, cr_ref, ik_ref, iv_ref, iq_ref, ic_ref,
             dq_ref, dk_ref, dv_ref, dcq_ref, dck_ref, km_ref):
        dk_ref[...] = ik_ref[...]
        dv_ref[...] = iv_ref[...]
        dcq_ref[...] = iq_ref[...]
        dck_ref[...] = ic_ref[...]
        for e in range(2):
            km_ref[e] = _head_mask(k_ref[...], e == 0)
        for i in range(T // FQ):
            n = (i + 1) * FQ
            rows = slice(i * FQ, n)
            dq = jnp.zeros((FQ, 128), F32)
            for e in range(2):
                qm = _head_mask(q_ref[rows, :], e == 0)
                dom = _head_mask(do_ref[rows, :], e == 0)
                pr, l = _fox_scores(qm, k_ref, cc_ref[0, rows, e:e + 1], cr_ref, e, i, n)
                pr = pr / l
                dp = _dot(dom, v_ref[0:n, :], NT)
                ds = pr * (dp - jnp.sum(pr * dp, axis=-1, keepdims=True))
                dsb = ds.astype(BF16)
                dq = dq + _dot(dsb, km_ref[e, 0:n, :], NN)
                dk_ref[0:n, :] += _dot(dsb, qm, TN)
                dv_ref[0:n, :] += _dot(pr.astype(BF16), dom, TN)
                dcq_ref[0, rows, e:e + 1] += jnp.sum(ds, axis=-1, keepdims=True)
                dck_ref[0, e:e + 1, 0:n] += jnp.sum(ds, axis=0, keepdims=True)
            dq_ref[rows, :] = (dq * HD ** -0.5).astype(BF16)

    pair = pl.BlockSpec((T, 128), lambda p: (0, p))
    cq = pl.BlockSpec((1, T, 128), lambda p: (p, 0, 0))
    ck = pl.BlockSpec((1, 8, T), lambda p: (p, 0, 0))
    return pl.pallas_call(
        body, grid=(D // 128,),
        in_specs=[pair, pair, pl.BlockSpec((T, 128), lambda p: (0, D // 128 + p)), pair,
                  pl.BlockSpec((1, T, 2), lambda p: (p, 0, 0)), pl.BlockSpec((1, 2, T), lambda p: (p, 0, 0)),
                  pair, pair, cq, ck],
        out_specs=[pair, pair, pair, cq, ck],
        out_shape=[jax.ShapeDtypeStruct((T, D), BF16), jax.ShapeDtypeStruct((T, D), F32), jax.ShapeDtypeStruct((T, D), F32),
                   jax.ShapeDtypeStruct((D // 128, T, 128), F32), jax.ShapeDtypeStruct((D // 128, 8, T), F32)],
        scratch_shapes=[pltpu.VMEM((2, T, 128), BF16)],
        compiler_params=_params(("parallel",), VMEM_BIG), name="fox_bwd",
    )(q, kv, kv, do, c_col, c_row, *init)


def _tri(lower):
    r = lax.broadcasted_iota(jnp.int32, (BLK, BLK), 0)
    c = lax.broadcasted_iota(jnp.int32, (BLK, BLK), 1)
    return jnp.where((c <= r) if lower else (c >= r), 1.0, 0.0).astype(BF16)


def _gates_fwd(z, b):
    def body(z_ref, b_ref, c_ref):
        tri = _tri(True)
        carry = jnp.zeros((1, 128), F32)
        for i in range(T // BLK):
            rows = slice(i * BLK, (i + 1) * BLK)
            x = z_ref[rows, :] + b_ref[...]
            logf = jnp.minimum(x, 0.0) - jnp.log(1.0 + jnp.exp(-jnp.abs(x)))
            hi, mid, lo = _split3(logf)
            y = _dot(tri, hi, NN) + _dot(tri, mid, NN) + _dot(tri, lo, NN) + carry
            c_ref[rows, :] = y
            carry = y[BLK - 1:BLK, :]

    return pl.pallas_call(body, out_shape=jax.ShapeDtypeStruct((T, 128), F32), name="gates_fwd")(z, b)


def _gates_bwd(dc, z, b):
    def body(dc_ref, z_ref, b_ref, dz_ref, db_ref):
        tri = _tri(False)
        carry = jnp.zeros((1, 128), F32)
        db = jnp.zeros((1, 128), F32)
        for i in reversed(range(T // BLK)):
            rows = slice(i * BLK, (i + 1) * BLK)
            hi, mid, lo = _split3(dc_ref[rows, :])
            dlogf = _dot(tri, hi, NN) + _dot(tri, mid, NN) + _dot(tri, lo, NN) + carry
            carry = dlogf[0:1, :]
            x = z_ref[rows, :] + b_ref[...]
            dz = dlogf / (1.0 + jnp.exp(x))
            dz_ref[rows, :] = dz.astype(BF16)
            db = db + jnp.sum(dz, axis=0, keepdims=True)
        db_ref[...] = db

    return pl.pallas_call(
        body, out_shape=[jax.ShapeDtypeStruct((T, 128), BF16), jax.ShapeDtypeStruct((1, 128), F32)], name="gates_bwd",
    )(dc, z, b)


def _conv_pair(a_refs, cw_refs, cb_refs):
    row = lax.broadcasted_iota(jnp.int32, (T, CT), 0)
    outs = []
    for a_ref, cw_ref, cb_ref in zip(a_refs, cw_refs, cb_refs):
        z = a_ref[...]
        z1 = jnp.where(row >= 1, pltpu.roll(z, 1, 0), 0.0)
        z2 = jnp.where(row >= 2, pltpu.roll(z, 2, 0), 0.0)
        y = cw_ref[2:3, :] * z + cw_ref[1:2, :] * z1 + cw_ref[0:1, :] * z2 + cb_ref[...]
        outs.append((y, z, z1, z2))
    return outs


_GELU_K = math.sqrt(2.0 / math.pi)
N_CT = D_FF // CT


def _conv_specs():
    def at(rows, off):
        return pl.BlockSpec((rows, CT), lambda j: (0, j + off))
    return [at(T, 0), at(T, N_CT), at(3, 0), at(3, N_CT), at(1, 0), at(1, N_CT)]


def _convgate_fwd(a, cw, cb):
    def body(ag_ref, av_ref, wg_ref, wv_ref, bg_ref, bv_ref, u_ref):
        (g, _, _, _), (v, _, _, _) = _conv_pair((ag_ref, av_ref), (wg_ref, wv_ref), (bg_ref, bv_ref))
        th = jnp.tanh(_GELU_K * (g + 0.044715 * g * g * g))
        u_ref[...] = (0.5 * g * (1.0 + th) * v).astype(BF16)

    return pl.pallas_call(
        body, grid=(N_CT,), in_specs=_conv_specs(),
        out_specs=pl.BlockSpec((T, CT), lambda j: (0, j)), out_shape=jax.ShapeDtypeStruct((T, D_FF), BF16),
        compiler_params=_params(("parallel",), VMEM_BIG), name="convgate_fwd",
    )(a, a, cw, cw, cb, cb)


def _convgate_bwd(a, du, cw, cb):
    def body(ag_ref, av_ref, wg_ref, wv_ref, bg_ref, bv_ref, du_ref, da_ref, dcw_ref, dcb_ref):
        (g, gz, gz1, gz2), (v, vz, vz1, vz2) = _conv_pair((ag_ref, av_ref), (wg_ref, wv_ref), (bg_ref, bv_ref))
        du = du_ref[...].astype(F32)
        th = jnp.tanh(_GELU_K * (g + 0.044715 * g * g * g))
        gelu = 0.5 * g * (1.0 + th)
        dgelu = 0.5 * (1.0 + th) + 0.5 * g * (1.0 - th * th) * _GELU_K * (1.0 + 3 * 0.044715 * g * g)
        row = lax.broadcasted_iota(jnp.int32, (T, CT), 0)
        for h, (d, z, z1, z2, w_ref) in enumerate(((du * v * dgelu, gz, gz1, gz2, wg_ref), (du * gelu, vz, vz1, vz2, wv_ref))):
            d1 = jnp.where(row < T - 1, pltpu.roll(d, T - 1, 0), 0.0)
            d2 = jnp.where(row < T - 2, pltpu.roll(d, T - 2, 0), 0.0)
            da_ref[h] = (w_ref[2:3, :] * d + w_ref[1:2, :] * d1 + w_ref[0:1, :] * d2).astype(BF16)
            dcw_ref[h, 0:1, :] = jnp.sum(d * z2, axis=0, keepdims=True)
            dcw_ref[h, 1:2, :] = jnp.sum(d * z1, axis=0, keepdims=True)
            dcw_ref[h, 2:3, :] = jnp.sum(d * z, axis=0, keepdims=True)
            dcb_ref[h] = jnp.sum(d, axis=0, keepdims=True)

    def both(rows):
        return pl.BlockSpec((2, rows, CT), lambda j: (0, 0, j))

    return pl.pallas_call(
        body, grid=(N_CT,),
        in_specs=_conv_specs() + [pl.BlockSpec((T, CT), lambda j: (0, j))],
        out_specs=[both(T), both(3), both(1)],
        out_shape=[jax.ShapeDtypeStruct((2, T, D_FF), BF16), jax.ShapeDtypeStruct((2, 3, D_FF), F32),
                   jax.ShapeDtypeStruct((2, 1, D_FF), F32)],
        compiler_params=_params(("parallel",), VMEM_BIG), name="convgate_bwd",
    )(a, a, cw, cw, cb, cb, du)


def _halves_a(tm, tn, tk):
    per = D_FF // tk
    return lambda i, j, k: (lax.div(k, per), i, lax.rem(k, per))


def _halves_b(tm, tn, tk):
    per = D_FF // tn
    return lambda i, j, k: (lax.div(j, per), k, lax.rem(j, per))


def _adamw(w, m, v, g, *, name):
    r, c = w.shape
    tr = r
    if r * c > 256 * 1024:
        for cand in range(8, r, 8):
            if r % cand == 0 and cand * c <= 256 * 1024:
                tr = cand

    def body(w_ref, m_ref, v_ref, g_ref, d_ref, nm_ref, nv_ref):
        gv = g_ref[...]
        mn = ADAM_B1 * m_ref[...] + (1.0 - ADAM_B1) * gv
        vn = ADAM_B2 * v_ref[...] + (1.0 - ADAM_B2) * (gv * gv)
        m_hat = mn / (1.0 - ADAM_B1 ** ADAM_STEP)
        v_hat = vn / (1.0 - ADAM_B2 ** ADAM_STEP)
        d_ref[...] = -ADAM_LR * (m_hat / (jnp.sqrt(v_hat) + ADAM_EPS) + ADAM_WD * w_ref[...])
        nm_ref[...] = mn
        nv_ref[...] = vn

    blk = pl.BlockSpec((tr, c), lambda i: (i, 0))
    shp = jax.ShapeDtypeStruct((r, c), F32)
    return pl.pallas_call(
        body, grid=(r // tr,), in_specs=[blk] * 4, out_specs=[blk] * 3, out_shape=[shp] * 3,
        compiler_params=_params(("parallel",)), name=name,
    )(w, m, v, g)


def _place():
    x, y, c = lax.axis_index("x"), lax.axis_index("y"), lax.axis_index("c")
    chips = [(1 - x, y), (x, 1 - y), (1 - x, 1 - y)]
    return x, y, c, chips


def _window(ref, kind, s, half=None):
    lead = () if half is None else (half,)
    b, c = ref.shape[-2], ref.shape[-1]
    if kind == "col":
        return ref.at[lead + (slice(None), slice(None), pl.ds(s * (c // N_CHIPS), c // N_CHIPS))]
    if kind == "row":
        return ref.at[lead + (slice(None), pl.ds(s * (b // N_CHIPS), b // N_CHIPS), slice(None))]
    return ref.at[lead + (s,)]


def _window_shape(shape3, kind):
    a, b, c = shape3
    return {"col": (a, b, c // N_CHIPS), "row": (a, b // N_CHIPS, c), "slab": (b, c)}[kind]


def _allgather(tensors, kinds, *, name):
    n = len(tensors)

    def body(*refs):
        bufs = refs[n:2 * n]
        send, recv = refs[2 * n:]
        x, y, c, chips = _place()
        me = 2 * x + y
        sib = (x, y, 1 - c)

        def rcopy(i, k, win, to):
            return pltpu.make_async_remote_copy(src_ref=win, dst_ref=win, send_sem=send.at[i * 6 + k], recv_sem=recv.at[i * 6 + k],
                                                device_id=to, device_id_type=MESH)

        started = []
        for i in range(n):
            for k, (px, py) in enumerate(chips):
                cp = rcopy(i, k, _window(bufs[i], kinds[i], me, c), (px, py, c))
                cp.start()
                started.append(cp)
        for i in range(n):
            for k, (px, py) in enumerate(chips):
                landed = _window(bufs[i], kinds[i], 2 * px + py, c)
                rcopy(i, k, landed, (px, py, c)).wait_recv()
                fw = rcopy(i, 3 + k, landed, sib)
                fw.start()
                started.append(fw)
        for i in range(n):
            for k, (px, py) in enumerate(chips):
                rcopy(i, 3 + k, _window(bufs[i], kinds[i], 2 * px + py, 1 - c), sib).wait_recv()
        for cp in started:
            cp.wait_send()

    return pl.pallas_call(
        body, in_specs=[ANY] * n, out_specs=[ANY] * n,
        out_shape=[jax.ShapeDtypeStruct(t.shape, t.dtype) for t in tensors],
        scratch_shapes=[pltpu.SemaphoreType.DMA((6 * n,)), pltpu.SemaphoreType.DMA((6 * n,))],
        input_output_aliases={i: i for i in range(n)},
        name=name,
    )(*tensors)


def _rows_tile(rows, cols, sub):
    best = None
    for t in range(sub, rows + 1, sub):
        if rows % t == 0 and t * cols <= 512 * 1024:
            best = t
    return rows if best is None else best


def _sequencer(name, cid, n_sems, peers_of, body):
    @pl.kernel(mesh=plsc.ScalarSubcoreMesh(axis_name="seq", num_cores=1), name=name,
               scratch_types=(pltpu.SemaphoreType.DMA((n_sems,)), pltpu.SemaphoreType.DMA((n_sems,))),
               compiler_params=pltpu.CompilerParams(collective_id=cid))
    def launch(send, recv):
        x, y, c, chips = _place()
        peers = peers_of(x, y, c, chips)
        barrier = pltpu.get_barrier_semaphore()
        for peer in peers:
            pl.semaphore_signal(barrier, inc=1, device_id=peer, device_id_type=MESH)
        pl.semaphore_wait(barrier, len(peers))
        body(send, recv)

    launch()


def _half_of_full(ref, kind, h):
    if kind == "col":
        b = ref.shape[0]
        return ref.at[pl.ds(h * (b // 2), b // 2), :]
    if kind == "row":
        c = ref.shape[1]
        return ref.at[:, pl.ds(h * (c // 2), c // 2)]
    b = ref.shape[1]
    return ref.at[:, pl.ds(h * (b // 2), b // 2), :]


def _half_shape(full, kind):
    if kind == "col":
        return (full[0] // 2, full[1])
    if kind == "row":
        return (full[0], full[1] // 2)
    return (full[0], full[1] // 2, full[2])


def _win_of_half(ref, kind, s):
    if kind == "col":
        c = ref.shape[1]
        return ref.at[:, pl.ds(s * (c // N_CHIPS), c // N_CHIPS)]
    if kind == "row":
        b = ref.shape[0]
        return ref.at[pl.ds(s * (b // N_CHIPS), b // N_CHIPS), :]
    return ref.at[s]


def _win_shape(half, kind):
    if kind == "col":
        return (half[0], half[1] // N_CHIPS)
    if kind == "row":
        return (half[0] // N_CHIPS, half[1])
    return half[1:]


def _seq_swap(parts, kinds, *, name):
    n = len(parts)
    srcs = [jax.new_ref(p, memory_space=pltpu.MemorySpace.HBM) for p in parts]
    outs = [jax.empty_ref(jax.ShapeDtypeStruct(_half_shape(p.shape, k), p.dtype), memory_space=pltpu.MemorySpace.HBM)
            for p, k in zip(parts, kinds)]

    def body(send, recv):
        x, y, c, _ = _place()
        cps = []
        for i in range(n):
            cp = pltpu.make_async_remote_copy(src_ref=_half_of_full(srcs[i], kinds[i], 1 - c), dst_ref=outs[i], send_sem=send.at[i],
                                              recv_sem=recv.at[i], device_id=(x, y, 1 - c), device_id_type=MESH)
            cp.start()
            cps.append(cp)
        for cp in cps:
            cp.wait()

    _sequencer(name, 2, n, lambda x, y, c, chips: [(x, y, 1 - c)], body)
    return [o[...] for o in outs]


def _seq_scatter(halves, kinds, *, name):
    n = len(halves)
    srcs = [jax.new_ref(h, memory_space=pltpu.MemorySpace.HBM) for h in halves]
    outs = [jax.empty_ref(jax.ShapeDtypeStruct((3,) + _win_shape(h.shape, k), h.dtype), memory_space=pltpu.MemorySpace.HBM)
            for h, k in zip(halves, kinds)]

    def body(send, recv):
        x, y, c, chips = _place()
        cps = []
        for i in range(n):
            for k, (px, py) in enumerate(chips):
                cp = pltpu.make_async_remote_copy(src_ref=_win_of_half(srcs[i], kinds[i], 2 * px + py), dst_ref=outs[i].at[k],
                                                  send_sem=send.at[3 * i + k], recv_sem=recv.at[3 * i + k],
                                                  device_id=(px, py, c), device_id_type=MESH)
                cp.start()
                cps.append(cp)
        for cp in cps:
            cp.wait()

    _sequencer(name, 3, 3 * n, lambda x, y, c, chips: [(px, py, c) for px, py in chips], body)
    return [o[...] for o in outs]


def _add_half(g, p, kind, where, after, *, name):
    if kind == "slab":
        s, b2, c = p.shape
        tr = _rows_tile(b2, c, 16)
        nr = b2 // tr
        grid = (s, nr)
        g_spec = pl.BlockSpec((None, tr, c), lambda i, r, w: (i, w[1] * nr + r, 0))
        p_spec = pl.BlockSpec((None, tr, c), lambda i, r, w: (i, r, 0))
    elif kind == "col":
        b2, c = p.shape
        tr = _rows_tile(b2, c, 16)
        nr = b2 // tr
        grid = (1, nr)
        g_spec = pl.BlockSpec((tr, c), lambda i, r, w: (w[1] * nr + r, 0))
        p_spec = pl.BlockSpec((tr, c), lambda i, r, w: (r, 0))
    else:
        b, c2 = p.shape
        tr = _rows_tile(b, c2, 16)
        grid = (1, b // tr)
        g_spec = pl.BlockSpec((tr, c2), lambda i, r, w: (r, w[1]))
        p_spec = pl.BlockSpec((tr, c2), lambda i, r, w: (r, 0))

    def body(w_ref, g_ref, p_ref, *rest):
        o_ref = rest[-1]
        o_ref[...] = (g_ref[...].astype(F32) + p_ref[...].astype(F32)).astype(o_ref.dtype)

    extra = [] if after is None else [after]
    return pl.pallas_call(
        body,
        grid_spec=pltpu.PrefetchScalarGridSpec(num_scalar_prefetch=1, grid=grid, in_specs=[g_spec, p_spec] + [ANY] * len(extra),
                                               out_specs=p_spec),
        out_shape=jax.ShapeDtypeStruct(p.shape, g.dtype),
        compiler_params=_params(("parallel", "parallel")), name=name,
    )(where, g, p, *extra)


def _sum_chips(r, h, kind, where, layer, layers, out_buf, after, *, name):
    _, br, cr = r.shape
    tr = _rows_tile(br, cr, 16)
    nr = br // tr
    if kind == "col":
        h_spec = pl.BlockSpec((tr, cr), lambda j, w: (j, w[0]))
        o_shape, o_spec = (layers, 2 * br, cr), pl.BlockSpec((None, tr, cr), lambda j, w: (layer, w[1] * nr + j, 0))
    elif kind == "row":
        h_spec = pl.BlockSpec((tr, cr), lambda j, w: (w[0] * nr + j, 0))
        o_shape, o_spec = (layers, br, 2 * cr), pl.BlockSpec((None, tr, cr), lambda j, w: (layer, j, w[1]))
    else:
        h_spec = pl.BlockSpec((None, tr, cr), lambda j, w: (w[0], j, 0))
        o_shape, o_spec = (layers, 2 * br, cr), pl.BlockSpec((None, tr, cr), lambda j, w: (layer, w[1] * nr + j, 0))

    def body(w_ref, h_ref, r0_ref, r1_ref, r2_ref, *rest):
        o_ref, t_ref = rest[-2], rest[-1]
        o_ref[...] = ((h_ref[...].astype(F32) + r0_ref[...].astype(F32)) + r1_ref[...].astype(F32)) + r2_ref[...].astype(F32)
        t_ref[...] = jnp.zeros_like(t_ref)

    def slot(k):
        return pl.BlockSpec((None, tr, cr), lambda j, w: (k, j, 0))

    ins, specs, alias = [h, r, r, r], [h_spec, slot(0), slot(1), slot(2)], {}
    if after is not None:
        ins.append(after)
        specs.append(ANY)
    if out_buf is not None:
        alias = {1 + len(ins): 0}
        ins.append(out_buf)
        specs.append(ANY)
    return pl.pallas_call(
        body,
        grid_spec=pltpu.PrefetchScalarGridSpec(num_scalar_prefetch=1, grid=(nr,), in_specs=specs,
                                               out_specs=[o_spec, pl.BlockSpec((8, 128), lambda j, w: (0, 0))]),
        out_shape=[jax.ShapeDtypeStruct(o_shape, F32), jax.ShapeDtypeStruct((8, 128), F32)], input_output_aliases=alias,
        compiler_params=_params(("arbitrary",)), name=name,
    )(where, *ins)


def _join_halves(tensors, kinds, *, name):
    n = len(tensors)

    def mine(ref, kind, h):
        if kind == "row":
            c = ref.shape[2]
            return ref.at[:, :, pl.ds(h * (c // 2), c // 2)]
        b = ref.shape[1]
        return ref.at[:, pl.ds(h * (b // 2), b // 2), :]

    def body(*refs):
        bufs = refs[n:2 * n]
        send, recv = refs[2 * n:]
        x, y, c, _ = _place()
        cps = []
        for i in range(n):
            part = mine(bufs[i], kinds[i], c)
            cp = pltpu.make_async_remote_copy(src_ref=part, dst_ref=part, send_sem=send.at[i],
                                              recv_sem=recv.at[i], device_id=(x, y, 1 - c), device_id_type=MESH)
            cp.start()
            cps.append(cp)
        for i in range(n):
            other = mine(bufs[i], kinds[i], 1 - c)
            pltpu.make_async_remote_copy(src_ref=other, dst_ref=other, send_sem=send.at[i],
                                         recv_sem=recv.at[i], device_id=(x, y, 1 - c), device_id_type=MESH).wait_recv()
        for cp in cps:
            cp.wait_send()

    return pl.pallas_call(
        body, in_specs=[ANY] * n, out_specs=[ANY] * n,
        out_shape=[jax.ShapeDtypeStruct(t.shape, t.dtype) for t in tensors],
        scratch_shapes=[pltpu.SemaphoreType.DMA((n,)), pltpu.SemaphoreType.DMA((n,))],
        input_output_aliases={i: i for i in range(n)},
        name=name,
    )(*tensors)


def _win(ref, kind, s, h=None):
    if kind == "col":
        b, c = ref.shape
        cols = pl.ds(s * (c // N_CHIPS), c // N_CHIPS)
        return ref.at[:, cols] if h is None else ref.at[pl.ds(h * (b // 2), b // 2), cols]
    if kind == "row":
        b, c = ref.shape
        rows = pl.ds(s * (b // N_CHIPS), b // N_CHIPS)
        return ref.at[rows, :] if h is None else ref.at[rows, pl.ds(h * (c // 2), c // 2)]
    b = ref.shape[1]
    return ref.at[s] if h is None else ref.at[s, pl.ds(h * (b // 2), b // 2)]


def _half(ref, kind, h):
    b, c = ref.shape
    if kind == "row":
        return ref.at[:, pl.ds(h * (c // 2), c // 2)]
    return ref.at[pl.ds(h * (b // 2), b // 2), :]


def _full_shape(shard_shape, kind):
    b, c = shard_shape
    return {"col": (b, N_CHIPS * c), "row": (N_CHIPS * b, c), "slab": (N_CHIPS, b, c)}[kind]


def _gather_body(srcs, outs, kinds, send, recv):
    x, y, c, chips = _place()
    me = 2 * x + y
    sib = (x, y, 1 - c)

    def rcopy(i, k, src, dst, to):
        return pltpu.make_async_remote_copy(src_ref=src, dst_ref=dst, send_sem=send.at[7 * i + k], recv_sem=recv.at[7 * i + k],
                                            device_id=to, device_id_type=MESH)

    started = []
    for i, (src, out, kind) in enumerate(zip(srcs, outs, kinds)):
        own = rcopy(i, 6, src, _win(out, kind, me), sib)
        own.start()
        started.append(own)
        for k, (px, py) in enumerate(chips):
            cp = rcopy(i, k, _half(src, kind, c), _win(out, kind, me, c), (px, py, c))
            cp.start()
            started.append(cp)
    for i, (out, kind) in enumerate(zip(outs, kinds)):
        for k, (px, py) in enumerate(chips):
            landed = _win(out, kind, 2 * px + py, c)
            rcopy(i, k, landed, landed, (px, py, c)).wait_recv()
            fw = rcopy(i, 3 + k, landed, landed, sib)
            fw.start()
            started.append(fw)
    for i, (src, out, kind) in enumerate(zip(srcs, outs, kinds)):
        for k, (px, py) in enumerate(chips):
            other = _win(out, kind, 2 * px + py, 1 - c)
            rcopy(i, 3 + k, other, other, sib).wait_recv()
        rcopy(i, 6, src, _win(out, kind, me), sib).wait_recv()
    for cp in started:
        cp.wait_send()


def _seq_gather(shards, kinds, *, name, cid):
    n = len(shards)
    srcs = [jax.new_ref(s, memory_space=pltpu.MemorySpace.HBM) for s in shards]
    outs = [jax.empty_ref(jax.ShapeDtypeStruct(_full_shape(s.shape, k), s.dtype), memory_space=pltpu.MemorySpace.HBM)
            for s, k in zip(shards, kinds)]

    @pl.kernel(mesh=plsc.ScalarSubcoreMesh(axis_name="seq", num_cores=1), name=name,
               scratch_types=(pltpu.SemaphoreType.DMA((7 * n,)), pltpu.SemaphoreType.DMA((7 * n,))),
               compiler_params=pltpu.CompilerParams(collective_id=cid))
    def launch(send, recv):
        x, y, c, chips = _place()
        barrier = pltpu.get_barrier_semaphore()
        for px, py in chips:
            pl.semaphore_signal(barrier, inc=1, device_id=(px, py, c), device_id_type=MESH)
        pl.semaphore_signal(barrier, inc=1, device_id=(x, y, 1 - c), device_id_type=MESH)
        pl.semaphore_wait(barrier, 4)
        _gather_body(srcs, outs, kinds, send, recv)

    launch()
    return [o[...] for o in outs]


KIND = dict(w_qkv_a="slab", w_o_a="col", w_q_b="row", w_o_b="row", w_kvf="slab", w_up="col", w_down="row", small="slab")
LAYERS = dict(w_qkv_a=N_A, w_o_a=N_A, w_q_b=DEPTH - N_A, w_o_b=DEPTH - N_A, w_kvf=1, w_up=DEPTH, w_down=DEPTH, small=1)
SMALL_W = 1792
SMALL_ROWS = 8


class _Reducer:
    def __init__(self, where):
        self.where = where
        self.acc = {nm: None for nm in KIND}
        self.pending = None

    def __call__(self, group, tag):
        names, layers, parts = zip(*group)
        kinds = [KIND[nm] for nm in names]
        summed = self._sum_pending(after=parts[-1])
        sib = _seq_swap(list(parts), kinds, name="reduce_swap_" + tag)
        halves = []
        for g, p, k, nm in zip(parts, sib, kinds, names):
            halves.append(_add_half(g, p, k, self.where, halves[-1] if halves else None, name="reduce_add_" + nm))
        landed = _seq_scatter(halves, kinds, name="reduce_scatter_" + tag)
        self.pending = (names, layers, landed, halves, kinds)
        return [halves[-1], summed]

    def flush(self, after):
        return self._sum_pending(after)

    def _sum_pending(self, after):
        if self.pending is None:
            return None
        for nm, l, r, h, k in zip(*self.pending):
            self.acc[nm], after = _sum_chips(r, h, k, self.where, l, LAYERS[nm], self.acc[nm], after, name="reduce_sum_" + nm)
        self.pending = None
        return after

    def finish(self):
        self._sum_pending(after=None)
        names = list(KIND)
        joined = _join_halves([self.acc[nm] for nm in names], [KIND[nm] for nm in names], name="reduce_pair_join")
        return dict(zip(names, joined))


def _headsum_matrix():
    r = lax.broadcasted_iota(jnp.int32, (128, 128), 0) // HD
    c = lax.broadcasted_iota(jnp.int32, (128, 128), 1) // HD
    return jnp.where(r == c, 1.0, 0.0).astype(BF16)


def kernel(x, norm_gains, w_qkv_a, w_o_a, w_q_b, w_o_b, kv_norm, w_kvf, b_f, w_up, conv_w, conv_b, w_down, loss_target, m_norm_gains, m_w_qkv_a, m_w_o_a, m_w_q_b, m_w_o_b, m_kv_norm, m_w_kvf, m_b_f, m_w_up, m_conv_w, m_conv_b, m_w_down, v_norm_gains, v_w_qkv_a, v_w_o_a, v_w_q_b, v_w_o_b, v_kv_norm, v_w_kvf, v_b_f, v_w_up, v_conv_w, v_conv_b, v_w_down):
    xi, yi, ci = lax.axis_index("x"), lax.axis_index("y"), lax.axis_index("c")
    chip = 2 * xi + yi
    where = jnp.stack([chip, ci]).astype(jnp.int32)
    ws = dict(norm_gains=norm_gains, w_qkv_a=w_qkv_a, w_o_a=w_o_a, w_q_b=w_q_b, w_o_b=w_o_b, kv_norm=kv_norm, w_kvf=w_kvf,
              b_f=b_f, w_up=w_up, conv_w=conv_w, conv_b=conv_b, w_down=w_down)
    ms = dict(norm_gains=m_norm_gains, w_qkv_a=m_w_qkv_a, w_o_a=m_w_o_a, w_q_b=m_w_q_b, w_o_b=m_w_o_b, kv_norm=m_kv_norm,
              w_kvf=m_w_kvf, b_f=m_b_f, w_up=m_w_up, conv_w=m_conv_w, conv_b=m_conv_b, w_down=m_w_down)
    vs = dict(norm_gains=v_norm_gains, w_qkv_a=v_w_qkv_a, w_o_a=v_w_o_a, w_q_b=v_w_q_b, w_o_b=v_w_o_b, kv_norm=v_kv_norm,
              w_kvf=v_w_kvf, b_f=v_b_f, w_up=v_w_up, conv_w=v_conv_w, conv_b=v_conv_b, w_down=v_w_down)

    small = jnp.concatenate([
        jnp.pad(norm_gains.reshape(16, 256), ((0, 0), (0, 1408 - 256))),
        jnp.pad(conv_w.reshape(12, 1408), ((0, 4), (0, 0)))], axis=0)
    big = [nm for nm in KIND if nm != "small"]
    half = {nm: ws[nm].astype(BF16) for nm in big}
    W = {nm: [None] * LAYERS[nm] for nm in big if nm != "w_kvf"}
    g_small = None
    groups = [("0a", [("w_qkv_a", 0), ("w_o_a", 0), ("small", 0)]), ("0b", [("w_up", 0)]), ("0c", [("w_down", 0)]),
              ("1", [("w_qkv_a", 1), ("w_o_a", 1), ("w_up", 1), ("w_down", 1)]),
              ("2", [("w_kvf", 0), ("w_q_b", 0), ("w_o_b", 0), ("w_up", 2), ("w_down", 2)]),
              ("3", [("w_q_b", 1), ("w_o_b", 1), ("w_up", 3), ("w_down", 3)])]
    for tag, group in groups:
        shards = [small if nm == "small" else half[nm] if nm == "w_kvf" else half[nm][i] for nm, i in group]
        got = _seq_gather(shards, [KIND[nm] for nm, _ in group], name="gather_layer" + tag, cid=1)
        for (nm, i), g in zip(group, got):
            if nm == "small":
                g_small = g
            elif nm == "w_kvf":
                W[nm] = g.transpose(1, 0, 2).reshape(D, 2 * D + 16)
            else:
                W[nm][i] = g.transpose(1, 0, 2).reshape(D, 3 * A_W) if nm == "w_qkv_a" else g
    gains = g_small[:, :16, :256].transpose(1, 0, 2).reshape(DEPTH, 4, 1, D)
    cw_full = g_small[:, 16:28, :].transpose(1, 0, 2).reshape(DEPTH, 3, 2 * D_FF)
    cb_full = conv_b.reshape(DEPTH, 1, 2 * D_FF)

    reducer = _Reducer(where)
    sq, dh = _fwd_bwd(x[0], loss_target[0], W, gains, cw_full, cb_full, kv_norm, b_f, reducer)
    loss = lax.psum(sq[0, 0] * (0.5 / D), ("x", "y", "c"))
    return _update(loss, dh[None], reducer.finish(), chip, ws, ms, vs)


def _fwd_bwd(h, target, W, gains, cw_full, cb_full, kv_norm, b_f, reduce):
    w_kv = W["w_kvf"][:, :2 * D]
    w_kvf_pad = jnp.pad(W["w_kvf"], ((0, 0), (0, 128 - 16)))
    w_f = w_kvf_pad[:, 2 * D:]
    kvn_g = kv_norm.reshape(1, D)
    bf_pad = jnp.pad(b_f, (0, 128 - 16)).reshape(1, 128)
    tabs = _rope_tables()
    headsum = _headsum_matrix()

    saved = []
    kv = zf = c_col = c_row = kvn = h_kv = None
    xn = _rms_fwd(h, gains[0][0], out_dtype=BF16, name="rms_in")
    for l in range(DEPTH):
        s = {"h": h}
        g = gains[l]
        s["xn"] = xn
        if l < N_A:
            qkv = _matmul(xn, W["w_qkv_a"][l], mode="nn", out_dtype=F32, name="mm_qkv", mnk=(T, 3 * A_W, D), tn=768)
            qkvp = _rope_fwd(qkv, tabs).reshape(9, 2, T, 128)
            o_p, lse_p = _band_fwd(qkvp)
            att, o3, lse3 = _combine_fwd(o_p, lse_p)
            s.update(qkvp=qkvp, o3=o3, lse3=lse3, lse_p=lse_p, att=att)
            mix = _matmul(att, W["w_o_a"][l], mode="nn", out_dtype=F32, name="mm_oa", mnk=(T, D, A_W))
        else:
            j = l - N_A
            if l == N_A:
                h_kv = h
                kvn = _rms_fwd(h, kvn_g, out_dtype=BF16, name="rms_in")
                kv = _matmul(kvn, w_kv, mode="nn", out_dtype=BF16, name="mm_kv")
                zf = _matmul(kvn, w_f, mode="nn", out_dtype=F32, name="mm_f")
                cum = _gates_fwd(zf, bf_pad)[:, :16]
                c_col = cum.reshape(T, 8, 2).transpose(1, 0, 2)
                c_row = cum.T.reshape(8, 2, T)
            q = _matmul(xn, W["w_q_b"][j], mode="nn", out_dtype=BF16, name="mm_qb", mnk=(T, D, D), alpha=HD ** -0.5)
            o = _fox_fwd(q, kv, c_col, c_row)
            s.update(q=q, o=o)
            mix = _matmul(o, W["w_o_b"][j], mode="nn", out_dtype=F32, name="mm_ob", mnk=(T, D, D))
        s["mix"] = mix
        h1, xn2 = _rms_res_in(mix, g[1], h, g[2], name="rms_res_in")
        a = _matmul(xn2, W["w_up"][l], mode="nn", out_dtype=F32, name="mm_up", mnk=(T, 2 * D_FF, D))
        u = _convgate_fwd(a, cw_full[l], cb_full[l])
        f = _matmul(u, W["w_down"][l], mode="nn", out_dtype=F32, name="mm_down", mnk=(T, D, D_FF), tm=1024, tk=D_FF)
        if l + 1 < DEPTH:
            h, xn = _rms_res_in(f, g[3], h1, gains[l + 1][0], name="rms_res_in")
        else:
            h = _rms_fwd(f, g[3], res=h1, out_dtype=F32, name="rms_res")
        s.update(h1=h1, xn2=xn2, a=a, u=u, f=f)
        saved.append(s)

    dh, sq = _loss_head(h, target)

    d_gains = [[None] * 4 for _ in range(DEPTH)]
    d_cw, d_cb = [None] * DEPTH, [None] * DEPTH
    zeros_td = jnp.zeros((T, D), F32)
    fox_acc = (zeros_td, zeros_td, jnp.zeros((D // 128, T, 128), F32), jnp.zeros((D // 128, 8, T), F32))
    d_kvnorm = d_bf = token = df = None

    def dw(nm, a, b, **kw):
        return _matmul(a, b, mode="tn", out_dtype=BF16, name="mm_dw_" + nm, **kw)

    flush = getattr(reduce, "flush", lambda after: None)

    def slabs(full, width):
        return full.reshape(full.shape[0], N_CHIPS, width).transpose(1, 0, 2)

    for l in reversed(range(DEPTH)):
        s = saved[l]
        g = gains[l]
        if df is None:
            df, d_gains[l][3] = _rms_bwd(dh, s["f"], g[3], out_dtype=BF16, name="rms_bwd")
        du = _matmul(df, W["w_down"][l], mode="nt", out_dtype=F32, name="mm_down_dx", mnk=(T, D_FF, D), tn=256, after=token)
        g_down = dw("w_down", s["u"], df, tm=1408, tn=1024)
        da, d_cw[l], d_cb[l] = _convgate_bwd(s["a"], du, cw_full[l], cb_full[l])
        dxn2 = _matmul(da, W["w_up"][l], mode="nt", out_dtype=F32, name="mm_up_dx", mnk=(T, D, 2 * D_FF), tm=1024, tn=1024, tk=1408,
                       a_map=_halves_a)
        g_up = dw("w_up", s["xn2"], da, mnk=(D, 2 * D_FF, T), tn=1408, b_map=_halves_b)
        token = reduce([("w_down", l, g_down), ("w_up", l, g_up)], "ffn%d" % l)
        dh1, dmix, d_gains[l][2], d_gains[l][1] = _rms_bwd2(dxn2, s["h1"], g[2], dh, s["mix"], g[1], name="rms_bwd2")
        if l < N_A:
            datt = _matmul(dmix, W["w_o_a"][l], mode="nt", out_dtype=F32, name="mm_oa_dx", mnk=(T, A_W, D), tn=768, after=token)
            g_o = dw("w_o_a", s["att"], dmix, tm=768, tn=1024)
            do_p, dlt_p = _combine_bwd(datt, s["o3"], s["lse3"], headsum)
            dqkv = None
            for which, d in enumerate(_band_bwd(s["qkvp"], do_p, s["lse_p"], dlt_p)):
                dqkv = _rope_bwd(d, which, tabs, dqkv)
            dxn = _matmul(dqkv, W["w_qkv_a"][l], mode="nt", out_dtype=F32, name="mm_qkv_dx", mnk=(T, D, 3 * A_W), tm=1024, tn=1024, tk=3 * A_W,
                          after=[flush(dqkv)])
            g_qkv = dw("w_qkv_a", s["xn"], dqkv, tn=768)
            group = [("w_o_a", l, g_o), ("w_qkv_a", l, slabs(g_qkv, 576))]
        else:
            j = l - N_A
            do = _matmul(dmix, W["w_o_b"][j], mode="nt", out_dtype=BF16, name="mm_ob_dx", mnk=(T, D, D), after=token)
            g_o = dw("w_o_b", s["o"], dmix, tn=1024)
            dq, *fox_acc = _fox_bwd(s["q"], kv, do, c_col, c_row, fox_acc)
            dxn = _matmul(dq, W["w_q_b"][j], mode="nt", out_dtype=F32, name="mm_qb_dx", mnk=(T, D, D), after=[flush(dq)])
            g_q = dw("w_q_b", s["xn"], dq, tn=1024)
            group = [("w_o_b", j, g_o), ("w_q_b", j, g_q)]
        if l > 0 and l != N_A:
            dh, df, d_gains[l][0], d_gains[l - 1][3] = _rms_bwd2(dxn, s["h"], g[0], dh1, saved[l - 1]["f"], gains[l - 1][3],
                                                                 name="rms_bwd2")
        else:
            dh, d_gains[l][0] = _rms_bwd(dxn, s["h"], g[0], dres=dh1, out_dtype=F32, name="rms_bwd_res")
            df = None
        if l == N_A:
            dk, dv, dcq, dck = fox_acc
            dc16 = dcq[:, :, :2].transpose(1, 0, 2).reshape(T, 16) - dck[:, :2, :].reshape(16, T).T
            dzf, d_bf = _gates_bwd(jnp.pad(dc16, ((0, 0), (0, 128 - 16))), zf, bf_pad)
            dkvf = jnp.concatenate([dk.astype(BF16), dv.astype(BF16), dzf], axis=1)
            g_kvf = _matmul(kvn, dkvf, mode="tn", out_dtype=BF16, name="mm_kvf_dw", tm=512, tn=2 * D + 128)[:, :2 * D + 16]
            dkvn = _matmul(dkvf, w_kvf_pad, mode="nt", out_dtype=F32, name="mm_kvf_dx", tm=1024, tn=1024, tk=2 * D + 128)
            dh, d_kvnorm = _rms_bwd(dkvn, h_kv, kvn_g, dres=dh, out_dtype=F32, name="rms_bwd_res")
            group.append(("w_kvf", 0, slabs(g_kvf, 516)))
        token = reduce(group, "mix%d" % l)
    small_flat = jnp.concatenate([
        jnp.stack([jnp.stack(r) for r in d_gains]).reshape(-1),
        jnp.stack(d_cw).transpose(0, 2, 1, 3).reshape(-1),
        jnp.stack(d_cb).reshape(-1),
        d_kvnorm.reshape(-1), d_bf[0, :16]])
    small = jnp.pad(small_flat, (0, 2 * N_CHIPS * SMALL_ROWS * SMALL_W - small_flat.shape[0]))
    reduce([("small", 0, small.reshape(N_CHIPS, 2 * SMALL_ROWS, SMALL_W))], "small")
    return sq, dh


def _update(loss, grad_x, reduced, chip, ws, ms, vs):
    red_s = reduced.pop("small")
    buf_s = lax.dynamic_update_slice(jnp.zeros((2, N_CHIPS, SMALL_ROWS, SMALL_W), F32), red_s.reshape(2, 1, SMALL_ROWS, SMALL_W),
                                     (0, chip, 0, 0))
    (all_s,) = _allgather([buf_s], ["slab"], name="gather_small_grads")
    sflat = all_s.transpose(1, 0, 2, 3).reshape(-1)

    grads = {nm: r.reshape(ws[nm].shape) for nm, r in reduced.items()}
    o = 0
    g_gains_full = sflat[o:o + 16 * D].reshape(DEPTH, 4, D); o += 16 * D
    g_cw_full = sflat[o:o + 12 * 2 * D_FF].reshape(DEPTH, 3, 2 * D_FF); o += 12 * 2 * D_FF
    grads["conv_b"] = sflat[o:o + 4 * 2 * D_FF].reshape(DEPTH, 2 * D_FF); o += 4 * 2 * D_FF
    grads["kv_norm"] = sflat[o:o + D]; o += D
    grads["b_f"] = sflat[o:o + 16]
    grads["norm_gains"] = lax.dynamic_slice_in_dim(g_gains_full, chip * 256, 256, axis=2)
    grads["conv_w"] = lax.dynamic_slice_in_dim(g_cw_full, chip * 1408, 1408, axis=2)

    names = ["norm_gains", "w_qkv_a", "w_o_a", "w_q_b", "w_o_b", "kv_norm", "w_kvf", "b_f", "w_up", "conv_w", "conv_b", "w_down"]
    deltas, new_m, new_v = {}, {}, {}
    for nm in names:
        shp = ws[nm].shape
        two = (math.prod(shp[:-1]), shp[-1]) if len(shp) > 1 else (1, shp[0])
        d, m2, v2 = _adamw(ws[nm].reshape(two), ms[nm].reshape(two), vs[nm].reshape(two), grads[nm].reshape(two),
                           name="adamw_" + nm)
        deltas[nm], new_m[nm], new_v[nm] = d.reshape(shp), m2.reshape(shp), v2.reshape(shp)

    return (loss, grad_x, *[grads[nm] for nm in names], *[deltas[nm] for nm in names],
            *[new_m[nm] for nm in names], *[new_v[nm] for nm in names])
```

```python
import math

import jax
import jax.numpy as jnp
from jax import lax
from jax.experimental import pallas as pl
from jax.experimental.pallas import tpu as pltpu
from jax.experimental.pallas import tpu_sc as plsc

F32 = jnp.float32
BF16 = jnp.bfloat16
MESH = pl.DeviceIdType.MESH
ANY = pl.BlockSpec(memory_space=pl.ANY)

T = 2048
D = 1024
HD = 64
DEPTH = 4
N_A = 2
A_W = 768
GW = 256
DIL = (1, 4, 16)
BLK = 128
D_FF = 2816
ROPE_THETA = 500000.0
EPS = 1e-6
NEG = -1e30
N_CHIPS = 4
FQ = 256
CT = 128
VMEM_BIG = 48 * 1024 * 1024

ADAM_LR, ADAM_B1, ADAM_B2, ADAM_EPS, ADAM_WD, ADAM_STEP = 0.001, 0.9, 0.999, 1e-08, 0.01, 10

NN = (((1,), (0,)), ((), ()))
NT = (((1,), (1,)), ((), ()))
TN = (((0,), (0,)), ((), ()))


def _dot(a, b, dims):
    return lax.dot_general(a, b, dims, preferred_element_type=F32)


def _pick(dim, pref):
    if dim <= pref:
        return dim
    best = None
    for t in range(128, pref + 1, 128):
        if dim % t == 0:
            best = t
    assert best is not None, (dim, pref)
    return best


def _params(sem=None, vmem=None):
    kw = {}
    if sem is not None:
        kw["dimension_semantics"] = sem
    if vmem is not None:
        kw["vmem_limit_bytes"] = vmem
    return pltpu.CompilerParams(**kw)


def _matmul(a, b, *, mode, out_dtype, name, mnk=None, alpha=None, tm=2048, tn=512, tk=2048,
            a_map=None, b_map=None, acc_init=None, out_slab=None, out_slabs=None, out_buf=None, after=None):
    if mnk is not None:
        M, N, K = mnk
    elif mode == "nn":
        (M, K), (_, N) = a.shape, b.shape
    elif mode == "nt":
        (M, K), (N, _) = a.shape, b.shape
    else:
        (K, M), (_, N) = a.shape, b.shape
    tm, tn, tk = _pick(M, tm), _pick(N, tn), _pick(K, tk)
    nk = K // tk
    dims = {"nn": NN, "nt": NT, "tn": TN}[mode]
    after = [t for t in (after or ()) if t is not None]
    n_in = 2 + (acc_init is not None) + len(after) + (out_buf is not None)

    def body(*refs):
        a_ref, b_ref = refs[0], refs[1]
        o_ref = refs[n_in]
        k = pl.program_id(2)

        def finish(r):
            if alpha is not None:
                r = r * alpha
            o_ref[...] = r.astype(out_dtype)

        def product():
            r = _dot(a_ref[...], b_ref[...], dims)
            return r if acc_init is None else r + refs[2][...]

        if nk == 1:
            finish(product())
            return
        acc_ref = refs[n_in + 1]

        @pl.when(k == 0)
        def _():
            acc_ref[...] = product()

        @pl.when((k > 0) & (k < nk - 1))
        def _():
            acc_ref[...] += _dot(a_ref[...], b_ref[...], dims)

        @pl.when(k == nk - 1)
        def _():
            finish(acc_ref[...] + _dot(a_ref[...], b_ref[...], dims))

    a_blk = (tk, tm) if mode == "tn" else (tm, tk)
    b_blk = (tn, tk) if mode == "nt" else (tk, tn)
    if a_map is not None:
        a_spec = pl.BlockSpec((None,) + a_blk, a_map(tm, tn, tk))
    elif mode == "tn":
        a_spec = pl.BlockSpec(a_blk, lambda i, j, k: (k, i))
    else:
        a_spec = pl.BlockSpec(a_blk, lambda i, j, k: (i, k))
    if b_map is not None:
        b_spec = pl.BlockSpec((None,) + b_blk, b_map(tm, tn, tk))
    elif mode == "nt":
        b_spec = pl.BlockSpec(b_blk, lambda i, j, k: (j, k))
    else:
        b_spec = pl.BlockSpec(b_blk, lambda i, j, k: (k, j))
    ins, specs, alias = [a, b], [a_spec, b_spec], {}
    if acc_init is not None:
        ins.append(acc_init)
        specs.append(pl.BlockSpec((tm, tn), lambda i, j, k: (i, j)))
    ins += after
    specs += [ANY] * len(after)
    if out_buf is not None:
        alias = {len(ins): 0}
        ins.append(out_buf)
        specs.append(ANY)
    if out_slab is None:
        o_spec = pl.BlockSpec((tm, tn), lambda i, j, k: (i, j))
        o_shape = jax.ShapeDtypeStruct((M, N), out_dtype)
    else:
        o_spec = pl.BlockSpec((None, tm, tn), lambda i, j, k: (out_slab, i, j))
        o_shape = jax.ShapeDtypeStruct((out_slabs, M, N), out_dtype)
    return pl.pallas_call(
        body,
        grid=(M // tm, N // tn, nk),
        in_specs=specs,
        out_specs=o_spec,
        out_shape=o_shape,
        scratch_shapes=[pltpu.VMEM((tm, tn), F32)] if nk > 1 else [],
        input_output_aliases=alias,
        compiler_params=_params(("parallel", "parallel", "arbitrary"), VMEM_BIG),
        name=name,
    )(*ins)


def _slab(l, mode):
    if mode == "nt":
        return lambda tm, tn, tk: (lambda i, j, k: (l, j, k))
    return lambda tm, tn, tk: (lambda i, j, k: (l, k, j))


def _rms_fwd(x, g, *, out_dtype, name, res=None, tr=256):
    n, d = x.shape

    def body(*refs):
        x_ref, g_ref = refs[0], refs[1]
        o_ref = refs[-1]
        xv = x_ref[...].astype(F32)
        y = xv * lax.rsqrt(jnp.mean(xv * xv, axis=-1, keepdims=True) + EPS) * g_ref[...]
        if res is not None:
            y = y + refs[2][...]
        o_ref[...] = y.astype(out_dtype)

    row = pl.BlockSpec((tr, d), lambda i: (i, 0))
    vec = pl.BlockSpec((1, d), lambda i: (0, 0))
    ins = [x, g] + ([] if res is None else [res])
    specs = [row, vec] + ([] if res is None else [row])
    return pl.pallas_call(
        body, grid=(n // tr,), in_specs=specs, out_specs=row,
        out_shape=jax.ShapeDtypeStruct((n, d), out_dtype),
        compiler_params=_params(("parallel",)), name=name,
    )(*ins)


def _rms_bwd(dy, x, g, *, out_dtype, name, dres=None, tr=256):
    n, d = x.shape

    def body(*refs):
        dy_ref, x_ref, g_ref = refs[0], refs[1], refs[2]
        dx_ref, dg_ref = refs[-2], refs[-1]
        xv = x_ref[...].astype(F32)
        dyv = dy_ref[...].astype(F32)
        rstd = lax.rsqrt(jnp.mean(xv * xv, axis=-1, keepdims=True) + EPS)
        xhat = xv * rstd
        dxh = dyv * g_ref[...]
        dx = rstd * (dxh - xhat * jnp.mean(dxh * xhat, axis=-1, keepdims=True))
        if dres is not None:
            dx = dx + refs[3][...]
        dx_ref[...] = dx.astype(out_dtype)

        @pl.when(pl.program_id(0) == 0)
        def _():
            dg_ref[...] = jnp.zeros_like(dg_ref)

        dg_ref[...] += jnp.sum(dyv * xhat, axis=0, keepdims=True)

    row = pl.BlockSpec((tr, d), lambda i: (i, 0))
    vec = pl.BlockSpec((1, d), lambda i: (0, 0))
    ins = [dy, x, g] + ([] if dres is None else [dres])
    specs = [row, row, vec] + ([] if dres is None else [row])
    return pl.pallas_call(
        body, grid=(n // tr,), in_specs=specs, out_specs=[row, vec],
        out_shape=[jax.ShapeDtypeStruct((n, d), out_dtype), jax.ShapeDtypeStruct((1, d), F32)],
        compiler_params=_params(("arbitrary",)), name=name,
    )(*ins)


def _rms_res_in(x, g_res, res, g_in, *, name, tr=256):
    n, d = x.shape

    def body(x_ref, gr_ref, r_ref, gi_ref, h_ref, n_ref):
        xv = x_ref[...].astype(F32)
        h = r_ref[...] + xv * lax.rsqrt(jnp.mean(xv * xv, axis=-1, keepdims=True) + EPS) * gr_ref[...]
        h_ref[...] = h
        n_ref[...] = (h * lax.rsqrt(jnp.mean(h * h, axis=-1, keepdims=True) + EPS) * gi_ref[...]).astype(BF16)

    row = pl.BlockSpec((tr, d), lambda i: (i, 0))
    vec = pl.BlockSpec((1, d), lambda i: (0, 0))
    return pl.pallas_call(
        body, grid=(n // tr,), in_specs=[row, vec, row, vec], out_specs=[row, row],
        out_shape=[jax.ShapeDtypeStruct((n, d), F32), jax.ShapeDtypeStruct((n, d), BF16)],
        compiler_params=_params(("parallel",)), name=name,
    )(x, g_res, res, g_in)


def _rms_bwd2(dy, x, g, dres, x2, g2, *, name, tr=256):
    n, d = x.shape

    def one(dyv, xv, gv):
        rstd = lax.rsqrt(jnp.mean(xv * xv, axis=-1, keepdims=True) + EPS)
        xhat = xv * rstd
        dxh = dyv * gv
        return rstd * (dxh - xhat * jnp.mean(dxh * xhat, axis=-1, keepdims=True)), jnp.sum(dyv * xhat, axis=0, keepdims=True)

    def body(dy_ref, x_ref, g_ref, r_ref, x2_ref, g2_ref, dx_ref, d2_ref, dg_ref, dg2_ref):
        dx, dg = one(dy_ref[...].astype(F32), x_ref[...].astype(F32), g_ref[...])
        dx = dx + r_ref[...]
        dx_ref[...] = dx
        d2, dg2 = one(dx, x2_ref[...].astype(F32), g2_ref[...])
        d2_ref[...] = d2.astype(BF16)

        @pl.when(pl.program_id(0) == 0)
        def _():
            dg_ref[...] = jnp.zeros_like(dg_ref)
            dg2_ref[...] = jnp.zeros_like(dg2_ref)

        dg_ref[...] += dg
        dg2_ref[...] += dg2

    row = pl.BlockSpec((tr, d), lambda i: (i, 0))
    vec = pl.BlockSpec((1, d), lambda i: (0, 0))
    return pl.pallas_call(
        body, grid=(n // tr,), in_specs=[row, row, vec, row, row, vec], out_specs=[row, row, vec, vec],
        out_shape=[jax.ShapeDtypeStruct((n, d), F32), jax.ShapeDtypeStruct((n, d), BF16),
                   jax.ShapeDtypeStruct((1, d), F32), jax.ShapeDtypeStruct((1, d), F32)],
        compiler_params=_params(("arbitrary",)), name=name,
    )(dy, x, g, dres, x2, g2)


def _loss_head(h, target, *, tr=256):
    n, d = h.shape

    def body(h_ref, t_ref, dh_ref, s_ref):
        err = h_ref[...] - t_ref[...]
        dh_ref[...] = err * (1.0 / d)

        @pl.when(pl.program_id(0) == 0)
        def _():
            s_ref[...] = jnp.zeros_like(s_ref)

        s_ref[...] += jnp.sum(err * err)

    row = pl.BlockSpec((tr, d), lambda i: (i, 0))
    acc = pl.BlockSpec((8, 128), lambda i: (0, 0))
    return pl.pallas_call(
        body, grid=(n // tr,), in_specs=[row, row], out_specs=[row, acc],
        out_shape=[jax.ShapeDtypeStruct((n, d), F32), jax.ShapeDtypeStruct((8, 128), F32)],
        compiler_params=_params(("arbitrary",)), name="loss_head",
    )(h, target)


def _rope_tables():
    pos = jnp.arange(T, dtype=F32)
    inv = ROPE_THETA ** (-jnp.arange(0, 16, 2, dtype=F32) / 16)
    ang = pos[:, None] * inv[None, :]
    cos, sin = jnp.cos(ang), jnp.sin(ang)
    one = jnp.ones((T, HD - 16), F32)
    zero8 = jnp.zeros((T, 8), F32)
    zero = jnp.zeros((T, HD - 16), F32)
    c = jnp.concatenate([cos, cos, one], axis=1)
    s1 = jnp.concatenate([zero8, sin, zero], axis=1)
    s2 = jnp.concatenate([-sin, zero8, zero], axis=1)
    c, s1, s2 = (jnp.concatenate([t, t], axis=1) for t in (c, s1, s2))
    scale = HD ** -0.5
    return (jnp.stack([c * scale, c, jnp.ones_like(c)]), jnp.stack([s1 * scale, s1, jnp.zeros_like(c)]),
            jnp.stack([s2 * scale, s2, jnp.zeros_like(c)]))


def _row_chunks(r):
    if r == 1:
        n = 4
        return [(slice(i * (T // n), (i + 1) * (T // n)),) * 2 for i in range(n)]
    per = T // r
    return [(pl.ds(j, per, stride=r), slice(j * per, (j + 1) * per)) for j in range(r)]


def _rope_fwd(qkv, tabs):
    def body(x_ref, c_ref, s1_ref, s2_ref, o_ref):
        g = lax.rem(lax.div(pl.program_id(0), 2), 3)
        for gi, r in enumerate(DIL):
            @pl.when(g == gi)
            def _(r=r):
                for tok, prm in _row_chunks(r):
                    x = x_ref[tok, :]
                    y = x * c_ref[tok, :] + pltpu.roll(x, 8, 1) * s1_ref[tok, :] + pltpu.roll(x, 120, 1) * s2_ref[tok, :]
                    o_ref[prm, :] = y.astype(BF16)

    tab = pl.BlockSpec((None, T, 128), lambda b: (lax.div(b, 6), 0, 0))
    return pl.pallas_call(
        body, grid=(18,), in_specs=[pl.BlockSpec((T, 128), lambda b: (0, b)), tab, tab, tab],
        out_specs=pl.BlockSpec((None, T, 128), lambda b: (b, 0, 0)), out_shape=jax.ShapeDtypeStruct((18, T, 128), BF16),
        compiler_params=_params(("parallel",)), name="rope_fwd",
    )(qkv, *tabs)


def _rope_bwd(d, which, tabs, out_buf):
    def body(d_ref, c_ref, s1_ref, s2_ref, *rest):
        o_ref, tok_ref = rest[-2], rest[-1]
        g = lax.div(pl.program_id(0), 2)
        for gi, r in enumerate(DIL):
            @pl.when(g == gi)
            def _(r=r):
                for tok, prm in _row_chunks(r):
                    tok_ref[tok, :] = d_ref[prm, :]
                for rows, _ in _row_chunks(1):
                    gx = tok_ref[rows, :]
                    y = gx * c_ref[rows, :] + pltpu.roll(gx * s1_ref[rows, :], 120, 1) + pltpu.roll(gx * s2_ref[rows, :], 8, 1)
                    o_ref[rows, :] = y.astype(BF16)

    tab = pl.BlockSpec((None, T, 128), lambda b: (which, 0, 0))
    ins = [d, *tabs] + ([] if out_buf is None else [out_buf])
    specs = [pl.BlockSpec((None, None, T, 128), lambda b: (lax.div(b, 2), lax.rem(b, 2), 0, 0)), tab, tab, tab]
    return pl.pallas_call(
        body, grid=(6,), in_specs=specs + ([] if out_buf is None else [ANY]),
        out_specs=pl.BlockSpec((T, 128), lambda b: (0, 6 * which + b)),
        out_shape=jax.ShapeDtypeStruct((T, 3 * A_W), BF16), scratch_shapes=[pltpu.VMEM((T, 128), F32)],
        input_output_aliases={} if out_buf is None else {4: 0},
        compiler_params=_params(("arbitrary",)), name="rope_bwd",
    )(*ins)


def _head_mask(x, lane_lo):
    lane = lax.broadcasted_iota(jnp.int32, x.shape, 1)
    keep = (lane < HD) if lane_lo else (lane >= HD)
    return jnp.where(keep, x.astype(F32), 0.0).astype(BF16)


def _band_scalars():
    g, b = pl.program_id(0), pl.program_id(1)
    nbs = lax.shift_right_logical(jnp.int32(T // BLK), 2 * g)
    has_prev = jnp.where((b & (nbs - 1)) != 0, 1, 0)
    next_ok = jnp.where(((b + 1) & (nbs - 1)) != 0, 1, 0)
    return has_prev, next_ok


def _band_mask_q(has_prev):
    row = lax.broadcasted_iota(jnp.int32, (BLK, 2 * BLK), 0)
    col = lax.broadcasted_iota(jnp.int32, (BLK, 2 * BLK), 1)
    return ((col < BLK) & (col >= row) & (has_prev == 1)) | ((col >= BLK) & (col - BLK <= row))


def _band_mask_k(next_ok):
    row = lax.broadcasted_iota(jnp.int32, (2 * BLK, BLK), 0)
    col = lax.broadcasted_iota(jnp.int32, (2 * BLK, BLK), 1)
    return ((row < BLK) & (col <= row)) | ((row >= BLK) & (col >= row - BLK) & (next_ok == 1))


def _band_spec(base, step):
    nb = T // BLK
    at = {"cur": lambda b: b, "prev": lambda b: jnp.maximum(b - 1, 0), "next": lambda b: jnp.minimum(b + 1, nb - 1)}[step]
    return pl.BlockSpec((None, 2, BLK, 128), lambda g, b: (base + g, 0, at(b), 0))


def _band_fwd(qkv):
    nb = T // BLK

    def body(q_ref, kc_ref, kp_ref, vc_ref, vp_ref, o_ref, l_ref):
        has_prev, _ = _band_scalars()
        mask = _band_mask_q(has_prev)
        lane = lax.broadcasted_iota(jnp.int32, (BLK, 128), 1)
        for p in range(2):
            qp = q_ref[p]
            kcat = jnp.concatenate([kp_ref[p], kc_ref[p]], axis=0)
            vcat = jnp.concatenate([vp_ref[p], vc_ref[p]], axis=0)
            o_acc = jnp.zeros((BLK, 128), F32)
            lse = jnp.zeros((BLK, 128), F32)
            for e in range(2):
                s = _dot(_head_mask(qp, e == 0), kcat, NT)
                s = jnp.where(mask, s, NEG)
                m = jnp.max(s, axis=-1, keepdims=True)
                pr = jnp.exp(s - m)
                l = jnp.sum(pr, axis=-1, keepdims=True)
                o_acc = o_acc + _dot(pr.astype(BF16), _head_mask(vcat, e == 0), NN) / l
                lse = jnp.where((lane < HD) if e == 0 else (lane >= HD), m + jnp.log(l), lse)
            o_ref[p] = o_acc
            l_ref[p] = lse

    out = _band_spec(0, "cur")
    shp = jax.ShapeDtypeStruct((3, 2, T, 128), F32)
    return pl.pallas_call(
        body, grid=(3, nb),
        in_specs=[_band_spec(0, "cur"), _band_spec(3, "cur"), _band_spec(3, "prev"), _band_spec(6, "cur"), _band_spec(6, "prev")],
        out_specs=[out, out], out_shape=[shp, shp],
        compiler_params=_params(("parallel", "parallel")), name="band_fwd",
    )(qkv, qkv, qkv, qkv, qkv)


def _band_bwd(qkv, do, lse, dlt):
    nb = T // BLK

    def body(qc_ref, qn_ref, kc_ref, kp_ref, vc_ref, vp_ref, doc_ref, don_ref, lc_ref, ln_ref, dc_ref, dn_ref,
             dq_ref, dk_ref, dv_ref):
        has_prev, next_ok = _band_scalars()
        mask_q = _band_mask_q(has_prev)
        mask_k = _band_mask_k(next_ok)
        for p in range(2):
            qc, qn = qc_ref[p], qn_ref[p]
            doc, don = doc_ref[p], don_ref[p]
            kc, vc = kc_ref[p], vc_ref[p]
            kcat = jnp.concatenate([kp_ref[p], kc], axis=0)
            vcat = jnp.concatenate([vp_ref[p], vc], axis=0)
            qcat = jnp.concatenate([qc, qn], axis=0)
            docat = jnp.concatenate([doc, don], axis=0)
            dq = jnp.zeros((BLK, 128), F32)
            dk = jnp.zeros((BLK, 128), F32)
            dv = jnp.zeros((BLK, 128), F32)
            for e in range(2):
                lo = e == 0
                col = slice(HD * e, HD * e + 1)
                lse_c, lse_n = lc_ref[p, :, col], ln_ref[p, :, col]
                dl_c, dl_n = dc_ref[p, :, col], dn_ref[p, :, col]
                s = jnp.where(mask_q, _dot(_head_mask(qc, lo), kcat, NT), NEG)
                pr = jnp.exp(s - lse_c)
                dp = _dot(_head_mask(doc, lo), vcat, NT)
                ds = pr * (dp - dl_c)
                dq = dq + _dot(ds.astype(BF16), _head_mask(kcat, lo), NN)
                qm, dom = _head_mask(qcat, lo), _head_mask(docat, lo)
                s2 = jnp.where(mask_k, _dot(qm, kc, NT), NEG)
                p2 = jnp.exp(s2 - jnp.concatenate([lse_c, lse_n], axis=0))
                dv = dv + _dot(p2.astype(BF16), dom, TN)
                dp2 = _dot(dom, vc, NT)
                ds2 = p2 * (dp2 - jnp.concatenate([dl_c, dl_n], axis=0))
                dk = dk + _dot(ds2.astype(BF16), qm, TN)
            dq_ref[p] = dq
            dk_ref[p] = dk
            dv_ref[p] = dv

    cur, nxt = _band_spec(0, "cur"), _band_spec(0, "next")
    shp = jax.ShapeDtypeStruct((3, 2, T, 128), F32)
    return pl.pallas_call(
        body, grid=(3, nb),
        in_specs=[cur, nxt, _band_spec(3, "cur"), _band_spec(3, "prev"), _band_spec(6, "cur"), _band_spec(6, "prev"),
                  cur, nxt, cur, nxt, cur, nxt],
        out_specs=[cur, cur, cur], out_shape=[shp, shp, shp],
        compiler_params=_params(("parallel", "parallel")), name="band_bwd",
    )(qkv, qkv, qkv, qkv, qkv, qkv, do, do, lse, lse, dlt, dlt)


def _split3(x):
    hi = x.astype(BF16)
    r = x - hi.astype(F32)
    mid = r.astype(BF16)
    lo = (r - mid.astype(F32)).astype(BF16)
    return hi, mid, lo


def _dot3(x, m, dims=NN):
    hi, mid, lo = _split3(x)
    return _dot(hi, m, dims) + _dot(mid, m, dims) + _dot(lo, m, dims)


def _combine_weights(lses):
    l0, l1, l2 = lses
    m = jnp.maximum(jnp.maximum(l0, l1), l2)
    e = [jnp.exp(l0 - m), jnp.exp(l1 - m), jnp.exp(l2 - m)]
    inv = 1.0 / (e[0] + e[1] + e[2])
    return [ei * inv for ei in e]


CR = 256


def _combine_fwd(o, lse):
    def body(o_ref, l_ref, att_ref, o3_ref, l3_ref):
        for g, r in enumerate(DIL):
            for p in range(2):
                for tok, prm in _row_chunks(r):
                    o3_ref[g, p, tok, :] = o_ref[g, p, prm, :]
                    l3_ref[g, p, tok, :] = l_ref[g, p, prm, :]
        for i in range(T // CR):
            rows = slice(i * CR, (i + 1) * CR)
            for p in range(2):
                alpha = _combine_weights([l3_ref[g, p, rows, :] for g in range(3)])
                for g in range(3):
                    att_ref[rows, g * GW + p * 128: g * GW + (p + 1) * 128] = (o3_ref[g, p, rows, :] * alpha[g]).astype(BF16)

    shp = jax.ShapeDtypeStruct((3, 2, T, 128), F32)
    return pl.pallas_call(
        body, out_shape=[jax.ShapeDtypeStruct((T, A_W), BF16), shp, shp],
        compiler_params=_params(vmem=VMEM_BIG), name="combine_fwd",
    )(o, lse)


def _combine_bwd(datt, o3, l3, headsum):
    def body(d_ref, o_ref, l_ref, hs_ref, do_ref, dl_ref, tdo_ref, tdl_ref):
        hs = hs_ref[...]
        for p in range(2):
            for i in range(T // CR):
                rows = slice(i * CR, (i + 1) * CR)
                alpha = _combine_weights([l_ref[g, p, rows, :] for g in range(3)])
                total = jnp.zeros((CR, 128), F32)
                for g in range(3):
                    dg = d_ref[rows, g * GW + p * 128: g * GW + (p + 1) * 128]
                    tdo_ref[g, rows, :] = dg * alpha[g]
                    total = total + alpha[g] * _dot3(dg * o_ref[g, p, rows, :], hs)
                for g in range(3):
                    tdl_ref[g, rows, :] = alpha[g] * total
            for g, r in enumerate(DIL):
                for tok, prm in _row_chunks(r):
                    do_ref[g, p, prm, :] = tdo_ref[g, tok, :].astype(BF16)
                    dl_ref[g, p, prm, :] = tdl_ref[g, tok, :]

    return pl.pallas_call(
        body, out_shape=[jax.ShapeDtypeStruct((3, 2, T, 128), BF16), jax.ShapeDtypeStruct((3, 2, T, 128), F32)],
        scratch_shapes=[pltpu.VMEM((3, T, 128), F32), pltpu.VMEM((3, T, 128), F32)],
        compiler_params=_params(vmem=VMEM_BIG), name="combine_bwd",
    )(datt, o3, l3, headsum)


def _fox_scores(qm, k_ref, ck_ref, e, i, n):
    s = _dot(qm, k_ref[0:n, :], NT) - ck_ref[0, e:e + 1, 0:n]
    row = lax.broadcasted_iota(jnp.int32, (FQ, n), 0)
    col = lax.broadcasted_iota(jnp.int32, (FQ, n), 1)
    s = jnp.where(col <= row + i * FQ, s, NEG)
    m = jnp.max(s, axis=-1, keepdims=True)
    pr = jnp.exp(s - m)
    return pr, jnp.sum(pr, axis=-1, keepdims=True)


def _fox_fwd(q, kv, c_row):
    def body(q_ref, k_ref, v_ref, cr_ref, o_ref, vm_ref):
        for e in range(2):
            vm_ref[e] = _head_mask(v_ref[...], e == 0)
        for i in range(T // FQ):
            n = (i + 1) * FQ
            rows = slice(i * FQ, n)
            acc = jnp.zeros((FQ, 128), F32)
            for e in range(2):
                qm = _head_mask(q_ref[rows, :], e == 0)
                pr, l = _fox_scores(qm, k_ref, cr_ref, e, i, n)
                acc = acc + _dot(pr.astype(BF16), vm_ref[e, 0:n, :], NN) / l
            o_ref[rows, :] = acc.astype(BF16)

    pair = pl.BlockSpec((T, 128), lambda p: (0, p))
    return pl.pallas_call(
        body, grid=(D // 128,),
        in_specs=[pair, pair, pl.BlockSpec((T, 128), lambda p: (0, D // 128 + p)), pl.BlockSpec((1, 2, T), lambda p: (p, 0, 0))],
        out_specs=pair, out_shape=jax.ShapeDtypeStruct((T, D), BF16),
        scratch_shapes=[pltpu.VMEM((2, T, 128), BF16)],
        compiler_params=_params(("parallel",), VMEM_BIG), name="fox_fwd",
    )(q, kv, kv, c_row)


def _fox_bwd(q, kv, do, c_row, init):
    def body(q_ref, k_ref, v_ref, do_ref, cr_ref, ik_ref, iv_ref, ic_ref, dq_ref, dk_ref, dv_ref, dck_ref, km_ref):
        dk_ref[...] = ik_ref[...]
        dv_ref[...] = iv_ref[...]
        dck_ref[...] = ic_ref[...]
        for e in range(2):
            km_ref[e] = _head_mask(k_ref[...], e == 0)
        for i in range(T // FQ):
            n = (i + 1) * FQ
            rows = slice(i * FQ, n)
            dq = jnp.zeros((FQ, 128), F32)
            for e in range(2):
                qm = _head_mask(q_ref[rows, :], e == 0)
                dom = _head_mask(do_ref[rows, :], e == 0)
                pr, l = _fox_scores(qm, k_ref, cr_ref, e, i, n)
                pr = pr / l
                dp = _dot(dom, v_ref[0:n, :], NT)
                ds = pr * (dp - jnp.sum(pr * dp, axis=-1, keepdims=True))
                dsb = ds.astype(BF16)
                dq = dq + _dot(dsb, km_ref[e, 0:n, :], NN)
                dk_ref[0:n, :] += _dot(dsb, qm, TN)
                dv_ref[0:n, :] += _dot(pr.astype(BF16), dom, TN)
                dck_ref[0, e:e + 1, 0:n] += jnp.sum(ds, axis=0, keepdims=True)
            dq_ref[rows, :] = (dq * HD ** -0.5).astype(BF16)

    pair = pl.BlockSpec((T, 128), lambda p: (0, p))
    ck = pl.BlockSpec((1, 8, T), lambda p: (p, 0, 0))
    return pl.pallas_call(
        body, grid=(D // 128,),
        in_specs=[pair, pair, pl.BlockSpec((T, 128), lambda p: (0, D // 128 + p)), pair,
                  pl.BlockSpec((1, 2, T), lambda p: (p, 0, 0)), pair, pair, ck],
        out_specs=[pair, pair, pair, ck],
        out_shape=[jax.ShapeDtypeStruct((T, D), BF16), jax.ShapeDtypeStruct((T, D), F32), jax.ShapeDtypeStruct((T, D), F32),
                   jax.ShapeDtypeStruct((D // 128, 8, T), F32)],
        scratch_shapes=[pltpu.VMEM((2, T, 128), BF16)],
        compiler_params=_params(("parallel",), VMEM_BIG), name="fox_bwd",
    )(q, kv, kv, do, c_row, *init)


def _tri(lower):
    r = lax.broadcasted_iota(jnp.int32, (BLK, BLK), 0)
    c = lax.broadcasted_iota(jnp.int32, (BLK, BLK), 1)
    return jnp.where((c <= r) if lower else (c >= r), 1.0, 0.0).astype(BF16)


def _gates_fwd(z, b):
    def body(z_ref, b_ref, c_ref):
        tri = _tri(True)
        carry = jnp.zeros((1, 128), F32)
        for i in range(T // BLK):
            rows = slice(i * BLK, (i + 1) * BLK)
            x = z_ref[rows, :] + b_ref[...]
            logf = jnp.minimum(x, 0.0) - jnp.log(1.0 + jnp.exp(-jnp.abs(x)))
            hi, mid, lo = _split3(logf)
            y = _dot(tri, hi, NN) + _dot(tri, mid, NN) + _dot(tri, lo, NN) + carry
            c_ref[rows, :] = y
            carry = y[BLK - 1:BLK, :]

    return pl.pallas_call(body, out_shape=jax.ShapeDtypeStruct((T, 128), F32), name="gates_fwd")(z, b)


def _gates_bwd(dc, z, b):
    def body(dc_ref, z_ref, b_ref, dz_ref, db_ref):
        tri = _tri(False)
        carry = jnp.zeros((1, 128), F32)
        db = jnp.zeros((1, 128), F32)
        for i in reversed(range(T // BLK)):
            rows = slice(i * BLK, (i + 1) * BLK)
            hi, mid, lo = _split3(dc_ref[rows, :])
            dlogf = _dot(tri, hi, NN) + _dot(tri, mid, NN) + _dot(tri, lo, NN) + carry
            carry = dlogf[0:1, :]
            x = z_ref[rows, :] + b_ref[...]
            dz = dlogf / (1.0 + jnp.exp(x))
            dz_ref[rows, :] = dz.astype(BF16)
            db = db + jnp.sum(dz, axis=0, keepdims=True)
        db_ref[...] = db

    return pl.pallas_call(
        body, out_shape=[jax.ShapeDtypeStruct((T, 128), BF16), jax.ShapeDtypeStruct((1, 128), F32)], name="gates_bwd",
    )(dc, z, b)


def _conv_pair(a_refs, cw_refs, cb_refs):
    row = lax.broadcasted_iota(jnp.int32, (T, CT), 0)
    outs = []
    for a_ref, cw_ref, cb_ref in zip(a_refs, cw_refs, cb_refs):
        z = a_ref[...]
        z1 = jnp.where(row >= 1, pltpu.roll(z, 1, 0), 0.0)
        z2 = jnp.where(row >= 2, pltpu.roll(z, 2, 0), 0.0)
        y = cw_ref[2:3, :] * z + cw_ref[1:2, :] * z1 + cw_ref[0:1, :] * z2 + cb_ref[...]
        outs.append((y, z, z1, z2))
    return outs


_GELU_K = math.sqrt(2.0 / math.pi)
N_CT = D_FF // CT


def _conv_specs():
    def at(rows, off):
        return pl.BlockSpec((rows, CT), lambda j: (0, j + off))
    return [at(T, 0), at(T, N_CT), at(3, 0), at(3, N_CT), at(1, 0), at(1, N_CT)]


def _convgate_fwd(a, cw, cb):
    def body(ag_ref, av_ref, wg_ref, wv_ref, bg_ref, bv_ref, u_ref):
        (g, _, _, _), (v, _, _, _) = _conv_pair((ag_ref, av_ref), (wg_ref, wv_ref), (bg_ref, bv_ref))
        th = jnp.tanh(_GELU_K * (g + 0.044715 * g * g * g))
        u_ref[...] = (0.5 * g * (1.0 + th) * v).astype(BF16)

    return pl.pallas_call(
        body, grid=(N_CT,), in_specs=_conv_specs(),
        out_specs=pl.BlockSpec((T, CT), lambda j: (0, j)), out_shape=jax.ShapeDtypeStruct((T, D_FF), BF16),
        compiler_params=_params(("parallel",), VMEM_BIG), name="convgate_fwd",
    )(a, a, cw, cw, cb, cb)


def _convgate_bwd(a, du, cw, cb):
    def body(ag_ref, av_ref, wg_ref, wv_ref, bg_ref, bv_ref, du_ref, da_ref, dcw_ref, dcb_ref):
        (g, gz, gz1, gz2), (v, vz, vz1, vz2) = _conv_pair((ag_ref, av_ref), (wg_ref, wv_ref), (bg_ref, bv_ref))
        du = du_ref[...].astype(F32)
        th = jnp.tanh(_GELU_K * (g + 0.044715 * g * g * g))
        gelu = 0.5 * g * (1.0 + th)
        dgelu = 0.5 * (1.0 + th) + 0.5 * g * (1.0 - th * th) * _GELU_K * (1.0 + 3 * 0.044715 * g * g)
        row = lax.broadcasted_iota(jnp.int32, (T, CT), 0)
        for h, (d, z, z1, z2, w_ref) in enumerate(((du * v * dgelu, gz, gz1, gz2, wg_ref), (du * gelu, vz, vz1, vz2, wv_ref))):
            d1 = jnp.where(row < T - 1, pltpu.roll(d, T - 1, 0), 0.0)
            d2 = jnp.where(row < T - 2, pltpu.roll(d, T - 2, 0), 0.0)
            da_ref[h] = (w_ref[2:3, :] * d + w_ref[1:2, :] * d1 + w_ref[0:1, :] * d2).astype(BF16)
            dcw_ref[h, 0:1, :] = jnp.sum(d * z2, axis=0, keepdims=True)
            dcw_ref[h, 1:2, :] = jnp.sum(d * z1, axis=0, keepdims=True)
            dcw_ref[h, 2:3, :] = jnp.sum(d * z, axis=0, keepdims=True)
            dcb_ref[h] = jnp.sum(d, axis=0, keepdims=True)

    def both(rows):
        return pl.BlockSpec((2, rows, CT), lambda j: (0, 0, j))

    return pl.pallas_call(
        body, grid=(N_CT,),
        in_specs=_conv_specs() + [pl.BlockSpec((T, CT), lambda j: (0, j))],
        out_specs=[both(T), both(3), both(1)],
        out_shape=[jax.ShapeDtypeStruct((2, T, D_FF), BF16), jax.ShapeDtypeStruct((2, 3, D_FF), F32),
                   jax.ShapeDtypeStruct((2, 1, D_FF), F32)],
        compiler_params=_params(("parallel",), VMEM_BIG), name="convgate_bwd",
    )(a, a, cw, cw, cb, cb, du)


def _halves_a(tm, tn, tk):
    per = D_FF // tk
    return lambda i, j, k: (lax.div(k, per), i, lax.rem(k, per))


def _halves_b(tm, tn, tk):
    per = D_FF // tn
    return lambda i, j, k: (lax.div(j, per), k, lax.rem(j, per))


def _adamw(w, m, v, g, *, name):
    r, c = w.shape
    tr = r
    if r * c > 256 * 1024:
        for cand in range(8, r, 8):
            if r % cand == 0 and cand * c <= 256 * 1024:
                tr = cand

    def body(w_ref, m_ref, v_ref, g_ref, d_ref, nm_ref, nv_ref):
        gv = g_ref[...]
        mn = ADAM_B1 * m_ref[...] + (1.0 - ADAM_B1) * gv
        vn = ADAM_B2 * v_ref[...] + (1.0 - ADAM_B2) * (gv * gv)
        m_hat = mn / (1.0 - ADAM_B1 ** ADAM_STEP)
        v_hat = vn / (1.0 - ADAM_B2 ** ADAM_STEP)
        d_ref[...] = -ADAM_LR * (m_hat / (jnp.sqrt(v_hat) + ADAM_EPS) + ADAM_WD * w_ref[...])
        nm_ref[...] = mn
        nv_ref[...] = vn

    blk = pl.BlockSpec((tr, c), lambda i: (i, 0))
    shp = jax.ShapeDtypeStruct((r, c), F32)
    return pl.pallas_call(
        body, grid=(r // tr,), in_specs=[blk] * 4, out_specs=[blk] * 3, out_shape=[shp] * 3,
        compiler_params=_params(("parallel",)), name=name,
    )(w, m, v, g)


def _place():
    x, y, c = lax.axis_index("x"), lax.axis_index("y"), lax.axis_index("c")
    chips = [(1 - x, y), (x, 1 - y), (1 - x, 1 - y)]
    return x, y, c, chips


def _window(ref, kind, s, half=None):
    lead = () if half is None else (half,)
    b, c = ref.shape[-2], ref.shape[-1]
    if kind == "col":
        return ref.at[lead + (slice(None), slice(None), pl.ds(s * (c // N_CHIPS), c // N_CHIPS))]
    if kind == "row":
        return ref.at[lead + (slice(None), pl.ds(s * (b // N_CHIPS), b // N_CHIPS), slice(None))]
    return ref.at[lead + (s,)]


def _window_shape(shape3, kind):
    a, b, c = shape3
    return {"col": (a, b, c // N_CHIPS), "row": (a, b // N_CHIPS, c), "slab": (b, c)}[kind]


def _allgather(tensors, kinds, *, name):
    n = len(tensors)

    def body(*refs):
        bufs = refs[n:2 * n]
        send, recv = refs[2 * n:]
        x, y, c, chips = _place()
        me = 2 * x + y
        sib = (x, y, 1 - c)

        def rcopy(i, k, win, to):
            return pltpu.make_async_remote_copy(src_ref=win, dst_ref=win, send_sem=send.at[i * 6 + k], recv_sem=recv.at[i * 6 + k],
                                                device_id=to, device_id_type=MESH)

        started = []
        for i in range(n):
            for k, (px, py) in enumerate(chips):
                cp = rcopy(i, k, _window(bufs[i], kinds[i], me, c), (px, py, c))
                cp.start()
                started.append(cp)
        for i in range(n):
            for k, (px, py) in enumerate(chips):
                landed = _window(bufs[i], kinds[i], 2 * px + py, c)
                rcopy(i, k, landed, (px, py, c)).wait_recv()
                fw = rcopy(i, 3 + k, landed, sib)
                fw.start()
                started.append(fw)
        for i in range(n):
            for k, (px, py) in enumerate(chips):
                rcopy(i, 3 + k, _window(bufs[i], kinds[i], 2 * px + py, 1 - c), sib).wait_recv()
        for cp in started:
            cp.wait_send()

    return pl.pallas_call(
        body, in_specs=[ANY] * n, out_specs=[ANY] * n,
        out_shape=[jax.ShapeDtypeStruct(t.shape, t.dtype) for t in tensors],
        scratch_shapes=[pltpu.SemaphoreType.DMA((6 * n,)), pltpu.SemaphoreType.DMA((6 * n,))],
        input_output_aliases={i: i for i in range(n)},
        name=name,
    )(*tensors)


def _rows_tile(rows, cols, sub):
    best = None
    for t in range(sub, rows + 1, sub):
        if rows % t == 0 and t * cols <= 512 * 1024:
            best = t
    return rows if best is None else best


def _sequencer(name, cid, n_sems, peers_of, body):
    @pl.kernel(mesh=plsc.ScalarSubcoreMesh(axis_name="seq", num_cores=1), name=name,
               scratch_types=(pltpu.SemaphoreType.DMA((n_sems,)), pltpu.SemaphoreType.DMA((n_sems,))),
               compiler_params=pltpu.CompilerParams(collective_id=cid))
    def launch(send, recv):
        x, y, c, chips = _place()
        peers = peers_of(x, y, c, chips)
        barrier = pltpu.get_barrier_semaphore()
        for peer in peers:
            pl.semaphore_signal(barrier, inc=1, device_id=peer, device_id_type=MESH)
        pl.semaphore_wait(barrier, len(peers))
        body(send, recv)

    launch()


def _half_of_full(ref, kind, h):
    if kind == "col":
        b = ref.shape[0]
        return ref.at[pl.ds(h * (b // 2), b // 2), :]
    if kind == "row":
        c = ref.shape[1]
        return ref.at[:, pl.ds(h * (c // 2), c // 2)]
    b = ref.shape[1]
    return ref.at[:, pl.ds(h * (b // 2), b // 2), :]


def _half_shape(full, kind):
    if kind == "col":
        return (full[0] // 2, full[1])
    if kind == "row":
        return (full[0], full[1] // 2)
    return (full[0], full[1] // 2, full[2])


def _win_of_half(ref, kind, s):
    if kind == "col":
        c = ref.shape[1]
        return ref.at[:, pl.ds(s * (c // N_CHIPS), c // N_CHIPS)]
    if kind == "row":
        b = ref.shape[0]
        return ref.at[pl.ds(s * (b // N_CHIPS), b // N_CHIPS), :]
    return ref.at[s]


def _win_shape(half, kind):
    if kind == "col":
        return (half[0], half[1] // N_CHIPS)
    if kind == "row":
        return (half[0] // N_CHIPS, half[1])
    return half[1:]


def _seq_swap(parts, kinds, *, name):
    n = len(parts)
    srcs = [jax.new_ref(p, memory_space=pltpu.MemorySpace.HBM) for p in parts]
    outs = [jax.empty_ref(jax.ShapeDtypeStruct(_half_shape(p.shape, k), p.dtype), memory_space=pltpu.MemorySpace.HBM)
            for p, k in zip(parts, kinds)]

    def body(send, recv):
        x, y, c, _ = _place()
        cps = []
        for i in range(n):
            cp = pltpu.make_async_remote_copy(src_ref=_half_of_full(srcs[i], kinds[i], 1 - c), dst_ref=outs[i], send_sem=send.at[i],
                                              recv_sem=recv.at[i], device_id=(x, y, 1 - c), device_id_type=MESH)
            cp.start()
            cps.append(cp)
        for cp in cps:
            cp.wait()

    _sequencer(name, 2, n, lambda x, y, c, chips: [(x, y, 1 - c)], body)
    return [o[...] for o in outs]


def _seq_scatter(halves, kinds, *, name):
    n = len(halves)
    srcs = [jax.new_ref(h, memory_space=pltpu.MemorySpace.HBM) for h in halves]
    outs = [jax.empty_ref(jax.ShapeDtypeStruct((3,) + _win_shape(h.shape, k), h.dtype), memory_space=pltpu.MemorySpace.HBM)
            for h, k in zip(halves, kinds)]

    def body(send, recv):
        x, y, c, chips = _place()
        cps = []
        for i in range(n):
            for k, (px, py) in enumerate(chips):
                cp = pltpu.make_async_remote_copy(src_ref=_win_of_half(srcs[i], kinds[i], 2 * px + py), dst_ref=outs[i].at[k],
                                                  send_sem=send.at[3 * i + k], recv_sem=recv.at[3 * i + k],
                                                  device_id=(px, py, c), device_id_type=MESH)
                cp.start()
                cps.append(cp)
        for cp in cps:
            cp.wait()

    _sequencer(name, 3, 3 * n, lambda x, y, c, chips: [(px, py, c) for px, py in chips], body)
    return [o[...] for o in outs]


def _add_half(g, p, kind, where, after, *, name):
    if kind == "slab":
        s, b2, c = p.shape
        tr = _rows_tile(b2, c, 16)
        nr = b2 // tr
        grid = (s, nr)
        g_spec = pl.BlockSpec((None, tr, c), lambda i, r, w: (i, w[1] * nr + r, 0))
        p_spec = pl.BlockSpec((None, tr, c), lambda i, r, w: (i, r, 0))
    elif kind == "col":
        b2, c = p.shape
        tr = _rows_tile(b2, c, 16)
        nr = b2 // tr
        grid = (1, nr)
        g_spec = pl.BlockSpec((tr, c), lambda i, r, w: (w[1] * nr + r, 0))
        p_spec = pl.BlockSpec((tr, c), lambda i, r, w: (r, 0))
    else:
        b, c2 = p.shape
        tr = _rows_tile(b, c2, 16)
        grid = (1, b // tr)
        g_spec = pl.BlockSpec((tr, c2), lambda i, r, w: (r, w[1]))
        p_spec = pl.BlockSpec((tr, c2), lambda i, r, w: (r, 0))

    def body(w_ref, g_ref, p_ref, *rest):
        o_ref = rest[-1]
        o_ref[...] = (g_ref[...].astype(F32) + p_ref[...].astype(F32)).astype(o_ref.dtype)

    extra = [] if after is None else [after]
    return pl.pallas_call(
        body,
        grid_spec=pltpu.PrefetchScalarGridSpec(num_scalar_prefetch=1, grid=grid, in_specs=[g_spec, p_spec] + [ANY] * len(extra),
                                               out_specs=p_spec),
        out_shape=jax.ShapeDtypeStruct(p.shape, g.dtype),
        compiler_params=_params(("parallel", "parallel")), name=name,
    )(where, g, p, *extra)


def _sum_chips(r, h, kind, where, layer, layers, out_buf, after, *, name):
    _, br, cr = r.shape
    tr = _rows_tile(br, cr, 16)
    nr = br // tr
    if kind == "col":
        h_spec = pl.BlockSpec((tr, cr), lambda j, w: (j, w[0]))
        o_shape, o_spec = (layers, 2 * br, cr), pl.BlockSpec((None, tr, cr), lambda j, w: (layer, w[1] * nr + j, 0))
    elif kind == "row":
        h_spec = pl.BlockSpec((tr, cr), lambda j, w: (w[0] * nr + j, 0))
        o_shape, o_spec = (layers, br, 2 * cr), pl.BlockSpec((None, tr, cr), lambda j, w: (layer, j, w[1]))
    else:
        h_spec = pl.BlockSpec((None, tr, cr), lambda j, w: (w[0], j, 0))
        o_shape, o_spec = (layers, 2 * br, cr), pl.BlockSpec((None, tr, cr), lambda j, w: (layer, w[1] * nr + j, 0))

    def body(w_ref, h_ref, r0_ref, r1_ref, r2_ref, *rest):
        o_ref, t_ref = rest[-2], rest[-1]
        o_ref[...] = ((h_ref[...].astype(F32) + r0_ref[...].astype(F32)) + r1_ref[...].astype(F32)) + r2_ref[...].astype(F32)
        t_ref[...] = jnp.zeros_like(t_ref)

    def slot(k):
        return pl.BlockSpec((None, tr, cr), lambda j, w: (k, j, 0))

    ins, specs, alias = [h, r, r, r], [h_spec, slot(0), slot(1), slot(2)], {}
    if after is not None:
        ins.append(after)
        specs.append(ANY)
    if out_buf is not None:
        alias = {1 + len(ins): 0}
        ins.append(out_buf)
        specs.append(ANY)
    return pl.pallas_call(
        body,
        grid_spec=pltpu.PrefetchScalarGridSpec(num_scalar_prefetch=1, grid=(nr,), in_specs=specs,
                                               out_specs=[o_spec, pl.BlockSpec((8, 128), lambda j, w: (0, 0))]),
        out_shape=[jax.ShapeDtypeStruct(o_shape, F32), jax.ShapeDtypeStruct((8, 128), F32)], input_output_aliases=alias,
        compiler_params=_params(("arbitrary",)), name=name,
    )(where, *ins)


def _join_halves(tensors, kinds, *, name):
    n = len(tensors)

    def mine(ref, kind, h):
        if kind == "row":
            c = ref.shape[2]
            return ref.at[:, :, pl.ds(h * (c // 2), c // 2)]
        b = ref.shape[1]
        return ref.at[:, pl.ds(h * (b // 2), b // 2), :]

    def body(*refs):
        bufs = refs[n:2 * n]
        send, recv = refs[2 * n:]
        x, y, c, _ = _place()
        cps = []
        for i in range(n):
            part = mine(bufs[i], kinds[i], c)
            cp = pltpu.make_async_remote_copy(src_ref=part, dst_ref=part, send_sem=send.at[i],
                                              recv_sem=recv.at[i], device_id=(x, y, 1 - c), device_id_type=MESH)
            cp.start()
            cps.append(cp)
        for i in range(n):
            other = mine(bufs[i], kinds[i], 1 - c)
            pltpu.make_async_remote_copy(src_ref=other, dst_ref=other, send_sem=send.at[i],
                                         recv_sem=recv.at[i], device_id=(x, y, 1 - c), device_id_type=MESH).wait_recv()
        for cp in cps:
            cp.wait_send()

    return pl.pallas_call(
        body, in_specs=[ANY] * n, out_specs=[ANY] * n,
        out_shape=[jax.ShapeDtypeStruct(t.shape, t.dtype) for t in tensors],
        scratch_shapes=[pltpu.SemaphoreType.DMA((n,)), pltpu.SemaphoreType.DMA((n,))],
        input_output_aliases={i: i for i in range(n)},
        name=name,
    )(*tensors)


def _win(ref, kind, s, h=None):
    if kind == "col":
        b, c = ref.shape
        cols = pl.ds(s * (c // N_CHIPS), c // N_CHIPS)
        return ref.at[:, cols] if h is None else ref.at[pl.ds(h * (b // 2), b // 2), cols]
    if kind == "row":
        b, c = ref.shape
        rows = pl.ds(s * (b // N_CHIPS), b // N_CHIPS)
        return ref.at[rows, :] if h is None else ref.at[rows, pl.ds(h * (c // 2), c // 2)]
    b = ref.shape[1]
    return ref.at[s] if h is None else ref.at[s, pl.ds(h * (b // 2), b // 2)]


def _half(ref, kind, h):
    b, c = ref.shape
    if kind == "row":
        return ref.at[:, pl.ds(h * (c // 2), c // 2)]
    return ref.at[pl.ds(h * (b // 2), b // 2), :]


def _full_shape(shard_shape, kind):
    b, c = shard_shape
    return {"col": (b, N_CHIPS * c), "row": (N_CHIPS * b, c), "slab": (N_CHIPS, b, c)}[kind]


def _gather_body(srcs, outs, kinds, send, recv):
    x, y, c, chips = _place()
    me = 2 * x + y
    sib = (x, y, 1 - c)

    def rcopy(i, k, src, dst, to):
        return pltpu.make_async_remote_copy(src_ref=src, dst_ref=dst, send_sem=send.at[7 * i + k], recv_sem=recv.at[7 * i + k],
                                            device_id=to, device_id_type=MESH)

    started = []
    for i, (src, out, kind) in enumerate(zip(srcs, outs, kinds)):
        own = rcopy(i, 6, src, _win(out, kind, me), sib)
        own.start()
        started.append(own)
        for k, (px, py) in enumerate(chips):
            cp = rcopy(i, k, _half(src, kind, c), _win(out, kind, me, c), (px, py, c))
            cp.start()
            started.append(cp)
    for i, (out, kind) in enumerate(zip(outs, kinds)):
        for k, (px, py) in enumerate(chips):
            landed = _win(out, kind, 2 * px + py, c)
            rcopy(i, k, landed, landed, (px, py, c)).wait_recv()
            fw = rcopy(i, 3 + k, landed, landed, sib)
            fw.start()
            started.append(fw)
    for i, (src, out, kind) in enumerate(zip(srcs, outs, kinds)):
        for k, (px, py) in enumerate(chips):
            other = _win(out, kind, 2 * px + py, 1 - c)
            rcopy(i, 3 + k, other, other, sib).wait_recv()
        rcopy(i, 6, src, _win(out, kind, me), sib).wait_recv()
    for cp in started:
        cp.wait_send()


def _seq_gather(shards, kinds, *, name, cid):
    n = len(shards)
    srcs = [jax.new_ref(s, memory_space=pltpu.MemorySpace.HBM) for s in shards]
    outs = [jax.empty_ref(jax.ShapeDtypeStruct(_full_shape(s.shape, k), s.dtype), memory_space=pltpu.MemorySpace.HBM)
            for s, k in zip(shards, kinds)]

    @pl.kernel(mesh=plsc.ScalarSubcoreMesh(axis_name="seq", num_cores=1), name=name,
               scratch_types=(pltpu.SemaphoreType.DMA((7 * n,)), pltpu.SemaphoreType.DMA((7 * n,))),
               compiler_params=pltpu.CompilerParams(collective_id=cid))
    def launch(send, recv):
        x, y, c, chips = _place()
        barrier = pltpu.get_barrier_semaphore()
        for px, py in chips:
            pl.semaphore_signal(barrier, inc=1, device_id=(px, py, c), device_id_type=MESH)
        pl.semaphore_signal(barrier, inc=1, device_id=(x, y, 1 - c), device_id_type=MESH)
        pl.semaphore_wait(barrier, 4)
        _gather_body(srcs, outs, kinds, send, recv)

    launch()
    return [o[...] for o in outs]


KIND = dict(w_qkv_a="slab", w_o_a="col", w_q_b="row", w_o_b="row", w_kvf="slab", w_up="col", w_down="row", small="slab")
LAYERS = dict(w_qkv_a=N_A, w_o_a=N_A, w_q_b=DEPTH - N_A, w_o_b=DEPTH - N_A, w_kvf=1, w_up=DEPTH, w_down=DEPTH, small=1)
SMALL_W = 1792
SMALL_ROWS = 8


class _Reducer:
    def __init__(self, where):
        self.where = where
        self.acc = {nm: None for nm in KIND}
        self.pending = None

    def __call__(self, group, tag):
        names, layers, parts = zip(*group)
        kinds = [KIND[nm] for nm in names]
        summed = self._sum_pending(after=parts[-1])
        sib = _seq_swap(list(parts), kinds, name="reduce_swap_" + tag)
        halves = []
        for g, p, k, nm in zip(parts, sib, kinds, names):
            halves.append(_add_half(g, p, k, self.where, halves[-1] if halves else None, name="reduce_add_" + nm))
        landed = _seq_scatter(halves, kinds, name="reduce_scatter_" + tag)
        self.pending = (names, layers, landed, halves, kinds)
        return [halves[-1], summed]

    def flush(self, after):
        return self._sum_pending(after)

    def _sum_pending(self, after):
        if self.pending is None:
            return None
        for nm, l, r, h, k in zip(*self.pending):
            self.acc[nm], after = _sum_chips(r, h, k, self.where, l, LAYERS[nm], self.acc[nm], after, name="reduce_sum_" + nm)
        self.pending = None
        return after

    def finish(self):
        self._sum_pending(after=None)
        names = list(KIND)
        joined = _join_halves([self.acc[nm] for nm in names], [KIND[nm] for nm in names], name="reduce_pair_join")
        return dict(zip(names, joined))


def _headsum_matrix():
    r = lax.broadcasted_iota(jnp.int32, (128, 128), 0) // HD
    c = lax.broadcasted_iota(jnp.int32, (128, 128), 1) // HD
    return jnp.where(r == c, 1.0, 0.0).astype(BF16)


def kernel(x, norm_gains, w_qkv_a, w_o_a, w_q_b, w_o_b, kv_norm, w_kvf, b_f, w_up, conv_w, conv_b, w_down, loss_target, m_norm_gains, m_w_qkv_a, m_w_o_a, m_w_q_b, m_w_o_b, m_kv_norm, m_w_kvf, m_b_f, m_w_up, m_conv_w, m_conv_b, m_w_down, v_norm_gains, v_w_qkv_a, v_w_o_a, v_w_q_b, v_w_o_b, v_kv_norm, v_w_kvf, v_b_f, v_w_up, v_conv_w, v_conv_b, v_w_down):
    xi, yi, ci = lax.axis_index("x"), lax.axis_index("y"), lax.axis_index("c")
    chip = 2 * xi + yi
    where = jnp.stack([chip, ci]).astype(jnp.int32)
    ws = dict(norm_gains=norm_gains, w_qkv_a=w_qkv_a, w_o_a=w_o_a, w_q_b=w_q_b, w_o_b=w_o_b, kv_norm=kv_norm, w_kvf=w_kvf,
              b_f=b_f, w_up=w_up, conv_w=conv_w, conv_b=conv_b, w_down=w_down)
    ms = dict(norm_gains=m_norm_gains, w_qkv_a=m_w_qkv_a, w_o_a=m_w_o_a, w_q_b=m_w_q_b, w_o_b=m_w_o_b, kv_norm=m_kv_norm,
              w_kvf=m_w_kvf, b_f=m_b_f, w_up=m_w_up, conv_w=m_conv_w, conv_b=m_conv_b, w_down=m_w_down)
    vs = dict(norm_gains=v_norm_gains, w_qkv_a=v_w_qkv_a, w_o_a=v_w_o_a, w_q_b=v_w_q_b, w_o_b=v_w_o_b, kv_norm=v_kv_norm,
              w_kvf=v_w_kvf, b_f=v_b_f, w_up=v_w_up, conv_w=v_conv_w, conv_b=v_conv_b, w_down=v_w_down)

    small = jnp.concatenate([
        jnp.pad(norm_gains.reshape(16, 256), ((0, 0), (0, 1408 - 256))),
        jnp.pad(conv_w.reshape(12, 1408), ((0, 4), (0, 0)))], axis=0)
    big = [nm for nm in KIND if nm != "small"]
    half = {nm: ws[nm].astype(BF16) for nm in big}
    W = {nm: [None] * LAYERS[nm] for nm in big if nm != "w_kvf"}
    g_small = None
    groups = [("0a", [("w_qkv_a", 0), ("w_o_a", 0), ("small", 0)]), ("0b", [("w_up", 0)]), ("0c", [("w_down", 0)]),
              ("1", [("w_qkv_a", 1), ("w_o_a", 1), ("w_up", 1), ("w_down", 1)]),
              ("2", [("w_kvf", 0), ("w_q_b", 0), ("w_o_b", 0), ("w_up", 2), ("w_down", 2)]),
              ("3", [("w_q_b", 1), ("w_o_b", 1), ("w_up", 3), ("w_down", 3)])]
    for tag, group in groups:
        shards = [small if nm == "small" else half[nm] if nm == "w_kvf" else half[nm][i] for nm, i in group]
        got = _seq_gather(shards, [KIND[nm] for nm, _ in group], name="gather_layer" + tag, cid=1)
        for (nm, i), g in zip(group, got):
            if nm == "small":
                g_small = g
            elif nm == "w_kvf":
                W[nm] = g.transpose(1, 0, 2).reshape(D, 2 * D + 16)
            else:
                W[nm][i] = g.transpose(1, 0, 2).reshape(D, 3 * A_W) if nm == "w_qkv_a" else g
    gains = g_small[:, :16, :256].transpose(1, 0, 2).reshape(DEPTH, 4, 1, D)
    cw_full = g_small[:, 16:28, :].transpose(1, 0, 2).reshape(DEPTH, 3, 2 * D_FF)
    cb_full = conv_b.reshape(DEPTH, 1, 2 * D_FF)

    reducer = _Reducer(where)
    sq, dh = _fwd_bwd(x[0], loss_target[0], W, gains, cw_full, cb_full, kv_norm, b_f, reducer)
    loss = lax.psum(sq[0, 0] * (0.5 / D), ("x", "y", "c"))
    return _update(loss, dh[None], reducer.finish(), chip, ws, ms, vs)


def _fwd_bwd(h, target, W, gains, cw_full, cb_full, kv_norm, b_f, reduce):
    w_kv = W["w_kvf"][:, :2 * D]
    w_kvf_pad = jnp.pad(W["w_kvf"], ((0, 0), (0, 128 - 16)))
    w_f = w_kvf_pad[:, 2 * D:]
    kvn_g = kv_norm.reshape(1, D)
    bf_pad = jnp.pad(b_f, (0, 128 - 16)).reshape(1, 128)
    tabs = _rope_tables()
    headsum = _headsum_matrix()

    saved = []
    kv = zf = c_row = kvn = h_kv = None
    xn = _rms_fwd(h, gains[0][0], out_dtype=BF16, name="rms_in")
    for l in range(DEPTH):
        s = {"h": h}
        g = gains[l]
        s["xn"] = xn
        if l < N_A:
            qkv = _matmul(xn, W["w_qkv_a"][l], mode="nn", out_dtype=F32, name="mm_qkv", mnk=(T, 3 * A_W, D), tn=768)
            qkvp = _rope_fwd(qkv, tabs).reshape(9, 2, T, 128)
            o_p, lse_p = _band_fwd(qkvp)
            att, o3, lse3 = _combine_fwd(o_p, lse_p)
            s.update(qkvp=qkvp, o3=o3, lse3=lse3, lse_p=lse_p, att=att)
            mix = _matmul(att, W["w_o_a"][l], mode="nn", out_dtype=F32, name="mm_oa", mnk=(T, D, A_W))
        else:
            j = l - N_A
            if l == N_A:
                h_kv = h
                kvn = _rms_fwd(h, kvn_g, out_dtype=BF16, name="rms_in")
                kv = _matmul(kvn, w_kv, mode="nn", out_dtype=BF16, name="mm_kv")
                zf = _matmul(kvn, w_f, mode="nn", out_dtype=F32, name="mm_f")
                cum = _gates_fwd(zf, bf_pad)[:, :16]
                c_row = cum.T.reshape(8, 2, T)
            q = _matmul(xn, W["w_q_b"][j], mode="nn", out_dtype=BF16, name="mm_qb", mnk=(T, D, D), alpha=HD ** -0.5)
            o = _fox_fwd(q, kv, c_row)
            s.update(q=q, o=o)
            mix = _matmul(o, W["w_o_b"][j], mode="nn", out_dtype=F32, name="mm_ob", mnk=(T, D, D))
        s["mix"] = mix
        h1, xn2 = _rms_res_in(mix, g[1], h, g[2], name="rms_res_in")
        a = _matmul(xn2, W["w_up"][l], mode="nn", out_dtype=F32, name="mm_up", mnk=(T, 2 * D_FF, D))
        u = _convgate_fwd(a, cw_full[l], cb_full[l])
        f = _matmul(u, W["w_down"][l], mode="nn", out_dtype=F32, name="mm_down", mnk=(T, D, D_FF), tm=1024, tk=D_FF)
        if l + 1 < DEPTH:
            h, xn = _rms_res_in(f, g[3], h1, gains[l + 1][0], name="rms_res_in")
        else:
            h = _rms_fwd(f, g[3], res=h1, out_dtype=F32, name="rms_res")
        s.update(h1=h1, xn2=xn2, a=a, u=u, f=f)
        saved.append(s)

    dh, sq = _loss_head(h, target)

    d_gains = [[None] * 4 for _ in range(DEPTH)]
    d_cw, d_cb = [None] * DEPTH, [None] * DEPTH
    zeros_td = jnp.zeros((T, D), F32)
    fox_acc = (zeros_td, zeros_td, jnp.zeros((D // 128, 8, T), F32))
    d_kvnorm = d_bf = token = df = None

    def dw(nm, a, b, **kw):
        return _matmul(a, b, mode="tn", out_dtype=BF16, name="mm_dw_" + nm, **kw)

    flush = getattr(reduce, "flush", lambda after: None)

    def slabs(full, width):
        return full.reshape(full.shape[0], N_CHIPS, width).transpose(1, 0, 2)

    for l in reversed(range(DEPTH)):
        s = saved[l]
        g = gains[l]
        if df is None:
            df, d_gains[l][3] = _rms_bwd(dh, s["f"], g[3], out_dtype=BF16, name="rms_bwd")
        du = _matmul(df, W["w_down"][l], mode="nt", out_dtype=F32, name="mm_down_dx", mnk=(T, D_FF, D), tn=256, after=token)
        g_down = dw("w_down", s["u"], df, tm=1408, tn=1024)
        da, d_cw[l], d_cb[l] = _convgate_bwd(s["a"], du, cw_full[l], cb_full[l])
        dxn2 = _matmul(da, W["w_up"][l], mode="nt", out_dtype=F32, name="mm_up_dx", mnk=(T, D, 2 * D_FF), tm=1024, tn=1024, tk=1408,
                       a_map=_halves_a)
        g_up = dw("w_up", s["xn2"], da, mnk=(D, 2 * D_FF, T), tn=1408, b_map=_halves_b)
        token = reduce([("w_down", l, g_down), ("w_up", l, g_up)], "ffn%d" % l)
        dh1, dmix, d_gains[l][2], d_gains[l][1] = _rms_bwd2(dxn2, s["h1"], g[2], dh, s["mix"], g[1], name="rms_bwd2")
        if l < N_A:
            datt = _matmul(dmix, W["w_o_a"][l], mode="nt", out_dtype=F32, name="mm_oa_dx", mnk=(T, A_W, D), tn=768)
            g_o = dw("w_o_a", s["att"], dmix, tm=768, tn=1024, after=token)
            do_p, dlt_p = _combine_bwd(datt, s["o3"], s["lse3"], headsum)
            dqkv = None
            for which, d in enumerate(_band_bwd(s["qkvp"], do_p, s["lse_p"], dlt_p)):
                dqkv = _rope_bwd(d, which, tabs, dqkv)
            dxn = _matmul(dqkv, W["w_qkv_a"][l], mode="nt", out_dtype=F32, name="mm_qkv_dx", mnk=(T, D, 3 * A_W), tm=1024, tn=1024, tk=3 * A_W,
                          after=[flush(dqkv)])
            g_qkv = dw("w_qkv_a", s["xn"], dqkv, tn=768)
            group = [("w_o_a", l, g_o), ("w_qkv_a", l, slabs(g_qkv, 576))]
        else:
            j = l - N_A
            do = _matmul(dmix, W["w_o_b"][j], mode="nt", out_dtype=BF16, name="mm_ob_dx", mnk=(T, D, D))
            g_o = dw("w_o_b", s["o"], dmix, tn=1024, after=token)
            dq, *fox_acc = _fox_bwd(s["q"], kv, do, c_row, fox_acc)
            dxn = _matmul(dq, W["w_q_b"][j], mode="nt", out_dtype=F32, name="mm_qb_dx", mnk=(T, D, D), after=[flush(dq)])
            g_q = dw("w_q_b", s["xn"], dq, tn=1024)
            group = [("w_o_b", j, g_o), ("w_q_b", j, g_q)]
        if l > 0 and l != N_A:
            dh, df, d_gains[l][0], d_gains[l - 1][3] = _rms_bwd2(dxn, s["h"], g[0], dh1, saved[l - 1]["f"], gains[l - 1][3],
                                                                 name="rms_bwd2")
        else:
            dh, d_gains[l][0] = _rms_bwd(dxn, s["h"], g[0], dres=dh1, out_dtype=F32, name="rms_bwd_res")
            df = None
        if l == N_A:
            dk, dv, dck = fox_acc
            dc16 = -dck[:, :2, :].reshape(16, T).T
            dzf, d_bf = _gates_bwd(jnp.pad(dc16, ((0, 0), (0, 128 - 16))), zf, bf_pad)
            dkvf = jnp.concatenate([dk.astype(BF16), dv.astype(BF16), dzf], axis=1)
            g_kvf = _matmul(kvn, dkvf, mode="tn", out_dtype=BF16, name="mm_kvf_dw", tm=512, tn=2 * D + 128)[:, :2 * D + 16]
            dkvn = _matmul(dkvf, w_kvf_pad, mode="nt", out_dtype=F32, name="mm_kvf_dx", tm=1024, tn=1024, tk=2 * D + 128)
            dh, d_kvnorm = _rms_bwd(dkvn, h_kv, kvn_g, dres=dh, out_dtype=F32, name="rms_bwd_res")
            group.append(("w_kvf", 0, slabs(g_kvf, 516)))
        token = reduce(group, "mix%d" % l)
    small_flat = jnp.concatenate([
        jnp.stack([jnp.stack(r) for r in d_gains]).reshape(-1),
        jnp.stack(d_cw).transpose(0, 2, 1, 3).reshape(-1),
        jnp.stack(d_cb).reshape(-1),
        d_kvnorm.reshape(-1), d_bf[0, :16]])
    small = jnp.pad(small_flat, (0, 2 * N_CHIPS * SMALL_ROWS * SMALL_W - small_flat.shape[0]))
    reduce([("small", 0, small.reshape(N_CHIPS, 2 * SMALL_ROWS, SMALL_W))], "small")
    return sq, dh


def _update(loss, grad_x, reduced, chip, ws, ms, vs):
    red_s = reduced.pop("small")
    buf_s = lax.dynamic_update_slice(jnp.zeros((2, N_CHIPS, SMALL_ROWS, SMALL_W), F32), red_s.reshape(2, 1, SMALL_ROWS, SMALL_W),
                                     (0, chip, 0, 0))
    (all_s,) = _allgather([buf_s], ["slab"], name="gather_small_grads")
    sflat = all_s.transpose(1, 0, 2, 3).reshape(-1)

    grads = {nm: r.reshape(ws[nm].shape) for nm, r in reduced.items()}
    o = 0
    g_gains_full = sflat[o:o + 16 * D].reshape(DEPTH, 4, D); o += 16 * D
    g_cw_full = sflat[o:o + 12 * 2 * D_FF].reshape(DEPTH, 3, 2 * D_FF); o += 12 * 2 * D_FF
    grads["conv_b"] = sflat[o:o + 4 * 2 * D_FF].reshape(DEPTH, 2 * D_FF); o += 4 * 2 * D_FF
    grads["kv_norm"] = sflat[o:o + D]; o += D
    grads["b_f"] = sflat[o:o + 16]
    grads["norm_gains"] = lax.dynamic_slice_in_dim(g_gains_full, chip * 256, 256, axis=2)
    grads["conv_w"] = lax.dynamic_slice_in_dim(g_cw_full, chip * 1408, 1408, axis=2)

    names = ["norm_gains", "w_qkv_a", "w_o_a", "w_q_b", "w_o_b", "kv_norm", "w_kvf", "b_f", "w_up", "conv_w", "conv_b", "w_down"]
    deltas, new_m, new_v = {}, {}, {}
    for nm in names:
        shp = ws[nm].shape
        two = (math.prod(shp[:-1]), shp[-1]) if len(shp) > 1 else (1, shp[0])
        d, m2, v2 = _adamw(ws[nm].reshape(two), ms[nm].reshape(two), vs[nm].reshape(two), grads[nm].reshape(two),
                           name="adamw_" + nm)
        deltas[nm], new_m[nm], new_v[nm] = d.reshape(shp), m2.reshape(shp), v2.reshape(shp)

    return (loss, grad_x, *[grads[nm] for nm in names], *[deltas[nm] for nm in names],
            *[new_m[nm] for nm in names], *[new_v[nm] for nm in names])
```

```python
import math

import jax
import jax.numpy as jnp
from jax import lax
from jax.experimental import pallas as pl
from jax.experimental.pallas import tpu as pltpu
from jax.experimental.pallas import tpu_sc as plsc

F32 = jnp.float32
BF16 = jnp.bfloat16
MESH = pl.DeviceIdType.MESH
ANY = pl.BlockSpec(memory_space=pl.ANY)

T = 2048
D = 1024
HD = 64
DEPTH = 4
N_A = 2
A_W = 768
GW = 256
DIL = (1, 4, 16)
BLK = 128
D_FF = 2816
ROPE_THETA = 500000.0
EPS = 1e-6
NEG = -1e30
N_CHIPS = 4
FQ = 256
CT = 128
VMEM_BIG = 48 * 1024 * 1024

ADAM_LR, ADAM_B1, ADAM_B2, ADAM_EPS, ADAM_WD, ADAM_STEP = 0.001, 0.9, 0.999, 1e-08, 0.01, 10

NN = (((1,), (0,)), ((), ()))
NT = (((1,), (1,)), ((), ()))
TN = (((0,), (0,)), ((), ()))


def _dot(a, b, dims):
    return lax.dot_general(a, b, dims, preferred_element_type=F32)


def _pick(dim, pref):
    if dim <= pref:
        return dim
    best = None
    for t in range(128, pref + 1, 128):
        if dim % t == 0:
            best = t
    assert best is not None, (dim, pref)
    return best


def _params(sem=None, vmem=None):
    kw = {}
    if sem is not None:
        kw["dimension_semantics"] = sem
    if vmem is not None:
        kw["vmem_limit_bytes"] = vmem
    return pltpu.CompilerParams(**kw)


def _matmul(a, b, *, mode, out_dtype, name, mnk=None, alpha=None, tm=2048, tn=512, tk=2048,
            a_map=None, b_map=None, acc_init=None, out_slab=None, out_slabs=None, out_buf=None, after=None):
    if mnk is not None:
        M, N, K = mnk
    elif mode == "nn":
        (M, K), (_, N) = a.shape, b.shape
    elif mode == "nt":
        (M, K), (N, _) = a.shape, b.shape
    else:
        (K, M), (_, N) = a.shape, b.shape
    tm, tn, tk = _pick(M, tm), _pick(N, tn), _pick(K, tk)
    nk = K // tk
    dims = {"nn": NN, "nt": NT, "tn": TN}[mode]
    after = [t for t in (after or ()) if t is not None]
    n_in = 2 + (acc_init is not None) + len(after) + (out_buf is not None)

    def body(*refs):
        a_ref, b_ref = refs[0], refs[1]
        o_ref = refs[n_in]
        k = pl.program_id(2)

        def finish(r):
            if alpha is not None:
                r = r * alpha
            o_ref[...] = r.astype(out_dtype)

        def product():
            r = _dot(a_ref[...], b_ref[...], dims)
            return r if acc_init is None else r + refs[2][...]

        if nk == 1:
            finish(product())
            return
        acc_ref = refs[n_in + 1]

        @pl.when(k == 0)
        def _():
            acc_ref[...] = product()

        @pl.when((k > 0) & (k < nk - 1))
        def _():
            acc_ref[...] += _dot(a_ref[...], b_ref[...], dims)

        @pl.when(k == nk - 1)
        def _():
            finish(acc_ref[...] + _dot(a_ref[...], b_ref[...], dims))

    a_blk = (tk, tm) if mode == "tn" else (tm, tk)
    b_blk = (tn, tk) if mode == "nt" else (tk, tn)
    if a_map is not None:
        a_spec = pl.BlockSpec((None,) + a_blk, a_map(tm, tn, tk))
    elif mode == "tn":
        a_spec = pl.BlockSpec(a_blk, lambda i, j, k: (k, i))
    else:
        a_spec = pl.BlockSpec(a_blk, lambda i, j, k: (i, k))
    if b_map is not None:
        b_spec = pl.BlockSpec((None,) + b_blk, b_map(tm, tn, tk))
    elif mode == "nt":
        b_spec = pl.BlockSpec(b_blk, lambda i, j, k: (j, k))
    else:
        b_spec = pl.BlockSpec(b_blk, lambda i, j, k: (k, j))
    ins, specs, alias = [a, b], [a_spec, b_spec], {}
    if acc_init is not None:
        ins.append(acc_init)
        specs.append(pl.BlockSpec((tm, tn), lambda i, j, k: (i, j)))
    ins += after
    specs += [ANY] * len(after)
    if out_buf is not None:
        alias = {len(ins): 0}
        ins.append(out_buf)
        specs.append(ANY)
    if out_slab is None:
        o_spec = pl.BlockSpec((tm, tn), lambda i, j, k: (i, j))
        o_shape = jax.ShapeDtypeStruct((M, N), out_dtype)
    else:
        o_spec = pl.BlockSpec((None, tm, tn), lambda i, j, k: (out_slab, i, j))
        o_shape = jax.ShapeDtypeStruct((out_slabs, M, N), out_dtype)
    return pl.pallas_call(
        body,
        grid=(M // tm, N // tn, nk),
        in_specs=specs,
        out_specs=o_spec,
        out_shape=o_shape,
        scratch_shapes=[pltpu.VMEM((tm, tn), F32)] if nk > 1 else [],
        input_output_aliases=alias,
        compiler_params=_params(("parallel", "parallel", "arbitrary"), VMEM_BIG),
        name=name,
    )(*ins)


def _slab(l, mode):
    if mode == "nt":
        return lambda tm, tn, tk: (lambda i, j, k: (l, j, k))
    return lambda tm, tn, tk: (lambda i, j, k: (l, k, j))


def _rms_fwd(x, g, *, out_dtype, name, res=None, tr=256):
    n, d = x.shape

    def body(*refs):
        x_ref, g_ref = refs[0], refs[1]
        o_ref = refs[-1]
        xv = x_ref[...].astype(F32)
        y = xv * lax.rsqrt(jnp.mean(xv * xv, axis=-1, keepdims=True) + EPS) * g_ref[...]
        if res is not None:
            y = y + refs[2][...]
        o_ref[...] = y.astype(out_dtype)

    row = pl.BlockSpec((tr, d), lambda i: (i, 0))
    vec = pl.BlockSpec((1, d), lambda i: (0, 0))
    ins = [x, g] + ([] if res is None else [res])
    specs = [row, vec] + ([] if res is None else [row])
    return pl.pallas_call(
        body, grid=(n // tr,), in_specs=specs, out_specs=row,
        out_shape=jax.ShapeDtypeStruct((n, d), out_dtype),
        compiler_params=_params(("parallel",)), name=name,
    )(*ins)


def _rms_bwd(dy, x, g, *, out_dtype, name, dres=None, tr=256):
    n, d = x.shape

    def body(*refs):
        dy_ref, x_ref, g_ref = refs[0], refs[1], refs[2]
        dx_ref, dg_ref = refs[-2], refs[-1]
        xv = x_ref[...].astype(F32)
        dyv = dy_ref[...].astype(F32)
        rstd = lax.rsqrt(jnp.mean(xv * xv, axis=-1, keepdims=True) + EPS)
        xhat = xv * rstd
        dxh = dyv * g_ref[...]
        dx = rstd * (dxh - xhat * jnp.mean(dxh * xhat, axis=-1, keepdims=True))
        if dres is not None:
            dx = dx + refs[3][...]
        dx_ref[...] = dx.astype(out_dtype)

        @pl.when(pl.program_id(0) == 0)
        def _():
            dg_ref[...] = jnp.zeros_like(dg_ref)

        dg_ref[...] += jnp.sum(dyv * xhat, axis=0, keepdims=True)

    row = pl.BlockSpec((tr, d), lambda i: (i, 0))
    vec = pl.BlockSpec((1, d), lambda i: (0, 0))
    ins = [dy, x, g] + ([] if dres is None else [dres])
    specs = [row, row, vec] + ([] if dres is None else [row])
    return pl.pallas_call(
        body, grid=(n // tr,), in_specs=specs, out_specs=[row, vec],
        out_shape=[jax.ShapeDtypeStruct((n, d), out_dtype), jax.ShapeDtypeStruct((1, d), F32)],
        compiler_params=_params(("arbitrary",)), name=name,
    )(*ins)


def _rms_res_in(x, g_res, res, g_in, *, name, tr=256):
    n, d = x.shape

    def body(x_ref, gr_ref, r_ref, gi_ref, h_ref, n_ref):
        xv = x_ref[...].astype(F32)
        h = r_ref[...] + xv * lax.rsqrt(jnp.mean(xv * xv, axis=-1, keepdims=True) + EPS) * gr_ref[...]
        h_ref[...] = h
        n_ref[...] = (h * lax.rsqrt(jnp.mean(h * h, axis=-1, keepdims=True) + EPS) * gi_ref[...]).astype(BF16)

    row = pl.BlockSpec((tr, d), lambda i: (i, 0))
    vec = pl.BlockSpec((1, d), lambda i: (0, 0))
    return pl.pallas_call(
        body, grid=(n // tr,), in_specs=[row, vec, row, vec], out_specs=[row, row],
        out_shape=[jax.ShapeDtypeStruct((n, d), F32), jax.ShapeDtypeStruct((n, d), BF16)],
        compiler_params=_params(("parallel",)), name=name,
    )(x, g_res, res, g_in)


def _rms_bwd2(dy, x, g, dres, x2, g2, *, name, tr=256):
    n, d = x.shape

    def one(dyv, xv, gv):
        rstd = lax.rsqrt(jnp.mean(xv * xv, axis=-1, keepdims=True) + EPS)
        xhat = xv * rstd
        dxh = dyv * gv
        return rstd * (dxh - xhat * jnp.mean(dxh * xhat, axis=-1, keepdims=True)), jnp.sum(dyv * xhat, axis=0, keepdims=True)

    def body(dy_ref, x_ref, g_ref, r_ref, x2_ref, g2_ref, dx_ref, d2_ref, dg_ref, dg2_ref):
        dx, dg = one(dy_ref[...].astype(F32), x_ref[...].astype(F32), g_ref[...])
        dx = dx + r_ref[...]
        dx_ref[...] = dx
        d2, dg2 = one(dx, x2_ref[...].astype(F32), g2_ref[...])
        d2_ref[...] = d2.astype(BF16)

        @pl.when(pl.program_id(0) == 0)
        def _():
            dg_ref[...] = jnp.zeros_like(dg_ref)
            dg2_ref[...] = jnp.zeros_like(dg2_ref)

        dg_ref[...] += dg
        dg2_ref[...] += dg2

    row = pl.BlockSpec((tr, d), lambda i: (i, 0))
    vec = pl.BlockSpec((1, d), lambda i: (0, 0))
    return pl.pallas_call(
        body, grid=(n // tr,), in_specs=[row, row, vec, row, row, vec], out_specs=[row, row, vec, vec],
        out_shape=[jax.ShapeDtypeStruct((n, d), F32), jax.ShapeDtypeStruct((n, d), BF16),
                   jax.ShapeDtypeStruct((1, d), F32), jax.ShapeDtypeStruct((1, d), F32)],
        compiler_params=_params(("arbitrary",)), name=name,
    )(dy, x, g, dres, x2, g2)


def _loss_head(h, target, *, tr=256):
    n, d = h.shape

    def body(h_ref, t_ref, dh_ref, s_ref):
        err = h_ref[...] - t_ref[...]
        dh_ref[...] = err * (1.0 / d)

        @pl.when(pl.program_id(0) == 0)
        def _():
            s_ref[...] = jnp.zeros_like(s_ref)

        s_ref[...] += jnp.sum(err * err)

    row = pl.BlockSpec((tr, d), lambda i: (i, 0))
    acc = pl.BlockSpec((8, 128), lambda i: (0, 0))
    return pl.pallas_call(
        body, grid=(n // tr,), in_specs=[row, row], out_specs=[row, acc],
        out_shape=[jax.ShapeDtypeStruct((n, d), F32), jax.ShapeDtypeStruct((8, 128), F32)],
        compiler_params=_params(("arbitrary",)), name="loss_head",
    )(h, target)


def _rope_tables():
    pos = jnp.arange(T, dtype=F32)
    inv = ROPE_THETA ** (-jnp.arange(0, 16, 2, dtype=F32) / 16)
    ang = pos[:, None] * inv[None, :]
    cos, sin = jnp.cos(ang), jnp.sin(ang)
    one = jnp.ones((T, HD - 16), F32)
    zero8 = jnp.zeros((T, 8), F32)
    zero = jnp.zeros((T, HD - 16), F32)
    c = jnp.concatenate([cos, cos, one], axis=1)
    s1 = jnp.concatenate([zero8, sin, zero], axis=1)
    s2 = jnp.concatenate([-sin, zero8, zero], axis=1)
    c, s1, s2 = (jnp.concatenate([t, t], axis=1) for t in (c, s1, s2))
    scale = HD ** -0.5
    return (jnp.stack([c * scale, c, jnp.ones_like(c)]), jnp.stack([s1 * scale, s1, jnp.zeros_like(c)]),
            jnp.stack([s2 * scale, s2, jnp.zeros_like(c)]))


def _row_chunks(r):
    if r == 1:
        n = 4
        return [(slice(i * (T // n), (i + 1) * (T // n)),) * 2 for i in range(n)]
    per = T // r
    return [(pl.ds(j, per, stride=r), slice(j * per, (j + 1) * per)) for j in range(r)]


def _rope_fwd(qkv, tabs):
    def body(x_ref, c_ref, s1_ref, s2_ref, o_ref):
        g = lax.rem(lax.div(pl.program_id(0), 2), 3)
        for gi, r in enumerate(DIL):
            @pl.when(g == gi)
            def _(r=r):
                for tok, prm in _row_chunks(r):
                    x = x_ref[tok, :]
                    y = x * c_ref[tok, :] + pltpu.roll(x, 8, 1) * s1_ref[tok, :] + pltpu.roll(x, 120, 1) * s2_ref[tok, :]
                    o_ref[prm, :] = y.astype(BF16)

    tab = pl.BlockSpec((None, T, 128), lambda b: (lax.div(b, 6), 0, 0))
    return pl.pallas_call(
        body, grid=(18,), in_specs=[pl.BlockSpec((T, 128), lambda b: (0, b)), tab, tab, tab],
        out_specs=pl.BlockSpec((None, T, 128), lambda b: (b, 0, 0)), out_shape=jax.ShapeDtypeStruct((18, T, 128), BF16),
        compiler_params=_params(("parallel",)), name="rope_fwd",
    )(qkv, *tabs)


def _rope_bwd(d, which, tabs, out_buf):
    def body(d_ref, c_ref, s1_ref, s2_ref, *rest):
        o_ref, tok_ref = rest[-2], rest[-1]
        g = lax.div(pl.program_id(0), 2)
        for gi, r in enumerate(DIL):
            @pl.when(g == gi)
            def _(r=r):
                for tok, prm in _row_chunks(r):
                    tok_ref[tok, :] = d_ref[prm, :]
                for rows, _ in _row_chunks(1):
                    gx = tok_ref[rows, :]
                    y = gx * c_ref[rows, :] + pltpu.roll(gx * s1_ref[rows, :], 120, 1) + pltpu.roll(gx * s2_ref[rows, :], 8, 1)
                    o_ref[rows, :] = y.astype(BF16)

    tab = pl.BlockSpec((None, T, 128), lambda b: (which, 0, 0))
    ins = [d, *tabs] + ([] if out_buf is None else [out_buf])
    specs = [pl.BlockSpec((None, None, T, 128), lambda b: (lax.div(b, 2), lax.rem(b, 2), 0, 0)), tab, tab, tab]
    return pl.pallas_call(
        body, grid=(6,), in_specs=specs + ([] if out_buf is None else [ANY]),
        out_specs=pl.BlockSpec((T, 128), lambda b: (0, 6 * which + b)),
        out_shape=jax.ShapeDtypeStruct((T, 3 * A_W), BF16), scratch_shapes=[pltpu.VMEM((T, 128), F32)],
        input_output_aliases={} if out_buf is None else {4: 0},
        compiler_params=_params(("arbitrary",)), name="rope_bwd",
    )(*ins)


def _head_mask(x, lane_lo):
    lane = lax.broadcasted_iota(jnp.int32, x.shape, 1)
    keep = (lane < HD) if lane_lo else (lane >= HD)
    return jnp.where(keep, x.astype(F32), 0.0).astype(BF16)


def _band_scalars():
    g, b = pl.program_id(0), pl.program_id(1)
    nbs = lax.shift_right_logical(jnp.int32(T // BLK), 2 * g)
    has_prev = jnp.where((b & (nbs - 1)) != 0, 1, 0)
    next_ok = jnp.where(((b + 1) & (nbs - 1)) != 0, 1, 0)
    return has_prev, next_ok


def _band_mask_q(has_prev):
    row = lax.broadcasted_iota(jnp.int32, (BLK, 2 * BLK), 0)
    col = lax.broadcasted_iota(jnp.int32, (BLK, 2 * BLK), 1)
    return ((col < BLK) & (col >= row) & (has_prev == 1)) | ((col >= BLK) & (col - BLK <= row))


def _band_mask_k(next_ok):
    row = lax.broadcasted_iota(jnp.int32, (2 * BLK, BLK), 0)
    col = lax.broadcasted_iota(jnp.int32, (2 * BLK, BLK), 1)
    return ((row < BLK) & (col <= row)) | ((row >= BLK) & (col >= row - BLK) & (next_ok == 1))


def _band_spec(base, step):
    nb = T // BLK
    at = {"cur": lambda b: b, "prev": lambda b: jnp.maximum(b - 1, 0), "next": lambda b: jnp.minimum(b + 1, nb - 1)}[step]
    return pl.BlockSpec((None, 2, BLK, 128), lambda g, b: (base + g, 0, at(b), 0))


def _band_fwd(qkv):
    nb = T // BLK

    def body(q_ref, kc_ref, kp_ref, vc_ref, vp_ref, o_ref, l_ref):
        has_prev, _ = _band_scalars()
        mask = _band_mask_q(has_prev)
        lane = lax.broadcasted_iota(jnp.int32, (BLK, 128), 1)
        for p in range(2):
            qp = q_ref[p]
            kcat = jnp.concatenate([kp_ref[p], kc_ref[p]], axis=0)
            vcat = jnp.concatenate([vp_ref[p], vc_ref[p]], axis=0)
            o_acc = jnp.zeros((BLK, 128), F32)
            lse = jnp.zeros((BLK, 128), F32)
            for e in range(2):
                s = _dot(_head_mask(qp, e == 0), kcat, NT)
                s = jnp.where(mask, s, NEG)
                m = jnp.max(s, axis=-1, keepdims=True)
                pr = jnp.exp(s - m)
                l = jnp.sum(pr, axis=-1, keepdims=True)
                o_acc = o_acc + _dot(pr.astype(BF16), _head_mask(vcat, e == 0), NN) / l
                lse = jnp.where((lane < HD) if e == 0 else (lane >= HD), m + jnp.log(l), lse)
            o_ref[p] = o_acc
            l_ref[p] = lse

    out = _band_spec(0, "cur")
    shp = jax.ShapeDtypeStruct((3, 2, T, 128), F32)
    return pl.pallas_call(
        body, grid=(3, nb),
        in_specs=[_band_spec(0, "cur"), _band_spec(3, "cur"), _band_spec(3, "prev"), _band_spec(6, "cur"), _band_spec(6, "prev")],
        out_specs=[out, out], out_shape=[shp, shp],
        compiler_params=_params(("parallel", "parallel")), name="band_fwd",
    )(qkv, qkv, qkv, qkv, qkv)


def _band_bwd(qkv, do, lse, dlt):
    nb = T // BLK

    def body(qc_ref, qn_ref, kc_ref, kp_ref, vc_ref, vp_ref, doc_ref, don_ref, lc_ref, ln_ref, dc_ref, dn_ref,
             dq_ref, dk_ref, dv_ref):
        has_prev, next_ok = _band_scalars()
        mask_q = _band_mask_q(has_prev)
        mask_k = _band_mask_k(next_ok)
        for p in range(2):
            qc, qn = qc_ref[p], qn_ref[p]
            doc, don = doc_ref[p], don_ref[p]
            kc, vc = kc_ref[p], vc_ref[p]
            kcat = jnp.concatenate([kp_ref[p], kc], axis=0)
            vcat = jnp.concatenate([vp_ref[p], vc], axis=0)
            qcat = jnp.concatenate([qc, qn], axis=0)
            docat = jnp.concatenate([doc, don], axis=0)
            dq = jnp.zeros((BLK, 128), F32)
            dk = jnp.zeros((BLK, 128), F32)
            dv = jnp.zeros((BLK, 128), F32)
            for e in range(2):
                lo = e == 0
                col = slice(HD * e, HD * e + 1)
                lse_c, lse_n = lc_ref[p, :, col], ln_ref[p, :, col]
                dl_c, dl_n = dc_ref[p, :, col], dn_ref[p, :, col]
                s = jnp.where(mask_q, _dot(_head_mask(qc, lo), kcat, NT), NEG)
                pr = jnp.exp(s - lse_c)
                dp = _dot(_head_mask(doc, lo), vcat, NT)
                ds = pr * (dp - dl_c)
                dq = dq + _dot(ds.astype(BF16), _head_mask(kcat, lo), NN)
                qm, dom = _head_mask(qcat, lo), _head_mask(docat, lo)
                s2 = jnp.where(mask_k, _dot(qm, kc, NT), NEG)
                p2 = jnp.exp(s2 - jnp.concatenate([lse_c, lse_n], axis=0))
                dv = dv + _dot(p2.astype(BF16), dom, TN)
                dp2 = _dot(dom, vc, NT)
                ds2 = p2 * (dp2 - jnp.concatenate([dl_c, dl_n], axis=0))
                dk = dk + _dot(ds2.astype(BF16), qm, TN)
            dq_ref[p] = dq
            dk_ref[p] = dk
            dv_ref[p] = dv

    cur, nxt = _band_spec(0, "cur"), _band_spec(0, "next")
    shp = jax.ShapeDtypeStruct((3, 2, T, 128), F32)
    return pl.pallas_call(
        body, grid=(3, nb),
        in_specs=[cur, nxt, _band_spec(3, "cur"), _band_spec(3, "prev"), _band_spec(6, "cur"), _band_spec(6, "prev"),
                  cur, nxt, cur, nxt, cur, nxt],
        out_specs=[cur, cur, cur], out_shape=[shp, shp, shp],
        compiler_params=_params(("parallel", "parallel")), name="band_bwd",
    )(qkv, qkv, qkv, qkv, qkv, qkv, do, do, lse, lse, dlt, dlt)


def _split3(x):
    hi = x.astype(BF16)
    r = x - hi.astype(F32)
    mid = r.astype(BF16)
    lo = (r - mid.astype(F32)).astype(BF16)
    return hi, mid, lo


def _dot3(x, m, dims=NN):
    hi, mid, lo = _split3(x)
    return _dot(hi, m, dims) + _dot(mid, m, dims) + _dot(lo, m, dims)


def _combine_weights(lses):
    l0, l1, l2 = lses
    m = jnp.maximum(jnp.maximum(l0, l1), l2)
    e = [jnp.exp(l0 - m), jnp.exp(l1 - m), jnp.exp(l2 - m)]
    inv = 1.0 / (e[0] + e[1] + e[2])
    return [ei * inv for ei in e]


CR = 256


def _combine_fwd(o, lse):
    def body(o_ref, l_ref, att_ref, o3_ref, l3_ref):
        for g, r in enumerate(DIL):
            for p in range(2):
                for tok, prm in _row_chunks(r):
                    o3_ref[g, p, tok, :] = o_ref[g, p, prm, :]
                    l3_ref[g, p, tok, :] = l_ref[g, p, prm, :]
        for i in range(T // CR):
            rows = slice(i * CR, (i + 1) * CR)
            for p in range(2):
                alpha = _combine_weights([l3_ref[g, p, rows, :] for g in range(3)])
                for g in range(3):
                    att_ref[rows, g * GW + p * 128: g * GW + (p + 1) * 128] = (o3_ref[g, p, rows, :] * alpha[g]).astype(BF16)

    shp = jax.ShapeDtypeStruct((3, 2, T, 128), F32)
    return pl.pallas_call(
        body, out_shape=[jax.ShapeDtypeStruct((T, A_W), BF16), shp, shp],
        compiler_params=_params(vmem=VMEM_BIG), name="combine_fwd",
    )(o, lse)


def _combine_bwd(datt, o3, l3, headsum):
    def body(d_ref, o_ref, l_ref, hs_ref, do_ref, dl_ref, tdo_ref, tdl_ref):
        hs = hs_ref[...]
        for p in range(2):
            for i in range(T // CR):
                rows = slice(i * CR, (i + 1) * CR)
                alpha = _combine_weights([l_ref[g, p, rows, :] for g in range(3)])
                total = jnp.zeros((CR, 128), F32)
                for g in range(3):
                    dg = d_ref[rows, g * GW + p * 128: g * GW + (p + 1) * 128]
                    tdo_ref[g, rows, :] = dg * alpha[g]
                    total = total + alpha[g] * _dot3(dg * o_ref[g, p, rows, :], hs)
                for g in range(3):
                    tdl_ref[g, rows, :] = alpha[g] * total
            for g, r in enumerate(DIL):
                for tok, prm in _row_chunks(r):
                    do_ref[g, p, prm, :] = tdo_ref[g, tok, :].astype(BF16)
                    dl_ref[g, p, prm, :] = tdl_ref[g, tok, :]

    return pl.pallas_call(
        body, out_shape=[jax.ShapeDtypeStruct((3, 2, T, 128), BF16), jax.ShapeDtypeStruct((3, 2, T, 128), F32)],
        scratch_shapes=[pltpu.VMEM((3, T, 128), F32), pltpu.VMEM((3, T, 128), F32)],
        compiler_params=_params(vmem=VMEM_BIG), name="combine_bwd",
    )(datt, o3, l3, headsum)


def _fox_scores(qm, k_ref, ck_ref, e, i, n):
    s = _dot(qm, k_ref[0:n, :], NT) - ck_ref[0, e:e + 1, 0:n]
    row = lax.broadcasted_iota(jnp.int32, (FQ, FQ), 0)
    col = lax.broadcasted_iota(jnp.int32, (FQ, FQ), 1)
    diag = jnp.where(col <= row, s[:, n - FQ:], NEG)
    m = jnp.max(diag, axis=-1, keepdims=True)
    if i == 0:
        pr = jnp.exp(diag - m)
        return pr, jnp.sum(pr, axis=-1, keepdims=True)
    past = s[:, :n - FQ]
    m = jnp.maximum(m, jnp.max(past, axis=-1, keepdims=True))
    p_past, p_diag = jnp.exp(past - m), jnp.exp(diag - m)
    l = jnp.sum(p_past, axis=-1, keepdims=True) + jnp.sum(p_diag, axis=-1, keepdims=True)
    return jnp.concatenate([p_past, p_diag], axis=1), l


def _fox_fwd(q, kv, c_row):
    def body(q_ref, k_ref, v_ref, cr_ref, o_ref, vm_ref):
        for e in range(2):
            vm_ref[e] = _head_mask(v_ref[...], e == 0)
        for i in range(T // FQ):
            n = (i + 1) * FQ
            rows = slice(i * FQ, n)
            acc = jnp.zeros((FQ, 128), F32)
            for e in range(2):
                qm = _head_mask(q_ref[rows, :], e == 0)
                pr, l = _fox_scores(qm, k_ref, cr_ref, e, i, n)
                acc = acc + _dot(pr.astype(BF16), vm_ref[e, 0:n, :], NN) / l
            o_ref[rows, :] = acc.astype(BF16)

    pair = pl.BlockSpec((T, 128), lambda p: (0, p))
    return pl.pallas_call(
        body, grid=(D // 128,),
        in_specs=[pair, pair, pl.BlockSpec((T, 128), lambda p: (0, D // 128 + p)), pl.BlockSpec((1, 2, T), lambda p: (p, 0, 0))],
        out_specs=pair, out_shape=jax.ShapeDtypeStruct((T, D), BF16),
        scratch_shapes=[pltpu.VMEM((2, T, 128), BF16)],
        compiler_params=_params(("parallel",), VMEM_BIG), name="fox_fwd",
    )(q, kv, kv, c_row)


def _fox_bwd(q, kv, do, c_row, init):
    def body(q_ref, k_ref, v_ref, do_ref, cr_ref, ik_ref, iv_ref, ic_ref, dq_ref, dk_ref, dv_ref, dck_ref, km_ref):
        dk_ref[...] = ik_ref[...]
        dv_ref[...] = iv_ref[...]
        dck_ref[...] = ic_ref[...]
        for e in range(2):
            km_ref[e] = _head_mask(k_ref[...], e == 0)
        for i in range(T // FQ):
            n = (i + 1) * FQ
            rows = slice(i * FQ, n)
            dq = jnp.zeros((FQ, 128), F32)
            for e in range(2):
                qm = _head_mask(q_ref[rows, :], e == 0)
                dom = _head_mask(do_ref[rows, :], e == 0)
                pr, l = _fox_scores(qm, k_ref, cr_ref, e, i, n)
                pr = pr * (1.0 / l)
                dp = _dot(dom, v_ref[0:n, :], NT)
                ds = pr * (dp - jnp.sum(pr * dp, axis=-1, keepdims=True))
                dsb = ds.astype(BF16)
                dq = dq + _dot(dsb, km_ref[e, 0:n, :], NN)
                dk_ref[0:n, :] += _dot(dsb, qm, TN)
                dv_ref[0:n, :] += _dot(pr.astype(BF16), dom, TN)
                dck_ref[0, e:e + 1, 0:n] += jnp.sum(ds, axis=0, keepdims=True)
            dq_ref[rows, :] = (dq * HD ** -0.5).astype(BF16)

    pair = pl.BlockSpec((T, 128), lambda p: (0, p))
    ck = pl.BlockSpec((1, 8, T), lambda p: (p, 0, 0))
    return pl.pallas_call(
        body, grid=(D // 128,),
        in_specs=[pair, pair, pl.BlockSpec((T, 128), lambda p: (0, D // 128 + p)), pair,
                  pl.BlockSpec((1, 2, T), lambda p: (p, 0, 0)), pair, pair, ck],
        out_specs=[pair, pair, pair, ck],
        out_shape=[jax.ShapeDtypeStruct((T, D), BF16), jax.ShapeDtypeStruct((T, D), F32), jax.ShapeDtypeStruct((T, D), F32),
                   jax.ShapeDtypeStruct((D // 128, 8, T), F32)],
        scratch_shapes=[pltpu.VMEM((2, T, 128), BF16)],
        compiler_params=_params(("parallel",), VMEM_BIG), name="fox_bwd",
    )(q, kv, kv, do, c_row, *init)


def _tri(lower):
    r = lax.broadcasted_iota(jnp.int32, (BLK, BLK), 0)
    c = lax.broadcasted_iota(jnp.int32, (BLK, BLK), 1)
    return jnp.where((c <= r) if lower else (c >= r), 1.0, 0.0).astype(BF16)


def _gates_fwd(z, b):
    def body(z_ref, b_ref, c_ref):
        tri = _tri(True)
        carry = jnp.zeros((1, 128), F32)
        for i in range(T // BLK):
            rows = slice(i * BLK, (i + 1) * BLK)
            x = z_ref[rows, :] + b_ref[...]
            logf = jnp.minimum(x, 0.0) - jnp.log(1.0 + jnp.exp(-jnp.abs(x)))
            hi, mid, lo = _split3(logf)
            y = _dot(tri, hi, NN) + _dot(tri, mid, NN) + _dot(tri, lo, NN) + carry
            c_ref[rows, :] = y
            carry = y[BLK - 1:BLK, :]

    return pl.pallas_call(body, out_shape=jax.ShapeDtypeStruct((T, 128), F32), name="gates_fwd")(z, b)


def _gates_bwd(dc, z, b):
    def body(dc_ref, z_ref, b_ref, dz_ref, db_ref):
        tri = _tri(False)
        carry = jnp.zeros((1, 128), F32)
        db = jnp.zeros((1, 128), F32)
        for i in reversed(range(T // BLK)):
            rows = slice(i * BLK, (i + 1) * BLK)
            hi, mid, lo = _split3(dc_ref[rows, :])
            dlogf = _dot(tri, hi, NN) + _dot(tri, mid, NN) + _dot(tri, lo, NN) + carry
            carry = dlogf[0:1, :]
            x = z_ref[rows, :] + b_ref[...]
            dz = dlogf / (1.0 + jnp.exp(x))
            dz_ref[rows, :] = dz.astype(BF16)
            db = db + jnp.sum(dz, axis=0, keepdims=True)
        db_ref[...] = db

    return pl.pallas_call(
        body, out_shape=[jax.ShapeDtypeStruct((T, 128), BF16), jax.ShapeDtypeStruct((1, 128), F32)], name="gates_bwd",
    )(dc, z, b)


def _conv_pair(a_refs, cw_refs, cb_refs):
    row = lax.broadcasted_iota(jnp.int32, (T, CT), 0)
    outs = []
    for a_ref, cw_ref, cb_ref in zip(a_refs, cw_refs, cb_refs):
        z = a_ref[...]
        z1 = jnp.where(row >= 1, pltpu.roll(z, 1, 0), 0.0)
        z2 = jnp.where(row >= 2, pltpu.roll(z, 2, 0), 0.0)
        y = cw_ref[2:3, :] * z + cw_ref[1:2, :] * z1 + cw_ref[0:1, :] * z2 + cb_ref[...]
        outs.append((y, z, z1, z2))
    return outs


_GELU_K = math.sqrt(2.0 / math.pi)
N_CT = D_FF // CT


def _conv_specs():
    def at(rows, off):
        return pl.BlockSpec((rows, CT), lambda j: (0, j + off))
    return [at(T, 0), at(T, N_CT), at(3, 0), at(3, N_CT), at(1, 0), at(1, N_CT)]


def _convgate_fwd(a, cw, cb):
    def body(ag_ref, av_ref, wg_ref, wv_ref, bg_ref, bv_ref, u_ref):
        (g, _, _, _), (v, _, _, _) = _conv_pair((ag_ref, av_ref), (wg_ref, wv_ref), (bg_ref, bv_ref))
        th = jnp.tanh(_GELU_K * (g + 0.044715 * g * g * g))
        u_ref[...] = (0.5 * g * (1.0 + th) * v).astype(BF16)

    return pl.pallas_call(
        body, grid=(N_CT,), in_specs=_conv_specs(),
        out_specs=pl.BlockSpec((T, CT), lambda j: (0, j)), out_shape=jax.ShapeDtypeStruct((T, D_FF), BF16),
        compiler_params=_params(("parallel",), VMEM_BIG), name="convgate_fwd",
    )(a, a, cw, cw, cb, cb)


def _convgate_bwd(a, du, cw, cb):
    def body(ag_ref, av_ref, wg_ref, wv_ref, bg_ref, bv_ref, du_ref, da_ref, dcw_ref, dcb_ref):
        (g, gz, gz1, gz2), (v, vz, vz1, vz2) = _conv_pair((ag_ref, av_ref), (wg_ref, wv_ref), (bg_ref, bv_ref))
        du = du_ref[...].astype(F32)
        th = jnp.tanh(_GELU_K * (g + 0.044715 * g * g * g))
        gelu = 0.5 * g * (1.0 + th)
        dgelu = 0.5 * (1.0 + th) + 0.5 * g * (1.0 - th * th) * _GELU_K * (1.0 + 3 * 0.044715 * g * g)
        row = lax.broadcasted_iota(jnp.int32, (T, CT), 0)
        for h, (d, z, z1, z2, w_ref) in enumerate(((du * v * dgelu, gz, gz1, gz2, wg_ref), (du * gelu, vz, vz1, vz2, wv_ref))):
            d1 = jnp.where(row < T - 1, pltpu.roll(d, T - 1, 0), 0.0)
            d2 = jnp.where(row < T - 2, pltpu.roll(d, T - 2, 0), 0.0)
            da_ref[h] = (w_ref[2:3, :] * d + w_ref[1:2, :] * d1 + w_ref[0:1, :] * d2).astype(BF16)
            dcw_ref[h, 0:1, :] = jnp.sum(d * z2, axis=0, keepdims=True)
            dcw_ref[h, 1:2, :] = jnp.sum(d * z1, axis=0, keepdims=True)
            dcw_ref[h, 2:3, :] = jnp.sum(d * z, axis=0, keepdims=True)
            dcb_ref[h] = jnp.sum(d, axis=0, keepdims=True)

    def both(rows):
        return pl.BlockSpec((2, rows, CT), lambda j: (0, 0, j))

    return pl.pallas_call(
        body, grid=(N_CT,),
        in_specs=_conv_specs() + [pl.BlockSpec((T, CT), lambda j: (0, j))],
        out_specs=[both(T), both(3), both(1)],
        out_shape=[jax.ShapeDtypeStruct((2, T, D_FF), BF16), jax.ShapeDtypeStruct((2, 3, D_FF), F32),
                   jax.ShapeDtypeStruct((2, 1, D_FF), F32)],
        compiler_params=_params(("parallel",), VMEM_BIG), name="convgate_bwd",
    )(a, a, cw, cw, cb, cb, du)


def _halves_a(tm, tn, tk):
    per = D_FF // tk
    return lambda i, j, k: (lax.div(k, per), i, lax.rem(k, per))


def _halves_b(tm, tn, tk):
    per = D_FF // tn
    return lambda i, j, k: (lax.div(j, per), k, lax.rem(j, per))


def _adamw(w, m, v, g, *, name):
    r, c = w.shape
    tr = r
    if r * c > 256 * 1024:
        for cand in range(8, r, 8):
            if r % cand == 0 and cand * c <= 256 * 1024:
                tr = cand

    def body(w_ref, m_ref, v_ref, g_ref, d_ref, nm_ref, nv_ref):
        gv = g_ref[...]
        mn = ADAM_B1 * m_ref[...] + (1.0 - ADAM_B1) * gv
        vn = ADAM_B2 * v_ref[...] + (1.0 - ADAM_B2) * (gv * gv)
        m_hat = mn * (1.0 / (1.0 - ADAM_B1 ** ADAM_STEP))
        v_hat = vn * (1.0 / (1.0 - ADAM_B2 ** ADAM_STEP))
        d_ref[...] = -ADAM_LR * (m_hat / (jnp.sqrt(v_hat) + ADAM_EPS) + ADAM_WD * w_ref[...])
        nm_ref[...] = mn
        nv_ref[...] = vn

    blk = pl.BlockSpec((tr, c), lambda i: (i, 0))
    shp = jax.ShapeDtypeStruct((r, c), F32)
    return pl.pallas_call(
        body, grid=(r // tr,), in_specs=[blk] * 4, out_specs=[blk] * 3, out_shape=[shp] * 3,
        compiler_params=_params(("parallel",)), name=name,
    )(w, m, v, g)


def _place():
    x, y, c = lax.axis_index("x"), lax.axis_index("y"), lax.axis_index("c")
    chips = [(1 - x, y), (x, 1 - y), (1 - x, 1 - y)]
    return x, y, c, chips


def _window(ref, kind, s, half=None):
    lead = () if half is None else (half,)
    b, c = ref.shape[-2], ref.shape[-1]
    if kind == "col":
        return ref.at[lead + (slice(None), slice(None), pl.ds(s * (c // N_CHIPS), c // N_CHIPS))]
    if kind == "row":
        return ref.at[lead + (slice(None), pl.ds(s * (b // N_CHIPS), b // N_CHIPS), slice(None))]
    return ref.at[lead + (s,)]


def _window_shape(shape3, kind):
    a, b, c = shape3
    return {"col": (a, b, c // N_CHIPS), "row": (a, b // N_CHIPS, c), "slab": (b, c)}[kind]


def _allgather(tensors, kinds, *, name):
    n = len(tensors)

    def body(*refs):
        bufs = refs[n:2 * n]
        send, recv = refs[2 * n:]
        x, y, c, chips = _place()
        me = 2 * x + y
        sib = (x, y, 1 - c)

        def rcopy(i, k, win, to):
            return pltpu.make_async_remote_copy(src_ref=win, dst_ref=win, send_sem=send.at[i * 6 + k], recv_sem=recv.at[i * 6 + k],
                                                device_id=to, device_id_type=MESH)

        started = []
        for i in range(n):
            for k, (px, py) in enumerate(chips):
                cp = rcopy(i, k, _window(bufs[i], kinds[i], me, c), (px, py, c))
                cp.start()
                started.append(cp)
        for i in range(n):
            for k, (px, py) in enumerate(chips):
                landed = _window(bufs[i], kinds[i], 2 * px + py, c)
                rcopy(i, k, landed, (px, py, c)).wait_recv()
                fw = rcopy(i, 3 + k, landed, sib)
                fw.start()
                started.append(fw)
        for i in range(n):
            for k, (px, py) in enumerate(chips):
                rcopy(i, 3 + k, _window(bufs[i], kinds[i], 2 * px + py, 1 - c), sib).wait_recv()
        for cp in started:
            cp.wait_send()

    return pl.pallas_call(
        body, in_specs=[ANY] * n, out_specs=[ANY] * n,
        out_shape=[jax.ShapeDtypeStruct(t.shape, t.dtype) for t in tensors],
        scratch_shapes=[pltpu.SemaphoreType.DMA((6 * n,)), pltpu.SemaphoreType.DMA((6 * n,))],
        input_output_aliases={i: i for i in range(n)},
        name=name,
    )(*tensors)


def _rows_tile(rows, cols, sub):
    best = None
    for t in range(sub, rows + 1, sub):
        if rows % t == 0 and t * cols <= 512 * 1024:
            best = t
    return rows if best is None else best


def _sequencer(name, cid, n_sems, peers_of, body):
    @pl.kernel(mesh=plsc.ScalarSubcoreMesh(axis_name="seq", num_cores=1), name=name,
               scratch_types=(pltpu.SemaphoreType.DMA((n_sems,)), pltpu.SemaphoreType.DMA((n_sems,))),
               compiler_params=pltpu.CompilerParams(collective_id=cid))
    def launch(send, recv):
        x, y, c, chips = _place()
        peers = peers_of(x, y, c, chips)
        barrier = pltpu.get_barrier_semaphore()
        for peer in peers:
            pl.semaphore_signal(barrier, inc=1, device_id=peer, device_id_type=MESH)
        pl.semaphore_wait(barrier, len(peers))
        body(send, recv)

    launch()


def _half_of_full(ref, kind, h):
    if kind == "col":
        b = ref.shape[0]
        return ref.at[pl.ds(h * (b // 2), b // 2), :]
    if kind == "row":
        c = ref.shape[1]
        return ref.at[:, pl.ds(h * (c // 2), c // 2)]
    b = ref.shape[1]
    return ref.at[:, pl.ds(h * (b // 2), b // 2), :]


def _half_shape(full, kind):
    if kind == "col":
        return (full[0] // 2, full[1])
    if kind == "row":
        return (full[0], full[1] // 2)
    return (full[0], full[1] // 2, full[2])


def _win_of_half(ref, kind, s):
    if kind == "col":
        c = ref.shape[1]
        return ref.at[:, pl.ds(s * (c // N_CHIPS), c // N_CHIPS)]
    if kind == "row":
        b = ref.shape[0]
        return ref.at[pl.ds(s * (b // N_CHIPS), b // N_CHIPS), :]
    return ref.at[s]


def _win_shape(half, kind):
    if kind == "col":
        return (half[0], half[1] // N_CHIPS)
    if kind == "row":
        return (half[0] // N_CHIPS, half[1])
    return half[1:]


def _seq_swap(parts, kinds, *, name):
    n = len(parts)
    srcs = [jax.new_ref(p, memory_space=pltpu.MemorySpace.HBM) for p in parts]
    outs = [jax.empty_ref(jax.ShapeDtypeStruct(_half_shape(p.shape, k), p.dtype), memory_space=pltpu.MemorySpace.HBM)
            for p, k in zip(parts, kinds)]

    def body(send, recv):
        x, y, c, _ = _place()
        cps = []
        for i in range(n):
            cp = pltpu.make_async_remote_copy(src_ref=_half_of_full(srcs[i], kinds[i], 1 - c), dst_ref=outs[i], send_sem=send.at[i],
                                              recv_sem=recv.at[i], device_id=(x, y, 1 - c), device_id_type=MESH)
            cp.start()
            cps.append(cp)
        for cp in cps:
            cp.wait()

    _sequencer(name, 2, n, lambda x, y, c, chips: [(x, y, 1 - c)], body)
    return [o[...] for o in outs]


def _seq_scatter(halves, kinds, *, name):
    n = len(halves)
    srcs = [jax.new_ref(h, memory_space=pltpu.MemorySpace.HBM) for h in halves]
    outs = [jax.empty_ref(jax.ShapeDtypeStruct((3,) + _win_shape(h.shape, k), h.dtype), memory_space=pltpu.MemorySpace.HBM)
            for h, k in zip(halves, kinds)]

    def body(send, recv):
        x, y, c, chips = _place()
        cps = []
        for i in range(n):
            for k, (px, py) in enumerate(chips):
                cp = pltpu.make_async_remote_copy(src_ref=_win_of_half(srcs[i], kinds[i], 2 * px + py), dst_ref=outs[i].at[k],
                                                  send_sem=send.at[3 * i + k], recv_sem=recv.at[3 * i + k],
                                                  device_id=(px, py, c), device_id_type=MESH)
                cp.start()
                cps.append(cp)
        for cp in cps:
            cp.wait()

    _sequencer(name, 3, 3 * n, lambda x, y, c, chips: [(px, py, c) for px, py in chips], body)
    return [o[...] for o in outs]


def _add_half(g, p, kind, where, after, *, name):
    if kind == "slab":
        s, b2, c = p.shape
        tr = _rows_tile(b2, c, 16)
        nr = b2 // tr
        grid = (s, nr)
        g_spec = pl.BlockSpec((None, tr, c), lambda i, r, w: (i, w[1] * nr + r, 0))
        p_spec = pl.BlockSpec((None, tr, c), lambda i, r, w: (i, r, 0))
    elif kind == "col":
        b2, c = p.shape
        tr = _rows_tile(b2, c, 16)
        nr = b2 // tr
        grid = (1, nr)
        g_spec = pl.BlockSpec((tr, c), lambda i, r, w: (w[1] * nr + r, 0))
        p_spec = pl.BlockSpec((tr, c), lambda i, r, w: (r, 0))
    else:
        b, c2 = p.shape
        tr = _rows_tile(b, c2, 16)
        grid = (1, b // tr)
        g_spec = pl.BlockSpec((tr, c2), lambda i, r, w: (r, w[1]))
        p_spec = pl.BlockSpec((tr, c2), lambda i, r, w: (r, 0))

    def body(w_ref, g_ref, p_ref, *rest):
        o_ref = rest[-1]
        o_ref[...] = (g_ref[...].astype(F32) + p_ref[...].astype(F32)).astype(o_ref.dtype)

    extra = [] if after is None else [after]
    return pl.pallas_call(
        body,
        grid_spec=pltpu.PrefetchScalarGridSpec(num_scalar_prefetch=1, grid=grid, in_specs=[g_spec, p_spec] + [ANY] * len(extra),
                                               out_specs=p_spec),
        out_shape=jax.ShapeDtypeStruct(p.shape, g.dtype),
        compiler_params=_params(("parallel", "parallel")), name=name,
    )(where, g, p, *extra)


def _sum_chips(r, h, kind, where, layer, layers, out_buf, after, *, name):
    _, br, cr = r.shape
    tr = _rows_tile(br, cr, 16)
    nr = br // tr
    if kind == "col":
        h_spec = pl.BlockSpec((tr, cr), lambda j, w: (j, w[0]))
        o_shape, o_spec = (layers, 2 * br, cr), pl.BlockSpec((None, tr, cr), lambda j, w: (layer, w[1] * nr + j, 0))
    elif kind == "row":
        h_spec = pl.BlockSpec((tr, cr), lambda j, w: (w[0] * nr + j, 0))
        o_shape, o_spec = (layers, br, 2 * cr), pl.BlockSpec((None, tr, cr), lambda j, w: (layer, j, w[1]))
    else:
        h_spec = pl.BlockSpec((None, tr, cr), lambda j, w: (w[0], j, 0))
        o_shape, o_spec = (layers, 2 * br, cr), pl.BlockSpec((None, tr, cr), lambda j, w: (layer, w[1] * nr + j, 0))

    def body(w_ref, h_ref, r0_ref, r1_ref, r2_ref, *rest):
        o_ref, t_ref = rest[-2], rest[-1]
        o_ref[...] = ((h_ref[...].astype(F32) + r0_ref[...].astype(F32)) + r1_ref[...].astype(F32)) + r2_ref[...].astype(F32)
        t_ref[...] = jnp.zeros_like(t_ref)

    def slot(k):
        return pl.BlockSpec((None, tr, cr), lambda j, w: (k, j, 0))

    ins, specs, alias = [h, r, r, r], [h_spec, slot(0), slot(1), slot(2)], {}
    if after is not None:
        ins.append(after)
        specs.append(ANY)
    if out_buf is not None:
        alias = {1 + len(ins): 0}
        ins.append(out_buf)
        specs.append(ANY)
    return pl.pallas_call(
        body,
        grid_spec=pltpu.PrefetchScalarGridSpec(num_scalar_prefetch=1, grid=(nr,), in_specs=specs,
                                               out_specs=[o_spec, pl.BlockSpec((8, 128), lambda j, w: (0, 0))]),
        out_shape=[jax.ShapeDtypeStruct(o_shape, F32), jax.ShapeDtypeStruct((8, 128), F32)], input_output_aliases=alias,
        compiler_params=_params(("arbitrary",)), name=name,
    )(where, *ins)


def _join_halves(tensors, kinds, *, name):
    n = len(tensors)

    def mine(ref, kind, h):
        if kind == "row":
            c = ref.shape[2]
            return ref.at[:, :, pl.ds(h * (c // 2), c // 2)]
        b = ref.shape[1]
        return ref.at[:, pl.ds(h * (b // 2), b // 2), :]

    def body(*refs):
        bufs = refs[n:2 * n]
        send, recv = refs[2 * n:]
        x, y, c, _ = _place()
        cps = []
        for i in range(n):
            part = mine(bufs[i], kinds[i], c)
            cp = pltpu.make_async_remote_copy(src_ref=part, dst_ref=part, send_sem=send.at[i],
                                              recv_sem=recv.at[i], device_id=(x, y, 1 - c), device_id_type=MESH)
            cp.start()
            cps.append(cp)
        for i in range(n):
            other = mine(bufs[i], kinds[i], 1 - c)
            pltpu.make_async_remote_copy(src_ref=other, dst_ref=other, send_sem=send.at[i],
                                         recv_sem=recv.at[i], device_id=(x, y, 1 - c), device_id_type=MESH).wait_recv()
        for cp in cps:
            cp.wait_send()

    return pl.pallas_call(
        body, in_specs=[ANY] * n, out_specs=[ANY] * n,
        out_shape=[jax.ShapeDtypeStruct(t.shape, t.dtype) for t in tensors],
        scratch_shapes=[pltpu.SemaphoreType.DMA((n,)), pltpu.SemaphoreType.DMA((n,))],
        input_output_aliases={i: i for i in range(n)},
        name=name,
    )(*tensors)


def _win(ref, kind, s, h=None):
    if kind == "col":
        b, c = ref.shape
        cols = pl.ds(s * (c // N_CHIPS), c // N_CHIPS)
        return ref.at[:, cols] if h is None else ref.at[pl.ds(h * (b // 2), b // 2), cols]
    if kind == "row":
        b, c = ref.shape
        rows = pl.ds(s * (b // N_CHIPS), b // N_CHIPS)
        return ref.at[rows, :] if h is None else ref.at[rows, pl.ds(h * (c // 2), c // 2)]
    b = ref.shape[1]
    return ref.at[s] if h is None else ref.at[s, pl.ds(h * (b // 2), b // 2)]


def _half(ref, kind, h):
    b, c = ref.shape
    if kind == "row":
        return ref.at[:, pl.ds(h * (c // 2), c // 2)]
    return ref.at[pl.ds(h * (b // 2), b // 2), :]


def _full_shape(shard_shape, kind):
    b, c = shard_shape
    return {"col": (b, N_CHIPS * c), "row": (N_CHIPS * b, c), "slab": (N_CHIPS, b, c)}[kind]


def _gather_body(srcs, outs, kinds, send, recv):
    x, y, c, chips = _place()
    me = 2 * x + y
    sib = (x, y, 1 - c)

    def rcopy(i, k, src, dst, to):
        return pltpu.make_async_remote_copy(src_ref=src, dst_ref=dst, send_sem=send.at[7 * i + k], recv_sem=recv.at[7 * i + k],
                                            device_id=to, device_id_type=MESH)

    started = []
    for i, (src, out, kind) in enumerate(zip(srcs, outs, kinds)):
        own = rcopy(i, 6, src, _win(out, kind, me), sib)
        own.start()
        started.append(own)
        for k, (px, py) in enumerate(chips):
            cp = rcopy(i, k, _half(src, kind, c), _win(out, kind, me, c), (px, py, c))
            cp.start()
            started.append(cp)
    for i, (out, kind) in enumerate(zip(outs, kinds)):
        for k, (px, py) in enumerate(chips):
            landed = _win(out, kind, 2 * px + py, c)
            rcopy(i, k, landed, landed, (px, py, c)).wait_recv()
            fw = rcopy(i, 3 + k, landed, landed, sib)
            fw.start()
            started.append(fw)
    for i, (src, out, kind) in enumerate(zip(srcs, outs, kinds)):
        for k, (px, py) in enumerate(chips):
            other = _win(out, kind, 2 * px + py, 1 - c)
            rcopy(i, 3 + k, other, other, sib).wait_recv()
        rcopy(i, 6, src, _win(out, kind, me), sib).wait_recv()
    for cp in started:
        cp.wait_send()


def _seq_gather(shards, kinds, *, name, cid):
    n = len(shards)
    srcs = [jax.new_ref(s, memory_space=pltpu.MemorySpace.HBM) for s in shards]
    outs = [jax.empty_ref(jax.ShapeDtypeStruct(_full_shape(s.shape, k), s.dtype), memory_space=pltpu.MemorySpace.HBM)
            for s, k in zip(shards, kinds)]

    @pl.kernel(mesh=plsc.ScalarSubcoreMesh(axis_name="seq", num_cores=1), name=name,
               scratch_types=(pltpu.SemaphoreType.DMA((7 * n,)), pltpu.SemaphoreType.DMA((7 * n,))),
               compiler_params=pltpu.CompilerParams(collective_id=cid))
    def launch(send, recv):
        x, y, c, chips = _place()
        barrier = pltpu.get_barrier_semaphore()
        for px, py in chips:
            pl.semaphore_signal(barrier, inc=1, device_id=(px, py, c), device_id_type=MESH)
        pl.semaphore_signal(barrier, inc=1, device_id=(x, y, 1 - c), device_id_type=MESH)
        pl.semaphore_wait(barrier, 4)
        _gather_body(srcs, outs, kinds, send, recv)

    launch()
    return [o[...] for o in outs]


KIND = dict(w_qkv_a="slab", w_o_a="col", w_q_b="row", w_o_b="row", w_kvf="slab", w_up="col", w_down="row", small="slab")
LAYERS = dict(w_qkv_a=N_A, w_o_a=N_A, w_q_b=DEPTH - N_A, w_o_b=DEPTH - N_A, w_kvf=1, w_up=DEPTH, w_down=DEPTH, small=1)
SMALL_W = 1792
SMALL_ROWS = 8


class _Reducer:
    def __init__(self, where):
        self.where = where
        self.acc = {nm: None for nm in KIND}
        self.pending = None

    def __call__(self, group, tag):
        names, layers, parts = zip(*group)
        kinds = [KIND[nm] for nm in names]
        summed = self._sum_pending(after=parts[-1])
        sib = _seq_swap(list(parts), kinds, name="reduce_swap_" + tag)
        halves = []
        for g, p, k, nm in zip(parts, sib, kinds, names):
            halves.append(_add_half(g, p, k, self.where, halves[-1] if halves else None, name="reduce_add_" + nm))
        landed = _seq_scatter(halves, kinds, name="reduce_scatter_" + tag)
        self.pending = (names, layers, landed, halves, kinds)
        return [halves[-1], summed]

    def flush(self, after):
        return self._sum_pending(after)

    def _sum_pending(self, after):
        if self.pending is None:
            return None
        for nm, l, r, h, k in zip(*self.pending):
            self.acc[nm], after = _sum_chips(r, h, k, self.where, l, LAYERS[nm], self.acc[nm], after, name="reduce_sum_" + nm)
        self.pending = None
        return after

    def finish(self):
        self._sum_pending(after=None)
        names = list(KIND)
        joined = _join_halves([self.acc[nm] for nm in names], [KIND[nm] for nm in names], name="reduce_pair_join")
        return dict(zip(names, joined))


def _headsum_matrix():
    r = lax.broadcasted_iota(jnp.int32, (128, 128), 0) // HD
    c = lax.broadcasted_iota(jnp.int32, (128, 128), 1) // HD
    return jnp.where(r == c, 1.0, 0.0).astype(BF16)


def kernel(x, norm_gains, w_qkv_a, w_o_a, w_q_b, w_o_b, kv_norm, w_kvf, b_f, w_up, conv_w, conv_b, w_down, loss_target, m_norm_gains, m_w_qkv_a, m_w_o_a, m_w_q_b, m_w_o_b, m_kv_norm, m_w_kvf, m_b_f, m_w_up, m_conv_w, m_conv_b, m_w_down, v_norm_gains, v_w_qkv_a, v_w_o_a, v_w_q_b, v_w_o_b, v_kv_norm, v_w_kvf, v_b_f, v_w_up, v_conv_w, v_conv_b, v_w_down):
    xi, yi, ci = lax.axis_index("x"), lax.axis_index("y"), lax.axis_index("c")
    chip = 2 * xi + yi
    where = jnp.stack([chip, ci]).astype(jnp.int32)
    ws = dict(norm_gains=norm_gains, w_qkv_a=w_qkv_a, w_o_a=w_o_a, w_q_b=w_q_b, w_o_b=w_o_b, kv_norm=kv_norm, w_kvf=w_kvf,
              b_f=b_f, w_up=w_up, conv_w=conv_w, conv_b=conv_b, w_down=w_down)
    ms = dict(norm_gains=m_norm_gains, w_qkv_a=m_w_qkv_a, w_o_a=m_w_o_a, w_q_b=m_w_q_b, w_o_b=m_w_o_b, kv_norm=m_kv_norm,
              w_kvf=m_w_kvf, b_f=m_b_f, w_up=m_w_up, conv_w=m_conv_w, conv_b=m_conv_b, w_down=m_w_down)
    vs = dict(norm_gains=v_norm_gains, w_qkv_a=v_w_qkv_a, w_o_a=v_w_o_a, w_q_b=v_w_q_b, w_o_b=v_w_o_b, kv_norm=v_kv_norm,
              w_kvf=v_w_kvf, b_f=v_b_f, w_up=v_w_up, conv_w=v_conv_w, conv_b=v_conv_b, w_down=v_w_down)

    small = jnp.concatenate([
        jnp.pad(norm_gains.reshape(16, 256), ((0, 0), (0, 1408 - 256))),
        jnp.pad(conv_w.reshape(12, 1408), ((0, 4), (0, 0)))], axis=0)
    big = [nm for nm in KIND if nm != "small"]
    half = {nm: ws[nm].astype(BF16) for nm in big}
    W = {nm: [None] * LAYERS[nm] for nm in big if nm != "w_kvf"}
    g_small = None
    groups = [("0a", [("w_qkv_a", 0), ("w_o_a", 0), ("small", 0)]), ("0b", [("w_up", 0)]), ("0c", [("w_down", 0)]),
              ("1", [("w_qkv_a", 1), ("w_o_a", 1), ("w_up", 1), ("w_down", 1)]),
              ("2", [("w_kvf", 0), ("w_q_b", 0), ("w_o_b", 0), ("w_up", 2), ("w_down", 2)]),
              ("3", [("w_q_b", 1), ("w_o_b", 1), ("w_up", 3), ("w_down", 3)])]
    for tag, group in groups:
        shards = [small if nm == "small" else half[nm] if nm == "w_kvf" else half[nm][i] for nm, i in group]
        got = _seq_gather(shards, [KIND[nm] for nm, _ in group], name="gather_layer" + tag, cid=1)
        for (nm, i), g in zip(group, got):
            if nm == "small":
                g_small = g
            elif nm == "w_kvf":
                W[nm] = g.transpose(1, 0, 2).reshape(D, 2 * D + 16)
            else:
                W[nm][i] = g.transpose(1, 0, 2).reshape(D, 3 * A_W) if nm == "w_qkv_a" else g
    gains = g_small[:, :16, :256].transpose(1, 0, 2).reshape(DEPTH, 4, 1, D)
    cw_full = g_small[:, 16:28, :].transpose(1, 0, 2).reshape(DEPTH, 3, 2 * D_FF)
    cb_full = conv_b.reshape(DEPTH, 1, 2 * D_FF)

    reducer = _Reducer(where)
    sq, dh = _fwd_bwd(x[0], loss_target[0], W, gains, cw_full, cb_full, kv_norm, b_f, reducer)
    loss = lax.psum(sq[0, 0] * (0.5 / D), ("x", "y", "c"))
    return _update(loss, dh[None], reducer.finish(), chip, ws, ms, vs)


def _fwd_bwd(h, target, W, gains, cw_full, cb_full, kv_norm, b_f, reduce):
    w_kv = W["w_kvf"][:, :2 * D]
    w_kvf_pad = jnp.pad(W["w_kvf"], ((0, 0), (0, 128 - 16)))
    w_f = w_kvf_pad[:, 2 * D:]
    kvn_g = kv_norm.reshape(1, D)
    bf_pad = jnp.pad(b_f, (0, 128 - 16)).reshape(1, 128)
    tabs = _rope_tables()
    headsum = _headsum_matrix()

    saved = []
    kv = zf = c_row = kvn = h_kv = None
    xn = _rms_fwd(h, gains[0][0], out_dtype=BF16, name="rms_in")
    for l in range(DEPTH):
        s = {"h": h}
        g = gains[l]
        s["xn"] = xn
        if l < N_A:
            qkv = _matmul(xn, W["w_qkv_a"][l], mode="nn", out_dtype=F32, name="mm_qkv", mnk=(T, 3 * A_W, D), tn=768)
            qkvp = _rope_fwd(qkv, tabs).reshape(9, 2, T, 128)
            o_p, lse_p = _band_fwd(qkvp)
            att, o3, lse3 = _combine_fwd(o_p, lse_p)
            s.update(qkvp=qkvp, o3=o3, lse3=lse3, lse_p=lse_p, att=att)
            mix = _matmul(att, W["w_o_a"][l], mode="nn", out_dtype=F32, name="mm_oa", mnk=(T, D, A_W))
        else:
            j = l - N_A
            if l == N_A:
                h_kv = h
                kvn = _rms_fwd(h, kvn_g, out_dtype=BF16, name="rms_in")
                kv = _matmul(kvn, w_kv, mode="nn", out_dtype=BF16, name="mm_kv")
                zf = _matmul(kvn, w_f, mode="nn", out_dtype=F32, name="mm_f")
                cum = _gates_fwd(zf, bf_pad)[:, :16]
                c_row = cum.T.reshape(8, 2, T)
            q = _matmul(xn, W["w_q_b"][j], mode="nn", out_dtype=BF16, name="mm_qb", mnk=(T, D, D), alpha=HD ** -0.5)
            o = _fox_fwd(q, kv, c_row)
            s.update(q=q, o=o)
            mix = _matmul(o, W["w_o_b"][j], mode="nn", out_dtype=F32, name="mm_ob", mnk=(T, D, D))
        s["mix"] = mix
        h1, xn2 = _rms_res_in(mix, g[1], h, g[2], name="rms_res_in")
        a = _matmul(xn2, W["w_up"][l], mode="nn", out_dtype=F32, name="mm_up", mnk=(T, 2 * D_FF, D))
        u = _convgate_fwd(a, cw_full[l], cb_full[l])
        f = _matmul(u, W["w_down"][l], mode="nn", out_dtype=F32, name="mm_down", mnk=(T, D, D_FF), tm=1024, tk=D_FF)
        if l + 1 < DEPTH:
            h, xn = _rms_res_in(f, g[3], h1, gains[l + 1][0], name="rms_res_in")
        else:
            h = _rms_fwd(f, g[3], res=h1, out_dtype=F32, name="rms_res")
        s.update(h1=h1, xn2=xn2, a=a, u=u, f=f)
        saved.append(s)

    dh, sq = _loss_head(h, target)

    d_gains = [[None] * 4 for _ in range(DEPTH)]
    d_cw, d_cb = [None] * DEPTH, [None] * DEPTH
    zeros_td = jnp.zeros((T, D), F32)
    fox_acc = (zeros_td, zeros_td, jnp.zeros((D // 128, 8, T), F32))
    d_kvnorm = d_bf = token = df = None

    def dw(nm, a, b, **kw):
        return _matmul(a, b, mode="tn", out_dtype=BF16, name="mm_dw_" + nm, **kw)

    flush = getattr(reduce, "flush", lambda after: None)

    def slabs(full, width):
        return full.reshape(full.shape[0], N_CHIPS, width).transpose(1, 0, 2)

    for l in reversed(range(DEPTH)):
        s = saved[l]
        g = gains[l]
        if df is None:
            df, d_gains[l][3] = _rms_bwd(dh, s["f"], g[3], out_dtype=BF16, name="rms_bwd")
        du = _matmul(df, W["w_down"][l], mode="nt", out_dtype=F32, name="mm_down_dx", mnk=(T, D_FF, D), tn=256, after=token)
        g_down = dw("w_down", s["u"], df, tm=1408, tn=1024)
        da, d_cw[l], d_cb[l] = _convgate_bwd(s["a"], du, cw_full[l], cb_full[l])
        dxn2 = _matmul(da, W["w_up"][l], mode="nt", out_dtype=F32, name="mm_up_dx", mnk=(T, D, 2 * D_FF), tm=1024, tn=1024, tk=1408,
                       a_map=_halves_a)
        g_up = dw("w_up", s["xn2"], da, mnk=(D, 2 * D_FF, T), tn=1408, b_map=_halves_b)
        token = reduce([("w_down", l, g_down), ("w_up", l, g_up)], "ffn%d" % l)
        dh1, dmix, d_gains[l][2], d_gains[l][1] = _rms_bwd2(dxn2, s["h1"], g[2], dh, s["mix"], g[1], name="rms_bwd2")
        if l < N_A:
            datt = _matmul(dmix, W["w_o_a"][l], mode="nt", out_dtype=F32, name="mm_oa_dx", mnk=(T, A_W, D), tn=768, after=token)
            g_o = dw("w_o_a", s["att"], dmix, tm=768, tn=1024)
            do_p, dlt_p = _combine_bwd(datt, s["o3"], s["lse3"], headsum)
            dqkv = None
            for which, d in enumerate(_band_bwd(s["qkvp"], do_p, s["lse_p"], dlt_p)):
                dqkv = _rope_bwd(d, which, tabs, dqkv)
            dxn = _matmul(dqkv, W["w_qkv_a"][l], mode="nt", out_dtype=F32, name="mm_qkv_dx", mnk=(T, D, 3 * A_W), tm=1024, tn=1024, tk=3 * A_W,
                          after=[flush(dqkv)])
            g_qkv = dw("w_qkv_a", s["xn"], dqkv, tn=768)
            group = [("w_o_a", l, g_o), ("w_qkv_a", l, slabs(g_qkv, 576))]
        else:
            j = l - N_A
            do = _matmul(dmix, W["w_o_b"][j], mode="nt", out_dtype=BF16, name="mm_ob_dx", mnk=(T, D, D), after=token)
            g_o = dw("w_o_b", s["o"], dmix, tn=1024)
            dq, *fox_acc = _fox_bwd(s["q"], kv, do, c_row, fox_acc)
            dxn = _matmul(dq, W["w_q_b"][j], mode="nt", out_dtype=F32, name="mm_qb_dx", mnk=(T, D, D), after=[flush(dq)])
            g_q = dw("w_q_b", s["xn"], dq, tn=1024)
            group = [("w_o_b", j, g_o), ("w_q_b", j, g_q)]
        if l > 0 and l != N_A:
            dh, df, d_gains[l][0], d_gains[l - 1][3] = _rms_bwd2(dxn, s["h"], g[0], dh1, saved[l - 1]["f"], gains[l - 1][3],
                                                                 name="rms_bwd2")
        else:
            dh, d_gains[l][0] = _rms_bwd(dxn, s["h"], g[0], dres=dh1, out_dtype=F32, name="rms_bwd_res")
            df = None
        if l == N_A:
            dk, dv, dck = fox_acc
            dc16 = -dck[:, :2, :].reshape(16, T).T
            dzf, d_bf = _gates_bwd(jnp.pad(dc16, ((0, 0), (0, 128 - 16))), zf, bf_pad)
            dkvf = jnp.concatenate([dk.astype(BF16), dv.astype(BF16), dzf], axis=1)
            g_kvf = _matmul(kvn, dkvf, mode="tn", out_dtype=BF16, name="mm_kvf_dw", tm=512, tn=2 * D + 128)[:, :2 * D + 16]
            dkvn = _matmul(dkvf, w_kvf_pad, mode="nt", out_dtype=F32, name="mm_kvf_dx", tm=1024, tn=1024, tk=2 * D + 128)
            dh, d_kvnorm = _rms_bwd(dkvn, h_kv, kvn_g, dres=dh, out_dtype=F32, name="rms_bwd_res")
            group.append(("w_kvf", 0, slabs(g_kvf, 516)))
        token = reduce(group, "mix%d" % l)
    small_flat = jnp.concatenate([
        jnp.stack([jnp.stack(r) for r in d_gains]).reshape(-1),
        jnp.stack(d_cw).transpose(0, 2, 1, 3).reshape(-1),
        jnp.stack(d_cb).reshape(-1),
        d_kvnorm.reshape(-1), d_bf[0, :16]])
    small = jnp.pad(small_flat, (0, 2 * N_CHIPS * SMALL_ROWS * SMALL_W - small_flat.shape[0]))
    reduce([("small", 0, small.reshape(N_CHIPS, 2 * SMALL_ROWS, SMALL_W))], "small")
    return sq, dh


def _update(loss, grad_x, reduced, chip, ws, ms, vs):
    red_s = reduced.pop("small")
    buf_s = lax.dynamic_update_slice(jnp.zeros((2, N_CHIPS, SMALL_ROWS, SMALL_W), F32), red_s.reshape(2, 1, SMALL_ROWS, SMALL_W),
                                     (0, chip, 0, 0))
    (all_s,) = _allgather([buf_s], ["slab"], name="gather_small_grads")
    sflat = all_s.transpose(1, 0, 2, 3).reshape(-1)

    grads = {nm: r.reshape(ws[nm].shape) for nm, r in reduced.items()}
    o = 0
    g_gains_full = sflat[o:o + 16 * D].reshape(DEPTH, 4, D); o += 16 * D
    g_cw_full = sflat[o:o + 12 * 2 * D_FF].reshape(DEPTH, 3, 2 * D_FF); o += 12 * 2 * D_FF
    grads["conv_b"] = sflat[o:o + 4 * 2 * D_FF].reshape(DEPTH, 2 * D_FF); o += 4 * 2 * D_FF
    grads["kv_norm"] = sflat[o:o + D]; o += D
    grads["b_f"] = sflat[o:o + 16]
    grads["norm_gains"] = lax.dynamic_slice_in_dim(g_gains_full, chip * 256, 256, axis=2)
    grads["conv_w"] = lax.dynamic_slice_in_dim(g_cw_full, chip * 1408, 1408, axis=2)

    names = ["norm_gains", "w_qkv_a", "w_o_a", "w_q_b", "w_o_b", "kv_norm", "w_kvf", "b_f", "w_up", "conv_w", "conv_b", "w_down"]
    deltas, new_m, new_v = {}, {}, {}
    for nm in names:
        shp = ws[nm].shape
        two = (math.prod(shp[:-1]), shp[-1]) if len(shp) > 1 else (1, shp[0])
        d, m2, v2 = _adamw(ws[nm].reshape(two), ms[nm].reshape(two), vs[nm].reshape(two), grads[nm].reshape(two),
                           name="adamw_" + nm)
        deltas[nm], new_m[nm], new_v[nm] = d.reshape(shp), m2.reshape(shp), v2.reshape(shp)

    return (loss, grad_x, *[grads[nm] for nm in names], *[deltas[nm] for nm in names],
            *[new_m[nm] for nm in names], *[new_v[nm] for nm in names])
```

```python
import math

import jax
import jax.numpy as jnp
from jax import lax
from jax.experimental import pallas as pl
from jax.experimental.pallas import tpu as pltpu
from jax.experimental.pallas import tpu_sc as plsc

F32 = jnp.float32
BF16 = jnp.bfloat16
MESH = pl.DeviceIdType.MESH
ANY = pl.BlockSpec(memory_space=pl.ANY)

T = 2048
D = 1024
HD = 64
DEPTH = 4
N_A = 2
A_W = 768
GW = 256
DIL = (1, 4, 16)
BLK = 128
D_FF = 2816
ROPE_THETA = 500000.0
EPS = 1e-6
NEG = -1e30
N_CHIPS = 4
FQ = 256
CT = 128
VMEM_BIG = 48 * 1024 * 1024

ADAM_LR, ADAM_B1, ADAM_B2, ADAM_EPS, ADAM_WD, ADAM_STEP = 0.001, 0.9, 0.999, 1e-08, 0.01, 10

NN = (((1,), (0,)), ((), ()))
NT = (((1,), (1,)), ((), ()))
TN = (((0,), (0,)), ((), ()))


def _dot(a, b, dims):
    return lax.dot_general(a, b, dims, preferred_element_type=F32)


def _pick(dim, pref):
    if dim <= pref:
        return dim
    best = None
    for t in range(128, pref + 1, 128):
        if dim % t == 0:
            best = t
    assert best is not None, (dim, pref)
    return best


def _params(sem=None, vmem=None):
    kw = {}
    if sem is not None:
        kw["dimension_semantics"] = sem
    if vmem is not None:
        kw["vmem_limit_bytes"] = vmem
    return pltpu.CompilerParams(**kw)


def _matmul(a, b, *, mode, out_dtype, name, mnk=None, alpha=None, tm=2048, tn=512, tk=2048, a_map=None, b_map=None, after=None):
    if mnk is not None:
        M, N, K = mnk
    elif mode == "nn":
        (M, K), (_, N) = a.shape, b.shape
    elif mode == "nt":
        (M, K), (N, _) = a.shape, b.shape
    else:
        (K, M), (_, N) = a.shape, b.shape
    tm, tn, tk = _pick(M, tm), _pick(N, tn), _pick(K, tk)
    nk = K // tk
    dims = {"nn": NN, "nt": NT, "tn": TN}[mode]
    after = [t for t in (after or ()) if t is not None]
    n_in = 2 + len(after)

    def body(*refs):
        a_ref, b_ref = refs[0], refs[1]
        o_ref = refs[n_in]
        k = pl.program_id(2)

        def finish(r):
            if alpha is not None:
                r = r * alpha
            o_ref[...] = r.astype(out_dtype)

        def product():
            return _dot(a_ref[...], b_ref[...], dims)

        if nk == 1:
            finish(product())
            return
        acc_ref = refs[n_in + 1]

        @pl.when(k == 0)
        def _():
            acc_ref[...] = product()

        @pl.when((k > 0) & (k < nk - 1))
        def _():
            acc_ref[...] += product()

        @pl.when(k == nk - 1)
        def _():
            finish(acc_ref[...] + product())

    a_blk = (tk, tm) if mode == "tn" else (tm, tk)
    b_blk = (tn, tk) if mode == "nt" else (tk, tn)
    if a_map is not None:
        a_spec = pl.BlockSpec((None,) + a_blk, a_map(tm, tn, tk))
    elif mode == "tn":
        a_spec = pl.BlockSpec(a_blk, lambda i, j, k: (k, i))
    else:
        a_spec = pl.BlockSpec(a_blk, lambda i, j, k: (i, k))
    if b_map is not None:
        b_spec = pl.BlockSpec((None,) + b_blk, b_map(tm, tn, tk))
    elif mode == "nt":
        b_spec = pl.BlockSpec(b_blk, lambda i, j, k: (j, k))
    else:
        b_spec = pl.BlockSpec(b_blk, lambda i, j, k: (k, j))
    return pl.pallas_call(
        body,
        grid=(M // tm, N // tn, nk),
        in_specs=[a_spec, b_spec] + [ANY] * len(after),
        out_specs=pl.BlockSpec((tm, tn), lambda i, j, k: (i, j)),
        out_shape=jax.ShapeDtypeStruct((M, N), out_dtype),
        scratch_shapes=[pltpu.VMEM((tm, tn), F32)] if nk > 1 else [],
        compiler_params=_params(("parallel", "parallel", "arbitrary"), VMEM_BIG),
        name=name,
    )(a, b, *after)


def _rms_fwd(x, g, *, out_dtype, name, res=None, tr=256):
    n, d = x.shape

    def body(*refs):
        x_ref, g_ref = refs[0], refs[1]
        o_ref = refs[-1]
        xv = x_ref[...].astype(F32)
        y = xv * lax.rsqrt(jnp.mean(xv * xv, axis=-1, keepdims=True) + EPS) * g_ref[...]
        if res is not None:
            y = y + refs[2][...]
        o_ref[...] = y.astype(out_dtype)

    row = pl.BlockSpec((tr, d), lambda i: (i, 0))
    vec = pl.BlockSpec((1, d), lambda i: (0, 0))
    ins = [x, g] + ([] if res is None else [res])
    specs = [row, vec] + ([] if res is None else [row])
    return pl.pallas_call(
        body, grid=(n // tr,), in_specs=specs, out_specs=row,
        out_shape=jax.ShapeDtypeStruct((n, d), out_dtype),
        compiler_params=_params(("parallel",)), name=name,
    )(*ins)


def _rms_bwd(dy, x, g, *, out_dtype, name, dres=None, tr=256):
    n, d = x.shape

    def body(*refs):
        dy_ref, x_ref, g_ref = refs[0], refs[1], refs[2]
        dx_ref, dg_ref = refs[-2], refs[-1]
        xv = x_ref[...].astype(F32)
        dyv = dy_ref[...].astype(F32)
        rstd = lax.rsqrt(jnp.mean(xv * xv, axis=-1, keepdims=True) + EPS)
        xhat = xv * rstd
        dxh = dyv * g_ref[...]
        dx = rstd * (dxh - xhat * jnp.mean(dxh * xhat, axis=-1, keepdims=True))
        if dres is not None:
            dx = dx + refs[3][...]
        dx_ref[...] = dx.astype(out_dtype)

        @pl.when(pl.program_id(0) == 0)
        def _():
            dg_ref[...] = jnp.zeros_like(dg_ref)

        dg_ref[...] += jnp.sum(dyv * xhat, axis=0, keepdims=True)

    row = pl.BlockSpec((tr, d), lambda i: (i, 0))
    vec = pl.BlockSpec((1, d), lambda i: (0, 0))
    ins = [dy, x, g] + ([] if dres is None else [dres])
    specs = [row, row, vec] + ([] if dres is None else [row])
    return pl.pallas_call(
        body, grid=(n // tr,), in_specs=specs, out_specs=[row, vec],
        out_shape=[jax.ShapeDtypeStruct((n, d), out_dtype), jax.ShapeDtypeStruct((1, d), F32)],
        compiler_params=_params(("arbitrary",)), name=name,
    )(*ins)


def _rms_res_in(x, g_res, res, g_in, *, name, tr=256):
    n, d = x.shape

    def body(x_ref, gr_ref, r_ref, gi_ref, h_ref, n_ref):
        xv = x_ref[...].astype(F32)
        h = r_ref[...] + xv * lax.rsqrt(jnp.mean(xv * xv, axis=-1, keepdims=True) + EPS) * gr_ref[...]
        h_ref[...] = h
        n_ref[...] = (h * lax.rsqrt(jnp.mean(h * h, axis=-1, keepdims=True) + EPS) * gi_ref[...]).astype(BF16)

    row = pl.BlockSpec((tr, d), lambda i: (i, 0))
    vec = pl.BlockSpec((1, d), lambda i: (0, 0))
    return pl.pallas_call(
        body, grid=(n // tr,), in_specs=[row, vec, row, vec], out_specs=[row, row],
        out_shape=[jax.ShapeDtypeStruct((n, d), F32), jax.ShapeDtypeStruct((n, d), BF16)],
        compiler_params=_params(("parallel",)), name=name,
    )(x, g_res, res, g_in)


def _rms_bwd2(dy, x, g, dres, x2, g2, *, name, tr=256):
    n, d = x.shape

    def one(dyv, xv, gv):
        rstd = lax.rsqrt(jnp.mean(xv * xv, axis=-1, keepdims=True) + EPS)
        xhat = xv * rstd
        dxh = dyv * gv
        return rstd * (dxh - xhat * jnp.mean(dxh * xhat, axis=-1, keepdims=True)), jnp.sum(dyv * xhat, axis=0, keepdims=True)

    def body(dy_ref, x_ref, g_ref, r_ref, x2_ref, g2_ref, dx_ref, d2_ref, dg_ref, dg2_ref):
        dx, dg = one(dy_ref[...].astype(F32), x_ref[...].astype(F32), g_ref[...])
        dx = dx + r_ref[...]
        dx_ref[...] = dx
        d2, dg2 = one(dx, x2_ref[...].astype(F32), g2_ref[...])
        d2_ref[...] = d2.astype(BF16)

        @pl.when(pl.program_id(0) == 0)
        def _():
            dg_ref[...] = jnp.zeros_like(dg_ref)
            dg2_ref[...] = jnp.zeros_like(dg2_ref)

        dg_ref[...] += dg
        dg2_ref[...] += dg2

    row = pl.BlockSpec((tr, d), lambda i: (i, 0))
    vec = pl.BlockSpec((1, d), lambda i: (0, 0))
    return pl.pallas_call(
        body, grid=(n // tr,), in_specs=[row, row, vec, row, row, vec], out_specs=[row, row, vec, vec],
        out_shape=[jax.ShapeDtypeStruct((n, d), F32), jax.ShapeDtypeStruct((n, d), BF16),
                   jax.ShapeDtypeStruct((1, d), F32), jax.ShapeDtypeStruct((1, d), F32)],
        compiler_params=_params(("arbitrary",)), name=name,
    )(dy, x, g, dres, x2, g2)


def _loss_head(h, target, *, tr=256):
    n, d = h.shape

    def body(h_ref, t_ref, dh_ref, s_ref):
        err = h_ref[...] - t_ref[...]
        dh_ref[...] = err * (1.0 / d)

        @pl.when(pl.program_id(0) == 0)
        def _():
            s_ref[...] = jnp.zeros_like(s_ref)

        s_ref[...] += jnp.sum(err * err)

    row = pl.BlockSpec((tr, d), lambda i: (i, 0))
    acc = pl.BlockSpec((8, 128), lambda i: (0, 0))
    return pl.pallas_call(
        body, grid=(n // tr,), in_specs=[row, row], out_specs=[row, acc],
        out_shape=[jax.ShapeDtypeStruct((n, d), F32), jax.ShapeDtypeStruct((8, 128), F32)],
        compiler_params=_params(("arbitrary",)), name="loss_head",
    )(h, target)


def _rope_tables():
    pos = jnp.arange(T, dtype=F32)
    inv = ROPE_THETA ** (-jnp.arange(0, 16, 2, dtype=F32) / 16)
    ang = pos[:, None] * inv[None, :]
    cos, sin = jnp.cos(ang), jnp.sin(ang)
    one = jnp.ones((T, HD - 16), F32)
    zero8 = jnp.zeros((T, 8), F32)
    zero = jnp.zeros((T, HD - 16), F32)
    c = jnp.concatenate([cos, cos, one], axis=1)
    s1 = jnp.concatenate([zero8, sin, zero], axis=1)
    s2 = jnp.concatenate([-sin, zero8, zero], axis=1)
    c, s1, s2 = (jnp.concatenate([t, t], axis=1) for t in (c, s1, s2))
    scale = HD ** -0.5
    return (jnp.stack([c * scale, c, jnp.ones_like(c)]), jnp.stack([s1 * scale, s1, jnp.zeros_like(c)]),
            jnp.stack([s2 * scale, s2, jnp.zeros_like(c)]))


def _row_chunks(r):
    if r == 1:
        n = 4
        return [(slice(i * (T // n), (i + 1) * (T // n)),) * 2 for i in range(n)]
    per = T // r
    return [(pl.ds(j, per, stride=r), slice(j * per, (j + 1) * per)) for j in range(r)]


def _rope_fwd(qkv, tabs):
    def body(x_ref, c_ref, s1_ref, s2_ref, o_ref):
        g = lax.rem(lax.div(pl.program_id(0), 2), 3)
        for gi, r in enumerate(DIL):
            @pl.when(g == gi)
            def _(r=r):
                for tok, prm in _row_chunks(r):
                    x = x_ref[tok, :]
                    y = x * c_ref[tok, :] + pltpu.roll(x, 8, 1) * s1_ref[tok, :] + pltpu.roll(x, 120, 1) * s2_ref[tok, :]
                    o_ref[prm, :] = y.astype(BF16)

    tab = pl.BlockSpec((None, T, 128), lambda b: (lax.div(b, 6), 0, 0))
    return pl.pallas_call(
        body, grid=(18,), in_specs=[pl.BlockSpec((T, 128), lambda b: (0, b)), tab, tab, tab],
        out_specs=pl.BlockSpec((None, T, 128), lambda b: (b, 0, 0)), out_shape=jax.ShapeDtypeStruct((18, T, 128), BF16),
        compiler_params=_params(("parallel",)), name="rope_fwd",
    )(qkv, *tabs)


def _rope_bwd(d, which, tabs, out_buf):
    def body(d_ref, c_ref, s1_ref, s2_ref, *rest):
        o_ref, tok_ref = rest[-2], rest[-1]
        g = lax.div(pl.program_id(0), 2)
        for gi, r in enumerate(DIL):
            @pl.when(g == gi)
            def _(r=r):
                for tok, prm in _row_chunks(r):
                    tok_ref[tok, :] = d_ref[prm, :]
                for rows, _ in _row_chunks(1):
                    gx = tok_ref[rows, :]
                    y = gx * c_ref[rows, :] + pltpu.roll(gx * s1_ref[rows, :], 120, 1) + pltpu.roll(gx * s2_ref[rows, :], 8, 1)
                    o_ref[rows, :] = y.astype(BF16)

    tab = pl.BlockSpec((None, T, 128), lambda b: (which, 0, 0))
    ins = [d, *tabs] + ([] if out_buf is None else [out_buf])
    specs = [pl.BlockSpec((None, None, T, 128), lambda b: (lax.div(b, 2), lax.rem(b, 2), 0, 0)), tab, tab, tab]
    return pl.pallas_call(
        body, grid=(6,), in_specs=specs + ([] if out_buf is None else [ANY]),
        out_specs=pl.BlockSpec((T, 128), lambda b: (0, 6 * which + b)),
        out_shape=jax.ShapeDtypeStruct((T, 3 * A_W), BF16), scratch_shapes=[pltpu.VMEM((T, 128), F32)],
        input_output_aliases={} if out_buf is None else {4: 0},
        compiler_params=_params(("arbitrary",)), name="rope_bwd",
    )(*ins)


def _head_mask(x, lane_lo):
    lane = lax.broadcasted_iota(jnp.int32, x.shape, 1)
    keep = (lane < HD) if lane_lo else (lane >= HD)
    return jnp.where(keep, x.astype(F32), 0.0).astype(BF16)


def _band_scalars():
    g, b = pl.program_id(0), pl.program_id(1)
    nbs = lax.shift_right_logical(jnp.int32(T // BLK), 2 * g)
    has_prev = jnp.where((b & (nbs - 1)) != 0, 1, 0)
    next_ok = jnp.where(((b + 1) & (nbs - 1)) != 0, 1, 0)
    return has_prev, next_ok


def _band_mask_q(has_prev):
    row = lax.broadcasted_iota(jnp.int32, (BLK, 2 * BLK), 0)
    col = lax.broadcasted_iota(jnp.int32, (BLK, 2 * BLK), 1)
    return ((col < BLK) & (col >= row) & (has_prev == 1)) | ((col >= BLK) & (col - BLK <= row))


def _band_mask_k(next_ok):
    row = lax.broadcasted_iota(jnp.int32, (2 * BLK, BLK), 0)
    col = lax.broadcasted_iota(jnp.int32, (2 * BLK, BLK), 1)
    return ((row < BLK) & (col <= row)) | ((row >= BLK) & (col >= row - BLK) & (next_ok == 1))


def _band_spec(base, step):
    nb = T // BLK
    at = {"cur": lambda b: b, "prev": lambda b: jnp.maximum(b - 1, 0), "next": lambda b: jnp.minimum(b + 1, nb - 1)}[step]
    return pl.BlockSpec((None, 2, BLK, 128), lambda g, b: (base + g, 0, at(b), 0))


def _band_fwd(qkv):
    nb = T // BLK

    def body(q_ref, kc_ref, kp_ref, vc_ref, vp_ref, o_ref, l_ref):
        has_prev, _ = _band_scalars()
        mask = _band_mask_q(has_prev)
        lane = lax.broadcasted_iota(jnp.int32, (BLK, 128), 1)
        for p in range(2):
            qp = q_ref[p]
            kcat = jnp.concatenate([kp_ref[p], kc_ref[p]], axis=0)
            vcat = jnp.concatenate([vp_ref[p], vc_ref[p]], axis=0)
            o_acc = jnp.zeros((BLK, 128), F32)
            lse = jnp.zeros((BLK, 128), F32)
            for e in range(2):
                s = _dot(_head_mask(qp, e == 0), kcat, NT)
                s = jnp.where(mask, s, NEG)
                m = jnp.max(s, axis=-1, keepdims=True)
                pr = jnp.exp(s - m)
                l = jnp.sum(pr, axis=-1, keepdims=True)
                o_acc = o_acc + _dot(pr.astype(BF16), _head_mask(vcat, e == 0), NN) / l
                lse = jnp.where((lane < HD) if e == 0 else (lane >= HD), m + jnp.log(l), lse)
            o_ref[p] = o_acc
            l_ref[p] = lse

    out = _band_spec(0, "cur")
    shp = jax.ShapeDtypeStruct((3, 2, T, 128), F32)
    return pl.pallas_call(
        body, grid=(3, nb),
        in_specs=[_band_spec(0, "cur"), _band_spec(3, "cur"), _band_spec(3, "prev"), _band_spec(6, "cur"), _band_spec(6, "prev")],
        out_specs=[out, out], out_shape=[shp, shp],
        compiler_params=_params(("parallel", "parallel")), name="band_fwd",
    )(qkv, qkv, qkv, qkv, qkv)


def _band_bwd(qkv, do, lse, dlt):
    nb = T // BLK

    def body(qc_ref, qn_ref, kc_ref, kp_ref, vc_ref, vp_ref, doc_ref, don_ref, lc_ref, ln_ref, dc_ref, dn_ref,
             dq_ref, dk_ref, dv_ref):
        has_prev, next_ok = _band_scalars()
        mask_q = _band_mask_q(has_prev)
        mask_k = _band_mask_k(next_ok)
        for p in range(2):
            qc, qn = qc_ref[p], qn_ref[p]
            doc, don = doc_ref[p], don_ref[p]
            kc, vc = kc_ref[p], vc_ref[p]
            kcat = jnp.concatenate([kp_ref[p], kc], axis=0)
            vcat = jnp.concatenate([vp_ref[p], vc], axis=0)
            qcat = jnp.concatenate([qc, qn], axis=0)
            docat = jnp.concatenate([doc, don], axis=0)
            dq = jnp.zeros((BLK, 128), F32)
            dk = jnp.zeros((BLK, 128), F32)
            dv = jnp.zeros((BLK, 128), F32)
            for e in range(2):
                lo = e == 0
                col = slice(HD * e, HD * e + 1)
                lse_c, lse_n = lc_ref[p, :, col], ln_ref[p, :, col]
                dl_c, dl_n = dc_ref[p, :, col], dn_ref[p, :, col]
                s = jnp.where(mask_q, _dot(_head_mask(qc, lo), kcat, NT), NEG)
                pr = jnp.exp(s - lse_c)
                dp = _dot(_head_mask(doc, lo), vcat, NT)
                ds = pr * (dp - dl_c)
                dq = dq + _dot(ds.astype(BF16), _head_mask(kcat, lo), NN)
                qm, dom = _head_mask(qcat, lo), _head_mask(docat, lo)
                s2 = jnp.where(mask_k, _dot(qm, kc, NT), NEG)
                p2 = jnp.exp(s2 - jnp.concatenate([lse_c, lse_n], axis=0))
                dv = dv + _dot(p2.astype(BF16), dom, TN)
                dp2 = _dot(dom, vc, NT)
                ds2 = p2 * (dp2 - jnp.concatenate([dl_c, dl_n], axis=0))
                dk = dk + _dot(ds2.astype(BF16), qm, TN)
            dq_ref[p] = dq
            dk_ref[p] = dk
            dv_ref[p] = dv

    cur, nxt = _band_spec(0, "cur"), _band_spec(0, "next")
    shp = jax.ShapeDtypeStruct((3, 2, T, 128), F32)
    return pl.pallas_call(
        body, grid=(3, nb),
        in_specs=[cur, nxt, _band_spec(3, "cur"), _band_spec(3, "prev"), _band_spec(6, "cur"), _band_spec(6, "prev"),
                  cur, nxt, cur, nxt, cur, nxt],
        out_specs=[cur, cur, cur], out_shape=[shp, shp, shp],
        compiler_params=_params(("parallel", "parallel")), name="band_bwd",
    )(qkv, qkv, qkv, qkv, qkv, qkv, do, do, lse, lse, dlt, dlt)


def _split3(x):
    hi = x.astype(BF16)
    r = x - hi.astype(F32)
    mid = r.astype(BF16)
    lo = (r - mid.astype(F32)).astype(BF16)
    return hi, mid, lo


def _dot3(x, m, dims=NN):
    hi, mid, lo = _split3(x)
    return _dot(hi, m, dims) + _dot(mid, m, dims) + _dot(lo, m, dims)


def _combine_weights(lses):
    l0, l1, l2 = lses
    m = jnp.maximum(jnp.maximum(l0, l1), l2)
    e = [jnp.exp(l0 - m), jnp.exp(l1 - m), jnp.exp(l2 - m)]
    inv = 1.0 / (e[0] + e[1] + e[2])
    return [ei * inv for ei in e]


CR = 256


def _combine_fwd(o, lse):
    def body(o_ref, l_ref, att_ref, o3_ref, l3_ref):
        for g, r in enumerate(DIL):
            for p in range(2):
                for tok, prm in _row_chunks(r):
                    o3_ref[g, p, tok, :] = o_ref[g, p, prm, :]
                    l3_ref[g, p, tok, :] = l_ref[g, p, prm, :]
        for i in range(T // CR):
            rows = slice(i * CR, (i + 1) * CR)
            for p in range(2):
                alpha = _combine_weights([l3_ref[g, p, rows, :] for g in range(3)])
                for g in range(3):
                    att_ref[rows, g * GW + p * 128: g * GW + (p + 1) * 128] = (o3_ref[g, p, rows, :] * alpha[g]).astype(BF16)

    shp = jax.ShapeDtypeStruct((3, 2, T, 128), F32)
    return pl.pallas_call(
        body, out_shape=[jax.ShapeDtypeStruct((T, A_W), BF16), shp, shp],
        compiler_params=_params(vmem=VMEM_BIG), name="combine_fwd",
    )(o, lse)


def _combine_bwd(datt, o3, l3, headsum):
    def body(d_ref, o_ref, l_ref, hs_ref, do_ref, dl_ref, tdo_ref, tdl_ref):
        hs = hs_ref[...]
        for p in range(2):
            for i in range(T // CR):
                rows = slice(i * CR, (i + 1) * CR)
                alpha = _combine_weights([l_ref[g, p, rows, :] for g in range(3)])
                total = jnp.zeros((CR, 128), F32)
                for g in range(3):
                    dg = d_ref[rows, g * GW + p * 128: g * GW + (p + 1) * 128]
                    tdo_ref[g, rows, :] = dg * alpha[g]
                    total = total + alpha[g] * _dot3(dg * o_ref[g, p, rows, :], hs)
                for g in range(3):
                    tdl_ref[g, rows, :] = alpha[g] * total
            for g, r in enumerate(DIL):
                for tok, prm in _row_chunks(r):
                    do_ref[g, p, prm, :] = tdo_ref[g, tok, :].astype(BF16)
                    dl_ref[g, p, prm, :] = tdl_ref[g, tok, :]

    return pl.pallas_call(
        body, out_shape=[jax.ShapeDtypeStruct((3, 2, T, 128), BF16), jax.ShapeDtypeStruct((3, 2, T, 128), F32)],
        scratch_shapes=[pltpu.VMEM((3, T, 128), F32), pltpu.VMEM((3, T, 128), F32)],
        compiler_params=_params(vmem=VMEM_BIG), name="combine_bwd",
    )(datt, o3, l3, headsum)


def _fox_scores(qm, k_ref, ck_ref, e, i, n):
    s = _dot(qm, k_ref[0:n, :], NT) - ck_ref[0, e:e + 1, 0:n]
    row = lax.broadcasted_iota(jnp.int32, (FQ, FQ), 0)
    col = lax.broadcasted_iota(jnp.int32, (FQ, FQ), 1)
    diag = jnp.where(col <= row, s[:, n - FQ:], NEG)
    m = jnp.max(diag, axis=-1, keepdims=True)
    if i == 0:
        pr = jnp.exp(diag - m)
        return pr, jnp.sum(pr, axis=-1, keepdims=True)
    past = s[:, :n - FQ]
    m = jnp.maximum(m, jnp.max(past, axis=-1, keepdims=True))
    p_past, p_diag = jnp.exp(past - m), jnp.exp(diag - m)
    l = jnp.sum(p_past, axis=-1, keepdims=True) + jnp.sum(p_diag, axis=-1, keepdims=True)
    return jnp.concatenate([p_past, p_diag], axis=1), l


def _fox_fwd(q, kv, c_row):
    def body(q_ref, k_ref, v_ref, cr_ref, o_ref, vm_ref):
        for e in range(2):
            vm_ref[e] = _head_mask(v_ref[...], e == 0)
        for i in range(T // FQ):
            n = (i + 1) * FQ
            rows = slice(i * FQ, n)
            acc = jnp.zeros((FQ, 128), F32)
            for e in range(2):
                qm = _head_mask(q_ref[rows, :], e == 0)
                pr, l = _fox_scores(qm, k_ref, cr_ref, e, i, n)
                acc = acc + _dot(pr.astype(BF16), vm_ref[e, 0:n, :], NN) / l
            o_ref[rows, :] = acc.astype(BF16)

    pair = pl.BlockSpec((T, 128), lambda p: (0, p))
    return pl.pallas_call(
        body, grid=(D // 128,),
        in_specs=[pair, pair, pl.BlockSpec((T, 128), lambda p: (0, D // 128 + p)), pl.BlockSpec((1, 2, T), lambda p: (p, 0, 0))],
        out_specs=pair, out_shape=jax.ShapeDtypeStruct((T, D), BF16),
        scratch_shapes=[pltpu.VMEM((2, T, 128), BF16)],
        compiler_params=_params(("parallel",), VMEM_BIG), name="fox_fwd",
    )(q, kv, kv, c_row)


def _fox_bwd(q, kv, do, c_row, init):
    def body(q_ref, k_ref, v_ref, do_ref, cr_ref, ik_ref, iv_ref, ic_ref, dq_ref, dk_ref, dv_ref, dck_ref, km_ref):
        dk_ref[...] = ik_ref[...]
        dv_ref[...] = iv_ref[...]
        dck_ref[...] = ic_ref[...]
        for e in range(2):
            km_ref[e] = _head_mask(k_ref[...], e == 0)
        for i in range(T // FQ):
            n = (i + 1) * FQ
            rows = slice(i * FQ, n)
            dq = jnp.zeros((FQ, 128), F32)
            dk = jnp.zeros((n, 128), F32)
            dv = jnp.zeros((n, 128), F32)
            for e in range(2):
                qm = _head_mask(q_ref[rows, :], e == 0)
                dom = _head_mask(do_ref[rows, :], e == 0)
                pr, l = _fox_scores(qm, k_ref, cr_ref, e, i, n)
                pr = pr * (1.0 / l)
                dp = _dot(dom, v_ref[0:n, :], NT)
                ds = pr * (dp - jnp.sum(pr * dp, axis=-1, keepdims=True))
                dsb = ds.astype(BF16)
                dq = dq + _dot(dsb, km_ref[e, 0:n, :], NN)
                dk = dk + _dot(dsb, qm, TN)
                dv = dv + _dot(pr.astype(BF16), dom, TN)
                dck_ref[0, e:e + 1, 0:n] += jnp.sum(ds, axis=0, keepdims=True)
            dk_ref[0:n, :] += dk
            dv_ref[0:n, :] += dv
            dq_ref[rows, :] = (dq * HD ** -0.5).astype(BF16)

    pair = pl.BlockSpec((T, 128), lambda p: (0, p))
    ck = pl.BlockSpec((1, 8, T), lambda p: (p, 0, 0))
    return pl.pallas_call(
        body, grid=(D // 128,),
        in_specs=[pair, pair, pl.BlockSpec((T, 128), lambda p: (0, D // 128 + p)), pair,
                  pl.BlockSpec((1, 2, T), lambda p: (p, 0, 0)), pair, pair, ck],
        out_specs=[pair, pair, pair, ck],
        out_shape=[jax.ShapeDtypeStruct((T, D), BF16), jax.ShapeDtypeStruct((T, D), F32), jax.ShapeDtypeStruct((T, D), F32),
                   jax.ShapeDtypeStruct((D // 128, 8, T), F32)],
        scratch_shapes=[pltpu.VMEM((2, T, 128), BF16)],
        compiler_params=_params(("parallel",), VMEM_BIG), name="fox_bwd",
    )(q, kv, kv, do, c_row, *init)


def _tri(lower):
    r = lax.broadcasted_iota(jnp.int32, (BLK, BLK), 0)
    c = lax.broadcasted_iota(jnp.int32, (BLK, BLK), 1)
    return jnp.where((c <= r) if lower else (c >= r), 1.0, 0.0).astype(BF16)


def _gates_fwd(z, b):
    def body(z_ref, b_ref, c_ref):
        tri = _tri(True)
        carry = jnp.zeros((1, 128), F32)
        for i in range(T // BLK):
            rows = slice(i * BLK, (i + 1) * BLK)
            x = z_ref[rows, :] + b_ref[...]
            logf = jnp.minimum(x, 0.0) - jnp.log(1.0 + jnp.exp(-jnp.abs(x)))
            hi, mid, lo = _split3(logf)
            y = _dot(tri, hi, NN) + _dot(tri, mid, NN) + _dot(tri, lo, NN) + carry
            c_ref[rows, :] = y
            carry = y[BLK - 1:BLK, :]

    return pl.pallas_call(body, out_shape=jax.ShapeDtypeStruct((T, 128), F32), name="gates_fwd")(z, b)


def _gates_bwd(dc, z, b):
    def body(dc_ref, z_ref, b_ref, dz_ref, db_ref):
        tri = _tri(False)
        carry = jnp.zeros((1, 128), F32)
        db = jnp.zeros((1, 128), F32)
        for i in reversed(range(T // BLK)):
            rows = slice(i * BLK, (i + 1) * BLK)
            hi, mid, lo = _split3(dc_ref[rows, :])
            dlogf = _dot(tri, hi, NN) + _dot(tri, mid, NN) + _dot(tri, lo, NN) + carry
            carry = dlogf[0:1, :]
            x = z_ref[rows, :] + b_ref[...]
            dz = dlogf / (1.0 + jnp.exp(x))
            dz_ref[rows, :] = dz.astype(BF16)
            db = db + jnp.sum(dz, axis=0, keepdims=True)
        db_ref[...] = db

    return pl.pallas_call(
        body, out_shape=[jax.ShapeDtypeStruct((T, 128), BF16), jax.ShapeDtypeStruct((1, 128), F32)], name="gates_bwd",
    )(dc, z, b)


def _conv_pair(a_refs, cw_refs, cb_refs):
    row = lax.broadcasted_iota(jnp.int32, (T, CT), 0)
    outs = []
    for a_ref, cw_ref, cb_ref in zip(a_refs, cw_refs, cb_refs):
        z = a_ref[...]
        z1 = jnp.where(row >= 1, pltpu.roll(z, 1, 0), 0.0)
        z2 = jnp.where(row >= 2, pltpu.roll(z, 2, 0), 0.0)
        y = cw_ref[2:3, :] * z + cw_ref[1:2, :] * z1 + cw_ref[0:1, :] * z2 + cb_ref[...]
        outs.append((y, z, z1, z2))
    return outs


_GELU_K = math.sqrt(2.0 / math.pi)
N_CT = D_FF // CT


def _conv_specs():
    def at(rows, off):
        return pl.BlockSpec((rows, CT), lambda j: (0, j + off))
    return [at(T, 0), at(T, N_CT), at(3, 0), at(3, N_CT), at(1, 0), at(1, N_CT)]


def _convgate_fwd(a, cw, cb):
    def body(ag_ref, av_ref, wg_ref, wv_ref, bg_ref, bv_ref, u_ref):
        (g, _, _, _), (v, _, _, _) = _conv_pair((ag_ref, av_ref), (wg_ref, wv_ref), (bg_ref, bv_ref))
        th = jnp.tanh(_GELU_K * (g + 0.044715 * g * g * g))
        u_ref[...] = (0.5 * g * (1.0 + th) * v).astype(BF16)

    return pl.pallas_call(
        body, grid=(N_CT,), in_specs=_conv_specs(),
        out_specs=pl.BlockSpec((T, CT), lambda j: (0, j)), out_shape=jax.ShapeDtypeStruct((T, D_FF), BF16),
        compiler_params=_params(("parallel",), VMEM_BIG), name="convgate_fwd",
    )(a, a, cw, cw, cb, cb)


def _convgate_bwd(a, du, cw, cb):
    def body(ag_ref, av_ref, wg_ref, wv_ref, bg_ref, bv_ref, du_ref, da_ref, dcw_ref, dcb_ref):
        (g, gz, gz1, gz2), (v, vz, vz1, vz2) = _conv_pair((ag_ref, av_ref), (wg_ref, wv_ref), (bg_ref, bv_ref))
        du = du_ref[...].astype(F32)
        th = jnp.tanh(_GELU_K * (g + 0.044715 * g * g * g))
        gelu = 0.5 * g * (1.0 + th)
        dgelu = 0.5 * (1.0 + th) + 0.5 * g * (1.0 - th * th) * _GELU_K * (1.0 + 3 * 0.044715 * g * g)
        row = lax.broadcasted_iota(jnp.int32, (T, CT), 0)
        for h, (d, z, z1, z2, w_ref) in enumerate(((du * v * dgelu, gz, gz1, gz2, wg_ref), (du * gelu, vz, vz1, vz2, wv_ref))):
            d1 = jnp.where(row < T - 1, pltpu.roll(d, T - 1, 0), 0.0)
            d2 = jnp.where(row < T - 2, pltpu.roll(d, T - 2, 0), 0.0)
            da_ref[h] = (w_ref[2:3, :] * d + w_ref[1:2, :] * d1 + w_ref[0:1, :] * d2).astype(BF16)
            dcw_ref[h, 0:1, :] = jnp.sum(d * z2, axis=0, keepdims=True)
            dcw_ref[h, 1:2, :] = jnp.sum(d * z1, axis=0, keepdims=True)
            dcw_ref[h, 2:3, :] = jnp.sum(d * z, axis=0, keepdims=True)
            dcb_ref[h] = jnp.sum(d, axis=0, keepdims=True)

    def both(rows):
        return pl.BlockSpec((2, rows, CT), lambda j: (0, 0, j))

    return pl.pallas_call(
        body, grid=(N_CT,),
        in_specs=_conv_specs() + [pl.BlockSpec((T, CT), lambda j: (0, j))],
        out_specs=[both(T), both(3), both(1)],
        out_shape=[jax.ShapeDtypeStruct((2, T, D_FF), BF16), jax.ShapeDtypeStruct((2, 3, D_FF), F32),
                   jax.ShapeDtypeStruct((2, 1, D_FF), F32)],
        compiler_params=_params(("parallel",), VMEM_BIG), name="convgate_bwd",
    )(a, a, cw, cw, cb, cb, du)


def _halves_a(tm, tn, tk):
    per = D_FF // tk
    return lambda i, j, k: (lax.div(k, per), i, lax.rem(k, per))


def _halves_b(tm, tn, tk):
    per = D_FF // tn
    return lambda i, j, k: (lax.div(j, per), k, lax.rem(j, per))


def _adamw(w, m, v, g, *, name):
    r, c = w.shape
    tr = r
    if r * c > 256 * 1024:
        for cand in range(8, r, 8):
            if r % cand == 0 and cand * c <= 256 * 1024:
                tr = cand

    def body(w_ref, m_ref, v_ref, g_ref, d_ref, nm_ref, nv_ref):
        gv = g_ref[...]
        mn = ADAM_B1 * m_ref[...] + (1.0 - ADAM_B1) * gv
        vn = ADAM_B2 * v_ref[...] + (1.0 - ADAM_B2) * (gv * gv)
        m_hat = mn * (1.0 / (1.0 - ADAM_B1 ** ADAM_STEP))
        v_hat = vn * (1.0 / (1.0 - ADAM_B2 ** ADAM_STEP))
        d_ref[...] = -ADAM_LR * (m_hat / (jnp.sqrt(v_hat) + ADAM_EPS) + ADAM_WD * w_ref[...])
        nm_ref[...] = mn
        nv_ref[...] = vn

    blk = pl.BlockSpec((tr, c), lambda i: (i, 0))
    shp = jax.ShapeDtypeStruct((r, c), F32)
    return pl.pallas_call(
        body, grid=(r // tr,), in_specs=[blk] * 4, out_specs=[blk] * 3, out_shape=[shp] * 3,
        compiler_params=_params(("parallel",)), name=name,
    )(w, m, v, g)


def _place():
    x, y, c = lax.axis_index("x"), lax.axis_index("y"), lax.axis_index("c")
    chips = [(1 - x, y), (x, 1 - y), (1 - x, 1 - y)]
    return x, y, c, chips


def _window(ref, kind, s, half=None):
    lead = () if half is None else (half,)
    b, c = ref.shape[-2], ref.shape[-1]
    if kind == "col":
        return ref.at[lead + (slice(None), slice(None), pl.ds(s * (c // N_CHIPS), c // N_CHIPS))]
    if kind == "row":
        return ref.at[lead + (slice(None), pl.ds(s * (b // N_CHIPS), b // N_CHIPS), slice(None))]
    return ref.at[lead + (s,)]


def _allgather(tensors, kinds, *, name):
    n = len(tensors)

    def body(*refs):
        bufs = refs[n:2 * n]
        send, recv = refs[2 * n:]
        x, y, c, chips = _place()
        me = 2 * x + y
        sib = (x, y, 1 - c)

        def rcopy(i, k, win, to):
            return pltpu.make_async_remote_copy(src_ref=win, dst_ref=win, send_sem=send.at[i * 6 + k], recv_sem=recv.at[i * 6 + k],
                                                device_id=to, device_id_type=MESH)

        started = []
        for i in range(n):
            for k, (px, py) in enumerate(chips):
                cp = rcopy(i, k, _window(bufs[i], kinds[i], me, c), (px, py, c))
                cp.start()
                started.append(cp)
        for i in range(n):
            for k, (px, py) in enumerate(chips):
                landed = _window(bufs[i], kinds[i], 2 * px + py, c)
                rcopy(i, k, landed, (px, py, c)).wait_recv()
                fw = rcopy(i, 3 + k, landed, sib)
                fw.start()
                started.append(fw)
        for i in range(n):
            for k, (px, py) in enumerate(chips):
                rcopy(i, 3 + k, _window(bufs[i], kinds[i], 2 * px + py, 1 - c), sib).wait_recv()
        for cp in started:
            cp.wait_send()

    return pl.pallas_call(
        body, in_specs=[ANY] * n, out_specs=[ANY] * n,
        out_shape=[jax.ShapeDtypeStruct(t.shape, t.dtype) for t in tensors],
        scratch_shapes=[pltpu.SemaphoreType.DMA((6 * n,)), pltpu.SemaphoreType.DMA((6 * n,))],
        input_output_aliases={i: i for i in range(n)},
        name=name,
    )(*tensors)


def _rows_tile(rows, cols, sub):
    best = None
    for t in range(sub, rows + 1, sub):
        if rows % t == 0 and t * cols <= 512 * 1024:
            best = t
    return rows if best is None else best


def _sequencer(name, cid, n_sems, peers_of, body):
    @pl.kernel(mesh=plsc.ScalarSubcoreMesh(axis_name="seq", num_cores=1), name=name,
               scratch_types=(pltpu.SemaphoreType.DMA((n_sems,)), pltpu.SemaphoreType.DMA((n_sems,))),
               compiler_params=pltpu.CompilerParams(collective_id=cid))
    def launch(send, recv):
        x, y, c, chips = _place()
        peers = peers_of(x, y, c, chips)
        barrier = pltpu.get_barrier_semaphore()
        for peer in peers:
            pl.semaphore_signal(barrier, inc=1, device_id=peer, device_id_type=MESH)
        pl.semaphore_wait(barrier, len(peers))
        body(send, recv)

    launch()


def _half_of_full(ref, kind, h):
    if kind == "col":
        b = ref.shape[0]
        return ref.at[pl.ds(h * (b // 2), b // 2), :]
    if kind == "row":
        c = ref.shape[1]
        return ref.at[:, pl.ds(h * (c // 2), c // 2)]
    b = ref.shape[1]
    return ref.at[:, pl.ds(h * (b // 2), b // 2), :]


def _half_shape(full, kind):
    if kind == "col":
        return (full[0] // 2, full[1])
    if kind == "row":
        return (full[0], full[1] // 2)
    return (full[0], full[1] // 2, full[2])


def _win_of_half(ref, kind, s):
    if kind == "col":
        c = ref.shape[1]
        return ref.at[:, pl.ds(s * (c // N_CHIPS), c // N_CHIPS)]
    if kind == "row":
        b = ref.shape[0]
        return ref.at[pl.ds(s * (b // N_CHIPS), b // N_CHIPS), :]
    return ref.at[s]


def _win_shape(half, kind):
    if kind == "col":
        return (half[0], half[1] // N_CHIPS)
    if kind == "row":
        return (half[0] // N_CHIPS, half[1])
    return half[1:]


def _seq_swap(parts, kinds, *, name):
    n = len(parts)
    srcs = [jax.new_ref(p, memory_space=pltpu.MemorySpace.HBM) for p in parts]
    outs = [jax.empty_ref(jax.ShapeDtypeStruct(_half_shape(p.shape, k), p.dtype), memory_space=pltpu.MemorySpace.HBM)
            for p, k in zip(parts, kinds)]

    def body(send, recv):
        x, y, c, _ = _place()
        cps = []
        for i in range(n):
            cp = pltpu.make_async_remote_copy(src_ref=_half_of_full(srcs[i], kinds[i], 1 - c), dst_ref=outs[i], send_sem=send.at[i],
                                              recv_sem=recv.at[i], device_id=(x, y, 1 - c), device_id_type=MESH)
            cp.start()
            cps.append(cp)
        for cp in cps:
            cp.wait()

    _sequencer(name, 2, n, lambda x, y, c, chips: [(x, y, 1 - c)], body)
    return [o[...] for o in outs]


def _seq_scatter(halves, kinds, *, name):
    n = len(halves)
    srcs = [jax.new_ref(h, memory_space=pltpu.MemorySpace.HBM) for h in halves]
    outs = [jax.empty_ref(jax.ShapeDtypeStruct((3,) + _win_shape(h.shape, k), h.dtype), memory_space=pltpu.MemorySpace.HBM)
            for h, k in zip(halves, kinds)]

    def body(send, recv):
        x, y, c, chips = _place()
        cps = []
        for i in range(n):
            for k, (px, py) in enumerate(chips):
                cp = pltpu.make_async_remote_copy(src_ref=_win_of_half(srcs[i], kinds[i], 2 * px + py), dst_ref=outs[i].at[k],
                                                  send_sem=send.at[3 * i + k], recv_sem=recv.at[3 * i + k],
                                                  device_id=(px, py, c), device_id_type=MESH)
                cp.start()
                cps.append(cp)
        for cp in cps:
            cp.wait()

    _sequencer(name, 3, 3 * n, lambda x, y, c, chips: [(px, py, c) for px, py in chips], body)
    return [o[...] for o in outs]


def _add_half(g, p, kind, where, after, *, name):
    if kind == "slab":
        s, b2, c = p.shape
        tr = _rows_tile(b2, c, 16)
        nr = b2 // tr
        grid = (s, nr)
        g_spec = pl.BlockSpec((None, tr, c), lambda i, r, w: (i, w[1] * nr + r, 0))
        p_spec = pl.BlockSpec((None, tr, c), lambda i, r, w: (i, r, 0))
    elif kind == "col":
        b2, c = p.shape
        tr = _rows_tile(b2, c, 16)
        nr = b2 // tr
        grid = (1, nr)
        g_spec = pl.BlockSpec((tr, c), lambda i, r, w: (w[1] * nr + r, 0))
        p_spec = pl.BlockSpec((tr, c), lambda i, r, w: (r, 0))
    else:
        b, c2 = p.shape
        tr = _rows_tile(b, c2, 16)
        grid = (1, b // tr)
        g_spec = pl.BlockSpec((tr, c2), lambda i, r, w: (r, w[1]))
        p_spec = pl.BlockSpec((tr, c2), lambda i, r, w: (r, 0))

    def body(w_ref, g_ref, p_ref, *rest):
        o_ref = rest[-1]
        o_ref[...] = (g_ref[...].astype(F32) + p_ref[...].astype(F32)).astype(o_ref.dtype)

    extra = [] if after is None else [after]
    return pl.pallas_call(
        body,
        grid_spec=pltpu.PrefetchScalarGridSpec(num_scalar_prefetch=1, grid=grid, in_specs=[g_spec, p_spec] + [ANY] * len(extra),
                                               out_specs=p_spec),
        out_shape=jax.ShapeDtypeStruct(p.shape, g.dtype),
        compiler_params=_params(("parallel", "parallel")), name=name,
    )(where, g, p, *extra)


def _sum_chips(r, h, kind, where, layer, layers, out_buf, after, *, name):
    _, br, cr = r.shape
    tr = _rows_tile(br, cr, 16)
    nr = br // tr
    if kind == "col":
        h_spec = pl.BlockSpec((tr, cr), lambda j, w: (j, w[0]))
        o_shape, o_spec = (layers, 2 * br, cr), pl.BlockSpec((None, tr, cr), lambda j, w: (layer, w[1] * nr + j, 0))
    elif kind == "row":
        h_spec = pl.BlockSpec((tr, cr), lambda j, w: (w[0] * nr + j, 0))
        o_shape, o_spec = (layers, br, 2 * cr), pl.BlockSpec((None, tr, cr), lambda j, w: (layer, j, w[1]))
    else:
        h_spec = pl.BlockSpec((None, tr, cr), lambda j, w: (w[0], j, 0))
        o_shape, o_spec = (layers, 2 * br, cr), pl.BlockSpec((None, tr, cr), lambda j, w: (layer, w[1] * nr + j, 0))

    def body(w_ref, h_ref, r0_ref, r1_ref, r2_ref, *rest):
        o_ref, t_ref = rest[-2], rest[-1]
        o_ref[...] = ((h_ref[...].astype(F32) + r0_ref[...].astype(F32)) + r1_ref[...].astype(F32)) + r2_ref[...].astype(F32)
        t_ref[...] = jnp.zeros_like(t_ref)

    def slot(k):
        return pl.BlockSpec((None, tr, cr), lambda j, w: (k, j, 0))

    ins, specs, alias = [h, r, r, r], [h_spec, slot(0), slot(1), slot(2)], {}
    if after is not None:
        ins.append(after)
        specs.append(ANY)
    if out_buf is not None:
        alias = {1 + len(ins): 0}
        ins.append(out_buf)
        specs.append(ANY)
    return pl.pallas_call(
        body,
        grid_spec=pltpu.PrefetchScalarGridSpec(num_scalar_prefetch=1, grid=(nr,), in_specs=specs,
                                               out_specs=[o_spec, pl.BlockSpec((8, 128), lambda j, w: (0, 0))]),
        out_shape=[jax.ShapeDtypeStruct(o_shape, F32), jax.ShapeDtypeStruct((8, 128), F32)], input_output_aliases=alias,
        compiler_params=_params(("arbitrary",)), name=name,
    )(where, *ins)


def _join_halves(tensors, kinds, *, name):
    n = len(tensors)

    def mine(ref, kind, h):
        if kind == "row":
            c = ref.shape[2]
            return ref.at[:, :, pl.ds(h * (c // 2), c // 2)]
        b = ref.shape[1]
        return ref.at[:, pl.ds(h * (b // 2), b // 2), :]

    def body(*refs):
        bufs = refs[n:2 * n]
        send, recv = refs[2 * n:]
        x, y, c, _ = _place()
        cps = []
        for i in range(n):
            part = mine(bufs[i], kinds[i], c)
            cp = pltpu.make_async_remote_copy(src_ref=part, dst_ref=part, send_sem=send.at[i],
                                              recv_sem=recv.at[i], device_id=(x, y, 1 - c), device_id_type=MESH)
            cp.start()
            cps.append(cp)
        for i in range(n):
            other = mine(bufs[i], kinds[i], 1 - c)
            pltpu.make_async_remote_copy(src_ref=other, dst_ref=other, send_sem=send.at[i],
                                         recv_sem=recv.at[i], device_id=(x, y, 1 - c), device_id_type=MESH).wait_recv()
        for cp in cps:
            cp.wait_send()

    return pl.pallas_call(
        body, in_specs=[ANY] * n, out_specs=[ANY] * n,
        out_shape=[jax.ShapeDtypeStruct(t.shape, t.dtype) for t in tensors],
        scratch_shapes=[pltpu.SemaphoreType.DMA((n,)), pltpu.SemaphoreType.DMA((n,))],
        input_output_aliases={i: i for i in range(n)},
        name=name,
    )(*tensors)


def _win(ref, kind, s, h=None):
    if kind == "col":
        b, c = ref.shape
        cols = pl.ds(s * (c // N_CHIPS), c // N_CHIPS)
        return ref.at[:, cols] if h is None else ref.at[pl.ds(h * (b // 2), b // 2), cols]
    if kind == "row":
        b, c = ref.shape
        rows = pl.ds(s * (b // N_CHIPS), b // N_CHIPS)
        return ref.at[rows, :] if h is None else ref.at[rows, pl.ds(h * (c // 2), c // 2)]
    b = ref.shape[1]
    return ref.at[s] if h is None else ref.at[s, pl.ds(h * (b // 2), b // 2)]


def _half(ref, kind, h):
    b, c = ref.shape
    if kind == "row":
        return ref.at[:, pl.ds(h * (c // 2), c // 2)]
    return ref.at[pl.ds(h * (b // 2), b // 2), :]


def _full_shape(shard_shape, kind):
    b, c = shard_shape
    return {"col": (b, N_CHIPS * c), "row": (N_CHIPS * b, c), "slab": (N_CHIPS, b, c)}[kind]


def _gather_body(srcs, outs, kinds, send, recv):
    x, y, c, chips = _place()
    me = 2 * x + y
    sib = (x, y, 1 - c)

    def rcopy(i, k, src, dst, to):
        return pltpu.make_async_remote_copy(src_ref=src, dst_ref=dst, send_sem=send.at[7 * i + k], recv_sem=recv.at[7 * i + k],
                                            device_id=to, device_id_type=MESH)

    started = []
    for i, (src, out, kind) in enumerate(zip(srcs, outs, kinds)):
        own = rcopy(i, 6, src, _win(out, kind, me), sib)
        own.start()
        started.append(own)
        for k, (px, py) in enumerate(chips):
            cp = rcopy(i, k, _half(src, kind, c), _win(out, kind, me, c), (px, py, c))
            cp.start()
            started.append(cp)
    for i, (out, kind) in enumerate(zip(outs, kinds)):
        for k, (px, py) in enumerate(chips):
            landed = _win(out, kind, 2 * px + py, c)
            rcopy(i, k, landed, landed, (px, py, c)).wait_recv()
            fw = rcopy(i, 3 + k, landed, landed, sib)
            fw.start()
            started.append(fw)
    for i, (src, out, kind) in enumerate(zip(srcs, outs, kinds)):
        for k, (px, py) in enumerate(chips):
            other = _win(out, kind, 2 * px + py, 1 - c)
            rcopy(i, 3 + k, other, other, sib).wait_recv()
        rcopy(i, 6, src, _win(out, kind, me), sib).wait_recv()
    for cp in started:
        cp.wait_send()


def _seq_gather(shards, kinds, *, name, cid):
    n = len(shards)
    srcs = [jax.new_ref(s, memory_space=pltpu.MemorySpace.HBM) for s in shards]
    outs = [jax.empty_ref(jax.ShapeDtypeStruct(_full_shape(s.shape, k), s.dtype), memory_space=pltpu.MemorySpace.HBM)
            for s, k in zip(shards, kinds)]

    @pl.kernel(mesh=plsc.ScalarSubcoreMesh(axis_name="seq", num_cores=1), name=name,
               scratch_types=(pltpu.SemaphoreType.DMA((7 * n,)), pltpu.SemaphoreType.DMA((7 * n,))),
               compiler_params=pltpu.CompilerParams(collective_id=cid))
    def launch(send, recv):
        x, y, c, chips = _place()
        barrier = pltpu.get_barrier_semaphore()
        for px, py in chips:
            pl.semaphore_signal(barrier, inc=1, device_id=(px, py, c), device_id_type=MESH)
        pl.semaphore_signal(barrier, inc=1, device_id=(x, y, 1 - c), device_id_type=MESH)
        pl.semaphore_wait(barrier, 4)
        _gather_body(srcs, outs, kinds, send, recv)

    launch()
    return [o[...] for o in outs]


KIND = dict(w_qkv_a="slab", w_o_a="col", w_q_b="row", w_o_b="row", w_kvf="slab", w_up="col", w_down="row", small="slab")
LAYERS = dict(w_qkv_a=N_A, w_o_a=N_A, w_q_b=DEPTH - N_A, w_o_b=DEPTH - N_A, w_kvf=1, w_up=DEPTH, w_down=DEPTH, small=1)
SMALL_W = 1792
SMALL_ROWS = 8


class _Reducer:
    def __init__(self, where):
        self.where = where
        self.acc = {nm: None for nm in KIND}
        self.pending = None

    def __call__(self, group, tag):
        names, layers, parts = zip(*group)
        kinds = [KIND[nm] for nm in names]
        summed = self._sum_pending(after=parts[-1])
        sib = _seq_swap(list(parts), kinds, name="reduce_swap_" + tag)
        halves = []
        for g, p, k, nm in zip(parts, sib, kinds, names):
            halves.append(_add_half(g, p, k, self.where, halves[-1] if halves else None, name="reduce_add_" + nm))
        landed = _seq_scatter(halves, kinds, name="reduce_scatter_" + tag)
        self.pending = (names, layers, landed, halves, kinds)
        return [halves[-1], summed]

    def flush(self, after):
        return self._sum_pending(after)

    def _sum_pending(self, after):
        if self.pending is None:
            return None
        for nm, l, r, h, k in zip(*self.pending):
            self.acc[nm], after = _sum_chips(r, h, k, self.where, l, LAYERS[nm], self.acc[nm], after, name="reduce_sum_" + nm)
        self.pending = None
        return after

    def finish(self):
        self._sum_pending(after=None)
        names = list(KIND)
        joined = _join_halves([self.acc[nm] for nm in names], [KIND[nm] for nm in names], name="reduce_pair_join")
        return dict(zip(names, joined))


def _headsum_matrix():
    r = lax.broadcasted_iota(jnp.int32, (128, 128), 0) // HD
    c = lax.broadcasted_iota(jnp.int32, (128, 128), 1) // HD
    return jnp.where(r == c, 1.0, 0.0).astype(BF16)


def kernel(x, norm_gains, w_qkv_a, w_o_a, w_q_b, w_o_b, kv_norm, w_kvf, b_f, w_up, conv_w, conv_b, w_down, loss_target, m_norm_gains, m_w_qkv_a, m_w_o_a, m_w_q_b, m_w_o_b, m_kv_norm, m_w_kvf, m_b_f, m_w_up, m_conv_w, m_conv_b, m_w_down, v_norm_gains, v_w_qkv_a, v_w_o_a, v_w_q_b, v_w_o_b, v_kv_norm, v_w_kvf, v_b_f, v_w_up, v_conv_w, v_conv_b, v_w_down):
    xi, yi, ci = lax.axis_index("x"), lax.axis_index("y"), lax.axis_index("c")
    chip = 2 * xi + yi
    where = jnp.stack([chip, ci]).astype(jnp.int32)
    ws = dict(norm_gains=norm_gains, w_qkv_a=w_qkv_a, w_o_a=w_o_a, w_q_b=w_q_b, w_o_b=w_o_b, kv_norm=kv_norm, w_kvf=w_kvf,
              b_f=b_f, w_up=w_up, conv_w=conv_w, conv_b=conv_b, w_down=w_down)
    ms = dict(norm_gains=m_norm_gains, w_qkv_a=m_w_qkv_a, w_o_a=m_w_o_a, w_q_b=m_w_q_b, w_o_b=m_w_o_b, kv_norm=m_kv_norm,
              w_kvf=m_w_kvf, b_f=m_b_f, w_up=m_w_up, conv_w=m_conv_w, conv_b=m_conv_b, w_down=m_w_down)
    vs = dict(norm_gains=v_norm_gains, w_qkv_a=v_w_qkv_a, w_o_a=v_w_o_a, w_q_b=v_w_q_b, w_o_b=v_w_o_b, kv_norm=v_kv_norm,
              w_kvf=v_w_kvf, b_f=v_b_f, w_up=v_w_up, conv_w=v_conv_w, conv_b=v_conv_b, w_down=v_w_down)

    small = jnp.concatenate([
        jnp.pad(norm_gains.reshape(16, 256), ((0, 0), (0, 1408 - 256))),
        jnp.pad(conv_w.reshape(12, 1408), ((0, 4), (0, 0)))], axis=0)
    big = [nm for nm in KIND if nm != "small"]
    half = {nm: ws[nm].astype(BF16) for nm in big}
    W = {nm: [None] * LAYERS[nm] for nm in big if nm != "w_kvf"}
    g_small = None
    groups = [("0a", [("w_qkv_a", 0), ("w_o_a", 0), ("small", 0)]), ("0b", [("w_up", 0)]), ("0c", [("w_down", 0)]),
              ("1", [("w_qkv_a", 1), ("w_o_a", 1), ("w_up", 1), ("w_down", 1)]),
              ("2", [("w_kvf", 0), ("w_q_b", 0), ("w_o_b", 0), ("w_up", 2), ("w_down", 2)]),
              ("3", [("w_q_b", 1), ("w_o_b", 1), ("w_up", 3), ("w_down", 3)])]
    for tag, group in groups:
        shards = [small if nm == "small" else half[nm] if nm == "w_kvf" else half[nm][i] for nm, i in group]
        got = _seq_gather(shards, [KIND[nm] for nm, _ in group], name="gather_layer" + tag, cid=1)
        for (nm, i), g in zip(group, got):
            if nm == "small":
                g_small = g
            elif nm == "w_kvf":
                W[nm] = g.transpose(1, 0, 2).reshape(D, 2 * D + 16)
            else:
                W[nm][i] = g.transpose(1, 0, 2).reshape(D, 3 * A_W) if nm == "w_qkv_a" else g
    gains = g_small[:, :16, :256].transpose(1, 0, 2).reshape(DEPTH, 4, 1, D)
    cw_full = g_small[:, 16:28, :].transpose(1, 0, 2).reshape(DEPTH, 3, 2 * D_FF)
    cb_full = conv_b.reshape(DEPTH, 1, 2 * D_FF)

    reducer = _Reducer(where)
    sq, dh = _fwd_bwd(x[0], loss_target[0], W, gains, cw_full, cb_full, kv_norm, b_f, reducer)
    loss = lax.psum(sq[0, 0] * (0.5 / D), ("x", "y", "c"))
    return _update(loss, dh[None], reducer.finish(), chip, ws, ms, vs)


def _fwd_bwd(h, target, W, gains, cw_full, cb_full, kv_norm, b_f, reduce):
    w_kv = W["w_kvf"][:, :2 * D]
    w_kvf_pad = jnp.pad(W["w_kvf"], ((0, 0), (0, 128 - 16)))
    w_f = w_kvf_pad[:, 2 * D:]
    kvn_g = kv_norm.reshape(1, D)
    bf_pad = jnp.pad(b_f, (0, 128 - 16)).reshape(1, 128)
    tabs = _rope_tables()
    headsum = _headsum_matrix()

    saved = []
    kv = zf = c_row = kvn = h_kv = None
    xn = _rms_fwd(h, gains[0][0], out_dtype=BF16, name="rms_in")
    for l in range(DEPTH):
        s = {"h": h}
        g = gains[l]
        s["xn"] = xn
        if l < N_A:
            qkv = _matmul(xn, W["w_qkv_a"][l], mode="nn", out_dtype=F32, name="mm_qkv", mnk=(T, 3 * A_W, D), tn=768)
            qkvp = _rope_fwd(qkv, tabs).reshape(9, 2, T, 128)
            o_p, lse_p = _band_fwd(qkvp)
            att, o3, lse3 = _combine_fwd(o_p, lse_p)
            s.update(qkvp=qkvp, o3=o3, lse3=lse3, lse_p=lse_p, att=att)
            mix = _matmul(att, W["w_o_a"][l], mode="nn", out_dtype=F32, name="mm_oa", mnk=(T, D, A_W))
        else:
            j = l - N_A
            if l == N_A:
                h_kv = h
                kvn = _rms_fwd(h, kvn_g, out_dtype=BF16, name="rms_in")
                kv = _matmul(kvn, w_kv, mode="nn", out_dtype=BF16, name="mm_kv")
                zf = _matmul(kvn, w_f, mode="nn", out_dtype=F32, name="mm_f")
                cum = _gates_fwd(zf, bf_pad)[:, :16]
                c_row = cum.T.reshape(8, 2, T)
            q = _matmul(xn, W["w_q_b"][j], mode="nn", out_dtype=BF16, name="mm_qb", mnk=(T, D, D), alpha=HD ** -0.5)
            o = _fox_fwd(q, kv, c_row)
            s.update(q=q, o=o)
            mix = _matmul(o, W["w_o_b"][j], mode="nn", out_dtype=F32, name="mm_ob", mnk=(T, D, D))
        s["mix"] = mix
        h1, xn2 = _rms_res_in(mix, g[1], h, g[2], name="rms_res_in")
        a = _matmul(xn2, W["w_up"][l], mode="nn", out_dtype=F32, name="mm_up", mnk=(T, 2 * D_FF, D))
        u = _convgate_fwd(a, cw_full[l], cb_full[l])
        f = _matmul(u, W["w_down"][l], mode="nn", out_dtype=F32, name="mm_down", mnk=(T, D, D_FF), tm=1024, tk=D_FF)
        if l + 1 < DEPTH:
            h, xn = _rms_res_in(f, g[3], h1, gains[l + 1][0], name="rms_res_in")
        else:
            h = _rms_fwd(f, g[3], res=h1, out_dtype=F32, name="rms_res")
        s.update(h1=h1, xn2=xn2, a=a, u=u, f=f)
        saved.append(s)

    dh, sq = _loss_head(h, target)

    d_gains = [[None] * 4 for _ in range(DEPTH)]
    d_cw, d_cb = [None] * DEPTH, [None] * DEPTH
    zeros_td = jnp.zeros((T, D), F32)
    fox_acc = (zeros_td, zeros_td, jnp.zeros((D // 128, 8, T), F32))
    d_kvnorm = d_bf = token = df = None

    def dw(nm, a, b, **kw):
        return _matmul(a, b, mode="tn", out_dtype=BF16, name="mm_dw_" + nm, **kw)

    flush = getattr(reduce, "flush", lambda after: None)

    def slabs(full, width):
        return full.reshape(full.shape[0], N_CHIPS, width).transpose(1, 0, 2)

    for l in reversed(range(DEPTH)):
        s = saved[l]
        g = gains[l]
        if df is None:
            df, d_gains[l][3] = _rms_bwd(dh, s["f"], g[3], out_dtype=BF16, name="rms_bwd")
        du = _matmul(df, W["w_down"][l], mode="nt", out_dtype=F32, name="mm_down_dx", mnk=(T, D_FF, D), tn=256, after=token)
        g_down = dw("w_down", s["u"], df, tm=1408, tn=1024)
        da, d_cw[l], d_cb[l] = _convgate_bwd(s["a"], du, cw_full[l], cb_full[l])
        dxn2 = _matmul(da, W["w_up"][l], mode="nt", out_dtype=F32, name="mm_up_dx", mnk=(T, D, 2 * D_FF), tm=1024, tn=1024, tk=1408,
                       a_map=_halves_a)
        g_up = dw("w_up", s["xn2"], da, mnk=(D, 2 * D_FF, T), tn=1408, b_map=_halves_b)
        token = reduce([("w_down", l, g_down), ("w_up", l, g_up)], "ffn%d" % l)
        dh1, dmix, d_gains[l][2], d_gains[l][1] = _rms_bwd2(dxn2, s["h1"], g[2], dh, s["mix"], g[1], name="rms_bwd2")
        if l < N_A:
            datt = _matmul(dmix, W["w_o_a"][l], mode="nt", out_dtype=F32, name="mm_oa_dx", mnk=(T, A_W, D), tn=768, after=token)
            g_o = dw("w_o_a", s["att"], dmix, tm=768, tn=1024)
            do_p, dlt_p = _combine_bwd(datt, s["o3"], s["lse3"], headsum)
            dqkv = None
            for which, d in enumerate(_band_bwd(s["qkvp"], do_p, s["lse_p"], dlt_p)):
                dqkv = _rope_bwd(d, which, tabs, dqkv)
            dxn = _matmul(dqkv, W["w_qkv_a"][l], mode="nt", out_dtype=F32, name="mm_qkv_dx", mnk=(T, D, 3 * A_W), tm=1024, tn=1024, tk=3 * A_W,
                          after=[flush(dqkv)])
            g_qkv = dw("w_qkv_a", s["xn"], dqkv, tn=768)
            group = [("w_o_a", l, g_o), ("w_qkv_a", l, slabs(g_qkv, 576))]
        else:
            j = l - N_A
            do = _matmul(dmix, W["w_o_b"][j], mode="nt", out_dtype=BF16, name="mm_ob_dx", mnk=(T, D, D), after=token)
            g_o = dw("w_o_b", s["o"], dmix, tn=1024)
            dq, *fox_acc = _fox_bwd(s["q"], kv, do, c_row, fox_acc)
            dxn = _matmul(dq, W["w_q_b"][j], mode="nt", out_dtype=F32, name="mm_qb_dx", mnk=(T, D, D), after=[flush(dq)])
            g_q = dw("w_q_b", s["xn"], dq, tn=1024)
            group = [("w_o_b", j, g_o), ("w_q_b", j, g_q)]
        if l > 0 and l != N_A:
            dh, df, d_gains[l][0], d_gains[l - 1][3] = _rms_bwd2(dxn, s["h"], g[0], dh1, saved[l - 1]["f"], gains[l - 1][3],
                                                                 name="rms_bwd2")
        else:
            dh, d_gains[l][0] = _rms_bwd(dxn, s["h"], g[0], dres=dh1, out_dtype=F32, name="rms_bwd_res")
            df = None
        if l == N_A:
            dk, dv, dck = fox_acc
            dc16 = -dck[:, :2, :].reshape(16, T).T
            dzf, d_bf = _gates_bwd(jnp.pad(dc16, ((0, 0), (0, 128 - 16))), zf, bf_pad)
            dkvf = jnp.concatenate([dk.astype(BF16), dv.astype(BF16), dzf], axis=1)
            g_kvf = _matmul(kvn, dkvf, mode="tn", out_dtype=BF16, name="mm_kvf_dw", tm=512, tn=2 * D + 128)[:, :2 * D + 16]
            dkvn = _matmul(dkvf, w_kvf_pad, mode="nt", out_dtype=F32, name="mm_kvf_dx", tm=1024, tn=1024, tk=2 * D + 128)
            dh, d_kvnorm = _rms_bwd(dkvn, h_kv, kvn_g, dres=dh, out_dtype=F32, name="rms_bwd_res")
            group.append(("w_kvf", 0, slabs(g_kvf, 516)))
        token = reduce(group, "mix%d" % l)
    small_flat = jnp.concatenate([
        jnp.stack([jnp.stack(r) for r in d_gains]).reshape(-1),
        jnp.stack(d_cw).transpose(0, 2, 1, 3).reshape(-1),
        jnp.stack(d_cb).reshape(-1),
        d_kvnorm.reshape(-1), d_bf[0, :16]])
    small = jnp.pad(small_flat, (0, 2 * N_CHIPS * SMALL_ROWS * SMALL_W - small_flat.shape[0]))
    reduce([("small", 0, small.reshape(N_CHIPS, 2 * SMALL_ROWS, SMALL_W))], "small")
    return sq, dh


def _update(loss, grad_x, reduced, chip, ws, ms, vs):
    red_s = reduced.pop("small")
    buf_s = lax.dynamic_update_slice(jnp.zeros((2, N_CHIPS, SMALL_ROWS, SMALL_W), F32), red_s.reshape(2, 1, SMALL_ROWS, SMALL_W),
                                     (0, chip, 0, 0))
    (all_s,) = _allgather([buf_s], ["slab"], name="gather_small_grads")
    sflat = all_s.transpose(1, 0, 2, 3).reshape(-1)

    grads = {nm: r.reshape(ws[nm].shape) for nm, r in reduced.items()}
    o = 0
    g_gains_full = sflat[o:o + 16 * D].reshape(DEPTH, 4, D); o += 16 * D
    g_cw_full = sflat[o:o + 12 * 2 * D_FF].reshape(DEPTH, 3, 2 * D_FF); o += 12 * 2 * D_FF
    grads["conv_b"] = sflat[o:o + 4 * 2 * D_FF].reshape(DEPTH, 2 * D_FF); o += 4 * 2 * D_FF
    grads["kv_norm"] = sflat[o:o + D]; o += D
    grads["b_f"] = sflat[o:o + 16]
    grads["norm_gains"] = lax.dynamic_slice_in_dim(g_gains_full, chip * 256, 256, axis=2)
    grads["conv_w"] = lax.dynamic_slice_in_dim(g_cw_full, chip * 1408, 1408, axis=2)

    names = ["norm_gains", "w_qkv_a", "w_o_a", "w_q_b", "w_o_b", "kv_norm", "w_kvf", "b_f", "w_up", "conv_w", "conv_b", "w_down"]
    deltas, new_m, new_v = {}, {}, {}
    for nm in names:
        shp = ws[nm].shape
        two = (math.prod(shp[:-1]), shp[-1]) if len(shp) > 1 else (1, shp[0])
        d, m2, v2 = _adamw(ws[nm].reshape(two), ms[nm].reshape(two), vs[nm].reshape(two), grads[nm].reshape(two),
                           name="adamw_" + nm)
        deltas[nm], new_m[nm], new_v[nm] = d.reshape(shp), m2.reshape(shp), v2.reshape(shp)

    return (loss, grad_x, *[grads[nm] for nm in names], *[deltas[nm] for nm in names],
            *[new_m[nm] for nm in names], *[new_v[nm] for nm in names])
```

```python
import math

import jax
import jax.numpy as jnp
from jax import lax
from jax.experimental import pallas as pl
from jax.experimental.pallas import tpu as pltpu
from jax.experimental.pallas import tpu_sc as plsc

F32 = jnp.float32
BF16 = jnp.bfloat16
MESH = pl.DeviceIdType.MESH
ANY = pl.BlockSpec(memory_space=pl.ANY)

T = 2048
D = 1024
HD = 64
DEPTH = 4
N_A = 2
A_W = 768
GW = 256
DIL = (1, 4, 16)
BLK = 128
D_FF = 2816
ROPE_THETA = 500000.0
EPS = 1e-6
NEG = -1e30
N_CHIPS = 4
FQ = 256
CT = 128
VMEM_BIG = 48 * 1024 * 1024

ADAM_LR, ADAM_B1, ADAM_B2, ADAM_EPS, ADAM_WD, ADAM_STEP = 0.001, 0.9, 0.999, 1e-08, 0.01, 10

NN = (((1,), (0,)), ((), ()))
NT = (((1,), (1,)), ((), ()))
TN = (((0,), (0,)), ((), ()))


def _dot(a, b, dims):
    return lax.dot_general(a, b, dims, preferred_element_type=F32)


def _pick(dim, pref):
    if dim <= pref:
        return dim
    best = None
    for t in range(128, pref + 1, 128):
        if dim % t == 0:
            best = t
    assert best is not None, (dim, pref)
    return best


def _params(sem=None, vmem=None):
    kw = {}
    if sem is not None:
        kw["dimension_semantics"] = sem
    if vmem is not None:
        kw["vmem_limit_bytes"] = vmem
    return pltpu.CompilerParams(**kw)


def _matmul(a, b, *, mode, out_dtype, name, mnk=None, alpha=None, tm=2048, tn=512, tk=2048, a_map=None, b_map=None, after=None):
    if mnk is not None:
        M, N, K = mnk
    elif mode == "nn":
        (M, K), (_, N) = a.shape, b.shape
    elif mode == "nt":
        (M, K), (N, _) = a.shape, b.shape
    else:
        (K, M), (_, N) = a.shape, b.shape
    tm, tn, tk = _pick(M, tm), _pick(N, tn), _pick(K, tk)
    nk = K // tk
    dims = {"nn": NN, "nt": NT, "tn": TN}[mode]
    after = [t for t in (after or ()) if t is not None]
    n_in = 2 + len(after)

    def body(*refs):
        a_ref, b_ref = refs[0], refs[1]
        o_ref = refs[n_in]
        k = pl.program_id(2)

        def finish(r):
            if alpha is not None:
                r = r * alpha
            o_ref[...] = r.astype(out_dtype)

        def product():
            return _dot(a_ref[...], b_ref[...], dims)

        if nk == 1:
            finish(product())
            return
        acc_ref = refs[n_in + 1]

        @pl.when(k == 0)
        def _():
            acc_ref[...] = product()

        @pl.when((k > 0) & (k < nk - 1))
        def _():
            acc_ref[...] += product()

        @pl.when(k == nk - 1)
        def _():
            finish(acc_ref[...] + product())

    a_blk = (tk, tm) if mode == "tn" else (tm, tk)
    b_blk = (tn, tk) if mode == "nt" else (tk, tn)
    if a_map is not None:
        a_spec = pl.BlockSpec((None,) + a_blk, a_map(tm, tn, tk))
    elif mode == "tn":
        a_spec = pl.BlockSpec(a_blk, lambda i, j, k: (k, i))
    else:
        a_spec = pl.BlockSpec(a_blk, lambda i, j, k: (i, k))
    if b_map is not None:
        b_spec = pl.BlockSpec((None,) + b_blk, b_map(tm, tn, tk))
    elif mode == "nt":
        b_spec = pl.BlockSpec(b_blk, lambda i, j, k: (j, k))
    else:
        b_spec = pl.BlockSpec(b_blk, lambda i, j, k: (k, j))
    return pl.pallas_call(
        body,
        grid=(M // tm, N // tn, nk),
        in_specs=[a_spec, b_spec] + [ANY] * len(after),
        out_specs=pl.BlockSpec((tm, tn), lambda i, j, k: (i, j)),
        out_shape=jax.ShapeDtypeStruct((M, N), out_dtype),
        scratch_shapes=[pltpu.VMEM((tm, tn), F32)] if nk > 1 else [],
        compiler_params=_params(("parallel", "parallel", "arbitrary"), VMEM_BIG),
        name=name,
    )(a, b, *after)


def _rms_fwd(x, g, *, out_dtype, name, res=None, tr=256):
    n, d = x.shape

    def body(*refs):
        x_ref, g_ref = refs[0], refs[1]
        o_ref = refs[-1]
        xv = x_ref[...].astype(F32)
        y = xv * lax.rsqrt(jnp.mean(xv * xv, axis=-1, keepdims=True) + EPS) * g_ref[...]
        if res is not None:
            y = y + refs[2][...]
        o_ref[...] = y.astype(out_dtype)

    row = pl.BlockSpec((tr, d), lambda i: (i, 0))
    vec = pl.BlockSpec((1, d), lambda i: (0, 0))
    ins = [x, g] + ([] if res is None else [res])
    specs = [row, vec] + ([] if res is None else [row])
    return pl.pallas_call(
        body, grid=(n // tr,), in_specs=specs, out_specs=row,
        out_shape=jax.ShapeDtypeStruct((n, d), out_dtype),
        compiler_params=_params(("parallel",)), name=name,
    )(*ins)


def _rms_bwd(dy, x, g, *, out_dtype, name, dres=None, tr=256):
    n, d = x.shape

    def body(*refs):
        dy_ref, x_ref, g_ref = refs[0], refs[1], refs[2]
        dx_ref, dg_ref = refs[-2], refs[-1]
        xv = x_ref[...].astype(F32)
        dyv = dy_ref[...].astype(F32)
        rstd = lax.rsqrt(jnp.mean(xv * xv, axis=-1, keepdims=True) + EPS)
        xhat = xv * rstd
        dxh = dyv * g_ref[...]
        dx = rstd * (dxh - xhat * jnp.mean(dxh * xhat, axis=-1, keepdims=True))
        if dres is not None:
            dx = dx + refs[3][...]
        dx_ref[...] = dx.astype(out_dtype)

        @pl.when(pl.program_id(0) == 0)
        def _():
            dg_ref[...] = jnp.zeros_like(dg_ref)

        dg_ref[...] += jnp.sum(dyv * xhat, axis=0, keepdims=True)

    row = pl.BlockSpec((tr, d), lambda i: (i, 0))
    vec = pl.BlockSpec((1, d), lambda i: (0, 0))
    ins = [dy, x, g] + ([] if dres is None else [dres])
    specs = [row, row, vec] + ([] if dres is None else [row])
    return pl.pallas_call(
        body, grid=(n // tr,), in_specs=specs, out_specs=[row, vec],
        out_shape=[jax.ShapeDtypeStruct((n, d), out_dtype), jax.ShapeDtypeStruct((1, d), F32)],
        compiler_params=_params(("arbitrary",)), name=name,
    )(*ins)


def _rms_res_in(x, g_res, res, g_in, *, name, tr=256):
    n, d = x.shape

    def body(x_ref, gr_ref, r_ref, gi_ref, h_ref, n_ref):
        xv = x_ref[...].astype(F32)
        h = r_ref[...] + xv * lax.rsqrt(jnp.mean(xv * xv, axis=-1, keepdims=True) + EPS) * gr_ref[...]
        h_ref[...] = h
        n_ref[...] = (h * lax.rsqrt(jnp.mean(h * h, axis=-1, keepdims=True) + EPS) * gi_ref[...]).astype(BF16)

    row = pl.BlockSpec((tr, d), lambda i: (i, 0))
    vec = pl.BlockSpec((1, d), lambda i: (0, 0))
    return pl.pallas_call(
        body, grid=(n // tr,), in_specs=[row, vec, row, vec], out_specs=[row, row],
        out_shape=[jax.ShapeDtypeStruct((n, d), F32), jax.ShapeDtypeStruct((n, d), BF16)],
        compiler_params=_params(("parallel",)), name=name,
    )(x, g_res, res, g_in)


def _rms_bwd2(dy, x, g, dres, x2, g2, *, name, tr=256):
    n, d = x.shape

    def one(dyv, xv, gv):
        rstd = lax.rsqrt(jnp.mean(xv * xv, axis=-1, keepdims=True) + EPS)
        xhat = xv * rstd
        dxh = dyv * gv
        return rstd * (dxh - xhat * jnp.mean(dxh * xhat, axis=-1, keepdims=True)), jnp.sum(dyv * xhat, axis=0, keepdims=True)

    def body(dy_ref, x_ref, g_ref, r_ref, x2_ref, g2_ref, dx_ref, d2_ref, dg_ref, dg2_ref):
        dx, dg = one(dy_ref[...].astype(F32), x_ref[...].astype(F32), g_ref[...])
        dx = dx + r_ref[...]
        dx_ref[...] = dx
        d2, dg2 = one(dx, x2_ref[...].astype(F32), g2_ref[...])
        d2_ref[...] = d2.astype(BF16)

        @pl.when(pl.program_id(0) == 0)
        def _():
            dg_ref[...] = jnp.zeros_like(dg_ref)
            dg2_ref[...] = jnp.zeros_like(dg2_ref)

        dg_ref[...] += dg
        dg2_ref[...] += dg2

    row = pl.BlockSpec((tr, d), lambda i: (i, 0))
    vec = pl.BlockSpec((1, d), lambda i: (0, 0))
    return pl.pallas_call(
        body, grid=(n // tr,), in_specs=[row, row, vec, row, row, vec], out_specs=[row, row, vec, vec],
        out_shape=[jax.ShapeDtypeStruct((n, d), F32), jax.ShapeDtypeStruct((n, d), BF16),
                   jax.ShapeDtypeStruct((1, d), F32), jax.ShapeDtypeStruct((1, d), F32)],
        compiler_params=_params(("arbitrary",)), name=name,
    )(dy, x, g, dres, x2, g2)


def _loss_head(h, target, *, tr=256):
    n, d = h.shape

    def body(h_ref, t_ref, dh_ref, s_ref):
        err = h_ref[...] - t_ref[...]
        dh_ref[...] = err * (1.0 / d)

        @pl.when(pl.program_id(0) == 0)
        def _():
            s_ref[...] = jnp.zeros_like(s_ref)

        s_ref[...] += jnp.sum(err * err)

    row = pl.BlockSpec((tr, d), lambda i: (i, 0))
    acc = pl.BlockSpec((8, 128), lambda i: (0, 0))
    return pl.pallas_call(
        body, grid=(n // tr,), in_specs=[row, row], out_specs=[row, acc],
        out_shape=[jax.ShapeDtypeStruct((n, d), F32), jax.ShapeDtypeStruct((8, 128), F32)],
        compiler_params=_params(("arbitrary",)), name="loss_head",
    )(h, target)


def _rope_tables():
    pos = jnp.arange(T, dtype=F32)
    inv = ROPE_THETA ** (-jnp.arange(0, 16, 2, dtype=F32) / 16)
    ang = pos[:, None] * inv[None, :]
    cos, sin = jnp.cos(ang), jnp.sin(ang)
    one = jnp.ones((T, HD - 16), F32)
    zero8 = jnp.zeros((T, 8), F32)
    zero = jnp.zeros((T, HD - 16), F32)
    c = jnp.concatenate([cos, cos, one], axis=1)
    s1 = jnp.concatenate([zero8, sin, zero], axis=1)
    s2 = jnp.concatenate([-sin, zero8, zero], axis=1)
    c, s1, s2 = (jnp.concatenate([t, t], axis=1) for t in (c, s1, s2))
    scale = HD ** -0.5
    return (jnp.stack([c * scale, c, jnp.ones_like(c)]), jnp.stack([s1 * scale, s1, jnp.zeros_like(c)]),
            jnp.stack([s2 * scale, s2, jnp.zeros_like(c)]))


def _row_chunks(r):
    if r == 1:
        n = 4
        return [(slice(i * (T // n), (i + 1) * (T // n)),) * 2 for i in range(n)]
    per = T // r
    return [(pl.ds(j, per, stride=r), slice(j * per, (j + 1) * per)) for j in range(r)]


def _rope_fwd(qkv, tabs):
    def body(x_ref, c_ref, s1_ref, s2_ref, o_ref):
        g = lax.rem(lax.div(pl.program_id(0), 2), 3)
        for gi, r in enumerate(DIL):
            @pl.when(g == gi)
            def _(r=r):
                for tok, prm in _row_chunks(r):
                    x = x_ref[tok, :]
                    y = x * c_ref[tok, :] + pltpu.roll(x, 8, 1) * s1_ref[tok, :] + pltpu.roll(x, 120, 1) * s2_ref[tok, :]
                    o_ref[prm, :] = y.astype(BF16)

    tab = pl.BlockSpec((None, T, 128), lambda b: (lax.div(b, 6), 0, 0))
    return pl.pallas_call(
        body, grid=(18,), in_specs=[pl.BlockSpec((T, 128), lambda b: (0, b)), tab, tab, tab],
        out_specs=pl.BlockSpec((None, T, 128), lambda b: (b, 0, 0)), out_shape=jax.ShapeDtypeStruct((18, T, 128), BF16),
        compiler_params=_params(("parallel",)), name="rope_fwd",
    )(qkv, *tabs)


def _rope_bwd(d, which, tabs, out_buf):
    def body(d_ref, c_ref, s1_ref, s2_ref, *rest):
        o_ref, tok_ref = rest[-2], rest[-1]
        g = lax.div(pl.program_id(0), 2)
        for gi, r in enumerate(DIL):
            @pl.when(g == gi)
            def _(r=r):
                for tok, prm in _row_chunks(r):
                    tok_ref[tok, :] = d_ref[prm, :]
                for rows, _ in _row_chunks(1):
                    gx = tok_ref[rows, :]
                    y = gx * c_ref[rows, :] + pltpu.roll(gx * s1_ref[rows, :], 120, 1) + pltpu.roll(gx * s2_ref[rows, :], 8, 1)
                    o_ref[rows, :] = y.astype(BF16)

    tab = pl.BlockSpec((None, T, 128), lambda b: (which, 0, 0))
    ins = [d, *tabs] + ([] if out_buf is None else [out_buf])
    specs = [pl.BlockSpec((None, None, T, 128), lambda b: (lax.div(b, 2), lax.rem(b, 2), 0, 0)), tab, tab, tab]
    return pl.pallas_call(
        body, grid=(6,), in_specs=specs + ([] if out_buf is None else [ANY]),
        out_specs=pl.BlockSpec((T, 128), lambda b: (0, 6 * which + b)),
        out_shape=jax.ShapeDtypeStruct((T, 3 * A_W), BF16), scratch_shapes=[pltpu.VMEM((T, 128), F32)],
        input_output_aliases={} if out_buf is None else {4: 0},
        compiler_params=_params(("arbitrary",)), name="rope_bwd",
    )(*ins)


def _head_mask(x, lane_lo):
    lane = lax.broadcasted_iota(jnp.int32, x.shape, 1)
    keep = (lane < HD) if lane_lo else (lane >= HD)
    return jnp.where(keep, x.astype(F32), 0.0).astype(BF16)


def _band_scalars(g):
    b = pl.program_id(0)
    nbs = (T // BLK) // DIL[g]
    has_prev = jnp.where((b & (nbs - 1)) != 0, 1, 0)
    next_ok = jnp.where(((b + 1) & (nbs - 1)) != 0, 1, 0)
    return has_prev, next_ok


def _band_mask_q(has_prev):
    row = lax.broadcasted_iota(jnp.int32, (BLK, 2 * BLK), 0)
    col = lax.broadcasted_iota(jnp.int32, (BLK, 2 * BLK), 1)
    return ((col < BLK) & (col >= row) & (has_prev == 1)) | ((col >= BLK) & (col - BLK <= row))


def _band_mask_k(next_ok):
    row = lax.broadcasted_iota(jnp.int32, (2 * BLK, BLK), 0)
    col = lax.broadcasted_iota(jnp.int32, (2 * BLK, BLK), 1)
    return ((row < BLK) & (col <= row)) | ((row >= BLK) & (col >= row - BLK) & (next_ok == 1))


def _band_spec(step, which=None):
    nb = T // BLK
    at = {"cur": lambda b: b, "prev": lambda b: jnp.maximum(b - 1, 0), "next": lambda b: jnp.minimum(b + 1, nb - 1)}[step]
    if which is None:
        return pl.BlockSpec((3, 2, BLK, 128), lambda b: (0, 0, at(b), 0))
    return pl.BlockSpec((None, 3, 2, BLK, 128), lambda b: (which, 0, 0, at(b), 0))


def _band_fwd(qkv):
    nb = T // BLK

    def body(q_ref, kc_ref, kp_ref, vc_ref, vp_ref, o_ref, l_ref):
        lane = lax.broadcasted_iota(jnp.int32, (BLK, 128), 1)
        for g in range(3):
            has_prev, _ = _band_scalars(g)
            mask = _band_mask_q(has_prev)
            for p in range(2):
                qp = q_ref[g, p]
                kcat = jnp.concatenate([kp_ref[g, p], kc_ref[g, p]], axis=0)
                vcat = jnp.concatenate([vp_ref[g, p], vc_ref[g, p]], axis=0)
                o_acc = jnp.zeros((BLK, 128), F32)
                lse = jnp.zeros((BLK, 128), F32)
                for e in range(2):
                    s = _dot(_head_mask(qp, e == 0), kcat, NT)
                    s = jnp.where(mask, s, NEG)
                    m = jnp.max(s, axis=-1, keepdims=True)
                    pr = jnp.exp(s - m)
                    l = jnp.sum(pr, axis=-1, keepdims=True)
                    o_acc = o_acc + _dot(pr.astype(BF16), _head_mask(vcat, e == 0), NN) / l
                    lse = jnp.where((lane < HD) if e == 0 else (lane >= HD), m + jnp.log(l), lse)
                o_ref[g, p] = o_acc
                l_ref[g, p] = lse

    out = _band_spec("cur")
    shp = jax.ShapeDtypeStruct((3, 2, T, 128), F32)
    return pl.pallas_call(
        body, grid=(nb,),
        in_specs=[_band_spec("cur", 0), _band_spec("cur", 1), _band_spec("prev", 1), _band_spec("cur", 2), _band_spec("prev", 2)],
        out_specs=[out, out], out_shape=[shp, shp],
        compiler_params=_params(("parallel",)), name="band_fwd",
    )(qkv, qkv, qkv, qkv, qkv)


def _band_bwd(qkv, do, lse, dlt):
    nb = T // BLK

    def body(qc_ref, qn_ref, kc_ref, kp_ref, vc_ref, vp_ref, doc_ref, don_ref, lc_ref, ln_ref, dc_ref, dn_ref,
             dq_ref, dk_ref, dv_ref):
        for g in range(3):
            has_prev, next_ok = _band_scalars(g)
            mask_q = _band_mask_q(has_prev)
            mask_k = _band_mask_k(next_ok)
            for p in range(2):
                qc, qn = qc_ref[g, p], qn_ref[g, p]
                doc, don = doc_ref[g, p], don_ref[g, p]
                kc, vc = kc_ref[g, p], vc_ref[g, p]
                kcat = jnp.concatenate([kp_ref[g, p], kc], axis=0)
                vcat = jnp.concatenate([vp_ref[g, p], vc], axis=0)
                qcat = jnp.concatenate([qc, qn], axis=0)
                docat = jnp.concatenate([doc, don], axis=0)
                dq = jnp.zeros((BLK, 128), F32)
                dk = jnp.zeros((BLK, 128), F32)
                dv = jnp.zeros((BLK, 128), F32)
                for e in range(2):
                    lo = e == 0
                    col = slice(HD * e, HD * e + 1)
                    lse_c, lse_n = lc_ref[g, p, :, col], ln_ref[g, p, :, col]
                    dl_c, dl_n = dc_ref[g, p, :, col], dn_ref[g, p, :, col]
                    s = jnp.where(mask_q, _dot(_head_mask(qc, lo), kcat, NT), NEG)
                    pr = jnp.exp(s - lse_c)
                    dp = _dot(_head_mask(doc, lo), vcat, NT)
                    ds = pr * (dp - dl_c)
                    dq = dq + _dot(ds.astype(BF16), _head_mask(kcat, lo), NN)
                    qm, dom = _head_mask(qcat, lo), _head_mask(docat, lo)
                    s2 = jnp.where(mask_k, _dot(qm, kc, NT), NEG)
                    p2 = jnp.exp(s2 - jnp.concatenate([lse_c, lse_n], axis=0))
                    dv = dv + _dot(p2.astype(BF16), dom, TN)
                    dp2 = _dot(dom, vc, NT)
                    ds2 = p2 * (dp2 - jnp.concatenate([dl_c, dl_n], axis=0))
                    dk = dk + _dot(ds2.astype(BF16), qm, TN)
                dq_ref[g, p] = dq
                dk_ref[g, p] = dk
                dv_ref[g, p] = dv

    cur, nxt = _band_spec("cur"), _band_spec("next")
    shp = jax.ShapeDtypeStruct((3, 2, T, 128), F32)
    return pl.pallas_call(
        body, grid=(nb,),
        in_specs=[_band_spec("cur", 0), _band_spec("next", 0), _band_spec("cur", 1), _band_spec("prev", 1),
                  _band_spec("cur", 2), _band_spec("prev", 2), cur, nxt, cur, nxt, cur, nxt],
        out_specs=[cur, cur, cur], out_shape=[shp, shp, shp],
        compiler_params=_params(("parallel",)), name="band_bwd",
    )(qkv, qkv, qkv, qkv, qkv, qkv, do, do, lse, lse, dlt, dlt)


def _split3(x):
    hi = x.astype(BF16)
    r = x - hi.astype(F32)
    mid = r.astype(BF16)
    lo = (r - mid.astype(F32)).astype(BF16)
    return hi, mid, lo


def _dot3(x, m, dims=NN):
    hi, mid, lo = _split3(x)
    return _dot(hi, m, dims) + _dot(mid, m, dims) + _dot(lo, m, dims)


def _combine_weights(lses):
    l0, l1, l2 = lses
    m = jnp.maximum(jnp.maximum(l0, l1), l2)
    e = [jnp.exp(l0 - m), jnp.exp(l1 - m), jnp.exp(l2 - m)]
    inv = 1.0 / (e[0] + e[1] + e[2])
    return [ei * inv for ei in e]


CR = 256


def _combine_fwd(o, lse):
    def body(o_ref, l_ref, att_ref, o3_ref, l3_ref):
        for g, r in enumerate(DIL):
            for p in range(2):
                for tok, prm in _row_chunks(r):
                    o3_ref[g, p, tok, :] = o_ref[g, p, prm, :]
                    l3_ref[g, p, tok, :] = l_ref[g, p, prm, :]
        for i in range(T // CR):
            rows = slice(i * CR, (i + 1) * CR)
            for p in range(2):
                alpha = _combine_weights([l3_ref[g, p, rows, :] for g in range(3)])
                for g in range(3):
                    att_ref[rows, g * GW + p * 128: g * GW + (p + 1) * 128] = (o3_ref[g, p, rows, :] * alpha[g]).astype(BF16)

    shp = jax.ShapeDtypeStruct((3, 2, T, 128), F32)
    return pl.pallas_call(
        body, out_shape=[jax.ShapeDtypeStruct((T, A_W), BF16), shp, shp],
        compiler_params=_params(vmem=VMEM_BIG), name="combine_fwd",
    )(o, lse)


def _combine_bwd(datt, o3, l3, headsum):
    def body(d_ref, o_ref, l_ref, hs_ref, do_ref, dl_ref, tdo_ref, tdl_ref):
        hs = hs_ref[...]
        for p in range(2):
            for i in range(T // CR):
                rows = slice(i * CR, (i + 1) * CR)
                alpha = _combine_weights([l_ref[g, p, rows, :] for g in range(3)])
                total = jnp.zeros((CR, 128), F32)
                for g in range(3):
                    dg = d_ref[rows, g * GW + p * 128: g * GW + (p + 1) * 128]
                    tdo_ref[g, rows, :] = dg * alpha[g]
                    total = total + alpha[g] * _dot3(dg * o_ref[g, p, rows, :], hs)
                for g in range(3):
                    tdl_ref[g, rows, :] = alpha[g] * total
            for g, r in enumerate(DIL):
                for tok, prm in _row_chunks(r):
                    do_ref[g, p, prm, :] = tdo_ref[g, tok, :].astype(BF16)
                    dl_ref[g, p, prm, :] = tdl_ref[g, tok, :]

    return pl.pallas_call(
        body, out_shape=[jax.ShapeDtypeStruct((3, 2, T, 128), BF16), jax.ShapeDtypeStruct((3, 2, T, 128), F32)],
        scratch_shapes=[pltpu.VMEM((3, T, 128), F32), pltpu.VMEM((3, T, 128), F32)],
        compiler_params=_params(vmem=VMEM_BIG), name="combine_bwd",
    )(datt, o3, l3, headsum)


def _fox_scores(qm, k_ref, ck_ref, e, i, n):
    s = _dot(qm, k_ref[0:n, :], NT) - ck_ref[0, e:e + 1, 0:n]
    row = lax.broadcasted_iota(jnp.int32, (FQ, FQ), 0)
    col = lax.broadcasted_iota(jnp.int32, (FQ, FQ), 1)
    diag = jnp.where(col <= row, s[:, n - FQ:], NEG)
    m = jnp.max(diag, axis=-1, keepdims=True)
    if i == 0:
        pr = jnp.exp(diag - m)
        return pr, jnp.sum(pr, axis=-1, keepdims=True)
    past = s[:, :n - FQ]
    m = jnp.maximum(m, jnp.max(past, axis=-1, keepdims=True))
    p_past, p_diag = jnp.exp(past - m), jnp.exp(diag - m)
    l = jnp.sum(p_past, axis=-1, keepdims=True) + jnp.sum(p_diag, axis=-1, keepdims=True)
    return jnp.concatenate([p_past, p_diag], axis=1), l


def _fox_fwd(q, kv, c_row):
    def body(q_ref, k_ref, v_ref, cr_ref, o_ref, vm_ref):
        for e in range(2):
            vm_ref[e] = _head_mask(v_ref[...], e == 0)
        for i in range(T // FQ):
            n = (i + 1) * FQ
            rows = slice(i * FQ, n)
            acc = jnp.zeros((FQ, 128), F32)
            for e in range(2):
                qm = _head_mask(q_ref[rows, :], e == 0)
                pr, l = _fox_scores(qm, k_ref, cr_ref, e, i, n)
                acc = acc + _dot(pr.astype(BF16), vm_ref[e, 0:n, :], NN) / l
            o_ref[rows, :] = acc.astype(BF16)

    pair = pl.BlockSpec((T, 128), lambda p: (0, p))
    return pl.pallas_call(
        body, grid=(D // 128,),
        in_specs=[pair, pair, pl.BlockSpec((T, 128), lambda p: (0, D // 128 + p)), pl.BlockSpec((1, 2, T), lambda p: (p, 0, 0))],
        out_specs=pair, out_shape=jax.ShapeDtypeStruct((T, D), BF16),
        scratch_shapes=[pltpu.VMEM((2, T, 128), BF16)],
        compiler_params=_params(("parallel",), VMEM_BIG), name="fox_fwd",
    )(q, kv, kv, c_row)


def _fox_bwd(q, kv, do, c_row, init):
    def body(q_ref, k_ref, v_ref, do_ref, cr_ref, *rest):
        dq_ref, dk_ref, dv_ref, dck_ref, km_ref = rest[-5:]
        for o_ref, i_ref in zip((dk_ref, dv_ref, dck_ref), rest[:-5] or (None,) * 3):
            o_ref[...] = jnp.zeros_like(o_ref) if i_ref is None else i_ref[...]
        for e in range(2):
            km_ref[e] = _head_mask(k_ref[...], e == 0)
        for i in range(T // FQ):
            n = (i + 1) * FQ
            rows = slice(i * FQ, n)
            dq = jnp.zeros((FQ, 128), F32)
            dk = jnp.zeros((n, 128), F32)
            dv = jnp.zeros((n, 128), F32)
            for e in range(2):
                qm = _head_mask(q_ref[rows, :], e == 0)
                dom = _head_mask(do_ref[rows, :], e == 0)
                pr, l = _fox_scores(qm, k_ref, cr_ref, e, i, n)
                pr = pr * (1.0 / l)
                dp = _dot(dom, v_ref[0:n, :], NT)
                ds = pr * (dp - jnp.sum(pr * dp, axis=-1, keepdims=True))
                dsb = ds.astype(BF16)
                dq = dq + _dot(dsb, km_ref[e, 0:n, :], NN)
                dk = dk + _dot(dsb, qm, TN)
                dv = dv + _dot(pr.astype(BF16), dom, TN)
                dck_ref[0, e:e + 1, 0:n] += jnp.sum(ds, axis=0, keepdims=True)
            dk_ref[0:n, :] += dk
            dv_ref[0:n, :] += dv
            dq_ref[rows, :] = (dq * HD ** -0.5).astype(BF16)

    pair = pl.BlockSpec((T, 128), lambda p: (0, p))
    ck = pl.BlockSpec((1, 8, T), lambda p: (p, 0, 0))
    return pl.pallas_call(
        body, grid=(D // 128,),
        in_specs=[pair, pair, pl.BlockSpec((T, 128), lambda p: (0, D // 128 + p)), pair,
                  pl.BlockSpec((1, 2, T), lambda p: (p, 0, 0))] + ([] if init is None else [pair, pair, ck]),
        out_specs=[pair, pair, pair, ck],
        out_shape=[jax.ShapeDtypeStruct((T, D), BF16), jax.ShapeDtypeStruct((T, D), F32), jax.ShapeDtypeStruct((T, D), F32),
                   jax.ShapeDtypeStruct((D // 128, 8, T), F32)],
        scratch_shapes=[pltpu.VMEM((2, T, 128), BF16)],
        compiler_params=_params(("parallel",), VMEM_BIG), name="fox_bwd",
    )(q, kv, kv, do, c_row, *(init or ()))


def _tri(lower):
    r = lax.broadcasted_iota(jnp.int32, (BLK, BLK), 0)
    c = lax.broadcasted_iota(jnp.int32, (BLK, BLK), 1)
    return jnp.where((c <= r) if lower else (c >= r), 1.0, 0.0).astype(BF16)


def _gates_fwd(z, b):
    def body(z_ref, b_ref, c_ref):
        tri = _tri(True)
        carry = jnp.zeros((1, 128), F32)
        for i in range(T // BLK):
            rows = slice(i * BLK, (i + 1) * BLK)
            x = z_ref[rows, :] + b_ref[...]
            logf = jnp.minimum(x, 0.0) - jnp.log(1.0 + jnp.exp(-jnp.abs(x)))
            hi, mid, lo = _split3(logf)
            y = _dot(tri, hi, NN) + _dot(tri, mid, NN) + _dot(tri, lo, NN) + carry
            c_ref[rows, :] = y
            carry = y[BLK - 1:BLK, :]

    return pl.pallas_call(body, out_shape=jax.ShapeDtypeStruct((T, 128), F32), name="gates_fwd")(z, b)


def _gates_bwd(dc, z, b):
    def body(dc_ref, z_ref, b_ref, dz_ref, db_ref):
        tri = _tri(False)
        carry = jnp.zeros((1, 128), F32)
        db = jnp.zeros((1, 128), F32)
        for i in reversed(range(T // BLK)):
            rows = slice(i * BLK, (i + 1) * BLK)
            hi, mid, lo = _split3(dc_ref[rows, :])
            dlogf = _dot(tri, hi, NN) + _dot(tri, mid, NN) + _dot(tri, lo, NN) + carry
            carry = dlogf[0:1, :]
            x = z_ref[rows, :] + b_ref[...]
            dz = dlogf / (1.0 + jnp.exp(x))
            dz_ref[rows, :] = dz.astype(BF16)
            db = db + jnp.sum(dz, axis=0, keepdims=True)
        db_ref[...] = db

    return pl.pallas_call(
        body, out_shape=[jax.ShapeDtypeStruct((T, 128), BF16), jax.ShapeDtypeStruct((1, 128), F32)], name="gates_bwd",
    )(dc, z, b)


def _conv_pair(a_refs, cw_refs, cb_refs):
    row = lax.broadcasted_iota(jnp.int32, (T, CT), 0)
    outs = []
    for a_ref, cw_ref, cb_ref in zip(a_refs, cw_refs, cb_refs):
        z = a_ref[...]
        z1 = jnp.where(row >= 1, pltpu.roll(z, 1, 0), 0.0)
        z2 = jnp.where(row >= 2, pltpu.roll(z, 2, 0), 0.0)
        y = cw_ref[2:3, :] * z + cw_ref[1:2, :] * z1 + cw_ref[0:1, :] * z2 + cb_ref[...]
        outs.append((y, z, z1, z2))
    return outs


_GELU_K = math.sqrt(2.0 / math.pi)
N_CT = D_FF // CT


def _conv_specs():
    def at(rows, off):
        return pl.BlockSpec((rows, CT), lambda j: (0, j + off))
    return [at(T, 0), at(T, N_CT), at(3, 0), at(3, N_CT), at(1, 0), at(1, N_CT)]


def _convgate_fwd(a, cw, cb):
    def body(ag_ref, av_ref, wg_ref, wv_ref, bg_ref, bv_ref, u_ref):
        (g, _, _, _), (v, _, _, _) = _conv_pair((ag_ref, av_ref), (wg_ref, wv_ref), (bg_ref, bv_ref))
        th = jnp.tanh(_GELU_K * (g + 0.044715 * g * g * g))
        u_ref[...] = (0.5 * g * (1.0 + th) * v).astype(BF16)

    return pl.pallas_call(
        body, grid=(N_CT,), in_specs=_conv_specs(),
        out_specs=pl.BlockSpec((T, CT), lambda j: (0, j)), out_shape=jax.ShapeDtypeStruct((T, D_FF), BF16),
        compiler_params=_params(("parallel",), VMEM_BIG), name="convgate_fwd",
    )(a, a, cw, cw, cb, cb)


def _convgate_bwd(a, du, cw, cb):
    def body(ag_ref, av_ref, wg_ref, wv_ref, bg_ref, bv_ref, du_ref, da_ref, dcw_ref, dcb_ref):
        (g, gz, gz1, gz2), (v, vz, vz1, vz2) = _conv_pair((ag_ref, av_ref), (wg_ref, wv_ref), (bg_ref, bv_ref))
        du = du_ref[...].astype(F32)
        th = jnp.tanh(_GELU_K * (g + 0.044715 * g * g * g))
        gelu = 0.5 * g * (1.0 + th)
        dgelu = 0.5 * (1.0 + th) + 0.5 * g * (1.0 - th * th) * _GELU_K * (1.0 + 3 * 0.044715 * g * g)
        row = lax.broadcasted_iota(jnp.int32, (T, CT), 0)
        for h, (d, z, z1, z2, w_ref) in enumerate(((du * v * dgelu, gz, gz1, gz2, wg_ref), (du * gelu, vz, vz1, vz2, wv_ref))):
            d1 = jnp.where(row < T - 1, pltpu.roll(d, T - 1, 0), 0.0)
            d2 = jnp.where(row < T - 2, pltpu.roll(d, T - 2, 0), 0.0)
            da_ref[h] = (w_ref[2:3, :] * d + w_ref[1:2, :] * d1 + w_ref[0:1, :] * d2).astype(BF16)
            dcw_ref[h, 0:1, :] = jnp.sum(d * z2, axis=0, keepdims=True)
            dcw_ref[h, 1:2, :] = jnp.sum(d * z1, axis=0, keepdims=True)
            dcw_ref[h, 2:3, :] = jnp.sum(d * z, axis=0, keepdims=True)
            dcb_ref[h] = jnp.sum(d, axis=0, keepdims=True)

    def both(rows):
        return pl.BlockSpec((2, rows, CT), lambda j: (0, 0, j))

    return pl.pallas_call(
        body, grid=(N_CT,),
        in_specs=_conv_specs() + [pl.BlockSpec((T, CT), lambda j: (0, j))],
        out_specs=[both(T), both(3), both(1)],
        out_shape=[jax.ShapeDtypeStruct((2, T, D_FF), BF16), jax.ShapeDtypeStruct((2, 3, D_FF), F32),
                   jax.ShapeDtypeStruct((2, 1, D_FF), F32)],
        compiler_params=_params(("parallel",), VMEM_BIG), name="convgate_bwd",
    )(a, a, cw, cw, cb, cb, du)


def _halves_a(tm, tn, tk):
    per = D_FF // tk
    return lambda i, j, k: (lax.div(k, per), i, lax.rem(k, per))


def _halves_b(tm, tn, tk):
    per = D_FF // tn
    return lambda i, j, k: (lax.div(j, per), k, lax.rem(j, per))


def _adamw(w, m, v, g, *, name):
    r, c = w.shape
    tr = r
    if r * c > 256 * 1024:
        for cand in range(8, r, 8):
            if r % cand == 0 and cand * c <= 256 * 1024:
                tr = cand

    def body(w_ref, m_ref, v_ref, g_ref, d_ref, nm_ref, nv_ref):
        gv = g_ref[...]
        mn = ADAM_B1 * m_ref[...] + (1.0 - ADAM_B1) * gv
        vn = ADAM_B2 * v_ref[...] + (1.0 - ADAM_B2) * (gv * gv)
        m_hat = mn * (1.0 / (1.0 - ADAM_B1 ** ADAM_STEP))
        v_hat = vn * (1.0 / (1.0 - ADAM_B2 ** ADAM_STEP))
        d_ref[...] = -ADAM_LR * (m_hat / (jnp.sqrt(v_hat) + ADAM_EPS) + ADAM_WD * w_ref[...])
        nm_ref[...] = mn
        nv_ref[...] = vn

    blk = pl.BlockSpec((tr, c), lambda i: (i, 0))
    shp = jax.ShapeDtypeStruct((r, c), F32)
    return pl.pallas_call(
        body, grid=(r // tr,), in_specs=[blk] * 4, out_specs=[blk] * 3, out_shape=[shp] * 3,
        compiler_params=_params(("parallel",)), name=name,
    )(w, m, v, g)


def _place():
    x, y, c = lax.axis_index("x"), lax.axis_index("y"), lax.axis_index("c")
    chips = [(1 - x, y), (x, 1 - y), (1 - x, 1 - y)]
    return x, y, c, chips


def _window(ref, kind, s, half=None):
    lead = () if half is None else (half,)
    b, c = ref.shape[-2], ref.shape[-1]
    if kind == "col":
        return ref.at[lead + (slice(None), slice(None), pl.ds(s * (c // N_CHIPS), c // N_CHIPS))]
    if kind == "row":
        return ref.at[lead + (slice(None), pl.ds(s * (b // N_CHIPS), b // N_CHIPS), slice(None))]
    return ref.at[lead + (s,)]


def _allgather(tensors, kinds, *, name):
    n = len(tensors)

    def body(*refs):
        bufs = refs[n:2 * n]
        send, recv = refs[2 * n:]
        x, y, c, chips = _place()
        me = 2 * x + y
        sib = (x, y, 1 - c)

        def rcopy(i, k, win, to):
            return pltpu.make_async_remote_copy(src_ref=win, dst_ref=win, send_sem=send.at[i * 6 + k], recv_sem=recv.at[i * 6 + k],
                                                device_id=to, device_id_type=MESH)

        started = []
        for i in range(n):
            for k, (px, py) in enumerate(chips):
                cp = rcopy(i, k, _window(bufs[i], kinds[i], me, c), (px, py, c))
                cp.start()
                started.append(cp)
        for i in range(n):
            for k, (px, py) in enumerate(chips):
                landed = _window(bufs[i], kinds[i], 2 * px + py, c)
                rcopy(i, k, landed, (px, py, c)).wait_recv()
                fw = rcopy(i, 3 + k, landed, sib)
                fw.start()
                started.append(fw)
        for i in range(n):
            for k, (px, py) in enumerate(chips):
                rcopy(i, 3 + k, _window(bufs[i], kinds[i], 2 * px + py, 1 - c), sib).wait_recv()
        for cp in started:
            cp.wait_send()

    return pl.pallas_call(
        body, in_specs=[ANY] * n, out_specs=[ANY] * n,
        out_shape=[jax.ShapeDtypeStruct(t.shape, t.dtype) for t in tensors],
        scratch_shapes=[pltpu.SemaphoreType.DMA((6 * n,)), pltpu.SemaphoreType.DMA((6 * n,))],
        input_output_aliases={i: i for i in range(n)},
        name=name,
    )(*tensors)


def _rows_tile(rows, cols, sub):
    best = None
    for t in range(sub, rows + 1, sub):
        if rows % t == 0 and t * cols <= 512 * 1024:
            best = t
    return rows if best is None else best


def _sequencer(name, cid, n_sems, peers_of, body):
    @pl.kernel(mesh=plsc.ScalarSubcoreMesh(axis_name="seq", num_cores=1), name=name,
               scratch_types=(pltpu.SemaphoreType.DMA((n_sems,)), pltpu.SemaphoreType.DMA((n_sems,))),
               compiler_params=pltpu.CompilerParams(collective_id=cid))
    def launch(send, recv):
        x, y, c, chips = _place()
        peers = peers_of(x, y, c, chips)
        barrier = pltpu.get_barrier_semaphore()
        for peer in peers:
            pl.semaphore_signal(barrier, inc=1, device_id=peer, device_id_type=MESH)
        pl.semaphore_wait(barrier, len(peers))
        body(send, recv)

    launch()


def _half_of_full(ref, kind, h):
    if kind == "col":
        b = ref.shape[0]
        return ref.at[pl.ds(h * (b // 2), b // 2), :]
    if kind == "row":
        c = ref.shape[1]
        return ref.at[:, pl.ds(h * (c // 2), c // 2)]
    b = ref.shape[1]
    return ref.at[:, pl.ds(h * (b // 2), b // 2), :]


def _half_shape(full, kind):
    if kind == "col":
        return (full[0] // 2, full[1])
    if kind == "row":
        return (full[0], full[1] // 2)
    return (full[0], full[1] // 2, full[2])


def _win_of_half(ref, kind, s):
    if kind == "col":
        c = ref.shape[1]
        return ref.at[:, pl.ds(s * (c // N_CHIPS), c // N_CHIPS)]
    if kind == "row":
        b = ref.shape[0]
        return ref.at[pl.ds(s * (b // N_CHIPS), b // N_CHIPS), :]
    return ref.at[s]


def _win_shape(half, kind):
    if kind == "col":
        return (half[0], half[1] // N_CHIPS)
    if kind == "row":
        return (half[0] // N_CHIPS, half[1])
    return half[1:]


def _seq_swap(parts, kinds, *, name):
    n = len(parts)
    srcs = [jax.new_ref(p, memory_space=pltpu.MemorySpace.HBM) for p in parts]
    outs = [jax.empty_ref(jax.ShapeDtypeStruct(_half_shape(p.shape, k), p.dtype), memory_space=pltpu.MemorySpace.HBM)
            for p, k in zip(parts, kinds)]

    def body(send, recv):
        x, y, c, _ = _place()
        cps = []
        for i in range(n):
            cp = pltpu.make_async_remote_copy(src_ref=_half_of_full(srcs[i], kinds[i], 1 - c), dst_ref=outs[i], send_sem=send.at[i],
                                              recv_sem=recv.at[i], device_id=(x, y, 1 - c), device_id_type=MESH)
            cp.start()
            cps.append(cp)
        for cp in cps:
            cp.wait()

    _sequencer(name, 2, n, lambda x, y, c, chips: [(x, y, 1 - c)], body)
    return [o[...] for o in outs]


def _seq_scatter(halves, kinds, *, name):
    n = len(halves)
    srcs = [jax.new_ref(h, memory_space=pltpu.MemorySpace.HBM) for h in halves]
    outs = [jax.empty_ref(jax.ShapeDtypeStruct((3,) + _win_shape(h.shape, k), h.dtype), memory_space=pltpu.MemorySpace.HBM)
            for h, k in zip(halves, kinds)]

    def body(send, recv):
        x, y, c, chips = _place()
        cps = []
        for i in range(n):
            for k, (px, py) in enumerate(chips):
                cp = pltpu.make_async_remote_copy(src_ref=_win_of_half(srcs[i], kinds[i], 2 * px + py), dst_ref=outs[i].at[k],
                                                  send_sem=send.at[3 * i + k], recv_sem=recv.at[3 * i + k],
                                                  device_id=(px, py, c), device_id_type=MESH)
                cp.start()
                cps.append(cp)
        for cp in cps:
            cp.wait()

    _sequencer(name, 3, 3 * n, lambda x, y, c, chips: [(px, py, c) for px, py in chips], body)
    return [o[...] for o in outs]


def _add_half(g, p, kind, where, after, *, name):
    if kind == "slab":
        s, b2, c = p.shape
        tr = _rows_tile(b2, c, 16)
        nr = b2 // tr
        grid = (s, nr)
        g_spec = pl.BlockSpec((None, tr, c), lambda i, r, w: (i, w[1] * nr + r, 0))
        p_spec = pl.BlockSpec((None, tr, c), lambda i, r, w: (i, r, 0))
    elif kind == "col":
        b2, c = p.shape
        tr = _rows_tile(b2, c, 16)
        nr = b2 // tr
        grid = (1, nr)
        g_spec = pl.BlockSpec((tr, c), lambda i, r, w: (w[1] * nr + r, 0))
        p_spec = pl.BlockSpec((tr, c), lambda i, r, w: (r, 0))
    else:
        b, c2 = p.shape
        tr = _rows_tile(b, c2, 16)
        grid = (1, b // tr)
        g_spec = pl.BlockSpec((tr, c2), lambda i, r, w: (r, w[1]))
        p_spec = pl.BlockSpec((tr, c2), lambda i, r, w: (r, 0))

    def body(w_ref, g_ref, p_ref, *rest):
        o_ref = rest[-1]
        o_ref[...] = (g_ref[...].astype(F32) + p_ref[...].astype(F32)).astype(o_ref.dtype)

    extra = [] if after is None else [after]
    return pl.pallas_call(
        body,
        grid_spec=pltpu.PrefetchScalarGridSpec(num_scalar_prefetch=1, grid=grid, in_specs=[g_spec, p_spec] + [ANY] * len(extra),
                                               out_specs=p_spec),
        out_shape=jax.ShapeDtypeStruct(p.shape, g.dtype),
        compiler_params=_params(("parallel", "parallel")), name=name,
    )(where, g, p, *extra)


def _sum_chips(r, h, kind, where, layer, layers, out_buf, after, *, name):
    _, br, cr = r.shape
    tr = _rows_tile(br, cr, 16)
    nr = br // tr
    if kind == "col":
        h_spec = pl.BlockSpec((tr, cr), lambda j, w: (j, w[0]))
        o_shape, o_spec = (layers, 2 * br, cr), pl.BlockSpec((None, tr, cr), lambda j, w: (layer, w[1] * nr + j, 0))
    elif kind == "row":
        h_spec = pl.BlockSpec((tr, cr), lambda j, w: (w[0] * nr + j, 0))
        o_shape, o_spec = (layers, br, 2 * cr), pl.BlockSpec((None, tr, cr), lambda j, w: (layer, j, w[1]))
    else:
        h_spec = pl.BlockSpec((None, tr, cr), lambda j, w: (w[0], j, 0))
        o_shape, o_spec = (layers, 2 * br, cr), pl.BlockSpec((None, tr, cr), lambda j, w: (layer, w[1] * nr + j, 0))

    def body(w_ref, h_ref, r0_ref, r1_ref, r2_ref, *rest):
        o_ref, t_ref = rest[-2], rest[-1]
        o_ref[...] = ((h_ref[...].astype(F32) + r0_ref[...].astype(F32)) + r1_ref[...].astype(F32)) + r2_ref[...].astype(F32)
        t_ref[...] = jnp.zeros_like(t_ref)

    def slot(k):
        return pl.BlockSpec((None, tr, cr), lambda j, w: (k, j, 0))

    ins, specs, alias = [h, r, r, r], [h_spec, slot(0), slot(1), slot(2)], {}
    if after is not None:
        ins.append(after)
        specs.append(ANY)
    if out_buf is not None:
        alias = {1 + len(ins): 0}
        ins.append(out_buf)
        specs.append(ANY)
    return pl.pallas_call(
        body,
        grid_spec=pltpu.PrefetchScalarGridSpec(num_scalar_prefetch=1, grid=(nr,), in_specs=specs,
                                               out_specs=[o_spec, pl.BlockSpec((8, 128), lambda j, w: (0, 0))]),
        out_shape=[jax.ShapeDtypeStruct(o_shape, F32), jax.ShapeDtypeStruct((8, 128), F32)], input_output_aliases=alias,
        compiler_params=_params(("arbitrary",)), name=name,
    )(where, *ins)


def _join_halves(tensors, kinds, *, name):
    n = len(tensors)

    def mine(ref, kind, h):
        if kind == "row":
            c = ref.shape[2]
            return ref.at[:, :, pl.ds(h * (c // 2), c // 2)]
        b = ref.shape[1]
        return ref.at[:, pl.ds(h * (b // 2), b // 2), :]

    def body(*refs):
        bufs = refs[n:2 * n]
        send, recv = refs[2 * n:]
        x, y, c, _ = _place()
        cps = []
        for i in range(n):
            part = mine(bufs[i], kinds[i], c)
            cp = pltpu.make_async_remote_copy(src_ref=part, dst_ref=part, send_sem=send.at[i],
                                              recv_sem=recv.at[i], device_id=(x, y, 1 - c), device_id_type=MESH)
            cp.start()
            cps.append(cp)
        for i in range(n):
            other = mine(bufs[i], kinds[i], 1 - c)
            pltpu.make_async_remote_copy(src_ref=other, dst_ref=other, send_sem=send.at[i],
                                         recv_sem=recv.at[i], device_id=(x, y, 1 - c), device_id_type=MESH).wait_recv()
        for cp in cps:
            cp.wait_send()

    return pl.pallas_call(
        body, in_specs=[ANY] * n, out_specs=[ANY] * n,
        out_shape=[jax.ShapeDtypeStruct(t.shape, t.dtype) for t in tensors],
        scratch_shapes=[pltpu.SemaphoreType.DMA((n,)), pltpu.SemaphoreType.DMA((n,))],
        input_output_aliases={i: i for i in range(n)},
        name=name,
    )(*tensors)


def _win(ref, kind, s, h=None):
    if kind == "col":
        b, c = ref.shape
        cols = pl.ds(s * (c // N_CHIPS), c // N_CHIPS)
        return ref.at[:, cols] if h is None else ref.at[pl.ds(h * (b // 2), b // 2), cols]
    if kind == "row":
        b, c = ref.shape
        rows = pl.ds(s * (b // N_CHIPS), b // N_CHIPS)
        return ref.at[rows, :] if h is None else ref.at[rows, pl.ds(h * (c // 2), c // 2)]
    b = ref.shape[1]
    return ref.at[s] if h is None else ref.at[s, pl.ds(h * (b // 2), b // 2)]


def _half(ref, kind, h):
    b, c = ref.shape
    if kind == "row":
        return ref.at[:, pl.ds(h * (c // 2), c // 2)]
    return ref.at[pl.ds(h * (b // 2), b // 2), :]


def _full_shape(shard_shape, kind):
    b, c = shard_shape
    return {"col": (b, N_CHIPS * c), "row": (N_CHIPS * b, c), "slab": (N_CHIPS, b, c)}[kind]


def _gather_body(srcs, outs, kinds, send, recv):
    x, y, c, chips = _place()
    me = 2 * x + y
    sib = (x, y, 1 - c)

    def rcopy(i, k, src, dst, to):
        return pltpu.make_async_remote_copy(src_ref=src, dst_ref=dst, send_sem=send.at[7 * i + k], recv_sem=recv.at[7 * i + k],
                                            device_id=to, device_id_type=MESH)

    started = []
    for i, (src, out, kind) in enumerate(zip(srcs, outs, kinds)):
        own = rcopy(i, 6, src, _win(out, kind, me), sib)
        own.start()
        started.append(own)
        for k, (px, py) in enumerate(chips):
            cp = rcopy(i, k, _half(src, kind, c), _win(out, kind, me, c), (px, py, c))
            cp.start()
            started.append(cp)
    for i, (out, kind) in enumerate(zip(outs, kinds)):
        for k, (px, py) in enumerate(chips):
            landed = _win(out, kind, 2 * px + py, c)
            rcopy(i, k, landed, landed, (px, py, c)).wait_recv()
            fw = rcopy(i, 3 + k, landed, landed, sib)
            fw.start()
            started.append(fw)
    for i, (src, out, kind) in enumerate(zip(srcs, outs, kinds)):
        for k, (px, py) in enumerate(chips):
            other = _win(out, kind, 2 * px + py, 1 - c)
            rcopy(i, 3 + k, other, other, sib).wait_recv()
        rcopy(i, 6, src, _win(out, kind, me), sib).wait_recv()
    for cp in started:
        cp.wait_send()


def _seq_gather(shards, kinds, *, name, cid):
    n = len(shards)
    srcs = [jax.new_ref(s, memory_space=pltpu.MemorySpace.HBM) for s in shards]
    outs = [jax.empty_ref(jax.ShapeDtypeStruct(_full_shape(s.shape, k), s.dtype), memory_space=pltpu.MemorySpace.HBM)
            for s, k in zip(shards, kinds)]

    @pl.kernel(mesh=plsc.ScalarSubcoreMesh(axis_name="seq", num_cores=1), name=name,
               scratch_types=(pltpu.SemaphoreType.DMA((7 * n,)), pltpu.SemaphoreType.DMA((7 * n,))),
               compiler_params=pltpu.CompilerParams(collective_id=cid))
    def launch(send, recv):
        x, y, c, chips = _place()
        barrier = pltpu.get_barrier_semaphore()
        for px, py in chips:
            pl.semaphore_signal(barrier, inc=1, device_id=(px, py, c), device_id_type=MESH)
        pl.semaphore_signal(barrier, inc=1, device_id=(x, y, 1 - c), device_id_type=MESH)
        pl.semaphore_wait(barrier, 4)
        _gather_body(srcs, outs, kinds, send, recv)

    launch()
    return [o[...] for o in outs]


KIND = dict(w_qkv_a="slab", w_o_a="col", w_q_b="row", w_o_b="row", w_kvf="slab", w_up="col", w_down="row", small="slab")
LAYERS = dict(w_qkv_a=N_A, w_o_a=N_A, w_q_b=DEPTH - N_A, w_o_b=DEPTH - N_A, w_kvf=1, w_up=DEPTH, w_down=DEPTH, small=1)
SMALL_W = 1792
SMALL_ROWS = 8


class _Reducer:
    def __init__(self, where):
        self.where = where
        self.acc = {nm: None for nm in KIND}
        self.pending = None

    def __call__(self, group, tag):
        names, layers, parts = zip(*group)
        kinds = [KIND[nm] for nm in names]
        summed = self._sum_pending(after=parts[-1])
        sib = _seq_swap(list(parts), kinds, name="reduce_swap_" + tag)
        halves = []
        for g, p, k, nm in zip(parts, sib, kinds, names):
            halves.append(_add_half(g, p, k, self.where, halves[-1] if halves else None, name="reduce_add_" + nm))
        landed = _seq_scatter(halves, kinds, name="reduce_scatter_" + tag)
        self.pending = (names, layers, landed, halves, kinds)
        return [halves[-1], summed]

    def flush(self, after):
        return self._sum_pending(after)

    def _sum_pending(self, after):
        if self.pending is None:
            return None
        for nm, l, r, h, k in zip(*self.pending):
            self.acc[nm], after = _sum_chips(r, h, k, self.where, l, LAYERS[nm], self.acc[nm], after, name="reduce_sum_" + nm)
        self.pending = None
        return after

    def finish(self):
        self._sum_pending(after=None)
        names = list(KIND)
        joined = _join_halves([self.acc[nm] for nm in names], [KIND[nm] for nm in names], name="reduce_pair_join")
        return dict(zip(names, joined))


def _headsum_matrix():
    r = lax.broadcasted_iota(jnp.int32, (128, 128), 0) // HD
    c = lax.broadcasted_iota(jnp.int32, (128, 128), 1) // HD
    return jnp.where(r == c, 1.0, 0.0).astype(BF16)


def kernel(x, norm_gains, w_qkv_a, w_o_a, w_q_b, w_o_b, kv_norm, w_kvf, b_f, w_up, conv_w, conv_b, w_down, loss_target, m_norm_gains, m_w_qkv_a, m_w_o_a, m_w_q_b, m_w_o_b, m_kv_norm, m_w_kvf, m_b_f, m_w_up, m_conv_w, m_conv_b, m_w_down, v_norm_gains, v_w_qkv_a, v_w_o_a, v_w_q_b, v_w_o_b, v_kv_norm, v_w_kvf, v_b_f, v_w_up, v_conv_w, v_conv_b, v_w_down):
    xi, yi, ci = lax.axis_index("x"), lax.axis_index("y"), lax.axis_index("c")
    chip = 2 * xi + yi
    where = jnp.stack([chip, ci]).astype(jnp.int32)
    ws = dict(norm_gains=norm_gains, w_qkv_a=w_qkv_a, w_o_a=w_o_a, w_q_b=w_q_b, w_o_b=w_o_b, kv_norm=kv_norm, w_kvf=w_kvf,
              b_f=b_f, w_up=w_up, conv_w=conv_w, conv_b=conv_b, w_down=w_down)
    ms = dict(norm_gains=m_norm_gains, w_qkv_a=m_w_qkv_a, w_o_a=m_w_o_a, w_q_b=m_w_q_b, w_o_b=m_w_o_b, kv_norm=m_kv_norm,
              w_kvf=m_w_kvf, b_f=m_b_f, w_up=m_w_up, conv_w=m_conv_w, conv_b=m_conv_b, w_down=m_w_down)
    vs = dict(norm_gains=v_norm_gains, w_qkv_a=v_w_qkv_a, w_o_a=v_w_o_a, w_q_b=v_w_q_b, w_o_b=v_w_o_b, kv_norm=v_kv_norm,
              w_kvf=v_w_kvf, b_f=v_b_f, w_up=v_w_up, conv_w=v_conv_w, conv_b=v_conv_b, w_down=v_w_down)

    small = jnp.concatenate([
        jnp.pad(norm_gains.reshape(16, 256), ((0, 0), (0, 1408 - 256))),
        jnp.pad(conv_w.reshape(12, 1408), ((0, 4), (0, 0)))], axis=0)
    big = [nm for nm in KIND if nm != "small"]
    half = {nm: ws[nm].astype(BF16) for nm in big}
    W = {nm: [None] * LAYERS[nm] for nm in big if nm != "w_kvf"}
    g_small = None
    groups = [("0a", [("w_qkv_a", 0), ("w_o_a", 0), ("small", 0)]), ("0b", [("w_up", 0)]), ("0c", [("w_down", 0)]),
              ("1", [("w_qkv_a", 1), ("w_o_a", 1), ("w_up", 1), ("w_down", 1)]),
              ("2", [("w_kvf", 0), ("w_q_b", 0), ("w_o_b", 0), ("w_up", 2), ("w_down", 2)]),
              ("3", [("w_q_b", 1), ("w_o_b", 1), ("w_up", 3), ("w_down", 3)])]
    for tag, group in groups:
        shards = [small if nm == "small" else half[nm] if nm == "w_kvf" else half[nm][i] for nm, i in group]
        got = _seq_gather(shards, [KIND[nm] for nm, _ in group], name="gather_layer" + tag, cid=1)
        for (nm, i), g in zip(group, got):
            if nm == "small":
                g_small = g
            elif nm == "w_kvf":
                W[nm] = g.transpose(1, 0, 2).reshape(D, 2 * D + 16)
            else:
                W[nm][i] = g.transpose(1, 0, 2).reshape(D, 3 * A_W) if nm == "w_qkv_a" else g
    gains = g_small[:, :16, :256].transpose(1, 0, 2).reshape(DEPTH, 4, 1, D)
    cw_full = g_small[:, 16:28, :].transpose(1, 0, 2).reshape(DEPTH, 3, 2 * D_FF)
    cb_full = conv_b.reshape(DEPTH, 1, 2 * D_FF)

    reducer = _Reducer(where)
    sq, dh = _fwd_bwd(x[0], loss_target[0], W, gains, cw_full, cb_full, kv_norm, b_f, reducer)
    loss = lax.psum(sq[0, 0] * (0.5 / D), ("x", "y", "c"))
    return _update(loss, dh[None], reducer.finish(), chip, ws, ms, vs)


def _fwd_bwd(h, target, W, gains, cw_full, cb_full, kv_norm, b_f, reduce):
    w_kv = W["w_kvf"][:, :2 * D]
    w_kvf_pad = jnp.pad(W["w_kvf"], ((0, 0), (0, 128 - 16)))
    w_f = w_kvf_pad[:, 2 * D:]
    kvn_g = kv_norm.reshape(1, D)
    bf_pad = jnp.pad(b_f, (0, 128 - 16)).reshape(1, 128)
    tabs = _rope_tables()
    headsum = _headsum_matrix()

    saved = []
    kv = zf = c_row = kvn = h_kv = None
    xn = _rms_fwd(h, gains[0][0], out_dtype=BF16, name="rms_in")
    for l in range(DEPTH):
        s = {"h": h}
        g = gains[l]
        s["xn"] = xn
        if l < N_A:
            qkv = _matmul(xn, W["w_qkv_a"][l], mode="nn", out_dtype=F32, name="mm_qkv", mnk=(T, 3 * A_W, D), tn=768)
            qkvp = _rope_fwd(qkv, tabs).reshape(3, 3, 2, T, 128)
            o_p, lse_p = _band_fwd(qkvp)
            att, o3, lse3 = _combine_fwd(o_p, lse_p)
            s.update(qkvp=qkvp, o3=o3, lse3=lse3, lse_p=lse_p, att=att)
            mix = _matmul(att, W["w_o_a"][l], mode="nn", out_dtype=F32, name="mm_oa", mnk=(T, D, A_W))
        else:
            j = l - N_A
            if l == N_A:
                h_kv = h
                kvn = _rms_fwd(h, kvn_g, out_dtype=BF16, name="rms_in")
                kv = _matmul(kvn, w_kv, mode="nn", out_dtype=BF16, name="mm_kv")
                zf = _matmul(kvn, w_f, mode="nn", out_dtype=F32, name="mm_f")
                cum = _gates_fwd(zf, bf_pad)[:, :16]
                c_row = cum.T.reshape(8, 2, T)
            q = _matmul(xn, W["w_q_b"][j], mode="nn", out_dtype=BF16, name="mm_qb", mnk=(T, D, D), alpha=HD ** -0.5)
            o = _fox_fwd(q, kv, c_row)
            s.update(q=q, o=o)
            mix = _matmul(o, W["w_o_b"][j], mode="nn", out_dtype=F32, name="mm_ob", mnk=(T, D, D))
        s["mix"] = mix
        h1, xn2 = _rms_res_in(mix, g[1], h, g[2], name="rms_res_in")
        a = _matmul(xn2, W["w_up"][l], mode="nn", out_dtype=F32, name="mm_up", mnk=(T, 2 * D_FF, D))
        u = _convgate_fwd(a, cw_full[l], cb_full[l])
        f = _matmul(u, W["w_down"][l], mode="nn", out_dtype=F32, name="mm_down", mnk=(T, D, D_FF), tm=1024, tk=D_FF)
        if l + 1 < DEPTH:
            h, xn = _rms_res_in(f, g[3], h1, gains[l + 1][0], name="rms_res_in")
        else:
            h = _rms_fwd(f, g[3], res=h1, out_dtype=F32, name="rms_res")
        s.update(h1=h1, xn2=xn2, a=a, u=u, f=f)
        saved.append(s)

    dh, sq = _loss_head(h, target)

    d_gains = [[None] * 4 for _ in range(DEPTH)]
    d_cw, d_cb = [None] * DEPTH, [None] * DEPTH
    fox_acc = None
    d_kvnorm = d_bf = token = df = None

    def dw(nm, a, b, **kw):
        return _matmul(a, b, mode="tn", out_dtype=BF16, name="mm_dw_" + nm, **kw)

    flush = getattr(reduce, "flush", lambda after: None)

    def slabs(full, width):
        return full.reshape(full.shape[0], N_CHIPS, width).transpose(1, 0, 2)

    for l in reversed(range(DEPTH)):
        s = saved[l]
        g = gains[l]
        if df is None:
            df, d_gains[l][3] = _rms_bwd(dh, s["f"], g[3], out_dtype=BF16, name="rms_bwd")
        du = _matmul(df, W["w_down"][l], mode="nt", out_dtype=F32, name="mm_down_dx", mnk=(T, D_FF, D), tn=256, after=token)
        g_down = dw("w_down", s["u"], df, tm=1408, tn=1024)
        da, d_cw[l], d_cb[l] = _convgate_bwd(s["a"], du, cw_full[l], cb_full[l])
        dxn2 = _matmul(da, W["w_up"][l], mode="nt", out_dtype=F32, name="mm_up_dx", mnk=(T, D, 2 * D_FF), tm=1024, tn=1024, tk=1408,
                       a_map=_halves_a)
        g_up = dw("w_up", s["xn2"], da, mnk=(D, 2 * D_FF, T), tn=1408, b_map=_halves_b)
        token = reduce([("w_down", l, g_down), ("w_up", l, g_up)], "ffn%d" % l)
        dh1, dmix, d_gains[l][2], d_gains[l][1] = _rms_bwd2(dxn2, s["h1"], g[2], dh, s["mix"], g[1], name="rms_bwd2")
        if l < N_A:
            datt = _matmul(dmix, W["w_o_a"][l], mode="nt", out_dtype=F32, name="mm_oa_dx", mnk=(T, A_W, D), tn=768, after=token)
            g_o = dw("w_o_a", s["att"], dmix, tm=768, tn=1024)
            do_p, dlt_p = _combine_bwd(datt, s["o3"], s["lse3"], headsum)
            dqkv = None
            for which, d in enumerate(_band_bwd(s["qkvp"], do_p, s["lse_p"], dlt_p)):
                dqkv = _rope_bwd(d, which, tabs, dqkv)
            dxn = _matmul(dqkv, W["w_qkv_a"][l], mode="nt", out_dtype=F32, name="mm_qkv_dx", mnk=(T, D, 3 * A_W), tm=1024, tn=1024, tk=3 * A_W,
                          after=[flush(dqkv)])
            g_qkv = dw("w_qkv_a", s["xn"], dqkv, tn=768)
            group = [("w_o_a", l, g_o), ("w_qkv_a", l, slabs(g_qkv, 576))]
        else:
            j = l - N_A
            do = _matmul(dmix, W["w_o_b"][j], mode="nt", out_dtype=BF16, name="mm_ob_dx", mnk=(T, D, D), after=token)
            g_o = dw("w_o_b", s["o"], dmix, tn=1024)
            dq, *fox_acc = _fox_bwd(s["q"], kv, do, c_row, fox_acc)
            dxn = _matmul(dq, W["w_q_b"][j], mode="nt", out_dtype=F32, name="mm_qb_dx", mnk=(T, D, D), after=[flush(dq)])
            g_q = dw("w_q_b", s["xn"], dq, tn=1024)
            group = [("w_o_b", j, g_o), ("w_q_b", j, g_q)]
        if l > 0 and l != N_A:
            dh, df, d_gains[l][0], d_gains[l - 1][3] = _rms_bwd2(dxn, s["h"], g[0], dh1, saved[l - 1]["f"], gains[l - 1][3],
                                                                 name="rms_bwd2")
        else:
            dh, d_gains[l][0] = _rms_bwd(dxn, s["h"], g[0], dres=dh1, out_dtype=F32, name="rms_bwd_res")
            df = None
        if l == N_A:
            dk, dv, dck = fox_acc
            dc16 = -dck[:, :2, :].reshape(16, T).T
            dzf, d_bf = _gates_bwd(jnp.pad(dc16, ((0, 0), (0, 128 - 16))), zf, bf_pad)
            dkvf = jnp.concatenate([dk.astype(BF16), dv.astype(BF16), dzf], axis=1)
            g_kvf = _matmul(kvn, dkvf, mode="tn", out_dtype=BF16, name="mm_kvf_dw", tm=512, tn=2 * D + 128)[:, :2 * D + 16]
            dkvn = _matmul(dkvf, w_kvf_pad, mode="nt", out_dtype=F32, name="mm_kvf_dx", tm=1024, tn=1024, tk=2 * D + 128)
            dh, d_kvnorm = _rms_bwd(dkvn, h_kv, kvn_g, dres=dh, out_dtype=F32, name="rms_bwd_res")
            group.append(("w_kvf", 0, slabs(g_kvf, 516)))
        token = reduce(group, "mix%d" % l)
    small_flat = jnp.concatenate([
        jnp.stack([jnp.stack(r) for r in d_gains]).reshape(-1),
        jnp.stack(d_cw).transpose(0, 2, 1, 3).reshape(-1),
        jnp.stack(d_cb).reshape(-1),
        d_kvnorm.reshape(-1), d_bf[0, :16]])
    small = jnp.pad(small_flat, (0, 2 * N_CHIPS * SMALL_ROWS * SMALL_W - small_flat.shape[0]))
    reduce([("small", 0, small.reshape(N_CHIPS, 2 * SMALL_ROWS, SMALL_W))], "small")
    return sq, dh


def _update(loss, grad_x, reduced, chip, ws, ms, vs):
    red_s = reduced.pop("small")
    buf_s = lax.dynamic_update_slice(jnp.zeros((2, N_CHIPS, SMALL_ROWS, SMALL_W), F32), red_s.reshape(2, 1, SMALL_ROWS, SMALL_W),
                                     (0, chip, 0, 0))
    (all_s,) = _allgather([buf_s], ["slab"], name="gather_small_grads")
    sflat = all_s.transpose(1, 0, 2, 3).reshape(-1)

    grads = {nm: r.reshape(ws[nm].shape) for nm, r in reduced.items()}
    o = 0
    g_gains_full = sflat[o:o + 16 * D].reshape(DEPTH, 4, D); o += 16 * D
    g_cw_full = sflat[o:o + 12 * 2 * D_FF].reshape(DEPTH, 3, 2 * D_FF); o += 12 * 2 * D_FF
    grads["conv_b"] = sflat[o:o + 4 * 2 * D_FF].reshape(DEPTH, 2 * D_FF); o += 4 * 2 * D_FF
    grads["kv_norm"] = sflat[o:o + D]; o += D
    grads["b_f"] = sflat[o:o + 16]
    grads["norm_gains"] = lax.dynamic_slice_in_dim(g_gains_full, chip * 256, 256, axis=2)
    grads["conv_w"] = lax.dynamic_slice_in_dim(g_cw_full, chip * 1408, 1408, axis=2)

    names = ["norm_gains", "w_qkv_a", "w_o_a", "w_q_b", "w_o_b", "kv_norm", "w_kvf", "b_f", "w_up", "conv_w", "conv_b", "w_down"]
    deltas, new_m, new_v = {}, {}, {}
    for nm in names:
        shp = ws[nm].shape
        two = (math.prod(shp[:-1]), shp[-1]) if len(shp) > 1 else (1, shp[0])
        d, m2, v2 = _adamw(ws[nm].reshape(two), ms[nm].reshape(two), vs[nm].reshape(two), grads[nm].reshape(two),
                           name="adamw_" + nm)
        deltas[nm], new_m[nm], new_v[nm] = d.reshape(shp), m2.reshape(shp), v2.reshape(shp)

    return (loss, grad_x, *[grads[nm] for nm in names], *[deltas[nm] for nm in names],
            *[new_m[nm] for nm in names], *[new_v[nm] for nm in names])
```

```python
import math

import jax
import jax.numpy as jnp
from jax import lax
from jax.experimental import pallas as pl
from jax.experimental.pallas import tpu as pltpu
from jax.experimental.pallas import tpu_sc as plsc

F32 = jnp.float32
BF16 = jnp.bfloat16
MESH = pl.DeviceIdType.MESH
ANY = pl.BlockSpec(memory_space=pl.ANY)

T = 2048
D = 1024
HD = 64
DEPTH = 4
N_A = 2
A_W = 768
GW = 256
DIL = (1, 4, 16)
BLK = 128
D_FF = 2816
ROPE_THETA = 500000.0
EPS = 1e-6
NEG = -1e30
N_CHIPS = 4
FQ = 256
CT = 128
VMEM_BIG = 48 * 1024 * 1024

ADAM_LR, ADAM_B1, ADAM_B2, ADAM_EPS, ADAM_WD, ADAM_STEP = 0.001, 0.9, 0.999, 1e-08, 0.01, 10

NN = (((1,), (0,)), ((), ()))
NT = (((1,), (1,)), ((), ()))
TN = (((0,), (0,)), ((), ()))


def _dot(a, b, dims):
    return lax.dot_general(a, b, dims, preferred_element_type=F32)


def _pick(dim, pref):
    if dim <= pref:
        return dim
    best = None
    for t in range(128, pref + 1, 128):
        if dim % t == 0:
            best = t
    assert best is not None, (dim, pref)
    return best


def _params(sem=None, vmem=None):
    kw = {}
    if sem is not None:
        kw["dimension_semantics"] = sem
    if vmem is not None:
        kw["vmem_limit_bytes"] = vmem
    return pltpu.CompilerParams(**kw)


def _matmul(a, b, *, mode, out_dtype, name, mnk=None, alpha=None, tm=2048, tn=512, tk=2048, a_map=None, b_map=None, after=None):
    if mnk is not None:
        M, N, K = mnk
    elif mode == "nn":
        (M, K), (_, N) = a.shape, b.shape
    elif mode == "nt":
        (M, K), (N, _) = a.shape, b.shape
    else:
        (K, M), (_, N) = a.shape, b.shape
    tm, tn, tk = _pick(M, tm), _pick(N, tn), _pick(K, tk)
    nk = K // tk
    dims = {"nn": NN, "nt": NT, "tn": TN}[mode]
    after = [t for t in (after or ()) if t is not None]
    n_in = 2 + len(after)

    def body(*refs):
        a_ref, b_ref = refs[0], refs[1]
        o_ref = refs[n_in]
        k = pl.program_id(2)

        def finish(r):
            if alpha is not None:
                r = r * alpha
            o_ref[...] = r.astype(out_dtype)

        def product():
            return _dot(a_ref[...], b_ref[...], dims)

        if nk == 1:
            finish(product())
            return
        acc_ref = refs[n_in + 1]

        @pl.when(k == 0)
        def _():
            acc_ref[...] = product()

        @pl.when((k > 0) & (k < nk - 1))
        def _():
            acc_ref[...] += product()

        @pl.when(k == nk - 1)
        def _():
            finish(acc_ref[...] + product())

    a_blk = (tk, tm) if mode == "tn" else (tm, tk)
    b_blk = (tn, tk) if mode == "nt" else (tk, tn)
    if a_map is not None:
        a_spec = pl.BlockSpec((None,) + a_blk, a_map(tm, tn, tk))
    elif mode == "tn":
        a_spec = pl.BlockSpec(a_blk, lambda i, j, k: (k, i))
    else:
        a_spec = pl.BlockSpec(a_blk, lambda i, j, k: (i, k))
    if b_map is not None:
        b_spec = pl.BlockSpec((None,) + b_blk, b_map(tm, tn, tk))
    elif mode == "nt":
        b_spec = pl.BlockSpec(b_blk, lambda i, j, k: (j, k))
    else:
        b_spec = pl.BlockSpec(b_blk, lambda i, j, k: (k, j))
    return pl.pallas_call(
        body,
        grid=(M // tm, N // tn, nk),
        in_specs=[a_spec, b_spec] + [ANY] * len(after),
        out_specs=pl.BlockSpec((tm, tn), lambda i, j, k: (i, j)),
        out_shape=jax.ShapeDtypeStruct((M, N), out_dtype),
        scratch_shapes=[pltpu.VMEM((tm, tn), F32)] if nk > 1 else [],
        compiler_params=_params(("parallel", "parallel", "arbitrary"), VMEM_BIG),
        name=name,
    )(a, b, *after)


def _rms_fwd(x, g, *, out_dtype, name, res=None, tr=256):
    n, d = x.shape

    def body(*refs):
        x_ref, g_ref = refs[0], refs[1]
        o_ref = refs[-1]
        xv = x_ref[...].astype(F32)
        y = xv * lax.rsqrt(jnp.mean(xv * xv, axis=-1, keepdims=True) + EPS) * g_ref[...]
        if res is not None:
            y = y + refs[2][...]
        o_ref[...] = y.astype(out_dtype)

    row = pl.BlockSpec((tr, d), lambda i: (i, 0))
    vec = pl.BlockSpec((1, d), lambda i: (0, 0))
    ins = [x, g] + ([] if res is None else [res])
    specs = [row, vec] + ([] if res is None else [row])
    return pl.pallas_call(
        body, grid=(n // tr,), in_specs=specs, out_specs=row,
        out_shape=jax.ShapeDtypeStruct((n, d), out_dtype),
        compiler_params=_params(("parallel",)), name=name,
    )(*ins)


def _rms_bwd(dy, x, g, *, out_dtype, name, dres=None, tr=256):
    n, d = x.shape

    def body(*refs):
        dy_ref, x_ref, g_ref = refs[0], refs[1], refs[2]
        dx_ref, dg_ref = refs[-2], refs[-1]
        xv = x_ref[...].astype(F32)
        dyv = dy_ref[...].astype(F32)
        rstd = lax.rsqrt(jnp.mean(xv * xv, axis=-1, keepdims=True) + EPS)
        xhat = xv * rstd
        dxh = dyv * g_ref[...]
        dx = rstd * (dxh - xhat * jnp.mean(dxh * xhat, axis=-1, keepdims=True))
        if dres is not None:
            dx = dx + refs[3][...]
        dx_ref[...] = dx.astype(out_dtype)

        @pl.when(pl.program_id(0) == 0)
        def _():
            dg_ref[...] = jnp.zeros_like(dg_ref)

        dg_ref[...] += jnp.sum(dyv * xhat, axis=0, keepdims=True)

    row = pl.BlockSpec((tr, d), lambda i: (i, 0))
    vec = pl.BlockSpec((1, d), lambda i: (0, 0))
    ins = [dy, x, g] + ([] if dres is None else [dres])
    specs = [row, row, vec] + ([] if dres is None else [row])
    return pl.pallas_call(
        body, grid=(n // tr,), in_specs=specs, out_specs=[row, vec],
        out_shape=[jax.ShapeDtypeStruct((n, d), out_dtype), jax.ShapeDtypeStruct((1, d), F32)],
        compiler_params=_params(("arbitrary",)), name=name,
    )(*ins)


def _rms_res_in(x, g_res, res, g_in, *, name, tr=512):
    n, d = x.shape

    def body(x_ref, gr_ref, r_ref, gi_ref, h_ref, n_ref):
        xv = x_ref[...].astype(F32)
        h = r_ref[...] + xv * lax.rsqrt(jnp.mean(xv * xv, axis=-1, keepdims=True) + EPS) * gr_ref[...]
        h_ref[...] = h
        n_ref[...] = (h * lax.rsqrt(jnp.mean(h * h, axis=-1, keepdims=True) + EPS) * gi_ref[...]).astype(BF16)

    row = pl.BlockSpec((tr, d), lambda i: (i, 0))
    vec = pl.BlockSpec((1, d), lambda i: (0, 0))
    return pl.pallas_call(
        body, grid=(n // tr,), in_specs=[row, vec, row, vec], out_specs=[row, row],
        out_shape=[jax.ShapeDtypeStruct((n, d), F32), jax.ShapeDtypeStruct((n, d), BF16)],
        compiler_params=_params(("parallel",), VMEM_BIG), name=name,
    )(x, g_res, res, g_in)


def _rms_bwd2(dy, x, g, dres, x2, g2, *, name, tr=512):
    n, d = x.shape

    def one(dyv, xv, gv):
        rstd = lax.rsqrt(jnp.mean(xv * xv, axis=-1, keepdims=True) + EPS)
        xhat = xv * rstd
        dxh = dyv * gv
        return rstd * (dxh - xhat * jnp.mean(dxh * xhat, axis=-1, keepdims=True)), jnp.sum(dyv * xhat, axis=0, keepdims=True)

    def body(dy_ref, x_ref, g_ref, r_ref, x2_ref, g2_ref, dx_ref, d2_ref, dg_ref, dg2_ref):
        dx, dg = one(dy_ref[...].astype(F32), x_ref[...].astype(F32), g_ref[...])
        dx = dx + r_ref[...]
        dx_ref[...] = dx
        d2, dg2 = one(dx, x2_ref[...].astype(F32), g2_ref[...])
        d2_ref[...] = d2.astype(BF16)

        @pl.when(pl.program_id(0) == 0)
        def _():
            dg_ref[...] = jnp.zeros_like(dg_ref)
            dg2_ref[...] = jnp.zeros_like(dg2_ref)

        dg_ref[...] += dg
        dg2_ref[...] += dg2

    row = pl.BlockSpec((tr, d), lambda i: (i, 0))
    vec = pl.BlockSpec((1, d), lambda i: (0, 0))
    return pl.pallas_call(
        body, grid=(n // tr,), in_specs=[row, row, vec, row, row, vec], out_specs=[row, row, vec, vec],
        out_shape=[jax.ShapeDtypeStruct((n, d), F32), jax.ShapeDtypeStruct((n, d), BF16),
                   jax.ShapeDtypeStruct((1, d), F32), jax.ShapeDtypeStruct((1, d), F32)],
        compiler_params=_params(("arbitrary",), VMEM_BIG), name=name,
    )(dy, x, g, dres, x2, g2)


def _loss_head(h, target, *, tr=256):
    n, d = h.shape

    def body(h_ref, t_ref, dh_ref, s_ref):
        err = h_ref[...] - t_ref[...]
        dh_ref[...] = err * (1.0 / d)

        @pl.when(pl.program_id(0) == 0)
        def _():
            s_ref[...] = jnp.zeros_like(s_ref)

        s_ref[...] += jnp.sum(err * err)

    row = pl.BlockSpec((tr, d), lambda i: (i, 0))
    acc = pl.BlockSpec((8, 128), lambda i: (0, 0))
    return pl.pallas_call(
        body, grid=(n // tr,), in_specs=[row, row], out_specs=[row, acc],
        out_shape=[jax.ShapeDtypeStruct((n, d), F32), jax.ShapeDtypeStruct((8, 128), F32)],
        compiler_params=_params(("arbitrary",)), name="loss_head",
    )(h, target)


def _rope_tables():
    pos = jnp.arange(T, dtype=F32)
    inv = ROPE_THETA ** (-jnp.arange(0, 16, 2, dtype=F32) / 16)
    ang = pos[:, None] * inv[None, :]
    cos, sin = jnp.cos(ang), jnp.sin(ang)
    one = jnp.ones((T, HD - 16), F32)
    zero8 = jnp.zeros((T, 8), F32)
    zero = jnp.zeros((T, HD - 16), F32)
    c = jnp.concatenate([cos, cos, one], axis=1)
    s1 = jnp.concatenate([zero8, sin, zero], axis=1)
    s2 = jnp.concatenate([-sin, zero8, zero], axis=1)
    c, s1, s2 = (jnp.concatenate([t, t], axis=1) for t in (c, s1, s2))
    scale = HD ** -0.5
    return (jnp.stack([c * scale, c, jnp.ones_like(c)]), jnp.stack([s1 * scale, s1, jnp.zeros_like(c)]),
            jnp.stack([s2 * scale, s2, jnp.zeros_like(c)]))


def _row_chunks(r):
    if r == 1:
        n = 4
        return [(slice(i * (T // n), (i + 1) * (T // n)),) * 2 for i in range(n)]
    per = T // r
    return [(pl.ds(j, per, stride=r), slice(j * per, (j + 1) * per)) for j in range(r)]


def _rope_fwd(qkv, tabs):
    def body(x_ref, c_ref, s1_ref, s2_ref, o_ref):
        g = lax.rem(lax.div(pl.program_id(0), 2), 3)
        for gi, r in enumerate(DIL):
            @pl.when(g == gi)
            def _(r=r):
                for tok, prm in _row_chunks(r):
                    x = x_ref[tok, :]
                    y = x * c_ref[tok, :] + pltpu.roll(x, 8, 1) * s1_ref[tok, :] + pltpu.roll(x, 120, 1) * s2_ref[tok, :]
                    o_ref[prm, :] = y.astype(BF16)

    tab = pl.BlockSpec((None, T, 128), lambda b: (lax.div(b, 6), 0, 0))
    return pl.pallas_call(
        body, grid=(18,), in_specs=[pl.BlockSpec((T, 128), lambda b: (0, b)), tab, tab, tab],
        out_specs=pl.BlockSpec((None, T, 128), lambda b: (b, 0, 0)), out_shape=jax.ShapeDtypeStruct((18, T, 128), BF16),
        compiler_params=_params(("parallel",)), name="rope_fwd",
    )(qkv, *tabs)


def _rope_bwd(d, which, tabs, out_buf):
    def body(d_ref, c_ref, s1_ref, s2_ref, *rest):
        o_ref, tok_ref = rest[-2], rest[-1]
        g = lax.div(pl.program_id(0), 2)
        for gi, r in enumerate(DIL):
            @pl.when(g == gi)
            def _(r=r):
                for tok, prm in _row_chunks(r):
                    tok_ref[tok, :] = d_ref[prm, :]
                for rows, _ in _row_chunks(1):
                    gx = tok_ref[rows, :]
                    y = gx * c_ref[rows, :] + pltpu.roll(gx * s1_ref[rows, :], 120, 1) + pltpu.roll(gx * s2_ref[rows, :], 8, 1)
                    o_ref[rows, :] = y.astype(BF16)

    tab = pl.BlockSpec((None, T, 128), lambda b: (which, 0, 0))
    ins = [d, *tabs] + ([] if out_buf is None else [out_buf])
    specs = [pl.BlockSpec((None, None, T, 128), lambda b: (lax.div(b, 2), lax.rem(b, 2), 0, 0)), tab, tab, tab]
    return pl.pallas_call(
        body, grid=(6,), in_specs=specs + ([] if out_buf is None else [ANY]),
        out_specs=pl.BlockSpec((T, 128), lambda b: (0, 6 * which + b)),
        out_shape=jax.ShapeDtypeStruct((T, 3 * A_W), BF16), scratch_shapes=[pltpu.VMEM((T, 128), F32)],
        input_output_aliases={} if out_buf is None else {4: 0},
        compiler_params=_params(("arbitrary",)), name="rope_bwd",
    )(*ins)


def _head_mask(x, lane_lo):
    lane = lax.broadcasted_iota(jnp.int32, x.shape, 1)
    keep = (lane < HD) if lane_lo else (lane >= HD)
    return jnp.where(keep, x.astype(F32), 0.0).astype(BF16)


def _band_scalars(g):
    b = pl.program_id(0)
    nbs = (T // BLK) // DIL[g]
    has_prev = jnp.where((b & (nbs - 1)) != 0, 1, 0)
    next_ok = jnp.where(((b + 1) & (nbs - 1)) != 0, 1, 0)
    return has_prev, next_ok


def _band_mask_q(has_prev):
    row = lax.broadcasted_iota(jnp.int32, (BLK, 2 * BLK), 0)
    col = lax.broadcasted_iota(jnp.int32, (BLK, 2 * BLK), 1)
    return ((col < BLK) & (col >= row) & (has_prev == 1)) | ((col >= BLK) & (col - BLK <= row))


def _band_mask_k(next_ok):
    row = lax.broadcasted_iota(jnp.int32, (2 * BLK, BLK), 0)
    col = lax.broadcasted_iota(jnp.int32, (2 * BLK, BLK), 1)
    return ((row < BLK) & (col <= row)) | ((row >= BLK) & (col >= row - BLK) & (next_ok == 1))


def _band_spec(step, which=None):
    nb = T // BLK
    at = {"cur": lambda b: b, "prev": lambda b: jnp.maximum(b - 1, 0), "next": lambda b: jnp.minimum(b + 1, nb - 1)}[step]
    if which is None:
        return pl.BlockSpec((3, 2, BLK, 128), lambda b: (0, 0, at(b), 0))
    return pl.BlockSpec((None, 3, 2, BLK, 128), lambda b: (which, 0, 0, at(b), 0))


def _band_fwd(qkv):
    nb = T // BLK

    def body(q_ref, kc_ref, kp_ref, vc_ref, vp_ref, o_ref, l_ref):
        lane = lax.broadcasted_iota(jnp.int32, (BLK, 128), 1)
        for g in range(3):
            has_prev, _ = _band_scalars(g)
            mask = _band_mask_q(has_prev)
            for p in range(2):
                qp = q_ref[g, p]
                kcat = jnp.concatenate([kp_ref[g, p], kc_ref[g, p]], axis=0)
                vcat = jnp.concatenate([vp_ref[g, p], vc_ref[g, p]], axis=0)
                o_acc = jnp.zeros((BLK, 128), F32)
                lse = jnp.zeros((BLK, 128), F32)
                for e in range(2):
                    s = _dot(_head_mask(qp, e == 0), kcat, NT)
                    s = jnp.where(mask, s, NEG)
                    m = jnp.max(s, axis=-1, keepdims=True)
                    pr = jnp.exp(s - m)
                    l = jnp.sum(pr, axis=-1, keepdims=True)
                    o_acc = o_acc + _dot(pr.astype(BF16), _head_mask(vcat, e == 0), NN) / l
                    lse = jnp.where((lane < HD) if e == 0 else (lane >= HD), m + jnp.log(l), lse)
                o_ref[g, p] = o_acc
                l_ref[g, p] = lse

    out = _band_spec("cur")
    shp = jax.ShapeDtypeStruct((3, 2, T, 128), F32)
    return pl.pallas_call(
        body, grid=(nb,),
        in_specs=[_band_spec("cur", 0), _band_spec("cur", 1), _band_spec("prev", 1), _band_spec("cur", 2), _band_spec("prev", 2)],
        out_specs=[out, out], out_shape=[shp, shp],
        compiler_params=_params(("parallel",)), name="band_fwd",
    )(qkv, qkv, qkv, qkv, qkv)


def _band_bwd(qkv, do, lse, dlt):
    nb = T // BLK

    def body(qc_ref, qn_ref, kc_ref, kp_ref, vc_ref, vp_ref, doc_ref, don_ref, lc_ref, ln_ref, dc_ref, dn_ref,
             dq_ref, dk_ref, dv_ref):
        for g in range(3):
            has_prev, next_ok = _band_scalars(g)
            mask_q = _band_mask_q(has_prev)
            mask_k = _band_mask_k(next_ok)
            for p in range(2):
                qc, qn = qc_ref[g, p], qn_ref[g, p]
                doc, don = doc_ref[g, p], don_ref[g, p]
                kc, vc = kc_ref[g, p], vc_ref[g, p]
                kcat = jnp.concatenate([kp_ref[g, p], kc], axis=0)
                vcat = jnp.concatenate([vp_ref[g, p], vc], axis=0)
                qcat = jnp.concatenate([qc, qn], axis=0)
                docat = jnp.concatenate([doc, don], axis=0)
                dq = jnp.zeros((BLK, 128), F32)
                dk = jnp.zeros((BLK, 128), F32)
                dv = jnp.zeros((BLK, 128), F32)
                for e in range(2):
                    lo = e == 0
                    col = slice(HD * e, HD * e + 1)
                    lse_c, lse_n = lc_ref[g, p, :, col], ln_ref[g, p, :, col]
                    dl_c, dl_n = dc_ref[g, p, :, col], dn_ref[g, p, :, col]
                    s = jnp.where(mask_q, _dot(_head_mask(qc, lo), kcat, NT), NEG)
                    pr = jnp.exp(s - lse_c)
                    dp = _dot(_head_mask(doc, lo), vcat, NT)
                    ds = pr * (dp - dl_c)
                    dq = dq + _dot(ds.astype(BF16), _head_mask(kcat, lo), NN)
                    qm, dom = _head_mask(qcat, lo), _head_mask(docat, lo)
                    s2 = jnp.where(mask_k, _dot(qm, kc, NT), NEG)
                    p2 = jnp.exp(s2 - jnp.concatenate([lse_c, lse_n], axis=0))
                    dv = dv + _dot(p2.astype(BF16), dom, TN)
                    dp2 = _dot(dom, vc, NT)
                    ds2 = p2 * (dp2 - jnp.concatenate([dl_c, dl_n], axis=0))
                    dk = dk + _dot(ds2.astype(BF16), qm, TN)
                dq_ref[g, p] = dq
                dk_ref[g, p] = dk
                dv_ref[g, p] = dv

    cur, nxt = _band_spec("cur"), _band_spec("next")
    shp = jax.ShapeDtypeStruct((3, 2, T, 128), F32)
    return pl.pallas_call(
        body, grid=(nb,),
        in_specs=[_band_spec("cur", 0), _band_spec("next", 0), _band_spec("cur", 1), _band_spec("prev", 1),
                  _band_spec("cur", 2), _band_spec("prev", 2), cur, nxt, cur, nxt, cur, nxt],
        out_specs=[cur, cur, cur], out_shape=[shp, shp, shp],
        compiler_params=_params(("parallel",)), name="band_bwd",
    )(qkv, qkv, qkv, qkv, qkv, qkv, do, do, lse, lse, dlt, dlt)


def _split3(x):
    hi = x.astype(BF16)
    r = x - hi.astype(F32)
    mid = r.astype(BF16)
    lo = (r - mid.astype(F32)).astype(BF16)
    return hi, mid, lo


def _dot3(x, m, dims=NN):
    hi, mid, lo = _split3(x)
    return _dot(hi, m, dims) + _dot(mid, m, dims) + _dot(lo, m, dims)


def _combine_weights(lses):
    l0, l1, l2 = lses
    m = jnp.maximum(jnp.maximum(l0, l1), l2)
    e = [jnp.exp(l0 - m), jnp.exp(l1 - m), jnp.exp(l2 - m)]
    inv = 1.0 / (e[0] + e[1] + e[2])
    return [ei * inv for ei in e]


CR = 256


def _combine_fwd(o, lse):
    def body(o_ref, l_ref, att_ref, o3_ref, l3_ref):
        for g, r in enumerate(DIL):
            for p in range(2):
                for tok, prm in _row_chunks(r):
                    o3_ref[g, p, tok, :] = o_ref[g, p, prm, :]
                    l3_ref[g, p, tok, :] = l_ref[g, p, prm, :]
        for i in range(T // CR):
            rows = slice(i * CR, (i + 1) * CR)
            for p in range(2):
                alpha = _combine_weights([l3_ref[g, p, rows, :] for g in range(3)])
                for g in range(3):
                    att_ref[rows, g * GW + p * 128: g * GW + (p + 1) * 128] = (o3_ref[g, p, rows, :] * alpha[g]).astype(BF16)

    shp = jax.ShapeDtypeStruct((3, 2, T, 128), F32)
    return pl.pallas_call(
        body, out_shape=[jax.ShapeDtypeStruct((T, A_W), BF16), shp, shp],
        compiler_params=_params(vmem=VMEM_BIG), name="combine_fwd",
    )(o, lse)


def _combine_bwd(datt, o3, l3, headsum):
    def body(d_ref, o_ref, l_ref, hs_ref, do_ref, dl_ref, tdo_ref, tdl_ref):
        hs = hs_ref[...]
        for p in range(2):
            for i in range(T // CR):
                rows = slice(i * CR, (i + 1) * CR)
                alpha = _combine_weights([l_ref[g, p, rows, :] for g in range(3)])
                total = jnp.zeros((CR, 128), F32)
                for g in range(3):
                    dg = d_ref[rows, g * GW + p * 128: g * GW + (p + 1) * 128]
                    tdo_ref[g, rows, :] = dg * alpha[g]
                    total = total + alpha[g] * _dot3(dg * o_ref[g, p, rows, :], hs)
                for g in range(3):
                    tdl_ref[g, rows, :] = alpha[g] * total
            for g, r in enumerate(DIL):
                for tok, prm in _row_chunks(r):
                    do_ref[g, p, prm, :] = tdo_ref[g, tok, :].astype(BF16)
                    dl_ref[g, p, prm, :] = tdl_ref[g, tok, :]

    return pl.pallas_call(
        body, out_shape=[jax.ShapeDtypeStruct((3, 2, T, 128), BF16), jax.ShapeDtypeStruct((3, 2, T, 128), F32)],
        scratch_shapes=[pltpu.VMEM((3, T, 128), F32), pltpu.VMEM((3, T, 128), F32)],
        compiler_params=_params(vmem=VMEM_BIG), name="combine_bwd",
    )(datt, o3, l3, headsum)


def _fox_scores(qm, k_ref, ck_ref, e, i, n):
    s = _dot(qm, k_ref[0:n, :], NT) - ck_ref[0, e:e + 1, 0:n]
    row = lax.broadcasted_iota(jnp.int32, (FQ, FQ), 0)
    col = lax.broadcasted_iota(jnp.int32, (FQ, FQ), 1)
    diag = jnp.where(col <= row, s[:, n - FQ:], NEG)
    m = jnp.max(diag, axis=-1, keepdims=True)
    if i == 0:
        pr = jnp.exp(diag - m)
        return pr, jnp.sum(pr, axis=-1, keepdims=True)
    past = s[:, :n - FQ]
    m = jnp.maximum(m, jnp.max(past, axis=-1, keepdims=True))
    p_past, p_diag = jnp.exp(past - m), jnp.exp(diag - m)
    l = jnp.sum(p_past, axis=-1, keepdims=True) + jnp.sum(p_diag, axis=-1, keepdims=True)
    return jnp.concatenate([p_past, p_diag], axis=1), l


def _fox_fwd(q, kv, c_row):
    def body(q_ref, k_ref, v_ref, cr_ref, o_ref, vm_ref):
        for e in range(2):
            vm_ref[e] = _head_mask(v_ref[...], e == 0)
        for i in range(T // FQ):
            n = (i + 1) * FQ
            rows = slice(i * FQ, n)
            acc = jnp.zeros((FQ, 128), F32)
            for e in range(2):
                qm = _head_mask(q_ref[rows, :], e == 0)
                pr, l = _fox_scores(qm, k_ref, cr_ref, e, i, n)
                acc = acc + _dot(pr.astype(BF16), vm_ref[e, 0:n, :], NN) / l
            o_ref[rows, :] = acc.astype(BF16)

    pair = pl.BlockSpec((T, 128), lambda p: (0, p))
    return pl.pallas_call(
        body, grid=(D // 128,),
        in_specs=[pair, pair, pl.BlockSpec((T, 128), lambda p: (0, D // 128 + p)), pl.BlockSpec((1, 2, T), lambda p: (p, 0, 0))],
        out_specs=pair, out_shape=jax.ShapeDtypeStruct((T, D), BF16),
        scratch_shapes=[pltpu.VMEM((2, T, 128), BF16)],
        compiler_params=_params(("parallel",), VMEM_BIG), name="fox_fwd",
    )(q, kv, kv, c_row)


def _fox_bwd(q, kv, do, c_row, init):
    def body(q_ref, k_ref, v_ref, do_ref, cr_ref, *rest):
        dq_ref, dk_ref, dv_ref, dck_ref, km_ref = rest[-5:]
        for o_ref, i_ref in zip((dk_ref, dv_ref, dck_ref), rest[:-5] or (None,) * 3):
            o_ref[...] = jnp.zeros_like(o_ref) if i_ref is None else i_ref[...]
        for e in range(2):
            km_ref[e] = _head_mask(k_ref[...], e == 0)
        for i in range(T // FQ):
            n = (i + 1) * FQ
            rows = slice(i * FQ, n)
            dq = jnp.zeros((FQ, 128), F32)
            dk = jnp.zeros((n, 128), F32)
            dv = jnp.zeros((n, 128), F32)
            for e in range(2):
                qm = _head_mask(q_ref[rows, :], e == 0)
                dom = _head_mask(do_ref[rows, :], e == 0)
                pr, l = _fox_scores(qm, k_ref, cr_ref, e, i, n)
                pr = pr * (1.0 / l)
                dp = _dot(dom, v_ref[0:n, :], NT)
                ds = pr * (dp - jnp.sum(pr * dp, axis=-1, keepdims=True))
                dsb = ds.astype(BF16)
                dq = dq + _dot(dsb, km_ref[e, 0:n, :], NN)
                dk = dk + _dot(dsb, qm, TN)
                dv = dv + _dot(pr.astype(BF16), dom, TN)
                dck_ref[0, e:e + 1, 0:n] += jnp.sum(ds, axis=0, keepdims=True)
            dk_ref[0:n, :] += dk
            dv_ref[0:n, :] += dv
            dq_ref[rows, :] = (dq * HD ** -0.5).astype(BF16)

    pair = pl.BlockSpec((T, 128), lambda p: (0, p))
    ck = pl.BlockSpec((1, 8, T), lambda p: (p, 0, 0))
    return pl.pallas_call(
        body, grid=(D // 128,),
        in_specs=[pair, pair, pl.BlockSpec((T, 128), lambda p: (0, D // 128 + p)), pair,
                  pl.BlockSpec((1, 2, T), lambda p: (p, 0, 0))] + ([] if init is None else [pair, pair, ck]),
        out_specs=[pair, pair, pair, ck],
        out_shape=[jax.ShapeDtypeStruct((T, D), BF16), jax.ShapeDtypeStruct((T, D), F32), jax.ShapeDtypeStruct((T, D), F32),
                   jax.ShapeDtypeStruct((D // 128, 8, T), F32)],
        scratch_shapes=[pltpu.VMEM((2, T, 128), BF16)],
        compiler_params=_params(("parallel",), VMEM_BIG), name="fox_bwd",
    )(q, kv, kv, do, c_row, *(init or ()))


def _tri(lower):
    r = lax.broadcasted_iota(jnp.int32, (BLK, BLK), 0)
    c = lax.broadcasted_iota(jnp.int32, (BLK, BLK), 1)
    return jnp.where((c <= r) if lower else (c >= r), 1.0, 0.0).astype(BF16)


def _gates_fwd(z, b):
    def body(z_ref, b_ref, c_ref):
        tri = _tri(True)
        carry = jnp.zeros((1, 128), F32)
        for i in range(T // BLK):
            rows = slice(i * BLK, (i + 1) * BLK)
            x = z_ref[rows, :] + b_ref[...]
            logf = jnp.minimum(x, 0.0) - jnp.log(1.0 + jnp.exp(-jnp.abs(x)))
            hi, mid, lo = _split3(logf)
            y = _dot(tri, hi, NN) + _dot(tri, mid, NN) + _dot(tri, lo, NN) + carry
            c_ref[rows, :] = y
            carry = y[BLK - 1:BLK, :]

    return pl.pallas_call(body, out_shape=jax.ShapeDtypeStruct((T, 128), F32), name="gates_fwd")(z, b)


def _gates_bwd(dc, z, b):
    def body(dc_ref, z_ref, b_ref, dz_ref, db_ref):
        tri = _tri(False)
        carry = jnp.zeros((1, 128), F32)
        db = jnp.zeros((1, 128), F32)
        for i in reversed(range(T // BLK)):
            rows = slice(i * BLK, (i + 1) * BLK)
            hi, mid, lo = _split3(dc_ref[rows, :])
            dlogf = _dot(tri, hi, NN) + _dot(tri, mid, NN) + _dot(tri, lo, NN) + carry
            carry = dlogf[0:1, :]
            x = z_ref[rows, :] + b_ref[...]
            dz = dlogf / (1.0 + jnp.exp(x))
            dz_ref[rows, :] = dz.astype(BF16)
            db = db + jnp.sum(dz, axis=0, keepdims=True)
        db_ref[...] = db

    return pl.pallas_call(
        body, out_shape=[jax.ShapeDtypeStruct((T, 128), BF16), jax.ShapeDtypeStruct((1, 128), F32)], name="gates_bwd",
    )(dc, z, b)


def _conv_pair(a_refs, cw_refs, cb_refs):
    row = lax.broadcasted_iota(jnp.int32, (T, CT), 0)
    outs = []
    for a_ref, cw_ref, cb_ref in zip(a_refs, cw_refs, cb_refs):
        z = a_ref[...]
        z1 = jnp.where(row >= 1, pltpu.roll(z, 1, 0), 0.0)
        z2 = jnp.where(row >= 2, pltpu.roll(z, 2, 0), 0.0)
        y = cw_ref[2:3, :] * z + cw_ref[1:2, :] * z1 + cw_ref[0:1, :] * z2 + cb_ref[...]
        outs.append((y, z, z1, z2))
    return outs


_GELU_K = math.sqrt(2.0 / math.pi)
N_CT = D_FF // CT


def _conv_specs():
    def at(rows, off):
        return pl.BlockSpec((rows, CT), lambda j: (0, j + off))
    return [at(T, 0), at(T, N_CT), at(3, 0), at(3, N_CT), at(1, 0), at(1, N_CT)]


def _convgate_fwd(a, cw, cb):
    def body(ag_ref, av_ref, wg_ref, wv_ref, bg_ref, bv_ref, u_ref):
        (g, _, _, _), (v, _, _, _) = _conv_pair((ag_ref, av_ref), (wg_ref, wv_ref), (bg_ref, bv_ref))
        th = jnp.tanh(_GELU_K * (g + 0.044715 * g * g * g))
        u_ref[...] = (0.5 * g * (1.0 + th) * v).astype(BF16)

    return pl.pallas_call(
        body, grid=(N_CT,), in_specs=_conv_specs(),
        out_specs=pl.BlockSpec((T, CT), lambda j: (0, j)), out_shape=jax.ShapeDtypeStruct((T, D_FF), BF16),
        compiler_params=_params(("parallel",), VMEM_BIG), name="convgate_fwd",
    )(a, a, cw, cw, cb, cb)


def _convgate_bwd(a, du, cw, cb):
    def body(ag_ref, av_ref, wg_ref, wv_ref, bg_ref, bv_ref, du_ref, da_ref, dcw_ref, dcb_ref):
        (g, gz, gz1, gz2), (v, vz, vz1, vz2) = _conv_pair((ag_ref, av_ref), (wg_ref, wv_ref), (bg_ref, bv_ref))
        du = du_ref[...].astype(F32)
        th = jnp.tanh(_GELU_K * (g + 0.044715 * g * g * g))
        gelu = 0.5 * g * (1.0 + th)
        dgelu = 0.5 * (1.0 + th) + 0.5 * g * (1.0 - th * th) * _GELU_K * (1.0 + 3 * 0.044715 * g * g)
        row = lax.broadcasted_iota(jnp.int32, (T, CT), 0)
        for h, (d, z, z1, z2, w_ref) in enumerate(((du * v * dgelu, gz, gz1, gz2, wg_ref), (du * gelu, vz, vz1, vz2, wv_ref))):
            d1 = jnp.where(row < T - 1, pltpu.roll(d, T - 1, 0), 0.0)
            d2 = jnp.where(row < T - 2, pltpu.roll(d, T - 2, 0), 0.0)
            da_ref[h] = (w_ref[2:3, :] * d + w_ref[1:2, :] * d1 + w_ref[0:1, :] * d2).astype(BF16)
            dcw_ref[h, 0:1, :] = jnp.sum(d * z2, axis=0, keepdims=True)
            dcw_ref[h, 1:2, :] = jnp.sum(d * z1, axis=0, keepdims=True)
            dcw_ref[h, 2:3, :] = jnp.sum(d * z, axis=0, keepdims=True)
            dcb_ref[h] = jnp.sum(d, axis=0, keepdims=True)

    def both(rows):
        return pl.BlockSpec((2, rows, CT), lambda j: (0, 0, j))

    return pl.pallas_call(
        body, grid=(N_CT,),
        in_specs=_conv_specs() + [pl.BlockSpec((T, CT), lambda j: (0, j))],
        out_specs=[both(T), both(3), both(1)],
        out_shape=[jax.ShapeDtypeStruct((2, T, D_FF), BF16), jax.ShapeDtypeStruct((2, 3, D_FF), F32),
                   jax.ShapeDtypeStruct((2, 1, D_FF), F32)],
        compiler_params=_params(("parallel",), VMEM_BIG), name="convgate_bwd",
    )(a, a, cw, cw, cb, cb, du)


def _halves_a(tm, tn, tk):
    per = D_FF // tk
    return lambda i, j, k: (lax.div(k, per), i, lax.rem(k, per))


def _halves_b(tm, tn, tk):
    per = D_FF // tn
    return lambda i, j, k: (lax.div(j, per), k, lax.rem(j, per))


def _adamw(w, m, v, g, *, name):
    r, c = w.shape
    tr = r
    if r * c > 256 * 1024:
        for cand in range(8, r, 8):
            if r % cand == 0 and cand * c <= 256 * 1024:
                tr = cand

    def body(w_ref, m_ref, v_ref, g_ref, d_ref, nm_ref, nv_ref):
        gv = g_ref[...]
        mn = ADAM_B1 * m_ref[...] + (1.0 - ADAM_B1) * gv
        vn = ADAM_B2 * v_ref[...] + (1.0 - ADAM_B2) * (gv * gv)
        m_hat = mn * (1.0 / (1.0 - ADAM_B1 ** ADAM_STEP))
        v_hat = vn * (1.0 / (1.0 - ADAM_B2 ** ADAM_STEP))
        d_ref[...] = -ADAM_LR * (m_hat / (jnp.sqrt(v_hat) + ADAM_EPS) + ADAM_WD * w_ref[...])
        nm_ref[...] = mn
        nv_ref[...] = vn

    blk = pl.BlockSpec((tr, c), lambda i: (i, 0))
    shp = jax.ShapeDtypeStruct((r, c), F32)
    return pl.pallas_call(
        body, grid=(r // tr,), in_specs=[blk] * 4, out_specs=[blk] * 3, out_shape=[shp] * 3,
        compiler_params=_params(("parallel",)), name=name,
    )(w, m, v, g)


def _place():
    x, y, c = lax.axis_index("x"), lax.axis_index("y"), lax.axis_index("c")
    chips = [(1 - x, y), (x, 1 - y), (1 - x, 1 - y)]
    return x, y, c, chips


def _window(ref, kind, s, half=None):
    lead = () if half is None else (half,)
    b, c = ref.shape[-2], ref.shape[-1]
    if kind == "col":
        return ref.at[lead + (slice(None), slice(None), pl.ds(s * (c // N_CHIPS), c // N_CHIPS))]
    if kind == "row":
        return ref.at[lead + (slice(None), pl.ds(s * (b // N_CHIPS), b // N_CHIPS), slice(None))]
    return ref.at[lead + (s,)]


def _allgather(tensors, kinds, *, name):
    n = len(tensors)

    def body(*refs):
        bufs = refs[n:2 * n]
        send, recv = refs[2 * n:]
        x, y, c, chips = _place()
        me = 2 * x + y
        sib = (x, y, 1 - c)

        def rcopy(i, k, win, to):
            return pltpu.make_async_remote_copy(src_ref=win, dst_ref=win, send_sem=send.at[i * 6 + k], recv_sem=recv.at[i * 6 + k],
                                                device_id=to, device_id_type=MESH)

        started = []
        for i in range(n):
            for k, (px, py) in enumerate(chips):
                cp = rcopy(i, k, _window(bufs[i], kinds[i], me, c), (px, py, c))
                cp.start()
                started.append(cp)
        for i in range(n):
            for k, (px, py) in enumerate(chips):
                landed = _window(bufs[i], kinds[i], 2 * px + py, c)
                rcopy(i, k, landed, (px, py, c)).wait_recv()
                fw = rcopy(i, 3 + k, landed, sib)
                fw.start()
                started.append(fw)
        for i in range(n):
            for k, (px, py) in enumerate(chips):
                rcopy(i, 3 + k, _window(bufs[i], kinds[i], 2 * px + py, 1 - c), sib).wait_recv()
        for cp in started:
            cp.wait_send()

    return pl.pallas_call(
        body, in_specs=[ANY] * n, out_specs=[ANY] * n,
        out_shape=[jax.ShapeDtypeStruct(t.shape, t.dtype) for t in tensors],
        scratch_shapes=[pltpu.SemaphoreType.DMA((6 * n,)), pltpu.SemaphoreType.DMA((6 * n,))],
        input_output_aliases={i: i for i in range(n)},
        name=name,
    )(*tensors)


def _rows_tile(rows, cols, sub):
    best = None
    for t in range(sub, rows + 1, sub):
        if rows % t == 0 and t * cols <= 512 * 1024:
            best = t
    return rows if best is None else best


def _sequencer(name, cid, n_sems, peers_of, body):
    @pl.kernel(mesh=plsc.ScalarSubcoreMesh(axis_name="seq", num_cores=1), name=name,
               scratch_types=(pltpu.SemaphoreType.DMA((n_sems,)), pltpu.SemaphoreType.DMA((n_sems,))),
               compiler_params=pltpu.CompilerParams(collective_id=cid))
    def launch(send, recv):
        x, y, c, chips = _place()
        peers = peers_of(x, y, c, chips)
        barrier = pltpu.get_barrier_semaphore()
        for peer in peers:
            pl.semaphore_signal(barrier, inc=1, device_id=peer, device_id_type=MESH)
        pl.semaphore_wait(barrier, len(peers))
        body(send, recv)

    launch()


def _half_of_full(ref, kind, h):
    if kind == "col":
        b = ref.shape[0]
        return ref.at[pl.ds(h * (b // 2), b // 2), :]
    if kind == "row":
        c = ref.shape[1]
        return ref.at[:, pl.ds(h * (c // 2), c // 2)]
    b = ref.shape[1]
    return ref.at[:, pl.ds(h * (b // 2), b // 2), :]


def _half_shape(full, kind):
    if kind == "col":
        return (full[0] // 2, full[1])
    if kind == "row":
        return (full[0], full[1] // 2)
    return (full[0], full[1] // 2, full[2])


def _win_of_half(ref, kind, s):
    if kind == "col":
        c = ref.shape[1]
        return ref.at[:, pl.ds(s * (c // N_CHIPS), c // N_CHIPS)]
    if kind == "row":
        b = ref.shape[0]
        return ref.at[pl.ds(s * (b // N_CHIPS), b // N_CHIPS), :]
    return ref.at[s]


def _win_shape(half, kind):
    if kind == "col":
        return (half[0], half[1] // N_CHIPS)
    if kind == "row":
        return (half[0] // N_CHIPS, half[1])
    return half[1:]


def _seq_swap(parts, kinds, *, name):
    n = len(parts)
    srcs = [jax.new_ref(p, memory_space=pltpu.MemorySpace.HBM) for p in parts]
    outs = [jax.empty_ref(jax.ShapeDtypeStruct(_half_shape(p.shape, k), p.dtype), memory_space=pltpu.MemorySpace.HBM)
            for p, k in zip(parts, kinds)]

    def body(send, recv):
        x, y, c, _ = _place()
        cps = []
        for i in range(n):
            cp = pltpu.make_async_remote_copy(src_ref=_half_of_full(srcs[i], kinds[i], 1 - c), dst_ref=outs[i], send_sem=send.at[i],
                                              recv_sem=recv.at[i], device_id=(x, y, 1 - c), device_id_type=MESH)
            cp.start()
            cps.append(cp)
        for cp in cps:
            cp.wait()

    _sequencer(name, 2, n, lambda x, y, c, chips: [(x, y, 1 - c)], body)
    return [o[...] for o in outs]


def _seq_scatter(halves, kinds, *, name):
    n = len(halves)
    srcs = [jax.new_ref(h, memory_space=pltpu.MemorySpace.HBM) for h in halves]
    outs = [jax.empty_ref(jax.ShapeDtypeStruct((3,) + _win_shape(h.shape, k), h.dtype), memory_space=pltpu.MemorySpace.HBM)
            for h, k in zip(halves, kinds)]

    def body(send, recv):
        x, y, c, chips = _place()
        cps = []
        for i in range(n):
            for k, (px, py) in enumerate(chips):
                cp = pltpu.make_async_remote_copy(src_ref=_win_of_half(srcs[i], kinds[i], 2 * px + py), dst_ref=outs[i].at[k],
                                                  send_sem=send.at[3 * i + k], recv_sem=recv.at[3 * i + k],
                                                  device_id=(px, py, c), device_id_type=MESH)
                cp.start()
                cps.append(cp)
        for cp in cps:
            cp.wait()

    _sequencer(name, 3, 3 * n, lambda x, y, c, chips: [(px, py, c) for px, py in chips], body)
    return [o[...] for o in outs]


def _add_half(g, p, kind, where, after, *, name):
    if kind == "slab":
        s, b2, c = p.shape
        tr = _rows_tile(b2, c, 16)
        nr = b2 // tr
        grid = (s, nr)
        g_spec = pl.BlockSpec((None, tr, c), lambda i, r, w: (i, w[1] * nr + r, 0))
        p_spec = pl.BlockSpec((None, tr, c), lambda i, r, w: (i, r, 0))
    elif kind == "col":
        b2, c = p.shape
        tr = _rows_tile(b2, c, 16)
        nr = b2 // tr
        grid = (1, nr)
        g_spec = pl.BlockSpec((tr, c), lambda i, r, w: (w[1] * nr + r, 0))
        p_spec = pl.BlockSpec((tr, c), lambda i, r, w: (r, 0))
    else:
        b, c2 = p.shape
        tr = _rows_tile(b, c2, 16)
        grid = (1, b // tr)
        g_spec = pl.BlockSpec((tr, c2), lambda i, r, w: (r, w[1]))
        p_spec = pl.BlockSpec((tr, c2), lambda i, r, w: (r, 0))

    def body(w_ref, g_ref, p_ref, *rest):
        o_ref = rest[-1]
        o_ref[...] = (g_ref[...].astype(F32) + p_ref[...].astype(F32)).astype(o_ref.dtype)

    extra = [] if after is None else [after]
    return pl.pallas_call(
        body,
        grid_spec=pltpu.PrefetchScalarGridSpec(num_scalar_prefetch=1, grid=grid, in_specs=[g_spec, p_spec] + [ANY] * len(extra),
                                               out_specs=p_spec),
        out_shape=jax.ShapeDtypeStruct(p.shape, g.dtype),
        compiler_params=_params(("parallel", "parallel")), name=name,
    )(where, g, p, *extra)


def _sum_chips(r, h, kind, where, layer, layers, out_buf, after, *, name):
    _, br, cr = r.shape
    tr = _rows_tile(br, cr, 16)
    nr = br // tr
    if kind == "col":
        h_spec = pl.BlockSpec((tr, cr), lambda j, w: (j, w[0]))
        o_shape, o_spec = (layers, 2 * br, cr), pl.BlockSpec((None, tr, cr), lambda j, w: (layer, w[1] * nr + j, 0))
    elif kind == "row":
        h_spec = pl.BlockSpec((tr, cr), lambda j, w: (w[0] * nr + j, 0))
        o_shape, o_spec = (layers, br, 2 * cr), pl.BlockSpec((None, tr, cr), lambda j, w: (layer, j, w[1]))
    else:
        h_spec = pl.BlockSpec((None, tr, cr), lambda j, w: (w[0], j, 0))
        o_shape, o_spec = (layers, 2 * br, cr), pl.BlockSpec((None, tr, cr), lambda j, w: (layer, w[1] * nr + j, 0))

    def body(w_ref, h_ref, r0_ref, r1_ref, r2_ref, *rest):
        o_ref, t_ref = rest[-2], rest[-1]
        o_ref[...] = ((h_ref[...].astype(F32) + r0_ref[...].astype(F32)) + r1_ref[...].astype(F32)) + r2_ref[...].astype(F32)
        t_ref[...] = jnp.zeros_like(t_ref)

    def slot(k):
        return pl.BlockSpec((None, tr, cr), lambda j, w: (k, j, 0))

    ins, specs, alias = [h, r, r, r], [h_spec, slot(0), slot(1), slot(2)], {}
    if after is not None:
        ins.append(after)
        specs.append(ANY)
    if out_buf is not None:
        alias = {1 + len(ins): 0}
        ins.append(out_buf)
        specs.append(ANY)
    return pl.pallas_call(
        body,
        grid_spec=pltpu.PrefetchScalarGridSpec(num_scalar_prefetch=1, grid=(nr,), in_specs=specs,
                                               out_specs=[o_spec, pl.BlockSpec((8, 128), lambda j, w: (0, 0))]),
        out_shape=[jax.ShapeDtypeStruct(o_shape, F32), jax.ShapeDtypeStruct((8, 128), F32)], input_output_aliases=alias,
        compiler_params=_params(("arbitrary",)), name=name,
    )(where, *ins)


def _join_halves(tensors, kinds, *, name):
    n = len(tensors)

    def mine(ref, kind, h):
        if kind == "row":
            c = ref.shape[2]
            return ref.at[:, :, pl.ds(h * (c // 2), c // 2)]
        b = ref.shape[1]
        return ref.at[:, pl.ds(h * (b // 2), b // 2), :]

    def body(*refs):
        bufs = refs[n:2 * n]
        send, recv = refs[2 * n:]
        x, y, c, _ = _place()
        cps = []
        for i in range(n):
            part = mine(bufs[i], kinds[i], c)
            cp = pltpu.make_async_remote_copy(src_ref=part, dst_ref=part, send_sem=send.at[i],
                                              recv_sem=recv.at[i], device_id=(x, y, 1 - c), device_id_type=MESH)
            cp.start()
            cps.append(cp)
        for i in range(n):
            other = mine(bufs[i], kinds[i], 1 - c)
            pltpu.make_async_remote_copy(src_ref=other, dst_ref=other, send_sem=send.at[i],
                                         recv_sem=recv.at[i], device_id=(x, y, 1 - c), device_id_type=MESH).wait_recv()
        for cp in cps:
            cp.wait_send()

    return pl.pallas_call(
        body, in_specs=[ANY] * n, out_specs=[ANY] * n,
        out_shape=[jax.ShapeDtypeStruct(t.shape, t.dtype) for t in tensors],
        scratch_shapes=[pltpu.SemaphoreType.DMA((n,)), pltpu.SemaphoreType.DMA((n,))],
        input_output_aliases={i: i for i in range(n)},
        name=name,
    )(*tensors)


def _win(ref, kind, s, h=None):
    if kind == "col":
        b, c = ref.shape
        cols = pl.ds(s * (c // N_CHIPS), c // N_CHIPS)
        return ref.at[:, cols] if h is None else ref.at[pl.ds(h * (b // 2), b // 2), cols]
    if kind == "row":
        b, c = ref.shape
        rows = pl.ds(s * (b // N_CHIPS), b // N_CHIPS)
        return ref.at[rows, :] if h is None else ref.at[rows, pl.ds(h * (c // 2), c // 2)]
    b = ref.shape[1]
    return ref.at[s] if h is None else ref.at[s, pl.ds(h * (b // 2), b // 2)]


def _half(ref, kind, h):
    b, c = ref.shape
    if kind == "row":
        return ref.at[:, pl.ds(h * (c // 2), c // 2)]
    return ref.at[pl.ds(h * (b // 2), b // 2), :]


def _full_shape(shard_shape, kind):
    b, c = shard_shape
    return {"col": (b, N_CHIPS * c), "row": (N_CHIPS * b, c), "slab": (N_CHIPS, b, c)}[kind]


def _gather_body(srcs, outs, kinds, send, recv):
    x, y, c, chips = _place()
    me = 2 * x + y
    sib = (x, y, 1 - c)

    def rcopy(i, k, src, dst, to):
        return pltpu.make_async_remote_copy(src_ref=src, dst_ref=dst, send_sem=send.at[7 * i + k], recv_sem=recv.at[7 * i + k],
                                            device_id=to, device_id_type=MESH)

    started = []
    for i, (src, out, kind) in enumerate(zip(srcs, outs, kinds)):
        own = rcopy(i, 6, src, _win(out, kind, me), sib)
        own.start()
        started.append(own)
        for k, (px, py) in enumerate(chips):
            cp = rcopy(i, k, _half(src, kind, c), _win(out, kind, me, c), (px, py, c))
            cp.start()
            started.append(cp)
    for i, (out, kind) in enumerate(zip(outs, kinds)):
        for k, (px, py) in enumerate(chips):
            landed = _win(out, kind, 2 * px + py, c)
            rcopy(i, k, landed, landed, (px, py, c)).wait_recv()
            fw = rcopy(i, 3 + k, landed, landed, sib)
            fw.start()
            started.append(fw)
    for i, (src, out, kind) in enumerate(zip(srcs, outs, kinds)):
        for k, (px, py) in enumerate(chips):
            other = _win(out, kind, 2 * px + py, 1 - c)
            rcopy(i, 3 + k, other, other, sib).wait_recv()
        rcopy(i, 6, src, _win(out, kind, me), sib).wait_recv()
    for cp in started:
        cp.wait_send()


def _seq_gather(shards, kinds, *, name, cid):
    n = len(shards)
    srcs = [jax.new_ref(s, memory_space=pltpu.MemorySpace.HBM) for s in shards]
    outs = [jax.empty_ref(jax.ShapeDtypeStruct(_full_shape(s.shape, k), s.dtype), memory_space=pltpu.MemorySpace.HBM)
            for s, k in zip(shards, kinds)]

    @pl.kernel(mesh=plsc.ScalarSubcoreMesh(axis_name="seq", num_cores=1), name=name,
               scratch_types=(pltpu.SemaphoreType.DMA((7 * n,)), pltpu.SemaphoreType.DMA((7 * n,))),
               compiler_params=pltpu.CompilerParams(collective_id=cid))
    def launch(send, recv):
        x, y, c, chips = _place()
        barrier = pltpu.get_barrier_semaphore()
        for px, py in chips:
            pl.semaphore_signal(barrier, inc=1, device_id=(px, py, c), device_id_type=MESH)
        pl.semaphore_signal(barrier, inc=1, device_id=(x, y, 1 - c), device_id_type=MESH)
        pl.semaphore_wait(barrier, 4)
        _gather_body(srcs, outs, kinds, send, recv)

    launch()
    return [o[...] for o in outs]


KIND = dict(w_qkv_a="slab", w_o_a="col", w_q_b="row", w_o_b="row", w_kvf="slab", w_up="col", w_down="row", small="slab")
LAYERS = dict(w_qkv_a=N_A, w_o_a=N_A, w_q_b=DEPTH - N_A, w_o_b=DEPTH - N_A, w_kvf=1, w_up=DEPTH, w_down=DEPTH, small=1)
SMALL_W = 1792
SMALL_ROWS = 8


class _Reducer:
    def __init__(self, where):
        self.where = where
        self.acc = {nm: None for nm in KIND}
        self.pending = None

    def __call__(self, group, tag):
        names, layers, parts = zip(*group)
        kinds = [KIND[nm] for nm in names]
        summed = self._sum_pending(after=parts[-1])
        sib = _seq_swap(list(parts), kinds, name="reduce_swap_" + tag)
        halves = []
        for g, p, k, nm in zip(parts, sib, kinds, names):
            halves.append(_add_half(g, p, k, self.where, halves[-1] if halves else None, name="reduce_add_" + nm))
        landed = _seq_scatter(halves, kinds, name="reduce_scatter_" + tag)
        self.pending = (names, layers, landed, halves, kinds)
        return [halves[-1], summed]

    def flush(self, after):
        return self._sum_pending(after)

    def _sum_pending(self, after):
        if self.pending is None:
            return None
        for nm, l, r, h, k in zip(*self.pending):
            self.acc[nm], after = _sum_chips(r, h, k, self.where, l, LAYERS[nm], self.acc[nm], after, name="reduce_sum_" + nm)
        self.pending = None
        return after

    def finish(self):
        self._sum_pending(after=None)
        names = list(KIND)
        joined = _join_halves([self.acc[nm] for nm in names], [KIND[nm] for nm in names], name="reduce_pair_join")
        return dict(zip(names, joined))


def _headsum_matrix():
    r = lax.broadcasted_iota(jnp.int32, (128, 128), 0) // HD
    c = lax.broadcasted_iota(jnp.int32, (128, 128), 1) // HD
    return jnp.where(r == c, 1.0, 0.0).astype(BF16)


def kernel(x, norm_gains, w_qkv_a, w_o_a, w_q_b, w_o_b, kv_norm, w_kvf, b_f, w_up, conv_w, conv_b, w_down, loss_target, m_norm_gains, m_w_qkv_a, m_w_o_a, m_w_q_b, m_w_o_b, m_kv_norm, m_w_kvf, m_b_f, m_w_up, m_conv_w, m_conv_b, m_w_down, v_norm_gains, v_w_qkv_a, v_w_o_a, v_w_q_b, v_w_o_b, v_kv_norm, v_w_kvf, v_b_f, v_w_up, v_conv_w, v_conv_b, v_w_down):
    xi, yi, ci = lax.axis_index("x"), lax.axis_index("y"), lax.axis_index("c")
    chip = 2 * xi + yi
    where = jnp.stack([chip, ci]).astype(jnp.int32)
    ws = dict(norm_gains=norm_gains, w_qkv_a=w_qkv_a, w_o_a=w_o_a, w_q_b=w_q_b, w_o_b=w_o_b, kv_norm=kv_norm, w_kvf=w_kvf,
              b_f=b_f, w_up=w_up, conv_w=conv_w, conv_b=conv_b, w_down=w_down)
    ms = dict(norm_gains=m_norm_gains, w_qkv_a=m_w_qkv_a, w_o_a=m_w_o_a, w_q_b=m_w_q_b, w_o_b=m_w_o_b, kv_norm=m_kv_norm,
              w_kvf=m_w_kvf, b_f=m_b_f, w_up=m_w_up, conv_w=m_conv_w, conv_b=m_conv_b, w_down=m_w_down)
    vs = dict(norm_gains=v_norm_gains, w_qkv_a=v_w_qkv_a, w_o_a=v_w_o_a, w_q_b=v_w_q_b, w_o_b=v_w_o_b, kv_norm=v_kv_norm,
              w_kvf=v_w_kvf, b_f=v_b_f, w_up=v_w_up, conv_w=v_conv_w, conv_b=v_conv_b, w_down=v_w_down)

    small = jnp.concatenate([
        jnp.pad(norm_gains.reshape(16, 256), ((0, 0), (0, 1408 - 256))),
        jnp.pad(conv_w.reshape(12, 1408), ((0, 4), (0, 0)))], axis=0)
    big = [nm for nm in KIND if nm != "small"]
    half = {nm: ws[nm].astype(BF16) for nm in big}
    W = {nm: [None] * LAYERS[nm] for nm in big if nm != "w_kvf"}
    g_small = None
    groups = [("0a", [("w_qkv_a", 0), ("w_o_a", 0), ("small", 0)]), ("0b", [("w_up", 0)]), ("0c", [("w_down", 0)]),
              ("1", [("w_qkv_a", 1), ("w_o_a", 1), ("w_up", 1), ("w_down", 1)]),
              ("2", [("w_kvf", 0), ("w_q_b", 0), ("w_o_b", 0), ("w_up", 2), ("w_down", 2)]),
              ("3", [("w_q_b", 1), ("w_o_b", 1), ("w_up", 3), ("w_down", 3)])]
    for tag, group in groups:
        shards = [small if nm == "small" else half[nm] if nm == "w_kvf" else half[nm][i] for nm, i in group]
        got = _seq_gather(shards, [KIND[nm] for nm, _ in group], name="gather_layer" + tag, cid=1)
        for (nm, i), g in zip(group, got):
            if nm == "small":
                g_small = g
            elif nm == "w_kvf":
                W[nm] = g.transpose(1, 0, 2).reshape(D, 2 * D + 16)
            else:
                W[nm][i] = g.transpose(1, 0, 2).reshape(D, 3 * A_W) if nm == "w_qkv_a" else g
    gains = g_small[:, :16, :256].transpose(1, 0, 2).reshape(DEPTH, 4, 1, D)
    cw_full = g_small[:, 16:28, :].transpose(1, 0, 2).reshape(DEPTH, 3, 2 * D_FF)
    cb_full = conv_b.reshape(DEPTH, 1, 2 * D_FF)

    reducer = _Reducer(where)
    sq, dh = _fwd_bwd(x[0], loss_target[0], W, gains, cw_full, cb_full, kv_norm, b_f, reducer)
    loss = lax.psum(sq[0, 0] * (0.5 / D), ("x", "y", "c"))
    return _update(loss, dh[None], reducer.finish(), chip, ws, ms, vs)


def _fwd_bwd(h, target, W, gains, cw_full, cb_full, kv_norm, b_f, reduce):
    w_kv = W["w_kvf"][:, :2 * D]
    w_kvf_pad = jnp.pad(W["w_kvf"], ((0, 0), (0, 128 - 16)))
    w_f = w_kvf_pad[:, 2 * D:]
    kvn_g = kv_norm.reshape(1, D)
    bf_pad = jnp.pad(b_f, (0, 128 - 16)).reshape(1, 128)
    tabs = _rope_tables()
    headsum = _headsum_matrix()

    saved = []
    kv = zf = c_row = kvn = h_kv = None
    xn = _rms_fwd(h, gains[0][0], out_dtype=BF16, name="rms_in")
    for l in range(DEPTH):
        s = {"h": h}
        g = gains[l]
        s["xn"] = xn
        if l < N_A:
            qkv = _matmul(xn, W["w_qkv_a"][l], mode="nn", out_dtype=F32, name="mm_qkv", mnk=(T, 3 * A_W, D), tn=768)
            qkvp = _rope_fwd(qkv, tabs).reshape(3, 3, 2, T, 128)
            o_p, lse_p = _band_fwd(qkvp)
            att, o3, lse3 = _combine_fwd(o_p, lse_p)
            s.update(qkvp=qkvp, o3=o3, lse3=lse3, lse_p=lse_p, att=att)
            mix = _matmul(att, W["w_o_a"][l], mode="nn", out_dtype=F32, name="mm_oa", mnk=(T, D, A_W))
        else:
            j = l - N_A
            if l == N_A:
                h_kv = h
                kvn = _rms_fwd(h, kvn_g, out_dtype=BF16, name="rms_in")
                kv = _matmul(kvn, w_kv, mode="nn", out_dtype=BF16, name="mm_kv")
                zf = _matmul(kvn, w_f, mode="nn", out_dtype=F32, name="mm_f")
                cum = _gates_fwd(zf, bf_pad)[:, :16]
                c_row = cum.T.reshape(8, 2, T)
            q = _matmul(xn, W["w_q_b"][j], mode="nn", out_dtype=BF16, name="mm_qb", mnk=(T, D, D), alpha=HD ** -0.5)
            o = _fox_fwd(q, kv, c_row)
            s.update(q=q, o=o)
            mix = _matmul(o, W["w_o_b"][j], mode="nn", out_dtype=F32, name="mm_ob", mnk=(T, D, D))
        s["mix"] = mix
        h1, xn2 = _rms_res_in(mix, g[1], h, g[2], name="rms_res_in")
        a = _matmul(xn2, W["w_up"][l], mode="nn", out_dtype=F32, name="mm_up", mnk=(T, 2 * D_FF, D))
        u = _convgate_fwd(a, cw_full[l], cb_full[l])
        f = _matmul(u, W["w_down"][l], mode="nn", out_dtype=F32, name="mm_down", mnk=(T, D, D_FF), tm=1024, tk=D_FF)
        if l + 1 < DEPTH:
            h, xn = _rms_res_in(f, g[3], h1, gains[l + 1][0], name="rms_res_in")
        else:
            h = _rms_fwd(f, g[3], res=h1, out_dtype=F32, name="rms_res")
        s.update(h1=h1, xn2=xn2, a=a, u=u, f=f)
        saved.append(s)

    dh, sq = _loss_head(h, target)

    d_gains = [[None] * 4 for _ in range(DEPTH)]
    d_cw, d_cb = [None] * DEPTH, [None] * DEPTH
    fox_acc = None
    d_kvnorm = d_bf = token = df = None

    def dw(nm, a, b, **kw):
        return _matmul(a, b, mode="tn", out_dtype=BF16, name="mm_dw_" + nm, **kw)

    flush = getattr(reduce, "flush", lambda after: None)

    def slabs(full, width):
        return full.reshape(full.shape[0], N_CHIPS, width).transpose(1, 0, 2)

    for l in reversed(range(DEPTH)):
        s = saved[l]
        g = gains[l]
        if df is None:
            df, d_gains[l][3] = _rms_bwd(dh, s["f"], g[3], out_dtype=BF16, name="rms_bwd")
        du = _matmul(df, W["w_down"][l], mode="nt", out_dtype=F32, name="mm_down_dx", mnk=(T, D_FF, D), tn=256, after=token)
        g_down = dw("w_down", s["u"], df, tm=1408, tn=1024)
        da, d_cw[l], d_cb[l] = _convgate_bwd(s["a"], du, cw_full[l], cb_full[l])
        dxn2 = _matmul(da, W["w_up"][l], mode="nt", out_dtype=F32, name="mm_up_dx", mnk=(T, D, 2 * D_FF), tm=1024, tn=1024, tk=1408,
                       a_map=_halves_a)
        g_up = dw("w_up", s["xn2"], da, mnk=(D, 2 * D_FF, T), tn=1408, b_map=_halves_b)
        token = reduce([("w_down", l, g_down), ("w_up", l, g_up)], "ffn%d" % l)
        dh1, dmix, d_gains[l][2], d_gains[l][1] = _rms_bwd2(dxn2, s["h1"], g[2], dh, s["mix"], g[1], name="rms_bwd2")
        if l < N_A:
            datt = _matmul(dmix, W["w_o_a"][l], mode="nt", out_dtype=F32, name="mm_oa_dx", mnk=(T, A_W, D), tn=768, after=token)
            g_o = dw("w_o_a", s["att"], dmix, tm=768, tn=1024)
            do_p, dlt_p = _combine_bwd(datt, s["o3"], s["lse3"], headsum)
            dqkv = None
            for which, d in enumerate(_band_bwd(s["qkvp"], do_p, s["lse_p"], dlt_p)):
                dqkv = _rope_bwd(d, which, tabs, dqkv)
            dxn = _matmul(dqkv, W["w_qkv_a"][l], mode="nt", out_dtype=F32, name="mm_qkv_dx", mnk=(T, D, 3 * A_W), tm=1024, tn=1024, tk=3 * A_W,
                          after=[flush(dqkv)])
            g_qkv = dw("w_qkv_a", s["xn"], dqkv, tn=768)
            group = [("w_o_a", l, g_o), ("w_qkv_a", l, slabs(g_qkv, 576))]
        else:
            j = l - N_A
            do = _matmul(dmix, W["w_o_b"][j], mode="nt", out_dtype=BF16, name="mm_ob_dx", mnk=(T, D, D), after=token)
            g_o = dw("w_o_b", s["o"], dmix, tn=1024)
            dq, *fox_acc = _fox_bwd(s["q"], kv, do, c_row, fox_acc)
            dxn = _matmul(dq, W["w_q_b"][j], mode="nt", out_dtype=F32, name="mm_qb_dx", mnk=(T, D, D), after=[flush(dq)])
            g_q = dw("w_q_b", s["xn"], dq, tn=1024)
            group = [("w_o_b", j, g_o), ("w_q_b", j, g_q)]
        if l > 0 and l != N_A:
            dh, df, d_gains[l][0], d_gains[l - 1][3] = _rms_bwd2(dxn, s["h"], g[0], dh1, saved[l - 1]["f"], gains[l - 1][3],
                                                                 name="rms_bwd2")
        else:
            dh, d_gains[l][0] = _rms_bwd(dxn, s["h"], g[0], dres=dh1, out_dtype=F32, name="rms_bwd_res")
            df = None
        if l == N_A:
            dk, dv, dck = fox_acc
            dc16 = -dck[:, :2, :].reshape(16, T).T
            dzf, d_bf = _gates_bwd(jnp.pad(dc16, ((0, 0), (0, 128 - 16))), zf, bf_pad)
            dkvf = jnp.concatenate([dk.astype(BF16), dv.astype(BF16), dzf], axis=1)
            g_kvf = _matmul(kvn, dkvf, mode="tn", out_dtype=BF16, name="mm_kvf_dw", tm=512, tn=2 * D + 128)[:, :2 * D + 16]
            dkvn = _matmul(dkvf, w_kvf_pad, mode="nt", out_dtype=F32, name="mm_kvf_dx", tm=1024, tn=1024, tk=2 * D + 128)
            dh, d_kvnorm = _rms_bwd(dkvn, h_kv, kvn_g, dres=dh, out_dtype=F32, name="rms_bwd_res")
            group.append(("w_kvf", 0, slabs(g_kvf, 516)))
        token = reduce(group, "mix%d" % l)
    small_flat = jnp.concatenate([
        jnp.stack([jnp.stack(r) for r in d_gains]).reshape(-1),
        jnp.stack(d_cw).transpose(0, 2, 1, 3).reshape(-1),
        jnp.stack(d_cb).reshape(-1),
        d_kvnorm.reshape(-1), d_bf[0, :16]])
    small = jnp.pad(small_flat, (0, 2 * N_CHIPS * SMALL_ROWS * SMALL_W - small_flat.shape[0]))
    reduce([("small", 0, small.reshape(N_CHIPS, 2 * SMALL_ROWS, SMALL_W))], "small")
    return sq, dh


def _update(loss, grad_x, reduced, chip, ws, ms, vs):
    red_s = reduced.pop("small")
    buf_s = lax.dynamic_update_slice(jnp.zeros((2, N_CHIPS, SMALL_ROWS, SMALL_W), F32), red_s.reshape(2, 1, SMALL_ROWS, SMALL_W),
                                     (0, chip, 0, 0))
    (all_s,) = _allgather([buf_s], ["slab"], name="gather_small_grads")
    sflat = all_s.transpose(1, 0, 2, 3).reshape(-1)

    grads = {nm: r.reshape(ws[nm].shape) for nm, r in reduced.items()}
    o = 0
    g_gains_full = sflat[o:o + 16 * D].reshape(DEPTH, 4, D); o += 16 * D
    g_cw_full = sflat[o:o + 12 * 2 * D_FF].reshape(DEPTH, 3, 2 * D_FF); o += 12 * 2 * D_FF
    grads["conv_b"] = sflat[o:o + 4 * 2 * D_FF].reshape(DEPTH, 2 * D_FF); o += 4 * 2 * D_FF
    grads["kv_norm"] = sflat[o:o + D]; o += D
    grads["b_f"] = sflat[o:o + 16]
    grads["norm_gains"] = lax.dynamic_slice_in_dim(g_gains_full, chip * 256, 256, axis=2)
    grads["conv_w"] = lax.dynamic_slice_in_dim(g_cw_full, chip * 1408, 1408, axis=2)

    names = ["norm_gains", "w_qkv_a", "w_o_a", "w_q_b", "w_o_b", "kv_norm", "w_kvf", "b_f", "w_up", "conv_w", "conv_b", "w_down"]
    deltas, new_m, new_v = {}, {}, {}
    for nm in names:
        shp = ws[nm].shape
        two = (math.prod(shp[:-1]), shp[-1]) if len(shp) > 1 else (1, shp[0])
        d, m2, v2 = _adamw(ws[nm].reshape(two), ms[nm].reshape(two), vs[nm].reshape(two), grads[nm].reshape(two),
                           name="adamw_" + nm)
        deltas[nm], new_m[nm], new_v[nm] = d.reshape(shp), m2.reshape(shp), v2.reshape(shp)

    return (loss, grad_x, *[grads[nm] for nm in names], *[deltas[nm] for nm in names],
            *[new_m[nm] for nm in names], *[new_v[nm] for nm in names])
```

```python
import math

import jax
import jax.numpy as jnp
from jax import lax
from jax.experimental import pallas as pl
from jax.experimental.pallas import tpu as pltpu
from jax.experimental.pallas import tpu_sc as plsc

F32 = jnp.float32
BF16 = jnp.bfloat16
MESH = pl.DeviceIdType.MESH
ANY = pl.BlockSpec(memory_space=pl.ANY)

T = 2048
D = 1024
HD = 64
DEPTH = 4
N_A = 2
A_W = 768
GW = 256
DIL = (1, 4, 16)
BLK = 128
D_FF = 2816
ROPE_THETA = 500000.0
EPS = 1e-6
NEG = -1e30
N_CHIPS = 4
FQ = 256
CT = 128
VMEM_BIG = 48 * 1024 * 1024

ADAM_LR, ADAM_B1, ADAM_B2, ADAM_EPS, ADAM_WD, ADAM_STEP = 0.001, 0.9, 0.999, 1e-08, 0.01, 10

NN = (((1,), (0,)), ((), ()))
NT = (((1,), (1,)), ((), ()))
TN = (((0,), (0,)), ((), ()))


def _dot(a, b, dims):
    return lax.dot_general(a, b, dims, preferred_element_type=F32)


def _pick(dim, pref):
    if dim <= pref:
        return dim
    best = None
    for t in range(128, pref + 1, 128):
        if dim % t == 0:
            best = t
    assert best is not None, (dim, pref)
    return best


def _params(sem=None, vmem=None):
    kw = {}
    if sem is not None:
        kw["dimension_semantics"] = sem
    if vmem is not None:
        kw["vmem_limit_bytes"] = vmem
    return pltpu.CompilerParams(**kw)


def _matmul(a, b, *, mode, out_dtype, name, mnk=None, alpha=None, tm=2048, tn=512, tk=2048, a_map=None, b_map=None, after=None):
    if mnk is not None:
        M, N, K = mnk
    elif mode == "nn":
        (M, K), (_, N) = a.shape, b.shape
    elif mode == "nt":
        (M, K), (N, _) = a.shape, b.shape
    else:
        (K, M), (_, N) = a.shape, b.shape
    tm, tn, tk = _pick(M, tm), _pick(N, tn), _pick(K, tk)
    nk = K // tk
    dims = {"nn": NN, "nt": NT, "tn": TN}[mode]
    after = [t for t in (after or ()) if t is not None]
    n_in = 2 + len(after)

    def body(*refs):
        a_ref, b_ref = refs[0], refs[1]
        o_ref = refs[n_in]
        k = pl.program_id(2)

        def finish(r):
            if alpha is not None:
                r = r * alpha
            o_ref[...] = r.astype(out_dtype)

        def product():
            return _dot(a_ref[...], b_ref[...], dims)

        if nk == 1:
            finish(product())
            return
        acc_ref = refs[n_in + 1]

        @pl.when(k == 0)
        def _():
            acc_ref[...] = product()

        @pl.when((k > 0) & (k < nk - 1))
        def _():
            acc_ref[...] += product()

        @pl.when(k == nk - 1)
        def _():
            finish(acc_ref[...] + product())

    a_blk = (tk, tm) if mode == "tn" else (tm, tk)
    b_blk = (tn, tk) if mode == "nt" else (tk, tn)
    if a_map is not None:
        a_spec = pl.BlockSpec((None,) + a_blk, a_map(tm, tn, tk))
    elif mode == "tn":
        a_spec = pl.BlockSpec(a_blk, lambda i, j, k: (k, i))
    else:
        a_spec = pl.BlockSpec(a_blk, lambda i, j, k: (i, k))
    if b_map is not None:
        b_spec = pl.BlockSpec((None,) + b_blk, b_map(tm, tn, tk))
    elif mode == "nt":
        b_spec = pl.BlockSpec(b_blk, lambda i, j, k: (j, k))
    else:
        b_spec = pl.BlockSpec(b_blk, lambda i, j, k: (k, j))
    return pl.pallas_call(
        body,
        grid=(M // tm, N // tn, nk),
        in_specs=[a_spec, b_spec] + [ANY] * len(after),
        out_specs=pl.BlockSpec((tm, tn), lambda i, j, k: (i, j)),
        out_shape=jax.ShapeDtypeStruct((M, N), out_dtype),
        scratch_shapes=[pltpu.VMEM((tm, tn), F32)] if nk > 1 else [],
        compiler_params=_params(("parallel", "parallel", "arbitrary"), VMEM_BIG),
        name=name,
    )(a, b, *after)


def _rms_fwd(x, g, *, out_dtype, name, res=None, tr=256):
    n, d = x.shape

    def body(*refs):
        x_ref, g_ref = refs[0], refs[1]
        o_ref = refs[-1]
        xv = x_ref[...].astype(F32)
        y = xv * lax.rsqrt(jnp.mean(xv * xv, axis=-1, keepdims=True) + EPS) * g_ref[...]
        if res is not None:
            y = y + refs[2][...]
        o_ref[...] = y.astype(out_dtype)

    row = pl.BlockSpec((tr, d), lambda i: (i, 0))
    vec = pl.BlockSpec((1, d), lambda i: (0, 0))
    ins = [x, g] + ([] if res is None else [res])
    specs = [row, vec] + ([] if res is None else [row])
    return pl.pallas_call(
        body, grid=(n // tr,), in_specs=specs, out_specs=row,
        out_shape=jax.ShapeDtypeStruct((n, d), out_dtype),
        compiler_params=_params(("parallel",)), name=name,
    )(*ins)


def _rms_bwd(dy, x, g, *, out_dtype, name, dres=None, tr=512):
    n, d = x.shape

    def body(*refs):
        dy_ref, x_ref, g_ref = refs[0], refs[1], refs[2]
        dx_ref, dg_ref = refs[-2], refs[-1]
        xv = x_ref[...].astype(F32)
        dyv = dy_ref[...].astype(F32)
        rstd = lax.rsqrt(jnp.mean(xv * xv, axis=-1, keepdims=True) + EPS)
        xhat = xv * rstd
        dxh = dyv * g_ref[...]
        dx = rstd * (dxh - xhat * jnp.mean(dxh * xhat, axis=-1, keepdims=True))
        if dres is not None:
            dx = dx + refs[3][...]
        dx_ref[...] = dx.astype(out_dtype)

        @pl.when(pl.program_id(0) == 0)
        def _():
            dg_ref[...] = jnp.zeros_like(dg_ref)

        dg_ref[...] += jnp.sum(dyv * xhat, axis=0, keepdims=True)

    row = pl.BlockSpec((tr, d), lambda i: (i, 0))
    vec = pl.BlockSpec((1, d), lambda i: (0, 0))
    ins = [dy, x, g] + ([] if dres is None else [dres])
    specs = [row, row, vec] + ([] if dres is None else [row])
    return pl.pallas_call(
        body, grid=(n // tr,), in_specs=specs, out_specs=[row, vec],
        out_shape=[jax.ShapeDtypeStruct((n, d), out_dtype), jax.ShapeDtypeStruct((1, d), F32)],
        compiler_params=_params(("arbitrary",), VMEM_BIG), name=name,
    )(*ins)


def _rms_res_in(x, g_res, res, g_in, *, name, tr=512):
    n, d = x.shape

    def body(x_ref, gr_ref, r_ref, gi_ref, h_ref, n_ref):
        xv = x_ref[...].astype(F32)
        h = r_ref[...] + xv * lax.rsqrt(jnp.mean(xv * xv, axis=-1, keepdims=True) + EPS) * gr_ref[...]
        h_ref[...] = h
        n_ref[...] = (h * lax.rsqrt(jnp.mean(h * h, axis=-1, keepdims=True) + EPS) * gi_ref[...]).astype(BF16)

    row = pl.BlockSpec((tr, d), lambda i: (i, 0))
    vec = pl.BlockSpec((1, d), lambda i: (0, 0))
    return pl.pallas_call(
        body, grid=(n // tr,), in_specs=[row, vec, row, vec], out_specs=[row, row],
        out_shape=[jax.ShapeDtypeStruct((n, d), F32), jax.ShapeDtypeStruct((n, d), BF16)],
        compiler_params=_params(("parallel",), VMEM_BIG), name=name,
    )(x, g_res, res, g_in)


def _rms_bwd2(dy, x, g, dres, x2, g2, *, name, tr=512):
    n, d = x.shape

    def one(dyv, xv, gv):
        rstd = lax.rsqrt(jnp.mean(xv * xv, axis=-1, keepdims=True) + EPS)
        xhat = xv * rstd
        dxh = dyv * gv
        return rstd * (dxh - xhat * jnp.mean(dxh * xhat, axis=-1, keepdims=True)), jnp.sum(dyv * xhat, axis=0, keepdims=True)

    def body(dy_ref, x_ref, g_ref, r_ref, x2_ref, g2_ref, dx_ref, d2_ref, dg_ref, dg2_ref):
        dx, dg = one(dy_ref[...].astype(F32), x_ref[...].astype(F32), g_ref[...])
        dx = dx + r_ref[...]
        dx_ref[...] = dx
        d2, dg2 = one(dx, x2_ref[...].astype(F32), g2_ref[...])
        d2_ref[...] = d2.astype(BF16)

        @pl.when(pl.program_id(0) == 0)
        def _():
            dg_ref[...] = jnp.zeros_like(dg_ref)
            dg2_ref[...] = jnp.zeros_like(dg2_ref)

        dg_ref[...] += dg
        dg2_ref[...] += dg2

    row = pl.BlockSpec((tr, d), lambda i: (i, 0))
    vec = pl.BlockSpec((1, d), lambda i: (0, 0))
    return pl.pallas_call(
        body, grid=(n // tr,), in_specs=[row, row, vec, row, row, vec], out_specs=[row, row, vec, vec],
        out_shape=[jax.ShapeDtypeStruct((n, d), F32), jax.ShapeDtypeStruct((n, d), BF16),
                   jax.ShapeDtypeStruct((1, d), F32), jax.ShapeDtypeStruct((1, d), F32)],
        compiler_params=_params(("arbitrary",), VMEM_BIG), name=name,
    )(dy, x, g, dres, x2, g2)


def _loss_head(h, target, *, tr=256):
    n, d = h.shape

    def body(h_ref, t_ref, dh_ref, s_ref):
        err = h_ref[...] - t_ref[...]
        dh_ref[...] = err * (1.0 / d)

        @pl.when(pl.program_id(0) == 0)
        def _():
            s_ref[...] = jnp.zeros_like(s_ref)

        s_ref[...] += jnp.sum(err * err)

    row = pl.BlockSpec((tr, d), lambda i: (i, 0))
    acc = pl.BlockSpec((8, 128), lambda i: (0, 0))
    return pl.pallas_call(
        body, grid=(n // tr,), in_specs=[row, row], out_specs=[row, acc],
        out_shape=[jax.ShapeDtypeStruct((n, d), F32), jax.ShapeDtypeStruct((8, 128), F32)],
        compiler_params=_params(("arbitrary",)), name="loss_head",
    )(h, target)


def _rope_tables():
    pos = jnp.arange(T, dtype=F32)
    inv = ROPE_THETA ** (-jnp.arange(0, 16, 2, dtype=F32) / 16)
    ang = pos[:, None] * inv[None, :]
    cos, sin = jnp.cos(ang), jnp.sin(ang)
    one = jnp.ones((T, HD - 16), F32)
    zero8 = jnp.zeros((T, 8), F32)
    zero = jnp.zeros((T, HD - 16), F32)
    c = jnp.concatenate([cos, cos, one], axis=1)
    s1 = jnp.concatenate([zero8, sin, zero], axis=1)
    s2 = jnp.concatenate([-sin, zero8, zero], axis=1)
    c, s1, s2 = (jnp.concatenate([t, t], axis=1) for t in (c, s1, s2))
    scale = HD ** -0.5
    return (jnp.stack([c * scale, c, jnp.ones_like(c)]), jnp.stack([s1 * scale, s1, jnp.zeros_like(c)]),
            jnp.stack([s2 * scale, s2, jnp.zeros_like(c)]))


def _row_chunks(r):
    if r == 1:
        n = 4
        return [(slice(i * (T // n), (i + 1) * (T // n)),) * 2 for i in range(n)]
    per = T // r
    return [(pl.ds(j, per, stride=r), slice(j * per, (j + 1) * per)) for j in range(r)]


def _rope_fwd(qkv, tabs):
    def body(x_ref, c_ref, s1_ref, s2_ref, o_ref):
        g = lax.rem(lax.div(pl.program_id(0), 2), 3)
        for gi, r in enumerate(DIL):
            @pl.when(g == gi)
            def _(r=r):
                for tok, prm in _row_chunks(r):
                    x = x_ref[tok, :]
                    y = x * c_ref[tok, :] + pltpu.roll(x, 8, 1) * s1_ref[tok, :] + pltpu.roll(x, 120, 1) * s2_ref[tok, :]
                    o_ref[prm, :] = y.astype(BF16)

    tab = pl.BlockSpec((None, T, 128), lambda b: (lax.div(b, 6), 0, 0))
    return pl.pallas_call(
        body, grid=(18,), in_specs=[pl.BlockSpec((T, 128), lambda b: (0, b)), tab, tab, tab],
        out_specs=pl.BlockSpec((None, T, 128), lambda b: (b, 0, 0)), out_shape=jax.ShapeDtypeStruct((18, T, 128), BF16),
        compiler_params=_params(("parallel",)), name="rope_fwd",
    )(qkv, *tabs)


def _rope_bwd(d, which, tabs, out_buf):
    def body(d_ref, c_ref, s1_ref, s2_ref, *rest):
        o_ref, tok_ref = rest[-2], rest[-1]
        g = lax.div(pl.program_id(0), 2)
        for gi, r in enumerate(DIL):
            @pl.when(g == gi)
            def _(r=r):
                for tok, prm in _row_chunks(r):
                    tok_ref[tok, :] = d_ref[prm, :]
                for rows, _ in _row_chunks(1):
                    gx = tok_ref[rows, :]
                    y = gx * c_ref[rows, :] + pltpu.roll(gx * s1_ref[rows, :], 120, 1) + pltpu.roll(gx * s2_ref[rows, :], 8, 1)
                    o_ref[rows, :] = y.astype(BF16)

    tab = pl.BlockSpec((None, T, 128), lambda b: (which, 0, 0))
    ins = [d, *tabs] + ([] if out_buf is None else [out_buf])
    specs = [pl.BlockSpec((None, None, T, 128), lambda b: (lax.div(b, 2), lax.rem(b, 2), 0, 0)), tab, tab, tab]
    return pl.pallas_call(
        body, grid=(6,), in_specs=specs + ([] if out_buf is None else [ANY]),
        out_specs=pl.BlockSpec((T, 128), lambda b: (0, 6 * which + b)),
        out_shape=jax.ShapeDtypeStruct((T, 3 * A_W), BF16), scratch_shapes=[pltpu.VMEM((T, 128), F32)],
        input_output_aliases={} if out_buf is None else {4: 0},
        compiler_params=_params(("arbitrary",)), name="rope_bwd",
    )(*ins)


def _head_mask(x, lane_lo):
    lane = lax.broadcasted_iota(jnp.int32, x.shape, 1)
    keep = (lane < HD) if lane_lo else (lane >= HD)
    return jnp.where(keep, x.astype(F32), 0.0).astype(BF16)


def _band_scalars(g):
    b = pl.program_id(0)
    nbs = (T // BLK) // DIL[g]
    has_prev = jnp.where((b & (nbs - 1)) != 0, 1, 0)
    next_ok = jnp.where(((b + 1) & (nbs - 1)) != 0, 1, 0)
    return has_prev, next_ok


def _band_mask_q(has_prev):
    row = lax.broadcasted_iota(jnp.int32, (BLK, 2 * BLK), 0)
    col = lax.broadcasted_iota(jnp.int32, (BLK, 2 * BLK), 1)
    return ((col < BLK) & (col >= row) & (has_prev == 1)) | ((col >= BLK) & (col - BLK <= row))


def _band_mask_k(next_ok):
    row = lax.broadcasted_iota(jnp.int32, (2 * BLK, BLK), 0)
    col = lax.broadcasted_iota(jnp.int32, (2 * BLK, BLK), 1)
    return ((row < BLK) & (col <= row)) | ((row >= BLK) & (col >= row - BLK) & (next_ok == 1))


def _band_spec(step, which=None):
    nb = T // BLK
    at = {"cur": lambda b: b, "prev": lambda b: jnp.maximum(b - 1, 0), "next": lambda b: jnp.minimum(b + 1, nb - 1)}[step]
    if which is None:
        return pl.BlockSpec((3, 2, BLK, 128), lambda b: (0, 0, at(b), 0))
    return pl.BlockSpec((None, 3, 2, BLK, 128), lambda b: (which, 0, 0, at(b), 0))


def _band_fwd(qkv):
    nb = T // BLK

    def body(q_ref, kc_ref, kp_ref, vc_ref, vp_ref, o_ref, l_ref):
        lane = lax.broadcasted_iota(jnp.int32, (BLK, 128), 1)
        for g in range(3):
            has_prev, _ = _band_scalars(g)
            mask = _band_mask_q(has_prev)
            for p in range(2):
                qp = q_ref[g, p]
                kcat = jnp.concatenate([kp_ref[g, p], kc_ref[g, p]], axis=0)
                vcat = jnp.concatenate([vp_ref[g, p], vc_ref[g, p]], axis=0)
                o_acc = jnp.zeros((BLK, 128), F32)
                lse = jnp.zeros((BLK, 128), F32)
                for e in range(2):
                    s = _dot(_head_mask(qp, e == 0), kcat, NT)
                    s = jnp.where(mask, s, NEG)
                    m = jnp.max(s, axis=-1, keepdims=True)
                    pr = jnp.exp(s - m)
                    l = jnp.sum(pr, axis=-1, keepdims=True)
                    o_acc = o_acc + _dot(pr.astype(BF16), _head_mask(vcat, e == 0), NN) / l
                    lse = jnp.where((lane < HD) if e == 0 else (lane >= HD), m + jnp.log(l), lse)
                o_ref[g, p] = o_acc
                l_ref[g, p] = lse

    out = _band_spec("cur")
    shp = jax.ShapeDtypeStruct((3, 2, T, 128), F32)
    return pl.pallas_call(
        body, grid=(nb,),
        in_specs=[_band_spec("cur", 0), _band_spec("cur", 1), _band_spec("prev", 1), _band_spec("cur", 2), _band_spec("prev", 2)],
        out_specs=[out, out], out_shape=[shp, shp],
        compiler_params=_params(("parallel",)), name="band_fwd",
    )(qkv, qkv, qkv, qkv, qkv)


def _band_bwd(qkv, do, lse, dlt):
    nb = T // BLK

    def body(qc_ref, qn_ref, kc_ref, kp_ref, vc_ref, vp_ref, doc_ref, don_ref, lc_ref, ln_ref, dc_ref, dn_ref,
             dq_ref, dk_ref, dv_ref):
        for g in range(3):
            has_prev, next_ok = _band_scalars(g)
            mask_q = _band_mask_q(has_prev)
            mask_k = _band_mask_k(next_ok)
            for p in range(2):
                qc, qn = qc_ref[g, p], qn_ref[g, p]
                doc, don = doc_ref[g, p], don_ref[g, p]
                kc, vc = kc_ref[g, p], vc_ref[g, p]
                kcat = jnp.concatenate([kp_ref[g, p], kc], axis=0)
                vcat = jnp.concatenate([vp_ref[g, p], vc], axis=0)
                qcat = jnp.concatenate([qc, qn], axis=0)
                docat = jnp.concatenate([doc, don], axis=0)
                dq = jnp.zeros((BLK, 128), F32)
                dk = jnp.zeros((BLK, 128), F32)
                dv = jnp.zeros((BLK, 128), F32)
                for e in range(2):
                    lo = e == 0
                    col = slice(HD * e, HD * e + 1)
                    lse_c, lse_n = lc_ref[g, p, :, col], ln_ref[g, p, :, col]
                    dl_c, dl_n = dc_ref[g, p, :, col], dn_ref[g, p, :, col]
                    s = jnp.where(mask_q, _dot(_head_mask(qc, lo), kcat, NT), NEG)
                    pr = jnp.exp(s - lse_c)
                    dp = _dot(_head_mask(doc, lo), vcat, NT)
                    ds = pr * (dp - dl_c)
                    dq = dq + _dot(ds.astype(BF16), _head_mask(kcat, lo), NN)
                    qm, dom = _head_mask(qcat, lo), _head_mask(docat, lo)
                    s2 = jnp.where(mask_k, _dot(qm, kc, NT), NEG)
                    p2 = jnp.exp(s2 - jnp.concatenate([lse_c, lse_n], axis=0))
                    dv = dv + _dot(p2.astype(BF16), dom, TN)
                    dp2 = _dot(dom, vc, NT)
                    ds2 = p2 * (dp2 - jnp.concatenate([dl_c, dl_n], axis=0))
                    dk = dk + _dot(ds2.astype(BF16), qm, TN)
                dq_ref[g, p] = dq
                dk_ref[g, p] = dk
                dv_ref[g, p] = dv

    cur, nxt = _band_spec("cur"), _band_spec("next")
    shp = jax.ShapeDtypeStruct((3, 2, T, 128), F32)
    return pl.pallas_call(
        body, grid=(nb,),
        in_specs=[_band_spec("cur", 0), _band_spec("next", 0), _band_spec("cur", 1), _band_spec("prev", 1),
                  _band_spec("cur", 2), _band_spec("prev", 2), cur, nxt, cur, nxt, cur, nxt],
        out_specs=[cur, cur, cur], out_shape=[shp, shp, shp],
        compiler_params=_params(("parallel",)), name="band_bwd",
    )(qkv, qkv, qkv, qkv, qkv, qkv, do, do, lse, lse, dlt, dlt)


def _split3(x):
    hi = x.astype(BF16)
    r = x - hi.astype(F32)
    mid = r.astype(BF16)
    lo = (r - mid.astype(F32)).astype(BF16)
    return hi, mid, lo


def _dot3(x, m, dims=NN):
    hi, mid, lo = _split3(x)
    return _dot(hi, m, dims) + _dot(mid, m, dims) + _dot(lo, m, dims)


def _combine_weights(lses):
    l0, l1, l2 = lses
    m = jnp.maximum(jnp.maximum(l0, l1), l2)
    e = [jnp.exp(l0 - m), jnp.exp(l1 - m), jnp.exp(l2 - m)]
    inv = 1.0 / (e[0] + e[1] + e[2])
    return [ei * inv for ei in e]


CR = 256


def _combine_fwd(o, lse):
    def body(o_ref, l_ref, att_ref, o3_ref, l3_ref):
        for g, r in enumerate(DIL):
            for p in range(2):
                for tok, prm in _row_chunks(r):
                    o3_ref[g, p, tok, :] = o_ref[g, p, prm, :]
                    l3_ref[g, p, tok, :] = l_ref[g, p, prm, :]
        for i in range(T // CR):
            rows = slice(i * CR, (i + 1) * CR)
            for p in range(2):
                alpha = _combine_weights([l3_ref[g, p, rows, :] for g in range(3)])
                for g in range(3):
                    att_ref[rows, g * GW + p * 128: g * GW + (p + 1) * 128] = (o3_ref[g, p, rows, :] * alpha[g]).astype(BF16)

    shp = jax.ShapeDtypeStruct((3, 2, T, 128), F32)
    return pl.pallas_call(
        body, out_shape=[jax.ShapeDtypeStruct((T, A_W), BF16), shp, shp],
        compiler_params=_params(vmem=VMEM_BIG), name="combine_fwd",
    )(o, lse)


def _combine_bwd(datt, o3, l3, headsum):
    def body(d_ref, o_ref, l_ref, hs_ref, do_ref, dl_ref, tdo_ref, tdl_ref):
        hs = hs_ref[...]
        for p in range(2):
            for i in range(T // CR):
                rows = slice(i * CR, (i + 1) * CR)
                alpha = _combine_weights([l_ref[g, p, rows, :] for g in range(3)])
                total = jnp.zeros((CR, 128), F32)
                for g in range(3):
                    dg = d_ref[rows, g * GW + p * 128: g * GW + (p + 1) * 128]
                    tdo_ref[g, rows, :] = dg * alpha[g]
                    total = total + alpha[g] * _dot3(dg * o_ref[g, p, rows, :], hs)
                for g in range(3):
                    tdl_ref[g, rows, :] = alpha[g] * total
            for g, r in enumerate(DIL):
                for tok, prm in _row_chunks(r):
                    do_ref[g, p, prm, :] = tdo_ref[g, tok, :].astype(BF16)
                    dl_ref[g, p, prm, :] = tdl_ref[g, tok, :]

    return pl.pallas_call(
        body, out_shape=[jax.ShapeDtypeStruct((3, 2, T, 128), BF16), jax.ShapeDtypeStruct((3, 2, T, 128), F32)],
        scratch_shapes=[pltpu.VMEM((3, T, 128), F32), pltpu.VMEM((3, T, 128), F32)],
        compiler_params=_params(vmem=VMEM_BIG), name="combine_bwd",
    )(datt, o3, l3, headsum)


def _fox_scores(qm, k_ref, ck_ref, e, i, n):
    s = _dot(qm, k_ref[0:n, :], NT) - ck_ref[0, e:e + 1, 0:n]
    row = lax.broadcasted_iota(jnp.int32, (FQ, FQ), 0)
    col = lax.broadcasted_iota(jnp.int32, (FQ, FQ), 1)
    diag = jnp.where(col <= row, s[:, n - FQ:], NEG)
    m = jnp.max(diag, axis=-1, keepdims=True)
    if i == 0:
        pr = jnp.exp(diag - m)
        return pr, jnp.sum(pr, axis=-1, keepdims=True)
    past = s[:, :n - FQ]
    m = jnp.maximum(m, jnp.max(past, axis=-1, keepdims=True))
    p_past, p_diag = jnp.exp(past - m), jnp.exp(diag - m)
    l = jnp.sum(p_past, axis=-1, keepdims=True) + jnp.sum(p_diag, axis=-1, keepdims=True)
    return jnp.concatenate([p_past, p_diag], axis=1), l


def _fox_fwd(q, kv, c_row):
    def body(q_ref, k_ref, v_ref, cr_ref, o_ref, vm_ref):
        for e in range(2):
            vm_ref[e] = _head_mask(v_ref[...], e == 0)
        for i in range(T // FQ):
            n = (i + 1) * FQ
            rows = slice(i * FQ, n)
            acc = jnp.zeros((FQ, 128), F32)
            for e in range(2):
                qm = _head_mask(q_ref[rows, :], e == 0)
                pr, l = _fox_scores(qm, k_ref, cr_ref, e, i, n)
                acc = acc + _dot(pr.astype(BF16), vm_ref[e, 0:n, :], NN) / l
            o_ref[rows, :] = acc.astype(BF16)

    pair = pl.BlockSpec((T, 128), lambda p: (0, p))
    return pl.pallas_call(
        body, grid=(D // 128,),
        in_specs=[pair, pair, pl.BlockSpec((T, 128), lambda p: (0, D // 128 + p)), pl.BlockSpec((1, 2, T), lambda p: (p, 0, 0))],
        out_specs=pair, out_shape=jax.ShapeDtypeStruct((T, D), BF16),
        scratch_shapes=[pltpu.VMEM((2, T, 128), BF16)],
        compiler_params=_params(("parallel",), VMEM_BIG), name="fox_fwd",
    )(q, kv, kv, c_row)


def _fox_bwd(q, kv, do, c_row, init):
    def body(q_ref, k_ref, v_ref, do_ref, cr_ref, *rest):
        dq_ref, dk_ref, dv_ref, dck_ref, km_ref = rest[-5:]
        for o_ref, i_ref in zip((dk_ref, dv_ref, dck_ref), rest[:-5] or (None,) * 3):
            o_ref[...] = jnp.zeros_like(o_ref) if i_ref is None else i_ref[...]
        for e in range(2):
            km_ref[e] = _head_mask(k_ref[...], e == 0)
        for i in range(T // FQ):
            n = (i + 1) * FQ
            rows = slice(i * FQ, n)
            dq = jnp.zeros((FQ, 128), F32)
            dk = jnp.zeros((n, 128), F32)
            dv = jnp.zeros((n, 128), F32)
            for e in range(2):
                qm = _head_mask(q_ref[rows, :], e == 0)
                dom = _head_mask(do_ref[rows, :], e == 0)
                pr, l = _fox_scores(qm, k_ref, cr_ref, e, i, n)
                pr = pr * (1.0 / l)
                dp = _dot(dom, v_ref[0:n, :], NT)
                ds = pr * (dp - jnp.sum(pr * dp, axis=-1, keepdims=True))
                dsb = ds.astype(BF16)
                dq = dq + _dot(dsb, km_ref[e, 0:n, :], NN)
                dk = dk + _dot(dsb, qm, TN)
                dv = dv + _dot(pr.astype(BF16), dom, TN)
                dck_ref[0, e:e + 1, 0:n] += jnp.sum(ds, axis=0, keepdims=True)
            dk_ref[0:n, :] += dk
            dv_ref[0:n, :] += dv
            dq_ref[rows, :] = (dq * HD ** -0.5).astype(BF16)

    pair = pl.BlockSpec((T, 128), lambda p: (0, p))
    ck = pl.BlockSpec((1, 8, T), lambda p: (p, 0, 0))
    return pl.pallas_call(
        body, grid=(D // 128,),
        in_specs=[pair, pair, pl.BlockSpec((T, 128), lambda p: (0, D // 128 + p)), pair,
                  pl.BlockSpec((1, 2, T), lambda p: (p, 0, 0))] + ([] if init is None else [pair, pair, ck]),
        out_specs=[pair, pair, pair, ck],
        out_shape=[jax.ShapeDtypeStruct((T, D), BF16), jax.ShapeDtypeStruct((T, D), F32), jax.ShapeDtypeStruct((T, D), F32),
                   jax.ShapeDtypeStruct((D // 128, 8, T), F32)],
        scratch_shapes=[pltpu.VMEM((2, T, 128), BF16)],
        compiler_params=_params(("parallel",), VMEM_BIG), name="fox_bwd",
    )(q, kv, kv, do, c_row, *(init or ()))


def _tri(lower):
    r = lax.broadcasted_iota(jnp.int32, (BLK, BLK), 0)
    c = lax.broadcasted_iota(jnp.int32, (BLK, BLK), 1)
    return jnp.where((c <= r) if lower else (c >= r), 1.0, 0.0).astype(BF16)


def _gates_fwd(z, b):
    def body(z_ref, b_ref, c_ref):
        tri = _tri(True)
        carry = jnp.zeros((1, 128), F32)
        for i in range(T // BLK):
            rows = slice(i * BLK, (i + 1) * BLK)
            x = z_ref[rows, :] + b_ref[...]
            logf = jnp.minimum(x, 0.0) - jnp.log(1.0 + jnp.exp(-jnp.abs(x)))
            hi, mid, lo = _split3(logf)
            y = _dot(tri, hi, NN) + _dot(tri, mid, NN) + _dot(tri, lo, NN) + carry
            c_ref[rows, :] = y
            carry = y[BLK - 1:BLK, :]

    return pl.pallas_call(body, out_shape=jax.ShapeDtypeStruct((T, 128), F32), name="gates_fwd")(z, b)


def _gates_bwd(dc, z, b):
    def body(dc_ref, z_ref, b_ref, dz_ref, db_ref):
        tri = _tri(False)
        carry = jnp.zeros((1, 128), F32)
        db = jnp.zeros((1, 128), F32)
        for i in reversed(range(T // BLK)):
            rows = slice(i * BLK, (i + 1) * BLK)
            hi, mid, lo = _split3(dc_ref[rows, :])
            dlogf = _dot(tri, hi, NN) + _dot(tri, mid, NN) + _dot(tri, lo, NN) + carry
            carry = dlogf[0:1, :]
            x = z_ref[rows, :] + b_ref[...]
            dz = dlogf / (1.0 + jnp.exp(x))
            dz_ref[rows, :] = dz.astype(BF16)
            db = db + jnp.sum(dz, axis=0, keepdims=True)
        db_ref[...] = db

    return pl.pallas_call(
        body, out_shape=[jax.ShapeDtypeStruct((T, 128), BF16), jax.ShapeDtypeStruct((1, 128), F32)], name="gates_bwd",
    )(dc, z, b)


def _conv_pair(a_refs, cw_refs, cb_refs):
    row = lax.broadcasted_iota(jnp.int32, (T, CT), 0)
    outs = []
    for a_ref, cw_ref, cb_ref in zip(a_refs, cw_refs, cb_refs):
        z = a_ref[...]
        z1 = jnp.where(row >= 1, pltpu.roll(z, 1, 0), 0.0)
        z2 = jnp.where(row >= 2, pltpu.roll(z, 2, 0), 0.0)
        y = cw_ref[2:3, :] * z + cw_ref[1:2, :] * z1 + cw_ref[0:1, :] * z2 + cb_ref[...]
        outs.append((y, z, z1, z2))
    return outs


_GELU_K = math.sqrt(2.0 / math.pi)
N_CT = D_FF // CT


def _conv_specs():
    def at(rows, off):
        return pl.BlockSpec((rows, CT), lambda j: (0, j + off))
    return [at(T, 0), at(T, N_CT), at(3, 0), at(3, N_CT), at(1, 0), at(1, N_CT)]


def _convgate_fwd(a, cw, cb):
    def body(ag_ref, av_ref, wg_ref, wv_ref, bg_ref, bv_ref, u_ref):
        (g, _, _, _), (v, _, _, _) = _conv_pair((ag_ref, av_ref), (wg_ref, wv_ref), (bg_ref, bv_ref))
        th = jnp.tanh(_GELU_K * (g + 0.044715 * g * g * g))
        u_ref[...] = (0.5 * g * (1.0 + th) * v).astype(BF16)

    return pl.pallas_call(
        body, grid=(N_CT,), in_specs=_conv_specs(),
        out_specs=pl.BlockSpec((T, CT), lambda j: (0, j)), out_shape=jax.ShapeDtypeStruct((T, D_FF), BF16),
        compiler_params=_params(("parallel",), VMEM_BIG), name="convgate_fwd",
    )(a, a, cw, cw, cb, cb)


def _convgate_bwd(a, du, cw, cb):
    def body(ag_ref, av_ref, wg_ref, wv_ref, bg_ref, bv_ref, du_ref, da_ref, dcw_ref, dcb_ref):
        (g, gz, gz1, gz2), (v, vz, vz1, vz2) = _conv_pair((ag_ref, av_ref), (wg_ref, wv_ref), (bg_ref, bv_ref))
        du = du_ref[...].astype(F32)
        th = jnp.tanh(_GELU_K * (g + 0.044715 * g * g * g))
        gelu = 0.5 * g * (1.0 + th)
        dgelu = 0.5 * (1.0 + th) + 0.5 * g * (1.0 - th * th) * _GELU_K * (1.0 + 3 * 0.044715 * g * g)
        row = lax.broadcasted_iota(jnp.int32, (T, CT), 0)
        for h, (d, z, z1, z2, w_ref) in enumerate(((du * v * dgelu, gz, gz1, gz2, wg_ref), (du * gelu, vz, vz1, vz2, wv_ref))):
            d1 = jnp.where(row < T - 1, pltpu.roll(d, T - 1, 0), 0.0)
            d2 = jnp.where(row < T - 2, pltpu.roll(d, T - 2, 0), 0.0)
            da_ref[h] = (w_ref[2:3, :] * d + w_ref[1:2, :] * d1 + w_ref[0:1, :] * d2).astype(BF16)
            dcw_ref[h, 0:1, :] = jnp.sum(d * z2, axis=0, keepdims=True)
            dcw_ref[h, 1:2, :] = jnp.sum(d * z1, axis=0, keepdims=True)
            dcw_ref[h, 2:3, :] = jnp.sum(d * z, axis=0, keepdims=True)
            dcb_ref[h] = jnp.sum(d, axis=0, keepdims=True)

    def both(rows):
        return pl.BlockSpec((2, rows, CT), lambda j: (0, 0, j))

    return pl.pallas_call(
        body, grid=(N_CT,),
        in_specs=_conv_specs() + [pl.BlockSpec((T, CT), lambda j: (0, j))],
        out_specs=[both(T), both(3), both(1)],
        out_shape=[jax.ShapeDtypeStruct((2, T, D_FF), BF16), jax.ShapeDtypeStruct((2, 3, D_FF), F32),
                   jax.ShapeDtypeStruct((2, 1, D_FF), F32)],
        compiler_params=_params(("parallel",), VMEM_BIG), name="convgate_bwd",
    )(a, a, cw, cw, cb, cb, du)


def _halves_a(tm, tn, tk):
    per = D_FF // tk
    return lambda i, j, k: (lax.div(k, per), i, lax.rem(k, per))


def _halves_b(tm, tn, tk):
    per = D_FF // tn
    return lambda i, j, k: (lax.div(j, per), k, lax.rem(j, per))


def _adamw(w, m, v, g, *, name):
    r, c = w.shape
    tr = r
    if r * c > 256 * 1024:
        for cand in range(8, r, 8):
            if r % cand == 0 and cand * c <= 256 * 1024:
                tr = cand

    def body(w_ref, m_ref, v_ref, g_ref, d_ref, nm_ref, nv_ref):
        gv = g_ref[...]
        mn = ADAM_B1 * m_ref[...] + (1.0 - ADAM_B1) * gv
        vn = ADAM_B2 * v_ref[...] + (1.0 - ADAM_B2) * (gv * gv)
        m_hat = mn * (1.0 / (1.0 - ADAM_B1 ** ADAM_STEP))
        v_hat = vn * (1.0 / (1.0 - ADAM_B2 ** ADAM_STEP))
        d_ref[...] = -ADAM_LR * (m_hat / (jnp.sqrt(v_hat) + ADAM_EPS) + ADAM_WD * w_ref[...])
        nm_ref[...] = mn
        nv_ref[...] = vn

    blk = pl.BlockSpec((tr, c), lambda i: (i, 0))
    shp = jax.ShapeDtypeStruct((r, c), F32)
    return pl.pallas_call(
        body, grid=(r // tr,), in_specs=[blk] * 4, out_specs=[blk] * 3, out_shape=[shp] * 3,
        compiler_params=_params(("parallel",)), name=name,
    )(w, m, v, g)


def _place():
    x, y, c = lax.axis_index("x"), lax.axis_index("y"), lax.axis_index("c")
    chips = [(1 - x, y), (x, 1 - y), (1 - x, 1 - y)]
    return x, y, c, chips


def _window(ref, kind, s, half=None):
    lead = () if half is None else (half,)
    b, c = ref.shape[-2], ref.shape[-1]
    if kind == "col":
        return ref.at[lead + (slice(None), slice(None), pl.ds(s * (c // N_CHIPS), c // N_CHIPS))]
    if kind == "row":
        return ref.at[lead + (slice(None), pl.ds(s * (b // N_CHIPS), b // N_CHIPS), slice(None))]
    return ref.at[lead + (s,)]


def _allgather(tensors, kinds, *, name):
    n = len(tensors)

    def body(*refs):
        bufs = refs[n:2 * n]
        send, recv = refs[2 * n:]
        x, y, c, chips = _place()
        me = 2 * x + y
        sib = (x, y, 1 - c)

        def rcopy(i, k, win, to):
            return pltpu.make_async_remote_copy(src_ref=win, dst_ref=win, send_sem=send.at[i * 6 + k], recv_sem=recv.at[i * 6 + k],
                                                device_id=to, device_id_type=MESH)

        started = []
        for i in range(n):
            for k, (px, py) in enumerate(chips):
                cp = rcopy(i, k, _window(bufs[i], kinds[i], me, c), (px, py, c))
                cp.start()
                started.append(cp)
        for i in range(n):
            for k, (px, py) in enumerate(chips):
                landed = _window(bufs[i], kinds[i], 2 * px + py, c)
                rcopy(i, k, landed, (px, py, c)).wait_recv()
                fw = rcopy(i, 3 + k, landed, sib)
                fw.start()
                started.append(fw)
        for i in range(n):
            for k, (px, py) in enumerate(chips):
                rcopy(i, 3 + k, _window(bufs[i], kinds[i], 2 * px + py, 1 - c), sib).wait_recv()
        for cp in started:
            cp.wait_send()

    return pl.pallas_call(
        body, in_specs=[ANY] * n, out_specs=[ANY] * n,
        out_shape=[jax.ShapeDtypeStruct(t.shape, t.dtype) for t in tensors],
        scratch_shapes=[pltpu.SemaphoreType.DMA((6 * n,)), pltpu.SemaphoreType.DMA((6 * n,))],
        input_output_aliases={i: i for i in range(n)},
        name=name,
    )(*tensors)


def _rows_tile(rows, cols, sub):
    best = None
    for t in range(sub, rows + 1, sub):
        if rows % t == 0 and t * cols <= 512 * 1024:
            best = t
    return rows if best is None else best


def _sequencer(name, cid, n_sems, peers_of, body):
    @pl.kernel(mesh=plsc.ScalarSubcoreMesh(axis_name="seq", num_cores=1), name=name,
               scratch_types=(pltpu.SemaphoreType.DMA((n_sems,)), pltpu.SemaphoreType.DMA((n_sems,))),
               compiler_params=pltpu.CompilerParams(collective_id=cid))
    def launch(send, recv):
        x, y, c, chips = _place()
        peers = peers_of(x, y, c, chips)
        barrier = pltpu.get_barrier_semaphore()
        for peer in peers:
            pl.semaphore_signal(barrier, inc=1, device_id=peer, device_id_type=MESH)
        pl.semaphore_wait(barrier, len(peers))
        body(send, recv)

    launch()


def _half_of_full(ref, kind, h):
    if kind == "col":
        b = ref.shape[0]
        return ref.at[pl.ds(h * (b // 2), b // 2), :]
    if kind == "row":
        c = ref.shape[1]
        return ref.at[:, pl.ds(h * (c // 2), c // 2)]
    b = ref.shape[1]
    return ref.at[:, pl.ds(h * (b // 2), b // 2), :]


def _half_shape(full, kind):
    if kind == "col":
        return (full[0] // 2, full[1])
    if kind == "row":
        return (full[0], full[1] // 2)
    return (full[0], full[1] // 2, full[2])


def _win_of_half(ref, kind, s):
    if kind == "col":
        c = ref.shape[1]
        return ref.at[:, pl.ds(s * (c // N_CHIPS), c // N_CHIPS)]
    if kind == "row":
        b = ref.shape[0]
        return ref.at[pl.ds(s * (b // N_CHIPS), b // N_CHIPS), :]
    return ref.at[s]


def _win_shape(half, kind):
    if kind == "col":
        return (half[0], half[1] // N_CHIPS)
    if kind == "row":
        return (half[0] // N_CHIPS, half[1])
    return half[1:]


def _seq_swap(parts, kinds, *, name):
    n = len(parts)
    srcs = [jax.new_ref(p, memory_space=pltpu.MemorySpace.HBM) for p in parts]
    outs = [jax.empty_ref(jax.ShapeDtypeStruct(_half_shape(p.shape, k), p.dtype), memory_space=pltpu.MemorySpace.HBM)
            for p, k in zip(parts, kinds)]

    def body(send, recv):
        x, y, c, _ = _place()
        cps = []
        for i in range(n):
            cp = pltpu.make_async_remote_copy(src_ref=_half_of_full(srcs[i], kinds[i], 1 - c), dst_ref=outs[i], send_sem=send.at[i],
                                              recv_sem=recv.at[i], device_id=(x, y, 1 - c), device_id_type=MESH)
            cp.start()
            cps.append(cp)
        for cp in cps:
            cp.wait()

    _sequencer(name, 2, n, lambda x, y, c, chips: [(x, y, 1 - c)], body)
    return [o[...] for o in outs]


def _seq_scatter(halves, kinds, *, name):
    n = len(halves)
    srcs = [jax.new_ref(h, memory_space=pltpu.MemorySpace.HBM) for h in halves]
    outs = [jax.empty_ref(jax.ShapeDtypeStruct((3,) + _win_shape(h.shape, k), h.dtype), memory_space=pltpu.MemorySpace.HBM)
            for h, k in zip(halves, kinds)]

    def body(send, recv):
        x, y, c, chips = _place()
        cps = []
        for i in range(n):
            for k, (px, py) in enumerate(chips):
                cp = pltpu.make_async_remote_copy(src_ref=_win_of_half(srcs[i], kinds[i], 2 * px + py), dst_ref=outs[i].at[k],
                                                  send_sem=send.at[3 * i + k], recv_sem=recv.at[3 * i + k],
                                                  device_id=(px, py, c), device_id_type=MESH)
                cp.start()
                cps.append(cp)
        for cp in cps:
            cp.wait()

    _sequencer(name, 3, 3 * n, lambda x, y, c, chips: [(px, py, c) for px, py in chips], body)
    return [o[...] for o in outs]


def _add_half(g, p, kind, where, after, *, name):
    if kind == "slab":
        s, b2, c = p.shape
        tr = _rows_tile(b2, c, 16)
        nr = b2 // tr
        grid = (s, nr)
        g_spec = pl.BlockSpec((None, tr, c), lambda i, r, w: (i, w[1] * nr + r, 0))
        p_spec = pl.BlockSpec((None, tr, c), lambda i, r, w: (i, r, 0))
    elif kind == "col":
        b2, c = p.shape
        tr = _rows_tile(b2, c, 16)
        nr = b2 // tr
        grid = (1, nr)
        g_spec = pl.BlockSpec((tr, c), lambda i, r, w: (w[1] * nr + r, 0))
        p_spec = pl.BlockSpec((tr, c), lambda i, r, w: (r, 0))
    else:
        b, c2 = p.shape
        tr = _rows_tile(b, c2, 16)
        grid = (1, b // tr)
        g_spec = pl.BlockSpec((tr, c2), lambda i, r, w: (r, w[1]))
        p_spec = pl.BlockSpec((tr, c2), lambda i, r, w: (r, 0))

    def body(w_ref, g_ref, p_ref, *rest):
        o_ref = rest[-1]
        o_ref[...] = (g_ref[...].astype(F32) + p_ref[...].astype(F32)).astype(o_ref.dtype)

    extra = [] if after is None else [after]
    return pl.pallas_call(
        body,
        grid_spec=pltpu.PrefetchScalarGridSpec(num_scalar_prefetch=1, grid=grid, in_specs=[g_spec, p_spec] + [ANY] * len(extra),
                                               out_specs=p_spec),
        out_shape=jax.ShapeDtypeStruct(p.shape, g.dtype),
        compiler_params=_params(("parallel", "parallel")), name=name,
    )(where, g, p, *extra)


def _sum_chips(r, h, kind, where, layer, layers, out_buf, after, *, name):
    _, br, cr = r.shape
    tr = _rows_tile(br, cr, 16)
    nr = br // tr
    if kind == "col":
        h_spec = pl.BlockSpec((tr, cr), lambda j, w: (j, w[0]))
        o_shape, o_spec = (layers, 2 * br, cr), pl.BlockSpec((None, tr, cr), lambda j, w: (layer, w[1] * nr + j, 0))
    elif kind == "row":
        h_spec = pl.BlockSpec((tr, cr), lambda j, w: (w[0] * nr + j, 0))
        o_shape, o_spec = (layers, br, 2 * cr), pl.BlockSpec((None, tr, cr), lambda j, w: (layer, j, w[1]))
    else:
        h_spec = pl.BlockSpec((None, tr, cr), lambda j, w: (w[0], j, 0))
        o_shape, o_spec = (layers, 2 * br, cr), pl.BlockSpec((None, tr, cr), lambda j, w: (layer, w[1] * nr + j, 0))

    def body(w_ref, h_ref, r0_ref, r1_ref, r2_ref, *rest):
        o_ref, t_ref = rest[-2], rest[-1]
        o_ref[...] = ((h_ref[...].astype(F32) + r0_ref[...].astype(F32)) + r1_ref[...].astype(F32)) + r2_ref[...].astype(F32)
        t_ref[...] = jnp.zeros_like(t_ref)

    def slot(k):
        return pl.BlockSpec((None, tr, cr), lambda j, w: (k, j, 0))

    ins, specs, alias = [h, r, r, r], [h_spec, slot(0), slot(1), slot(2)], {}
    if after is not None:
        ins.append(after)
        specs.append(ANY)
    if out_buf is not None:
        alias = {1 + len(ins): 0}
        ins.append(out_buf)
        specs.append(ANY)
    return pl.pallas_call(
        body,
        grid_spec=pltpu.PrefetchScalarGridSpec(num_scalar_prefetch=1, grid=(nr,), in_specs=specs,
                                               out_specs=[o_spec, pl.BlockSpec((8, 128), lambda j, w: (0, 0))]),
        out_shape=[jax.ShapeDtypeStruct(o_shape, F32), jax.ShapeDtypeStruct((8, 128), F32)], input_output_aliases=alias,
        compiler_params=_params(("arbitrary",)), name=name,
    )(where, *ins)


def _join_halves(tensors, kinds, *, name):
    n = len(tensors)

    def mine(ref, kind, h):
        if kind == "row":
            c = ref.shape[2]
            return ref.at[:, :, pl.ds(h * (c // 2), c // 2)]
        b = ref.shape[1]
        return ref.at[:, pl.ds(h * (b // 2), b // 2), :]

    def body(*refs):
        bufs = refs[n:2 * n]
        send, recv = refs[2 * n:]
        x, y, c, _ = _place()
        cps = []
        for i in range(n):
            part = mine(bufs[i], kinds[i], c)
            cp = pltpu.make_async_remote_copy(src_ref=part, dst_ref=part, send_sem=send.at[i],
                                              recv_sem=recv.at[i], device_id=(x, y, 1 - c), device_id_type=MESH)
            cp.start()
            cps.append(cp)
        for i in range(n):
            other = mine(bufs[i], kinds[i], 1 - c)
            pltpu.make_async_remote_copy(src_ref=other, dst_ref=other, send_sem=send.at[i],
                                         recv_sem=recv.at[i], device_id=(x, y, 1 - c), device_id_type=MESH).wait_recv()
        for cp in cps:
            cp.wait_send()

    return pl.pallas_call(
        body, in_specs=[ANY] * n, out_specs=[ANY] * n,
        out_shape=[jax.ShapeDtypeStruct(t.shape, t.dtype) for t in tensors],
        scratch_shapes=[pltpu.SemaphoreType.DMA((n,)), pltpu.SemaphoreType.DMA((n,))],
        input_output_aliases={i: i for i in range(n)},
        name=name,
    )(*tensors)


def _win(ref, kind, s, h=None):
    if kind == "col":
        b, c = ref.shape
        cols = pl.ds(s * (c // N_CHIPS), c // N_CHIPS)
        return ref.at[:, cols] if h is None else ref.at[pl.ds(h * (b // 2), b // 2), cols]
    if kind == "row":
        b, c = ref.shape
        rows = pl.ds(s * (b // N_CHIPS), b // N_CHIPS)
        return ref.at[rows, :] if h is None else ref.at[rows, pl.ds(h * (c // 2), c // 2)]
    b = ref.shape[1]
    return ref.at[s] if h is None else ref.at[s, pl.ds(h * (b // 2), b // 2)]


def _half(ref, kind, h):
    b, c = ref.shape
    if kind == "row":
        return ref.at[:, pl.ds(h * (c // 2), c // 2)]
    return ref.at[pl.ds(h * (b // 2), b // 2), :]


def _full_shape(shard_shape, kind):
    b, c = shard_shape
    return {"col": (b, N_CHIPS * c), "row": (N_CHIPS * b, c), "slab": (N_CHIPS, b, c)}[kind]


def _gather_body(srcs, outs, kinds, send, recv):
    x, y, c, chips = _place()
    me = 2 * x + y
    sib = (x, y, 1 - c)

    def rcopy(i, k, src, dst, to):
        return pltpu.make_async_remote_copy(src_ref=src, dst_ref=dst, send_sem=send.at[7 * i + k], recv_sem=recv.at[7 * i + k],
                                            device_id=to, device_id_type=MESH)

    started = []
    for i, (src, out, kind) in enumerate(zip(srcs, outs, kinds)):
        own = rcopy(i, 6, src, _win(out, kind, me), sib)
        own.start()
        started.append(own)
        for k, (px, py) in enumerate(chips):
            cp = rcopy(i, k, _half(src, kind, c), _win(out, kind, me, c), (px, py, c))
            cp.start()
            started.append(cp)
    for i, (out, kind) in enumerate(zip(outs, kinds)):
        for k, (px, py) in enumerate(chips):
            landed = _win(out, kind, 2 * px + py, c)
            rcopy(i, k, landed, landed, (px, py, c)).wait_recv()
            fw = rcopy(i, 3 + k, landed, landed, sib)
            fw.start()
            started.append(fw)
    for i, (src, out, kind) in enumerate(zip(srcs, outs, kinds)):
        for k, (px, py) in enumerate(chips):
            other = _win(out, kind, 2 * px + py, 1 - c)
            rcopy(i, 3 + k, other, other, sib).wait_recv()
        rcopy(i, 6, src, _win(out, kind, me), sib).wait_recv()
    for cp in started:
        cp.wait_send()


def _seq_gather(shards, kinds, *, name, cid):
    n = len(shards)
    srcs = [jax.new_ref(s, memory_space=pltpu.MemorySpace.HBM) for s in shards]
    outs = [jax.empty_ref(jax.ShapeDtypeStruct(_full_shape(s.shape, k), s.dtype), memory_space=pltpu.MemorySpace.HBM)
            for s, k in zip(shards, kinds)]

    @pl.kernel(mesh=plsc.ScalarSubcoreMesh(axis_name="seq", num_cores=1), name=name,
               scratch_types=(pltpu.SemaphoreType.DMA((7 * n,)), pltpu.SemaphoreType.DMA((7 * n,))),
               compiler_params=pltpu.CompilerParams(collective_id=cid))
    def launch(send, recv):
        x, y, c, chips = _place()
        barrier = pltpu.get_barrier_semaphore()
        for px, py in chips:
            pl.semaphore_signal(barrier, inc=1, device_id=(px, py, c), device_id_type=MESH)
        pl.semaphore_signal(barrier, inc=1, device_id=(x, y, 1 - c), device_id_type=MESH)
        pl.semaphore_wait(barrier, 4)
        _gather_body(srcs, outs, kinds, send, recv)

    launch()
    return [o[...] for o in outs]


KIND = dict(w_qkv_a="slab", w_o_a="col", w_q_b="row", w_o_b="row", w_kvf="slab", w_up="col", w_down="row", small="slab")
LAYERS = dict(w_qkv_a=N_A, w_o_a=N_A, w_q_b=DEPTH - N_A, w_o_b=DEPTH - N_A, w_kvf=1, w_up=DEPTH, w_down=DEPTH, small=1)
SMALL_W = 1792
SMALL_ROWS = 8


class _Reducer:
    def __init__(self, where):
        self.where = where
        self.acc = {nm: None for nm in KIND}
        self.pending = None

    def __call__(self, group, tag):
        names, layers, parts = zip(*group)
        kinds = [KIND[nm] for nm in names]
        summed = self._sum_pending(after=parts[-1])
        sib = _seq_swap(list(parts), kinds, name="reduce_swap_" + tag)
        halves = []
        for g, p, k, nm in zip(parts, sib, kinds, names):
            halves.append(_add_half(g, p, k, self.where, halves[-1] if halves else None, name="reduce_add_" + nm))
        landed = _seq_scatter(halves, kinds, name="reduce_scatter_" + tag)
        self.pending = (names, layers, landed, halves, kinds)
        return [halves[-1], summed]

    def flush(self, after):
        return self._sum_pending(after)

    def _sum_pending(self, after):
        if self.pending is None:
            return None
        for nm, l, r, h, k in zip(*self.pending):
            self.acc[nm], after = _sum_chips(r, h, k, self.where, l, LAYERS[nm], self.acc[nm], after, name="reduce_sum_" + nm)
        self.pending = None
        return after

    def finish(self):
        self._sum_pending(after=None)
        names = list(KIND)
        joined = _join_halves([self.acc[nm] for nm in names], [KIND[nm] for nm in names], name="reduce_pair_join")
        return dict(zip(names, joined))


def _headsum_matrix():
    r = lax.broadcasted_iota(jnp.int32, (128, 128), 0) // HD
    c = lax.broadcasted_iota(jnp.int32, (128, 128), 1) // HD
    return jnp.where(r == c, 1.0, 0.0).astype(BF16)


def kernel(x, norm_gains, w_qkv_a, w_o_a, w_q_b, w_o_b, kv_norm, w_kvf, b_f, w_up, conv_w, conv_b, w_down, loss_target, m_norm_gains, m_w_qkv_a, m_w_o_a, m_w_q_b, m_w_o_b, m_kv_norm, m_w_kvf, m_b_f, m_w_up, m_conv_w, m_conv_b, m_w_down, v_norm_gains, v_w_qkv_a, v_w_o_a, v_w_q_b, v_w_o_b, v_kv_norm, v_w_kvf, v_b_f, v_w_up, v_conv_w, v_conv_b, v_w_down):
    xi, yi, ci = lax.axis_index("x"), lax.axis_index("y"), lax.axis_index("c")
    chip = 2 * xi + yi
    where = jnp.stack([chip, ci]).astype(jnp.int32)
    ws = dict(norm_gains=norm_gains, w_qkv_a=w_qkv_a, w_o_a=w_o_a, w_q_b=w_q_b, w_o_b=w_o_b, kv_norm=kv_norm, w_kvf=w_kvf,
              b_f=b_f, w_up=w_up, conv_w=conv_w, conv_b=conv_b, w_down=w_down)
    ms = dict(norm_gains=m_norm_gains, w_qkv_a=m_w_qkv_a, w_o_a=m_w_o_a, w_q_b=m_w_q_b, w_o_b=m_w_o_b, kv_norm=m_kv_norm,
              w_kvf=m_w_kvf, b_f=m_b_f, w_up=m_w_up, conv_w=m_conv_w, conv_b=m_conv_b, w_down=m_w_down)
    vs = dict(norm_gains=v_norm_gains, w_qkv_a=v_w_qkv_a, w_o_a=v_w_o_a, w_q_b=v_w_q_b, w_o_b=v_w_o_b, kv_norm=v_kv_norm,
              w_kvf=v_w_kvf, b_f=v_b_f, w_up=v_w_up, conv_w=v_conv_w, conv_b=v_conv_b, w_down=v_w_down)

    small = jnp.concatenate([
        jnp.pad(norm_gains.reshape(16, 256), ((0, 0), (0, 1408 - 256))),
        jnp.pad(conv_w.reshape(12, 1408), ((0, 4), (0, 0)))], axis=0)
    big = [nm for nm in KIND if nm != "small"]
    half = {nm: ws[nm].astype(BF16) for nm in big}
    W = {nm: [None] * LAYERS[nm] for nm in big if nm != "w_kvf"}
    g_small = None
    groups = [("0a", [("w_qkv_a", 0), ("w_o_a", 0), ("small", 0)]), ("0b", [("w_up", 0)]), ("0c", [("w_down", 0)]),
              ("1a", [("w_qkv_a", 1), ("w_o_a", 1)]), ("1b", [("w_up", 1), ("w_down", 1)]),
              ("2", [("w_kvf", 0), ("w_q_b", 0), ("w_o_b", 0), ("w_up", 2), ("w_down", 2)]),
              ("3", [("w_q_b", 1), ("w_o_b", 1), ("w_up", 3), ("w_down", 3)])]
    for tag, group in groups:
        shards = [small if nm == "small" else half[nm] if nm == "w_kvf" else half[nm][i] for nm, i in group]
        got = _seq_gather(shards, [KIND[nm] for nm, _ in group], name="gather_layer" + tag, cid=1)
        for (nm, i), g in zip(group, got):
            if nm == "small":
                g_small = g
            elif nm == "w_kvf":
                W[nm] = g.transpose(1, 0, 2).reshape(D, 2 * D + 16)
            else:
                W[nm][i] = g.transpose(1, 0, 2).reshape(D, 3 * A_W) if nm == "w_qkv_a" else g
    gains = g_small[:, :16, :256].transpose(1, 0, 2).reshape(DEPTH, 4, 1, D)
    cw_full = g_small[:, 16:28, :].transpose(1, 0, 2).reshape(DEPTH, 3, 2 * D_FF)
    cb_full = conv_b.reshape(DEPTH, 1, 2 * D_FF)

    reducer = _Reducer(where)
    sq, dh = _fwd_bwd(x[0], loss_target[0], W, gains, cw_full, cb_full, kv_norm, b_f, reducer)
    loss = lax.psum(sq[0, 0] * (0.5 / D), ("x", "y", "c"))
    return _update(loss, dh[None], reducer.finish(), chip, ws, ms, vs)


def _fwd_bwd(h, target, W, gains, cw_full, cb_full, kv_norm, b_f, reduce):
    w_kv = W["w_kvf"][:, :2 * D]
    w_kvf_pad = jnp.pad(W["w_kvf"], ((0, 0), (0, 128 - 16)))
    w_f = w_kvf_pad[:, 2 * D:]
    kvn_g = kv_norm.reshape(1, D)
    bf_pad = jnp.pad(b_f, (0, 128 - 16)).reshape(1, 128)
    tabs = _rope_tables()
    headsum = _headsum_matrix()

    saved = []
    kv = zf = c_row = kvn = h_kv = None
    xn = _rms_fwd(h, gains[0][0], out_dtype=BF16, name="rms_in")
    for l in range(DEPTH):
        s = {"h": h}
        g = gains[l]
        s["xn"] = xn
        if l < N_A:
            qkv = _matmul(xn, W["w_qkv_a"][l], mode="nn", out_dtype=F32, name="mm_qkv", mnk=(T, 3 * A_W, D), tn=768)
            qkvp = _rope_fwd(qkv, tabs).reshape(3, 3, 2, T, 128)
            o_p, lse_p = _band_fwd(qkvp)
            att, o3, lse3 = _combine_fwd(o_p, lse_p)
            s.update(qkvp=qkvp, o3=o3, lse3=lse3, lse_p=lse_p, att=att)
            mix = _matmul(att, W["w_o_a"][l], mode="nn", out_dtype=F32, name="mm_oa", mnk=(T, D, A_W))
        else:
            j = l - N_A
            if l == N_A:
                h_kv = h
                kvn = _rms_fwd(h, kvn_g, out_dtype=BF16, name="rms_in")
                kv = _matmul(kvn, w_kv, mode="nn", out_dtype=BF16, name="mm_kv")
                zf = _matmul(kvn, w_f, mode="nn", out_dtype=F32, name="mm_f")
                cum = _gates_fwd(zf, bf_pad)[:, :16]
                c_row = cum.T.reshape(8, 2, T)
            q = _matmul(xn, W["w_q_b"][j], mode="nn", out_dtype=BF16, name="mm_qb", mnk=(T, D, D), alpha=HD ** -0.5)
            o = _fox_fwd(q, kv, c_row)
            s.update(q=q, o=o)
            mix = _matmul(o, W["w_o_b"][j], mode="nn", out_dtype=F32, name="mm_ob", mnk=(T, D, D))
        s["mix"] = mix
        h1, xn2 = _rms_res_in(mix, g[1], h, g[2], name="rms_res_in")
        a = _matmul(xn2, W["w_up"][l], mode="nn", out_dtype=F32, name="mm_up", mnk=(T, 2 * D_FF, D))
        u = _convgate_fwd(a, cw_full[l], cb_full[l])
        f = _matmul(u, W["w_down"][l], mode="nn", out_dtype=F32, name="mm_down", mnk=(T, D, D_FF), tm=1024, tk=D_FF)
        if l + 1 < DEPTH:
            h, xn = _rms_res_in(f, g[3], h1, gains[l + 1][0], name="rms_res_in")
        else:
            h = _rms_fwd(f, g[3], res=h1, out_dtype=F32, name="rms_res")
        s.update(h1=h1, xn2=xn2, a=a, u=u, f=f)
        saved.append(s)

    dh, sq = _loss_head(h, target)

    d_gains = [[None] * 4 for _ in range(DEPTH)]
    d_cw, d_cb = [None] * DEPTH, [None] * DEPTH
    fox_acc = None
    d_kvnorm = d_bf = token = df = None

    def dw(nm, a, b, **kw):
        return _matmul(a, b, mode="tn", out_dtype=BF16, name="mm_dw_" + nm, **kw)

    flush = getattr(reduce, "flush", lambda after: None)

    def slabs(full, width):
        return full.reshape(full.shape[0], N_CHIPS, width).transpose(1, 0, 2)

    for l in reversed(range(DEPTH)):
        s = saved[l]
        g = gains[l]
        if df is None:
            df, d_gains[l][3] = _rms_bwd(dh, s["f"], g[3], out_dtype=BF16, name="rms_bwd")
        du = _matmul(df, W["w_down"][l], mode="nt", out_dtype=F32, name="mm_down_dx", mnk=(T, D_FF, D), tn=256, after=token)
        g_down = dw("w_down", s["u"], df, tm=1408, tn=1024)
        da, d_cw[l], d_cb[l] = _convgate_bwd(s["a"], du, cw_full[l], cb_full[l])
        dxn2 = _matmul(da, W["w_up"][l], mode="nt", out_dtype=F32, name="mm_up_dx", mnk=(T, D, 2 * D_FF), tm=1024, tn=1024, tk=1408,
                       a_map=_halves_a)
        g_up = dw("w_up", s["xn2"], da, mnk=(D, 2 * D_FF, T), tn=1408, b_map=_halves_b)
        token = reduce([("w_down", l, g_down), ("w_up", l, g_up)], "ffn%d" % l)
        dh1, dmix, d_gains[l][2], d_gains[l][1] = _rms_bwd2(dxn2, s["h1"], g[2], dh, s["mix"], g[1], name="rms_bwd2")
        if l < N_A:
            datt = _matmul(dmix, W["w_o_a"][l], mode="nt", out_dtype=F32, name="mm_oa_dx", mnk=(T, A_W, D), tn=768, after=token)
            g_o = dw("w_o_a", s["att"], dmix, tm=768, tn=1024)
            do_p, dlt_p = _combine_bwd(datt, s["o3"], s["lse3"], headsum)
            dqkv = None
            for which, d in enumerate(_band_bwd(s["qkvp"], do_p, s["lse_p"], dlt_p)):
                dqkv = _rope_bwd(d, which, tabs, dqkv)
            dxn = _matmul(dqkv, W["w_qkv_a"][l], mode="nt", out_dtype=F32, name="mm_qkv_dx", mnk=(T, D, 3 * A_W), tm=1024, tn=1024, tk=3 * A_W,
                          after=[flush(dqkv)])
            g_qkv = dw("w_qkv_a", s["xn"], dqkv, tn=768)
            group = [("w_o_a", l, g_o), ("w_qkv_a", l, slabs(g_qkv, 576))]
        else:
            j = l - N_A
            do = _matmul(dmix, W["w_o_b"][j], mode="nt", out_dtype=BF16, name="mm_ob_dx", mnk=(T, D, D), after=token)
            g_o = dw("w_o_b", s["o"], dmix, tn=1024)
            dq, *fox_acc = _fox_bwd(s["q"], kv, do, c_row, fox_acc)
            dxn = _matmul(dq, W["w_q_b"][j], mode="nt", out_dtype=F32, name="mm_qb_dx", mnk=(T, D, D), after=[flush(dq)])
            g_q = dw("w_q_b", s["xn"], dq, tn=1024)
            group = [("w_o_b", j, g_o), ("w_q_b", j, g_q)]
        if l > 0 and l != N_A:
            dh, df, d_gains[l][0], d_gains[l - 1][3] = _rms_bwd2(dxn, s["h"], g[0], dh1, saved[l - 1]["f"], gains[l - 1][3],
                                                                 name="rms_bwd2")
        else:
            dh, d_gains[l][0] = _rms_bwd(dxn, s["h"], g[0], dres=dh1, out_dtype=F32, name="rms_bwd_res")
            df = None
        if l == N_A:
            dk, dv, dck = fox_acc
            dc16 = -dck[:, :2, :].reshape(16, T).T
            dzf, d_bf = _gates_bwd(jnp.pad(dc16, ((0, 0), (0, 128 - 16))), zf, bf_pad)
            dkvf = jnp.concatenate([dk.astype(BF16), dv.astype(BF16), dzf], axis=1)
            g_kvf = _matmul(kvn, dkvf, mode="tn", out_dtype=BF16, name="mm_kvf_dw", tm=512, tn=2 * D + 128)[:, :2 * D + 16]
            dkvn = _matmul(dkvf, w_kvf_pad, mode="nt", out_dtype=F32, name="mm_kvf_dx", tm=1024, tn=1024, tk=2 * D + 128)
            dh, d_kvnorm = _rms_bwd(dkvn, h_kv, kvn_g, dres=dh, out_dtype=F32, name="rms_bwd_res")
            group.append(("w_kvf", 0, slabs(g_kvf, 516)))
        token = reduce(group, "mix%d" % l)
    small_flat = jnp.concatenate([
        jnp.stack([jnp.stack(r) for r in d_gains]).reshape(-1),
        jnp.stack(d_cw).transpose(0, 2, 1, 3).reshape(-1),
        jnp.stack(d_cb).reshape(-1),
        d_kvnorm.reshape(-1), d_bf[0, :16]])
    small = jnp.pad(small_flat, (0, 2 * N_CHIPS * SMALL_ROWS * SMALL_W - small_flat.shape[0]))
    reduce([("small", 0, small.reshape(N_CHIPS, 2 * SMALL_ROWS, SMALL_W))], "small")
    return sq, dh


def _update(loss, grad_x, reduced, chip, ws, ms, vs):
    red_s = reduced.pop("small")
    buf_s = lax.dynamic_update_slice(jnp.zeros((2, N_CHIPS, SMALL_ROWS, SMALL_W), F32), red_s.reshape(2, 1, SMALL_ROWS, SMALL_W),
                                     (0, chip, 0, 0))
    (all_s,) = _allgather([buf_s], ["slab"], name="gather_small_grads")
    sflat = all_s.transpose(1, 0, 2, 3).reshape(-1)

    grads = {nm: r.reshape(ws[nm].shape) for nm, r in reduced.items()}
    o = 0
    g_gains_full = sflat[o:o + 16 * D].reshape(DEPTH, 4, D); o += 16 * D
    g_cw_full = sflat[o:o + 12 * 2 * D_FF].reshape(DEPTH, 3, 2 * D_FF); o += 12 * 2 * D_FF
    grads["conv_b"] = sflat[o:o + 4 * 2 * D_FF].reshape(DEPTH, 2 * D_FF); o += 4 * 2 * D_FF
    grads["kv_norm"] = sflat[o:o + D]; o += D
    grads["b_f"] = sflat[o:o + 16]
    grads["norm_gains"] = lax.dynamic_slice_in_dim(g_gains_full, chip * 256, 256, axis=2)
    grads["conv_w"] = lax.dynamic_slice_in_dim(g_cw_full, chip * 1408, 1408, axis=2)

    names = ["norm_gains", "w_qkv_a", "w_o_a", "w_q_b", "w_o_b", "kv_norm", "w_kvf", "b_f", "w_up", "conv_w", "conv_b", "w_down"]
    deltas, new_m, new_v = {}, {}, {}
    for nm in names:
        shp = ws[nm].shape
        two = (math.prod(shp[:-1]), shp[-1]) if len(shp) > 1 else (1, shp[0])
        d, m2, v2 = _adamw(ws[nm].reshape(two), ms[nm].reshape(two), vs[nm].reshape(two), grads[nm].reshape(two),
                           name="adamw_" + nm)
        deltas[nm], new_m[nm], new_v[nm] = d.reshape(shp), m2.reshape(shp), v2.reshape(shp)

    return (loss, grad_x, *[grads[nm] for nm in names], *[deltas[nm] for nm in names],
            *[new_m[nm] for nm in names], *[new_v[nm] for nm in names])
```

```python
import math

import jax
import jax.numpy as jnp
from jax import lax
from jax.experimental import pallas as pl
from jax.experimental.pallas import tpu as pltpu
from jax.experimental.pallas import tpu_sc as plsc

F32 = jnp.float32
BF16 = jnp.bfloat16
MESH = pl.DeviceIdType.MESH
ANY = pl.BlockSpec(memory_space=pl.ANY)

T = 2048
D = 1024
HD = 64
DEPTH = 4
N_A = 2
A_W = 768
GW = 256
DIL = (1, 4, 16)
BLK = 128
D_FF = 2816
ROPE_THETA = 500000.0
EPS = 1e-6
NEG = -1e30
N_CHIPS = 4
FQ = 256
CT = 128
VMEM_BIG = 48 * 1024 * 1024

ADAM_LR, ADAM_B1, ADAM_B2, ADAM_EPS, ADAM_WD, ADAM_STEP = 0.001, 0.9, 0.999, 1e-08, 0.01, 10

NN = (((1,), (0,)), ((), ()))
NT = (((1,), (1,)), ((), ()))
TN = (((0,), (0,)), ((), ()))


def _dot(a, b, dims):
    return lax.dot_general(a, b, dims, preferred_element_type=F32)


def _pick(dim, pref):
    if dim <= pref:
        return dim
    best = None
    for t in range(128, pref + 1, 128):
        if dim % t == 0:
            best = t
    assert best is not None, (dim, pref)
    return best


def _params(sem=None, vmem=None):
    kw = {}
    if sem is not None:
        kw["dimension_semantics"] = sem
    if vmem is not None:
        kw["vmem_limit_bytes"] = vmem
    return pltpu.CompilerParams(**kw)


def _matmul(a, b, *, mode, out_dtype, name, mnk=None, alpha=None, tm=2048, tn=512, tk=2048, a_map=None, b_map=None, after=None):
    if mnk is not None:
        M, N, K = mnk
    elif mode == "nn":
        (M, K), (_, N) = a.shape, b.shape
    elif mode == "nt":
        (M, K), (N, _) = a.shape, b.shape
    else:
        (K, M), (_, N) = a.shape, b.shape
    tm, tn, tk = _pick(M, tm), _pick(N, tn), _pick(K, tk)
    nk = K // tk
    dims = {"nn": NN, "nt": NT, "tn": TN}[mode]
    after = [t for t in (after or ()) if t is not None]
    n_in = 2 + len(after)

    def body(*refs):
        a_ref, b_ref = refs[0], refs[1]
        o_ref = refs[n_in]
        k = pl.program_id(2)

        def finish(r):
            if alpha is not None:
                r = r * alpha
            o_ref[...] = r.astype(out_dtype)

        def product():
            return _dot(a_ref[...], b_ref[...], dims)

        if nk == 1:
            finish(product())
            return
        acc_ref = refs[n_in + 1]

        @pl.when(k == 0)
        def _():
            acc_ref[...] = product()

        @pl.when((k > 0) & (k < nk - 1))
        def _():
            acc_ref[...] += product()

        @pl.when(k == nk - 1)
        def _():
            finish(acc_ref[...] + product())

    a_blk = (tk, tm) if mode == "tn" else (tm, tk)
    b_blk = (tn, tk) if mode == "nt" else (tk, tn)
    if a_map is not None:
        a_spec = pl.BlockSpec((None,) + a_blk, a_map(tm, tn, tk))
    elif mode == "tn":
        a_spec = pl.BlockSpec(a_blk, lambda i, j, k: (k, i))
    else:
        a_spec = pl.BlockSpec(a_blk, lambda i, j, k: (i, k))
    if b_map is not None:
        b_spec = pl.BlockSpec((None,) + b_blk, b_map(tm, tn, tk))
    elif mode == "nt":
        b_spec = pl.BlockSpec(b_blk, lambda i, j, k: (j, k))
    else:
        b_spec = pl.BlockSpec(b_blk, lambda i, j, k: (k, j))
    return pl.pallas_call(
        body,
        grid=(M // tm, N // tn, nk),
        in_specs=[a_spec, b_spec] + [ANY] * len(after),
        out_specs=pl.BlockSpec((tm, tn), lambda i, j, k: (i, j)),
        out_shape=jax.ShapeDtypeStruct((M, N), out_dtype),
        scratch_shapes=[pltpu.VMEM((tm, tn), F32)] if nk > 1 else [],
        compiler_params=_params(("parallel", "parallel", "arbitrary"), VMEM_BIG),
        name=name,
    )(a, b, *after)


def _rms_fwd(x, g, *, out_dtype, name, res=None, tr=256):
    n, d = x.shape

    def body(*refs):
        x_ref, g_ref = refs[0], refs[1]
        o_ref = refs[-1]
        xv = x_ref[...].astype(F32)
        y = xv * lax.rsqrt(jnp.mean(xv * xv, axis=-1, keepdims=True) + EPS) * g_ref[...]
        if res is not None:
            y = y + refs[2][...]
        o_ref[...] = y.astype(out_dtype)

    row = pl.BlockSpec((tr, d), lambda i: (i, 0))
    vec = pl.BlockSpec((1, d), lambda i: (0, 0))
    ins = [x, g] + ([] if res is None else [res])
    specs = [row, vec] + ([] if res is None else [row])
    return pl.pallas_call(
        body, grid=(n // tr,), in_specs=specs, out_specs=row,
        out_shape=jax.ShapeDtypeStruct((n, d), out_dtype),
        compiler_params=_params(("parallel",)), name=name,
    )(*ins)


def _rms_bwd(dy, x, g, *, out_dtype, name, dres=None, tr=512):
    n, d = x.shape

    def body(*refs):
        dy_ref, x_ref, g_ref = refs[0], refs[1], refs[2]
        dx_ref, dg_ref = refs[-2], refs[-1]
        xv = x_ref[...].astype(F32)
        dyv = dy_ref[...].astype(F32)
        rstd = lax.rsqrt(jnp.mean(xv * xv, axis=-1, keepdims=True) + EPS)
        xhat = xv * rstd
        dxh = dyv * g_ref[...]
        dx = rstd * (dxh - xhat * jnp.mean(dxh * xhat, axis=-1, keepdims=True))
        if dres is not None:
            dx = dx + refs[3][...]
        dx_ref[...] = dx.astype(out_dtype)

        @pl.when(pl.program_id(0) == 0)
        def _():
            dg_ref[...] = jnp.zeros_like(dg_ref)

        dg_ref[...] += jnp.sum(dyv * xhat, axis=0, keepdims=True)

    row = pl.BlockSpec((tr, d), lambda i: (i, 0))
    vec = pl.BlockSpec((1, d), lambda i: (0, 0))
    ins = [dy, x, g] + ([] if dres is None else [dres])
    specs = [row, row, vec] + ([] if dres is None else [row])
    return pl.pallas_call(
        body, grid=(n // tr,), in_specs=specs, out_specs=[row, vec],
        out_shape=[jax.ShapeDtypeStruct((n, d), out_dtype), jax.ShapeDtypeStruct((1, d), F32)],
        compiler_params=_params(("arbitrary",), VMEM_BIG), name=name,
    )(*ins)


def _rms_res_in(x, g_res, res, g_in, *, name, tr=512):
    n, d = x.shape

    def body(x_ref, gr_ref, r_ref, gi_ref, h_ref, n_ref):
        xv = x_ref[...].astype(F32)
        h = r_ref[...] + xv * lax.rsqrt(jnp.mean(xv * xv, axis=-1, keepdims=True) + EPS) * gr_ref[...]
        h_ref[...] = h
        n_ref[...] = (h * lax.rsqrt(jnp.mean(h * h, axis=-1, keepdims=True) + EPS) * gi_ref[...]).astype(BF16)

    row = pl.BlockSpec((tr, d), lambda i: (i, 0))
    vec = pl.BlockSpec((1, d), lambda i: (0, 0))
    return pl.pallas_call(
        body, grid=(n // tr,), in_specs=[row, vec, row, vec], out_specs=[row, row],
        out_shape=[jax.ShapeDtypeStruct((n, d), F32), jax.ShapeDtypeStruct((n, d), BF16)],
        compiler_params=_params(("parallel",), VMEM_BIG), name=name,
    )(x, g_res, res, g_in)


def _rms_bwd2(dy, x, g, dres, x2, g2, *, name, tr=512):
    n, d = x.shape

    def one(dyv, xv, gv):
        rstd = lax.rsqrt(jnp.mean(xv * xv, axis=-1, keepdims=True) + EPS)
        xhat = xv * rstd
        dxh = dyv * gv
        return rstd * (dxh - xhat * jnp.mean(dxh * xhat, axis=-1, keepdims=True)), jnp.sum(dyv * xhat, axis=0, keepdims=True)

    def body(dy_ref, x_ref, g_ref, r_ref, x2_ref, g2_ref, dx_ref, d2_ref, dg_ref, dg2_ref):
        dx, dg = one(dy_ref[...].astype(F32), x_ref[...].astype(F32), g_ref[...])
        dx = dx + r_ref[...]
        dx_ref[...] = dx
        d2, dg2 = one(dx, x2_ref[...].astype(F32), g2_ref[...])
        d2_ref[...] = d2.astype(BF16)

        @pl.when(pl.program_id(0) == 0)
        def _():
            dg_ref[...] = jnp.zeros_like(dg_ref)
            dg2_ref[...] = jnp.zeros_like(dg2_ref)

        dg_ref[...] += dg
        dg2_ref[...] += dg2

    row = pl.BlockSpec((tr, d), lambda i: (i, 0))
    vec = pl.BlockSpec((1, d), lambda i: (0, 0))
    return pl.pallas_call(
        body, grid=(n // tr,), in_specs=[row, row, vec, row, row, vec], out_specs=[row, row, vec, vec],
        out_shape=[jax.ShapeDtypeStruct((n, d), F32), jax.ShapeDtypeStruct((n, d), BF16),
                   jax.ShapeDtypeStruct((1, d), F32), jax.ShapeDtypeStruct((1, d), F32)],
        compiler_params=_params(("arbitrary",), VMEM_BIG), name=name,
    )(dy, x, g, dres, x2, g2)


def _loss_head(h, target, *, tr=256):
    n, d = h.shape

    def body(h_ref, t_ref, dh_ref, s_ref):
        err = h_ref[...] - t_ref[...]
        dh_ref[...] = err * (1.0 / d)

        @pl.when(pl.program_id(0) == 0)
        def _():
            s_ref[...] = jnp.zeros_like(s_ref)

        s_ref[...] += jnp.sum(err * err)

    row = pl.BlockSpec((tr, d), lambda i: (i, 0))
    acc = pl.BlockSpec((8, 128), lambda i: (0, 0))
    return pl.pallas_call(
        body, grid=(n // tr,), in_specs=[row, row], out_specs=[row, acc],
        out_shape=[jax.ShapeDtypeStruct((n, d), F32), jax.ShapeDtypeStruct((8, 128), F32)],
        compiler_params=_params(("arbitrary",)), name="loss_head",
    )(h, target)


def _rope_tables():
    pos = jnp.arange(T, dtype=F32)
    inv = ROPE_THETA ** (-jnp.arange(0, 16, 2, dtype=F32) / 16)
    ang = pos[:, None] * inv[None, :]
    cos, sin = jnp.cos(ang), jnp.sin(ang)
    one = jnp.ones((T, HD - 16), F32)
    zero8 = jnp.zeros((T, 8), F32)
    zero = jnp.zeros((T, HD - 16), F32)
    c = jnp.concatenate([cos, cos, one], axis=1)
    s1 = jnp.concatenate([zero8, sin, zero], axis=1)
    s2 = jnp.concatenate([-sin, zero8, zero], axis=1)
    c, s1, s2 = (jnp.concatenate([t, t], axis=1) for t in (c, s1, s2))
    scale = HD ** -0.5
    return (jnp.stack([c * scale, c, jnp.ones_like(c)]), jnp.stack([s1 * scale, s1, jnp.zeros_like(c)]),
            jnp.stack([s2 * scale, s2, jnp.zeros_like(c)]))


def _row_chunks(r):
    if r == 1:
        n = 4
        return [(slice(i * (T // n), (i + 1) * (T // n)),) * 2 for i in range(n)]
    per = T // r
    return [(pl.ds(j, per, stride=r), slice(j * per, (j + 1) * per)) for j in range(r)]


def _rope_fwd(qkv, tabs):
    def body(x_ref, c_ref, s1_ref, s2_ref, o_ref):
        g = lax.rem(lax.div(pl.program_id(0), 2), 3)
        for gi, r in enumerate(DIL):
            @pl.when(g == gi)
            def _(r=r):
                for tok, prm in _row_chunks(r):
                    x = x_ref[tok, :]
                    y = x * c_ref[tok, :] + pltpu.roll(x, 8, 1) * s1_ref[tok, :] + pltpu.roll(x, 120, 1) * s2_ref[tok, :]
                    o_ref[prm, :] = y.astype(BF16)

    tab = pl.BlockSpec((None, T, 128), lambda b: (lax.div(b, 6), 0, 0))
    return pl.pallas_call(
        body, grid=(18,), in_specs=[pl.BlockSpec((T, 128), lambda b: (0, b)), tab, tab, tab],
        out_specs=pl.BlockSpec((None, T, 128), lambda b: (b, 0, 0)), out_shape=jax.ShapeDtypeStruct((18, T, 128), BF16),
        compiler_params=_params(("parallel",)), name="rope_fwd",
    )(qkv, *tabs)


def _rope_bwd(d, which, tabs, out_buf):
    def body(d_ref, c_ref, s1_ref, s2_ref, *rest):
        o_ref, tok_ref = rest[-2], rest[-1]
        g = lax.div(pl.program_id(0), 2)
        for gi, r in enumerate(DIL):
            @pl.when(g == gi)
            def _(r=r):
                for tok, prm in _row_chunks(r):
                    tok_ref[tok, :] = d_ref[prm, :]
                for rows, _ in _row_chunks(1):
                    gx = tok_ref[rows, :]
                    y = gx * c_ref[rows, :] + pltpu.roll(gx * s1_ref[rows, :], 120, 1) + pltpu.roll(gx * s2_ref[rows, :], 8, 1)
                    o_ref[rows, :] = y.astype(BF16)

    tab = pl.BlockSpec((None, T, 128), lambda b: (which, 0, 0))
    ins = [d, *tabs] + ([] if out_buf is None else [out_buf])
    specs = [pl.BlockSpec((None, None, T, 128), lambda b: (lax.div(b, 2), lax.rem(b, 2), 0, 0)), tab, tab, tab]
    return pl.pallas_call(
        body, grid=(6,), in_specs=specs + ([] if out_buf is None else [ANY]),
        out_specs=pl.BlockSpec((T, 128), lambda b: (0, 6 * which + b)),
        out_shape=jax.ShapeDtypeStruct((T, 3 * A_W), BF16), scratch_shapes=[pltpu.VMEM((T, 128), F32)],
        input_output_aliases={} if out_buf is None else {4: 0},
        compiler_params=_params(("arbitrary",)), name="rope_bwd",
    )(*ins)


def _head_mask(x, lane_lo):
    lane = lax.broadcasted_iota(jnp.int32, x.shape, 1)
    keep = (lane < HD) if lane_lo else (lane >= HD)
    return jnp.where(keep, x.astype(F32), 0.0).astype(BF16)


def _band_scalars(g):
    b = pl.program_id(0)
    nbs = (T // BLK) // DIL[g]
    has_prev = jnp.where((b & (nbs - 1)) != 0, 1, 0)
    next_ok = jnp.where(((b + 1) & (nbs - 1)) != 0, 1, 0)
    return has_prev, next_ok


def _band_mask_q(has_prev):
    row = lax.broadcasted_iota(jnp.int32, (BLK, 2 * BLK), 0)
    col = lax.broadcasted_iota(jnp.int32, (BLK, 2 * BLK), 1)
    return ((col < BLK) & (col >= row) & (has_prev == 1)) | ((col >= BLK) & (col - BLK <= row))


def _band_mask_k(next_ok):
    row = lax.broadcasted_iota(jnp.int32, (2 * BLK, BLK), 0)
    col = lax.broadcasted_iota(jnp.int32, (2 * BLK, BLK), 1)
    return ((row < BLK) & (col <= row)) | ((row >= BLK) & (col >= row - BLK) & (next_ok == 1))


def _band_spec(step, which=None):
    nb = T // BLK
    at = {"cur": lambda b: b, "prev": lambda b: jnp.maximum(b - 1, 0), "next": lambda b: jnp.minimum(b + 1, nb - 1)}[step]
    if which is None:
        return pl.BlockSpec((3, 2, BLK, 128), lambda b: (0, 0, at(b), 0))
    return pl.BlockSpec((None, 3, 2, BLK, 128), lambda b: (which, 0, 0, at(b), 0))


def _band_fwd(qkv):
    nb = T // BLK

    def body(q_ref, kc_ref, kp_ref, vc_ref, vp_ref, o_ref, l_ref):
        lane = lax.broadcasted_iota(jnp.int32, (BLK, 128), 1)
        for g in range(3):
            has_prev, _ = _band_scalars(g)
            mask = _band_mask_q(has_prev)
            for p in range(2):
                qp = q_ref[g, p]
                kcat = jnp.concatenate([kp_ref[g, p], kc_ref[g, p]], axis=0)
                vcat = jnp.concatenate([vp_ref[g, p], vc_ref[g, p]], axis=0)
                o_acc = jnp.zeros((BLK, 128), F32)
                lse = jnp.zeros((BLK, 128), F32)
                for e in range(2):
                    s = _dot(_head_mask(qp, e == 0), kcat, NT)
                    s = jnp.where(mask, s, NEG)
                    m = jnp.max(s, axis=-1, keepdims=True)
                    pr = jnp.exp(s - m)
                    l = jnp.sum(pr, axis=-1, keepdims=True)
                    o_acc = o_acc + _dot(pr.astype(BF16), _head_mask(vcat, e == 0), NN) / l
                    lse = jnp.where((lane < HD) if e == 0 else (lane >= HD), m + jnp.log(l), lse)
                o_ref[g, p] = o_acc
                l_ref[g, p] = lse

    out = _band_spec("cur")
    shp = jax.ShapeDtypeStruct((3, 2, T, 128), F32)
    return pl.pallas_call(
        body, grid=(nb,),
        in_specs=[_band_spec("cur", 0), _band_spec("cur", 1), _band_spec("prev", 1), _band_spec("cur", 2), _band_spec("prev", 2)],
        out_specs=[out, out], out_shape=[shp, shp],
        compiler_params=_params(("parallel",)), name="band_fwd",
    )(qkv, qkv, qkv, qkv, qkv)


def _band_bwd(qkv, do, lse, dlt):
    nb = T // BLK

    def body(qc_ref, qn_ref, kc_ref, kp_ref, vc_ref, vp_ref, doc_ref, don_ref, lc_ref, ln_ref, dc_ref, dn_ref,
             dq_ref, dk_ref, dv_ref):
        for g in range(3):
            has_prev, next_ok = _band_scalars(g)
            mask_q = _band_mask_q(has_prev)
            mask_k = _band_mask_k(next_ok)
            for p in range(2):
                qc, qn = qc_ref[g, p], qn_ref[g, p]
                doc, don = doc_ref[g, p], don_ref[g, p]
                kc, vc = kc_ref[g, p], vc_ref[g, p]
                kcat = jnp.concatenate([kp_ref[g, p], kc], axis=0)
                vcat = jnp.concatenate([vp_ref[g, p], vc], axis=0)
                qcat = jnp.concatenate([qc, qn], axis=0)
                docat = jnp.concatenate([doc, don], axis=0)
                dq = jnp.zeros((BLK, 128), F32)
                dk = jnp.zeros((BLK, 128), F32)
                dv = jnp.zeros((BLK, 128), F32)
                for e in range(2):
                    lo = e == 0
                    col = slice(HD * e, HD * e + 1)
                    lse_c, lse_n = lc_ref[g, p, :, col], ln_ref[g, p, :, col]
                    dl_c, dl_n = dc_ref[g, p, :, col], dn_ref[g, p, :, col]
                    s = jnp.where(mask_q, _dot(_head_mask(qc, lo), kcat, NT), NEG)
                    pr = jnp.exp(s - lse_c)
                    dp = _dot(_head_mask(doc, lo), vcat, NT)
                    ds = pr * (dp - dl_c)
                    dq = dq + _dot(ds.astype(BF16), _head_mask(kcat, lo), NN)
                    qm, dom = _head_mask(qcat, lo), _head_mask(docat, lo)
                    s2 = jnp.where(mask_k, _dot(qm, kc, NT), NEG)
                    p2 = jnp.exp(s2 - jnp.concatenate([lse_c, lse_n], axis=0))
                    dv = dv + _dot(p2.astype(BF16), dom, TN)
                    dp2 = _dot(dom, vc, NT)
                    ds2 = p2 * (dp2 - jnp.concatenate([dl_c, dl_n], axis=0))
                    dk = dk + _dot(ds2.astype(BF16), qm, TN)
                dq_ref[g, p] = dq
                dk_ref[g, p] = dk
                dv_ref[g, p] = dv

    cur, nxt = _band_spec("cur"), _band_spec("next")
    shp = jax.ShapeDtypeStruct((3, 2, T, 128), F32)
    return pl.pallas_call(
        body, grid=(nb,),
        in_specs=[_band_spec("cur", 0), _band_spec("next", 0), _band_spec("cur", 1), _band_spec("prev", 1),
                  _band_spec("cur", 2), _band_spec("prev", 2), cur, nxt, cur, nxt, cur, nxt],
        out_specs=[cur, cur, cur], out_shape=[shp, shp, shp],
        compiler_params=_params(("parallel",)), name="band_bwd",
    )(qkv, qkv, qkv, qkv, qkv, qkv, do, do, lse, lse, dlt, dlt)


def _split3(x):
    hi = x.astype(BF16)
    r = x - hi.astype(F32)
    mid = r.astype(BF16)
    lo = (r - mid.astype(F32)).astype(BF16)
    return hi, mid, lo


def _dot3(x, m, dims=NN):
    hi, mid, lo = _split3(x)
    return _dot(hi, m, dims) + _dot(mid, m, dims) + _dot(lo, m, dims)


def _combine_weights(lses):
    l0, l1, l2 = lses
    m = jnp.maximum(jnp.maximum(l0, l1), l2)
    e = [jnp.exp(l0 - m), jnp.exp(l1 - m), jnp.exp(l2 - m)]
    inv = 1.0 / (e[0] + e[1] + e[2])
    return [ei * inv for ei in e]


CR = 256


def _combine_fwd(o, lse):
    def body(o_ref, l_ref, att_ref, o3_ref, l3_ref):
        for g, r in enumerate(DIL):
            for p in range(2):
                for tok, prm in _row_chunks(r):
                    o3_ref[g, p, tok, :] = o_ref[g, p, prm, :]
                    l3_ref[g, p, tok, :] = l_ref[g, p, prm, :]
        for i in range(T // CR):
            rows = slice(i * CR, (i + 1) * CR)
            for p in range(2):
                alpha = _combine_weights([l3_ref[g, p, rows, :] for g in range(3)])
                for g in range(3):
                    att_ref[rows, g * GW + p * 128: g * GW + (p + 1) * 128] = (o3_ref[g, p, rows, :] * alpha[g]).astype(BF16)

    shp = jax.ShapeDtypeStruct((3, 2, T, 128), F32)
    return pl.pallas_call(
        body, out_shape=[jax.ShapeDtypeStruct((T, A_W), BF16), shp, shp],
        compiler_params=_params(vmem=VMEM_BIG), name="combine_fwd",
    )(o, lse)


def _combine_bwd(datt, o3, l3, headsum):
    def body(d_ref, o_ref, l_ref, hs_ref, do_ref, dl_ref, tdo_ref, tdl_ref):
        hs = hs_ref[...]
        for p in range(2):
            for i in range(T // CR):
                rows = slice(i * CR, (i + 1) * CR)
                alpha = _combine_weights([l_ref[g, p, rows, :] for g in range(3)])
                total = jnp.zeros((CR, 128), F32)
                for g in range(3):
                    dg = d_ref[rows, g * GW + p * 128: g * GW + (p + 1) * 128]
                    tdo_ref[g, rows, :] = dg * alpha[g]
                    total = total + alpha[g] * _dot3(dg * o_ref[g, p, rows, :], hs)
                for g in range(3):
                    tdl_ref[g, rows, :] = alpha[g] * total
            for g, r in enumerate(DIL):
                for tok, prm in _row_chunks(r):
                    do_ref[g, p, prm, :] = tdo_ref[g, tok, :].astype(BF16)
                    dl_ref[g, p, prm, :] = tdl_ref[g, tok, :]

    return pl.pallas_call(
        body, out_shape=[jax.ShapeDtypeStruct((3, 2, T, 128), BF16), jax.ShapeDtypeStruct((3, 2, T, 128), F32)],
        scratch_shapes=[pltpu.VMEM((3, T, 128), F32), pltpu.VMEM((3, T, 128), F32)],
        compiler_params=_params(vmem=VMEM_BIG), name="combine_bwd",
    )(datt, o3, l3, headsum)


def _fox_scores(qm, k_ref, ck_ref, e, i, n):
    s = _dot(qm, k_ref[0:n, :], NT) - ck_ref[0, e:e + 1, 0:n]
    row = lax.broadcasted_iota(jnp.int32, (FQ, FQ), 0)
    col = lax.broadcasted_iota(jnp.int32, (FQ, FQ), 1)
    diag = jnp.where(col <= row, s[:, n - FQ:], NEG)
    m = jnp.max(diag, axis=-1, keepdims=True)
    if i == 0:
        pr = jnp.exp(diag - m)
        return pr, jnp.sum(pr, axis=-1, keepdims=True)
    past = s[:, :n - FQ]
    m = jnp.maximum(m, jnp.max(past, axis=-1, keepdims=True))
    p_past, p_diag = jnp.exp(past - m), jnp.exp(diag - m)
    l = jnp.sum(p_past, axis=-1, keepdims=True) + jnp.sum(p_diag, axis=-1, keepdims=True)
    return jnp.concatenate([p_past, p_diag], axis=1), l


def _fox_fwd(q, kv, c_row):
    def body(q_ref, k_ref, v_ref, cr_ref, o_ref, vm_ref):
        for e in range(2):
            vm_ref[e] = _head_mask(v_ref[...], e == 0)
        for i in range(T // FQ):
            n = (i + 1) * FQ
            rows = slice(i * FQ, n)
            acc = jnp.zeros((FQ, 128), F32)
            for e in range(2):
                qm = _head_mask(q_ref[rows, :], e == 0)
                pr, l = _fox_scores(qm, k_ref, cr_ref, e, i, n)
                acc = acc + _dot(pr.astype(BF16), vm_ref[e, 0:n, :], NN) / l
            o_ref[rows, :] = acc.astype(BF16)

    pair = pl.BlockSpec((T, 128), lambda p: (0, p))
    return pl.pallas_call(
        body, grid=(D // 128,),
        in_specs=[pair, pair, pl.BlockSpec((T, 128), lambda p: (0, D // 128 + p)), pl.BlockSpec((1, 2, T), lambda p: (p, 0, 0))],
        out_specs=pair, out_shape=jax.ShapeDtypeStruct((T, D), BF16),
        scratch_shapes=[pltpu.VMEM((2, T, 128), BF16)],
        compiler_params=_params(("parallel",), VMEM_BIG), name="fox_fwd",
    )(q, kv, kv, c_row)


def _fox_bwd(q, kv, do, c_row, init):
    def body(q_ref, k_ref, v_ref, do_ref, cr_ref, *rest):
        dq_ref, dk_ref, dv_ref, dck_ref, km_ref = rest[-5:]
        for o_ref, i_ref in zip((dk_ref, dv_ref, dck_ref), rest[:-5] or (None,) * 3):
            o_ref[...] = jnp.zeros_like(o_ref) if i_ref is None else i_ref[...]
        for e in range(2):
            km_ref[e] = _head_mask(k_ref[...], e == 0)
        for i in range(T // FQ):
            n = (i + 1) * FQ
            rows = slice(i * FQ, n)
            dq = jnp.zeros((FQ, 128), F32)
            dk = jnp.zeros((n, 128), F32)
            dv = jnp.zeros((n, 128), F32)
            for e in range(2):
                qm = _head_mask(q_ref[rows, :], e == 0)
                dom = _head_mask(do_ref[rows, :], e == 0)
                pr, l = _fox_scores(qm, k_ref, cr_ref, e, i, n)
                pr = pr * (1.0 / l)
                dp = _dot(dom, v_ref[0:n, :], NT)
                ds = pr * (dp - jnp.sum(pr * dp, axis=-1, keepdims=True))
                dsb = ds.astype(BF16)
                dq = dq + _dot(dsb, km_ref[e, 0:n, :], NN)
                dk = dk + _dot(dsb, qm, TN)
                dv = dv + _dot(pr.astype(BF16), dom, TN)
                dck_ref[0, e:e + 1, 0:n] += jnp.sum(ds, axis=0, keepdims=True)
            dk_ref[0:n, :] += dk
            dv_ref[0:n, :] += dv
            dq_ref[rows, :] = (dq * HD ** -0.5).astype(BF16)

    pair = pl.BlockSpec((T, 128), lambda p: (0, p))
    ck = pl.BlockSpec((1, 8, T), lambda p: (p, 0, 0))
    return pl.pallas_call(
        body, grid=(D // 128,),
        in_specs=[pair, pair, pl.BlockSpec((T, 128), lambda p: (0, D // 128 + p)), pair,
                  pl.BlockSpec((1, 2, T), lambda p: (p, 0, 0))] + ([] if init is None else [pair, pair, ck]),
        out_specs=[pair, pair, pair, ck],
        out_shape=[jax.ShapeDtypeStruct((T, D), BF16), jax.ShapeDtypeStruct((T, D), F32), jax.ShapeDtypeStruct((T, D), F32),
                   jax.ShapeDtypeStruct((D // 128, 8, T), F32)],
        scratch_shapes=[pltpu.VMEM((2, T, 128), BF16)],
        compiler_params=_params(("parallel",), VMEM_BIG), name="fox_bwd",
    )(q, kv, kv, do, c_row, *(init or ()))


def _tri(lower):
    r = lax.broadcasted_iota(jnp.int32, (BLK, BLK), 0)
    c = lax.broadcasted_iota(jnp.int32, (BLK, BLK), 1)
    return jnp.where((c <= r) if lower else (c >= r), 1.0, 0.0).astype(BF16)


def _gates_fwd(z, b):
    def body(z_ref, b_ref, c_ref):
        tri = _tri(True)
        carry = jnp.zeros((1, 128), F32)
        for i in range(T // BLK):
            rows = slice(i * BLK, (i + 1) * BLK)
            x = z_ref[rows, :] + b_ref[...]
            logf = jnp.minimum(x, 0.0) - jnp.log(1.0 + jnp.exp(-jnp.abs(x)))
            hi, mid, lo = _split3(logf)
            y = _dot(tri, hi, NN) + _dot(tri, mid, NN) + _dot(tri, lo, NN) + carry
            c_ref[rows, :] = y
            carry = y[BLK - 1:BLK, :]

    return pl.pallas_call(body, out_shape=jax.ShapeDtypeStruct((T, 128), F32), name="gates_fwd")(z, b)


def _gates_bwd(dc, z, b):
    def body(dc_ref, z_ref, b_ref, dz_ref, db_ref):
        tri = _tri(False)
        carry = jnp.zeros((1, 128), F32)
        db = jnp.zeros((1, 128), F32)
        for i in reversed(range(T // BLK)):
            rows = slice(i * BLK, (i + 1) * BLK)
            hi, mid, lo = _split3(dc_ref[rows, :])
            dlogf = _dot(tri, hi, NN) + _dot(tri, mid, NN) + _dot(tri, lo, NN) + carry
            carry = dlogf[0:1, :]
            x = z_ref[rows, :] + b_ref[...]
            dz = dlogf / (1.0 + jnp.exp(x))
            dz_ref[rows, :] = dz.astype(BF16)
            db = db + jnp.sum(dz, axis=0, keepdims=True)
        db_ref[...] = db

    return pl.pallas_call(
        body, out_shape=[jax.ShapeDtypeStruct((T, 128), BF16), jax.ShapeDtypeStruct((1, 128), F32)], name="gates_bwd",
    )(dc, z, b)


def _conv_pair(a_refs, cw_refs, cb_refs):
    row = lax.broadcasted_iota(jnp.int32, (T, CT), 0)
    outs = []
    for a_ref, cw_ref, cb_ref in zip(a_refs, cw_refs, cb_refs):
        z = a_ref[...]
        z1 = jnp.where(row >= 1, pltpu.roll(z, 1, 0), 0.0)
        z2 = jnp.where(row >= 2, pltpu.roll(z, 2, 0), 0.0)
        y = cw_ref[2:3, :] * z + cw_ref[1:2, :] * z1 + cw_ref[0:1, :] * z2 + cb_ref[...]
        outs.append((y, z, z1, z2))
    return outs


_GELU_K = math.sqrt(2.0 / math.pi)
N_CT = D_FF // CT


def _conv_specs():
    def at(rows, off):
        return pl.BlockSpec((rows, CT), lambda j: (0, j + off))
    return [at(T, 0), at(T, N_CT), at(3, 0), at(3, N_CT), at(1, 0), at(1, N_CT)]


def _convgate_fwd(a, cw, cb):
    def body(ag_ref, av_ref, wg_ref, wv_ref, bg_ref, bv_ref, u_ref):
        (g, _, _, _), (v, _, _, _) = _conv_pair((ag_ref, av_ref), (wg_ref, wv_ref), (bg_ref, bv_ref))
        th = jnp.tanh(_GELU_K * (g + 0.044715 * g * g * g))
        u_ref[...] = (0.5 * g * (1.0 + th) * v).astype(BF16)

    return pl.pallas_call(
        body, grid=(N_CT,), in_specs=_conv_specs(),
        out_specs=pl.BlockSpec((T, CT), lambda j: (0, j)), out_shape=jax.ShapeDtypeStruct((T, D_FF), BF16),
        compiler_params=_params(("parallel",), VMEM_BIG), name="convgate_fwd",
    )(a, a, cw, cw, cb, cb)


def _convgate_bwd(a, du, cw, cb):
    def body(ag_ref, av_ref, wg_ref, wv_ref, bg_ref, bv_ref, du_ref, da_ref, dcw_ref, dcb_ref):
        (g, gz, gz1, gz2), (v, vz, vz1, vz2) = _conv_pair((ag_ref, av_ref), (wg_ref, wv_ref), (bg_ref, bv_ref))
        du = du_ref[...].astype(F32)
        th = jnp.tanh(_GELU_K * (g + 0.044715 * g * g * g))
        gelu = 0.5 * g * (1.0 + th)
        dgelu = 0.5 * (1.0 + th) + 0.5 * g * (1.0 - th * th) * _GELU_K * (1.0 + 3 * 0.044715 * g * g)
        row = lax.broadcasted_iota(jnp.int32, (T, CT), 0)
        for h, (d, z, z1, z2, w_ref) in enumerate(((du * v * dgelu, gz, gz1, gz2, wg_ref), (du * gelu, vz, vz1, vz2, wv_ref))):
            d1 = jnp.where(row < T - 1, pltpu.roll(d, T - 1, 0), 0.0)
            d2 = jnp.where(row < T - 2, pltpu.roll(d, T - 2, 0), 0.0)
            da_ref[h] = (w_ref[2:3, :] * d + w_ref[1:2, :] * d1 + w_ref[0:1, :] * d2).astype(BF16)
            dcw_ref[h, 0:1, :] = jnp.sum(d * z2, axis=0, keepdims=True)
            dcw_ref[h, 1:2, :] = jnp.sum(d * z1, axis=0, keepdims=True)
            dcw_ref[h, 2:3, :] = jnp.sum(d * z, axis=0, keepdims=True)
            dcb_ref[h] = jnp.sum(d, axis=0, keepdims=True)

    def both(rows):
        return pl.BlockSpec((2, rows, CT), lambda j: (0, 0, j))

    return pl.pallas_call(
        body, grid=(N_CT,),
        in_specs=_conv_specs() + [pl.BlockSpec((T, CT), lambda j: (0, j))],
        out_specs=[both(T), both(3), both(1)],
        out_shape=[jax.ShapeDtypeStruct((2, T, D_FF), BF16), jax.ShapeDtypeStruct((2, 3, D_FF), F32),
                   jax.ShapeDtypeStruct((2, 1, D_FF), F32)],
        compiler_params=_params(("parallel",), VMEM_BIG), name="convgate_bwd",
    )(a, a, cw, cw, cb, cb, du)


def _halves_a(tm, tn, tk):
    per = D_FF // tk
    return lambda i, j, k: (lax.div(k, per), i, lax.rem(k, per))


def _halves_b(tm, tn, tk):
    per = D_FF // tn
    return lambda i, j, k: (lax.div(j, per), k, lax.rem(j, per))


def _adamw(w, m, v, g, *, name):
    r, c = w.shape
    tr = r
    if r * c > 256 * 1024:
        for cand in range(8, r, 8):
            if r % cand == 0 and cand * c <= 256 * 1024:
                tr = cand

    def body(w_ref, m_ref, v_ref, g_ref, d_ref, nm_ref, nv_ref):
        gv = g_ref[...]
        mn = ADAM_B1 * m_ref[...] + (1.0 - ADAM_B1) * gv
        vn = ADAM_B2 * v_ref[...] + (1.0 - ADAM_B2) * (gv * gv)
        m_hat = mn * (1.0 / (1.0 - ADAM_B1 ** ADAM_STEP))
        v_hat = vn * (1.0 / (1.0 - ADAM_B2 ** ADAM_STEP))
        d_ref[...] = -ADAM_LR * (m_hat / (jnp.sqrt(v_hat) + ADAM_EPS) + ADAM_WD * w_ref[...])
        nm_ref[...] = mn
        nv_ref[...] = vn

    blk = pl.BlockSpec((tr, c), lambda i: (i, 0))
    shp = jax.ShapeDtypeStruct((r, c), F32)
    return pl.pallas_call(
        body, grid=(r // tr,), in_specs=[blk] * 4, out_specs=[blk] * 3, out_shape=[shp] * 3,
        compiler_params=_params(("parallel",)), name=name,
    )(w, m, v, g)


def _place():
    x, y, c = lax.axis_index("x"), lax.axis_index("y"), lax.axis_index("c")
    chips = [(1 - x, y), (x, 1 - y), (1 - x, 1 - y)]
    return x, y, c, chips


def _window(ref, kind, s, half=None):
    lead = () if half is None else (half,)
    b, c = ref.shape[-2], ref.shape[-1]
    if kind == "col":
        return ref.at[lead + (slice(None), slice(None), pl.ds(s * (c // N_CHIPS), c // N_CHIPS))]
    if kind == "row":
        return ref.at[lead + (slice(None), pl.ds(s * (b // N_CHIPS), b // N_CHIPS), slice(None))]
    return ref.at[lead + (s,)]


def _allgather(tensors, kinds, *, name):
    n = len(tensors)

    def body(*refs):
        bufs = refs[n:2 * n]
        send, recv = refs[2 * n:]
        x, y, c, chips = _place()
        me = 2 * x + y
        sib = (x, y, 1 - c)

        def rcopy(i, k, win, to):
            return pltpu.make_async_remote_copy(src_ref=win, dst_ref=win, send_sem=send.at[i * 6 + k], recv_sem=recv.at[i * 6 + k],
                                                device_id=to, device_id_type=MESH)

        started = []
        for i in range(n):
            for k, (px, py) in enumerate(chips):
                cp = rcopy(i, k, _window(bufs[i], kinds[i], me, c), (px, py, c))
                cp.start()
                started.append(cp)
        for i in range(n):
            for k, (px, py) in enumerate(chips):
                landed = _window(bufs[i], kinds[i], 2 * px + py, c)
                rcopy(i, k, landed, (px, py, c)).wait_recv()
                fw = rcopy(i, 3 + k, landed, sib)
                fw.start()
                started.append(fw)
        for i in range(n):
            for k, (px, py) in enumerate(chips):
                rcopy(i, 3 + k, _window(bufs[i], kinds[i], 2 * px + py, 1 - c), sib).wait_recv()
        for cp in started:
            cp.wait_send()

    return pl.pallas_call(
        body, in_specs=[ANY] * n, out_specs=[ANY] * n,
        out_shape=[jax.ShapeDtypeStruct(t.shape, t.dtype) for t in tensors],
        scratch_shapes=[pltpu.SemaphoreType.DMA((6 * n,)), pltpu.SemaphoreType.DMA((6 * n,))],
        input_output_aliases={i: i for i in range(n)},
        name=name,
    )(*tensors)


def _rows_tile(rows, cols, sub):
    best = None
    for t in range(sub, rows + 1, sub):
        if rows % t == 0 and t * cols <= 512 * 1024:
            best = t
    return rows if best is None else best


def _sequencer(name, cid, n_sems, peers_of, body):
    @pl.kernel(mesh=plsc.ScalarSubcoreMesh(axis_name="seq", num_cores=1), name=name,
               scratch_types=(pltpu.SemaphoreType.DMA((n_sems,)), pltpu.SemaphoreType.DMA((n_sems,))),
               compiler_params=pltpu.CompilerParams(collective_id=cid))
    def launch(send, recv):
        x, y, c, chips = _place()
        peers = peers_of(x, y, c, chips)
        barrier = pltpu.get_barrier_semaphore()
        for peer in peers:
            pl.semaphore_signal(barrier, inc=1, device_id=peer, device_id_type=MESH)
        pl.semaphore_wait(barrier, len(peers))
        body(send, recv)

    launch()


def _half_of_full(ref, kind, h):
    if kind == "col":
        b = ref.shape[0]
        return ref.at[pl.ds(h * (b // 2), b // 2), :]
    if kind == "row":
        c = ref.shape[1]
        return ref.at[:, pl.ds(h * (c // 2), c // 2)]
    b = ref.shape[1]
    return ref.at[:, pl.ds(h * (b // 2), b // 2), :]


def _half_shape(full, kind):
    if kind == "col":
        return (full[0] // 2, full[1])
    if kind == "row":
        return (full[0], full[1] // 2)
    return (full[0], full[1] // 2, full[2])


def _win_of_half(ref, kind, s):
    if kind == "col":
        c = ref.shape[1]
        return ref.at[:, pl.ds(s * (c // N_CHIPS), c // N_CHIPS)]
    if kind == "row":
        b = ref.shape[0]
        return ref.at[pl.ds(s * (b // N_CHIPS), b // N_CHIPS), :]
    return ref.at[s]


def _win_shape(half, kind):
    if kind == "col":
        return (half[0], half[1] // N_CHIPS)
    if kind == "row":
        return (half[0] // N_CHIPS, half[1])
    return half[1:]


def _seq_swap(parts, kinds, *, name):
    n = len(parts)
    srcs = [jax.new_ref(p, memory_space=pltpu.MemorySpace.HBM) for p in parts]
    outs = [jax.empty_ref(jax.ShapeDtypeStruct(_half_shape(p.shape, k), p.dtype), memory_space=pltpu.MemorySpace.HBM)
            for p, k in zip(parts, kinds)]

    def body(send, recv):
        x, y, c, _ = _place()
        cps = []
        for i in range(n):
            cp = pltpu.make_async_remote_copy(src_ref=_half_of_full(srcs[i], kinds[i], 1 - c), dst_ref=outs[i], send_sem=send.at[i],
                                              recv_sem=recv.at[i], device_id=(x, y, 1 - c), device_id_type=MESH)
            cp.start()
            cps.append(cp)
        for cp in cps:
            cp.wait()

    _sequencer(name, 2, n, lambda x, y, c, chips: [(x, y, 1 - c)], body)
    return [o[...] for o in outs]


def _seq_scatter(halves, kinds, *, name):
    n = len(halves)
    srcs = [jax.new_ref(h, memory_space=pltpu.MemorySpace.HBM) for h in halves]
    outs = [jax.empty_ref(jax.ShapeDtypeStruct((3,) + _win_shape(h.shape, k), h.dtype), memory_space=pltpu.MemorySpace.HBM)
            for h, k in zip(halves, kinds)]

    def body(send, recv):
        x, y, c, chips = _place()
        cps = []
        for i in range(n):
            for k, (px, py) in enumerate(chips):
                cp = pltpu.make_async_remote_copy(src_ref=_win_of_half(srcs[i], kinds[i], 2 * px + py), dst_ref=outs[i].at[k],
                                                  send_sem=send.at[3 * i + k], recv_sem=recv.at[3 * i + k],
                                                  device_id=(px, py, c), device_id_type=MESH)
                cp.start()
                cps.append(cp)
        for cp in cps:
            cp.wait()

    _sequencer(name, 3, 3 * n, lambda x, y, c, chips: [(px, py, c) for px, py in chips], body)
    return [o[...] for o in outs]


def _add_half(g, p, kind, where, after, *, name):
    if kind == "slab":
        s, b2, c = p.shape
        tr = _rows_tile(b2, c, 16)
        nr = b2 // tr
        grid = (s, nr)
        g_spec = pl.BlockSpec((None, tr, c), lambda i, r, w: (i, w[1] * nr + r, 0))
        p_spec = pl.BlockSpec((None, tr, c), lambda i, r, w: (i, r, 0))
    elif kind == "col":
        b2, c = p.shape
        tr = _rows_tile(b2, c, 16)
        nr = b2 // tr
        grid = (1, nr)
        g_spec = pl.BlockSpec((tr, c), lambda i, r, w: (w[1] * nr + r, 0))
        p_spec = pl.BlockSpec((tr, c), lambda i, r, w: (r, 0))
    else:
        b, c2 = p.shape
        tr = _rows_tile(b, c2, 16)
        grid = (1, b // tr)
        g_spec = pl.BlockSpec((tr, c2), lambda i, r, w: (r, w[1]))
        p_spec = pl.BlockSpec((tr, c2), lambda i, r, w: (r, 0))

    def body(w_ref, g_ref, p_ref, *rest):
        o_ref = rest[-1]
        o_ref[...] = (g_ref[...].astype(F32) + p_ref[...].astype(F32)).astype(o_ref.dtype)

    extra = [] if after is None else [after]
    return pl.pallas_call(
        body,
        grid_spec=pltpu.PrefetchScalarGridSpec(num_scalar_prefetch=1, grid=grid, in_specs=[g_spec, p_spec] + [ANY] * len(extra),
                                               out_specs=p_spec),
        out_shape=jax.ShapeDtypeStruct(p.shape, g.dtype),
        compiler_params=_params(("parallel", "parallel")), name=name,
    )(where, g, p, *extra)


def _sum_chips(r, h, kind, where, layer, layers, out_buf, after, *, name):
    _, br, cr = r.shape
    tr = _rows_tile(br, cr, 16)
    nr = br // tr
    if kind == "col":
        h_spec = pl.BlockSpec((tr, cr), lambda j, w: (j, w[0]))
        o_shape, o_spec = (layers, 2 * br, cr), pl.BlockSpec((None, tr, cr), lambda j, w: (layer, w[1] * nr + j, 0))
    elif kind == "row":
        h_spec = pl.BlockSpec((tr, cr), lambda j, w: (w[0] * nr + j, 0))
        o_shape, o_spec = (layers, br, 2 * cr), pl.BlockSpec((None, tr, cr), lambda j, w: (layer, j, w[1]))
    else:
        h_spec = pl.BlockSpec((None, tr, cr), lambda j, w: (w[0], j, 0))
        o_shape, o_spec = (layers, 2 * br, cr), pl.BlockSpec((None, tr, cr), lambda j, w: (layer, w[1] * nr + j, 0))

    def body(w_ref, h_ref, r0_ref, r1_ref, r2_ref, *rest):
        o_ref, t_ref = rest[-2], rest[-1]
        o_ref[...] = ((h_ref[...].astype(F32) + r0_ref[...].astype(F32)) + r1_ref[...].astype(F32)) + r2_ref[...].astype(F32)
        t_ref[...] = jnp.zeros_like(t_ref)

    def slot(k):
        return pl.BlockSpec((None, tr, cr), lambda j, w: (k, j, 0))

    ins, specs, alias = [h, r, r, r], [h_spec, slot(0), slot(1), slot(2)], {}
    if after is not None:
        ins.append(after)
        specs.append(ANY)
    if out_buf is not None:
        alias = {1 + len(ins): 0}
        ins.append(out_buf)
        specs.append(ANY)
    return pl.pallas_call(
        body,
        grid_spec=pltpu.PrefetchScalarGridSpec(num_scalar_prefetch=1, grid=(nr,), in_specs=specs,
                                               out_specs=[o_spec, pl.BlockSpec((8, 128), lambda j, w: (0, 0))]),
        out_shape=[jax.ShapeDtypeStruct(o_shape, F32), jax.ShapeDtypeStruct((8, 128), F32)], input_output_aliases=alias,
        compiler_params=_params(("arbitrary",)), name=name,
    )(where, *ins)


def _join_halves(tensors, kinds, *, name):
    n = len(tensors)

    def mine(ref, kind, h):
        if kind == "row":
            c = ref.shape[2]
            return ref.at[:, :, pl.ds(h * (c // 2), c // 2)]
        b = ref.shape[1]
        return ref.at[:, pl.ds(h * (b // 2), b // 2), :]

    def body(*refs):
        bufs = refs[n:2 * n]
        send, recv = refs[2 * n:]
        x, y, c, _ = _place()
        cps = []
        for i in range(n):
            part = mine(bufs[i], kinds[i], c)
            cp = pltpu.make_async_remote_copy(src_ref=part, dst_ref=part, send_sem=send.at[i],
                                              recv_sem=recv.at[i], device_id=(x, y, 1 - c), device_id_type=MESH)
            cp.start()
            cps.append(cp)
        for i in range(n):
            other = mine(bufs[i], kinds[i], 1 - c)
            pltpu.make_async_remote_copy(src_ref=other, dst_ref=other, send_sem=send.at[i],
                                         recv_sem=recv.at[i], device_id=(x, y, 1 - c), device_id_type=MESH).wait_recv()
        for cp in cps:
            cp.wait_send()

    return pl.pallas_call(
        body, in_specs=[ANY] * n, out_specs=[ANY] * n,
        out_shape=[jax.ShapeDtypeStruct(t.shape, t.dtype) for t in tensors],
        scratch_shapes=[pltpu.SemaphoreType.DMA((n,)), pltpu.SemaphoreType.DMA((n,))],
        input_output_aliases={i: i for i in range(n)},
        name=name,
    )(*tensors)


def _win(ref, kind, s, h=None):
    if kind == "col":
        b, c = ref.shape
        cols = pl.ds(s * (c // N_CHIPS), c // N_CHIPS)
        return ref.at[:, cols] if h is None else ref.at[pl.ds(h * (b // 2), b // 2), cols]
    if kind == "row":
        b, c = ref.shape
        rows = pl.ds(s * (b // N_CHIPS), b // N_CHIPS)
        return ref.at[rows, :] if h is None else ref.at[rows, pl.ds(h * (c // 2), c // 2)]
    b = ref.shape[1]
    return ref.at[s] if h is None else ref.at[s, pl.ds(h * (b // 2), b // 2)]


def _half(ref, kind, h):
    b, c = ref.shape
    if kind == "row":
        return ref.at[:, pl.ds(h * (c // 2), c // 2)]
    return ref.at[pl.ds(h * (b // 2), b // 2), :]


def _full_shape(shard_shape, kind):
    b, c = shard_shape
    return {"col": (b, N_CHIPS * c), "row": (N_CHIPS * b, c), "slab": (N_CHIPS, b, c)}[kind]


def _gather_body(srcs, outs, kinds, send, recv):
    x, y, c, chips = _place()
    me = 2 * x + y
    sib = (x, y, 1 - c)

    def rcopy(i, k, src, dst, to):
        return pltpu.make_async_remote_copy(src_ref=src, dst_ref=dst, send_sem=send.at[7 * i + k], recv_sem=recv.at[7 * i + k],
                                            device_id=to, device_id_type=MESH)

    started = []
    for i, (src, out, kind) in enumerate(zip(srcs, outs, kinds)):
        own = rcopy(i, 6, src, _win(out, kind, me), sib)
        own.start()
        started.append(own)
        for k, (px, py) in enumerate(chips):
            cp = rcopy(i, k, _half(src, kind, c), _win(out, kind, me, c), (px, py, c))
            cp.start()
            started.append(cp)
    for i, (out, kind) in enumerate(zip(outs, kinds)):
        for k, (px, py) in enumerate(chips):
            landed = _win(out, kind, 2 * px + py, c)
            rcopy(i, k, landed, landed, (px, py, c)).wait_recv()
            fw = rcopy(i, 3 + k, landed, landed, sib)
            fw.start()
            started.append(fw)
    for i, (src, out, kind) in enumerate(zip(srcs, outs, kinds)):
        for k, (px, py) in enumerate(chips):
            other = _win(out, kind, 2 * px + py, 1 - c)
            rcopy(i, 3 + k, other, other, sib).wait_recv()
        rcopy(i, 6, src, _win(out, kind, me), sib).wait_recv()
    for cp in started:
        cp.wait_send()


def _seq_gather(shards, kinds, *, name, cid):
    n = len(shards)
    srcs = [jax.new_ref(s, memory_space=pltpu.MemorySpace.HBM) for s in shards]
    outs = [jax.empty_ref(jax.ShapeDtypeStruct(_full_shape(s.shape, k), s.dtype), memory_space=pltpu.MemorySpace.HBM)
            for s, k in zip(shards, kinds)]

    @pl.kernel(mesh=plsc.ScalarSubcoreMesh(axis_name="seq", num_cores=1), name=name,
               scratch_types=(pltpu.SemaphoreType.DMA((7 * n,)), pltpu.SemaphoreType.DMA((7 * n,))),
               compiler_params=pltpu.CompilerParams(collective_id=cid))
    def launch(send, recv):
        x, y, c, chips = _place()
        barrier = pltpu.get_barrier_semaphore()
        for px, py in chips:
            pl.semaphore_signal(barrier, inc=1, device_id=(px, py, c), device_id_type=MESH)
        pl.semaphore_signal(barrier, inc=1, device_id=(x, y, 1 - c), device_id_type=MESH)
        pl.semaphore_wait(barrier, 4)
        _gather_body(srcs, outs, kinds, send, recv)

    launch()
    return [o[...] for o in outs]


KIND = dict(w_qkv_a="slab", w_o_a="col", w_q_b="row", w_o_b="row", w_kvf="slab", w_up="col", w_down="row", small="slab")
LAYERS = dict(w_qkv_a=N_A, w_o_a=N_A, w_q_b=DEPTH - N_A, w_o_b=DEPTH - N_A, w_kvf=1, w_up=DEPTH, w_down=DEPTH, small=1)
SMALL_W = 1792
SMALL_ROWS = 8


class _Reducer:
    def __init__(self, where):
        self.where = where
        self.acc = {nm: None for nm in KIND}
        self.pending = None

    def __call__(self, group, tag):
        names, layers, parts = zip(*group)
        kinds = [KIND[nm] for nm in names]
        summed = self._sum_pending(after=parts[-1])
        sib = _seq_swap(list(parts), kinds, name="reduce_swap_" + tag)
        halves = []
        for g, p, k, nm in zip(parts, sib, kinds, names):
            halves.append(_add_half(g, p, k, self.where, halves[-1] if halves else None, name="reduce_add_" + nm))
        landed = _seq_scatter(halves, kinds, name="reduce_scatter_" + tag)
        self.pending = (names, layers, landed, halves, kinds)
        return [halves[-1], summed]

    def flush(self, after):
        return self._sum_pending(after)

    def _sum_pending(self, after):
        if self.pending is None:
            return None
        for nm, l, r, h, k in zip(*self.pending):
            self.acc[nm], after = _sum_chips(r, h, k, self.where, l, LAYERS[nm], self.acc[nm], after, name="reduce_sum_" + nm)
        self.pending = None
        return after

    def finish(self):
        self._sum_pending(after=None)
        names = list(KIND)
        joined = _join_halves([self.acc[nm] for nm in names], [KIND[nm] for nm in names], name="reduce_pair_join")
        return dict(zip(names, joined))


def _headsum_matrix():
    r = lax.broadcasted_iota(jnp.int32, (128, 128), 0) // HD
    c = lax.broadcasted_iota(jnp.int32, (128, 128), 1) // HD
    return jnp.where(r == c, 1.0, 0.0).astype(BF16)


def kernel(x, norm_gains, w_qkv_a, w_o_a, w_q_b, w_o_b, kv_norm, w_kvf, b_f, w_up, conv_w, conv_b, w_down, loss_target, m_norm_gains, m_w_qkv_a, m_w_o_a, m_w_q_b, m_w_o_b, m_kv_norm, m_w_kvf, m_b_f, m_w_up, m_conv_w, m_conv_b, m_w_down, v_norm_gains, v_w_qkv_a, v_w_o_a, v_w_q_b, v_w_o_b, v_kv_norm, v_w_kvf, v_b_f, v_w_up, v_conv_w, v_conv_b, v_w_down):
    xi, yi, ci = lax.axis_index("x"), lax.axis_index("y"), lax.axis_index("c")
    chip = 2 * xi + yi
    where = jnp.stack([chip, ci]).astype(jnp.int32)
    ws = dict(norm_gains=norm_gains, w_qkv_a=w_qkv_a, w_o_a=w_o_a, w_q_b=w_q_b, w_o_b=w_o_b, kv_norm=kv_norm, w_kvf=w_kvf,
              b_f=b_f, w_up=w_up, conv_w=conv_w, conv_b=conv_b, w_down=w_down)
    ms = dict(norm_gains=m_norm_gains, w_qkv_a=m_w_qkv_a, w_o_a=m_w_o_a, w_q_b=m_w_q_b, w_o_b=m_w_o_b, kv_norm=m_kv_norm,
              w_kvf=m_w_kvf, b_f=m_b_f, w_up=m_w_up, conv_w=m_conv_w, conv_b=m_conv_b, w_down=m_w_down)
    vs = dict(norm_gains=v_norm_gains, w_qkv_a=v_w_qkv_a, w_o_a=v_w_o_a, w_q_b=v_w_q_b, w_o_b=v_w_o_b, kv_norm=v_kv_norm,
              w_kvf=v_w_kvf, b_f=v_b_f, w_up=v_w_up, conv_w=v_conv_w, conv_b=v_conv_b, w_down=v_w_down)

    small = jnp.concatenate([
        jnp.pad(norm_gains.reshape(16, 256), ((0, 0), (0, 1408 - 256))),
        jnp.pad(conv_w.reshape(12, 1408), ((0, 4), (0, 0)))], axis=0)
    big = [nm for nm in KIND if nm != "small"]
    half = {nm: ws[nm].astype(BF16) for nm in big}
    W = {nm: [None] * LAYERS[nm] for nm in big if nm != "w_kvf"}
    g_small = None
    groups = [("0a", [("w_qkv_a", 0), ("w_o_a", 0), ("small", 0)]), ("0b", [("w_up", 0)]), ("0c", [("w_down", 0)]),
              ("1a", [("w_qkv_a", 1), ("w_o_a", 1)]), ("1b", [("w_up", 1)]), ("1c", [("w_down", 1)]),
              ("2", [("w_kvf", 0), ("w_q_b", 0), ("w_o_b", 0), ("w_up", 2), ("w_down", 2)]),
              ("3", [("w_q_b", 1), ("w_o_b", 1), ("w_up", 3), ("w_down", 3)])]
    for tag, group in groups:
        shards = [small if nm == "small" else half[nm] if nm == "w_kvf" else half[nm][i] for nm, i in group]
        got = _seq_gather(shards, [KIND[nm] for nm, _ in group], name="gather_layer" + tag, cid=1)
        for (nm, i), g in zip(group, got):
            if nm == "small":
                g_small = g
            elif nm == "w_kvf":
                W[nm] = g.transpose(1, 0, 2).reshape(D, 2 * D + 16)
            else:
                W[nm][i] = g.transpose(1, 0, 2).reshape(D, 3 * A_W) if nm == "w_qkv_a" else g
    gains = g_small[:, :16, :256].transpose(1, 0, 2).reshape(DEPTH, 4, 1, D)
    cw_full = g_small[:, 16:28, :].transpose(1, 0, 2).reshape(DEPTH, 3, 2 * D_FF)
    cb_full = conv_b.reshape(DEPTH, 1, 2 * D_FF)

    reducer = _Reducer(where)
    sq, dh = _fwd_bwd(x[0], loss_target[0], W, gains, cw_full, cb_full, kv_norm, b_f, reducer)
    loss = lax.psum(sq[0, 0] * (0.5 / D), ("x", "y", "c"))
    return _update(loss, dh[None], reducer.finish(), chip, ws, ms, vs)


def _fwd_bwd(h, target, W, gains, cw_full, cb_full, kv_norm, b_f, reduce):
    w_kv = W["w_kvf"][:, :2 * D]
    w_kvf_pad = jnp.pad(W["w_kvf"], ((0, 0), (0, 128 - 16)))
    w_f = w_kvf_pad[:, 2 * D:]
    kvn_g = kv_norm.reshape(1, D)
    bf_pad = jnp.pad(b_f, (0, 128 - 16)).reshape(1, 128)
    tabs = _rope_tables()
    headsum = _headsum_matrix()

    saved = []
    kv = zf = c_row = kvn = h_kv = None
    xn = _rms_fwd(h, gains[0][0], out_dtype=BF16, name="rms_in")
    for l in range(DEPTH):
        s = {"h": h}
        g = gains[l]
        s["xn"] = xn
        if l < N_A:
            qkv = _matmul(xn, W["w_qkv_a"][l], mode="nn", out_dtype=F32, name="mm_qkv", mnk=(T, 3 * A_W, D), tn=768)
            qkvp = _rope_fwd(qkv, tabs).reshape(3, 3, 2, T, 128)
            o_p, lse_p = _band_fwd(qkvp)
            att, o3, lse3 = _combine_fwd(o_p, lse_p)
            s.update(qkvp=qkvp, o3=o3, lse3=lse3, lse_p=lse_p, att=att)
            mix = _matmul(att, W["w_o_a"][l], mode="nn", out_dtype=F32, name="mm_oa", mnk=(T, D, A_W))
        else:
            j = l - N_A
            if l == N_A:
                h_kv = h
                kvn = _rms_fwd(h, kvn_g, out_dtype=BF16, name="rms_in")
                kv = _matmul(kvn, w_kv, mode="nn", out_dtype=BF16, name="mm_kv")
                zf = _matmul(kvn, w_f, mode="nn", out_dtype=F32, name="mm_f")
                cum = _gates_fwd(zf, bf_pad)[:, :16]
                c_row = cum.T.reshape(8, 2, T)
            q = _matmul(xn, W["w_q_b"][j], mode="nn", out_dtype=BF16, name="mm_qb", mnk=(T, D, D), alpha=HD ** -0.5)
            o = _fox_fwd(q, kv, c_row)
            s.update(q=q, o=o)
            mix = _matmul(o, W["w_o_b"][j], mode="nn", out_dtype=F32, name="mm_ob", mnk=(T, D, D))
        s["mix"] = mix
        h1, xn2 = _rms_res_in(mix, g[1], h, g[2], name="rms_res_in")
        a = _matmul(xn2, W["w_up"][l], mode="nn", out_dtype=F32, name="mm_up", mnk=(T, 2 * D_FF, D))
        u = _convgate_fwd(a, cw_full[l], cb_full[l])
        f = _matmul(u, W["w_down"][l], mode="nn", out_dtype=F32, name="mm_down", mnk=(T, D, D_FF), tm=1024, tk=D_FF)
        if l + 1 < DEPTH:
            h, xn = _rms_res_in(f, g[3], h1, gains[l + 1][0], name="rms_res_in")
        else:
            h = _rms_fwd(f, g[3], res=h1, out_dtype=F32, name="rms_res")
        s.update(h1=h1, xn2=xn2, a=a, u=u, f=f)
        saved.append(s)

    dh, sq = _loss_head(h, target)

    d_gains = [[None] * 4 for _ in range(DEPTH)]
    d_cw, d_cb = [None] * DEPTH, [None] * DEPTH
    fox_acc = None
    d_kvnorm = d_bf = token = df = None

    def dw(nm, a, b, **kw):
        return _matmul(a, b, mode="tn", out_dtype=BF16, name="mm_dw_" + nm, **kw)

    flush = getattr(reduce, "flush", lambda after: None)

    def slabs(full, width):
        return full.reshape(full.shape[0], N_CHIPS, width).transpose(1, 0, 2)

    for l in reversed(range(DEPTH)):
        s = saved[l]
        g = gains[l]
        if df is None:
            df, d_gains[l][3] = _rms_bwd(dh, s["f"], g[3], out_dtype=BF16, name="rms_bwd")
        du = _matmul(df, W["w_down"][l], mode="nt", out_dtype=F32, name="mm_down_dx", mnk=(T, D_FF, D), tn=256, after=token)
        g_down = dw("w_down", s["u"], df, tm=1408, tn=1024)
        da, d_cw[l], d_cb[l] = _convgate_bwd(s["a"], du, cw_full[l], cb_full[l])
        dxn2 = _matmul(da, W["w_up"][l], mode="nt", out_dtype=F32, name="mm_up_dx", mnk=(T, D, 2 * D_FF), tm=1024, tn=1024, tk=1408,
                       a_map=_halves_a)
        g_up = dw("w_up", s["xn2"], da, mnk=(D, 2 * D_FF, T), tn=1408, b_map=_halves_b)
        token = reduce([("w_down", l, g_down), ("w_up", l, g_up)], "ffn%d" % l)
        dh1, dmix, d_gains[l][2], d_gains[l][1] = _rms_bwd2(dxn2, s["h1"], g[2], dh, s["mix"], g[1], name="rms_bwd2")
        if l < N_A:
            datt = _matmul(dmix, W["w_o_a"][l], mode="nt", out_dtype=F32, name="mm_oa_dx", mnk=(T, A_W, D), tn=768, after=token)
            g_o = dw("w_o_a", s["att"], dmix, tm=768, tn=1024)
            do_p, dlt_p = _combine_bwd(datt, s["o3"], s["lse3"], headsum)
            dqkv = None
            for which, d in enumerate(_band_bwd(s["qkvp"], do_p, s["lse_p"], dlt_p)):
                dqkv = _rope_bwd(d, which, tabs, dqkv)
            dxn = _matmul(dqkv, W["w_qkv_a"][l], mode="nt", out_dtype=F32, name="mm_qkv_dx", mnk=(T, D, 3 * A_W), tm=1024, tn=1024, tk=3 * A_W,
                          after=[flush(dqkv)])
            g_qkv = dw("w_qkv_a", s["xn"], dqkv, tn=768)
            group = [("w_o_a", l, g_o), ("w_qkv_a", l, slabs(g_qkv, 576))]
        else:
            j = l - N_A
            do = _matmul(dmix, W["w_o_b"][j], mode="nt", out_dtype=BF16, name="mm_ob_dx", mnk=(T, D, D), after=token)
            g_o = dw("w_o_b", s["o"], dmix, tn=1024)
            dq, *fox_acc = _fox_bwd(s["q"], kv, do, c_row, fox_acc)
            dxn = _matmul(dq, W["w_q_b"][j], mode="nt", out_dtype=F32, name="mm_qb_dx", mnk=(T, D, D), after=[flush(dq)])
            g_q = dw("w_q_b", s["xn"], dq, tn=1024)
            group = [("w_o_b", j, g_o), ("w_q_b", j, g_q)]
        if l > 0 and l != N_A:
            dh, df, d_gains[l][0], d_gains[l - 1][3] = _rms_bwd2(dxn, s["h"], g[0], dh1, saved[l - 1]["f"], gains[l - 1][3],
                                                                 name="rms_bwd2")
        else:
            dh, d_gains[l][0] = _rms_bwd(dxn, s["h"], g[0], dres=dh1, out_dtype=F32, name="rms_bwd_res")
            df = None
        if l == N_A:
            dk, dv, dck = fox_acc
            dc16 = -dck[:, :2, :].reshape(16, T).T
            dzf, d_bf = _gates_bwd(jnp.pad(dc16, ((0, 0), (0, 128 - 16))), zf, bf_pad)
            dkvf = jnp.concatenate([dk.astype(BF16), dv.astype(BF16), dzf], axis=1)
            g_kvf = _matmul(kvn, dkvf, mode="tn", out_dtype=BF16, name="mm_kvf_dw", tm=512, tn=2 * D + 128)[:, :2 * D + 16]
            dkvn = _matmul(dkvf, w_kvf_pad, mode="nt", out_dtype=F32, name="mm_kvf_dx", tm=1024, tn=1024, tk=2 * D + 128)
            dh, d_kvnorm = _rms_bwd(dkvn, h_kv, kvn_g, dres=dh, out_dtype=F32, name="rms_bwd_res")
            group.append(("w_kvf", 0, slabs(g_kvf, 516)))
        token = reduce(group, "mix%d" % l)
    small_flat = jnp.concatenate([
        jnp.stack([jnp.stack(r) for r in d_gains]).reshape(-1),
        jnp.stack(d_cw).transpose(0, 2, 1, 3).reshape(-1),
        jnp.stack(d_cb).reshape(-1),
        d_kvnorm.reshape(-1), d_bf[0, :16]])
    small = jnp.pad(small_flat, (0, 2 * N_CHIPS * SMALL_ROWS * SMALL_W - small_flat.shape[0]))
    reduce([("small", 0, small.reshape(N_CHIPS, 2 * SMALL_ROWS, SMALL_W))], "small")
    return sq, dh


def _update(loss, grad_x, reduced, chip, ws, ms, vs):
    red_s = reduced.pop("small")
    buf_s = lax.dynamic_update_slice(jnp.zeros((2, N_CHIPS, SMALL_ROWS, SMALL_W), F32), red_s.reshape(2, 1, SMALL_ROWS, SMALL_W),
                                     (0, chip, 0, 0))
    (all_s,) = _allgather([buf_s], ["slab"], name="gather_small_grads")
    sflat = all_s.transpose(1, 0, 2, 3).reshape(-1)

    grads = {nm: r.reshape(ws[nm].shape) for nm, r in reduced.items()}
    o = 0
    g_gains_full = sflat[o:o + 16 * D].reshape(DEPTH, 4, D); o += 16 * D
    g_cw_full = sflat[o:o + 12 * 2 * D_FF].reshape(DEPTH, 3, 2 * D_FF); o += 12 * 2 * D_FF
    grads["conv_b"] = sflat[o:o + 4 * 2 * D_FF].reshape(DEPTH, 2 * D_FF); o += 4 * 2 * D_FF
    grads["kv_norm"] = sflat[o:o + D]; o += D
    grads["b_f"] = sflat[o:o + 16]
    grads["norm_gains"] = lax.dynamic_slice_in_dim(g_gains_full, chip * 256, 256, axis=2)
    grads["conv_w"] = lax.dynamic_slice_in_dim(g_cw_full, chip * 1408, 1408, axis=2)

    names = ["norm_gains", "w_qkv_a", "w_o_a", "w_q_b", "w_o_b", "kv_norm", "w_kvf", "b_f", "w_up", "conv_w", "conv_b", "w_down"]
    deltas, new_m, new_v = {}, {}, {}
    for nm in names:
        shp = ws[nm].shape
        two = (math.prod(shp[:-1]), shp[-1]) if len(shp) > 1 else (1, shp[0])
        d, m2, v2 = _adamw(ws[nm].reshape(two), ms[nm].reshape(two), vs[nm].reshape(two), grads[nm].reshape(two),
                           name="adamw_" + nm)
        deltas[nm], new_m[nm], new_v[nm] = d.reshape(shp), m2.reshape(shp), v2.reshape(shp)

    return (loss, grad_x, *[grads[nm] for nm in names], *[deltas[nm] for nm in names],
            *[new_m[nm] for nm in names], *[new_v[nm] for nm in names])
```

```python
import math

import jax
import jax.numpy as jnp
from jax import lax
from jax.experimental import pallas as pl
from jax.experimental.pallas import tpu as pltpu
from jax.experimental.pallas import tpu_sc as plsc

F32 = jnp.float32
BF16 = jnp.bfloat16
MESH = pl.DeviceIdType.MESH
ANY = pl.BlockSpec(memory_space=pl.ANY)

T = 2048
D = 1024
HD = 64
DEPTH = 4
N_A = 2
A_W = 768
GW = 256
DIL = (1, 4, 16)
BLK = 128
D_FF = 2816
ROPE_THETA = 500000.0
EPS = 1e-6
NEG = -1e30
N_CHIPS = 4
FQ = 256
CT = 128
VMEM_BIG = 48 * 1024 * 1024

ADAM_LR, ADAM_B1, ADAM_B2, ADAM_EPS, ADAM_WD, ADAM_STEP = 0.001, 0.9, 0.999, 1e-08, 0.01, 10

NN = (((1,), (0,)), ((), ()))
NT = (((1,), (1,)), ((), ()))
TN = (((0,), (0,)), ((), ()))


def _dot(a, b, dims):
    return lax.dot_general(a, b, dims, preferred_element_type=F32)


def _pick(dim, pref):
    if dim <= pref:
        return dim
    best = None
    for t in range(128, pref + 1, 128):
        if dim % t == 0:
            best = t
    assert best is not None, (dim, pref)
    return best


def _params(sem=None, vmem=None):
    kw = {}
    if sem is not None:
        kw["dimension_semantics"] = sem
    if vmem is not None:
        kw["vmem_limit_bytes"] = vmem
    return pltpu.CompilerParams(**kw)


def _matmul(a, b, *, mode, out_dtype, name, mnk=None, alpha=None, tm=2048, tn=512, tk=2048, a_map=None, b_map=None, after=None):
    if mnk is not None:
        M, N, K = mnk
    elif mode == "nn":
        (M, K), (_, N) = a.shape, b.shape
    elif mode == "nt":
        (M, K), (N, _) = a.shape, b.shape
    else:
        (K, M), (_, N) = a.shape, b.shape
    tm, tn, tk = _pick(M, tm), _pick(N, tn), _pick(K, tk)
    nk = K // tk
    dims = {"nn": NN, "nt": NT, "tn": TN}[mode]
    after = [t for t in (after or ()) if t is not None]
    n_in = 2 + len(after)

    def body(*refs):
        a_ref, b_ref = refs[0], refs[1]
        o_ref = refs[n_in]
        k = pl.program_id(2)

        def finish(r):
            if alpha is not None:
                r = r * alpha
            o_ref[...] = r.astype(out_dtype)

        def product():
            return _dot(a_ref[...], b_ref[...], dims)

        if nk == 1:
            finish(product())
            return
        acc_ref = refs[n_in + 1]

        @pl.when(k == 0)
        def _():
            acc_ref[...] = product()

        @pl.when((k > 0) & (k < nk - 1))
        def _():
            acc_ref[...] += product()

        @pl.when(k == nk - 1)
        def _():
            finish(acc_ref[...] + product())

    a_blk = (tk, tm) if mode == "tn" else (tm, tk)
    b_blk = (tn, tk) if mode == "nt" else (tk, tn)
    if a_map is not None:
        a_spec = pl.BlockSpec((None,) + a_blk, a_map(tm, tn, tk))
    elif mode == "tn":
        a_spec = pl.BlockSpec(a_blk, lambda i, j, k: (k, i))
    else:
        a_spec = pl.BlockSpec(a_blk, lambda i, j, k: (i, k))
    if b_map is not None:
        b_spec = pl.BlockSpec((None,) + b_blk, b_map(tm, tn, tk))
    elif mode == "nt":
        b_spec = pl.BlockSpec(b_blk, lambda i, j, k: (j, k))
    else:
        b_spec = pl.BlockSpec(b_blk, lambda i, j, k: (k, j))
    return pl.pallas_call(
        body,
        grid=(M // tm, N // tn, nk),
        in_specs=[a_spec, b_spec] + [ANY] * len(after),
        out_specs=pl.BlockSpec((tm, tn), lambda i, j, k: (i, j)),
        out_shape=jax.ShapeDtypeStruct((M, N), out_dtype),
        scratch_shapes=[pltpu.VMEM((tm, tn), F32)] if nk > 1 else [],
        compiler_params=_params(("parallel", "parallel", "arbitrary"), VMEM_BIG),
        name=name,
    )(a, b, *after)


def _rms_fwd(x, g, *, out_dtype, name, res=None, tr=256):
    n, d = x.shape

    def body(*refs):
        x_ref, g_ref = refs[0], refs[1]
        o_ref = refs[-1]
        xv = x_ref[...].astype(F32)
        y = xv * lax.rsqrt(jnp.mean(xv * xv, axis=-1, keepdims=True) + EPS) * g_ref[...]
        if res is not None:
            y = y + refs[2][...]
        o_ref[...] = y.astype(out_dtype)

    row = pl.BlockSpec((tr, d), lambda i: (i, 0))
    vec = pl.BlockSpec((1, d), lambda i: (0, 0))
    ins = [x, g] + ([] if res is None else [res])
    specs = [row, vec] + ([] if res is None else [row])
    return pl.pallas_call(
        body, grid=(n // tr,), in_specs=specs, out_specs=row,
        out_shape=jax.ShapeDtypeStruct((n, d), out_dtype),
        compiler_params=_params(("parallel",)), name=name,
    )(*ins)


def _rms_bwd(dy, x, g, *, out_dtype, name, dres=None, tr=512):
    n, d = x.shape

    def body(*refs):
        dy_ref, x_ref, g_ref = refs[0], refs[1], refs[2]
        dx_ref, dg_ref = refs[-2], refs[-1]
        xv = x_ref[...].astype(F32)
        dyv = dy_ref[...].astype(F32)
        rstd = lax.rsqrt(jnp.mean(xv * xv, axis=-1, keepdims=True) + EPS)
        xhat = xv * rstd
        dxh = dyv * g_ref[...]
        dx = rstd * (dxh - xhat * jnp.mean(dxh * xhat, axis=-1, keepdims=True))
        if dres is not None:
            dx = dx + refs[3][...]
        dx_ref[...] = dx.astype(out_dtype)

        @pl.when(pl.program_id(0) == 0)
        def _():
            dg_ref[...] = jnp.zeros_like(dg_ref)

        dg_ref[...] += jnp.sum(dyv * xhat, axis=0, keepdims=True)

    row = pl.BlockSpec((tr, d), lambda i: (i, 0))
    vec = pl.BlockSpec((1, d), lambda i: (0, 0))
    ins = [dy, x, g] + ([] if dres is None else [dres])
    specs = [row, row, vec] + ([] if dres is None else [row])
    return pl.pallas_call(
        body, grid=(n // tr,), in_specs=specs, out_specs=[row, vec],
        out_shape=[jax.ShapeDtypeStruct((n, d), out_dtype), jax.ShapeDtypeStruct((1, d), F32)],
        compiler_params=_params(("arbitrary",), VMEM_BIG), name=name,
    )(*ins)


def _rms_res_in(x, g_res, res, g_in, *, name, tr=512):
    n, d = x.shape

    def body(x_ref, gr_ref, r_ref, gi_ref, h_ref, n_ref):
        xv = x_ref[...].astype(F32)
        h = r_ref[...] + xv * lax.rsqrt(jnp.mean(xv * xv, axis=-1, keepdims=True) + EPS) * gr_ref[...]
        h_ref[...] = h
        n_ref[...] = (h * lax.rsqrt(jnp.mean(h * h, axis=-1, keepdims=True) + EPS) * gi_ref[...]).astype(BF16)

    row = pl.BlockSpec((tr, d), lambda i: (i, 0))
    vec = pl.BlockSpec((1, d), lambda i: (0, 0))
    return pl.pallas_call(
        body, grid=(n // tr,), in_specs=[row, vec, row, vec], out_specs=[row, row],
        out_shape=[jax.ShapeDtypeStruct((n, d), F32), jax.ShapeDtypeStruct((n, d), BF16)],
        compiler_params=_params(("parallel",), VMEM_BIG), name=name,
    )(x, g_res, res, g_in)


def _rms_bwd2(dy, x, g, dres, x2, g2, *, name, tr=512):
    n, d = x.shape

    def one(dyv, xv, gv):
        rstd = lax.rsqrt(jnp.mean(xv * xv, axis=-1, keepdims=True) + EPS)
        xhat = xv * rstd
        dxh = dyv * gv
        return rstd * (dxh - xhat * jnp.mean(dxh * xhat, axis=-1, keepdims=True)), jnp.sum(dyv * xhat, axis=0, keepdims=True)

    def body(dy_ref, x_ref, g_ref, r_ref, x2_ref, g2_ref, dx_ref, d2_ref, dg_ref, dg2_ref):
        dx, dg = one(dy_ref[...].astype(F32), x_ref[...].astype(F32), g_ref[...])
        dx = dx + r_ref[...]
        dx_ref[...] = dx
        d2, dg2 = one(dx, x2_ref[...].astype(F32), g2_ref[...])
        d2_ref[...] = d2.astype(BF16)

        @pl.when(pl.program_id(0) == 0)
        def _():
            dg_ref[...] = jnp.zeros_like(dg_ref)
            dg2_ref[...] = jnp.zeros_like(dg2_ref)

        dg_ref[...] += dg
        dg2_ref[...] += dg2

    row = pl.BlockSpec((tr, d), lambda i: (i, 0))
    vec = pl.BlockSpec((1, d), lambda i: (0, 0))
    return pl.pallas_call(
        body, grid=(n // tr,), in_specs=[row, row, vec, row, row, vec], out_specs=[row, row, vec, vec],
        out_shape=[jax.ShapeDtypeStruct((n, d), F32), jax.ShapeDtypeStruct((n, d), BF16),
                   jax.ShapeDtypeStruct((1, d), F32), jax.ShapeDtypeStruct((1, d), F32)],
        compiler_params=_params(("arbitrary",), VMEM_BIG), name=name,
    )(dy, x, g, dres, x2, g2)


def _loss_head(h, target, *, tr=256):
    n, d = h.shape

    def body(h_ref, t_ref, dh_ref, s_ref):
        err = h_ref[...] - t_ref[...]
        dh_ref[...] = err * (1.0 / d)

        @pl.when(pl.program_id(0) == 0)
        def _():
            s_ref[...] = jnp.zeros_like(s_ref)

        s_ref[...] += jnp.sum(err * err)

    row = pl.BlockSpec((tr, d), lambda i: (i, 0))
    acc = pl.BlockSpec((8, 128), lambda i: (0, 0))
    return pl.pallas_call(
        body, grid=(n // tr,), in_specs=[row, row], out_specs=[row, acc],
        out_shape=[jax.ShapeDtypeStruct((n, d), F32), jax.ShapeDtypeStruct((8, 128), F32)],
        compiler_params=_params(("arbitrary",)), name="loss_head",
    )(h, target)


def _rope_tables():
    pos = jnp.arange(T, dtype=F32)
    inv = ROPE_THETA ** (-jnp.arange(0, 16, 2, dtype=F32) / 16)
    ang = pos[:, None] * inv[None, :]
    cos, sin = jnp.cos(ang), jnp.sin(ang)
    one = jnp.ones((T, HD - 16), F32)
    zero8 = jnp.zeros((T, 8), F32)
    zero = jnp.zeros((T, HD - 16), F32)
    c = jnp.concatenate([cos, cos, one], axis=1)
    s1 = jnp.concatenate([zero8, sin, zero], axis=1)
    s2 = jnp.concatenate([-sin, zero8, zero], axis=1)
    c, s1, s2 = (jnp.concatenate([t, t], axis=1) for t in (c, s1, s2))
    scale = HD ** -0.5
    return (jnp.stack([c * scale, c, jnp.ones_like(c)]), jnp.stack([s1 * scale, s1, jnp.zeros_like(c)]),
            jnp.stack([s2 * scale, s2, jnp.zeros_like(c)]))


def _row_chunks(r):
    if r == 1:
        n = 4
        return [(slice(i * (T // n), (i + 1) * (T // n)),) * 2 for i in range(n)]
    per = T // r
    return [(pl.ds(j, per, stride=r), slice(j * per, (j + 1) * per)) for j in range(r)]


def _rope_fwd(qkv, tabs):
    def body(x_ref, c_ref, s1_ref, s2_ref, o_ref):
        g = lax.rem(lax.div(pl.program_id(0), 2), 3)
        for gi, r in enumerate(DIL):
            @pl.when(g == gi)
            def _(r=r):
                for tok, prm in _row_chunks(r):
                    x = x_ref[tok, :]
                    y = x * c_ref[tok, :] + pltpu.roll(x, 8, 1) * s1_ref[tok, :] + pltpu.roll(x, 120, 1) * s2_ref[tok, :]
                    o_ref[prm, :] = y.astype(BF16)

    tab = pl.BlockSpec((None, T, 128), lambda b: (lax.div(b, 6), 0, 0))
    return pl.pallas_call(
        body, grid=(18,), in_specs=[pl.BlockSpec((T, 128), lambda b: (0, b)), tab, tab, tab],
        out_specs=pl.BlockSpec((None, T, 128), lambda b: (b, 0, 0)), out_shape=jax.ShapeDtypeStruct((18, T, 128), BF16),
        compiler_params=_params(("parallel",)), name="rope_fwd",
    )(qkv, *tabs)


def _rope_bwd(d, which, tabs, out_buf):
    def body(d_ref, c_ref, s1_ref, s2_ref, *rest):
        o_ref, tok_ref = rest[-2], rest[-1]
        g = lax.div(pl.program_id(0), 2)
        for gi, r in enumerate(DIL):
            @pl.when(g == gi)
            def _(r=r):
                for tok, prm in _row_chunks(r):
                    tok_ref[tok, :] = d_ref[prm, :]
                for rows, _ in _row_chunks(1):
                    gx = tok_ref[rows, :]
                    y = gx * c_ref[rows, :] + pltpu.roll(gx * s1_ref[rows, :], 120, 1) + pltpu.roll(gx * s2_ref[rows, :], 8, 1)
                    o_ref[rows, :] = y.astype(BF16)

    tab = pl.BlockSpec((None, T, 128), lambda b: (which, 0, 0))
    ins = [d, *tabs] + ([] if out_buf is None else [out_buf])
    specs = [pl.BlockSpec((None, None, T, 128), lambda b: (lax.div(b, 2), lax.rem(b, 2), 0, 0)), tab, tab, tab]
    return pl.pallas_call(
        body, grid=(6,), in_specs=specs + ([] if out_buf is None else [ANY]),
        out_specs=pl.BlockSpec((T, 128), lambda b: (0, 6 * which + b)),
        out_shape=jax.ShapeDtypeStruct((T, 3 * A_W), BF16), scratch_shapes=[pltpu.VMEM((T, 128), F32)],
        input_output_aliases={} if out_buf is None else {4: 0},
        compiler_params=_params(("arbitrary",)), name="rope_bwd",
    )(*ins)


def _head_mask(x, lane_lo):
    lane = lax.broadcasted_iota(jnp.int32, x.shape, 1)
    keep = (lane < HD) if lane_lo else (lane >= HD)
    return jnp.where(keep, x.astype(F32), 0.0).astype(BF16)


def _band_scalars(g):
    b = pl.program_id(0)
    nbs = (T // BLK) // DIL[g]
    has_prev = jnp.where((b & (nbs - 1)) != 0, 1, 0)
    next_ok = jnp.where(((b + 1) & (nbs - 1)) != 0, 1, 0)
    return has_prev, next_ok


def _band_mask_q(has_prev):
    row = lax.broadcasted_iota(jnp.int32, (BLK, 2 * BLK), 0)
    col = lax.broadcasted_iota(jnp.int32, (BLK, 2 * BLK), 1)
    return ((col < BLK) & (col >= row) & (has_prev == 1)) | ((col >= BLK) & (col - BLK <= row))


def _band_mask_k(next_ok):
    row = lax.broadcasted_iota(jnp.int32, (2 * BLK, BLK), 0)
    col = lax.broadcasted_iota(jnp.int32, (2 * BLK, BLK), 1)
    return ((row < BLK) & (col <= row)) | ((row >= BLK) & (col >= row - BLK) & (next_ok == 1))


def _band_spec(step, which=None):
    nb = T // BLK
    at = {"cur": lambda b: b, "prev": lambda b: jnp.maximum(b - 1, 0), "next": lambda b: jnp.minimum(b + 1, nb - 1)}[step]
    if which is None:
        return pl.BlockSpec((3, 2, BLK, 128), lambda b: (0, 0, at(b), 0))
    return pl.BlockSpec((None, 3, 2, BLK, 128), lambda b: (which, 0, 0, at(b), 0))


def _band_fwd(qkv):
    nb = T // BLK

    def body(q_ref, kc_ref, kp_ref, vc_ref, vp_ref, o_ref, l_ref):
        lane = lax.broadcasted_iota(jnp.int32, (BLK, 128), 1)
        for g in range(3):
            has_prev, _ = _band_scalars(g)
            mask = _band_mask_q(has_prev)
            for p in range(2):
                qp = q_ref[g, p]
                kcat = jnp.concatenate([kp_ref[g, p], kc_ref[g, p]], axis=0)
                vcat = jnp.concatenate([vp_ref[g, p], vc_ref[g, p]], axis=0)
                o_acc = jnp.zeros((BLK, 128), F32)
                lse = jnp.zeros((BLK, 128), F32)
                for e in range(2):
                    s = _dot(_head_mask(qp, e == 0), kcat, NT)
                    s = jnp.where(mask, s, NEG)
                    m = jnp.max(s, axis=-1, keepdims=True)
                    pr = jnp.exp(s - m)
                    l = jnp.sum(pr, axis=-1, keepdims=True)
                    o_acc = o_acc + _dot(pr.astype(BF16), _head_mask(vcat, e == 0), NN) / l
                    lse = jnp.where((lane < HD) if e == 0 else (lane >= HD), m + jnp.log(l), lse)
                o_ref[g, p] = o_acc
                l_ref[g, p] = lse

    out = _band_spec("cur")
    shp = jax.ShapeDtypeStruct((3, 2, T, 128), F32)
    return pl.pallas_call(
        body, grid=(nb,),
        in_specs=[_band_spec("cur", 0), _band_spec("cur", 1), _band_spec("prev", 1), _band_spec("cur", 2), _band_spec("prev", 2)],
        out_specs=[out, out], out_shape=[shp, shp],
        compiler_params=_params(("parallel",)), name="band_fwd",
    )(qkv, qkv, qkv, qkv, qkv)


def _band_bwd(qkv, do, lse, dlt):
    nb = T // BLK

    def body(qc_ref, qn_ref, kc_ref, kp_ref, vc_ref, vp_ref, doc_ref, don_ref, lc_ref, ln_ref, dc_ref, dn_ref,
             dq_ref, dk_ref, dv_ref):
        for g in range(3):
            has_prev, next_ok = _band_scalars(g)
            mask_q = _band_mask_q(has_prev)
            mask_k = _band_mask_k(next_ok)
            for p in range(2):
                qc, qn = qc_ref[g, p], qn_ref[g, p]
                doc, don = doc_ref[g, p], don_ref[g, p]
                kc, vc = kc_ref[g, p], vc_ref[g, p]
                kcat = jnp.concatenate([kp_ref[g, p], kc], axis=0)
                vcat = jnp.concatenate([vp_ref[g, p], vc], axis=0)
                qcat = jnp.concatenate([qc, qn], axis=0)
                docat = jnp.concatenate([doc, don], axis=0)
                dq = jnp.zeros((BLK, 128), F32)
                dk = jnp.zeros((BLK, 128), F32)
                dv = jnp.zeros((BLK, 128), F32)
                for e in range(2):
                    lo = e == 0
                    col = slice(HD * e, HD * e + 1)
                    lse_c, lse_n = lc_ref[g, p, :, col], ln_ref[g, p, :, col]
                    dl_c, dl_n = dc_ref[g, p, :, col], dn_ref[g, p, :, col]
                    s = jnp.where(mask_q, _dot(_head_mask(qc, lo), kcat, NT), NEG)
                    pr = jnp.exp(s - lse_c)
                    dp = _dot(_head_mask(doc, lo), vcat, NT)
                    ds = pr * (dp - dl_c)
                    dq = dq + _dot(ds.astype(BF16), _head_mask(kcat, lo), NN)
                    qm, dom = _head_mask(qcat, lo), _head_mask(docat, lo)
                    s2 = jnp.where(mask_k, _dot(qm, kc, NT), NEG)
                    p2 = jnp.exp(s2 - jnp.concatenate([lse_c, lse_n], axis=0))
                    dv = dv + _dot(p2.astype(BF16), dom, TN)
                    dp2 = _dot(dom, vc, NT)
                    ds2 = p2 * (dp2 - jnp.concatenate([dl_c, dl_n], axis=0))
                    dk = dk + _dot(ds2.astype(BF16), qm, TN)
                dq_ref[g, p] = dq
                dk_ref[g, p] = dk
                dv_ref[g, p] = dv

    cur, nxt = _band_spec("cur"), _band_spec("next")
    shp = jax.ShapeDtypeStruct((3, 2, T, 128), F32)
    return pl.pallas_call(
        body, grid=(nb,),
        in_specs=[_band_spec("cur", 0), _band_spec("next", 0), _band_spec("cur", 1), _band_spec("prev", 1),
                  _band_spec("cur", 2), _band_spec("prev", 2), cur, nxt, cur, nxt, cur, nxt],
        out_specs=[cur, cur, cur], out_shape=[shp, shp, shp],
        compiler_params=_params(("parallel",)), name="band_bwd",
    )(qkv, qkv, qkv, qkv, qkv, qkv, do, do, lse, lse, dlt, dlt)


def _split3(x):
    hi = x.astype(BF16)
    r = x - hi.astype(F32)
    mid = r.astype(BF16)
    lo = (r - mid.astype(F32)).astype(BF16)
    return hi, mid, lo


def _dot3(x, m, dims=NN):
    hi, mid, lo = _split3(x)
    return _dot(hi, m, dims) + _dot(mid, m, dims) + _dot(lo, m, dims)


def _combine_weights(lses):
    l0, l1, l2 = lses
    m = jnp.maximum(jnp.maximum(l0, l1), l2)
    e = [jnp.exp(l0 - m), jnp.exp(l1 - m), jnp.exp(l2 - m)]
    inv = 1.0 / (e[0] + e[1] + e[2])
    return [ei * inv for ei in e]


CR = 256


def _combine_fwd(o, lse):
    def body(o_ref, l_ref, att_ref, o3_ref, l3_ref):
        for g, r in enumerate(DIL):
            for p in range(2):
                for tok, prm in _row_chunks(r):
                    o3_ref[g, p, tok, :] = o_ref[g, p, prm, :]
                    l3_ref[g, p, tok, :] = l_ref[g, p, prm, :]
        for i in range(T // CR):
            rows = slice(i * CR, (i + 1) * CR)
            for p in range(2):
                alpha = _combine_weights([l3_ref[g, p, rows, :] for g in range(3)])
                for g in range(3):
                    att_ref[rows, g * GW + p * 128: g * GW + (p + 1) * 128] = (o3_ref[g, p, rows, :] * alpha[g]).astype(BF16)

    shp = jax.ShapeDtypeStruct((3, 2, T, 128), F32)
    return pl.pallas_call(
        body, out_shape=[jax.ShapeDtypeStruct((T, A_W), BF16), shp, shp],
        compiler_params=_params(vmem=VMEM_BIG), name="combine_fwd",
    )(o, lse)


def _combine_bwd(datt, o3, l3, headsum):
    def body(d_ref, o_ref, l_ref, hs_ref, do_ref, dl_ref, tdo_ref, tdl_ref):
        hs = hs_ref[...]
        for p in range(2):
            for i in range(T // CR):
                rows = slice(i * CR, (i + 1) * CR)
                alpha = _combine_weights([l_ref[g, p, rows, :] for g in range(3)])
                total = jnp.zeros((CR, 128), F32)
                for g in range(3):
                    dg = d_ref[rows, g * GW + p * 128: g * GW + (p + 1) * 128]
                    tdo_ref[g, rows, :] = dg * alpha[g]
                    total = total + alpha[g] * _dot3(dg * o_ref[g, p, rows, :], hs)
                for g in range(3):
                    tdl_ref[g, rows, :] = alpha[g] * total
            for g, r in enumerate(DIL):
                for tok, prm in _row_chunks(r):
                    do_ref[g, p, prm, :] = tdo_ref[g, tok, :].astype(BF16)
                    dl_ref[g, p, prm, :] = tdl_ref[g, tok, :]

    return pl.pallas_call(
        body, out_shape=[jax.ShapeDtypeStruct((3, 2, T, 128), BF16), jax.ShapeDtypeStruct((3, 2, T, 128), F32)],
        scratch_shapes=[pltpu.VMEM((3, T, 128), F32), pltpu.VMEM((3, T, 128), F32)],
        compiler_params=_params(vmem=VMEM_BIG), name="combine_bwd",
    )(datt, o3, l3, headsum)


def _fox_scores(qm, k_ref, ck_ref, e, i, n):
    s = _dot(qm, k_ref[0:n, :], NT) - ck_ref[0, e:e + 1, 0:n]
    row = lax.broadcasted_iota(jnp.int32, (FQ, FQ), 0)
    col = lax.broadcasted_iota(jnp.int32, (FQ, FQ), 1)
    diag = jnp.where(col <= row, s[:, n - FQ:], NEG)
    m = jnp.max(diag, axis=-1, keepdims=True)
    if i == 0:
        pr = jnp.exp(diag - m)
        return pr, jnp.sum(pr, axis=-1, keepdims=True)
    past = s[:, :n - FQ]
    m = jnp.maximum(m, jnp.max(past, axis=-1, keepdims=True))
    p_past, p_diag = jnp.exp(past - m), jnp.exp(diag - m)
    l = jnp.sum(p_past, axis=-1, keepdims=True) + jnp.sum(p_diag, axis=-1, keepdims=True)
    return jnp.concatenate([p_past, p_diag], axis=1), l


def _fox_fwd(q, kv, c_row):
    def body(q_ref, k_ref, v_ref, cr_ref, o_ref, vm_ref):
        for e in range(2):
            vm_ref[e] = _head_mask(v_ref[...], e == 0)
        for i in range(T // FQ):
            n = (i + 1) * FQ
            rows = slice(i * FQ, n)
            acc = jnp.zeros((FQ, 128), F32)
            for e in range(2):
                qm = _head_mask(q_ref[rows, :], e == 0)
                pr, l = _fox_scores(qm, k_ref, cr_ref, e, i, n)
                acc = acc + _dot(pr.astype(BF16), vm_ref[e, 0:n, :], NN) / l
            o_ref[rows, :] = acc.astype(BF16)

    pair = pl.BlockSpec((T, 128), lambda p: (0, p))
    return pl.pallas_call(
        body, grid=(D // 128,),
        in_specs=[pair, pair, pl.BlockSpec((T, 128), lambda p: (0, D // 128 + p)), pl.BlockSpec((1, 2, T), lambda p: (p, 0, 0))],
        out_specs=pair, out_shape=jax.ShapeDtypeStruct((T, D), BF16),
        scratch_shapes=[pltpu.VMEM((2, T, 128), BF16)],
        compiler_params=_params(("parallel",), VMEM_BIG), name="fox_fwd",
    )(q, kv, kv, c_row)


def _fox_bwd(q, kv, do, c_row, init):
    def body(q_ref, k_ref, v_ref, do_ref, cr_ref, *rest):
        dq_ref, dk_ref, dv_ref, dck_ref, km_ref = rest[-5:]
        for o_ref, i_ref in zip((dk_ref, dv_ref, dck_ref), rest[:-5] or (None,) * 3):
            o_ref[...] = jnp.zeros_like(o_ref) if i_ref is None else i_ref[...]
        for e in range(2):
            km_ref[e] = _head_mask(k_ref[...], e == 0)
        for i in range(T // FQ):
            n = (i + 1) * FQ
            rows = slice(i * FQ, n)
            dq = jnp.zeros((FQ, 128), F32)
            dk = jnp.zeros((n, 128), F32)
            dv = jnp.zeros((n, 128), F32)
            for e in range(2):
                qm = _head_mask(q_ref[rows, :], e == 0)
                dom = _head_mask(do_ref[rows, :], e == 0)
                pr, l = _fox_scores(qm, k_ref, cr_ref, e, i, n)
                pr = pr * (1.0 / l)
                dp = _dot(dom, v_ref[0:n, :], NT)
                ds = pr * (dp - jnp.sum(pr * dp, axis=-1, keepdims=True))
                dsb = ds.astype(BF16)
                dq = dq + _dot(dsb, km_ref[e, 0:n, :], NN)
                dk = dk + _dot(dsb, qm, TN)
                dv = dv + _dot(pr.astype(BF16), dom, TN)
                dck_ref[0, e:e + 1, 0:n] += jnp.sum(ds, axis=0, keepdims=True)
            dk_ref[0:n, :] += dk
            dv_ref[0:n, :] += dv
            dq_ref[rows, :] = (dq * HD ** -0.5).astype(BF16)

    pair = pl.BlockSpec((T, 128), lambda p: (0, p))
    ck = pl.BlockSpec((1, 8, T), lambda p: (p, 0, 0))
    return pl.pallas_call(
        body, grid=(D // 128,),
        in_specs=[pair, pair, pl.BlockSpec((T, 128), lambda p: (0, D // 128 + p)), pair,
                  pl.BlockSpec((1, 2, T), lambda p: (p, 0, 0))] + ([] if init is None else [pair, pair, ck]),
        out_specs=[pair, pair, pair, ck],
        out_shape=[jax.ShapeDtypeStruct((T, D), BF16), jax.ShapeDtypeStruct((T, D), F32), jax.ShapeDtypeStruct((T, D), F32),
                   jax.ShapeDtypeStruct((D // 128, 8, T), F32)],
        scratch_shapes=[pltpu.VMEM((2, T, 128), BF16)],
        compiler_params=_params(("parallel",), VMEM_BIG), name="fox_bwd",
    )(q, kv, kv, do, c_row, *(init or ()))


def _tri(lower):
    r = lax.broadcasted_iota(jnp.int32, (BLK, BLK), 0)
    c = lax.broadcasted_iota(jnp.int32, (BLK, BLK), 1)
    return jnp.where((c <= r) if lower else (c >= r), 1.0, 0.0).astype(BF16)


def _gates_fwd(z, b):
    def body(z_ref, b_ref, c_ref):
        tri = _tri(True)
        carry = jnp.zeros((1, 128), F32)
        for i in range(T // BLK):
            rows = slice(i * BLK, (i + 1) * BLK)
            x = z_ref[rows, :] + b_ref[...]
            logf = jnp.minimum(x, 0.0) - jnp.log(1.0 + jnp.exp(-jnp.abs(x)))
            hi, mid, lo = _split3(logf)
            y = _dot(tri, hi, NN) + _dot(tri, mid, NN) + _dot(tri, lo, NN) + carry
            c_ref[rows, :] = y
            carry = y[BLK - 1:BLK, :]

    return pl.pallas_call(body, out_shape=jax.ShapeDtypeStruct((T, 128), F32), name="gates_fwd")(z, b)


def _gates_bwd(dc, z, b):
    def body(dc_ref, z_ref, b_ref, dz_ref, db_ref):
        tri = _tri(False)
        carry = jnp.zeros((1, 128), F32)
        db = jnp.zeros((1, 128), F32)
        for i in reversed(range(T // BLK)):
            rows = slice(i * BLK, (i + 1) * BLK)
            hi, mid, lo = _split3(dc_ref[rows, :])
            dlogf = _dot(tri, hi, NN) + _dot(tri, mid, NN) + _dot(tri, lo, NN) + carry
            carry = dlogf[0:1, :]
            x = z_ref[rows, :] + b_ref[...]
            dz = dlogf / (1.0 + jnp.exp(x))
            dz_ref[rows, :] = dz.astype(BF16)
            db = db + jnp.sum(dz, axis=0, keepdims=True)
        db_ref[...] = db

    return pl.pallas_call(
        body, out_shape=[jax.ShapeDtypeStruct((T, 128), BF16), jax.ShapeDtypeStruct((1, 128), F32)], name="gates_bwd",
    )(dc, z, b)


def _conv_pair(a_refs, cw_refs, cb_refs):
    row = lax.broadcasted_iota(jnp.int32, (T, CT), 0)
    outs = []
    for a_ref, cw_ref, cb_ref in zip(a_refs, cw_refs, cb_refs):
        z = a_ref[...]
        z1 = jnp.where(row >= 1, pltpu.roll(z, 1, 0), 0.0)
        z2 = jnp.where(row >= 2, pltpu.roll(z, 2, 0), 0.0)
        y = cw_ref[2:3, :] * z + cw_ref[1:2, :] * z1 + cw_ref[0:1, :] * z2 + cb_ref[...]
        outs.append((y, z, z1, z2))
    return outs


_GELU_K = math.sqrt(2.0 / math.pi)
N_CT = D_FF // CT


def _conv_specs():
    def at(rows, off):
        return pl.BlockSpec((rows, CT), lambda j: (0, j + off))
    return [at(T, 0), at(T, N_CT), at(3, 0), at(3, N_CT), at(1, 0), at(1, N_CT)]


def _convgate_fwd(a, cw, cb):
    def body(ag_ref, av_ref, wg_ref, wv_ref, bg_ref, bv_ref, u_ref):
        (g, _, _, _), (v, _, _, _) = _conv_pair((ag_ref, av_ref), (wg_ref, wv_ref), (bg_ref, bv_ref))
        th = jnp.tanh(_GELU_K * (g + 0.044715 * g * g * g))
        u_ref[...] = (0.5 * g * (1.0 + th) * v).astype(BF16)

    return pl.pallas_call(
        body, grid=(N_CT,), in_specs=_conv_specs(),
        out_specs=pl.BlockSpec((T, CT), lambda j: (0, j)), out_shape=jax.ShapeDtypeStruct((T, D_FF), BF16),
        compiler_params=_params(("parallel",), VMEM_BIG), name="convgate_fwd",
    )(a, a, cw, cw, cb, cb)


def _convgate_bwd(a, du, cw, cb):
    def body(ag_ref, av_ref, wg_ref, wv_ref, bg_ref, bv_ref, du_ref, da_ref, dcw_ref, dcb_ref):
        (g, gz, gz1, gz2), (v, vz, vz1, vz2) = _conv_pair((ag_ref, av_ref), (wg_ref, wv_ref), (bg_ref, bv_ref))
        du = du_ref[...].astype(F32)
        th = jnp.tanh(_GELU_K * (g + 0.044715 * g * g * g))
        gelu = 0.5 * g * (1.0 + th)
        dgelu = 0.5 * (1.0 + th) + 0.5 * g * (1.0 - th * th) * _GELU_K * (1.0 + 3 * 0.044715 * g * g)
        row = lax.broadcasted_iota(jnp.int32, (T, CT), 0)
        eye = lax.broadcasted_iota(jnp.int32, (CT, CT), 0) == lax.broadcasted_iota(jnp.int32, (CT, CT), 1)
        ones = jnp.ones((8, T), BF16)
        for h, (d, z, w_ref) in enumerate(((du * v * dgelu, gz, wg_ref), (du * gelu, vz, wv_ref))):
            d1 = jnp.where(row < T - 1, pltpu.roll(d, T - 1, 0), 0.0)
            d2 = jnp.where(row < T - 2, pltpu.roll(d, T - 2, 0), 0.0)
            da_ref[h] = (w_ref[2:3, :] * d + w_ref[1:2, :] * d1 + w_ref[0:1, :] * d2).astype(BF16)
            zb = z.astype(BF16)
            for j, dd in ((2, d), (1, d1), (0, d2)):
                g = _dot(dd.astype(BF16), zb, TN)
                dcw_ref[h, j:j + 1, :] = jnp.sum(jnp.where(eye, g, 0.0), axis=0, keepdims=True)
            dcb_ref[h] = _dot(ones, d.astype(BF16), NN)[0:1, :]

    def both(rows):
        return pl.BlockSpec((2, rows, CT), lambda j: (0, 0, j))

    return pl.pallas_call(
        body, grid=(N_CT,),
        in_specs=_conv_specs() + [pl.BlockSpec((T, CT), lambda j: (0, j))],
        out_specs=[both(T), both(3), both(1)],
        out_shape=[jax.ShapeDtypeStruct((2, T, D_FF), BF16), jax.ShapeDtypeStruct((2, 3, D_FF), F32),
                   jax.ShapeDtypeStruct((2, 1, D_FF), F32)],
        compiler_params=_params(("parallel",), VMEM_BIG), name="convgate_bwd",
    )(a, a, cw, cw, cb, cb, du)


def _halves_a(tm, tn, tk):
    per = D_FF // tk
    return lambda i, j, k: (lax.div(k, per), i, lax.rem(k, per))


def _halves_b(tm, tn, tk):
    per = D_FF // tn
    return lambda i, j, k: (lax.div(j, per), k, lax.rem(j, per))


def _adamw(w, m, v, g, *, name):
    r, c = w.shape
    tr = r
    if r * c > 256 * 1024:
        for cand in range(8, r, 8):
            if r % cand == 0 and cand * c <= 256 * 1024:
                tr = cand

    def body(w_ref, m_ref, v_ref, g_ref, d_ref, nm_ref, nv_ref):
        gv = g_ref[...]
        mn = ADAM_B1 * m_ref[...] + (1.0 - ADAM_B1) * gv
        vn = ADAM_B2 * v_ref[...] + (1.0 - ADAM_B2) * (gv * gv)
        m_hat = mn * (1.0 / (1.0 - ADAM_B1 ** ADAM_STEP))
        v_hat = vn * (1.0 / (1.0 - ADAM_B2 ** ADAM_STEP))
        d_ref[...] = -ADAM_LR * (m_hat / (jnp.sqrt(v_hat) + ADAM_EPS) + ADAM_WD * w_ref[...])
        nm_ref[...] = mn
        nv_ref[...] = vn

    blk = pl.BlockSpec((tr, c), lambda i: (i, 0))
    shp = jax.ShapeDtypeStruct((r, c), F32)
    return pl.pallas_call(
        body, grid=(r // tr,), in_specs=[blk] * 4, out_specs=[blk] * 3, out_shape=[shp] * 3,
        compiler_params=_params(("parallel",)), name=name,
    )(w, m, v, g)


def _place():
    x, y, c = lax.axis_index("x"), lax.axis_index("y"), lax.axis_index("c")
    chips = [(1 - x, y), (x, 1 - y), (1 - x, 1 - y)]
    return x, y, c, chips


def _window(ref, kind, s, half=None):
    lead = () if half is None else (half,)
    b, c = ref.shape[-2], ref.shape[-1]
    if kind == "col":
        return ref.at[lead + (slice(None), slice(None), pl.ds(s * (c // N_CHIPS), c // N_CHIPS))]
    if kind == "row":
        return ref.at[lead + (slice(None), pl.ds(s * (b // N_CHIPS), b // N_CHIPS), slice(None))]
    return ref.at[lead + (s,)]


def _allgather(tensors, kinds, *, name):
    n = len(tensors)

    def body(*refs):
        bufs = refs[n:2 * n]
        send, recv = refs[2 * n:]
        x, y, c, chips = _place()
        me = 2 * x + y
        sib = (x, y, 1 - c)

        def rcopy(i, k, win, to):
            return pltpu.make_async_remote_copy(src_ref=win, dst_ref=win, send_sem=send.at[i * 6 + k], recv_sem=recv.at[i * 6 + k],
                                                device_id=to, device_id_type=MESH)

        started = []
        for i in range(n):
            for k, (px, py) in enumerate(chips):
                cp = rcopy(i, k, _window(bufs[i], kinds[i], me, c), (px, py, c))
                cp.start()
                started.append(cp)
        for i in range(n):
            for k, (px, py) in enumerate(chips):
                landed = _window(bufs[i], kinds[i], 2 * px + py, c)
                rcopy(i, k, landed, (px, py, c)).wait_recv()
                fw = rcopy(i, 3 + k, landed, sib)
                fw.start()
                started.append(fw)
        for i in range(n):
            for k, (px, py) in enumerate(chips):
                rcopy(i, 3 + k, _window(bufs[i], kinds[i], 2 * px + py, 1 - c), sib).wait_recv()
        for cp in started:
            cp.wait_send()

    return pl.pallas_call(
        body, in_specs=[ANY] * n, out_specs=[ANY] * n,
        out_shape=[jax.ShapeDtypeStruct(t.shape, t.dtype) for t in tensors],
        scratch_shapes=[pltpu.SemaphoreType.DMA((6 * n,)), pltpu.SemaphoreType.DMA((6 * n,))],
        input_output_aliases={i: i for i in range(n)},
        name=name,
    )(*tensors)


def _rows_tile(rows, cols, sub):
    best = None
    for t in range(sub, rows + 1, sub):
        if rows % t == 0 and t * cols <= 512 * 1024:
            best = t
    return rows if best is None else best


def _sequencer(name, cid, n_sems, peers_of, body):
    @pl.kernel(mesh=plsc.ScalarSubcoreMesh(axis_name="seq", num_cores=1), name=name,
               scratch_types=(pltpu.SemaphoreType.DMA((n_sems,)), pltpu.SemaphoreType.DMA((n_sems,))),
               compiler_params=pltpu.CompilerParams(collective_id=cid))
    def launch(send, recv):
        x, y, c, chips = _place()
        peers = peers_of(x, y, c, chips)
        barrier = pltpu.get_barrier_semaphore()
        for peer in peers:
            pl.semaphore_signal(barrier, inc=1, device_id=peer, device_id_type=MESH)
        pl.semaphore_wait(barrier, len(peers))
        body(send, recv)

    launch()


def _half_of_full(ref, kind, h):
    if kind == "col":
        b = ref.shape[0]
        return ref.at[pl.ds(h * (b // 2), b // 2), :]
    if kind == "row":
        c = ref.shape[1]
        return ref.at[:, pl.ds(h * (c // 2), c // 2)]
    b = ref.shape[1]
    return ref.at[:, pl.ds(h * (b // 2), b // 2), :]


def _half_shape(full, kind):
    if kind == "col":
        return (full[0] // 2, full[1])
    if kind == "row":
        return (full[0], full[1] // 2)
    return (full[0], full[1] // 2, full[2])


def _win_of_half(ref, kind, s):
    if kind == "col":
        c = ref.shape[1]
        return ref.at[:, pl.ds(s * (c // N_CHIPS), c // N_CHIPS)]
    if kind == "row":
        b = ref.shape[0]
        return ref.at[pl.ds(s * (b // N_CHIPS), b // N_CHIPS), :]
    return ref.at[s]


def _win_shape(half, kind):
    if kind == "col":
        return (half[0], half[1] // N_CHIPS)
    if kind == "row":
        return (half[0] // N_CHIPS, half[1])
    return half[1:]


def _seq_swap(parts, kinds, *, name):
    n = len(parts)
    srcs = [jax.new_ref(p, memory_space=pltpu.MemorySpace.HBM) for p in parts]
    outs = [jax.empty_ref(jax.ShapeDtypeStruct(_half_shape(p.shape, k), p.dtype), memory_space=pltpu.MemorySpace.HBM)
            for p, k in zip(parts, kinds)]

    def body(send, recv):
        x, y, c, _ = _place()
        cps = []
        for i in range(n):
            cp = pltpu.make_async_remote_copy(src_ref=_half_of_full(srcs[i], kinds[i], 1 - c), dst_ref=outs[i], send_sem=send.at[i],
                                              recv_sem=recv.at[i], device_id=(x, y, 1 - c), device_id_type=MESH)
            cp.start()
            cps.append(cp)
        for cp in cps:
            cp.wait()

    _sequencer(name, 2, n, lambda x, y, c, chips: [(x, y, 1 - c)], body)
    return [o[...] for o in outs]


def _seq_scatter(halves, kinds, *, name):
    n = len(halves)
    srcs = [jax.new_ref(h, memory_space=pltpu.MemorySpace.HBM) for h in halves]
    outs = [jax.empty_ref(jax.ShapeDtypeStruct((3,) + _win_shape(h.shape, k), h.dtype), memory_space=pltpu.MemorySpace.HBM)
            for h, k in zip(halves, kinds)]

    def body(send, recv):
        x, y, c, chips = _place()
        cps = []
        for i in range(n):
            for k, (px, py) in enumerate(chips):
                cp = pltpu.make_async_remote_copy(src_ref=_win_of_half(srcs[i], kinds[i], 2 * px + py), dst_ref=outs[i].at[k],
                                                  send_sem=send.at[3 * i + k], recv_sem=recv.at[3 * i + k],
                                                  device_id=(px, py, c), device_id_type=MESH)
                cp.start()
                cps.append(cp)
        for cp in cps:
            cp.wait()

    _sequencer(name, 3, 3 * n, lambda x, y, c, chips: [(px, py, c) for px, py in chips], body)
    return [o[...] for o in outs]


def _add_half(g, p, kind, where, after, *, name):
    if kind == "slab":
        s, b2, c = p.shape
        tr = _rows_tile(b2, c, 16)
        nr = b2 // tr
        grid = (s, nr)
        g_spec = pl.BlockSpec((None, tr, c), lambda i, r, w: (i, w[1] * nr + r, 0))
        p_spec = pl.BlockSpec((None, tr, c), lambda i, r, w: (i, r, 0))
    elif kind == "col":
        b2, c = p.shape
        tr = _rows_tile(b2, c, 16)
        nr = b2 // tr
        grid = (1, nr)
        g_spec = pl.BlockSpec((tr, c), lambda i, r, w: (w[1] * nr + r, 0))
        p_spec = pl.BlockSpec((tr, c), lambda i, r, w: (r, 0))
    else:
        b, c2 = p.shape
        tr = _rows_tile(b, c2, 16)
        grid = (1, b // tr)
        g_spec = pl.BlockSpec((tr, c2), lambda i, r, w: (r, w[1]))
        p_spec = pl.BlockSpec((tr, c2), lambda i, r, w: (r, 0))

    def body(w_ref, g_ref, p_ref, *rest):
        o_ref = rest[-1]
        o_ref[...] = (g_ref[...].astype(F32) + p_ref[...].astype(F32)).astype(o_ref.dtype)

    extra = [] if after is None else [after]
    return pl.pallas_call(
        body,
        grid_spec=pltpu.PrefetchScalarGridSpec(num_scalar_prefetch=1, grid=grid, in_specs=[g_spec, p_spec] + [ANY] * len(extra),
                                               out_specs=p_spec),
        out_shape=jax.ShapeDtypeStruct(p.shape, g.dtype),
        compiler_params=_params(("parallel", "parallel")), name=name,
    )(where, g, p, *extra)


def _sum_chips(r, h, kind, where, layer, layers, out_buf, after, *, name):
    _, br, cr = r.shape
    tr = _rows_tile(br, cr, 16)
    nr = br // tr
    if kind == "col":
        h_spec = pl.BlockSpec((tr, cr), lambda j, w: (j, w[0]))
        o_shape, o_spec = (layers, 2 * br, cr), pl.BlockSpec((None, tr, cr), lambda j, w: (layer, w[1] * nr + j, 0))
    elif kind == "row":
        h_spec = pl.BlockSpec((tr, cr), lambda j, w: (w[0] * nr + j, 0))
        o_shape, o_spec = (layers, br, 2 * cr), pl.BlockSpec((None, tr, cr), lambda j, w: (layer, j, w[1]))
    else:
        h_spec = pl.BlockSpec((None, tr, cr), lambda j, w: (w[0], j, 0))
        o_shape, o_spec = (layers, 2 * br, cr), pl.BlockSpec((None, tr, cr), lambda j, w: (layer, w[1] * nr + j, 0))

    def body(w_ref, h_ref, r0_ref, r1_ref, r2_ref, *rest):
        o_ref, t_ref = rest[-2], rest[-1]
        o_ref[...] = ((h_ref[...].astype(F32) + r0_ref[...].astype(F32)) + r1_ref[...].astype(F32)) + r2_ref[...].astype(F32)
        t_ref[...] = jnp.zeros_like(t_ref)

    def slot(k):
        return pl.BlockSpec((None, tr, cr), lambda j, w: (k, j, 0))

    ins, specs, alias = [h, r, r, r], [h_spec, slot(0), slot(1), slot(2)], {}
    if after is not None:
        ins.append(after)
        specs.append(ANY)
    if out_buf is not None:
        alias = {1 + len(ins): 0}
        ins.append(out_buf)
        specs.append(ANY)
    return pl.pallas_call(
        body,
        grid_spec=pltpu.PrefetchScalarGridSpec(num_scalar_prefetch=1, grid=(nr,), in_specs=specs,
                                               out_specs=[o_spec, pl.BlockSpec((8, 128), lambda j, w: (0, 0))]),
        out_shape=[jax.ShapeDtypeStruct(o_shape, F32), jax.ShapeDtypeStruct((8, 128), F32)], input_output_aliases=alias,
        compiler_params=_params(("arbitrary",)), name=name,
    )(where, *ins)


def _join_halves(tensors, kinds, *, name):
    n = len(tensors)

    def mine(ref, kind, h):
        if kind == "row":
            c = ref.shape[2]
            return ref.at[:, :, pl.ds(h * (c // 2), c // 2)]
        b = ref.shape[1]
        return ref.at[:, pl.ds(h * (b // 2), b // 2), :]

    def body(*refs):
        bufs = refs[n:2 * n]
        send, recv = refs[2 * n:]
        x, y, c, _ = _place()
        cps = []
        for i in range(n):
            part = mine(bufs[i], kinds[i], c)
            cp = pltpu.make_async_remote_copy(src_ref=part, dst_ref=part, send_sem=send.at[i],
                                              recv_sem=recv.at[i], device_id=(x, y, 1 - c), device_id_type=MESH)
            cp.start()
            cps.append(cp)
        for i in range(n):
            other = mine(bufs[i], kinds[i], 1 - c)
            pltpu.make_async_remote_copy(src_ref=other, dst_ref=other, send_sem=send.at[i],
                                         recv_sem=recv.at[i], device_id=(x, y, 1 - c), device_id_type=MESH).wait_recv()
        for cp in cps:
            cp.wait_send()

    return pl.pallas_call(
        body, in_specs=[ANY] * n, out_specs=[ANY] * n,
        out_shape=[jax.ShapeDtypeStruct(t.shape, t.dtype) for t in tensors],
        scratch_shapes=[pltpu.SemaphoreType.DMA((n,)), pltpu.SemaphoreType.DMA((n,))],
        input_output_aliases={i: i for i in range(n)},
        name=name,
    )(*tensors)


def _win(ref, kind, s, h=None):
    if kind == "col":
        b, c = ref.shape
        cols = pl.ds(s * (c // N_CHIPS), c // N_CHIPS)
        return ref.at[:, cols] if h is None else ref.at[pl.ds(h * (b // 2), b // 2), cols]
    if kind == "row":
        b, c = ref.shape
        rows = pl.ds(s * (b // N_CHIPS), b // N_CHIPS)
        return ref.at[rows, :] if h is None else ref.at[rows, pl.ds(h * (c // 2), c // 2)]
    b = ref.shape[1]
    return ref.at[s] if h is None else ref.at[s, pl.ds(h * (b // 2), b // 2)]


def _half(ref, kind, h):
    b, c = ref.shape
    if kind == "row":
        return ref.at[:, pl.ds(h * (c // 2), c // 2)]
    return ref.at[pl.ds(h * (b // 2), b // 2), :]


def _full_shape(shard_shape, kind):
    b, c = shard_shape
    return {"col": (b, N_CHIPS * c), "row": (N_CHIPS * b, c), "slab": (N_CHIPS, b, c)}[kind]


def _gather_body(srcs, outs, kinds, send, recv):
    x, y, c, chips = _place()
    me = 2 * x + y
    sib = (x, y, 1 - c)

    def rcopy(i, k, src, dst, to):
        return pltpu.make_async_remote_copy(src_ref=src, dst_ref=dst, send_sem=send.at[7 * i + k], recv_sem=recv.at[7 * i + k],
                                            device_id=to, device_id_type=MESH)

    started = []
    for i, (src, out, kind) in enumerate(zip(srcs, outs, kinds)):
        own = rcopy(i, 6, src, _win(out, kind, me), sib)
        own.start()
        started.append(own)
        for k, (px, py) in enumerate(chips):
            cp = rcopy(i, k, _half(src, kind, c), _win(out, kind, me, c), (px, py, c))
            cp.start()
            started.append(cp)
    for i, (out, kind) in enumerate(zip(outs, kinds)):
        for k, (px, py) in enumerate(chips):
            landed = _win(out, kind, 2 * px + py, c)
            rcopy(i, k, landed, landed, (px, py, c)).wait_recv()
            fw = rcopy(i, 3 + k, landed, landed, sib)
            fw.start()
            started.append(fw)
    for i, (src, out, kind) in enumerate(zip(srcs, outs, kinds)):
        for k, (px, py) in enumerate(chips):
            other = _win(out, kind, 2 * px + py, 1 - c)
            rcopy(i, 3 + k, other, other, sib).wait_recv()
        rcopy(i, 6, src, _win(out, kind, me), sib).wait_recv()
    for cp in started:
        cp.wait_send()


def _seq_gather(shards, kinds, *, name, cid):
    n = len(shards)
    srcs = [jax.new_ref(s, memory_space=pltpu.MemorySpace.HBM) for s in shards]
    outs = [jax.empty_ref(jax.ShapeDtypeStruct(_full_shape(s.shape, k), s.dtype), memory_space=pltpu.MemorySpace.HBM)
            for s, k in zip(shards, kinds)]

    @pl.kernel(mesh=plsc.ScalarSubcoreMesh(axis_name="seq", num_cores=1), name=name,
               scratch_types=(pltpu.SemaphoreType.DMA((7 * n,)), pltpu.SemaphoreType.DMA((7 * n,))),
               compiler_params=pltpu.CompilerParams(collective_id=cid))
    def launch(send, recv):
        x, y, c, chips = _place()
        barrier = pltpu.get_barrier_semaphore()
        for px, py in chips:
            pl.semaphore_signal(barrier, inc=1, device_id=(px, py, c), device_id_type=MESH)
        pl.semaphore_signal(barrier, inc=1, device_id=(x, y, 1 - c), device_id_type=MESH)
        pl.semaphore_wait(barrier, 4)
        _gather_body(srcs, outs, kinds, send, recv)

    launch()
    return [o[...] for o in outs]


KIND = dict(w_qkv_a="slab", w_o_a="col", w_q_b="row", w_o_b="row", w_kvf="slab", w_up="col", w_down="row", small="slab")
LAYERS = dict(w_qkv_a=N_A, w_o_a=N_A, w_q_b=DEPTH - N_A, w_o_b=DEPTH - N_A, w_kvf=1, w_up=DEPTH, w_down=DEPTH, small=1)
SMALL_W = 1792
SMALL_ROWS = 8


class _Reducer:
    def __init__(self, where):
        self.where = where
        self.acc = {nm: None for nm in KIND}
        self.pending = None

    def __call__(self, group, tag):
        names, layers, parts = zip(*group)
        kinds = [KIND[nm] for nm in names]
        summed = self._sum_pending(after=parts[-1])
        sib = _seq_swap(list(parts), kinds, name="reduce_swap_" + tag)
        halves = []
        for g, p, k, nm in zip(parts, sib, kinds, names):
            halves.append(_add_half(g, p, k, self.where, halves[-1] if halves else None, name="reduce_add_" + nm))
        landed = _seq_scatter(halves, kinds, name="reduce_scatter_" + tag)
        self.pending = (names, layers, landed, halves, kinds)
        return [halves[-1], summed]

    def flush(self, after):
        return self._sum_pending(after)

    def _sum_pending(self, after):
        if self.pending is None:
            return None
        for nm, l, r, h, k in zip(*self.pending):
            self.acc[nm], after = _sum_chips(r, h, k, self.where, l, LAYERS[nm], self.acc[nm], after, name="reduce_sum_" + nm)
        self.pending = None
        return after

    def finish(self):
        self._sum_pending(after=None)
        names = list(KIND)
        joined = _join_halves([self.acc[nm] for nm in names], [KIND[nm] for nm in names], name="reduce_pair_join")
        return dict(zip(names, joined))


def _headsum_matrix():
    r = lax.broadcasted_iota(jnp.int32, (128, 128), 0) // HD
    c = lax.broadcasted_iota(jnp.int32, (128, 128), 1) // HD
    return jnp.where(r == c, 1.0, 0.0).astype(BF16)


def kernel(x, norm_gains, w_qkv_a, w_o_a, w_q_b, w_o_b, kv_norm, w_kvf, b_f, w_up, conv_w, conv_b, w_down, loss_target, m_norm_gains, m_w_qkv_a, m_w_o_a, m_w_q_b, m_w_o_b, m_kv_norm, m_w_kvf, m_b_f, m_w_up, m_conv_w, m_conv_b, m_w_down, v_norm_gains, v_w_qkv_a, v_w_o_a, v_w_q_b, v_w_o_b, v_kv_norm, v_w_kvf, v_b_f, v_w_up, v_conv_w, v_conv_b, v_w_down):
    xi, yi, ci = lax.axis_index("x"), lax.axis_index("y"), lax.axis_index("c")
    chip = 2 * xi + yi
    where = jnp.stack([chip, ci]).astype(jnp.int32)
    ws = dict(norm_gains=norm_gains, w_qkv_a=w_qkv_a, w_o_a=w_o_a, w_q_b=w_q_b, w_o_b=w_o_b, kv_norm=kv_norm, w_kvf=w_kvf,
              b_f=b_f, w_up=w_up, conv_w=conv_w, conv_b=conv_b, w_down=w_down)
    ms = dict(norm_gains=m_norm_gains, w_qkv_a=m_w_qkv_a, w_o_a=m_w_o_a, w_q_b=m_w_q_b, w_o_b=m_w_o_b, kv_norm=m_kv_norm,
              w_kvf=m_w_kvf, b_f=m_b_f, w_up=m_w_up, conv_w=m_conv_w, conv_b=m_conv_b, w_down=m_w_down)
    vs = dict(norm_gains=v_norm_gains, w_qkv_a=v_w_qkv_a, w_o_a=v_w_o_a, w_q_b=v_w_q_b, w_o_b=v_w_o_b, kv_norm=v_kv_norm,
              w_kvf=v_w_kvf, b_f=v_b_f, w_up=v_w_up, conv_w=v_conv_w, conv_b=v_conv_b, w_down=v_w_down)

    small = jnp.concatenate([
        jnp.pad(norm_gains.reshape(16, 256), ((0, 0), (0, 1408 - 256))),
        jnp.pad(conv_w.reshape(12, 1408), ((0, 4), (0, 0)))], axis=0)
    big = [nm for nm in KIND if nm != "small"]
    half = {nm: ws[nm].astype(BF16) for nm in big}
    W = {nm: [None] * LAYERS[nm] for nm in big if nm != "w_kvf"}
    g_small = None
    groups = [("0a", [("w_qkv_a", 0), ("w_o_a", 0), ("small", 0)]), ("0b", [("w_up", 0)]), ("0c", [("w_down", 0)]),
              ("1a", [("w_qkv_a", 1), ("w_o_a", 1)]), ("1b", [("w_up", 1)]), ("1c", [("w_down", 1)]),
              ("2", [("w_kvf", 0), ("w_q_b", 0), ("w_o_b", 0), ("w_up", 2), ("w_down", 2)]),
              ("3", [("w_q_b", 1), ("w_o_b", 1), ("w_up", 3), ("w_down", 3)])]
    for tag, group in groups:
        shards = [small if nm == "small" else half[nm] if nm == "w_kvf" else half[nm][i] for nm, i in group]
        got = _seq_gather(shards, [KIND[nm] for nm, _ in group], name="gather_layer" + tag, cid=1)
        for (nm, i), g in zip(group, got):
            if nm == "small":
                g_small = g
            elif nm == "w_kvf":
                W[nm] = g.transpose(1, 0, 2).reshape(D, 2 * D + 16)
            else:
                W[nm][i] = g.transpose(1, 0, 2).reshape(D, 3 * A_W) if nm == "w_qkv_a" else g
    gains = g_small[:, :16, :256].transpose(1, 0, 2).reshape(DEPTH, 4, 1, D)
    cw_full = g_small[:, 16:28, :].transpose(1, 0, 2).reshape(DEPTH, 3, 2 * D_FF)
    cb_full = conv_b.reshape(DEPTH, 1, 2 * D_FF)

    reducer = _Reducer(where)
    sq, dh = _fwd_bwd(x[0], loss_target[0], W, gains, cw_full, cb_full, kv_norm, b_f, reducer)
    loss = lax.psum(sq[0, 0] * (0.5 / D), ("x", "y", "c"))
    return _update(loss, dh[None], reducer.finish(), chip, ws, ms, vs)


def _fwd_bwd(h, target, W, gains, cw_full, cb_full, kv_norm, b_f, reduce):
    w_kv = W["w_kvf"][:, :2 * D]
    w_kvf_pad = jnp.pad(W["w_kvf"], ((0, 0), (0, 128 - 16)))
    w_f = w_kvf_pad[:, 2 * D:]
    kvn_g = kv_norm.reshape(1, D)
    bf_pad = jnp.pad(b_f, (0, 128 - 16)).reshape(1, 128)
    tabs = _rope_tables()
    headsum = _headsum_matrix()

    saved = []
    kv = zf = c_row = kvn = h_kv = None
    xn = _rms_fwd(h, gains[0][0], out_dtype=BF16, name="rms_in")
    for l in range(DEPTH):
        s = {"h": h}
        g = gains[l]
        s["xn"] = xn
        if l < N_A:
            qkv = _matmul(xn, W["w_qkv_a"][l], mode="nn", out_dtype=F32, name="mm_qkv", mnk=(T, 3 * A_W, D), tn=768)
            qkvp = _rope_fwd(qkv, tabs).reshape(3, 3, 2, T, 128)
            o_p, lse_p = _band_fwd(qkvp)
            att, o3, lse3 = _combine_fwd(o_p, lse_p)
            s.update(qkvp=qkvp, o3=o3, lse3=lse3, lse_p=lse_p, att=att)
            mix = _matmul(att, W["w_o_a"][l], mode="nn", out_dtype=F32, name="mm_oa", mnk=(T, D, A_W))
        else:
            j = l - N_A
            if l == N_A:
                h_kv = h
                kvn = _rms_fwd(h, kvn_g, out_dtype=BF16, name="rms_in")
                kv = _matmul(kvn, w_kv, mode="nn", out_dtype=BF16, name="mm_kv")
                zf = _matmul(kvn, w_f, mode="nn", out_dtype=F32, name="mm_f")
                cum = _gates_fwd(zf, bf_pad)[:, :16]
                c_row = cum.T.reshape(8, 2, T)
            q = _matmul(xn, W["w_q_b"][j], mode="nn", out_dtype=BF16, name="mm_qb", mnk=(T, D, D), alpha=HD ** -0.5)
            o = _fox_fwd(q, kv, c_row)
            s.update(q=q, o=o)
            mix = _matmul(o, W["w_o_b"][j], mode="nn", out_dtype=F32, name="mm_ob", mnk=(T, D, D))
        s["mix"] = mix
        h1, xn2 = _rms_res_in(mix, g[1], h, g[2], name="rms_res_in")
        a = _matmul(xn2, W["w_up"][l], mode="nn", out_dtype=F32, name="mm_up", mnk=(T, 2 * D_FF, D))
        u = _convgate_fwd(a, cw_full[l], cb_full[l])
        f = _matmul(u, W["w_down"][l], mode="nn", out_dtype=F32, name="mm_down", mnk=(T, D, D_FF), tm=1024, tk=D_FF)
        if l + 1 < DEPTH:
            h, xn = _rms_res_in(f, g[3], h1, gains[l + 1][0], name="rms_res_in")
        else:
            h = _rms_fwd(f, g[3], res=h1, out_dtype=F32, name="rms_res")
        s.update(h1=h1, xn2=xn2, a=a, u=u, f=f)
        saved.append(s)

    dh, sq = _loss_head(h, target)

    d_gains = [[None] * 4 for _ in range(DEPTH)]
    d_cw, d_cb = [None] * DEPTH, [None] * DEPTH
    fox_acc = None
    d_kvnorm = d_bf = token = df = None

    def dw(nm, a, b, **kw):
        return _matmul(a, b, mode="tn", out_dtype=BF16, name="mm_dw_" + nm, **kw)

    flush = getattr(reduce, "flush", lambda after: None)

    def slabs(full, width):
        return full.reshape(full.shape[0], N_CHIPS, width).transpose(1, 0, 2)

    for l in reversed(range(DEPTH)):
        s = saved[l]
        g = gains[l]
        if df is None:
            df, d_gains[l][3] = _rms_bwd(dh, s["f"], g[3], out_dtype=BF16, name="rms_bwd")
        du = _matmul(df, W["w_down"][l], mode="nt", out_dtype=F32, name="mm_down_dx", mnk=(T, D_FF, D), tn=256, after=token)
        g_down = dw("w_down", s["u"], df, tm=1408, tn=1024)
        da, d_cw[l], d_cb[l] = _convgate_bwd(s["a"], du, cw_full[l], cb_full[l])
        dxn2 = _matmul(da, W["w_up"][l], mode="nt", out_dtype=F32, name="mm_up_dx", mnk=(T, D, 2 * D_FF), tm=1024, tn=1024, tk=1408,
                       a_map=_halves_a)
        g_up = dw("w_up", s["xn2"], da, mnk=(D, 2 * D_FF, T), tn=1408, b_map=_halves_b)
        token = reduce([("w_down", l, g_down), ("w_up", l, g_up)], "ffn%d" % l)
        dh1, dmix, d_gains[l][2], d_gains[l][1] = _rms_bwd2(dxn2, s["h1"], g[2], dh, s["mix"], g[1], name="rms_bwd2")
        if l < N_A:
            datt = _matmul(dmix, W["w_o_a"][l], mode="nt", out_dtype=F32, name="mm_oa_dx", mnk=(T, A_W, D), tn=768, after=token)
            g_o = dw("w_o_a", s["att"], dmix, tm=768, tn=1024)
            do_p, dlt_p = _combine_bwd(datt, s["o3"], s["lse3"], headsum)
            dqkv = None
            for which, d in enumerate(_band_bwd(s["qkvp"], do_p, s["lse_p"], dlt_p)):
                dqkv = _rope_bwd(d, which, tabs, dqkv)
            dxn = _matmul(dqkv, W["w_qkv_a"][l], mode="nt", out_dtype=F32, name="mm_qkv_dx", mnk=(T, D, 3 * A_W), tm=1024, tn=1024, tk=3 * A_W,
                          after=[flush(dqkv)])
            g_qkv = dw("w_qkv_a", s["xn"], dqkv, tn=768)
            group = [("w_o_a", l, g_o), ("w_qkv_a", l, slabs(g_qkv, 576))]
        else:
            j = l - N_A
            do = _matmul(dmix, W["w_o_b"][j], mode="nt", out_dtype=BF16, name="mm_ob_dx", mnk=(T, D, D), after=token)
            g_o = dw("w_o_b", s["o"], dmix, tn=1024)
            dq, *fox_acc = _fox_bwd(s["q"], kv, do, c_row, fox_acc)
            dxn = _matmul(dq, W["w_q_b"][j], mode="nt", out_dtype=F32, name="mm_qb_dx", mnk=(T, D, D), after=[flush(dq)])
            g_q = dw("w_q_b", s["xn"], dq, tn=1024)
            group = [("w_o_b", j, g_o), ("w_q_b", j, g_q)]
        if l > 0 and l != N_A:
            dh, df, d_gains[l][0], d_gains[l - 1][3] = _rms_bwd2(dxn, s["h"], g[0], dh1, saved[l - 1]["f"], gains[l - 1][3],
                                                                 name="rms_bwd2")
        else:
            dh, d_gains[l][0] = _rms_bwd(dxn, s["h"], g[0], dres=dh1, out_dtype=F32, name="rms_bwd_res")
            df = None
        if l == N_A:
            dk, dv, dck = fox_acc
            dc16 = -dck[:, :2, :].reshape(16, T).T
            dzf, d_bf = _gates_bwd(jnp.pad(dc16, ((0, 0), (0, 128 - 16))), zf, bf_pad)
            dkvf = jnp.concatenate([dk.astype(BF16), dv.astype(BF16), dzf], axis=1)
            g_kvf = _matmul(kvn, dkvf, mode="tn", out_dtype=BF16, name="mm_kvf_dw", tm=512, tn=2 * D + 128)[:, :2 * D + 16]
            dkvn = _matmul(dkvf, w_kvf_pad, mode="nt", out_dtype=F32, name="mm_kvf_dx", tm=1024, tn=1024, tk=2 * D + 128)
            dh, d_kvnorm = _rms_bwd(dkvn, h_kv, kvn_g, dres=dh, out_dtype=F32, name="rms_bwd_res")
            group.append(("w_kvf", 0, slabs(g_kvf, 516)))
        token = reduce(group, "mix%d" % l)
    small_flat = jnp.concatenate([
        jnp.stack([jnp.stack(r) for r in d_gains]).reshape(-1),
        jnp.stack(d_cw).transpose(0, 2, 1, 3).reshape(-1),
        jnp.stack(d_cb).reshape(-1),
        d_kvnorm.reshape(-1), d_bf[0, :16]])
    small = jnp.pad(small_flat, (0, 2 * N_CHIPS * SMALL_ROWS * SMALL_W - small_flat.shape[0]))
    reduce([("small", 0, small.reshape(N_CHIPS, 2 * SMALL_ROWS, SMALL_W))], "small")
    return sq, dh


def _update(loss, grad_x, reduced, chip, ws, ms, vs):
    red_s = reduced.pop("small")
    buf_s = lax.dynamic_update_slice(jnp.zeros((2, N_CHIPS, SMALL_ROWS, SMALL_W), F32), red_s.reshape(2, 1, SMALL_ROWS, SMALL_W),
                                     (0, chip, 0, 0))
    (all_s,) = _allgather([buf_s], ["slab"], name="gather_small_grads")
    sflat = all_s.transpose(1, 0, 2, 3).reshape(-1)

    grads = {nm: r.reshape(ws[nm].shape) for nm, r in reduced.items()}
    o = 0
    g_gains_full = sflat[o:o + 16 * D].reshape(DEPTH, 4, D); o += 16 * D
    g_cw_full = sflat[o:o + 12 * 2 * D_FF].reshape(DEPTH, 3, 2 * D_FF); o += 12 * 2 * D_FF
    grads["conv_b"] = sflat[o:o + 4 * 2 * D_FF].reshape(DEPTH, 2 * D_FF); o += 4 * 2 * D_FF
    grads["kv_norm"] = sflat[o:o + D]; o += D
    grads["b_f"] = sflat[o:o + 16]
    grads["norm_gains"] = lax.dynamic_slice_in_dim(g_gains_full, chip * 256, 256, axis=2)
    grads["conv_w"] = lax.dynamic_slice_in_dim(g_cw_full, chip * 1408, 1408, axis=2)

    names = ["norm_gains", "w_qkv_a", "w_o_a", "w_q_b", "w_o_b", "kv_norm", "w_kvf", "b_f", "w_up", "conv_w", "conv_b", "w_down"]
    deltas, new_m, new_v = {}, {}, {}
    for nm in names:
        shp = ws[nm].shape
        two = (math.prod(shp[:-1]), shp[-1]) if len(shp) > 1 else (1, shp[0])
        d, m2, v2 = _adamw(ws[nm].reshape(two), ms[nm].reshape(two), vs[nm].reshape(two), grads[nm].reshape(two),
                           name="adamw_" + nm)
        deltas[nm], new_m[nm], new_v[nm] = d.reshape(shp), m2.reshape(shp), v2.reshape(shp)

    return (loss, grad_x, *[grads[nm] for nm in names], *[deltas[nm] for nm in names],
            *[new_m[nm] for nm in names], *[new_v[nm] for nm in names])
```

```python
import math

import jax
import jax.numpy as jnp
from jax import lax
from jax.experimental import pallas as pl
from jax.experimental.pallas import tpu as pltpu
from jax.experimental.pallas import tpu_sc as plsc

F32 = jnp.float32
BF16 = jnp.bfloat16
MESH = pl.DeviceIdType.MESH
ANY = pl.BlockSpec(memory_space=pl.ANY)

T = 2048
D = 1024
HD = 64
DEPTH = 4
N_A = 2
A_W = 768
GW = 256
DIL = (1, 4, 16)
BLK = 128
D_FF = 2816
ROPE_THETA = 500000.0
EPS = 1e-6
NEG = -1e30
N_CHIPS = 4
FQ = 512
CT = 128
VMEM_BIG = 48 * 1024 * 1024

ADAM_LR, ADAM_B1, ADAM_B2, ADAM_EPS, ADAM_WD, ADAM_STEP = 0.001, 0.9, 0.999, 1e-08, 0.01, 10

NN = (((1,), (0,)), ((), ()))
NT = (((1,), (1,)), ((), ()))
TN = (((0,), (0,)), ((), ()))


def _dot(a, b, dims):
    return lax.dot_general(a, b, dims, preferred_element_type=F32)


def _pick(dim, pref):
    if dim <= pref:
        return dim
    best = None
    for t in range(128, pref + 1, 128):
        if dim % t == 0:
            best = t
    assert best is not None, (dim, pref)
    return best


def _params(sem=None, vmem=None):
    kw = {}
    if sem is not None:
        kw["dimension_semantics"] = sem
    if vmem is not None:
        kw["vmem_limit_bytes"] = vmem
    return pltpu.CompilerParams(**kw)


def _matmul(a, b, *, mode, out_dtype, name, mnk=None, alpha=None, tm=2048, tn=512, tk=2048, a_map=None, b_map=None, after=None):
    if mnk is not None:
        M, N, K = mnk
    elif mode == "nn":
        (M, K), (_, N) = a.shape, b.shape
    elif mode == "nt":
        (M, K), (N, _) = a.shape, b.shape
    else:
        (K, M), (_, N) = a.shape, b.shape
    tm, tn, tk = _pick(M, tm), _pick(N, tn), _pick(K, tk)
    nk = K // tk
    dims = {"nn": NN, "nt": NT, "tn": TN}[mode]
    after = [t for t in (after or ()) if t is not None]
    n_in = 2 + len(after)

    def body(*refs):
        a_ref, b_ref = refs[0], refs[1]
        o_ref = refs[n_in]
        k = pl.program_id(2)

        def finish(r):
            if alpha is not None:
                r = r * alpha
            o_ref[...] = r.astype(out_dtype)

        def product():
            return _dot(a_ref[...], b_ref[...], dims)

        if nk == 1:
            finish(product())
            return
        acc_ref = refs[n_in + 1]

        @pl.when(k == 0)
        def _():
            acc_ref[...] = product()

        @pl.when((k > 0) & (k < nk - 1))
        def _():
            acc_ref[...] += product()

        @pl.when(k == nk - 1)
        def _():
            finish(acc_ref[...] + product())

    a_blk = (tk, tm) if mode == "tn" else (tm, tk)
    b_blk = (tn, tk) if mode == "nt" else (tk, tn)
    if a_map is not None:
        a_spec = pl.BlockSpec((None,) + a_blk, a_map(tm, tn, tk))
    elif mode == "tn":
        a_spec = pl.BlockSpec(a_blk, lambda i, j, k: (k, i))
    else:
        a_spec = pl.BlockSpec(a_blk, lambda i, j, k: (i, k))
    if b_map is not None:
        b_spec = pl.BlockSpec((None,) + b_blk, b_map(tm, tn, tk))
    elif mode == "nt":
        b_spec = pl.BlockSpec(b_blk, lambda i, j, k: (j, k))
    else:
        b_spec = pl.BlockSpec(b_blk, lambda i, j, k: (k, j))
    return pl.pallas_call(
        body,
        grid=(M // tm, N // tn, nk),
        in_specs=[a_spec, b_spec] + [ANY] * len(after),
        out_specs=pl.BlockSpec((tm, tn), lambda i, j, k: (i, j)),
        out_shape=jax.ShapeDtypeStruct((M, N), out_dtype),
        scratch_shapes=[pltpu.VMEM((tm, tn), F32)] if nk > 1 else [],
        compiler_params=_params(("parallel", "parallel", "arbitrary"), VMEM_BIG),
        name=name,
    )(a, b, *after)


def _rms_fwd(x, g, *, out_dtype, name, res=None, tr=256):
    n, d = x.shape

    def body(*refs):
        x_ref, g_ref = refs[0], refs[1]
        o_ref = refs[-1]
        xv = x_ref[...].astype(F32)
        y = xv * lax.rsqrt(jnp.mean(xv * xv, axis=-1, keepdims=True) + EPS) * g_ref[...]
        if res is not None:
            y = y + refs[2][...]
        o_ref[...] = y.astype(out_dtype)

    row = pl.BlockSpec((tr, d), lambda i: (i, 0))
    vec = pl.BlockSpec((1, d), lambda i: (0, 0))
    ins = [x, g] + ([] if res is None else [res])
    specs = [row, vec] + ([] if res is None else [row])
    return pl.pallas_call(
        body, grid=(n // tr,), in_specs=specs, out_specs=row,
        out_shape=jax.ShapeDtypeStruct((n, d), out_dtype),
        compiler_params=_params(("parallel",)), name=name,
    )(*ins)


def _rms_bwd(dy, x, g, *, out_dtype, name, dres=None, tr=512):
    n, d = x.shape

    def body(*refs):
        dy_ref, x_ref, g_ref = refs[0], refs[1], refs[2]
        dx_ref, dg_ref = refs[-2], refs[-1]
        xv = x_ref[...].astype(F32)
        dyv = dy_ref[...].astype(F32)
        rstd = lax.rsqrt(jnp.mean(xv * xv, axis=-1, keepdims=True) + EPS)
        xhat = xv * rstd
        dxh = dyv * g_ref[...]
        dx = rstd * (dxh - xhat * jnp.mean(dxh * xhat, axis=-1, keepdims=True))
        if dres is not None:
            dx = dx + refs[3][...]
        dx_ref[...] = dx.astype(out_dtype)

        @pl.when(pl.program_id(0) == 0)
        def _():
            dg_ref[...] = jnp.zeros_like(dg_ref)

        dg_ref[...] += jnp.sum(dyv * xhat, axis=0, keepdims=True)

    row = pl.BlockSpec((tr, d), lambda i: (i, 0))
    vec = pl.BlockSpec((1, d), lambda i: (0, 0))
    ins = [dy, x, g] + ([] if dres is None else [dres])
    specs = [row, row, vec] + ([] if dres is None else [row])
    return pl.pallas_call(
        body, grid=(n // tr,), in_specs=specs, out_specs=[row, vec],
        out_shape=[jax.ShapeDtypeStruct((n, d), out_dtype), jax.ShapeDtypeStruct((1, d), F32)],
        compiler_params=_params(("arbitrary",), VMEM_BIG), name=name,
    )(*ins)


def _rms_res_in(x, g_res, res, g_in, *, name, tr=512):
    n, d = x.shape

    def body(x_ref, gr_ref, r_ref, gi_ref, h_ref, n_ref):
        xv = x_ref[...].astype(F32)
        h = r_ref[...] + xv * lax.rsqrt(jnp.mean(xv * xv, axis=-1, keepdims=True) + EPS) * gr_ref[...]
        h_ref[...] = h
        n_ref[...] = (h * lax.rsqrt(jnp.mean(h * h, axis=-1, keepdims=True) + EPS) * gi_ref[...]).astype(BF16)

    row = pl.BlockSpec((tr, d), lambda i: (i, 0))
    vec = pl.BlockSpec((1, d), lambda i: (0, 0))
    return pl.pallas_call(
        body, grid=(n // tr,), in_specs=[row, vec, row, vec], out_specs=[row, row],
        out_shape=[jax.ShapeDtypeStruct((n, d), F32), jax.ShapeDtypeStruct((n, d), BF16)],
        compiler_params=_params(("parallel",), VMEM_BIG), name=name,
    )(x, g_res, res, g_in)


def _rms_bwd2(dy, x, g, dres, x2, g2, *, name, tr=512):
    n, d = x.shape

    def one(dyv, xv, gv):
        rstd = lax.rsqrt(jnp.mean(xv * xv, axis=-1, keepdims=True) + EPS)
        xhat = xv * rstd
        dxh = dyv * gv
        return rstd * (dxh - xhat * jnp.mean(dxh * xhat, axis=-1, keepdims=True)), jnp.sum(dyv * xhat, axis=0, keepdims=True)

    def body(dy_ref, x_ref, g_ref, r_ref, x2_ref, g2_ref, dx_ref, d2_ref, dg_ref, dg2_ref):
        dx, dg = one(dy_ref[...].astype(F32), x_ref[...].astype(F32), g_ref[...])
        dx = dx + r_ref[...]
        dx_ref[...] = dx
        d2, dg2 = one(dx, x2_ref[...].astype(F32), g2_ref[...])
        d2_ref[...] = d2.astype(BF16)

        @pl.when(pl.program_id(0) == 0)
        def _():
            dg_ref[...] = jnp.zeros_like(dg_ref)
            dg2_ref[...] = jnp.zeros_like(dg2_ref)

        dg_ref[...] += dg
        dg2_ref[...] += dg2

    row = pl.BlockSpec((tr, d), lambda i: (i, 0))
    vec = pl.BlockSpec((1, d), lambda i: (0, 0))
    return pl.pallas_call(
        body, grid=(n // tr,), in_specs=[row, row, vec, row, row, vec], out_specs=[row, row, vec, vec],
        out_shape=[jax.ShapeDtypeStruct((n, d), F32), jax.ShapeDtypeStruct((n, d), BF16),
                   jax.ShapeDtypeStruct((1, d), F32), jax.ShapeDtypeStruct((1, d), F32)],
        compiler_params=_params(("arbitrary",), VMEM_BIG), name=name,
    )(dy, x, g, dres, x2, g2)


def _loss_head(h, target, *, tr=256):
    n, d = h.shape

    def body(h_ref, t_ref, dh_ref, s_ref):
        err = h_ref[...] - t_ref[...]
        dh_ref[...] = err * (1.0 / d)

        @pl.when(pl.program_id(0) == 0)
        def _():
            s_ref[...] = jnp.zeros_like(s_ref)

        s_ref[...] += jnp.sum(err * err)

    row = pl.BlockSpec((tr, d), lambda i: (i, 0))
    acc = pl.BlockSpec((8, 128), lambda i: (0, 0))
    return pl.pallas_call(
        body, grid=(n // tr,), in_specs=[row, row], out_specs=[row, acc],
        out_shape=[jax.ShapeDtypeStruct((n, d), F32), jax.ShapeDtypeStruct((8, 128), F32)],
        compiler_params=_params(("arbitrary",)), name="loss_head",
    )(h, target)


def _rope_tables():
    pos = jnp.arange(T, dtype=F32)
    inv = ROPE_THETA ** (-jnp.arange(0, 16, 2, dtype=F32) / 16)
    ang = pos[:, None] * inv[None, :]
    cos, sin = jnp.cos(ang), jnp.sin(ang)
    one = jnp.ones((T, HD - 16), F32)
    zero8 = jnp.zeros((T, 8), F32)
    zero = jnp.zeros((T, HD - 16), F32)
    c = jnp.concatenate([cos, cos, one], axis=1)
    s1 = jnp.concatenate([zero8, sin, zero], axis=1)
    s2 = jnp.concatenate([-sin, zero8, zero], axis=1)
    c, s1, s2 = (jnp.concatenate([t, t], axis=1) for t in (c, s1, s2))
    scale = HD ** -0.5
    return (jnp.stack([c * scale, c, jnp.ones_like(c)]), jnp.stack([s1 * scale, s1, jnp.zeros_like(c)]),
            jnp.stack([s2 * scale, s2, jnp.zeros_like(c)]))


def _row_chunks(r):
    if r == 1:
        n = 4
        return [(slice(i * (T // n), (i + 1) * (T // n)),) * 2 for i in range(n)]
    per = T // r
    return [(pl.ds(j, per, stride=r), slice(j * per, (j + 1) * per)) for j in range(r)]


def _rope_fwd(qkv, tabs):
    def body(x_ref, c_ref, s1_ref, s2_ref, o_ref):
        g = lax.rem(lax.div(pl.program_id(0), 2), 3)
        for gi, r in enumerate(DIL):
            @pl.when(g == gi)
            def _(r=r):
                for tok, prm in _row_chunks(r):
                    x = x_ref[tok, :]
                    y = x * c_ref[tok, :] + pltpu.roll(x, 8, 1) * s1_ref[tok, :] + pltpu.roll(x, 120, 1) * s2_ref[tok, :]
                    o_ref[prm, :] = y.astype(BF16)

    tab = pl.BlockSpec((None, T, 128), lambda b: (lax.div(b, 6), 0, 0))
    return pl.pallas_call(
        body, grid=(18,), in_specs=[pl.BlockSpec((T, 128), lambda b: (0, b)), tab, tab, tab],
        out_specs=pl.BlockSpec((None, T, 128), lambda b: (b, 0, 0)), out_shape=jax.ShapeDtypeStruct((18, T, 128), BF16),
        compiler_params=_params(("parallel",)), name="rope_fwd",
    )(qkv, *tabs)


def _rope_bwd(d, which, tabs, out_buf):
    def body(d_ref, c_ref, s1_ref, s2_ref, *rest):
        o_ref, tok_ref = rest[-2], rest[-1]
        g = lax.div(pl.program_id(0), 2)
        for gi, r in enumerate(DIL):
            @pl.when(g == gi)
            def _(r=r):
                for tok, prm in _row_chunks(r):
                    tok_ref[tok, :] = d_ref[prm, :]
                for rows, _ in _row_chunks(1):
                    gx = tok_ref[rows, :]
                    y = gx * c_ref[rows, :] + pltpu.roll(gx * s1_ref[rows, :], 120, 1) + pltpu.roll(gx * s2_ref[rows, :], 8, 1)
                    o_ref[rows, :] = y.astype(BF16)

    tab = pl.BlockSpec((None, T, 128), lambda b: (which, 0, 0))
    ins = [d, *tabs] + ([] if out_buf is None else [out_buf])
    specs = [pl.BlockSpec((None, None, T, 128), lambda b: (lax.div(b, 2), lax.rem(b, 2), 0, 0)), tab, tab, tab]
    return pl.pallas_call(
        body, grid=(6,), in_specs=specs + ([] if out_buf is None else [ANY]),
        out_specs=pl.BlockSpec((T, 128), lambda b: (0, 6 * which + b)),
        out_shape=jax.ShapeDtypeStruct((T, 3 * A_W), BF16), scratch_shapes=[pltpu.VMEM((T, 128), F32)],
        input_output_aliases={} if out_buf is None else {4: 0},
        compiler_params=_params(("arbitrary",)), name="rope_bwd",
    )(*ins)


def _head_mask(x, lane_lo):
    lane = lax.broadcasted_iota(jnp.int32, x.shape, 1)
    keep = (lane < HD) if lane_lo else (lane >= HD)
    return jnp.where(keep, x.astype(F32), 0.0).astype(BF16)


def _band_scalars(g):
    b = pl.program_id(0)
    nbs = (T // BLK) // DIL[g]
    has_prev = jnp.where((b & (nbs - 1)) != 0, 1, 0)
    next_ok = jnp.where(((b + 1) & (nbs - 1)) != 0, 1, 0)
    return has_prev, next_ok


def _band_mask_q(has_prev):
    row = lax.broadcasted_iota(jnp.int32, (BLK, 2 * BLK), 0)
    col = lax.broadcasted_iota(jnp.int32, (BLK, 2 * BLK), 1)
    return ((col < BLK) & (col >= row) & (has_prev == 1)) | ((col >= BLK) & (col - BLK <= row))


def _band_mask_k(next_ok):
    row = lax.broadcasted_iota(jnp.int32, (2 * BLK, BLK), 0)
    col = lax.broadcasted_iota(jnp.int32, (2 * BLK, BLK), 1)
    return ((row < BLK) & (col <= row)) | ((row >= BLK) & (col >= row - BLK) & (next_ok == 1))


def _band_spec(step, which=None):
    nb = T // BLK
    at = {"cur": lambda b: b, "prev": lambda b: jnp.maximum(b - 1, 0), "next": lambda b: jnp.minimum(b + 1, nb - 1)}[step]
    if which is None:
        return pl.BlockSpec((3, 2, BLK, 128), lambda b: (0, 0, at(b), 0))
    return pl.BlockSpec((None, 3, 2, BLK, 128), lambda b: (which, 0, 0, at(b), 0))


def _band_fwd(qkv):
    nb = T // BLK

    def body(q_ref, kc_ref, kp_ref, vc_ref, vp_ref, o_ref, l_ref):
        lane = lax.broadcasted_iota(jnp.int32, (BLK, 128), 1)
        for g in range(3):
            has_prev, _ = _band_scalars(g)
            mask = _band_mask_q(has_prev)
            for p in range(2):
                qp = q_ref[g, p]
                kcat = jnp.concatenate([kp_ref[g, p], kc_ref[g, p]], axis=0)
                vcat = jnp.concatenate([vp_ref[g, p], vc_ref[g, p]], axis=0)
                o_acc = jnp.zeros((BLK, 128), F32)
                lse = jnp.zeros((BLK, 128), F32)
                for e in range(2):
                    s = _dot(_head_mask(qp, e == 0), kcat, NT)
                    s = jnp.where(mask, s, NEG)
                    m = jnp.max(s, axis=-1, keepdims=True)
                    pr = jnp.exp(s - m)
                    l = jnp.sum(pr, axis=-1, keepdims=True)
                    o_acc = o_acc + _dot(pr.astype(BF16), _head_mask(vcat, e == 0), NN) / l
                    lse = jnp.where((lane < HD) if e == 0 else (lane >= HD), m + jnp.log(l), lse)
                o_ref[g, p] = o_acc
                l_ref[g, p] = lse

    out = _band_spec("cur")
    shp = jax.ShapeDtypeStruct((3, 2, T, 128), F32)
    return pl.pallas_call(
        body, grid=(nb,),
        in_specs=[_band_spec("cur", 0), _band_spec("cur", 1), _band_spec("prev", 1), _band_spec("cur", 2), _band_spec("prev", 2)],
        out_specs=[out, out], out_shape=[shp, shp],
        compiler_params=_params(("parallel",)), name="band_fwd",
    )(qkv, qkv, qkv, qkv, qkv)


def _band_bwd(qkv, do, lse, dlt):
    nb = T // BLK

    def body(qc_ref, qn_ref, kc_ref, kp_ref, vc_ref, vp_ref, doc_ref, don_ref, lc_ref, ln_ref, dc_ref, dn_ref,
             dq_ref, dk_ref, dv_ref):
        for g in range(3):
            has_prev, next_ok = _band_scalars(g)
            mask_q = _band_mask_q(has_prev)
            mask_k = _band_mask_k(next_ok)
            for p in range(2):
                qc, qn = qc_ref[g, p], qn_ref[g, p]
                doc, don = doc_ref[g, p], don_ref[g, p]
                kc, vc = kc_ref[g, p], vc_ref[g, p]
                kcat = jnp.concatenate([kp_ref[g, p], kc], axis=0)
                vcat = jnp.concatenate([vp_ref[g, p], vc], axis=0)
                qcat = jnp.concatenate([qc, qn], axis=0)
                docat = jnp.concatenate([doc, don], axis=0)
                dq = jnp.zeros((BLK, 128), F32)
                dk = jnp.zeros((BLK, 128), F32)
                dv = jnp.zeros((BLK, 128), F32)
                for e in range(2):
                    lo = e == 0
                    col = slice(HD * e, HD * e + 1)
                    lse_c, lse_n = lc_ref[g, p, :, col], ln_ref[g, p, :, col]
                    dl_c, dl_n = dc_ref[g, p, :, col], dn_ref[g, p, :, col]
                    s = jnp.where(mask_q, _dot(_head_mask(qc, lo), kcat, NT), NEG)
                    pr = jnp.exp(s - lse_c)
                    dp = _dot(_head_mask(doc, lo), vcat, NT)
                    ds = pr * (dp - dl_c)
                    dq = dq + _dot(ds.astype(BF16), _head_mask(kcat, lo), NN)
                    qm, dom = _head_mask(qcat, lo), _head_mask(docat, lo)
                    s2 = jnp.where(mask_k, _dot(qm, kc, NT), NEG)
                    p2 = jnp.exp(s2 - jnp.concatenate([lse_c, lse_n], axis=0))
                    dv = dv + _dot(p2.astype(BF16), dom, TN)
                    dp2 = _dot(dom, vc, NT)
                    ds2 = p2 * (dp2 - jnp.concatenate([dl_c, dl_n], axis=0))
                    dk = dk + _dot(ds2.astype(BF16), qm, TN)
                dq_ref[g, p] = dq
                dk_ref[g, p] = dk
                dv_ref[g, p] = dv

    cur, nxt = _band_spec("cur"), _band_spec("next")
    shp = jax.ShapeDtypeStruct((3, 2, T, 128), F32)
    return pl.pallas_call(
        body, grid=(nb,),
        in_specs=[_band_spec("cur", 0), _band_spec("next", 0), _band_spec("cur", 1), _band_spec("prev", 1),
                  _band_spec("cur", 2), _band_spec("prev", 2), cur, nxt, cur, nxt, cur, nxt],
        out_specs=[cur, cur, cur], out_shape=[shp, shp, shp],
        compiler_params=_params(("parallel",)), name="band_bwd",
    )(qkv, qkv, qkv, qkv, qkv, qkv, do, do, lse, lse, dlt, dlt)


def _split3(x):
    hi = x.astype(BF16)
    r = x - hi.astype(F32)
    mid = r.astype(BF16)
    lo = (r - mid.astype(F32)).astype(BF16)
    return hi, mid, lo


def _dot3(x, m, dims=NN):
    hi, mid, lo = _split3(x)
    return _dot(hi, m, dims) + _dot(mid, m, dims) + _dot(lo, m, dims)


def _combine_weights(lses):
    l0, l1, l2 = lses
    m = jnp.maximum(jnp.maximum(l0, l1), l2)
    e = [jnp.exp(l0 - m), jnp.exp(l1 - m), jnp.exp(l2 - m)]
    inv = 1.0 / (e[0] + e[1] + e[2])
    return [ei * inv for ei in e]


CR = 256


def _combine_fwd(o, lse):
    def body(o_ref, l_ref, att_ref, o3_ref, l3_ref):
        for g, r in enumerate(DIL):
            for p in range(2):
                for tok, prm in _row_chunks(r):
                    o3_ref[g, p, tok, :] = o_ref[g, p, prm, :]
                    l3_ref[g, p, tok, :] = l_ref[g, p, prm, :]
        for i in range(T // CR):
            rows = slice(i * CR, (i + 1) * CR)
            for p in range(2):
                alpha = _combine_weights([l3_ref[g, p, rows, :] for g in range(3)])
                for g in range(3):
                    att_ref[rows, g * GW + p * 128: g * GW + (p + 1) * 128] = (o3_ref[g, p, rows, :] * alpha[g]).astype(BF16)

    shp = jax.ShapeDtypeStruct((3, 2, T, 128), F32)
    return pl.pallas_call(
        body, out_shape=[jax.ShapeDtypeStruct((T, A_W), BF16), shp, shp],
        compiler_params=_params(vmem=VMEM_BIG), name="combine_fwd",
    )(o, lse)


def _combine_bwd(datt, o3, l3, headsum):
    def body(d_ref, o_ref, l_ref, hs_ref, do_ref, dl_ref, tdo_ref, tdl_ref):
        hs = hs_ref[...]
        for p in range(2):
            for i in range(T // CR):
                rows = slice(i * CR, (i + 1) * CR)
                alpha = _combine_weights([l_ref[g, p, rows, :] for g in range(3)])
                total = jnp.zeros((CR, 128), F32)
                for g in range(3):
                    dg = d_ref[rows, g * GW + p * 128: g * GW + (p + 1) * 128]
                    tdo_ref[g, rows, :] = dg * alpha[g]
                    total = total + alpha[g] * _dot3(dg * o_ref[g, p, rows, :], hs)
                for g in range(3):
                    tdl_ref[g, rows, :] = alpha[g] * total
            for g, r in enumerate(DIL):
                for tok, prm in _row_chunks(r):
                    do_ref[g, p, prm, :] = tdo_ref[g, tok, :].astype(BF16)
                    dl_ref[g, p, prm, :] = tdl_ref[g, tok, :]

    return pl.pallas_call(
        body, out_shape=[jax.ShapeDtypeStruct((3, 2, T, 128), BF16), jax.ShapeDtypeStruct((3, 2, T, 128), F32)],
        scratch_shapes=[pltpu.VMEM((3, T, 128), F32), pltpu.VMEM((3, T, 128), F32)],
        compiler_params=_params(vmem=VMEM_BIG), name="combine_bwd",
    )(datt, o3, l3, headsum)


def _fox_scores(qm, k_ref, ck_ref, e, i, n):
    s = _dot(qm, k_ref[0:n, :], NT) - ck_ref[0, e:e + 1, 0:n]
    row = lax.broadcasted_iota(jnp.int32, (FQ, FQ), 0)
    col = lax.broadcasted_iota(jnp.int32, (FQ, FQ), 1)
    diag = jnp.where(col <= row, s[:, n - FQ:], NEG)
    m = jnp.max(diag, axis=-1, keepdims=True)
    if i == 0:
        pr = jnp.exp(diag - m)
        return pr, jnp.sum(pr, axis=-1, keepdims=True)
    past = s[:, :n - FQ]
    m = jnp.maximum(m, jnp.max(past, axis=-1, keepdims=True))
    p_past, p_diag = jnp.exp(past - m), jnp.exp(diag - m)
    l = jnp.sum(p_past, axis=-1, keepdims=True) + jnp.sum(p_diag, axis=-1, keepdims=True)
    return jnp.concatenate([p_past, p_diag], axis=1), l


def _fox_fwd(q, kv, c_row):
    def body(q_ref, k_ref, v_ref, cr_ref, o_ref, vm_ref):
        for e in range(2):
            vm_ref[e] = _head_mask(v_ref[...], e == 0)
        for i in range(T // FQ):
            n = (i + 1) * FQ
            rows = slice(i * FQ, n)
            acc = jnp.zeros((FQ, 128), F32)
            for e in range(2):
                qm = _head_mask(q_ref[rows, :], e == 0)
                pr, l = _fox_scores(qm, k_ref, cr_ref, e, i, n)
                acc = acc + _dot(pr.astype(BF16), vm_ref[e, 0:n, :], NN) / l
            o_ref[rows, :] = acc.astype(BF16)

    pair = pl.BlockSpec((T, 128), lambda p: (0, p))
    return pl.pallas_call(
        body, grid=(D // 128,),
        in_specs=[pair, pair, pl.BlockSpec((T, 128), lambda p: (0, D // 128 + p)), pl.BlockSpec((1, 2, T), lambda p: (p, 0, 0))],
        out_specs=pair, out_shape=jax.ShapeDtypeStruct((T, D), BF16),
        scratch_shapes=[pltpu.VMEM((2, T, 128), BF16)],
        compiler_params=_params(("parallel",), VMEM_BIG), name="fox_fwd",
    )(q, kv, kv, c_row)


def _fox_bwd(q, kv, do, c_row, init):
    def body(q_ref, k_ref, v_ref, do_ref, cr_ref, *rest):
        dq_ref, dk_ref, dv_ref, dck_ref, km_ref = rest[-5:]
        for o_ref, i_ref in zip((dk_ref, dv_ref, dck_ref), rest[:-5] or (None,) * 3):
            o_ref[...] = jnp.zeros_like(o_ref) if i_ref is None else i_ref[...]
        for e in range(2):
            km_ref[e] = _head_mask(k_ref[...], e == 0)
        for i in range(T // FQ):
            n = (i + 1) * FQ
            rows = slice(i * FQ, n)
            dq = jnp.zeros((FQ, 128), F32)
            dk = jnp.zeros((n, 128), F32)
            dv = jnp.zeros((n, 128), F32)
            for e in range(2):
                qm = _head_mask(q_ref[rows, :], e == 0)
                dom = _head_mask(do_ref[rows, :], e == 0)
                pr, l = _fox_scores(qm, k_ref, cr_ref, e, i, n)
                pr = pr * (1.0 / l)
                dp = _dot(dom, v_ref[0:n, :], NT)
                ds = pr * (dp - jnp.sum(pr * dp, axis=-1, keepdims=True))
                dsb = ds.astype(BF16)
                dq = dq + _dot(dsb, km_ref[e, 0:n, :], NN)
                dk = dk + _dot(dsb, qm, TN)
                dv = dv + _dot(pr.astype(BF16), dom, TN)
                dck_ref[0, e:e + 1, 0:n] += jnp.sum(ds, axis=0, keepdims=True)
            dk_ref[0:n, :] += dk
            dv_ref[0:n, :] += dv
            dq_ref[rows, :] = (dq * HD ** -0.5).astype(BF16)

    pair = pl.BlockSpec((T, 128), lambda p: (0, p))
    ck = pl.BlockSpec((1, 8, T), lambda p: (p, 0, 0))
    return pl.pallas_call(
        body, grid=(D // 128,),
        in_specs=[pair, pair, pl.BlockSpec((T, 128), lambda p: (0, D // 128 + p)), pair,
                  pl.BlockSpec((1, 2, T), lambda p: (p, 0, 0))] + ([] if init is None else [pair, pair, ck]),
        out_specs=[pair, pair, pair, ck],
        out_shape=[jax.ShapeDtypeStruct((T, D), BF16), jax.ShapeDtypeStruct((T, D), F32), jax.ShapeDtypeStruct((T, D), F32),
                   jax.ShapeDtypeStruct((D // 128, 8, T), F32)],
        scratch_shapes=[pltpu.VMEM((2, T, 128), BF16)],
        compiler_params=_params(("parallel",), VMEM_BIG), name="fox_bwd",
    )(q, kv, kv, do, c_row, *(init or ()))


def _tri(lower):
    r = lax.broadcasted_iota(jnp.int32, (BLK, BLK), 0)
    c = lax.broadcasted_iota(jnp.int32, (BLK, BLK), 1)
    return jnp.where((c <= r) if lower else (c >= r), 1.0, 0.0).astype(BF16)


def _gates_fwd(z, b):
    def body(z_ref, b_ref, c_ref):
        tri = _tri(True)
        carry = jnp.zeros((1, 128), F32)
        for i in range(T // BLK):
            rows = slice(i * BLK, (i + 1) * BLK)
            x = z_ref[rows, :] + b_ref[...]
            logf = jnp.minimum(x, 0.0) - jnp.log(1.0 + jnp.exp(-jnp.abs(x)))
            hi, mid, lo = _split3(logf)
            y = _dot(tri, hi, NN) + _dot(tri, mid, NN) + _dot(tri, lo, NN) + carry
            c_ref[rows, :] = y
            carry = y[BLK - 1:BLK, :]

    return pl.pallas_call(body, out_shape=jax.ShapeDtypeStruct((T, 128), F32), name="gates_fwd")(z, b)


def _gates_bwd(dc, z, b):
    def body(dc_ref, z_ref, b_ref, dz_ref, db_ref):
        tri = _tri(False)
        carry = jnp.zeros((1, 128), F32)
        db = jnp.zeros((1, 128), F32)
        for i in reversed(range(T // BLK)):
            rows = slice(i * BLK, (i + 1) * BLK)
            hi, mid, lo = _split3(dc_ref[rows, :])
            dlogf = _dot(tri, hi, NN) + _dot(tri, mid, NN) + _dot(tri, lo, NN) + carry
            carry = dlogf[0:1, :]
            x = z_ref[rows, :] + b_ref[...]
            dz = dlogf / (1.0 + jnp.exp(x))
            dz_ref[rows, :] = dz.astype(BF16)
            db = db + jnp.sum(dz, axis=0, keepdims=True)
        db_ref[...] = db

    return pl.pallas_call(
        body, out_shape=[jax.ShapeDtypeStruct((T, 128), BF16), jax.ShapeDtypeStruct((1, 128), F32)], name="gates_bwd",
    )(dc, z, b)


def _conv_pair(a_refs, cw_refs, cb_refs):
    row = lax.broadcasted_iota(jnp.int32, (T, CT), 0)
    outs = []
    for a_ref, cw_ref, cb_ref in zip(a_refs, cw_refs, cb_refs):
        z = a_ref[...]
        z1 = jnp.where(row >= 1, pltpu.roll(z, 1, 0), 0.0)
        z2 = jnp.where(row >= 2, pltpu.roll(z, 2, 0), 0.0)
        y = cw_ref[2:3, :] * z + cw_ref[1:2, :] * z1 + cw_ref[0:1, :] * z2 + cb_ref[...]
        outs.append((y, z, z1, z2))
    return outs


_GELU_K = math.sqrt(2.0 / math.pi)
N_CT = D_FF // CT


def _conv_specs():
    def at(rows, off):
        return pl.BlockSpec((rows, CT), lambda j: (0, j + off))
    return [at(T, 0), at(T, N_CT), at(3, 0), at(3, N_CT), at(1, 0), at(1, N_CT)]


def _convgate_fwd(a, cw, cb):
    def body(ag_ref, av_ref, wg_ref, wv_ref, bg_ref, bv_ref, u_ref):
        (g, _, _, _), (v, _, _, _) = _conv_pair((ag_ref, av_ref), (wg_ref, wv_ref), (bg_ref, bv_ref))
        th = jnp.tanh(_GELU_K * (g + 0.044715 * g * g * g))
        u_ref[...] = (0.5 * g * (1.0 + th) * v).astype(BF16)

    return pl.pallas_call(
        body, grid=(N_CT,), in_specs=_conv_specs(),
        out_specs=pl.BlockSpec((T, CT), lambda j: (0, j)), out_shape=jax.ShapeDtypeStruct((T, D_FF), BF16),
        compiler_params=_params(("parallel",), VMEM_BIG), name="convgate_fwd",
    )(a, a, cw, cw, cb, cb)


def _convgate_bwd(a, du, cw, cb):
    def body(ag_ref, av_ref, wg_ref, wv_ref, bg_ref, bv_ref, du_ref, da_ref, dcw_ref, dcb_ref):
        (g, gz, gz1, gz2), (v, vz, vz1, vz2) = _conv_pair((ag_ref, av_ref), (wg_ref, wv_ref), (bg_ref, bv_ref))
        du = du_ref[...].astype(F32)
        th = jnp.tanh(_GELU_K * (g + 0.044715 * g * g * g))
        gelu = 0.5 * g * (1.0 + th)
        dgelu = 0.5 * (1.0 + th) + 0.5 * g * (1.0 - th * th) * _GELU_K * (1.0 + 3 * 0.044715 * g * g)
        row = lax.broadcasted_iota(jnp.int32, (T, CT), 0)
        for h, (d, z, z1, z2, w_ref) in enumerate(((du * v * dgelu, gz, gz1, gz2, wg_ref), (du * gelu, vz, vz1, vz2, wv_ref))):
            d1 = jnp.where(row < T - 1, pltpu.roll(d, T - 1, 0), 0.0)
            d2 = jnp.where(row < T - 2, pltpu.roll(d, T - 2, 0), 0.0)
            da_ref[h] = (w_ref[2:3, :] * d + w_ref[1:2, :] * d1 + w_ref[0:1, :] * d2).astype(BF16)
            dcw_ref[h, 0:1, :] = jnp.sum(d * z2, axis=0, keepdims=True)
            dcw_ref[h, 1:2, :] = jnp.sum(d * z1, axis=0, keepdims=True)
            dcw_ref[h, 2:3, :] = jnp.sum(d * z, axis=0, keepdims=True)
            dcb_ref[h] = jnp.sum(d, axis=0, keepdims=True)

    def both(rows):
        return pl.BlockSpec((2, rows, CT), lambda j: (0, 0, j))

    return pl.pallas_call(
        body, grid=(N_CT,),
        in_specs=_conv_specs() + [pl.BlockSpec((T, CT), lambda j: (0, j))],
        out_specs=[both(T), both(3), both(1)],
        out_shape=[jax.ShapeDtypeStruct((2, T, D_FF), BF16), jax.ShapeDtypeStruct((2, 3, D_FF), F32),
                   jax.ShapeDtypeStruct((2, 1, D_FF), F32)],
        compiler_params=_params(("parallel",), VMEM_BIG), name="convgate_bwd",
    )(a, a, cw, cw, cb, cb, du)


def _halves_a(tm, tn, tk):
    per = D_FF // tk
    return lambda i, j, k: (lax.div(k, per), i, lax.rem(k, per))


def _halves_b(tm, tn, tk):
    per = D_FF // tn
    return lambda i, j, k: (lax.div(j, per), k, lax.rem(j, per))


def _adamw(w, m, v, g, *, name):
    r, c = w.shape
    tr = r
    if r * c > 256 * 1024:
        for cand in range(8, r, 8):
            if r % cand == 0 and cand * c <= 256 * 1024:
                tr = cand

    def body(w_ref, m_ref, v_ref, g_ref, d_ref, nm_ref, nv_ref):
        gv = g_ref[...]
        mn = ADAM_B1 * m_ref[...] + (1.0 - ADAM_B1) * gv
        vn = ADAM_B2 * v_ref[...] + (1.0 - ADAM_B2) * (gv * gv)
        m_hat = mn * (1.0 / (1.0 - ADAM_B1 ** ADAM_STEP))
        v_hat = vn * (1.0 / (1.0 - ADAM_B2 ** ADAM_STEP))
        d_ref[...] = -ADAM_LR * (m_hat / (jnp.sqrt(v_hat) + ADAM_EPS) + ADAM_WD * w_ref[...])
        nm_ref[...] = mn
        nv_ref[...] = vn

    blk = pl.BlockSpec((tr, c), lambda i: (i, 0))
    shp = jax.ShapeDtypeStruct((r, c), F32)
    return pl.pallas_call(
        body, grid=(r // tr,), in_specs=[blk] * 4, out_specs=[blk] * 3, out_shape=[shp] * 3,
        compiler_params=_params(("parallel",)), name=name,
    )(w, m, v, g)


def _place():
    x, y, c = lax.axis_index("x"), lax.axis_index("y"), lax.axis_index("c")
    chips = [(1 - x, y), (x, 1 - y), (1 - x, 1 - y)]
    return x, y, c, chips


def _window(ref, kind, s, half=None):
    lead = () if half is None else (half,)
    b, c = ref.shape[-2], ref.shape[-1]
    if kind == "col":
        return ref.at[lead + (slice(None), slice(None), pl.ds(s * (c // N_CHIPS), c // N_CHIPS))]
    if kind == "row":
        return ref.at[lead + (slice(None), pl.ds(s * (b // N_CHIPS), b // N_CHIPS), slice(None))]
    return ref.at[lead + (s,)]


def _allgather(tensors, kinds, *, name):
    n = len(tensors)

    def body(*refs):
        bufs = refs[n:2 * n]
        send, recv = refs[2 * n:]
        x, y, c, chips = _place()
        me = 2 * x + y
        sib = (x, y, 1 - c)

        def rcopy(i, k, win, to):
            return pltpu.make_async_remote_copy(src_ref=win, dst_ref=win, send_sem=send.at[i * 6 + k], recv_sem=recv.at[i * 6 + k],
                                                device_id=to, device_id_type=MESH)

        started = []
        for i in range(n):
            for k, (px, py) in enumerate(chips):
                cp = rcopy(i, k, _window(bufs[i], kinds[i], me, c), (px, py, c))
                cp.start()
                started.append(cp)
        for i in range(n):
            for k, (px, py) in enumerate(chips):
                landed = _window(bufs[i], kinds[i], 2 * px + py, c)
                rcopy(i, k, landed, (px, py, c)).wait_recv()
                fw = rcopy(i, 3 + k, landed, sib)
                fw.start()
                started.append(fw)
        for i in range(n):
            for k, (px, py) in enumerate(chips):
                rcopy(i, 3 + k, _window(bufs[i], kinds[i], 2 * px + py, 1 - c), sib).wait_recv()
        for cp in started:
            cp.wait_send()

    return pl.pallas_call(
        body, in_specs=[ANY] * n, out_specs=[ANY] * n,
        out_shape=[jax.ShapeDtypeStruct(t.shape, t.dtype) for t in tensors],
        scratch_shapes=[pltpu.SemaphoreType.DMA((6 * n,)), pltpu.SemaphoreType.DMA((6 * n,))],
        input_output_aliases={i: i for i in range(n)},
        name=name,
    )(*tensors)


def _rows_tile(rows, cols, sub):
    best = None
    for t in range(sub, rows + 1, sub):
        if rows % t == 0 and t * cols <= 512 * 1024:
            best = t
    return rows if best is None else best


def _sequencer(name, cid, n_sems, peers_of, body):
    @pl.kernel(mesh=plsc.ScalarSubcoreMesh(axis_name="seq", num_cores=1), name=name,
               scratch_types=(pltpu.SemaphoreType.DMA((n_sems,)), pltpu.SemaphoreType.DMA((n_sems,))),
               compiler_params=pltpu.CompilerParams(collective_id=cid))
    def launch(send, recv):
        x, y, c, chips = _place()
        peers = peers_of(x, y, c, chips)
        barrier = pltpu.get_barrier_semaphore()
        for peer in peers:
            pl.semaphore_signal(barrier, inc=1, device_id=peer, device_id_type=MESH)
        pl.semaphore_wait(barrier, len(peers))
        body(send, recv)

    launch()


def _half_of_full(ref, kind, h):
    if kind == "col":
        b = ref.shape[0]
        return ref.at[pl.ds(h * (b // 2), b // 2), :]
    if kind == "row":
        c = ref.shape[1]
        return ref.at[:, pl.ds(h * (c // 2), c // 2)]
    b = ref.shape[1]
    return ref.at[:, pl.ds(h * (b // 2), b // 2), :]


def _half_shape(full, kind):
    if kind == "col":
        return (full[0] // 2, full[1])
    if kind == "row":
        return (full[0], full[1] // 2)
    return (full[0], full[1] // 2, full[2])


def _win_of_half(ref, kind, s):
    if kind == "col":
        c = ref.shape[1]
        return ref.at[:, pl.ds(s * (c // N_CHIPS), c // N_CHIPS)]
    if kind == "row":
        b = ref.shape[0]
        return ref.at[pl.ds(s * (b // N_CHIPS), b // N_CHIPS), :]
    return ref.at[s]


def _win_shape(half, kind):
    if kind == "col":
        return (half[0], half[1] // N_CHIPS)
    if kind == "row":
        return (half[0] // N_CHIPS, half[1])
    return half[1:]


def _seq_swap(parts, kinds, *, name):
    n = len(parts)
    srcs = [jax.new_ref(p, memory_space=pltpu.MemorySpace.HBM) for p in parts]
    outs = [jax.empty_ref(jax.ShapeDtypeStruct(_half_shape(p.shape, k), p.dtype), memory_space=pltpu.MemorySpace.HBM)
            for p, k in zip(parts, kinds)]

    def body(send, recv):
        x, y, c, _ = _place()
        cps = []
        for i in range(n):
            cp = pltpu.make_async_remote_copy(src_ref=_half_of_full(srcs[i], kinds[i], 1 - c), dst_ref=outs[i], send_sem=send.at[i],
                                              recv_sem=recv.at[i], device_id=(x, y, 1 - c), device_id_type=MESH)
            cp.start()
            cps.append(cp)
        for cp in cps:
            cp.wait()

    _sequencer(name, 2, n, lambda x, y, c, chips: [(x, y, 1 - c)], body)
    return [o[...] for o in outs]


def _seq_scatter(halves, kinds, *, name):
    n = len(halves)
    srcs = [jax.new_ref(h, memory_space=pltpu.MemorySpace.HBM) for h in halves]
    outs = [jax.empty_ref(jax.ShapeDtypeStruct((3,) + _win_shape(h.shape, k), h.dtype), memory_space=pltpu.MemorySpace.HBM)
            for h, k in zip(halves, kinds)]

    def body(send, recv):
        x, y, c, chips = _place()
        cps = []
        for i in range(n):
            for k, (px, py) in enumerate(chips):
                cp = pltpu.make_async_remote_copy(src_ref=_win_of_half(srcs[i], kinds[i], 2 * px + py), dst_ref=outs[i].at[k],
                                                  send_sem=send.at[3 * i + k], recv_sem=recv.at[3 * i + k],
                                                  device_id=(px, py, c), device_id_type=MESH)
                cp.start()
                cps.append(cp)
        for cp in cps:
            cp.wait()

    _sequencer(name, 3, 3 * n, lambda x, y, c, chips: [(px, py, c) for px, py in chips], body)
    return [o[...] for o in outs]


def _add_half(g, p, kind, where, after, *, name):
    if kind == "slab":
        s, b2, c = p.shape
        tr = _rows_tile(b2, c, 16)
        nr = b2 // tr
        grid = (s, nr)
        g_spec = pl.BlockSpec((None, tr, c), lambda i, r, w: (i, w[1] * nr + r, 0))
        p_spec = pl.BlockSpec((None, tr, c), lambda i, r, w: (i, r, 0))
    elif kind == "col":
        b2, c = p.shape
        tr = _rows_tile(b2, c, 16)
        nr = b2 // tr
        grid = (1, nr)
        g_spec = pl.BlockSpec((tr, c), lambda i, r, w: (w[1] * nr + r, 0))
        p_spec = pl.BlockSpec((tr, c), lambda i, r, w: (r, 0))
    else:
        b, c2 = p.shape
        tr = _rows_tile(b, c2, 16)
        grid = (1, b // tr)
        g_spec = pl.BlockSpec((tr, c2), lambda i, r, w: (r, w[1]))
        p_spec = pl.BlockSpec((tr, c2), lambda i, r, w: (r, 0))

    def body(w_ref, g_ref, p_ref, *rest):
        o_ref = rest[-1]
        o_ref[...] = (g_ref[...].astype(F32) + p_ref[...].astype(F32)).astype(o_ref.dtype)

    extra = [] if after is None else [after]
    return pl.pallas_call(
        body,
        grid_spec=pltpu.PrefetchScalarGridSpec(num_scalar_prefetch=1, grid=grid, in_specs=[g_spec, p_spec] + [ANY] * len(extra),
                                               out_specs=p_spec),
        out_shape=jax.ShapeDtypeStruct(p.shape, g.dtype),
        compiler_params=_params(("parallel", "parallel")), name=name,
    )(where, g, p, *extra)


def _sum_chips(r, h, kind, where, layer, layers, out_buf, after, *, name):
    _, br, cr = r.shape
    tr = _rows_tile(br, cr, 16)
    nr = br // tr
    if kind == "col":
        h_spec = pl.BlockSpec((tr, cr), lambda j, w: (j, w[0]))
        o_shape, o_spec = (layers, 2 * br, cr), pl.BlockSpec((None, tr, cr), lambda j, w: (layer, w[1] * nr + j, 0))
    elif kind == "row":
        h_spec = pl.BlockSpec((tr, cr), lambda j, w: (w[0] * nr + j, 0))
        o_shape, o_spec = (layers, br, 2 * cr), pl.BlockSpec((None, tr, cr), lambda j, w: (layer, j, w[1]))
    else:
        h_spec = pl.BlockSpec((None, tr, cr), lambda j, w: (w[0], j, 0))
        o_shape, o_spec = (layers, 2 * br, cr), pl.BlockSpec((None, tr, cr), lambda j, w: (layer, w[1] * nr + j, 0))

    def body(w_ref, h_ref, r0_ref, r1_ref, r2_ref, *rest):
        o_ref, t_ref = rest[-2], rest[-1]
        o_ref[...] = ((h_ref[...].astype(F32) + r0_ref[...].astype(F32)) + r1_ref[...].astype(F32)) + r2_ref[...].astype(F32)
        t_ref[...] = jnp.zeros_like(t_ref)

    def slot(k):
        return pl.BlockSpec((None, tr, cr), lambda j, w: (k, j, 0))

    ins, specs, alias = [h, r, r, r], [h_spec, slot(0), slot(1), slot(2)], {}
    if after is not None:
        ins.append(after)
        specs.append(ANY)
    if out_buf is not None:
        alias = {1 + len(ins): 0}
        ins.append(out_buf)
        specs.append(ANY)
    return pl.pallas_call(
        body,
        grid_spec=pltpu.PrefetchScalarGridSpec(num_scalar_prefetch=1, grid=(nr,), in_specs=specs,
                                               out_specs=[o_spec, pl.BlockSpec((8, 128), lambda j, w: (0, 0))]),
        out_shape=[jax.ShapeDtypeStruct(o_shape, F32), jax.ShapeDtypeStruct((8, 128), F32)], input_output_aliases=alias,
        compiler_params=_params(("arbitrary",)), name=name,
    )(where, *ins)


def _join_halves(tensors, kinds, *, name):
    n = len(tensors)

    def mine(ref, kind, h):
        if kind == "row":
            c = ref.shape[2]
            return ref.at[:, :, pl.ds(h * (c // 2), c // 2)]
        b = ref.shape[1]
        return ref.at[:, pl.ds(h * (b // 2), b // 2), :]

    def body(*refs):
        bufs = refs[n:2 * n]
        send, recv = refs[2 * n:]
        x, y, c, _ = _place()
        cps = []
        for i in range(n):
            part = mine(bufs[i], kinds[i], c)
            cp = pltpu.make_async_remote_copy(src_ref=part, dst_ref=part, send_sem=send.at[i],
                                              recv_sem=recv.at[i], device_id=(x, y, 1 - c), device_id_type=MESH)
            cp.start()
            cps.append(cp)
        for i in range(n):
            other = mine(bufs[i], kinds[i], 1 - c)
            pltpu.make_async_remote_copy(src_ref=other, dst_ref=other, send_sem=send.at[i],
                                         recv_sem=recv.at[i], device_id=(x, y, 1 - c), device_id_type=MESH).wait_recv()
        for cp in cps:
            cp.wait_send()

    return pl.pallas_call(
        body, in_specs=[ANY] * n, out_specs=[ANY] * n,
        out_shape=[jax.ShapeDtypeStruct(t.shape, t.dtype) for t in tensors],
        scratch_shapes=[pltpu.SemaphoreType.DMA((n,)), pltpu.SemaphoreType.DMA((n,))],
        input_output_aliases={i: i for i in range(n)},
        name=name,
    )(*tensors)


def _win(ref, kind, s, h=None):
    if kind == "col":
        b, c = ref.shape
        cols = pl.ds(s * (c // N_CHIPS), c // N_CHIPS)
        return ref.at[:, cols] if h is None else ref.at[pl.ds(h * (b // 2), b // 2), cols]
    if kind == "row":
        b, c = ref.shape
        rows = pl.ds(s * (b // N_CHIPS), b // N_CHIPS)
        return ref.at[rows, :] if h is None else ref.at[rows, pl.ds(h * (c // 2), c // 2)]
    b = ref.shape[1]
    return ref.at[s] if h is None else ref.at[s, pl.ds(h * (b // 2), b // 2)]


def _half(ref, kind, h):
    b, c = ref.shape
    if kind == "row":
        return ref.at[:, pl.ds(h * (c // 2), c // 2)]
    return ref.at[pl.ds(h * (b // 2), b // 2), :]


def _full_shape(shard_shape, kind):
    b, c = shard_shape
    return {"col": (b, N_CHIPS * c), "row": (N_CHIPS * b, c), "slab": (N_CHIPS, b, c)}[kind]


def _gather_body(srcs, outs, kinds, send, recv):
    x, y, c, chips = _place()
    me = 2 * x + y
    sib = (x, y, 1 - c)

    def rcopy(i, k, src, dst, to):
        return pltpu.make_async_remote_copy(src_ref=src, dst_ref=dst, send_sem=send.at[7 * i + k], recv_sem=recv.at[7 * i + k],
                                            device_id=to, device_id_type=MESH)

    started = []
    for i, (src, out, kind) in enumerate(zip(srcs, outs, kinds)):
        own = rcopy(i, 6, src, _win(out, kind, me), sib)
        own.start()
        started.append(own)
        for k, (px, py) in enumerate(chips):
            cp = rcopy(i, k, _half(src, kind, c), _win(out, kind, me, c), (px, py, c))
            cp.start()
            started.append(cp)
    for i, (out, kind) in enumerate(zip(outs, kinds)):
        for k, (px, py) in enumerate(chips):
            landed = _win(out, kind, 2 * px + py, c)
            rcopy(i, k, landed, landed, (px, py, c)).wait_recv()
            fw = rcopy(i, 3 + k, landed, landed, sib)
            fw.start()
            started.append(fw)
    for i, (src, out, kind) in enumerate(zip(srcs, outs, kinds)):
        for k, (px, py) in enumerate(chips):
            other = _win(out, kind, 2 * px + py, 1 - c)
            rcopy(i, 3 + k, other, other, sib).wait_recv()
        rcopy(i, 6, src, _win(out, kind, me), sib).wait_recv()
    for cp in started:
        cp.wait_send()


def _seq_gather(shards, kinds, *, name, cid):
    n = len(shards)
    srcs = [jax.new_ref(s, memory_space=pltpu.MemorySpace.HBM) for s in shards]
    outs = [jax.empty_ref(jax.ShapeDtypeStruct(_full_shape(s.shape, k), s.dtype), memory_space=pltpu.MemorySpace.HBM)
            for s, k in zip(shards, kinds)]

    @pl.kernel(mesh=plsc.ScalarSubcoreMesh(axis_name="seq", num_cores=1), name=name,
               scratch_types=(pltpu.SemaphoreType.DMA((7 * n,)), pltpu.SemaphoreType.DMA((7 * n,))),
               compiler_params=pltpu.CompilerParams(collective_id=cid))
    def launch(send, recv):
        x, y, c, chips = _place()
        barrier = pltpu.get_barrier_semaphore()
        for px, py in chips:
            pl.semaphore_signal(barrier, inc=1, device_id=(px, py, c), device_id_type=MESH)
        pl.semaphore_signal(barrier, inc=1, device_id=(x, y, 1 - c), device_id_type=MESH)
        pl.semaphore_wait(barrier, 4)
        _gather_body(srcs, outs, kinds, send, recv)

    launch()
    return [o[...] for o in outs]


KIND = dict(w_qkv_a="slab", w_o_a="col", w_q_b="row", w_o_b="row", w_kvf="slab", w_up="col", w_down="row", small="slab")
LAYERS = dict(w_qkv_a=N_A, w_o_a=N_A, w_q_b=DEPTH - N_A, w_o_b=DEPTH - N_A, w_kvf=1, w_up=DEPTH, w_down=DEPTH, small=1)
SMALL_W = 1792
SMALL_ROWS = 8


class _Reducer:
    def __init__(self, where):
        self.where = where
        self.acc = {nm: None for nm in KIND}
        self.pending = None

    def __call__(self, group, tag):
        names, layers, parts = zip(*group)
        kinds = [KIND[nm] for nm in names]
        summed = self._sum_pending(after=parts[-1])
        sib = _seq_swap(list(parts), kinds, name="reduce_swap_" + tag)
        halves = []
        for g, p, k, nm in zip(parts, sib, kinds, names):
            halves.append(_add_half(g, p, k, self.where, halves[-1] if halves else None, name="reduce_add_" + nm))
        landed = _seq_scatter(halves, kinds, name="reduce_scatter_" + tag)
        self.pending = (names, layers, landed, halves, kinds)
        return [halves[-1], summed]

    def flush(self, after):
        return self._sum_pending(after)

    def _sum_pending(self, after):
        if self.pending is None:
            return None
        for nm, l, r, h, k in zip(*self.pending):
            self.acc[nm], after = _sum_chips(r, h, k, self.where, l, LAYERS[nm], self.acc[nm], after, name="reduce_sum_" + nm)
        self.pending = None
        return after

    def finish(self):
        self._sum_pending(after=None)
        names = list(KIND)
        joined = _join_halves([self.acc[nm] for nm in names], [KIND[nm] for nm in names], name="reduce_pair_join")
        return dict(zip(names, joined))


def _headsum_matrix():
    r = lax.broadcasted_iota(jnp.int32, (128, 128), 0) // HD
    c = lax.broadcasted_iota(jnp.int32, (128, 128), 1) // HD
    return jnp.where(r == c, 1.0, 0.0).astype(BF16)


def kernel(x, norm_gains, w_qkv_a, w_o_a, w_q_b, w_o_b, kv_norm, w_kvf, b_f, w_up, conv_w, conv_b, w_down, loss_target, m_norm_gains, m_w_qkv_a, m_w_o_a, m_w_q_b, m_w_o_b, m_kv_norm, m_w_kvf, m_b_f, m_w_up, m_conv_w, m_conv_b, m_w_down, v_norm_gains, v_w_qkv_a, v_w_o_a, v_w_q_b, v_w_o_b, v_kv_norm, v_w_kvf, v_b_f, v_w_up, v_conv_w, v_conv_b, v_w_down):
    xi, yi, ci = lax.axis_index("x"), lax.axis_index("y"), lax.axis_index("c")
    chip = 2 * xi + yi
    where = jnp.stack([chip, ci]).astype(jnp.int32)
    ws = dict(norm_gains=norm_gains, w_qkv_a=w_qkv_a, w_o_a=w_o_a, w_q_b=w_q_b, w_o_b=w_o_b, kv_norm=kv_norm, w_kvf=w_kvf,
              b_f=b_f, w_up=w_up, conv_w=conv_w, conv_b=conv_b, w_down=w_down)
    ms = dict(norm_gains=m_norm_gains, w_qkv_a=m_w_qkv_a, w_o_a=m_w_o_a, w_q_b=m_w_q_b, w_o_b=m_w_o_b, kv_norm=m_kv_norm,
              w_kvf=m_w_kvf, b_f=m_b_f, w_up=m_w_up, conv_w=m_conv_w, conv_b=m_conv_b, w_down=m_w_down)
    vs = dict(norm_gains=v_norm_gains, w_qkv_a=v_w_qkv_a, w_o_a=v_w_o_a, w_q_b=v_w_q_b, w_o_b=v_w_o_b, kv_norm=v_kv_norm,
              w_kvf=v_w_kvf, b_f=v_b_f, w_up=v_w_up, conv_w=v_conv_w, conv_b=v_conv_b, w_down=v_w_down)

    small = jnp.concatenate([
        jnp.pad(norm_gains.reshape(16, 256), ((0, 0), (0, 1408 - 256))),
        jnp.pad(conv_w.reshape(12, 1408), ((0, 4), (0, 0)))], axis=0)
    big = [nm for nm in KIND if nm != "small"]
    half = {nm: ws[nm].astype(BF16) for nm in big}
    W = {nm: [None] * LAYERS[nm] for nm in big if nm != "w_kvf"}
    g_small = None
    groups = [("0a", [("w_qkv_a", 0), ("w_o_a", 0), ("small", 0)]), ("0b", [("w_up", 0)]), ("0c", [("w_down", 0)]),
              ("1a", [("w_qkv_a", 1), ("w_o_a", 1)]), ("1b", [("w_up", 1)]), ("1c", [("w_down", 1)]),
              ("2", [("w_kvf", 0), ("w_q_b", 0), ("w_o_b", 0), ("w_up", 2), ("w_down", 2)]),
              ("3", [("w_q_b", 1), ("w_o_b", 1), ("w_up", 3), ("w_down", 3)])]
    for tag, group in groups:
        shards = [small if nm == "small" else half[nm] if nm == "w_kvf" else half[nm][i] for nm, i in group]
        got = _seq_gather(shards, [KIND[nm] for nm, _ in group], name="gather_layer" + tag, cid=1)
        for (nm, i), g in zip(group, got):
            if nm == "small":
                g_small = g
            elif nm == "w_kvf":
                W[nm] = g.transpose(1, 0, 2).reshape(D, 2 * D + 16)
            else:
                W[nm][i] = g.transpose(1, 0, 2).reshape(D, 3 * A_W) if nm == "w_qkv_a" else g
    gains = g_small[:, :16, :256].transpose(1, 0, 2).reshape(DEPTH, 4, 1, D)
    cw_full = g_small[:, 16:28, :].transpose(1, 0, 2).reshape(DEPTH, 3, 2 * D_FF)
    cb_full = conv_b.reshape(DEPTH, 1, 2 * D_FF)

    reducer = _Reducer(where)
    sq, dh = _fwd_bwd(x[0], loss_target[0], W, gains, cw_full, cb_full, kv_norm, b_f, reducer)
    loss = lax.psum(sq[0, 0] * (0.5 / D), ("x", "y", "c"))
    return _update(loss, dh[None], reducer.finish(), chip, ws, ms, vs)


def _fwd_bwd(h, target, W, gains, cw_full, cb_full, kv_norm, b_f, reduce):
    w_kv = W["w_kvf"][:, :2 * D]
    w_kvf_pad = jnp.pad(W["w_kvf"], ((0, 0), (0, 128 - 16)))
    w_f = w_kvf_pad[:, 2 * D:]
    kvn_g = kv_norm.reshape(1, D)
    bf_pad = jnp.pad(b_f, (0, 128 - 16)).reshape(1, 128)
    tabs = _rope_tables()
    headsum = _headsum_matrix()

    saved = []
    kv = zf = c_row = kvn = h_kv = None
    xn = _rms_fwd(h, gains[0][0], out_dtype=BF16, name="rms_in")
    for l in range(DEPTH):
        s = {"h": h}
        g = gains[l]
        s["xn"] = xn
        if l < N_A:
            qkv = _matmul(xn, W["w_qkv_a"][l], mode="nn", out_dtype=F32, name="mm_qkv", mnk=(T, 3 * A_W, D), tn=768)
            qkvp = _rope_fwd(qkv, tabs).reshape(3, 3, 2, T, 128)
            o_p, lse_p = _band_fwd(qkvp)
            att, o3, lse3 = _combine_fwd(o_p, lse_p)
            s.update(qkvp=qkvp, o3=o3, lse3=lse3, lse_p=lse_p, att=att)
            mix = _matmul(att, W["w_o_a"][l], mode="nn", out_dtype=F32, name="mm_oa", mnk=(T, D, A_W))
        else:
            j = l - N_A
            if l == N_A:
                h_kv = h
                kvn = _rms_fwd(h, kvn_g, out_dtype=BF16, name="rms_in")
                kv = _matmul(kvn, w_kv, mode="nn", out_dtype=BF16, name="mm_kv")
                zf = _matmul(kvn, w_f, mode="nn", out_dtype=F32, name="mm_f")
                cum = _gates_fwd(zf, bf_pad)[:, :16]
                c_row = cum.T.reshape(8, 2, T)
            q = _matmul(xn, W["w_q_b"][j], mode="nn", out_dtype=BF16, name="mm_qb", mnk=(T, D, D), alpha=HD ** -0.5)
            o = _fox_fwd(q, kv, c_row)
            s.update(q=q, o=o)
            mix = _matmul(o, W["w_o_b"][j], mode="nn", out_dtype=F32, name="mm_ob", mnk=(T, D, D))
        s["mix"] = mix
        h1, xn2 = _rms_res_in(mix, g[1], h, g[2], name="rms_res_in")
        a = _matmul(xn2, W["w_up"][l], mode="nn", out_dtype=F32, name="mm_up", mnk=(T, 2 * D_FF, D))
        u = _convgate_fwd(a, cw_full[l], cb_full[l])
        f = _matmul(u, W["w_down"][l], mode="nn", out_dtype=F32, name="mm_down", mnk=(T, D, D_FF), tm=1024, tk=D_FF)
        if l + 1 < DEPTH:
            h, xn = _rms_res_in(f, g[3], h1, gains[l + 1][0], name="rms_res_in")
        else:
            h = _rms_fwd(f, g[3], res=h1, out_dtype=F32, name="rms_res")
        s.update(h1=h1, xn2=xn2, a=a, u=u, f=f)
        saved.append(s)

    dh, sq = _loss_head(h, target)

    d_gains = [[None] * 4 for _ in range(DEPTH)]
    d_cw, d_cb = [None] * DEPTH, [None] * DEPTH
    fox_acc = None
    d_kvnorm = d_bf = token = df = None

    def dw(nm, a, b, **kw):
        return _matmul(a, b, mode="tn", out_dtype=BF16, name="mm_dw_" + nm, **kw)

    flush = getattr(reduce, "flush", lambda after: None)

    def slabs(full, width):
        return full.reshape(full.shape[0], N_CHIPS, width).transpose(1, 0, 2)

    for l in reversed(range(DEPTH)):
        s = saved[l]
        g = gains[l]
        if df is None:
            df, d_gains[l][3] = _rms_bwd(dh, s["f"], g[3], out_dtype=BF16, name="rms_bwd")
        du = _matmul(df, W["w_down"][l], mode="nt", out_dtype=F32, name="mm_down_dx", mnk=(T, D_FF, D), tn=256, after=token)
        g_down = dw("w_down", s["u"], df, tm=1408, tn=1024)
        da, d_cw[l], d_cb[l] = _convgate_bwd(s["a"], du, cw_full[l], cb_full[l])
        dxn2 = _matmul(da, W["w_up"][l], mode="nt", out_dtype=F32, name="mm_up_dx", mnk=(T, D, 2 * D_FF), tm=1024, tn=1024, tk=1408,
                       a_map=_halves_a)
        g_up = dw("w_up", s["xn2"], da, mnk=(D, 2 * D_FF, T), tn=1408, b_map=_halves_b)
        token = reduce([("w_down", l, g_down), ("w_up", l, g_up)], "ffn%d" % l)
        dh1, dmix, d_gains[l][2], d_gains[l][1] = _rms_bwd2(dxn2, s["h1"], g[2], dh, s["mix"], g[1], name="rms_bwd2")
        if l < N_A:
            datt = _matmul(dmix, W["w_o_a"][l], mode="nt", out_dtype=F32, name="mm_oa_dx", mnk=(T, A_W, D), tn=768, after=token)
            g_o = dw("w_o_a", s["att"], dmix, tm=768, tn=1024)
            do_p, dlt_p = _combine_bwd(datt, s["o3"], s["lse3"], headsum)
            dqkv = None
            for which, d in enumerate(_band_bwd(s["qkvp"], do_p, s["lse_p"], dlt_p)):
                dqkv = _rope_bwd(d, which, tabs, dqkv)
            dxn = _matmul(dqkv, W["w_qkv_a"][l], mode="nt", out_dtype=F32, name="mm_qkv_dx", mnk=(T, D, 3 * A_W), tm=1024, tn=1024, tk=3 * A_W,
                          after=[flush(dqkv)])
            g_qkv = dw("w_qkv_a", s["xn"], dqkv, tn=768)
            group = [("w_o_a", l, g_o), ("w_qkv_a", l, slabs(g_qkv, 576))]
        else:
            j = l - N_A
            do = _matmul(dmix, W["w_o_b"][j], mode="nt", out_dtype=BF16, name="mm_ob_dx", mnk=(T, D, D), after=token)
            g_o = dw("w_o_b", s["o"], dmix, tn=1024)
            dq, *fox_acc = _fox_bwd(s["q"], kv, do, c_row, fox_acc)
            dxn = _matmul(dq, W["w_q_b"][j], mode="nt", out_dtype=F32, name="mm_qb_dx", mnk=(T, D, D), after=[flush(dq)])
            g_q = dw("w_q_b", s["xn"], dq, tn=1024)
            group = [("w_o_b", j, g_o), ("w_q_b", j, g_q)]
        if l > 0 and l != N_A:
            dh, df, d_gains[l][0], d_gains[l - 1][3] = _rms_bwd2(dxn, s["h"], g[0], dh1, saved[l - 1]["f"], gains[l - 1][3],
                                                                 name="rms_bwd2")
        else:
            dh, d_gains[l][0] = _rms_bwd(dxn, s["h"], g[0], dres=dh1, out_dtype=F32, name="rms_bwd_res")
            df = None
        if l == N_A:
            dk, dv, dck = fox_acc
            dc16 = -dck[:, :2, :].reshape(16, T).T
            dzf, d_bf = _gates_bwd(jnp.pad(dc16, ((0, 0), (0, 128 - 16))), zf, bf_pad)
            dkvf = jnp.concatenate([dk.astype(BF16), dv.astype(BF16), dzf], axis=1)
            g_kvf = _matmul(kvn, dkvf, mode="tn", out_dtype=BF16, name="mm_kvf_dw", tm=512, tn=2 * D + 128)[:, :2 * D + 16]
            dkvn = _matmul(dkvf, w_kvf_pad, mode="nt", out_dtype=F32, name="mm_kvf_dx", tm=1024, tn=1024, tk=2 * D + 128)
            dh, d_kvnorm = _rms_bwd(dkvn, h_kv, kvn_g, dres=dh, out_dtype=F32, name="rms_bwd_res")
            group.append(("w_kvf", 0, slabs(g_kvf, 516)))
        token = reduce(group, "mix%d" % l)
    small_flat = jnp.concatenate([
        jnp.stack([jnp.stack(r) for r in d_gains]).reshape(-1),
        jnp.stack(d_cw).transpose(0, 2, 1, 3).reshape(-1),
        jnp.stack(d_cb).reshape(-1),
        d_kvnorm.reshape(-1), d_bf[0, :16]])
    small = jnp.pad(small_flat, (0, 2 * N_CHIPS * SMALL_ROWS * SMALL_W - small_flat.shape[0]))
    reduce([("small", 0, small.reshape(N_CHIPS, 2 * SMALL_ROWS, SMALL_W))], "small")
    return sq, dh


def _update(loss, grad_x, reduced, chip, ws, ms, vs):
    red_s = reduced.pop("small")
    buf_s = lax.dynamic_update_slice(jnp.zeros((2, N_CHIPS, SMALL_ROWS, SMALL_W), F32), red_s.reshape(2, 1, SMALL_ROWS, SMALL_W),
                                     (0, chip, 0, 0))
    (all_s,) = _allgather([buf_s], ["slab"], name="gather_small_grads")
    sflat = all_s.transpose(1, 0, 2, 3).reshape(-1)

    grads = {nm: r.reshape(ws[nm].shape) for nm, r in reduced.items()}
    o = 0
    g_gains_full = sflat[o:o + 16 * D].reshape(DEPTH, 4, D); o += 16 * D
    g_cw_full = sflat[o:o + 12 * 2 * D_FF].reshape(DEPTH, 3, 2 * D_FF); o += 12 * 2 * D_FF
    grads["conv_b"] = sflat[o:o + 4 * 2 * D_FF].reshape(DEPTH, 2 * D_FF); o += 4 * 2 * D_FF
    grads["kv_norm"] = sflat[o:o + D]; o += D
    grads["b_f"] = sflat[o:o + 16]
    grads["norm_gains"] = lax.dynamic_slice_in_dim(g_gains_full, chip * 256, 256, axis=2)
    grads["conv_w"] = lax.dynamic_slice_in_dim(g_cw_full, chip * 1408, 1408, axis=2)

    names = ["norm_gains", "w_qkv_a", "w_o_a", "w_q_b", "w_o_b", "kv_norm", "w_kvf", "b_f", "w_up", "conv_w", "conv_b", "w_down"]
    deltas, new_m, new_v = {}, {}, {}
    for nm in names:
        shp = ws[nm].shape
        two = (math.prod(shp[:-1]), shp[-1]) if len(shp) > 1 else (1, shp[0])
        d, m2, v2 = _adamw(ws[nm].reshape(two), ms[nm].reshape(two), vs[nm].reshape(two), grads[nm].reshape(two),
                           name="adamw_" + nm)
        deltas[nm], new_m[nm], new_v[nm] = d.reshape(shp), m2.reshape(shp), v2.reshape(shp)

    return (loss, grad_x, *[grads[nm] for nm in names], *[deltas[nm] for nm in names],
            *[new_m[nm] for nm in names], *[new_v[nm] for nm in names])
```
